```python
import math
import jax, jax.numpy as jnp
from jax import lax
import numpy as np

D_MODEL = 1024
BATCH = 8
SEQ = 4096
DEPTH = 1

MIX_WIDTH = D_MODEL
ATTN_WIDTH = MIX_WIDTH // 2
POOL_WIDTH = MIX_WIDTH - ATTN_WIDTH
HEAD_DIM = 64
N_HEADS = ATTN_WIDTH // HEAD_DIM
POOL_WINDOWS = (2, 4, 8, 16)
N_POOL_GROUPS = len(POOL_WINDOWS)
POOL_GROUP = POOL_WIDTH // N_POOL_GROUPS
D_FF = 4 * D_MODEL
PLE_DIM = 256
Q_BLOCK = 128
LN_EPS = 1e-5
RMS_EPS = 1e-6
DN_ALPHA = float((2 * DEPTH) ** 0.25)
DN_BETA = float((8 * DEPTH) ** -0.25)
PROJ_WIDTH = 3 * ATTN_WIDTH + POOL_WIDTH

kernel_name = "hymba_stickbreak_pool_deepnorm"


def layer_norm(x, g, b):
    xf = x.astype(jnp.float32)
    mu = jnp.mean(xf, axis=-1, keepdims=True)
    var = jnp.mean(jnp.square(xf - mu), axis=-1, keepdims=True)
    y = (xf - mu) * lax.rsqrt(var + LN_EPS)
    return (y * g.astype(jnp.float32) + b.astype(jnp.float32)).astype(x.dtype)


def stick_breaking_attention(q, k, v):
    B, H, S, Dh = q.shape
    n_blk = S // Q_BLOCK
    scale = 1.0 / math.sqrt(Dh)
    q_blocks = q.reshape(B, H, n_blk, Q_BLOCK, Dh).transpose(2, 0, 1, 3, 4)
    starts = jnp.arange(n_blk, dtype=jnp.int32) * Q_BLOCK
    k_pos = jnp.arange(S, dtype=jnp.int32)
    kf = k.astype(jnp.float32)
    vf = v.astype(jnp.float32)

    def one_block(args):
        q_blk, start = args
        z = jnp.einsum("bhqd,bhkd->bhqk", q_blk.astype(jnp.float32), kf) * scale
        q_pos = start + jnp.arange(Q_BLOCK, dtype=jnp.int32)
        mask = k_pos[None, :] < q_pos[:, None]
        log_not = jnp.where(mask, jax.nn.log_sigmoid(-z), 0.0)
        suffix = lax.cumsum(log_not, axis=3, reverse=True) - log_not
        weights = jnp.where(mask, jnp.exp(jax.nn.log_sigmoid(z) + suffix), 0.0)
        return jnp.einsum("bhqk,bhkd->bhqd", weights, vf)

    out = lax.map(one_block, (q_blocks, starts))
    return out.transpose(1, 2, 0, 3, 4).reshape(B, H, S, Dh)


def multiscale_pool(u, w_pool, pool_scale):
    B, S, _ = u.shape
    uf = u.astype(jnp.float32)
    csum = jnp.cumsum(uf, axis=1)
    pos = jnp.arange(S, dtype=jnp.int32)
    diffs = []
    for g, w in enumerate(POOL_WINDOWS):
        sl = slice(g * POOL_GROUP, (g + 1) * POOL_GROUP)
        cg = csum[..., sl]
        lag = jnp.pad(cg, ((0, 0), (w, 0), (0, 0)))[:, :S]
        count = jnp.minimum(pos + 1, w).astype(jnp.float32)[None, :, None]
        diffs.append((cg - lag) / count - uf[..., sl])
    d = jnp.stack(diffs, axis=2)
    y = jnp.einsum("bsgc,gcd->bsgd", d, w_pool.astype(jnp.float32))
    return y.reshape(B, S, POOL_WIDTH) * pool_scale.astype(jnp.float32)


def _fwd_setup_inputs(seed: int = 0) -> dict:
    key = jax.random.key(seed)
    ks = jax.random.split(key, 24)
    nrm = lambda k, shape, s: jax.random.normal(k, shape, jnp.float32) * s
    L = DEPTH
    x = jax.random.normal(ks[0], (BATCH, SEQ, D_MODEL), jnp.float32)
    p = jax.random.normal(ks[1], (L, BATCH, SEQ, PLE_DIM), jnp.float32)
    emb_ln_g = 1.0 + nrm(ks[2], (D_MODEL,), 0.02)
    emb_ln_b = nrm(ks[3], (D_MODEL,), 0.02)
    col_scale = jnp.concatenate([
        jnp.ones((2 * ATTN_WIDTH,), jnp.float32),
        jnp.full((ATTN_WIDTH + POOL_WIDTH,), DN_BETA, jnp.float32)])
    w_in = nrm(ks[4], (L, D_MODEL, PROJ_WIDTH), D_MODEL ** -0.5) * col_scale
    attn_out_g = 1.0 + nrm(ks[5], (L, ATTN_WIDTH), 0.02)
    w_pool = nrm(ks[6], (L, N_POOL_GROUPS, POOL_GROUP, POOL_GROUP), POOL_GROUP ** -0.5 * DN_BETA)
    pool_scale = 1.0 + nrm(ks[7], (L, POOL_WIDTH), 0.02)
    w_out = nrm(ks[8], (L, MIX_WIDTH, D_MODEL), MIX_WIDTH ** -0.5 * DN_BETA)
    ln1_g = 1.0 + nrm(ks[9], (L, D_MODEL), 0.02)
    ln1_b = nrm(ks[10], (L, D_MODEL), 0.02)
    w_up = nrm(ks[11], (L, D_MODEL, D_FF), D_MODEL ** -0.5 * DN_BETA)
    w_down = nrm(ks[12], (L, D_FF, D_MODEL), D_FF ** -0.5 * DN_BETA)
    ln2_g = 1.0 + nrm(ks[13], (L, D_MODEL), 0.02)
    ln2_b = nrm(ks[14], (L, D_MODEL), 0.02)
    w_ple = nrm(ks[15], (L, PLE_DIM, D_MODEL), PLE_DIM ** -0.5 * DN_BETA)
    w_ple_gate = nrm(ks[16], (L, D_MODEL, D_MODEL), D_MODEL ** -0.5)
    ln3_g = 1.0 + nrm(ks[17], (L, D_MODEL), 0.02)
    ln3_b = nrm(ks[18], (L, D_MODEL), 0.02)
    return {"x": x, "p": p, "emb_ln_g": emb_ln_g, "emb_ln_b": emb_ln_b,
            "w_in": w_in, "attn_out_g": attn_out_g, "w_pool": w_pool, "pool_scale": pool_scale,
            "w_out": w_out, "ln1_g": ln1_g, "ln1_b": ln1_b, "w_up": w_up, "w_down": w_down,
            "ln2_g": ln2_g, "ln2_b": ln2_b, "w_ple": w_ple, "w_ple_gate": w_ple_gate,
            "ln3_g": ln3_g, "ln3_b": ln3_b}


def _fwd_reference(x, p, emb_ln_g, emb_ln_b, w_in, attn_out_g, w_pool, pool_scale, w_out,
              ln1_g, ln1_b, w_up, w_down, ln2_g, ln2_b, w_ple, w_ple_gate, ln3_g, ln3_b):
    B, S, D = x.shape
    dt = x.dtype
    x = layer_norm(x, emb_ln_g, emb_ln_b)
    for i in range(DEPTH):
        proj = x @ w_in[i]
        q, k, v, u = jnp.split(proj, [ATTN_WIDTH, 2 * ATTN_WIDTH, 3 * ATTN_WIDTH], axis=-1)
        to_heads = lambda t: t.reshape(B, S, N_HEADS, HEAD_DIM).transpose(0, 2, 1, 3)
        o = stick_breaking_attention(to_heads(q), to_heads(k), to_heads(v))
        o = o * lax.rsqrt(jnp.mean(o * o, axis=-1, keepdims=True) + RMS_EPS)
        o = o.transpose(0, 2, 1, 3).reshape(B, S, ATTN_WIDTH) * attn_out_g[i].astype(jnp.float32)
        pooled = multiscale_pool(u, w_pool[i], pool_scale[i])
        mixed = jnp.concatenate([o, pooled], axis=-1).astype(dt) @ w_out[i]
        x = layer_norm(DN_ALPHA * x + mixed, ln1_g[i], ln1_b[i])
        h = jnp.square(jax.nn.relu(x @ w_up[i])) @ w_down[i]
        x = layer_norm(DN_ALPHA * x + h, ln2_g[i], ln2_b[i])
        ple = (p[i] @ w_ple[i]) * jax.nn.sigmoid(x @ w_ple_gate[i])
        x = layer_norm(DN_ALPHA * x + ple, ln3_g[i], ln3_b[i])
    return x


import jax as _jax
import jax.numpy as _jnp

TWIN_FORMAT = 'train_step'
FWD_PARAMS = ['x', 'p', 'emb_ln_g', 'emb_ln_b', 'w_in', 'attn_out_g', 'w_pool', 'pool_scale', 'w_out', 'ln1_g', 'ln1_b', 'w_up', 'w_down', 'ln2_g', 'ln2_b', 'w_ple', 'w_ple_gate', 'ln3_g', 'ln3_b']
TWIN_WEIGHTS = ['emb_ln_g', 'emb_ln_b', 'w_in', 'attn_out_g', 'w_pool', 'pool_scale', 'w_out', 'ln1_g', 'ln1_b', 'w_up', 'w_down', 'ln2_g', 'ln2_b', 'w_ple', 'w_ple_gate', 'ln3_g', 'ln3_b']
TWIN_DIFF_INPUT = 'x'
TWIN_INPUTS = ['x', 'p', 'emb_ln_g', 'emb_ln_b', 'w_in', 'attn_out_g', 'w_pool', 'pool_scale', 'w_out', 'ln1_g', 'ln1_b', 'w_up', 'w_down', 'ln2_g', 'ln2_b', 'w_ple', 'w_ple_gate', 'ln3_g', 'ln3_b', 'loss_target', 'm_emb_ln_g', 'm_emb_ln_b', 'm_w_in', 'm_attn_out_g', 'm_w_pool', 'm_pool_scale', 'm_w_out', 'm_ln1_g', 'm_ln1_b', 'm_w_up', 'm_w_down', 'm_ln2_g', 'm_ln2_b', 'm_w_ple', 'm_w_ple_gate', 'm_ln3_g', 'm_ln3_b', 'v_emb_ln_g', 'v_emb_ln_b', 'v_w_in', 'v_attn_out_g', 'v_w_pool', 'v_pool_scale', 'v_w_out', 'v_ln1_g', 'v_ln1_b', 'v_w_up', 'v_w_down', 'v_ln2_g', 'v_ln2_b', 'v_w_ple', 'v_w_ple_gate', 'v_ln3_g', 'v_ln3_b']
TWIN_OUTPUTS = ['loss', 'grad_x', 'grad_emb_ln_g', 'grad_emb_ln_b', 'grad_w_in', 'grad_attn_out_g', 'grad_w_pool', 'grad_pool_scale', 'grad_w_out', 'grad_ln1_g', 'grad_ln1_b', 'grad_w_up', 'grad_w_down', 'grad_ln2_g', 'grad_ln2_b', 'grad_w_ple', 'grad_w_ple_gate', 'grad_ln3_g', 'grad_ln3_b', 'delta_emb_ln_g', 'delta_emb_ln_b', 'delta_w_in', 'delta_attn_out_g', 'delta_w_pool', 'delta_pool_scale', 'delta_w_out', 'delta_ln1_g', 'delta_ln1_b', 'delta_w_up', 'delta_w_down', 'delta_ln2_g', 'delta_ln2_b', 'delta_w_ple', 'delta_w_ple_gate', 'delta_ln3_g', 'delta_ln3_b', 'new_m_emb_ln_g', 'new_m_emb_ln_b', 'new_m_w_in', 'new_m_attn_out_g', 'new_m_w_pool', 'new_m_pool_scale', 'new_m_w_out', 'new_m_ln1_g', 'new_m_ln1_b', 'new_m_w_up', 'new_m_w_down', 'new_m_ln2_g', 'new_m_ln2_b', 'new_m_w_ple', 'new_m_w_ple_gate', 'new_m_ln3_g', 'new_m_ln3_b', 'new_v_emb_ln_g', 'new_v_emb_ln_b', 'new_v_w_in', 'new_v_attn_out_g', 'new_v_w_pool', 'new_v_pool_scale', 'new_v_w_out', 'new_v_ln1_g', 'new_v_ln1_b', 'new_v_w_up', 'new_v_w_down', 'new_v_ln2_g', 'new_v_ln2_b', 'new_v_w_ple', 'new_v_w_ple_gate', 'new_v_ln3_g', 'new_v_ln3_b']
TWIN_LEAF_KINDS = {'loss': 'loss', 'grad_x': 'grad_x', 'grad_emb_ln_g': 'grad_w', 'grad_emb_ln_b': 'grad_w', 'grad_w_in': 'grad_w', 'grad_attn_out_g': 'grad_w', 'grad_w_pool': 'grad_w', 'grad_pool_scale': 'grad_w', 'grad_w_out': 'grad_w', 'grad_ln1_g': 'grad_w', 'grad_ln1_b': 'grad_w', 'grad_w_up': 'grad_w', 'grad_w_down': 'grad_w', 'grad_ln2_g': 'grad_w', 'grad_ln2_b': 'grad_w', 'grad_w_ple': 'grad_w', 'grad_w_ple_gate': 'grad_w', 'grad_ln3_g': 'grad_w', 'grad_ln3_b': 'grad_w', 'delta_emb_ln_g': 'delta_w', 'delta_emb_ln_b': 'delta_w', 'delta_w_in': 'delta_w', 'delta_attn_out_g': 'delta_w', 'delta_w_pool': 'delta_w', 'delta_pool_scale': 'delta_w', 'delta_w_out': 'delta_w', 'delta_ln1_g': 'delta_w', 'delta_ln1_b': 'delta_w', 'delta_w_up': 'delta_w', 'delta_w_down': 'delta_w', 'delta_ln2_g': 'delta_w', 'delta_ln2_b': 'delta_w', 'delta_w_ple': 'delta_w', 'delta_w_ple_gate': 'delta_w', 'delta_ln3_g': 'delta_w', 'delta_ln3_b': 'delta_w', 'new_m_emb_ln_g': 'new_m', 'new_m_emb_ln_b': 'new_m', 'new_m_w_in': 'new_m', 'new_m_attn_out_g': 'new_m', 'new_m_w_pool': 'new_m', 'new_m_pool_scale': 'new_m', 'new_m_w_out': 'new_m', 'new_m_ln1_g': 'new_m', 'new_m_ln1_b': 'new_m', 'new_m_w_up': 'new_m', 'new_m_w_down': 'new_m', 'new_m_ln2_g': 'new_m', 'new_m_ln2_b': 'new_m', 'new_m_w_ple': 'new_m', 'new_m_w_ple_gate': 'new_m', 'new_m_ln3_g': 'new_m', 'new_m_ln3_b': 'new_m', 'new_v_emb_ln_g': 'new_v', 'new_v_emb_ln_b': 'new_v', 'new_v_w_in': 'new_v', 'new_v_attn_out_g': 'new_v', 'new_v_w_pool': 'new_v', 'new_v_pool_scale': 'new_v', 'new_v_w_out': 'new_v', 'new_v_ln1_g': 'new_v', 'new_v_ln1_b': 'new_v', 'new_v_w_up': 'new_v', 'new_v_w_down': 'new_v', 'new_v_ln2_g': 'new_v', 'new_v_ln2_b': 'new_v', 'new_v_w_ple': 'new_v', 'new_v_w_ple_gate': 'new_v', 'new_v_ln3_g': 'new_v', 'new_v_ln3_b': 'new_v'}


def _forward(args):
    return _fwd_reference(*[args[k] for k in FWD_PARAMS])


def _output_shape():
    out = _jax.eval_shape(lambda: _forward(_fwd_setup_inputs(0)))
    return out.shape, out.dtype

N_MICROBATCH = 1
ADAM_LR = 0.001
ADAM_B1 = 0.9
ADAM_B2 = 0.999
ADAM_EPS = 1e-08
ADAM_WD = 0.01
ADAM_STEP = 10
PER_EXAMPLE_BATCH_AXIS = {'x': 0, 'p': 1, 'loss_target': 0}
SHARED_INPUTS = []
_WEIGHT_DTYPES = {'emb_ln_g': _jnp.float32, 'emb_ln_b': _jnp.float32, 'w_in': _jnp.float32, 'attn_out_g': _jnp.float32, 'w_pool': _jnp.float32, 'pool_scale': _jnp.float32, 'w_out': _jnp.float32, 'ln1_g': _jnp.float32, 'ln1_b': _jnp.float32, 'w_up': _jnp.float32, 'w_down': _jnp.float32, 'ln2_g': _jnp.float32, 'ln2_b': _jnp.float32, 'w_ple': _jnp.float32, 'w_ple_gate': _jnp.float32, 'ln3_g': _jnp.float32, 'ln3_b': _jnp.float32}
MOMENT_SCALE = {'emb_ln_g': 7.752049e-01, 'emb_ln_b': 4.752576e-01, 'w_in': 7.775631e-02, 'attn_out_g': 8.379771e-02, 'w_pool': 4.249083e-02, 'pool_scale': 2.730069e-02, 'w_out': 1.011554e-01, 'ln1_g': 9.386870e-01, 'ln1_b': 4.256563e-01, 'w_up': 3.540119e-02, 'w_down': 7.895577e-02, 'ln2_g': 9.877060e-01, 'ln2_b': 4.489967e-01, 'w_ple': 7.799720e-02, 'w_ple_gate': 1.820317e-02, 'ln3_g': 3.203638e+01, 'ln3_b': 2.623408e+00}


def _to_microbatches(a, axis):
    t = _jnp.moveaxis(a, axis, 0)
    t = t.reshape((N_MICROBATCH, t.shape[0] // N_MICROBATCH) + t.shape[1:])
    return _jnp.moveaxis(t, 1, axis + 1)


def setup_inputs(seed: int = 0) -> dict:
    inp = _fwd_setup_inputs(seed)
    key = _jax.random.fold_in(_jax.random.key(seed), 7919)
    shape, _ = _output_shape()
    out = dict(inp)
    out["loss_target"] = _jax.random.normal(_jax.random.fold_in(key, 0), shape, _jnp.float32)
    for i, name in enumerate(TWIN_WEIGHTS):
        w = inp[name].astype(_jnp.float32)
        if MOMENT_SCALE is None:
            s = _jnp.sqrt(_jnp.mean(_jnp.square(w)) + 1e-30)
        else:
            s = MOMENT_SCALE[name]
        km, kv = _jax.random.split(_jax.random.fold_in(key, i + 1))
        out[name] = w
        out["m_" + name] = s * _jax.random.normal(km, w.shape, _jnp.float32)
        out["v_" + name] = (s * s) * _jax.random.uniform(kv, w.shape, _jnp.float32, 0.5, 1.5)
    if N_MICROBATCH > 1:
        for name, axis in PER_EXAMPLE_BATCH_AXIS.items():
            out[name] = _to_microbatches(out[name], axis)
    return {'x': out['x'], 'p': out['p'], 'emb_ln_g': out['emb_ln_g'], 'emb_ln_b': out['emb_ln_b'], 'w_in': out['w_in'], 'attn_out_g': out['attn_out_g'], 'w_pool': out['w_pool'], 'pool_scale': out['pool_scale'], 'w_out': out['w_out'], 'ln1_g': out['ln1_g'], 'ln1_b': out['ln1_b'], 'w_up': out['w_up'], 'w_down': out['w_down'], 'ln2_g': out['ln2_g'], 'ln2_b': out['ln2_b'], 'w_ple': out['w_ple'], 'w_ple_gate': out['w_ple_gate'], 'ln3_g': out['ln3_g'], 'ln3_b': out['ln3_b'], 'loss_target': out['loss_target'], 'm_emb_ln_g': out['m_emb_ln_g'], 'm_emb_ln_b': out['m_emb_ln_b'], 'm_w_in': out['m_w_in'], 'm_attn_out_g': out['m_attn_out_g'], 'm_w_pool': out['m_w_pool'], 'm_pool_scale': out['m_pool_scale'], 'm_w_out': out['m_w_out'], 'm_ln1_g': out['m_ln1_g'], 'm_ln1_b': out['m_ln1_b'], 'm_w_up': out['m_w_up'], 'm_w_down': out['m_w_down'], 'm_ln2_g': out['m_ln2_g'], 'm_ln2_b': out['m_ln2_b'], 'm_w_ple': out['m_w_ple'], 'm_w_ple_gate': out['m_w_ple_gate'], 'm_ln3_g': out['m_ln3_g'], 'm_ln3_b': out['m_ln3_b'], 'v_emb_ln_g': out['v_emb_ln_g'], 'v_emb_ln_b': out['v_emb_ln_b'], 'v_w_in': out['v_w_in'], 'v_attn_out_g': out['v_attn_out_g'], 'v_w_pool': out['v_w_pool'], 'v_pool_scale': out['v_pool_scale'], 'v_w_out': out['v_w_out'], 'v_ln1_g': out['v_ln1_g'], 'v_ln1_b': out['v_ln1_b'], 'v_w_up': out['v_w_up'], 'v_w_down': out['v_w_down'], 'v_ln2_g': out['v_ln2_g'], 'v_ln2_b': out['v_ln2_b'], 'v_w_ple': out['v_w_ple'], 'v_w_ple_gate': out['v_w_ple_gate'], 'v_ln3_g': out['v_ln3_g'], 'v_ln3_b': out['v_ln3_b']}


def _loss(weights, diff, rest, loss_target):
    with _jax.named_scope("forward"):
        args = {**rest, TWIN_DIFF_INPUT: diff, **{k: w.astype(_WEIGHT_DTYPES[k]) for k, w in weights.items()}}
        y = _forward(args)
    with _jax.named_scope("loss_head"):
        err = _jnp.square(y.astype(_jnp.float32) - loss_target)
        return 0.5 * _jnp.sum(_jnp.mean(err, axis=-1)) if err.ndim else 0.5 * err


def _adamw(w, g, m, v):
    m = ADAM_B1 * m + (1.0 - ADAM_B1) * g
    v = ADAM_B2 * v + (1.0 - ADAM_B2) * _jnp.square(g)
    m_hat = m / (1.0 - ADAM_B1 ** ADAM_STEP)
    v_hat = v / (1.0 - ADAM_B2 ** ADAM_STEP)
    delta = -ADAM_LR * (m_hat / (_jnp.sqrt(v_hat) + ADAM_EPS) + ADAM_WD * w)
    return delta, m, v


def reference(x, p, emb_ln_g, emb_ln_b, w_in, attn_out_g, w_pool, pool_scale, w_out, ln1_g, ln1_b, w_up, w_down, ln2_g, ln2_b, w_ple, w_ple_gate, ln3_g, ln3_b, loss_target, m_emb_ln_g, m_emb_ln_b, m_w_in, m_attn_out_g, m_w_pool, m_pool_scale, m_w_out, m_ln1_g, m_ln1_b, m_w_up, m_w_down, m_ln2_g, m_ln2_b, m_w_ple, m_w_ple_gate, m_ln3_g, m_ln3_b, v_emb_ln_g, v_emb_ln_b, v_w_in, v_attn_out_g, v_w_pool, v_pool_scale, v_w_out, v_ln1_g, v_ln1_b, v_w_up, v_w_down, v_ln2_g, v_ln2_b, v_w_ple, v_w_ple_gate, v_ln3_g, v_ln3_b):
    given = dict(x=x, p=p, emb_ln_g=emb_ln_g, emb_ln_b=emb_ln_b, w_in=w_in, attn_out_g=attn_out_g, w_pool=w_pool, pool_scale=pool_scale, w_out=w_out, ln1_g=ln1_g, ln1_b=ln1_b, w_up=w_up, w_down=w_down, ln2_g=ln2_g, ln2_b=ln2_b, w_ple=w_ple, w_ple_gate=w_ple_gate, ln3_g=ln3_g, ln3_b=ln3_b, loss_target=loss_target, m_emb_ln_g=m_emb_ln_g, m_emb_ln_b=m_emb_ln_b, m_w_in=m_w_in, m_attn_out_g=m_attn_out_g, m_w_pool=m_w_pool, m_pool_scale=m_pool_scale, m_w_out=m_w_out, m_ln1_g=m_ln1_g, m_ln1_b=m_ln1_b, m_w_up=m_w_up, m_w_down=m_w_down, m_ln2_g=m_ln2_g, m_ln2_b=m_ln2_b, m_w_ple=m_w_ple, m_w_ple_gate=m_w_ple_gate, m_ln3_g=m_ln3_g, m_ln3_b=m_ln3_b, v_emb_ln_g=v_emb_ln_g, v_emb_ln_b=v_emb_ln_b, v_w_in=v_w_in, v_attn_out_g=v_attn_out_g, v_w_pool=v_w_pool, v_pool_scale=v_pool_scale, v_w_out=v_w_out, v_ln1_g=v_ln1_g, v_ln1_b=v_ln1_b, v_w_up=v_w_up, v_w_down=v_w_down, v_ln2_g=v_ln2_g, v_ln2_b=v_ln2_b, v_w_ple=v_w_ple, v_w_ple_gate=v_w_ple_gate, v_ln3_g=v_ln3_g, v_ln3_b=v_ln3_b)
    weights = {n: given[n] for n in TWIN_WEIGHTS}
    shared = {n: given[n] for n in SHARED_INPUTS}
    per_example = {n: given[n] for n in ['x', 'p']}
    grad_fn = _jax.value_and_grad(_loss, argnums=(0, 1))

    def one_microbatch(ex, loss_target):
        ex = dict(ex)
        diff = ex.pop(TWIN_DIFF_INPUT)
        return grad_fn(weights, diff, {**shared, **ex}, loss_target)

    if N_MICROBATCH == 1:
        loss, (grad_w, grad_x) = one_microbatch(per_example, given["loss_target"])
    else:
        def body(carry, xs):
            loss_sum, grad_sum = carry
            l_k, (gw_k, gx_k) = one_microbatch(xs[0], xs[1])
            with _jax.named_scope("update"):
                return (loss_sum + l_k, _jax.tree.map(_jnp.add, grad_sum, gw_k)), gx_k

        init = (_jnp.zeros((), _jnp.float32), _jax.tree.map(_jnp.zeros_like, weights))
        (loss, grad_w), grad_x = _jax.lax.scan(body, init, (per_example, given["loss_target"]))
    with _jax.named_scope("update"):
        delta_w, new_m, new_v = {}, {}, {}
        for n in TWIN_WEIGHTS:
            delta_w[n], new_m[n], new_v[n] = _adamw(weights[n], grad_w[n], given["m_" + n], given["v_" + n])
    return (loss, grad_x, *[grad_w[n] for n in TWIN_WEIGHTS], *[delta_w[n] for n in TWIN_WEIGHTS],
            *[new_m[n] for n in TWIN_WEIGHTS], *[new_v[n] for n in TWIN_WEIGHTS])
```

```python
import functools

import jax
import jax.numpy as jnp
from jax import lax
from jax.experimental import pallas as pl
from jax.experimental.pallas import tpu as pltpu

F32 = jnp.float32
BF16 = jnp.bfloat16

D_MODEL = 1024
ATTN_WIDTH = 512
POOL_WIDTH = 512
HEAD_DIM = 64
PAIR = 2 * HEAD_DIM
N_PAIRS = ATTN_WIDTH // PAIR
N_POOL_GROUPS = 4
POOL_GROUP = 128
POOL_HALO = 16
D_FF = 4096
PLE_DIM = 256
N_CHIPS = 4
N_DEV = 8
LN_EPS = 1e-5
RMS_EPS = 1e-6
ALPHA = float(2.0 ** 0.25)
Q_SCALE = 0.125
ADAM_LR = 0.001
ADAM_B1 = 0.9
ADAM_B2 = 0.999
ADAM_EPS = 1e-08
ADAM_WD = 0.01
ADAM_STEP = 10
LANES = 128
MIB = 1024 * 1024

MESH = pl.DeviceIdType.MESH
HBM_SPEC = pl.BlockSpec(memory_space=pltpu.HBM)
VMEM_SPEC = pl.BlockSpec(memory_space=pltpu.VMEM)


def _cp(vmem_mib):
    return pltpu.CompilerParams(vmem_limit_bytes=vmem_mib * MIB)


def _dot(a, b):
    return jnp.dot(a, b, preferred_element_type=F32)


def _dot_nt(a, b):
    return lax.dot_general(a, b, (((1,), (1,)), ((), ())), preferred_element_type=F32)


def _dot_tn(a, b):
    return lax.dot_general(a, b, (((0,), (0,)), ((), ())), preferred_element_type=F32)


def _ln_fwd(pre):
    mu = jnp.mean(pre, axis=-1, keepdims=True)
    xc = pre - mu
    var = jnp.mean(xc * xc, axis=-1, keepdims=True)
    rstd = lax.rsqrt(var + LN_EPS)
    return xc * rstd, rstd


def _ln_bwd(dy, xh, rstd, g):
    dxh = dy * g
    m1 = jnp.mean(dxh, axis=-1, keepdims=True)
    m2 = jnp.mean(dxh * xh, axis=-1, keepdims=True)
    return rstd * (dxh - m1 - xh * m2)


def _colsum(a):
    return jnp.sum(a, axis=0, keepdims=True)


def _neg_softplus(z):
    return -(jnp.maximum(z, 0.0) + jnp.log1p(jnp.exp(-jnp.abs(z))))


def _split_bf16(a):
    hi = a.astype(BF16)
    lo = (a - hi.astype(F32)).astype(BF16)
    return hi, lo


def _row_spec(tm, n):
    return pl.BlockSpec((tm, n), lambda i: (i, 0))


def _const_spec(shape):
    nd = len(shape)
    return pl.BlockSpec(shape, lambda *_: (0,) * nd)


def _sds(shape, dtype):
    return jax.ShapeDtypeStruct(shape, dtype)


def _embln_inproj(x, g0, b0, w_in_s, tm):
    S, D = x.shape

    def body(x_ref, g_ref, b_ref, w_ref, xh_ref, rstd_ref, q_ref, k_ref, v_ref, u_ref):
        xh, rstd = _ln_fwd(x_ref[...])
        xh_ref[...] = xh
        rstd_ref[...] = rstd
        xb = (xh * g_ref[...] + b_ref[...]).astype(BF16)
        q_ref[...] = (_dot(xb, w_ref[0]) * Q_SCALE).astype(BF16)
        k_ref[...] = _dot(xb, w_ref[1]).astype(BF16)
        v_ref[...] = _dot(xb, w_ref[2]).astype(BF16)
        u_ref[...] = _dot(xb, w_ref[3])

    return pl.pallas_call(
        body, grid=(S // tm,), name="embln_inproj",
        in_specs=[_row_spec(tm, D), _const_spec((1, D)), _const_spec((1, D)),
                  _const_spec((N_CHIPS, D, 512))],
        out_specs=[_row_spec(tm, D), _row_spec(tm, 1), _row_spec(tm, 512), _row_spec(tm, 512),
                   _row_spec(tm, 512), _row_spec(tm, 512)],
        out_shape=[_sds((S, D), F32), _sds((S, 1), F32), _sds((S, 512), BF16), _sds((S, 512), BF16),
                   _sds((S, 512), BF16), _sds((S, 512), F32)],
        compiler_params=_cp(40),
    )(x, g0, b0, w_in_s)


def _tri(n, upper):
    r = lax.broadcasted_iota(jnp.int32, (n, n), 0)
    c = lax.broadcasted_iota(jnp.int32, (n, n), 1)
    keep = (r < c) if upper else (r > c)
    return jnp.where(keep, 1.0, 0.0).astype(BF16)


def _attn_fwd(q, k, v, ga, tq):
    S = q.shape[0]
    nq = S // tq

    def body(q_ref, k_ref, v_ref, ga_ref, o_ref, on_ref):
        i = pl.program_id(1)
        lane = lax.broadcasted_iota(jnp.int32, (1, PAIR), 1)
        m0 = lane < HEAD_DIM
        low = _tri(tq, upper=False)
        diff = (lax.broadcasted_iota(jnp.int32, (tq, tq), 1)
                - lax.broadcasted_iota(jnp.int32, (tq, tq), 0))
        q2 = q_ref[...]
        accs = []
        for hh in range(2):
            mask = m0 if hh == 0 else jnp.logical_not(m0)
            qh = jnp.where(mask, q2, jnp.zeros_like(q2))

            def step(n, carry, qh=qh):
                c_l, acc = carry
                kb = i - n
                ks = pl.multiple_of(kb * tq, tq)
                kt = k_ref[pl.ds(ks, tq), :]
                vt = v_ref[pl.ds(ks, tq), :]
                z = _dot_nt(qh, kt)
                lr = _neg_softplus(z)
                valid = diff < n * tq
                l = jnp.where(valid, lr, 0.0)
                hi, lo = _split_bf16(l)
                sfx = _dot(hi, low) + _dot(lo, low) + c_l
                w = jnp.where(valid, jnp.exp(z + lr + sfx), 0.0)
                acc = acc + _dot(w.astype(BF16), vt)
                c_l = c_l + jnp.sum(l, axis=1, keepdims=True)
                return c_l, acc

            _, acc = lax.fori_loop(0, i + 1, step,
                                   (jnp.zeros((tq, 1), F32), jnp.zeros((tq, PAIR), F32)))
            accs.append(acc)
        o = jnp.where(m0, accs[0], accs[1])
        o_ref[...] = o
        sq = o * o
        ms0 = jnp.sum(jnp.where(m0, sq, 0.0), axis=-1, keepdims=True) * (1.0 / HEAD_DIM)
        ms1 = jnp.sum(jnp.where(m0, 0.0, sq), axis=-1, keepdims=True) * (1.0 / HEAD_DIM)
        rs = jnp.where(m0, lax.rsqrt(ms0 + RMS_EPS), lax.rsqrt(ms1 + RMS_EPS))
        on_ref[...] = (o * rs * ga_ref[...]).astype(BF16)

    return pl.pallas_call(
        body, grid=(N_PAIRS, nq), name="attn_fwd",
        in_specs=[pl.BlockSpec((tq, PAIR), lambda p, i: (i, p)),
                  pl.BlockSpec((S, PAIR), lambda p, i: (0, p)),
                  pl.BlockSpec((S, PAIR), lambda p, i: (0, p)),
                  pl.BlockSpec((1, PAIR), lambda p, i: (0, p))],
        out_specs=[pl.BlockSpec((tq, PAIR), lambda p, i: (i, p)),
                   pl.BlockSpec((tq, PAIR), lambda p, i: (i, p))],
        out_shape=[_sds((S, ATTN_WIDTH), F32), _sds((S, ATTN_WIDTH), BF16)],
        compiler_params=_cp(40),
    )(q, k, v, ga)


def _pool_fwd(u, w_pool, pscale, tm):
    S = u.shape[0]
    hb = tm // POOL_HALO

    def body(u_ref, uh_ref, wp_ref, sc_ref, d_ref, pooled_ref):
        i = pl.program_id(0)
        halo = jnp.where(i > 0, uh_ref[...], 0.0)
        pos = i * tm + lax.broadcasted_iota(jnp.int32, (tm, 1), 0)
        for g in range(N_POOL_GROUPS):
            win = 2 ** (g + 1)
            cols = slice(g * POOL_GROUP, (g + 1) * POOL_GROUP)
            ut = u_ref[:, cols]
            s = jnp.concatenate([halo[:, cols], ut], axis=0)
            for sh in (1, 2, 4, 8)[:g + 1]:
                s = s + pltpu.roll(s, sh, 0)
            cnt = jnp.minimum(pos + 1, win).astype(F32)
            db = (s[POOL_HALO:, :] / cnt - ut).astype(BF16)
            y = _dot(db, wp_ref[g].astype(BF16))
            d_ref[:, cols] = db
            pooled_ref[:, cols] = (y * sc_ref[:, cols]).astype(BF16)

    return pl.pallas_call(
        body, grid=(S // tm,), name="pool_fwd",
        in_specs=[_row_spec(tm, POOL_WIDTH),
                  pl.BlockSpec((POOL_HALO, POOL_WIDTH), lambda i: (jnp.maximum(i * hb - 1, 0), 0)),
                  _const_spec((N_POOL_GROUPS, POOL_GROUP, POOL_GROUP)), _const_spec((1, POOL_WIDTH))],
        out_specs=[_row_spec(tm, POOL_WIDTH), _row_spec(tm, POOL_WIDTH)],
        out_shape=[_sds((S, POOL_WIDTH), BF16), _sds((S, POOL_WIDTH), BF16)],
        compiler_params=_cp(32),
    )(u, u, w_pool, pscale)


def _mix_ln1(on, pooled, xh0, g0, b0, w_out, g1, b1, tm):
    S, D = xh0.shape

    def body(on_ref, po_ref, xh0_ref, g0_ref, b0_ref, w_ref, g1_ref, b1_ref, xh_ref, rstd_ref, xb_ref):
        mixed = _dot(on_ref[...], w_ref[:ATTN_WIDTH, :]) + _dot(po_ref[...], w_ref[ATTN_WIDTH:, :])
        x0 = xh0_ref[...] * g0_ref[...] + b0_ref[...]
        xh, rstd = _ln_fwd(ALPHA * x0 + mixed)
        xh_ref[...] = xh
        rstd_ref[...] = rstd
        xb_ref[...] = (xh * g1_ref[...] + b1_ref[...]).astype(BF16)

    return pl.pallas_call(
        body, grid=(S // tm,), name="mix_ln1",
        in_specs=[_row_spec(tm, ATTN_WIDTH), _row_spec(tm, POOL_WIDTH), _row_spec(tm, D),
                  _const_spec((1, D)), _const_spec((1, D)), _const_spec((D, D)),
                  _const_spec((1, D)), _const_spec((1, D))],
        out_specs=[_row_spec(tm, D), _row_spec(tm, 1), _row_spec(tm, D)],
        out_shape=[_sds((S, D), F32), _sds((S, 1), F32), _sds((S, D), BF16)],
        compiler_params=_cp(40),
    )(on, pooled, xh0, g0, b0, w_out, g1, b1)


def _mlp_ln2(xh1, x1b, g1, b1, w_up_s, w_down, tm):
    S, D = xh1.shape
    fc = D_FF // N_CHIPS

    def body(xh_ref, xb_ref, g_ref, b_ref, wu_ref, wd_ref, xh2_ref, rstd_ref, acc_ref):
        j = pl.program_id(1)

        @pl.when(j == 0)
        def _():
            acc_ref[...] = jnp.zeros_like(acc_ref)

        a = _dot(xb_ref[...], wu_ref[0])
        r = jnp.maximum(a, 0.0)
        acc_ref[...] += _dot((r * r).astype(BF16), wd_ref[...])

        @pl.when(j == N_CHIPS - 1)
        def _():
            x1 = xh_ref[...] * g_ref[...] + b_ref[...]
            xh, rstd = _ln_fwd(ALPHA * x1 + acc_ref[...])
            xh2_ref[...] = xh
            rstd_ref[...] = rstd

    return pl.pallas_call(
        body, grid=(S // tm, N_CHIPS), name="mlp_ln2",
        in_specs=[pl.BlockSpec((tm, D), lambda i, j: (i, 0)), pl.BlockSpec((tm, D), lambda i, j: (i, 0)),
                  pl.BlockSpec((1, D), lambda i, j: (0, 0)), pl.BlockSpec((1, D), lambda i, j: (0, 0)),
                  pl.BlockSpec((1, D, fc), lambda i, j: (j, 0, 0)),
                  pl.BlockSpec((fc, D), lambda i, j: (j, 0))],
        out_specs=[pl.BlockSpec((tm, D), lambda i, j: (i, 0)), pl.BlockSpec((tm, 1), lambda i, j: (i, 0))],
        out_shape=[_sds((S, D), F32), _sds((S, 1), F32)],
        scratch_shapes=[pltpu.VMEM((tm, D), F32)],
        compiler_params=_cp(40),
    )(xh1, x1b, g1, b1, w_up_s, w_down)


def _ple_ln3_loss(xh2, rstd2, g2, b2, p, w_ple_s, w_gate, g3, b3, target, tm):
    S, D = xh2.shape
    pc = D // N_CHIPS

    def body(xh2_ref, rstd2_ref, g2_ref, b2_ref, p_ref, wp_ref, wg_ref, g3_ref, b3_ref, t_ref,
             dpre2_ref, dhb_ref, dwp_ref, dwg_ref, dg3_ref, db3_ref, dg2_ref, db2_ref, loss_ref):
        i = pl.program_id(0)

        @pl.when(i == 0)
        def _():
            for r in (dwp_ref, dwg_ref, dg3_ref, db3_ref, dg2_ref, db2_ref, loss_ref):
                r[...] = jnp.zeros_like(r)

        xh2 = xh2_ref[...]
        x2 = xh2 * g2_ref[...] + b2_ref[...]
        x2b = x2.astype(BF16)
        gate = 1.0 / (1.0 + jnp.exp(-_dot(x2b, wg_ref[...])))
        pb = p_ref[...].astype(BF16)
        pe = jnp.concatenate([_dot(pb, wp_ref[c]) for c in range(N_CHIPS)], axis=1)
        xh3, rstd3 = _ln_fwd(ALPHA * x2 + pe * gate)
        diff = xh3 * g3_ref[...] + b3_ref[...] - t_ref[...]
        loss_ref[...] += (0.5 / D) * jnp.sum(diff * diff)
        dy = diff * (1.0 / D)
        dg3_ref[...] += _colsum(dy * xh3)
        db3_ref[...] += _colsum(dy)
        dpre3 = _ln_bwd(dy, xh3, rstd3, g3_ref[...])
        dpe_b = (dpre3 * gate).astype(BF16)
        dgp_b = (dpre3 * pe * gate * (1.0 - gate)).astype(BF16)
        dx2 = ALPHA * dpre3 + _dot_nt(dgp_b, wg_ref[...])
        dwg_ref[...] += _dot_tn(x2b, dgp_b)
        for c in range(N_CHIPS):
            dwp_ref[c] += _dot_tn(pb, dpe_b[:, c * pc:(c + 1) * pc])
        dg2_ref[...] += _colsum(dx2 * xh2)
        db2_ref[...] += _colsum(dx2)
        dpre2 = _ln_bwd(dx2, xh2, rstd2_ref[...], g2_ref[...])
        dpre2_ref[...] = dpre2
        dhb_ref[...] = dpre2.astype(BF16)

    vec = _const_spec((1, D))
    return pl.pallas_call(
        body, grid=(S // tm,), name="ple_ln3_loss",
        in_specs=[_row_spec(tm, D), _row_spec(tm, 1), vec, vec, _row_spec(tm, PLE_DIM),
                  _const_spec((N_CHIPS, PLE_DIM, pc)), _const_spec((D, D)), vec, vec, _row_spec(tm, D)],
        out_specs=[_row_spec(tm, D), _row_spec(tm, D), _const_spec((N_CHIPS, PLE_DIM, pc)),
                   _const_spec((D, D)), vec, vec, vec, vec, _const_spec((1, LANES))],
        out_shape=[_sds((S, D), F32), _sds((S, D), BF16), _sds((N_CHIPS, PLE_DIM, pc), F32),
                   _sds((D, D), F32), _sds((1, D), F32), _sds((1, D), F32), _sds((1, D), F32),
                   _sds((1, D), F32), _sds((1, LANES), F32)],
        compiler_params=_cp(48),
    )(xh2, rstd2, g2, b2, p, w_ple_s, w_gate, g3, b3, target)


def _mlp_bwd(x1b, dhb, w_up_s, w_down, tm):
    S, D = x1b.shape
    fc = D_FF // N_CHIPS

    def body(xb_ref, dh_ref, wu_ref, wd_ref, dx_ref, da_ref, h1_ref):
        j = pl.program_id(1)

        @pl.when(j == 0)
        def _():
            dx_ref[...] = jnp.zeros_like(dx_ref)

        a = _dot(xb_ref[...], wu_ref[0])
        r = jnp.maximum(a, 0.0)
        h1_ref[...] = (r * r).astype(BF16)
        da = (_dot_nt(dh_ref[...], wd_ref[...]) * (2.0 * r)).astype(BF16)
        da_ref[...] = da
        dx_ref[...] += _dot_nt(da, wu_ref[0])

    return pl.pallas_call(
        body, grid=(S // tm, N_CHIPS), name="mlp_bwd",
        in_specs=[pl.BlockSpec((tm, D), lambda i, j: (i, 0)), pl.BlockSpec((tm, D), lambda i, j: (i, 0)),
                  pl.BlockSpec((1, D, fc), lambda i, j: (j, 0, 0)),
                  pl.BlockSpec((fc, D), lambda i, j: (j, 0))],
        out_specs=[pl.BlockSpec((tm, D), lambda i, j: (i, 0)), pl.BlockSpec((tm, fc), lambda i, j: (i, j)),
                   pl.BlockSpec((tm, fc), lambda i, j: (i, j))],
        out_shape=[_sds((S, D), F32), _sds((S, D_FF), BF16), _sds((S, D_FF), BF16)],
        compiler_params=_cp(40),
    )(x1b, dhb, w_up_s, w_down)


def _tn_matmul(a, b, name, tk, tt, stacked):
    T, K = a.shape
    N = b.shape[1]
    tn = 1024

    def body(a_ref, b_ref, o_ref):
        @pl.when(pl.program_id(2) == 0)
        def _():
            o_ref[...] = jnp.zeros_like(o_ref)

        prod = _dot_tn(a_ref[...], b_ref[...])
        if stacked:
            o_ref[0] += prod
        else:
            o_ref[...] += prod

    if stacked:
        out_spec = pl.BlockSpec((1, tk, tn), lambda k, n, t: (n, k, 0))
        out_shape = _sds((N // tn, K, tn), F32)
    else:
        out_spec = pl.BlockSpec((tk, tn), lambda k, n, t: (k, n))
        out_shape = _sds((K, N), F32)
    return pl.pallas_call(
        body, grid=(K // tk, N // tn, T // tt), name=name,
        in_specs=[pl.BlockSpec((tt, tk), lambda k, n, t: (t, k)),
                  pl.BlockSpec((tt, tn), lambda k, n, t: (t, n))],
        out_specs=out_spec, out_shape=out_shape,
        compiler_params=_cp(40),
    )(a, b)


def _mix_bwd(dpre2, dx1m, xh1, rstd1, g1, w_out, on, pooled, tm):
    S, D = xh1.shape

    def body(dp2_ref, dxm_ref, xh_ref, rstd_ref, g_ref, w_ref, on_ref, po_ref,
             dpre1_ref, don_ref, dpo_ref, dw_ref, dg_ref, db_ref):
        @pl.when(pl.program_id(0) == 0)
        def _():
            for r in (dw_ref, dg_ref, db_ref):
                r[...] = jnp.zeros_like(r)

        xh = xh_ref[...]
        dx1 = ALPHA * dp2_ref[...] + dxm_ref[...]
        dg_ref[...] += _colsum(dx1 * xh)
        db_ref[...] += _colsum(dx1)
        dpre1 = _ln_bwd(dx1, xh, rstd_ref[...], g_ref[...])
        dpre1_ref[...] = dpre1
        dmb = dpre1.astype(BF16)
        dcat = _dot_nt(dmb, w_ref[...])
        don_ref[...] = dcat[:, :ATTN_WIDTH]
        dpo_ref[...] = dcat[:, ATTN_WIDTH:]
        dw_ref[:ATTN_WIDTH, :] += _dot_tn(on_ref[...], dmb)
        dw_ref[ATTN_WIDTH:, :] += _dot_tn(po_ref[...], dmb)

    vec = _const_spec((1, D))
    return pl.pallas_call(
        body, grid=(S // tm,), name="mix_bwd",
        in_specs=[_row_spec(tm, D), _row_spec(tm, D), _row_spec(tm, D), _row_spec(tm, 1), vec,
                  _const_spec((D, D)), _row_spec(tm, ATTN_WIDTH), _row_spec(tm, POOL_WIDTH)],
        out_specs=[_row_spec(tm, D), _row_spec(tm, ATTN_WIDTH), _row_spec(tm, POOL_WIDTH),
                   _const_spec((D, D)), vec, vec],
        out_shape=[_sds((S, D), F32), _sds((S, ATTN_WIDTH), F32), _sds((S, POOL_WIDTH), F32),
                   _sds((D, D), F32), _sds((1, D), F32), _sds((1, D), F32)],
        compiler_params=_cp(48),
    )(dpre2, dx1m, xh1, rstd1, g1, w_out, on, pooled)


def _pool_bwd(dpooled, d_b, w_pool, pscale, tm):
    S = dpooled.shape[0]
    hb = tm // POOL_HALO
    n_t = S // tm
    te = tm + POOL_HALO

    def body(dp_ref, dph_ref, d_ref, wp_ref, sc_ref, du_ref, dwp_ref, dsc_ref):
        i = pl.program_id(0)

        @pl.when(i == 0)
        def _():
            dwp_ref[...] = jnp.zeros_like(dwp_ref)
            dsc_ref[...] = jnp.zeros_like(dsc_ref)

        halo = jnp.where(i < n_t - 1, dph_ref[...], 0.0)
        pos = i * tm + lax.broadcasted_iota(jnp.int32, (te, 1), 0)
        for g in range(N_POOL_GROUPS):
            win = 2 ** (g + 1)
            cols = slice(g * POOL_GROUP, (g + 1) * POOL_GROUP)
            wpb = wp_ref[g].astype(BF16)
            dpt = dp_ref[:, cols]
            dpe = jnp.concatenate([dpt, halo[:, cols]], axis=0)
            dyb = (dpe * sc_ref[:, cols]).astype(BF16)
            dd = _dot_nt(dyb, wpb)
            s = dd / jnp.minimum(pos + 1, win).astype(F32)
            for sh in (1, 2, 4, 8)[:g + 1]:
                s = s + pltpu.roll(s, te - sh, 0)
            du_ref[:, cols] = s[:tm, :] - dd[:tm, :]
            db = d_ref[:, cols]
            dwp_ref[g] += _dot_tn(db, dyb[:tm, :])
            dsc_ref[:, cols] += _colsum(dpt * _dot(db, wpb))

    return pl.pallas_call(
        body, grid=(n_t,), name="pool_bwd",
        in_specs=[_row_spec(tm, POOL_WIDTH),
                  pl.BlockSpec((POOL_HALO, POOL_WIDTH),
                               lambda i: (jnp.minimum((i + 1) * hb, S // POOL_HALO - 1), 0)),
                  _row_spec(tm, POOL_WIDTH),
                  _const_spec((N_POOL_GROUPS, POOL_GROUP, POOL_GROUP)), _const_spec((1, POOL_WIDTH))],
        out_specs=[_row_spec(tm, POOL_WIDTH), _const_spec((N_POOL_GROUPS, POOL_GROUP, POOL_GROUP)),
                   _const_spec((1, POOL_WIDTH))],
        out_shape=[_sds((S, POOL_WIDTH), F32), _sds((N_POOL_GROUPS, POOL_GROUP, POOL_GROUP), F32),
                   _sds((1, POOL_WIDTH), F32)],
        compiler_params=_cp(32),
    )(dpooled, dpooled, d_b, w_pool, pscale)


def _attn_bwd(q, k, v, don, o_raw, ga, tq):
    S = q.shape[0]
    nq = S // tq

    def body(q_ref, k_ref, v_ref, don_ref, o_ref, ga_ref, dq_ref, dk_ref, dv_ref, dga_ref, g_s, b_s):
        i = pl.program_id(1)

        @pl.when(i == 0)
        def _():
            for r in (dk_ref, dv_ref, dga_ref):
                r[...] = jnp.zeros_like(r)

        lane = lax.broadcasted_iota(jnp.int32, (1, PAIR), 1)
        m0 = lane < HEAD_DIM
        low = _tri(tq, upper=False)
        upp = _tri(tq, upper=True)
        diff = (lax.broadcasted_iota(jnp.int32, (tq, tq), 1)
                - lax.broadcasted_iota(jnp.int32, (tq, tq), 0))

        def seg_mean(a):
            s0 = jnp.sum(jnp.where(m0, a, 0.0), axis=-1, keepdims=True)
            s1 = jnp.sum(jnp.where(m0, 0.0, a), axis=-1, keepdims=True)
            return jnp.where(m0, s0, s1) * (1.0 / HEAD_DIM)

        o = o_ref[...]
        rs = lax.rsqrt(seg_mean(o * o) + RMS_EPS)
        oh = o * rs
        don = don_ref[...]
        dga_ref[...] += _colsum(don * oh)
        doh = don * ga_ref[...]
        do = rs * (doh - oh * seg_mean(doh * oh))
        dob = do.astype(BF16)
        q2 = q_ref[...]
        dqs = []
        for hh in range(2):
            mask = m0 if hh == 0 else jnp.logical_not(m0)
            qh = jnp.where(mask, q2, jnp.zeros_like(q2))
            dh = jnp.where(mask, dob, jnp.zeros_like(dob))

            def sweep_down(n, c_l, qh=qh, dh=dh):
                kb = i - n
                ks = pl.multiple_of(kb * tq, tq)
                kt = k_ref[pl.ds(ks, tq), :]
                vt = v_ref[pl.ds(ks, tq), :]
                z = _dot_nt(qh, kt)
                lr = _neg_softplus(z)
                valid = diff < n * tq
                l = jnp.where(valid, lr, 0.0)
                hi, lo = _split_bf16(l)
                sfx = _dot(hi, low) + _dot(lo, low) + c_l
                ls = z + lr
                w = jnp.where(valid, jnp.exp(ls + sfx), 0.0)
                g_s[kb] = _dot_nt(dh, vt) * w
                b_s[kb] = jnp.exp(ls)
                dv_ref[pl.ds(ks, tq), :] += _dot_tn(w.astype(BF16), dh)
                return c_l + jnp.sum(l, axis=1, keepdims=True)

            lax.fori_loop(0, i + 1, sweep_down, jnp.zeros((tq, 1), F32))

            def sweep_up(kb, carry, qh=qh):
                c_g, acc = carry
                ks = pl.multiple_of(kb * tq, tq)
                kt = k_ref[pl.ds(ks, tq), :]
                g = g_s[kb]
                beta = b_s[kb]
                hi, lo = _split_bf16(g)
                pre = _dot(hi, upp) + _dot(lo, upp) + c_g
                valid = diff < (i - kb) * tq
                dzb = jnp.where(valid, g * (1.0 - beta) - beta * pre, 0.0).astype(BF16)
                acc = acc + _dot(dzb, kt)
                dk_ref[pl.ds(ks, tq), :] += _dot_tn(dzb, qh)
                return c_g + jnp.sum(g, axis=1, keepdims=True), acc

            _, acc = lax.fori_loop(0, i + 1, sweep_up,
                                   (jnp.zeros((tq, 1), F32), jnp.zeros((tq, PAIR), F32)))
            dqs.append(acc)
        dq_ref[...] = jnp.where(m0, dqs[0], dqs[1]) * Q_SCALE

    return pl.pallas_call(
        body, grid=(N_PAIRS, nq), name="attn_bwd",
        in_specs=[pl.BlockSpec((tq, PAIR), lambda p, i: (i, p)),
                  pl.BlockSpec((S, PAIR), lambda p, i: (0, p)),
                  pl.BlockSpec((S, PAIR), lambda p, i: (0, p)),
                  pl.BlockSpec((tq, PAIR), lambda p, i: (i, p)),
                  pl.BlockSpec((tq, PAIR), lambda p, i: (i, p)),
                  pl.BlockSpec((1, PAIR), lambda p, i: (0, p))],
        out_specs=[pl.BlockSpec((tq, PAIR), lambda p, i: (i, p)),
                   pl.BlockSpec((S, PAIR), lambda p, i: (0, p)),
                   pl.BlockSpec((S, PAIR), lambda p, i: (0, p)),
                   pl.BlockSpec((1, PAIR), lambda p, i: (0, p))],
        out_shape=[_sds((S, ATTN_WIDTH), F32), _sds((S, ATTN_WIDTH), F32), _sds((S, ATTN_WIDTH), F32),
                   _sds((1, ATTN_WIDTH), F32)],
        scratch_shapes=[pltpu.VMEM((nq, tq, tq), F32), pltpu.VMEM((nq, tq, tq), F32)],
        compiler_params=_cp(56),
    )(q, k, v, don, o_raw, ga)


def _inproj_bwd(dq, dk, dv, du, dpre1, xh0, rstd0, g0, b0, w_in_s, tm):
    S, D = xh0.shape

    def body(dq_ref, dk_ref, dv_ref, du_ref, dp1_ref, xh_ref, rstd_ref, g_ref, b_ref, w_ref,
             gx_ref, dw_ref, dg_ref, db_ref):
        @pl.when(pl.program_id(0) == 0)
        def _():
            for r in (dw_ref, dg_ref, db_ref):
                r[...] = jnp.zeros_like(r)

        xh = xh_ref[...]
        xb = (xh * g_ref[...] + b_ref[...]).astype(BF16)
        dx0 = ALPHA * dp1_ref[...]
        for c, r in enumerate((dq_ref, dk_ref, dv_ref, du_ref)):
            dpb = r[...].astype(BF16)
            dx0 = dx0 + _dot_nt(dpb, w_ref[c])
            dw_ref[c] += _dot_tn(xb, dpb)
        dg_ref[...] += _colsum(dx0 * xh)
        db_ref[...] += _colsum(dx0)
        gx_ref[...] = _ln_bwd(dx0, xh, rstd_ref[...], g_ref[...])

    vec = _const_spec((1, D))
    half = _row_spec(tm, 512)
    return pl.pallas_call(
        body, grid=(S // tm,), name="inproj_bwd",
        in_specs=[half, half, half, half, _row_spec(tm, D), _row_spec(tm, D), _row_spec(tm, 1), vec, vec,
                  _const_spec((N_CHIPS, D, 512))],
        out_specs=[_row_spec(tm, D), _const_spec((N_CHIPS, D, 512)), vec, vec],
        out_shape=[_sds((S, D), F32), _sds((N_CHIPS, D, 512), F32), _sds((1, D), F32), _sds((1, D), F32)],
        compiler_params=_cp(56),
    )(dq, dk, dv, du, dpre1, xh0, rstd0, g0, b0, w_in_s)


def _place():
    return lax.axis_index("x"), lax.axis_index("y"), lax.axis_index("c")


CHIP_FLIPS = ((0, 1), (1, 0), (1, 1))


def _exchange(name, ins, out_shapes, plan):
    n_in, n_out = len(ins), len(out_shapes)

    def body(*refs):
        in_refs, out_refs = refs[:n_in], refs[n_in:n_in + n_out]
        lsem, ssem, rsem = refs[n_in + n_out:]
        local, remote = plan(in_refs, out_refs)
        copies = [pltpu.make_async_copy(s, d, lsem.at[n]) for n, (s, d) in enumerate(local)]
        copies += [pltpu.make_async_remote_copy(src_ref=s, dst_ref=d, send_sem=ssem.at[n], recv_sem=rsem.at[n],
                                                device_id=dev, device_id_type=MESH)
                   for n, (s, d, dev) in enumerate(remote)]
        for cp in copies:
            cp.start()
        for cp in copies:
            cp.wait()

    n_local, n_remote = plan(None, None)
    return pl.pallas_call(
        body, name=name, in_specs=[HBM_SPEC] * n_in, out_specs=[HBM_SPEC] * n_out, out_shape=out_shapes,
        scratch_shapes=[pltpu.SemaphoreType.DMA((max(n_local, 1),)), pltpu.SemaphoreType.DMA((n_remote,)),
                        pltpu.SemaphoreType.DMA((n_remote,))],
    )(*ins)


def _all_gather_weights(shards):
    n = len(shards)

    def plan(in_refs, out_refs):
        if in_refs is None:
            return n, n * len(CHIP_FLIPS)
        x, y, c = _place()
        slot = 2 * x + y
        local = [(s, o.at[slot]) for s, o in zip(in_refs, out_refs)]
        remote = [(s, o.at[slot], (x ^ fx, y ^ fy, c))
                  for s, o in zip(in_refs, out_refs) for fx, fy in CHIP_FLIPS]
        return local, remote

    return _exchange("gather_weights", shards, [_sds((N_CHIPS,) + s.shape, s.dtype) for s in shards], plan)


def _pair_swap(grads):
    n = len(grads)

    def plan(in_refs, out_refs):
        if in_refs is None:
            return 0, n
        x, y, c = _place()
        return [], [(g.at[:, 1 - c], o, (x, y, 1 - c)) for g, o in zip(in_refs, out_refs)]

    return _exchange("reduce_pair", grads, [_sds((N_CHIPS,) + g.shape[2:], g.dtype) for g in grads], plan)


def _chip_scatter(parts):
    n = len(parts)

    def plan(in_refs, out_refs):
        if in_refs is None:
            return 0, n * len(CHIP_FLIPS)
        x, y, c = _place()
        remote = []
        for r, o in zip(in_refs, out_refs):
            for f, (fx, fy) in enumerate(CHIP_FLIPS):
                remote.append((r.at[2 * (x ^ fx) + (y ^ fy)], o.at[f], (x ^ fx, y ^ fy, c)))
        return [], remote

    return _exchange("reduce_chips", parts,
                     [_sds((len(CHIP_FLIPS),) + r.shape[1:], r.dtype) for r in parts], plan)


def _pair_gather(halves):
    n = len(halves)

    def plan(in_refs, out_refs):
        if in_refs is None:
            return n, n
        x, y, c = _place()
        local = [(h, o.at[c]) for h, o in zip(in_refs, out_refs)]
        remote = [(h, o.at[c], (x, y, 1 - c)) for h, o in zip(in_refs, out_refs)]
        return local, remote

    return _exchange("gather_pair", halves, [_sds((2,) + h.shape, h.dtype) for h in halves], plan)


def _add_pair(grad, recv, place, name):
    _, _, H, C = grad.shape
    th = min(H, 256)

    def body(pl_ref, g_ref, r_ref, o_ref):
        o_ref[...] = g_ref[:, 0] + r_ref[...]

    return pl.pallas_call(
        body, name=name,
        grid_spec=pltpu.PrefetchScalarGridSpec(
            num_scalar_prefetch=1, grid=(N_CHIPS, H // th),
            in_specs=[pl.BlockSpec((1, 1, th, C), lambda j, h, pr: (j, pr[0], h, 0)),
                      pl.BlockSpec((1, th, C), lambda j, h, pr: (j, h, 0))],
            out_specs=pl.BlockSpec((1, th, C), lambda j, h, pr: (j, h, 0))),
        out_shape=_sds((N_CHIPS, H, C), F32),
    )(place, grad, recv)


def _add_chips(part, recv, place, name):
    _, H, C = part.shape
    th = min(H, 256)

    def body(pl_ref, p_ref, r_ref, o_ref):
        o_ref[...] = p_ref[0] + r_ref[0] + r_ref[1] + r_ref[2]

    return pl.pallas_call(
        body, name=name,
        grid_spec=pltpu.PrefetchScalarGridSpec(
            num_scalar_prefetch=1, grid=(H // th,),
            in_specs=[pl.BlockSpec((1, th, C), lambda h, pr: (pr[1], h, 0)),
                      pl.BlockSpec((len(CHIP_FLIPS), th, C), lambda h, pr: (0, h, 0))],
            out_specs=pl.BlockSpec((th, C), lambda h, pr: (h, 0))),
        out_shape=_sds((H, C), F32),
    )(place, part, recv)


def _adamw_math(w, g, m, v):
    m = ADAM_B1 * m + (1.0 - ADAM_B1) * g
    v = ADAM_B2 * v + (1.0 - ADAM_B2) * (g * g)
    m_hat = m / (1.0 - ADAM_B1 ** ADAM_STEP)
    v_hat = v / (1.0 - ADAM_B2 ** ADAM_STEP)
    delta = -ADAM_LR * (m_hat / (jnp.sqrt(v_hat) + ADAM_EPS) + ADAM_WD * w)
    return delta, m, v


def _adamw(w, g, m, v, name):
    R, C = w.shape
    tr = min(R, 256)

    def body(w_ref, g_ref, m_ref, v_ref, d_ref, mo_ref, vo_ref):
        d, mo, vo = _adamw_math(w_ref[...], g_ref[...], m_ref[...], v_ref[...])
        d_ref[...] = d
        mo_ref[...] = mo
        vo_ref[...] = vo

    spec = _row_spec(tr, C)
    return pl.pallas_call(
        body, grid=(R // tr,), name=name, in_specs=[spec] * 4, out_specs=[spec] * 3,
        out_shape=[_sds((R, C), F32)] * 3,
    )(w, g, m, v)


def _small_allreduce_adamw(gpack, wpack, mpack, vpack):
    R = gpack.shape[0]
    flips = [(fx, fy, fc) for fx in (0, 1) for fy in (0, 1) for fc in (0, 1)][1:]

    def body(g_ref, w_ref, m_ref, v_ref, gs_ref, d_ref, mo_ref, vo_ref, recv, lsem, ssem, rsem):
        x, y, c = _place()
        me = 4 * x + 2 * y + c
        own = pltpu.make_async_copy(g_ref, recv.at[me], lsem)
        sends = [pltpu.make_async_remote_copy(src_ref=g_ref, dst_ref=recv.at[me], send_sem=ssem.at[n],
                                              recv_sem=rsem.at[n], device_id=(x ^ fx, y ^ fy, c ^ fc),
                                              device_id_type=MESH)
                 for n, (fx, fy, fc) in enumerate(flips)]
        own.start()
        for cp in sends:
            cp.start()
        own.wait()
        for cp in sends:
            cp.wait()
        total = recv[0]
        for dev in range(1, N_DEV):
            total = total + recv[dev]
        gs_ref[...] = total
        d, mo, vo = _adamw_math(w_ref[...], total, m_ref[...], v_ref[...])
        d_ref[...] = d
        mo_ref[...] = mo
        vo_ref[...] = vo

    return pl.pallas_call(
        body, name="small_allreduce_adamw", in_specs=[VMEM_SPEC] * 4, out_specs=[VMEM_SPEC] * 4,
        out_shape=[_sds((R, LANES), F32)] * 4,
        scratch_shapes=[pltpu.VMEM((N_DEV, R, LANES), F32), pltpu.SemaphoreType.DMA,
                        pltpu.SemaphoreType.DMA((len(flips),)), pltpu.SemaphoreType.DMA((len(flips),))],
    )(gpack, wpack, mpack, vpack)


def _rows8(a):
    a = a.reshape(-1, LANES)
    pad = (-a.shape[0]) % 8
    return jnp.pad(a, ((0, pad), (0, 0))) if pad else a


def _pack(parts):
    return jnp.concatenate([_rows8(a) for a in parts], axis=0)


def _unpack(pack, like):
    out, row = [], 0
    for a in like:
        n = a.size // LANES
        out.append(pack[row:row + n].reshape(a.shape))
        row += n + (-n) % 8
    return out


def kernel(x, p, emb_ln_g, emb_ln_b, w_in, attn_out_g, w_pool, pool_scale, w_out, ln1_g, ln1_b, w_up, w_down, ln2_g, ln2_b, w_ple, w_ple_gate, ln3_g, ln3_b, loss_target, m_emb_ln_g, m_emb_ln_b, m_w_in, m_attn_out_g, m_w_pool, m_pool_scale, m_w_out, m_ln1_g, m_ln1_b, m_w_up, m_w_down, m_ln2_g, m_ln2_b, m_w_ple, m_w_ple_gate, m_ln3_g, m_ln3_b, v_emb_ln_g, v_emb_ln_b, v_w_in, v_attn_out_g, v_w_pool, v_pool_scale, v_w_out, v_ln1_g, v_ln1_b, v_w_up, v_w_down, v_ln2_g, v_ln2_b, v_w_ple, v_w_ple_gate, v_ln3_g, v_ln3_b):
    S = x.shape[1]
    tm = min(256, S)
    tq = min(256, S)
    xs = x[0]
    ps = p[0, 0]
    tgt = loss_target[0]
    row = lambda a: a.reshape(1, -1)
    g0, b0 = row(emb_ln_g), row(emb_ln_b)
    g1, b1, g2, b2, g3, b3 = ln1_g, ln1_b, ln2_g, ln2_b, ln3_g, ln3_b
    wp = w_pool[0]

    big = [w_in[0], w_out[0], w_up[0], w_down[0], w_ple[0], w_ple_gate[0]]
    w_in_s, w_out_s, w_up_s, w_down_s, w_ple_s, w_gate_s = _all_gather_weights([w.astype(BF16) for w in big])
    w_out_f = w_out_s.reshape(D_MODEL, D_MODEL)
    w_down_f = w_down_s.reshape(D_FF, D_MODEL)
    w_gate_f = w_gate_s.reshape(D_MODEL, D_MODEL)

    xh0, rstd0, q, k, v, u = _embln_inproj(xs, g0, b0, w_in_s, tm)
    o_raw, on = _attn_fwd(q, k, v, attn_out_g, tq)
    d_b, pooled = _pool_fwd(u, wp, pool_scale, tm)
    xh1, rstd1, x1b = _mix_ln1(on, pooled, xh0, g0, b0, w_out_f, g1, b1, tm)
    xh2, rstd2 = _mlp_ln2(xh1, x1b, g1, b1, w_up_s, w_down_f, tm)

    (dpre2, dhb, dw_ple, dw_gate, dg3, db3, dg2, db2, loss_row) = _ple_ln3_loss(
        xh2, rstd2, g2, b2, ps, w_ple_s, w_gate_f, g3, b3, tgt, tm)
    dx1m, da, h1 = _mlp_bwd(x1b, dhb, w_up_s, w_down_f, tm)
    dw_up = _tn_matmul(x1b, da, "grad_w_up", 512, min(512, S), stacked=True)
    dw_down = _tn_matmul(h1, dhb, "grad_w_down", 512, min(512, S), stacked=False)
    dpre1, don, dpooled, dw_out, dg1, db1 = _mix_bwd(dpre2, dx1m, xh1, rstd1, g1, w_out_f, on, pooled, tm)
    du, dwp, dsc = _pool_bwd(dpooled, d_b, wp, pool_scale, tm)
    dq, dk, dv, dga = _attn_bwd(q, k, v, don, o_raw, attn_out_g, tq)
    grad_x, dw_in, dg0, db0 = _inproj_bwd(dq, dk, dv, du, dpre1, xh0, rstd0, g0, b0, w_in_s, tm)

    xi, yi, ci = _place()
    place = jnp.stack([ci, 2 * xi + yi]).astype(jnp.int32)
    names = ["w_in", "w_out", "w_up", "w_down", "w_ple", "w_ple_gate"]
    full = [dw_in, dw_out.reshape(N_CHIPS, D_MODEL // N_CHIPS, D_MODEL), dw_up,
            dw_down.reshape(N_CHIPS, D_FF // N_CHIPS, D_MODEL), dw_ple,
            dw_gate.reshape(N_CHIPS, D_MODEL // N_CHIPS, D_MODEL)]
    split = [g.reshape(N_CHIPS, 2, g.shape[1] // 2, g.shape[2]) for g in full]
    from_pair = _pair_swap(split)
    pair_sum = [_add_pair(g, r, place, "pair_sum_" + n) for g, r, n in zip(split, from_pair, names)]
    from_chips = _chip_scatter(pair_sum)
    halves = [_add_chips(s, r, place, "chip_sum_" + n) for s, r, n in zip(pair_sum, from_chips, names)]
    shard_grads = [g.reshape(w.shape) for g, w in zip(_pair_gather(halves), big)]

    ms = [m_w_in, m_w_out, m_w_up, m_w_down, m_w_ple, m_w_ple_gate]
    vs = [v_w_in, v_w_out, v_w_up, v_w_down, v_w_ple, v_w_ple_gate]
    big_out = {}
    for n, w, g, m, vv in zip(names, big, shard_grads, ms, vs):
        d, mo, vo = _adamw(w, g, m[0], vv[0], "adamw_" + n)
        big_out[n] = tuple(a.reshape(m.shape) for a in (g, d, mo, vo))

    small_names = ["emb_ln_g", "emb_ln_b", "attn_out_g", "w_pool", "pool_scale", "ln1_g", "ln1_b",
                   "ln2_g", "ln2_b", "ln3_g", "ln3_b"]
    small_w = [emb_ln_g, emb_ln_b, attn_out_g, w_pool, pool_scale, ln1_g, ln1_b, ln2_g, ln2_b, ln3_g, ln3_b]
    small_m = [m_emb_ln_g, m_emb_ln_b, m_attn_out_g, m_w_pool, m_pool_scale, m_ln1_g, m_ln1_b, m_ln2_g,
               m_ln2_b, m_ln3_g, m_ln3_b]
    small_v = [v_emb_ln_g, v_emb_ln_b, v_attn_out_g, v_w_pool, v_pool_scale, v_ln1_g, v_ln1_b, v_ln2_g,
               v_ln2_b, v_ln3_g, v_ln3_b]
    small_g = [dg0, db0, dga, dwp, dsc, dg1, db1, dg2, db2, dg3, db3]
    loss_like = jnp.zeros((8, LANES), F32)
    gpack = _pack([jnp.broadcast_to(loss_row, (8, LANES))] + small_g)
    gs, ds, mos, vos = _small_allreduce_adamw(gpack, _pack([loss_like] + small_w), _pack([loss_like] + small_m),
                                              _pack([jnp.ones((8, LANES), F32)] + small_v))
    like = [loss_like] + small_w
    gs_u, ds_u, mos_u, vos_u = (_unpack(a, like) for a in (gs, ds, mos, vos))
    loss = gs_u[0][0, 0]
    small_out = {n: (gs_u[i + 1], ds_u[i + 1], mos_u[i + 1], vos_u[i + 1]) for i, n in enumerate(small_names)}

    order = ["emb_ln_g", "emb_ln_b", "w_in", "attn_out_g", "w_pool", "pool_scale", "w_out", "ln1_g", "ln1_b",
             "w_up", "w_down", "ln2_g", "ln2_b", "w_ple", "w_ple_gate", "ln3_g", "ln3_b"]
    res = {**big_out, **small_out}
    outs = [loss, grad_x.reshape(x.shape)]
    for kind in range(4):
        outs += [res[n][kind] for n in order]
    return tuple(outs)
```

```python
import functools

import jax
import jax.numpy as jnp
from jax import lax
from jax.experimental import pallas as pl
from jax.experimental.pallas import tpu as pltpu

F32 = jnp.float32
BF16 = jnp.bfloat16

D_MODEL = 1024
ATTN_WIDTH = 512
POOL_WIDTH = 512
HEAD_DIM = 64
PAIR = 2 * HEAD_DIM
N_PAIRS = ATTN_WIDTH // PAIR
N_POOL_GROUPS = 4
POOL_GROUP = 128
POOL_HALO = 16
D_FF = 4096
PLE_DIM = 256
N_CHIPS = 4
N_DEV = 8
LN_EPS = 1e-5
RMS_EPS = 1e-6
ALPHA = float(2.0 ** 0.25)
Q_SCALE = 0.125
ADAM_LR = 0.001
ADAM_B1 = 0.9
ADAM_B2 = 0.999
ADAM_EPS = 1e-08
ADAM_WD = 0.01
ADAM_STEP = 10
LANES = 128
MIB = 1024 * 1024

MESH = pl.DeviceIdType.MESH
HBM_SPEC = pl.BlockSpec(memory_space=pltpu.HBM)
VMEM_SPEC = pl.BlockSpec(memory_space=pltpu.VMEM)


def _cp(vmem_mib):
    return pltpu.CompilerParams(vmem_limit_bytes=vmem_mib * MIB)


def _dot(a, b):
    return jnp.dot(a, b, preferred_element_type=F32)


def _dot_nt(a, b):
    return lax.dot_general(a, b, (((1,), (1,)), ((), ())), preferred_element_type=F32)


def _dot_tn(a, b):
    return lax.dot_general(a, b, (((0,), (0,)), ((), ())), preferred_element_type=F32)


def _ln_fwd(pre):
    mu = jnp.mean(pre, axis=-1, keepdims=True)
    xc = pre - mu
    var = jnp.mean(xc * xc, axis=-1, keepdims=True)
    rstd = lax.rsqrt(var + LN_EPS)
    return xc * rstd, rstd


def _ln_bwd(dy, xh, rstd, g):
    dxh = dy * g
    m1 = jnp.mean(dxh, axis=-1, keepdims=True)
    m2 = jnp.mean(dxh * xh, axis=-1, keepdims=True)
    return rstd * (dxh - m1 - xh * m2)


def _colsum(a):
    return jnp.sum(a, axis=0, keepdims=True)


def _neg_softplus(z):
    return -(jnp.maximum(z, 0.0) + jnp.log(1.0 + jnp.exp(-jnp.abs(z))))


def _split_bf16(a):
    hi = a.astype(BF16)
    lo = (a - hi.astype(F32)).astype(BF16)
    return hi, lo


def _row_spec(tm, n):
    return pl.BlockSpec((tm, n), lambda i: (i, 0))


def _const_spec(shape):
    nd = len(shape)
    return pl.BlockSpec(shape, lambda *_: (0,) * nd)


def _sds(shape, dtype):
    return jax.ShapeDtypeStruct(shape, dtype)


def _embln_inproj(x, g0, b0, w_in_s, tm):
    S, D = x.shape

    def body(x_ref, g_ref, b_ref, w_ref, xh_ref, rstd_ref, q_ref, k_ref, v_ref, u_ref):
        xh, rstd = _ln_fwd(x_ref[...])
        xh_ref[...] = xh
        rstd_ref[...] = rstd
        xb = (xh * g_ref[...] + b_ref[...]).astype(BF16)
        q_ref[...] = (_dot(xb, w_ref[0]) * Q_SCALE).astype(BF16)
        k_ref[...] = _dot(xb, w_ref[1]).astype(BF16)
        v_ref[...] = _dot(xb, w_ref[2]).astype(BF16)
        u_ref[...] = _dot(xb, w_ref[3])

    return pl.pallas_call(
        body, grid=(S // tm,), name="embln_inproj",
        in_specs=[_row_spec(tm, D), _const_spec((1, D)), _const_spec((1, D)),
                  _const_spec((N_CHIPS, D, 512))],
        out_specs=[_row_spec(tm, D), _row_spec(tm, 1), _row_spec(tm, 512), _row_spec(tm, 512),
                   _row_spec(tm, 512), _row_spec(tm, 512)],
        out_shape=[_sds((S, D), F32), _sds((S, 1), F32), _sds((S, 512), BF16), _sds((S, 512), BF16),
                   _sds((S, 512), BF16), _sds((S, 512), F32)],
        compiler_params=_cp(40),
    )(x, g0, b0, w_in_s)


def _tri(n, upper):
    r = lax.broadcasted_iota(jnp.int32, (n, n), 0)
    c = lax.broadcasted_iota(jnp.int32, (n, n), 1)
    keep = (r < c) if upper else (r > c)
    return jnp.where(keep, 1.0, 0.0).astype(BF16)


def _strictly_causal(n):
    return lax.broadcasted_iota(jnp.int32, (n, n), 1) < lax.broadcasted_iota(jnp.int32, (n, n), 0)


LOG_WEIGHT_FLOOR = -110.0


def _sb_tile(qh, kt, low, c_l, valid):
    z = _dot_nt(qh, kt)
    lr = _neg_softplus(z)
    l = lr if valid is None else jnp.where(valid, lr, 0.0)
    hi, lo = _split_bf16(l)
    sfx = _dot(hi, low) + _dot(lo, low) + c_l
    ls = z + lr
    w = jnp.exp(ls + sfx)
    if valid is not None:
        w = jnp.where(valid, w, 0.0)
    return ls, l, w


def _attn_fwd(q, k, v, ga, tq):
    S = q.shape[0]
    nq = S // tq

    def body(q_ref, k_ref, v_ref, ga_ref, o_ref, on_ref):
        i = pl.program_id(1)
        lane = lax.broadcasted_iota(jnp.int32, (1, PAIR), 1)
        m0 = lane < HEAD_DIM
        low = _tri(tq, upper=False)
        q2 = q_ref[...]
        qhs = [jnp.where(m0, q2, jnp.zeros_like(q2)), jnp.where(m0, jnp.zeros_like(q2), q2)]

        def tile(kb, c_ls, accs, valid):
            ks = pl.multiple_of(kb * tq, tq)
            kt = k_ref[pl.ds(ks, tq), :]
            vt = v_ref[pl.ds(ks, tq), :]
            new_c, new_a = [], []
            for hh in range(2):
                _, l, w = _sb_tile(qhs[hh], kt, low, c_ls[hh], valid)
                new_a.append(accs[hh] + _dot(w.astype(BF16), vt))
                new_c.append(c_ls[hh] + jnp.sum(l, axis=1, keepdims=True))
            return new_c, new_a

        zc, za = jnp.zeros((tq, 1), F32), jnp.zeros((tq, PAIR), F32)
        c_ls, accs = tile(i, [zc, zc], [za, za], _strictly_causal(tq))

        def more(st):
            return jnp.logical_and(st[0] <= i, jnp.max(jnp.maximum(st[1], st[2])) > LOG_WEIGHT_FLOOR)

        def step(st):
            n, c0, c1, a0, a1 = st
            c_ls, accs = tile(i - n, [c0, c1], [a0, a1], None)
            return (n + 1, c_ls[0], c_ls[1], accs[0], accs[1])

        st = lax.while_loop(more, step, (jnp.int32(1), c_ls[0], c_ls[1], accs[0], accs[1]))
        o = jnp.where(m0, st[3], st[4])
        o_ref[...] = o
        sq = o * o
        ms0 = jnp.sum(jnp.where(m0, sq, 0.0), axis=-1, keepdims=True) * (1.0 / HEAD_DIM)
        ms1 = jnp.sum(jnp.where(m0, 0.0, sq), axis=-1, keepdims=True) * (1.0 / HEAD_DIM)
        rs = jnp.where(m0, lax.rsqrt(ms0 + RMS_EPS), lax.rsqrt(ms1 + RMS_EPS))
        on_ref[...] = (o * rs * ga_ref[...]).astype(BF16)

    return pl.pallas_call(
        body, grid=(N_PAIRS, nq), name="attn_fwd",
        in_specs=[pl.BlockSpec((tq, PAIR), lambda p, i: (i, p)),
                  pl.BlockSpec((S, PAIR), lambda p, i: (0, p)),
                  pl.BlockSpec((S, PAIR), lambda p, i: (0, p)),
                  pl.BlockSpec((1, PAIR), lambda p, i: (0, p))],
        out_specs=[pl.BlockSpec((tq, PAIR), lambda p, i: (i, p)),
                   pl.BlockSpec((tq, PAIR), lambda p, i: (i, p))],
        out_shape=[_sds((S, ATTN_WIDTH), F32), _sds((S, ATTN_WIDTH), BF16)],
        compiler_params=_cp(40),
    )(q, k, v, ga)


def _pool_fwd(u, w_pool, pscale, tm):
    S = u.shape[0]
    hb = tm // POOL_HALO

    def body(u_ref, uh_ref, wp_ref, sc_ref, d_ref, pooled_ref):
        i = pl.program_id(0)
        halo = jnp.where(i > 0, uh_ref[...], 0.0)
        pos = i * tm + lax.broadcasted_iota(jnp.int32, (tm, 1), 0)
        for g in range(N_POOL_GROUPS):
            win = 2 ** (g + 1)
            cols = slice(g * POOL_GROUP, (g + 1) * POOL_GROUP)
            ut = u_ref[:, cols]
            s = jnp.concatenate([halo[:, cols], ut], axis=0)
            for sh in (1, 2, 4, 8)[:g + 1]:
                s = s + pltpu.roll(s, sh, 0)
            cnt = jnp.minimum(pos + 1, win).astype(F32)
            db = (s[POOL_HALO:, :] / cnt - ut).astype(BF16)
            y = _dot(db, wp_ref[g].astype(BF16))
            d_ref[:, cols] = db
            pooled_ref[:, cols] = (y * sc_ref[:, cols]).astype(BF16)

    return pl.pallas_call(
        body, grid=(S // tm,), name="pool_fwd",
        in_specs=[_row_spec(tm, POOL_WIDTH),
                  pl.BlockSpec((POOL_HALO, POOL_WIDTH), lambda i: (jnp.maximum(i * hb - 1, 0), 0)),
                  _const_spec((N_POOL_GROUPS, POOL_GROUP, POOL_GROUP)), _const_spec((1, POOL_WIDTH))],
        out_specs=[_row_spec(tm, POOL_WIDTH), _row_spec(tm, POOL_WIDTH)],
        out_shape=[_sds((S, POOL_WIDTH), BF16), _sds((S, POOL_WIDTH), BF16)],
        compiler_params=_cp(32),
    )(u, u, w_pool, pscale)


def _mix_ln1(on, pooled, xh0, g0, b0, w_out, g1, b1, tm):
    S, D = xh0.shape

    def body(on_ref, po_ref, xh0_ref, g0_ref, b0_ref, w_ref, g1_ref, b1_ref, xh_ref, rstd_ref, xb_ref):
        mixed = _dot(on_ref[...], w_ref[:ATTN_WIDTH, :]) + _dot(po_ref[...], w_ref[ATTN_WIDTH:, :])
        x0 = xh0_ref[...] * g0_ref[...] + b0_ref[...]
        xh, rstd = _ln_fwd(ALPHA * x0 + mixed)
        xh_ref[...] = xh
        rstd_ref[...] = rstd
        xb_ref[...] = (xh * g1_ref[...] + b1_ref[...]).astype(BF16)

    return pl.pallas_call(
        body, grid=(S // tm,), name="mix_ln1",
        in_specs=[_row_spec(tm, ATTN_WIDTH), _row_spec(tm, POOL_WIDTH), _row_spec(tm, D),
                  _const_spec((1, D)), _const_spec((1, D)), _const_spec((D, D)),
                  _const_spec((1, D)), _const_spec((1, D))],
        out_specs=[_row_spec(tm, D), _row_spec(tm, 1), _row_spec(tm, D)],
        out_shape=[_sds((S, D), F32), _sds((S, 1), F32), _sds((S, D), BF16)],
        compiler_params=_cp(40),
    )(on, pooled, xh0, g0, b0, w_out, g1, b1)


def _mlp_ln2(xh1, x1b, g1, b1, w_up_s, w_down, tm):
    S, D = xh1.shape
    fc = D_FF // N_CHIPS

    def body(xh_ref, xb_ref, g_ref, b_ref, wu_ref, wd_ref, xh2_ref, rstd_ref, acc_ref):
        j = pl.program_id(1)

        @pl.when(j == 0)
        def _():
            acc_ref[...] = jnp.zeros_like(acc_ref)

        a = _dot(xb_ref[...], wu_ref[0])
        r = jnp.maximum(a, 0.0)
        acc_ref[...] += _dot((r * r).astype(BF16), wd_ref[...])

        @pl.when(j == N_CHIPS - 1)
        def _():
            x1 = xh_ref[...] * g_ref[...] + b_ref[...]
            xh, rstd = _ln_fwd(ALPHA * x1 + acc_ref[...])
            xh2_ref[...] = xh
            rstd_ref[...] = rstd

    return pl.pallas_call(
        body, grid=(S // tm, N_CHIPS), name="mlp_ln2",
        in_specs=[pl.BlockSpec((tm, D), lambda i, j: (i, 0)), pl.BlockSpec((tm, D), lambda i, j: (i, 0)),
                  pl.BlockSpec((1, D), lambda i, j: (0, 0)), pl.BlockSpec((1, D), lambda i, j: (0, 0)),
                  pl.BlockSpec((1, D, fc), lambda i, j: (j, 0, 0)),
                  pl.BlockSpec((fc, D), lambda i, j: (j, 0))],
        out_specs=[pl.BlockSpec((tm, D), lambda i, j: (i, 0)), pl.BlockSpec((tm, 1), lambda i, j: (i, 0))],
        out_shape=[_sds((S, D), F32), _sds((S, 1), F32)],
        scratch_shapes=[pltpu.VMEM((tm, D), F32)],
        compiler_params=_cp(40),
    )(xh1, x1b, g1, b1, w_up_s, w_down)


def _ple_ln3_loss(xh2, rstd2, g2, b2, p, w_ple_s, w_gate, g3, b3, target, tm):
    S, D = xh2.shape
    pc = D // N_CHIPS

    def body(xh2_ref, rstd2_ref, g2_ref, b2_ref, p_ref, wp_ref, wg_ref, g3_ref, b3_ref, t_ref,
             dpre2_ref, dhb_ref, dwp_ref, dwg_ref, dg3_ref, db3_ref, dg2_ref, db2_ref, loss_ref):
        i = pl.program_id(0)

        @pl.when(i == 0)
        def _():
            for r in (dwp_ref, dwg_ref, dg3_ref, db3_ref, dg2_ref, db2_ref, loss_ref):
                r[...] = jnp.zeros_like(r)

        xh2 = xh2_ref[...]
        x2 = xh2 * g2_ref[...] + b2_ref[...]
        x2b = x2.astype(BF16)
        gate = 1.0 / (1.0 + jnp.exp(-_dot(x2b, wg_ref[...])))
        pb = p_ref[...].astype(BF16)
        pe = jnp.concatenate([_dot(pb, wp_ref[c]) for c in range(N_CHIPS)], axis=1)
        xh3, rstd3 = _ln_fwd(ALPHA * x2 + pe * gate)
        diff = xh3 * g3_ref[...] + b3_ref[...] - t_ref[...]
        loss_ref[...] += (0.5 / D) * jnp.sum(diff * diff)
        dy = diff * (1.0 / D)
        dg3_ref[...] += _colsum(dy * xh3)
        db3_ref[...] += _colsum(dy)
        dpre3 = _ln_bwd(dy, xh3, rstd3, g3_ref[...])
        dpe_b = (dpre3 * gate).astype(BF16)
        dgp_b = (dpre3 * pe * gate * (1.0 - gate)).astype(BF16)
        dx2 = ALPHA * dpre3 + _dot_nt(dgp_b, wg_ref[...])
        dwg_ref[...] += _dot_tn(x2b, dgp_b)
        for c in range(N_CHIPS):
            dwp_ref[c] += _dot_tn(pb, dpe_b[:, c * pc:(c + 1) * pc])
        dg2_ref[...] += _colsum(dx2 * xh2)
        db2_ref[...] += _colsum(dx2)
        dpre2 = _ln_bwd(dx2, xh2, rstd2_ref[...], g2_ref[...])
        dpre2_ref[...] = dpre2
        dhb_ref[...] = dpre2.astype(BF16)

    vec = _const_spec((1, D))
    return pl.pallas_call(
        body, grid=(S // tm,), name="ple_ln3_loss",
        in_specs=[_row_spec(tm, D), _row_spec(tm, 1), vec, vec, _row_spec(tm, PLE_DIM),
                  _const_spec((N_CHIPS, PLE_DIM, pc)), _const_spec((D, D)), vec, vec, _row_spec(tm, D)],
        out_specs=[_row_spec(tm, D), _row_spec(tm, D), _const_spec((N_CHIPS, PLE_DIM, pc)),
                   _const_spec((D, D)), vec, vec, vec, vec, _const_spec((1, LANES))],
        out_shape=[_sds((S, D), F32), _sds((S, D), BF16), _sds((N_CHIPS, PLE_DIM, pc), F32),
                   _sds((D, D), F32), _sds((1, D), F32), _sds((1, D), F32), _sds((1, D), F32),
                   _sds((1, D), F32), _sds((1, LANES), F32)],
        compiler_params=_cp(48),
    )(xh2, rstd2, g2, b2, p, w_ple_s, w_gate, g3, b3, target)


def _mlp_bwd(x1b, dhb, w_up_s, w_down, tm):
    S, D = x1b.shape
    fc = D_FF // N_CHIPS

    def body(xb_ref, dh_ref, wu_ref, wd_ref, dx_ref, da_ref, h1_ref):
        j = pl.program_id(1)

        @pl.when(j == 0)
        def _():
            dx_ref[...] = jnp.zeros_like(dx_ref)

        a = _dot(xb_ref[...], wu_ref[0])
        r = jnp.maximum(a, 0.0)
        h1_ref[...] = (r * r).astype(BF16)
        da = (_dot_nt(dh_ref[...], wd_ref[...]) * (2.0 * r)).astype(BF16)
        da_ref[...] = da
        dx_ref[...] += _dot_nt(da, wu_ref[0])

    return pl.pallas_call(
        body, grid=(S // tm, N_CHIPS), name="mlp_bwd",
        in_specs=[pl.BlockSpec((tm, D), lambda i, j: (i, 0)), pl.BlockSpec((tm, D), lambda i, j: (i, 0)),
                  pl.BlockSpec((1, D, fc), lambda i, j: (j, 0, 0)),
                  pl.BlockSpec((fc, D), lambda i, j: (j, 0))],
        out_specs=[pl.BlockSpec((tm, D), lambda i, j: (i, 0)), pl.BlockSpec((tm, fc), lambda i, j: (i, j)),
                   pl.BlockSpec((tm, fc), lambda i, j: (i, j))],
        out_shape=[_sds((S, D), F32), _sds((S, D_FF), BF16), _sds((S, D_FF), BF16)],
        compiler_params=_cp(40),
    )(x1b, dhb, w_up_s, w_down)


def _tn_matmul(a, b, name, tk, tt, stacked):
    T, K = a.shape
    N = b.shape[1]
    tn = 1024

    def body(a_ref, b_ref, o_ref):
        @pl.when(pl.program_id(2) == 0)
        def _():
            o_ref[...] = jnp.zeros_like(o_ref)

        prod = _dot_tn(a_ref[...], b_ref[...])
        if stacked:
            o_ref[0] += prod
        else:
            o_ref[...] += prod

    if stacked:
        out_spec = pl.BlockSpec((1, tk, tn), lambda k, n, t: (n, k, 0))
        out_shape = _sds((N // tn, K, tn), F32)
    else:
        out_spec = pl.BlockSpec((tk, tn), lambda k, n, t: (k, n))
        out_shape = _sds((K, N), F32)
    return pl.pallas_call(
        body, grid=(K // tk, N // tn, T // tt), name=name,
        in_specs=[pl.BlockSpec((tt, tk), lambda k, n, t: (t, k)),
                  pl.BlockSpec((tt, tn), lambda k, n, t: (t, n))],
        out_specs=out_spec, out_shape=out_shape,
        compiler_params=_cp(40),
    )(a, b)


def _mix_bwd(dpre2, dx1m, xh1, rstd1, g1, w_out, on, pooled, tm):
    S, D = xh1.shape

    def body(dp2_ref, dxm_ref, xh_ref, rstd_ref, g_ref, w_ref, on_ref, po_ref,
             dpre1_ref, don_ref, dpo_ref, dw_ref, dg_ref, db_ref):
        @pl.when(pl.program_id(0) == 0)
        def _():
            for r in (dw_ref, dg_ref, db_ref):
                r[...] = jnp.zeros_like(r)

        xh = xh_ref[...]
        dx1 = ALPHA * dp2_ref[...] + dxm_ref[...]
        dg_ref[...] += _colsum(dx1 * xh)
        db_ref[...] += _colsum(dx1)
        dpre1 = _ln_bwd(dx1, xh, rstd_ref[...], g_ref[...])
        dpre1_ref[...] = dpre1
        dmb = dpre1.astype(BF16)
        dcat = _dot_nt(dmb, w_ref[...])
        don_ref[...] = dcat[:, :ATTN_WIDTH]
        dpo_ref[...] = dcat[:, ATTN_WIDTH:]
        dw_ref[:ATTN_WIDTH, :] += _dot_tn(on_ref[...], dmb)
        dw_ref[ATTN_WIDTH:, :] += _dot_tn(po_ref[...], dmb)

    vec = _const_spec((1, D))
    return pl.pallas_call(
        body, grid=(S // tm,), name="mix_bwd",
        in_specs=[_row_spec(tm, D), _row_spec(tm, D), _row_spec(tm, D), _row_spec(tm, 1), vec,
                  _const_spec((D, D)), _row_spec(tm, ATTN_WIDTH), _row_spec(tm, POOL_WIDTH)],
        out_specs=[_row_spec(tm, D), _row_spec(tm, ATTN_WIDTH), _row_spec(tm, POOL_WIDTH),
                   _const_spec((D, D)), vec, vec],
        out_shape=[_sds((S, D), F32), _sds((S, ATTN_WIDTH), F32), _sds((S, POOL_WIDTH), F32),
                   _sds((D, D), F32), _sds((1, D), F32), _sds((1, D), F32)],
        compiler_params=_cp(48),
    )(dpre2, dx1m, xh1, rstd1, g1, w_out, on, pooled)


def _pool_bwd(dpooled, d_b, w_pool, pscale, tm):
    S = dpooled.shape[0]
    hb = tm // POOL_HALO
    n_t = S // tm
    te = tm + POOL_HALO

    def body(dp_ref, dph_ref, d_ref, wp_ref, sc_ref, du_ref, dwp_ref, dsc_ref):
        i = pl.program_id(0)

        @pl.when(i == 0)
        def _():
            dwp_ref[...] = jnp.zeros_like(dwp_ref)
            dsc_ref[...] = jnp.zeros_like(dsc_ref)

        halo = jnp.where(i < n_t - 1, dph_ref[...], 0.0)
        pos = i * tm + lax.broadcasted_iota(jnp.int32, (te, 1), 0)
        for g in range(N_POOL_GROUPS):
            win = 2 ** (g + 1)
            cols = slice(g * POOL_GROUP, (g + 1) * POOL_GROUP)
            wpb = wp_ref[g].astype(BF16)
            dpt = dp_ref[:, cols]
            dpe = jnp.concatenate([dpt, halo[:, cols]], axis=0)
            dyb = (dpe * sc_ref[:, cols]).astype(BF16)
            dd = _dot_nt(dyb, wpb)
            s = dd / jnp.minimum(pos + 1, win).astype(F32)
            for sh in (1, 2, 4, 8)[:g + 1]:
                s = s + pltpu.roll(s, te - sh, 0)
            du_ref[:, cols] = s[:tm, :] - dd[:tm, :]
            db = d_ref[:, cols]
            dwp_ref[g] += _dot_tn(db, dyb[:tm, :])
            dsc_ref[:, cols] += _colsum(dpt * _dot(db, wpb))

    return pl.pallas_call(
        body, grid=(n_t,), name="pool_bwd",
        in_specs=[_row_spec(tm, POOL_WIDTH),
                  pl.BlockSpec((POOL_HALO, POOL_WIDTH),
                               lambda i: (jnp.minimum((i + 1) * hb, S // POOL_HALO - 1), 0)),
                  _row_spec(tm, POOL_WIDTH),
                  _const_spec((N_POOL_GROUPS, POOL_GROUP, POOL_GROUP)), _const_spec((1, POOL_WIDTH))],
        out_specs=[_row_spec(tm, POOL_WIDTH), _const_spec((N_POOL_GROUPS, POOL_GROUP, POOL_GROUP)),
                   _const_spec((1, POOL_WIDTH))],
        out_shape=[_sds((S, POOL_WIDTH), F32), _sds((N_POOL_GROUPS, POOL_GROUP, POOL_GROUP), F32),
                   _sds((1, POOL_WIDTH), F32)],
        compiler_params=_cp(32),
    )(dpooled, dpooled, d_b, w_pool, pscale)


def _attn_bwd(q, k, v, don, o_raw, ga, tq):
    S = q.shape[0]
    nq = S // tq

    def body(q_ref, k_ref, v_ref, don_ref, o_ref, ga_ref, dq_ref, dk_ref, dv_ref, dga_ref, g_s, b_s):
        i = pl.program_id(1)

        @pl.when(i == 0)
        def _():
            for r in (dk_ref, dv_ref, dga_ref):
                r[...] = jnp.zeros_like(r)

        lane = lax.broadcasted_iota(jnp.int32, (1, PAIR), 1)
        m0 = lane < HEAD_DIM
        low = _tri(tq, upper=False)
        upp = _tri(tq, upper=True)

        def seg_mean(a):
            s0 = jnp.sum(jnp.where(m0, a, 0.0), axis=-1, keepdims=True)
            s1 = jnp.sum(jnp.where(m0, 0.0, a), axis=-1, keepdims=True)
            return jnp.where(m0, s0, s1) * (1.0 / HEAD_DIM)

        o = o_ref[...]
        rs = lax.rsqrt(seg_mean(o * o) + RMS_EPS)
        oh = o * rs
        don = don_ref[...]
        dga_ref[...] += _colsum(don * oh)
        doh = don * ga_ref[...]
        do = rs * (doh - oh * seg_mean(doh * oh))
        dob = do.astype(BF16)
        q2 = q_ref[...]
        qhs = [jnp.where(m0, q2, jnp.zeros_like(q2)), jnp.where(m0, jnp.zeros_like(q2), q2)]
        dhs = [jnp.where(m0, dob, jnp.zeros_like(dob)), jnp.where(m0, jnp.zeros_like(dob), dob)]
        causal = _strictly_causal(tq)

        def down(kb, c_ls, valid):
            ks = pl.multiple_of(kb * tq, tq)
            kt = k_ref[pl.ds(ks, tq), :]
            vt = v_ref[pl.ds(ks, tq), :]
            new_c, dv = [], None
            for hh in range(2):
                ls, l, w = _sb_tile(qhs[hh], kt, low, c_ls[hh], valid)
                g_s[hh, kb] = _dot_nt(dhs[hh], vt) * w
                b_s[hh, kb] = jnp.exp(ls)
                part = _dot_tn(w.astype(BF16), dhs[hh])
                dv = part if dv is None else dv + part
                new_c.append(c_ls[hh] + jnp.sum(l, axis=1, keepdims=True))
            dv_ref[pl.ds(ks, tq), :] += dv
            return new_c

        zc, za = jnp.zeros((tq, 1), F32), jnp.zeros((tq, PAIR), F32)
        c_ls = down(i, [zc, zc], causal)

        def more(st):
            return jnp.logical_and(st[0] <= i, jnp.max(jnp.maximum(st[1], st[2])) > LOG_WEIGHT_FLOOR)

        def down_step(st):
            c_ls = down(i - st[0], [st[1], st[2]], None)
            return (st[0] + 1, c_ls[0], c_ls[1])

        n_tiles = lax.while_loop(more, down_step, (jnp.int32(1), c_ls[0], c_ls[1]))[0]

        def up(kb, c_gs, accs, valid):
            ks = pl.multiple_of(kb * tq, tq)
            kt = k_ref[pl.ds(ks, tq), :]
            new_c, new_a, dk = [], [], None
            for hh in range(2):
                g = g_s[hh, kb]
                beta = b_s[hh, kb]
                hi, lo = _split_bf16(g)
                pre = _dot(hi, upp) + _dot(lo, upp) + c_gs[hh]
                dz = g * (1.0 - beta) - beta * pre
                if valid is not None:
                    dz = jnp.where(valid, dz, 0.0)
                dzb = dz.astype(BF16)
                new_a.append(accs[hh] + _dot(dzb, kt))
                part = _dot_tn(dzb, qhs[hh])
                dk = part if dk is None else dk + part
                new_c.append(c_gs[hh] + jnp.sum(g, axis=1, keepdims=True))
            dk_ref[pl.ds(ks, tq), :] += dk
            return new_c, new_a

        def up_step(kb, st):
            c_gs, accs = up(kb, [st[0], st[1]], [st[2], st[3]], None)
            return (c_gs[0], c_gs[1], accs[0], accs[1])

        st = lax.fori_loop(i - n_tiles + 1, i, up_step, (zc, zc, za, za))
        _, accs = up(i, [st[0], st[1]], [st[2], st[3]], causal)
        dq_ref[...] = jnp.where(m0, accs[0], accs[1]) * Q_SCALE

    return pl.pallas_call(
        body, grid=(N_PAIRS, nq), name="attn_bwd",
        in_specs=[pl.BlockSpec((tq, PAIR), lambda p, i: (i, p)),
                  pl.BlockSpec((S, PAIR), lambda p, i: (0, p)),
                  pl.BlockSpec((S, PAIR), lambda p, i: (0, p)),
                  pl.BlockSpec((tq, PAIR), lambda p, i: (i, p)),
                  pl.BlockSpec((tq, PAIR), lambda p, i: (i, p)),
                  pl.BlockSpec((1, PAIR), lambda p, i: (0, p))],
        out_specs=[pl.BlockSpec((tq, PAIR), lambda p, i: (i, p)),
                   pl.BlockSpec((S, PAIR), lambda p, i: (0, p)),
                   pl.BlockSpec((S, PAIR), lambda p, i: (0, p)),
                   pl.BlockSpec((1, PAIR), lambda p, i: (0, p))],
        out_shape=[_sds((S, ATTN_WIDTH), F32), _sds((S, ATTN_WIDTH), F32), _sds((S, ATTN_WIDTH), F32),
                   _sds((1, ATTN_WIDTH), F32)],
        scratch_shapes=[pltpu.VMEM((2, nq, tq, tq), F32), pltpu.VMEM((2, nq, tq, tq), F32)],
        compiler_params=_cp(56),
    )(q, k, v, don, o_raw, ga)


def _inproj_bwd(dq, dk, dv, du, dpre1, xh0, rstd0, g0, b0, w_in_s, tm):
    S, D = xh0.shape

    def body(dq_ref, dk_ref, dv_ref, du_ref, dp1_ref, xh_ref, rstd_ref, g_ref, b_ref, w_ref,
             gx_ref, dw_ref, dg_ref, db_ref):
        @pl.when(pl.program_id(0) == 0)
        def _():
            for r in (dw_ref, dg_ref, db_ref):
                r[...] = jnp.zeros_like(r)

        xh = xh_ref[...]
        xb = (xh * g_ref[...] + b_ref[...]).astype(BF16)
        dx0 = ALPHA * dp1_ref[...]
        for c, r in enumerate((dq_ref, dk_ref, dv_ref, du_ref)):
            dpb = r[...].astype(BF16)
            dx0 = dx0 + _dot_nt(dpb, w_ref[c])
            dw_ref[c] += _dot_tn(xb, dpb)
        dg_ref[...] += _colsum(dx0 * xh)
        db_ref[...] += _colsum(dx0)
        gx_ref[...] = _ln_bwd(dx0, xh, rstd_ref[...], g_ref[...])

    vec = _const_spec((1, D))
    half = _row_spec(tm, 512)
    return pl.pallas_call(
        body, grid=(S // tm,), name="inproj_bwd",
        in_specs=[half, half, half, half, _row_spec(tm, D), _row_spec(tm, D), _row_spec(tm, 1), vec, vec,
                  _const_spec((N_CHIPS, D, 512))],
        out_specs=[_row_spec(tm, D), _const_spec((N_CHIPS, D, 512)), vec, vec],
        out_shape=[_sds((S, D), F32), _sds((N_CHIPS, D, 512), F32), _sds((1, D), F32), _sds((1, D), F32)],
        compiler_params=_cp(56),
    )(dq, dk, dv, du, dpre1, xh0, rstd0, g0, b0, w_in_s)


def _place():
    return lax.axis_index("x"), lax.axis_index("y"), lax.axis_index("c")


CHIP_FLIPS = ((0, 1), (1, 0), (1, 1))


def _exchange(name, ins, out_shapes, plan):
    n_in, n_out = len(ins), len(out_shapes)

    def body(*refs):
        in_refs, out_refs = refs[:n_in], refs[n_in:n_in + n_out]
        lsem, ssem, rsem = refs[n_in + n_out:]
        local, remote = plan(in_refs, out_refs)
        copies = [pltpu.make_async_copy(s, d, lsem.at[n]) for n, (s, d) in enumerate(local)]
        copies += [pltpu.make_async_remote_copy(src_ref=s, dst_ref=d, send_sem=ssem.at[n], recv_sem=rsem.at[n],
                                                device_id=dev, device_id_type=MESH)
                   for n, (s, d, dev) in enumerate(remote)]
        for cp in copies:
            cp.start()
        for cp in copies:
            cp.wait()

    n_local, n_remote = plan(None, None)
    return pl.pallas_call(
        body, name=name, in_specs=[HBM_SPEC] * n_in, out_specs=[HBM_SPEC] * n_out, out_shape=out_shapes,
        scratch_shapes=[pltpu.SemaphoreType.DMA((max(n_local, 1),)), pltpu.SemaphoreType.DMA((n_remote,)),
                        pltpu.SemaphoreType.DMA((n_remote,))],
    )(*ins)


def _all_gather_weights(shards):
    n = len(shards)

    def plan(in_refs, out_refs):
        if in_refs is None:
            return n, n * len(CHIP_FLIPS)
        x, y, c = _place()
        slot = 2 * x + y
        local = [(s, o.at[slot]) for s, o in zip(in_refs, out_refs)]
        remote = [(s, o.at[slot], (x ^ fx, y ^ fy, c))
                  for s, o in zip(in_refs, out_refs) for fx, fy in CHIP_FLIPS]
        return local, remote

    return _exchange("gather_weights", shards, [_sds((N_CHIPS,) + s.shape, s.dtype) for s in shards], plan)


def _pair_swap(grads):
    n = len(grads)

    def plan(in_refs, out_refs):
        if in_refs is None:
            return 0, n
        x, y, c = _place()
        return [], [(g.at[:, 1 - c], o, (x, y, 1 - c)) for g, o in zip(in_refs, out_refs)]

    return _exchange("reduce_pair", grads, [_sds((N_CHIPS,) + g.shape[2:], g.dtype) for g in grads], plan)


def _chip_scatter(parts):
    n = len(parts)

    def plan(in_refs, out_refs):
        if in_refs is None:
            return 0, n * len(CHIP_FLIPS)
        x, y, c = _place()
        remote = []
        for r, o in zip(in_refs, out_refs):
            for f, (fx, fy) in enumerate(CHIP_FLIPS):
                remote.append((r.at[2 * (x ^ fx) + (y ^ fy)], o.at[f], (x ^ fx, y ^ fy, c)))
        return [], remote

    return _exchange("reduce_chips", parts,
                     [_sds((len(CHIP_FLIPS),) + r.shape[1:], r.dtype) for r in parts], plan)


def _pair_gather(halves):
    n = len(halves)

    def plan(in_refs, out_refs):
        if in_refs is None:
            return n, n
        x, y, c = _place()
        local = [(h, o.at[c]) for h, o in zip(in_refs, out_refs)]
        remote = [(h, o.at[c], (x, y, 1 - c)) for h, o in zip(in_refs, out_refs)]
        return local, remote

    return _exchange("gather_pair", halves, [_sds((2,) + h.shape, h.dtype) for h in halves], plan)


def _add_pair(grad, recv, place, name):
    _, _, H, C = grad.shape
    th = min(H, 256)

    def body(pl_ref, g_ref, r_ref, o_ref):
        o_ref[...] = g_ref[:, 0] + r_ref[...]

    return pl.pallas_call(
        body, name=name,
        grid_spec=pltpu.PrefetchScalarGridSpec(
            num_scalar_prefetch=1, grid=(N_CHIPS, H // th),
            in_specs=[pl.BlockSpec((1, 1, th, C), lambda j, h, pr: (j, pr[0], h, 0)),
                      pl.BlockSpec((1, th, C), lambda j, h, pr: (j, h, 0))],
            out_specs=pl.BlockSpec((1, th, C), lambda j, h, pr: (j, h, 0))),
        out_shape=_sds((N_CHIPS, H, C), F32),
    )(place, grad, recv)


def _add_chips(part, recv, place, name):
    _, H, C = part.shape
    th = min(H, 256)

    def body(pl_ref, p_ref, r_ref, o_ref):
        o_ref[...] = p_ref[0] + r_ref[0] + r_ref[1] + r_ref[2]

    return pl.pallas_call(
        body, name=name,
        grid_spec=pltpu.PrefetchScalarGridSpec(
            num_scalar_prefetch=1, grid=(H // th,),
            in_specs=[pl.BlockSpec((1, th, C), lambda h, pr: (pr[1], h, 0)),
                      pl.BlockSpec((len(CHIP_FLIPS), th, C), lambda h, pr: (0, h, 0))],
            out_specs=pl.BlockSpec((th, C), lambda h, pr: (h, 0))),
        out_shape=_sds((H, C), F32),
    )(place, part, recv)


def _adamw_math(w, g, m, v):
    m = ADAM_B1 * m + (1.0 - ADAM_B1) * g
    v = ADAM_B2 * v + (1.0 - ADAM_B2) * (g * g)
    m_hat = m / (1.0 - ADAM_B1 ** ADAM_STEP)
    v_hat = v / (1.0 - ADAM_B2 ** ADAM_STEP)
    delta = -ADAM_LR * (m_hat / (jnp.sqrt(v_hat) + ADAM_EPS) + ADAM_WD * w)
    return delta, m, v


def _adamw(w, g, m, v, name):
    R, C = w.shape
    tr = min(R, 256)

    def body(w_ref, g_ref, m_ref, v_ref, d_ref, mo_ref, vo_ref):
        d, mo, vo = _adamw_math(w_ref[...], g_ref[...], m_ref[...], v_ref[...])
        d_ref[...] = d
        mo_ref[...] = mo
        vo_ref[...] = vo

    spec = _row_spec(tr, C)
    return pl.pallas_call(
        body, grid=(R // tr,), name=name, in_specs=[spec] * 4, out_specs=[spec] * 3,
        out_shape=[_sds((R, C), F32)] * 3,
    )(w, g, m, v)


def _small_allreduce_adamw(gpack, wpack, mpack, vpack):
    R = gpack.shape[0]
    flips = [(fx, fy, fc) for fx in (0, 1) for fy in (0, 1) for fc in (0, 1)][1:]

    def body(g_ref, w_ref, m_ref, v_ref, gs_ref, d_ref, mo_ref, vo_ref, recv, lsem, ssem, rsem):
        x, y, c = _place()
        me = 4 * x + 2 * y + c
        own = pltpu.make_async_copy(g_ref, recv.at[me], lsem)
        sends = [pltpu.make_async_remote_copy(src_ref=g_ref, dst_ref=recv.at[me], send_sem=ssem.at[n],
                                              recv_sem=rsem.at[n], device_id=(x ^ fx, y ^ fy, c ^ fc),
                                              device_id_type=MESH)
                 for n, (fx, fy, fc) in enumerate(flips)]
        own.start()
        for cp in sends:
            cp.start()
        own.wait()
        for cp in sends:
            cp.wait()
        total = recv[0]
        for dev in range(1, N_DEV):
            total = total + recv[dev]
        gs_ref[...] = total
        d, mo, vo = _adamw_math(w_ref[...], total, m_ref[...], v_ref[...])
        d_ref[...] = d
        mo_ref[...] = mo
        vo_ref[...] = vo

    return pl.pallas_call(
        body, name="small_allreduce_adamw", in_specs=[VMEM_SPEC] * 4, out_specs=[VMEM_SPEC] * 4,
        out_shape=[_sds((R, LANES), F32)] * 4,
        scratch_shapes=[pltpu.VMEM((N_DEV, R, LANES), F32), pltpu.SemaphoreType.DMA,
                        pltpu.SemaphoreType.DMA((len(flips),)), pltpu.SemaphoreType.DMA((len(flips),))],
    )(gpack, wpack, mpack, vpack)


def _rows8(a):
    a = a.reshape(-1, LANES)
    pad = (-a.shape[0]) % 8
    return jnp.pad(a, ((0, pad), (0, 0))) if pad else a


def _pack(parts):
    return jnp.concatenate([_rows8(a) for a in parts], axis=0)


def _unpack(pack, like):
    out, row = [], 0
    for a in like:
        n = a.size // LANES
        out.append(pack[row:row + n].reshape(a.shape))
        row += n + (-n) % 8
    return out


def kernel(x, p, emb_ln_g, emb_ln_b, w_in, attn_out_g, w_pool, pool_scale, w_out, ln1_g, ln1_b, w_up, w_down, ln2_g, ln2_b, w_ple, w_ple_gate, ln3_g, ln3_b, loss_target, m_emb_ln_g, m_emb_ln_b, m_w_in, m_attn_out_g, m_w_pool, m_pool_scale, m_w_out, m_ln1_g, m_ln1_b, m_w_up, m_w_down, m_ln2_g, m_ln2_b, m_w_ple, m_w_ple_gate, m_ln3_g, m_ln3_b, v_emb_ln_g, v_emb_ln_b, v_w_in, v_attn_out_g, v_w_pool, v_pool_scale, v_w_out, v_ln1_g, v_ln1_b, v_w_up, v_w_down, v_ln2_g, v_ln2_b, v_w_ple, v_w_ple_gate, v_ln3_g, v_ln3_b):
    S = x.shape[1]
    tm = min(256, S)
    tq = min(256, S)
    xs = x[0]
    ps = p[0, 0]
    tgt = loss_target[0]
    row = lambda a: a.reshape(1, -1)
    g0, b0 = row(emb_ln_g), row(emb_ln_b)
    g1, b1, g2, b2, g3, b3 = ln1_g, ln1_b, ln2_g, ln2_b, ln3_g, ln3_b
    wp = w_pool[0]

    big = [w_in[0], w_out[0], w_up[0], w_down[0], w_ple[0], w_ple_gate[0]]
    w_in_s, w_out_s, w_up_s, w_down_s, w_ple_s, w_gate_s = _all_gather_weights([w.astype(BF16) for w in big])
    w_out_f = w_out_s.reshape(D_MODEL, D_MODEL)
    w_down_f = w_down_s.reshape(D_FF, D_MODEL)
    w_gate_f = w_gate_s.reshape(D_MODEL, D_MODEL)

    xh0, rstd0, q, k, v, u = _embln_inproj(xs, g0, b0, w_in_s, tm)
    o_raw, on = _attn_fwd(q, k, v, attn_out_g, tq)
    d_b, pooled = _pool_fwd(u, wp, pool_scale, tm)
    xh1, rstd1, x1b = _mix_ln1(on, pooled, xh0, g0, b0, w_out_f, g1, b1, tm)
    xh2, rstd2 = _mlp_ln2(xh1, x1b, g1, b1, w_up_s, w_down_f, tm)

    (dpre2, dhb, dw_ple, dw_gate, dg3, db3, dg2, db2, loss_row) = _ple_ln3_loss(
        xh2, rstd2, g2, b2, ps, w_ple_s, w_gate_f, g3, b3, tgt, tm)
    dx1m, da, h1 = _mlp_bwd(x1b, dhb, w_up_s, w_down_f, tm)
    dw_up = _tn_matmul(x1b, da, "grad_w_up", 512, min(512, S), stacked=True)
    dw_down = _tn_matmul(h1, dhb, "grad_w_down", 512, min(512, S), stacked=False)
    dpre1, don, dpooled, dw_out, dg1, db1 = _mix_bwd(dpre2, dx1m, xh1, rstd1, g1, w_out_f, on, pooled, tm)
    du, dwp, dsc = _pool_bwd(dpooled, d_b, wp, pool_scale, tm)
    dq, dk, dv, dga = _attn_bwd(q, k, v, don, o_raw, attn_out_g, tq)
    grad_x, dw_in, dg0, db0 = _inproj_bwd(dq, dk, dv, du, dpre1, xh0, rstd0, g0, b0, w_in_s, tm)

    xi, yi, ci = _place()
    place = jnp.stack([ci, 2 * xi + yi]).astype(jnp.int32)
    names = ["w_in", "w_out", "w_up", "w_down", "w_ple", "w_ple_gate"]
    full = [dw_in, dw_out.reshape(N_CHIPS, D_MODEL // N_CHIPS, D_MODEL), dw_up,
            dw_down.reshape(N_CHIPS, D_FF // N_CHIPS, D_MODEL), dw_ple,
            dw_gate.reshape(N_CHIPS, D_MODEL // N_CHIPS, D_MODEL)]
    split = [g.reshape(N_CHIPS, 2, g.shape[1] // 2, g.shape[2]) for g in full]
    from_pair = _pair_swap(split)
    pair_sum = [_add_pair(g, r, place, "pair_sum_" + n) for g, r, n in zip(split, from_pair, names)]
    from_chips = _chip_scatter(pair_sum)
    halves = [_add_chips(s, r, place, "chip_sum_" + n) for s, r, n in zip(pair_sum, from_chips, names)]
    shard_grads = [g.reshape(w.shape) for g, w in zip(_pair_gather(halves), big)]

    ms = [m_w_in, m_w_out, m_w_up, m_w_down, m_w_ple, m_w_ple_gate]
    vs = [v_w_in, v_w_out, v_w_up, v_w_down, v_w_ple, v_w_ple_gate]
    big_out = {}
    for n, w, g, m, vv in zip(names, big, shard_grads, ms, vs):
        d, mo, vo = _adamw(w, g, m[0], vv[0], "adamw_" + n)
        big_out[n] = tuple(a.reshape(m.shape) for a in (g, d, mo, vo))

    small_names = ["emb_ln_g", "emb_ln_b", "attn_out_g", "w_pool", "pool_scale", "ln1_g", "ln1_b",
                   "ln2_g", "ln2_b", "ln3_g", "ln3_b"]
    small_w = [emb_ln_g, emb_ln_b, attn_out_g, w_pool, pool_scale, ln1_g, ln1_b, ln2_g, ln2_b, ln3_g, ln3_b]
    small_m = [m_emb_ln_g, m_emb_ln_b, m_attn_out_g, m_w_pool, m_pool_scale, m_ln1_g, m_ln1_b, m_ln2_g,
               m_ln2_b, m_ln3_g, m_ln3_b]
    small_v = [v_emb_ln_g, v_emb_ln_b, v_attn_out_g, v_w_pool, v_pool_scale, v_ln1_g, v_ln1_b, v_ln2_g,
               v_ln2_b, v_ln3_g, v_ln3_b]
    small_g = [dg0, db0, dga, dwp, dsc, dg1, db1, dg2, db2, dg3, db3]
    loss_like = jnp.zeros((8, LANES), F32)
    gpack = _pack([jnp.broadcast_to(loss_row, (8, LANES))] + small_g)
    gs, ds, mos, vos = _small_allreduce_adamw(gpack, _pack([loss_like] + small_w), _pack([loss_like] + small_m),
                                              _pack([jnp.ones((8, LANES), F32)] + small_v))
    like = [loss_like] + small_w
    gs_u, ds_u, mos_u, vos_u = (_unpack(a, like) for a in (gs, ds, mos, vos))
    loss = gs_u[0][0, 0]
    small_out = {n: (gs_u[i + 1], ds_u[i + 1], mos_u[i + 1], vos_u[i + 1]) for i, n in enumerate(small_names)}

    order = ["emb_ln_g", "emb_ln_b", "w_in", "attn_out_g", "w_pool", "pool_scale", "w_out", "ln1_g", "ln1_b",
             "w_up", "w_down", "ln2_g", "ln2_b", "w_ple", "w_ple_gate", "ln3_g", "ln3_b"]
    res = {**big_out, **small_out}
    outs = [loss, grad_x.reshape(x.shape)]
    for kind in range(4):
        outs += [res[n][kind] for n in order]
    return tuple(outs)
```

```python
import functools

import jax
import jax.numpy as jnp
from jax import lax
from jax.experimental import pallas as pl
from jax.experimental.pallas import tpu as pltpu

F32 = jnp.float32
BF16 = jnp.bfloat16

D_MODEL = 1024
ATTN_WIDTH = 512
POOL_WIDTH = 512
HEAD_DIM = 64
PAIR = 2 * HEAD_DIM
N_PAIRS = ATTN_WIDTH // PAIR
N_POOL_GROUPS = 4
POOL_GROUP = 128
POOL_HALO = 16
D_FF = 4096
PLE_DIM = 256
N_CHIPS = 4
N_DEV = 8
LN_EPS = 1e-5
RMS_EPS = 1e-6
ALPHA = float(2.0 ** 0.25)
Q_SCALE = 0.125
ADAM_LR = 0.001
ADAM_B1 = 0.9
ADAM_B2 = 0.999
ADAM_EPS = 1e-08
ADAM_WD = 0.01
ADAM_STEP = 10
LANES = 128
MIB = 1024 * 1024

MESH = pl.DeviceIdType.MESH
HBM_SPEC = pl.BlockSpec(memory_space=pltpu.HBM)
VMEM_SPEC = pl.BlockSpec(memory_space=pltpu.VMEM)


def _cp(vmem_mib):
    return pltpu.CompilerParams(vmem_limit_bytes=vmem_mib * MIB)


def _dot(a, b):
    return jnp.dot(a, b, preferred_element_type=F32)


def _dot_nt(a, b):
    return lax.dot_general(a, b, (((1,), (1,)), ((), ())), preferred_element_type=F32)


def _dot_tn(a, b):
    return lax.dot_general(a, b, (((0,), (0,)), ((), ())), preferred_element_type=F32)


def _ln_fwd(pre):
    mu = jnp.mean(pre, axis=-1, keepdims=True)
    xc = pre - mu
    var = jnp.mean(xc * xc, axis=-1, keepdims=True)
    rstd = lax.rsqrt(var + LN_EPS)
    return xc * rstd, rstd


def _ln_bwd(dy, xh, rstd, g):
    dxh = dy * g
    m1 = jnp.mean(dxh, axis=-1, keepdims=True)
    m2 = jnp.mean(dxh * xh, axis=-1, keepdims=True)
    return rstd * (dxh - m1 - xh * m2)


def _colsum(a):
    return jnp.sum(a, axis=0, keepdims=True)


def _neg_softplus(z):
    return -(jnp.maximum(z, 0.0) + jnp.log(1.0 + jnp.exp(-jnp.abs(z))))


def _split_bf16(a):
    hi = a.astype(BF16)
    lo = (a - hi.astype(F32)).astype(BF16)
    return hi, lo


def _row_spec(tm, n):
    return pl.BlockSpec((tm, n), lambda i: (i, 0))


def _const_spec(shape):
    nd = len(shape)
    return pl.BlockSpec(shape, lambda *_: (0,) * nd)


def _sds(shape, dtype):
    return jax.ShapeDtypeStruct(shape, dtype)


def _embln_inproj(x, g0, b0, w_in_s, tm):
    S, D = x.shape

    def body(x_ref, g_ref, b_ref, w_ref, xh_ref, rstd_ref, q_ref, k_ref, v_ref, u_ref):
        xh, rstd = _ln_fwd(x_ref[...])
        xh_ref[...] = xh
        rstd_ref[...] = rstd
        xb = (xh * g_ref[...] + b_ref[...]).astype(BF16)
        q_ref[...] = (_dot(xb, w_ref[0]) * Q_SCALE).astype(BF16)
        k_ref[...] = _dot(xb, w_ref[1]).astype(BF16)
        v_ref[...] = _dot(xb, w_ref[2]).astype(BF16)
        u_ref[...] = _dot(xb, w_ref[3])

    return pl.pallas_call(
        body, grid=(S // tm,), name="embln_inproj",
        in_specs=[_row_spec(tm, D), _const_spec((1, D)), _const_spec((1, D)),
                  _const_spec((N_CHIPS, D, 512))],
        out_specs=[_row_spec(tm, D), _row_spec(tm, 1), _row_spec(tm, 512), _row_spec(tm, 512),
                   _row_spec(tm, 512), _row_spec(tm, 512)],
        out_shape=[_sds((S, D), F32), _sds((S, 1), F32), _sds((S, 512), BF16), _sds((S, 512), BF16),
                   _sds((S, 512), BF16), _sds((S, 512), F32)],
        compiler_params=_cp(40),
    )(x, g0, b0, w_in_s)


def _tri(n, upper):
    r = lax.broadcasted_iota(jnp.int32, (n, n), 0)
    c = lax.broadcasted_iota(jnp.int32, (n, n), 1)
    keep = (r < c) if upper else (r > c)
    return jnp.where(keep, 1.0, 0.0).astype(BF16)


def _strictly_causal(n):
    return lax.broadcasted_iota(jnp.int32, (n, n), 1) < lax.broadcasted_iota(jnp.int32, (n, n), 0)


LOG_WEIGHT_FLOOR = -110.0


def _sb_tile(qh, kt, low, c_l, valid):
    z = _dot_nt(qh, kt)
    lr = _neg_softplus(z)
    l = lr if valid is None else jnp.where(valid, lr, 0.0)
    hi, lo = _split_bf16(l)
    sfx = _dot(hi, low) + _dot(lo, low) + c_l
    ls = z + lr
    w = jnp.exp(ls + sfx)
    if valid is not None:
        w = jnp.where(valid, w, 0.0)
    return ls, l, w


def _attn_fwd(q, k, v, ga, tq, rider):
    S = q.shape[0]
    nq = S // tq

    def body(*refs):
        (q_ref, k_ref, v_ref, ga_ref), (o_ref, on_ref), _, ride = rider.split(refs, 4, 2, 0)
        p, i = pl.program_id(0), pl.program_id(1)

        @pl.when(jnp.logical_and(p == 0, i == 0))
        def _():
            rider.first(ride)

        @pl.when(jnp.logical_and(p == N_PAIRS - 1, i == 0))
        def _():
            rider.mid(ride)

        lane = lax.broadcasted_iota(jnp.int32, (1, PAIR), 1)
        m0 = lane < HEAD_DIM
        low = _tri(tq, upper=False)
        q2 = q_ref[...]
        qhs = [jnp.where(m0, q2, jnp.zeros_like(q2)), jnp.where(m0, jnp.zeros_like(q2), q2)]

        def tile(kb, c_ls, accs, valid):
            ks = pl.multiple_of(kb * tq, tq)
            kt = k_ref[pl.ds(ks, tq), :]
            vt = v_ref[pl.ds(ks, tq), :]
            new_c, new_a = [], []
            for hh in range(2):
                _, l, w = _sb_tile(qhs[hh], kt, low, c_ls[hh], valid)
                new_a.append(accs[hh] + _dot(w.astype(BF16), vt))
                new_c.append(c_ls[hh] + jnp.sum(l, axis=1, keepdims=True))
            return new_c, new_a

        zc, za = jnp.zeros((tq, 1), F32), jnp.zeros((tq, PAIR), F32)
        c_ls, accs = tile(i, [zc, zc], [za, za], _strictly_causal(tq))

        def more(st):
            return jnp.logical_and(st[0] <= i, jnp.max(jnp.maximum(st[1], st[2])) > LOG_WEIGHT_FLOOR)

        def step(st):
            n, c0, c1, a0, a1 = st
            c_ls, accs = tile(i - n, [c0, c1], [a0, a1], None)
            return (n + 1, c_ls[0], c_ls[1], accs[0], accs[1])

        st = lax.while_loop(more, step, (jnp.int32(1), c_ls[0], c_ls[1], accs[0], accs[1]))
        o = jnp.where(m0, st[3], st[4])
        o_ref[...] = o
        sq = o * o
        ms0 = jnp.sum(jnp.where(m0, sq, 0.0), axis=-1, keepdims=True) * (1.0 / HEAD_DIM)
        ms1 = jnp.sum(jnp.where(m0, 0.0, sq), axis=-1, keepdims=True) * (1.0 / HEAD_DIM)
        rs = jnp.where(m0, lax.rsqrt(ms0 + RMS_EPS), lax.rsqrt(ms1 + RMS_EPS))
        on_ref[...] = (o * rs * ga_ref[...]).astype(BF16)

        @pl.when(jnp.logical_and(p == N_PAIRS - 1, i == nq - 1))
        def _():
            rider.last(ride)

    return rider.call(
        body, [q, k, v, ga], grid=(N_PAIRS, nq), name="attn_fwd",
        in_specs=[pl.BlockSpec((tq, PAIR), lambda p, i: (i, p)),
                  pl.BlockSpec((S, PAIR), lambda p, i: (0, p)),
                  pl.BlockSpec((S, PAIR), lambda p, i: (0, p)),
                  pl.BlockSpec((1, PAIR), lambda p, i: (0, p))],
        out_specs=[pl.BlockSpec((tq, PAIR), lambda p, i: (i, p)),
                   pl.BlockSpec((tq, PAIR), lambda p, i: (i, p))],
        out_shape=[_sds((S, ATTN_WIDTH), F32), _sds((S, ATTN_WIDTH), BF16)],
        scratch_shapes=[], vmem_mib=40)


def _pool_fwd(u, w_pool, pscale, tm):
    S = u.shape[0]
    hb = tm // POOL_HALO

    def body(u_ref, uh_ref, wp_ref, sc_ref, d_ref, pooled_ref):
        i = pl.program_id(0)
        halo = jnp.where(i > 0, uh_ref[...], 0.0)
        pos = i * tm + lax.broadcasted_iota(jnp.int32, (tm, 1), 0)
        for g in range(N_POOL_GROUPS):
            win = 2 ** (g + 1)
            cols = slice(g * POOL_GROUP, (g + 1) * POOL_GROUP)
            ut = u_ref[:, cols]
            s = jnp.concatenate([halo[:, cols], ut], axis=0)
            for sh in (1, 2, 4, 8)[:g + 1]:
                s = s + pltpu.roll(s, sh, 0)
            cnt = jnp.minimum(pos + 1, win).astype(F32)
            db = (s[POOL_HALO:, :] / cnt - ut).astype(BF16)
            y = _dot(db, wp_ref[g].astype(BF16))
            d_ref[:, cols] = db
            pooled_ref[:, cols] = (y * sc_ref[:, cols]).astype(BF16)

    return pl.pallas_call(
        body, grid=(S // tm,), name="pool_fwd",
        in_specs=[_row_spec(tm, POOL_WIDTH),
                  pl.BlockSpec((POOL_HALO, POOL_WIDTH), lambda i: (jnp.maximum(i * hb - 1, 0), 0)),
                  _const_spec((N_POOL_GROUPS, POOL_GROUP, POOL_GROUP)), _const_spec((1, POOL_WIDTH))],
        out_specs=[_row_spec(tm, POOL_WIDTH), _row_spec(tm, POOL_WIDTH)],
        out_shape=[_sds((S, POOL_WIDTH), BF16), _sds((S, POOL_WIDTH), BF16)],
        compiler_params=_cp(32),
    )(u, u, w_pool, pscale)


def _mix_ln1(on, pooled, xh0, g0, b0, w_out, g1, b1, tm):
    S, D = xh0.shape

    def body(on_ref, po_ref, xh0_ref, g0_ref, b0_ref, w_ref, g1_ref, b1_ref, xh_ref, rstd_ref, xb_ref):
        mixed = _dot(on_ref[...], w_ref[:ATTN_WIDTH, :]) + _dot(po_ref[...], w_ref[ATTN_WIDTH:, :])
        x0 = xh0_ref[...] * g0_ref[...] + b0_ref[...]
        xh, rstd = _ln_fwd(ALPHA * x0 + mixed)
        xh_ref[...] = xh
        rstd_ref[...] = rstd
        xb_ref[...] = (xh * g1_ref[...] + b1_ref[...]).astype(BF16)

    return pl.pallas_call(
        body, grid=(S // tm,), name="mix_ln1",
        in_specs=[_row_spec(tm, ATTN_WIDTH), _row_spec(tm, POOL_WIDTH), _row_spec(tm, D),
                  _const_spec((1, D)), _const_spec((1, D)), _const_spec((D, D)),
                  _const_spec((1, D)), _const_spec((1, D))],
        out_specs=[_row_spec(tm, D), _row_spec(tm, 1), _row_spec(tm, D)],
        out_shape=[_sds((S, D), F32), _sds((S, 1), F32), _sds((S, D), BF16)],
        compiler_params=_cp(40),
    )(on, pooled, xh0, g0, b0, w_out, g1, b1)


def _mlp_ln2(xh1, x1b, g1, b1, w_up_s, w_down, tm):
    S, D = xh1.shape
    fc = D_FF // N_CHIPS

    def body(xh_ref, xb_ref, g_ref, b_ref, wu_ref, wd_ref, xh2_ref, rstd_ref, acc_ref):
        j = pl.program_id(1)

        @pl.when(j == 0)
        def _():
            acc_ref[...] = jnp.zeros_like(acc_ref)

        a = _dot(xb_ref[...], wu_ref[0])
        r = jnp.maximum(a, 0.0)
        acc_ref[...] += _dot((r * r).astype(BF16), wd_ref[...])

        @pl.when(j == N_CHIPS - 1)
        def _():
            x1 = xh_ref[...] * g_ref[...] + b_ref[...]
            xh, rstd = _ln_fwd(ALPHA * x1 + acc_ref[...])
            xh2_ref[...] = xh
            rstd_ref[...] = rstd

    return pl.pallas_call(
        body, grid=(S // tm, N_CHIPS), name="mlp_ln2",
        in_specs=[pl.BlockSpec((tm, D), lambda i, j: (i, 0)), pl.BlockSpec((tm, D), lambda i, j: (i, 0)),
                  pl.BlockSpec((1, D), lambda i, j: (0, 0)), pl.BlockSpec((1, D), lambda i, j: (0, 0)),
                  pl.BlockSpec((1, D, fc), lambda i, j: (j, 0, 0)),
                  pl.BlockSpec((fc, D), lambda i, j: (j, 0))],
        out_specs=[pl.BlockSpec((tm, D), lambda i, j: (i, 0)), pl.BlockSpec((tm, 1), lambda i, j: (i, 0))],
        out_shape=[_sds((S, D), F32), _sds((S, 1), F32)],
        scratch_shapes=[pltpu.VMEM((tm, D), F32)],
        compiler_params=_cp(40),
    )(xh1, x1b, g1, b1, w_up_s, w_down)


def _ple_ln3_loss(xh2, rstd2, g2, b2, p, w_ple_s, w_gate, g3, b3, target, tm):
    S, D = xh2.shape
    pc = D // N_CHIPS

    def body(xh2_ref, rstd2_ref, g2_ref, b2_ref, p_ref, wp_ref, wg_ref, g3_ref, b3_ref, t_ref,
             dpre2_ref, dhb_ref, dwp_ref, dwg_ref, dg3_ref, db3_ref, dg2_ref, db2_ref, loss_ref):
        i = pl.program_id(0)

        @pl.when(i == 0)
        def _():
            for r in (dwp_ref, dwg_ref, dg3_ref, db3_ref, dg2_ref, db2_ref, loss_ref):
                r[...] = jnp.zeros_like(r)

        xh2 = xh2_ref[...]
        x2 = xh2 * g2_ref[...] + b2_ref[...]
        x2b = x2.astype(BF16)
        gate = 1.0 / (1.0 + jnp.exp(-_dot(x2b, wg_ref[...])))
        pb = p_ref[...].astype(BF16)
        pe = jnp.concatenate([_dot(pb, wp_ref[c]) for c in range(N_CHIPS)], axis=1)
        xh3, rstd3 = _ln_fwd(ALPHA * x2 + pe * gate)
        diff = xh3 * g3_ref[...] + b3_ref[...] - t_ref[...]
        loss_ref[...] += (0.5 / D) * jnp.sum(diff * diff)
        dy = diff * (1.0 / D)
        dg3_ref[...] += _colsum(dy * xh3)
        db3_ref[...] += _colsum(dy)
        dpre3 = _ln_bwd(dy, xh3, rstd3, g3_ref[...])
        dpe_b = (dpre3 * gate).astype(BF16)
        dgp_b = (dpre3 * pe * gate * (1.0 - gate)).astype(BF16)
        dx2 = ALPHA * dpre3 + _dot_nt(dgp_b, wg_ref[...])
        dwg_ref[...] += _dot_tn(x2b, dgp_b)
        for c in range(N_CHIPS):
            dwp_ref[c] += _dot_tn(pb, dpe_b[:, c * pc:(c + 1) * pc])
        dg2_ref[...] += _colsum(dx2 * xh2)
        db2_ref[...] += _colsum(dx2)
        dpre2 = _ln_bwd(dx2, xh2, rstd2_ref[...], g2_ref[...])
        dpre2_ref[...] = dpre2
        dhb_ref[...] = dpre2.astype(BF16)

    vec = _const_spec((1, D))
    return pl.pallas_call(
        body, grid=(S // tm,), name="ple_ln3_loss",
        in_specs=[_row_spec(tm, D), _row_spec(tm, 1), vec, vec, _row_spec(tm, PLE_DIM),
                  _const_spec((N_CHIPS, PLE_DIM, pc)), _const_spec((D, D)), vec, vec, _row_spec(tm, D)],
        out_specs=[_row_spec(tm, D), _row_spec(tm, D), _const_spec((N_CHIPS, PLE_DIM, pc)),
                   _const_spec((D, D)), vec, vec, vec, vec, _const_spec((1, LANES))],
        out_shape=[_sds((S, D), F32), _sds((S, D), BF16), _sds((N_CHIPS, PLE_DIM, pc), F32),
                   _sds((D, D), F32), _sds((1, D), F32), _sds((1, D), F32), _sds((1, D), F32),
                   _sds((1, D), F32), _sds((1, LANES), F32)],
        compiler_params=_cp(48),
    )(xh2, rstd2, g2, b2, p, w_ple_s, w_gate, g3, b3, target)


def _mlp_bwd(x1b, dhb, w_up_s, w_down, tm):
    S, D = x1b.shape
    fc = D_FF // N_CHIPS

    def body(xb_ref, dh_ref, wu_ref, wd_ref, dx_ref, da_ref, h1_ref):
        j = pl.program_id(1)

        @pl.when(j == 0)
        def _():
            dx_ref[...] = jnp.zeros_like(dx_ref)

        a = _dot(xb_ref[...], wu_ref[0])
        r = jnp.maximum(a, 0.0)
        h1_ref[...] = (r * r).astype(BF16)
        da = (_dot_nt(dh_ref[...], wd_ref[...]) * (2.0 * r)).astype(BF16)
        da_ref[...] = da
        dx_ref[...] += _dot_nt(da, wu_ref[0])

    return pl.pallas_call(
        body, grid=(S // tm, N_CHIPS), name="mlp_bwd",
        in_specs=[pl.BlockSpec((tm, D), lambda i, j: (i, 0)), pl.BlockSpec((tm, D), lambda i, j: (i, 0)),
                  pl.BlockSpec((1, D, fc), lambda i, j: (j, 0, 0)),
                  pl.BlockSpec((fc, D), lambda i, j: (j, 0))],
        out_specs=[pl.BlockSpec((tm, D), lambda i, j: (i, 0)), pl.BlockSpec((tm, fc), lambda i, j: (i, j)),
                   pl.BlockSpec((tm, fc), lambda i, j: (i, j))],
        out_shape=[_sds((S, D), F32), _sds((S, D_FF), BF16), _sds((S, D_FF), BF16)],
        compiler_params=_cp(40),
    )(x1b, dhb, w_up_s, w_down)


def _tn_matmul(a, b, name, tk, tt, stacked):
    T, K = a.shape
    N = b.shape[1]
    tn = 1024

    def body(a_ref, b_ref, o_ref):
        @pl.when(pl.program_id(2) == 0)
        def _():
            o_ref[...] = jnp.zeros_like(o_ref)

        prod = _dot_tn(a_ref[...], b_ref[...])
        if stacked:
            o_ref[0] += prod
        else:
            o_ref[...] += prod

    if stacked:
        out_spec = pl.BlockSpec((1, tk, tn), lambda k, n, t: (n, k, 0))
        out_shape = _sds((N // tn, K, tn), F32)
    else:
        out_spec = pl.BlockSpec((tk, tn), lambda k, n, t: (k, n))
        out_shape = _sds((K, N), F32)
    return pl.pallas_call(
        body, grid=(K // tk, N // tn, T // tt), name=name,
        in_specs=[pl.BlockSpec((tt, tk), lambda k, n, t: (t, k)),
                  pl.BlockSpec((tt, tn), lambda k, n, t: (t, n))],
        out_specs=out_spec, out_shape=out_shape,
        compiler_params=_cp(40),
    )(a, b)


def _mix_bwd(dpre2, dx1m, xh1, rstd1, g1, w_out, on, pooled, tm, rider):
    S, D = xh1.shape
    n_t = S // tm

    def body(*refs):
        ((dp2_ref, dxm_ref, xh_ref, rstd_ref, g_ref, w_ref, on_ref, po_ref),
         (dpre1_ref, don_ref, dpo_ref, dw_ref, dg_ref, db_ref), _, ride) = rider.split(refs, 8, 6, 0)

        @pl.when(pl.program_id(0) == 0)
        def _():
            rider.first(ride)
            for r in (dw_ref, dg_ref, db_ref):
                r[...] = jnp.zeros_like(r)

        xh = xh_ref[...]
        dx1 = ALPHA * dp2_ref[...] + dxm_ref[...]
        dg_ref[...] += _colsum(dx1 * xh)
        db_ref[...] += _colsum(dx1)
        dpre1 = _ln_bwd(dx1, xh, rstd_ref[...], g_ref[...])
        dpre1_ref[...] = dpre1
        dmb = dpre1.astype(BF16)
        dcat = _dot_nt(dmb, w_ref[...])
        don_ref[...] = dcat[:, :ATTN_WIDTH]
        dpo_ref[...] = dcat[:, ATTN_WIDTH:]
        dw_ref[:ATTN_WIDTH, :] += _dot_tn(on_ref[...], dmb)
        dw_ref[ATTN_WIDTH:, :] += _dot_tn(po_ref[...], dmb)

        @pl.when(pl.program_id(0) == n_t - 1)
        def _():
            rider.last(ride)

    vec = _const_spec((1, D))
    return rider.call(
        body, [dpre2, dx1m, xh1, rstd1, g1, w_out, on, pooled], grid=(n_t,), name="mix_bwd",
        in_specs=[_row_spec(tm, D), _row_spec(tm, D), _row_spec(tm, D), _row_spec(tm, 1), vec,
                  _const_spec((D, D)), _row_spec(tm, ATTN_WIDTH), _row_spec(tm, POOL_WIDTH)],
        out_specs=[_row_spec(tm, D), _row_spec(tm, ATTN_WIDTH), _row_spec(tm, POOL_WIDTH),
                   _const_spec((D, D)), vec, vec],
        out_shape=[_sds((S, D), F32), _sds((S, ATTN_WIDTH), F32), _sds((S, POOL_WIDTH), F32),
                   _sds((D, D), F32), _sds((1, D), F32), _sds((1, D), F32)],
        scratch_shapes=[], vmem_mib=48)


def _pool_bwd(dpooled, d_b, w_pool, pscale, tm):
    S = dpooled.shape[0]
    hb = tm // POOL_HALO
    n_t = S // tm
    te = tm + POOL_HALO

    def body(dp_ref, dph_ref, d_ref, wp_ref, sc_ref, du_ref, dwp_ref, dsc_ref):
        i = pl.program_id(0)

        @pl.when(i == 0)
        def _():
            dwp_ref[...] = jnp.zeros_like(dwp_ref)
            dsc_ref[...] = jnp.zeros_like(dsc_ref)

        halo = jnp.where(i < n_t - 1, dph_ref[...], 0.0)
        pos = i * tm + lax.broadcasted_iota(jnp.int32, (te, 1), 0)
        for g in range(N_POOL_GROUPS):
            win = 2 ** (g + 1)
            cols = slice(g * POOL_GROUP, (g + 1) * POOL_GROUP)
            wpb = wp_ref[g].astype(BF16)
            dpt = dp_ref[:, cols]
            dpe = jnp.concatenate([dpt, halo[:, cols]], axis=0)
            dyb = (dpe * sc_ref[:, cols]).astype(BF16)
            dd = _dot_nt(dyb, wpb)
            s = dd / jnp.minimum(pos + 1, win).astype(F32)
            for sh in (1, 2, 4, 8)[:g + 1]:
                s = s + pltpu.roll(s, te - sh, 0)
            du_ref[:, cols] = s[:tm, :] - dd[:tm, :]
            db = d_ref[:, cols]
            dwp_ref[g] += _dot_tn(db, dyb[:tm, :])
            dsc_ref[:, cols] += _colsum(dpt * _dot(db, wpb))

    return pl.pallas_call(
        body, grid=(n_t,), name="pool_bwd",
        in_specs=[_row_spec(tm, POOL_WIDTH),
                  pl.BlockSpec((POOL_HALO, POOL_WIDTH),
                               lambda i: (jnp.minimum((i + 1) * hb, S // POOL_HALO - 1), 0)),
                  _row_spec(tm, POOL_WIDTH),
                  _const_spec((N_POOL_GROUPS, POOL_GROUP, POOL_GROUP)), _const_spec((1, POOL_WIDTH))],
        out_specs=[_row_spec(tm, POOL_WIDTH), _const_spec((N_POOL_GROUPS, POOL_GROUP, POOL_GROUP)),
                   _const_spec((1, POOL_WIDTH))],
        out_shape=[_sds((S, POOL_WIDTH), F32), _sds((N_POOL_GROUPS, POOL_GROUP, POOL_GROUP), F32),
                   _sds((1, POOL_WIDTH), F32)],
        compiler_params=_cp(32),
    )(dpooled, dpooled, d_b, w_pool, pscale)


def _attn_bwd(q, k, v, don, o_raw, ga, tq, rider):
    S = q.shape[0]
    nq = S // tq

    def body(*refs):
        ((q_ref, k_ref, v_ref, don_ref, o_ref, ga_ref), (dq_ref, dk_ref, dv_ref, dga_ref),
         (g_s, b_s), ride) = rider.split(refs, 6, 4, 2)
        p, i = pl.program_id(0), pl.program_id(1)

        @pl.when(jnp.logical_and(p == 0, i == 0))
        def _():
            rider.first(ride)

        @pl.when(i == 0)
        def _():
            for r in (dk_ref, dv_ref, dga_ref):
                r[...] = jnp.zeros_like(r)

        lane = lax.broadcasted_iota(jnp.int32, (1, PAIR), 1)
        m0 = lane < HEAD_DIM
        low = _tri(tq, upper=False)
        upp = _tri(tq, upper=True)

        def seg_mean(a):
            s0 = jnp.sum(jnp.where(m0, a, 0.0), axis=-1, keepdims=True)
            s1 = jnp.sum(jnp.where(m0, 0.0, a), axis=-1, keepdims=True)
            return jnp.where(m0, s0, s1) * (1.0 / HEAD_DIM)

        o = o_ref[...]
        rs = lax.rsqrt(seg_mean(o * o) + RMS_EPS)
        oh = o * rs
        don = don_ref[...]
        dga_ref[...] += _colsum(don * oh)
        doh = don * ga_ref[...]
        do = rs * (doh - oh * seg_mean(doh * oh))
        dob = do.astype(BF16)
        q2 = q_ref[...]
        qhs = [jnp.where(m0, q2, jnp.zeros_like(q2)), jnp.where(m0, jnp.zeros_like(q2), q2)]
        dhs = [jnp.where(m0, dob, jnp.zeros_like(dob)), jnp.where(m0, jnp.zeros_like(dob), dob)]
        causal = _strictly_causal(tq)

        def down(kb, c_ls, valid):
            ks = pl.multiple_of(kb * tq, tq)
            kt = k_ref[pl.ds(ks, tq), :]
            vt = v_ref[pl.ds(ks, tq), :]
            new_c, dv = [], None
            for hh in range(2):
                ls, l, w = _sb_tile(qhs[hh], kt, low, c_ls[hh], valid)
                g_s[hh, kb] = _dot_nt(dhs[hh], vt) * w
                b_s[hh, kb] = jnp.exp(ls)
                part = _dot_tn(w.astype(BF16), dhs[hh])
                dv = part if dv is None else dv + part
                new_c.append(c_ls[hh] + jnp.sum(l, axis=1, keepdims=True))
            dv_ref[pl.ds(ks, tq), :] += dv
            return new_c

        zc, za = jnp.zeros((tq, 1), F32), jnp.zeros((tq, PAIR), F32)
        c_ls = down(i, [zc, zc], causal)

        def more(st):
            return jnp.logical_and(st[0] <= i, jnp.max(jnp.maximum(st[1], st[2])) > LOG_WEIGHT_FLOOR)

        def down_step(st):
            c_ls = down(i - st[0], [st[1], st[2]], None)
            return (st[0] + 1, c_ls[0], c_ls[1])

        n_tiles = lax.while_loop(more, down_step, (jnp.int32(1), c_ls[0], c_ls[1]))[0]

        def up(kb, c_gs, accs, valid):
            ks = pl.multiple_of(kb * tq, tq)
            kt = k_ref[pl.ds(ks, tq), :]
            new_c, new_a, dk = [], [], None
            for hh in range(2):
                g = g_s[hh, kb]
                beta = b_s[hh, kb]
                hi, lo = _split_bf16(g)
                pre = _dot(hi, upp) + _dot(lo, upp) + c_gs[hh]
                dz = g * (1.0 - beta) - beta * pre
                if valid is not None:
                    dz = jnp.where(valid, dz, 0.0)
                dzb = dz.astype(BF16)
                new_a.append(accs[hh] + _dot(dzb, kt))
                part = _dot_tn(dzb, qhs[hh])
                dk = part if dk is None else dk + part
                new_c.append(c_gs[hh] + jnp.sum(g, axis=1, keepdims=True))
            dk_ref[pl.ds(ks, tq), :] += dk
            return new_c, new_a

        def up_step(kb, st):
            c_gs, accs = up(kb, [st[0], st[1]], [st[2], st[3]], None)
            return (c_gs[0], c_gs[1], accs[0], accs[1])

        st = lax.fori_loop(i - n_tiles + 1, i, up_step, (zc, zc, za, za))
        _, accs = up(i, [st[0], st[1]], [st[2], st[3]], causal)
        dq_ref[...] = jnp.where(m0, accs[0], accs[1]) * Q_SCALE

        @pl.when(jnp.logical_and(p == N_PAIRS - 1, i == nq - 1))
        def _():
            rider.last(ride)

    return rider.call(
        body, [q, k, v, don, o_raw, ga], grid=(N_PAIRS, nq), name="attn_bwd",
        in_specs=[pl.BlockSpec((tq, PAIR), lambda p, i: (i, p)),
                  pl.BlockSpec((S, PAIR), lambda p, i: (0, p)),
                  pl.BlockSpec((S, PAIR), lambda p, i: (0, p)),
                  pl.BlockSpec((tq, PAIR), lambda p, i: (i, p)),
                  pl.BlockSpec((tq, PAIR), lambda p, i: (i, p)),
                  pl.BlockSpec((1, PAIR), lambda p, i: (0, p))],
        out_specs=[pl.BlockSpec((tq, PAIR), lambda p, i: (i, p)),
                   pl.BlockSpec((S, PAIR), lambda p, i: (0, p)),
                   pl.BlockSpec((S, PAIR), lambda p, i: (0, p)),
                   pl.BlockSpec((1, PAIR), lambda p, i: (0, p))],
        out_shape=[_sds((S, ATTN_WIDTH), F32), _sds((S, ATTN_WIDTH), F32), _sds((S, ATTN_WIDTH), F32),
                   _sds((1, ATTN_WIDTH), F32)],
        scratch_shapes=[pltpu.VMEM((2, nq, tq, tq), F32), pltpu.VMEM((2, nq, tq, tq), F32)],
        vmem_mib=56)


def _inproj_bwd(dq, dk, dv, du, dpre1, xh0, rstd0, g0, b0, w_in_s, tm):
    S, D = xh0.shape

    def body(dq_ref, dk_ref, dv_ref, du_ref, dp1_ref, xh_ref, rstd_ref, g_ref, b_ref, w_ref,
             gx_ref, dw_ref, dg_ref, db_ref):
        @pl.when(pl.program_id(0) == 0)
        def _():
            for r in (dw_ref, dg_ref, db_ref):
                r[...] = jnp.zeros_like(r)

        xh = xh_ref[...]
        xb = (xh * g_ref[...] + b_ref[...]).astype(BF16)
        dx0 = ALPHA * dp1_ref[...]
        for c, r in enumerate((dq_ref, dk_ref, dv_ref, du_ref)):
            dpb = r[...].astype(BF16)
            dx0 = dx0 + _dot_nt(dpb, w_ref[c])
            dw_ref[c] += _dot_tn(xb, dpb)
        dg_ref[...] += _colsum(dx0 * xh)
        db_ref[...] += _colsum(dx0)
        gx_ref[...] = _ln_bwd(dx0, xh, rstd_ref[...], g_ref[...])

    vec = _const_spec((1, D))
    half = _row_spec(tm, 512)
    return pl.pallas_call(
        body, grid=(S // tm,), name="inproj_bwd",
        in_specs=[half, half, half, half, _row_spec(tm, D), _row_spec(tm, D), _row_spec(tm, 1), vec, vec,
                  _const_spec((N_CHIPS, D, 512))],
        out_specs=[_row_spec(tm, D), _const_spec((N_CHIPS, D, 512)), vec, vec],
        out_shape=[_sds((S, D), F32), _sds((N_CHIPS, D, 512), F32), _sds((1, D), F32), _sds((1, D), F32)],
        compiler_params=_cp(56),
    )(dq, dk, dv, du, dpre1, xh0, rstd0, g0, b0, w_in_s)


def _place():
    return lax.axis_index("x"), lax.axis_index("y"), lax.axis_index("c")


CHIP_FLIPS = ((0, 1), (1, 0), (1, 1))


class _Rider:
    def __init__(self, ins, out_shapes, n_sem, phases, aliases=None):
        self.ins, self.out_shapes, self.n_sem, self.phases = list(ins), list(out_shapes), n_sem, phases
        self.aliases = aliases or {}

    def split(self, refs, n_in, n_out, n_scratch):
        a = n_in + len(self.ins)
        b = a + n_out
        c = b + len(self.out_shapes)
        own = (refs[:n_in], refs[a:b], refs[c:c + n_scratch])
        return own + ((refs[n_in:a], refs[b:c]) + tuple(refs[c + n_scratch:]),)

    def first(self, ride):
        for make in self.phases(*ride)[0]:
            make().start()

    def mid(self, ride):
        ph = self.phases(*ride)
        if len(ph) == 2:
            for make in ph[0]:
                make().wait_recv()
            for make in ph[1]:
                make().start()

    def last(self, ride):
        ph = self.phases(*ride)
        if len(ph) == 2:
            for make in ph[0]:
                make().wait_send()
        for make in ph[-1]:
            make().wait()

    def call(self, body, args, *, grid, name, in_specs, out_specs, out_shape, scratch_shapes, vmem_mib):
        n_in, n_out = len(in_specs), len(out_specs)
        sems = [pltpu.SemaphoreType.DMA((self.n_sem,)), pltpu.SemaphoreType.DMA((self.n_sem,))]
        return pl.pallas_call(
            body, grid=grid, name=name,
            in_specs=list(in_specs) + [HBM_SPEC] * len(self.ins),
            out_specs=list(out_specs) + [HBM_SPEC] * len(self.out_shapes),
            out_shape=list(out_shape) + self.out_shapes,
            scratch_shapes=list(scratch_shapes) + sems,
            input_output_aliases={n_in + i: n_out + o for i, o in self.aliases.items()},
            compiler_params=_cp(vmem_mib),
        )(*args, *self.ins)

    def run(self, name):
        def body(*refs):
            ride = self.split(refs, 0, 0, 0)[3]
            self.first(ride)
            self.mid(ride)
            self.last(ride)

        return self.call(body, [], grid=(), name=name, in_specs=[], out_specs=[], out_shape=[],
                         scratch_shapes=[], vmem_mib=16)


def _remote(src, dst, ssem, rsem, n, dev):
    return functools.partial(pltpu.make_async_remote_copy, src_ref=src, dst_ref=dst, send_sem=ssem.at[n],
                             recv_sem=rsem.at[n], device_id=dev, device_id_type=MESH)


def _cast_into_slot(w, place, name):
    R, C = w.shape
    tr = min(R, 512)

    def body(pl_ref, w_ref, o_ref):
        o_ref[0] = w_ref[...].astype(BF16)

    return pl.pallas_call(
        body, name=name,
        grid_spec=pltpu.PrefetchScalarGridSpec(
            num_scalar_prefetch=1, grid=(R // tr,),
            in_specs=[pl.BlockSpec((tr, C), lambda r, pr: (r, 0))],
            out_specs=pl.BlockSpec((1, tr, C), lambda r, pr: (pr[1], r, 0))),
        out_shape=_sds((N_CHIPS, R, C), BF16),
    )(place, w)


def _gather_rider(stacked):
    n, nf = len(stacked), len(CHIP_FLIPS)

    def phases(ins, outs, ssem, rsem):
        x, y, c = _place()
        slot = 2 * x + y
        ici, d2d = [], []
        for w, (i_ref, o_ref) in enumerate(zip(ins, outs)):
            hh = o_ref.shape[1] // 2
            rows = pl.ds(c * hh, hh)
            for f, (fx, fy) in enumerate(CHIP_FLIPS):
                k = w * nf + f
                theirs = 2 * (x ^ fx) + (y ^ fy)
                ici.append(_remote(i_ref.at[slot, rows], o_ref.at[slot, rows], ssem, rsem, k, (x ^ fx, y ^ fy, c)))
                d2d.append(_remote(o_ref.at[theirs, rows], o_ref.at[theirs, rows], ssem, rsem, n * nf + k,
                                   (x, y, 1 - c)))
        return [ici, d2d]

    return _Rider(stacked, [_sds(s.shape, s.dtype) for s in stacked], 2 * n * nf, phases,
                  aliases={i: i for i in range(n)})


def _pair_swap_rider(grads):
    def phases(ins, outs, ssem, rsem):
        x, y, c = _place()
        return [[_remote(g.at[:, 1 - c], o, ssem, rsem, k, (x, y, 1 - c))
                 for k, (g, o) in enumerate(zip(ins, outs))]]

    return _Rider(grads, [_sds((N_CHIPS,) + g.shape[2:], g.dtype) for g in grads], len(grads), phases)


def _chip_scatter_rider(parts):
    nf = len(CHIP_FLIPS)

    def phases(ins, outs, ssem, rsem):
        x, y, c = _place()
        return [[_remote(r.at[2 * (x ^ fx) + (y ^ fy)], o.at[f], ssem, rsem, w * nf + f, (x ^ fx, y ^ fy, c))
                 for w, (r, o) in enumerate(zip(ins, outs)) for f, (fx, fy) in enumerate(CHIP_FLIPS)]]

    return _Rider(parts, [_sds((nf,) + r.shape[1:], r.dtype) for r in parts], len(parts) * nf, phases)


def _pair_send_rider(halves):
    def phases(ins, outs, ssem, rsem):
        x, y, c = _place()
        return [[_remote(h, o, ssem, rsem, k, (x, y, 1 - c)) for k, (h, o) in enumerate(zip(ins, outs))]]

    return _Rider(halves, [_sds(h.shape, h.dtype) for h in halves], len(halves), phases)


def _add_pair(grad, recv, place, name):
    _, _, H, C = grad.shape
    th = min(H, 256)

    def body(pl_ref, g_ref, r_ref, o_ref, ob_ref):
        s = g_ref[:, 0] + r_ref[...]
        o_ref[...] = s
        ob_ref[...] = s.astype(BF16)

    spec = pl.BlockSpec((1, th, C), lambda j, h, pr: (j, h, 0))
    return pl.pallas_call(
        body, name=name,
        grid_spec=pltpu.PrefetchScalarGridSpec(
            num_scalar_prefetch=1, grid=(N_CHIPS, H // th),
            in_specs=[pl.BlockSpec((1, 1, th, C), lambda j, h, pr: (j, pr[0], h, 0)), spec],
            out_specs=[spec, spec]),
        out_shape=[_sds((N_CHIPS, H, C), F32), _sds((N_CHIPS, H, C), BF16)],
    )(place, grad, recv)


def _add_chips(part, recv, place, name):
    _, H, C = part.shape
    th = min(H, 256)

    def body(pl_ref, p_ref, r_ref, o_ref):
        o_ref[...] = p_ref[0] + r_ref[0].astype(F32) + r_ref[1].astype(F32) + r_ref[2].astype(F32)

    return pl.pallas_call(
        body, name=name,
        grid_spec=pltpu.PrefetchScalarGridSpec(
            num_scalar_prefetch=1, grid=(H // th,),
            in_specs=[pl.BlockSpec((1, th, C), lambda h, pr: (pr[1], h, 0)),
                      pl.BlockSpec((len(CHIP_FLIPS), th, C), lambda h, pr: (0, h, 0))],
            out_specs=pl.BlockSpec((th, C), lambda h, pr: (h, 0))),
        out_shape=_sds((H, C), F32),
    )(place, part, recv)


def _adamw_math(w, g, m, v):
    m = ADAM_B1 * m + (1.0 - ADAM_B1) * g
    v = ADAM_B2 * v + (1.0 - ADAM_B2) * (g * g)
    m_hat = m / (1.0 - ADAM_B1 ** ADAM_STEP)
    v_hat = v / (1.0 - ADAM_B2 ** ADAM_STEP)
    delta = -ADAM_LR * (m_hat / (jnp.sqrt(v_hat) + ADAM_EPS) + ADAM_WD * w)
    return delta, m, v


def _adamw(w, mine, theirs, m, v, place, name):
    R, C = w.shape
    th = min(R // 2, 256)
    nb = (R // 2) // th

    def body(pl_ref, w_ref, a_ref, b_ref, m_ref, v_ref, g_ref, d_ref, mo_ref, vo_ref):
        g = jnp.where(pl.program_id(0) == pl_ref[0], a_ref[...], b_ref[...])
        g_ref[...] = g
        d, mo, vo = _adamw_math(w_ref[...], g, m_ref[...], v_ref[...])
        d_ref[...] = d
        mo_ref[...] = mo
        vo_ref[...] = vo

    whole = pl.BlockSpec((th, C), lambda h, r, pr: (h * nb + r, 0))
    half = pl.BlockSpec((th, C), lambda h, r, pr: (r, 0))
    return pl.pallas_call(
        body, name=name,
        grid_spec=pltpu.PrefetchScalarGridSpec(
            num_scalar_prefetch=1, grid=(2, nb),
            in_specs=[whole, half, half, whole, whole], out_specs=[whole] * 4),
        out_shape=[_sds((R, C), F32)] * 4,
    )(place, w, mine, theirs, m, v)


def _small_allreduce_adamw(gpack, wpack, mpack, vpack):
    R = gpack.shape[0]
    flips = [(fx, fy, fc) for fx in (0, 1) for fy in (0, 1) for fc in (0, 1)][1:]

    def body(g_ref, w_ref, m_ref, v_ref, gs_ref, d_ref, mo_ref, vo_ref, recv, lsem, ssem, rsem):
        x, y, c = _place()
        me = 4 * x + 2 * y + c
        own = pltpu.make_async_copy(g_ref, recv.at[me], lsem)
        sends = [pltpu.make_async_remote_copy(src_ref=g_ref, dst_ref=recv.at[me], send_sem=ssem.at[n],
                                              recv_sem=rsem.at[n], device_id=(x ^ fx, y ^ fy, c ^ fc),
                                              device_id_type=MESH)
                 for n, (fx, fy, fc) in enumerate(flips)]
        own.start()
        for cp in sends:
            cp.start()
        own.wait()
        for cp in sends:
            cp.wait()
        total = recv[0]
        for dev in range(1, N_DEV):
            total = total + recv[dev]
        gs_ref[...] = total
        d, mo, vo = _adamw_math(w_ref[...], total, m_ref[...], v_ref[...])
        d_ref[...] = d
        mo_ref[...] = mo
        vo_ref[...] = vo

    return pl.pallas_call(
        body, name="small_allreduce_adamw", in_specs=[VMEM_SPEC] * 4, out_specs=[VMEM_SPEC] * 4,
        out_shape=[_sds((R, LANES), F32)] * 4,
        scratch_shapes=[pltpu.VMEM((N_DEV, R, LANES), F32), pltpu.SemaphoreType.DMA,
                        pltpu.SemaphoreType.DMA((len(flips),)), pltpu.SemaphoreType.DMA((len(flips),))],
    )(gpack, wpack, mpack, vpack)


def _rows8(a):
    a = a.reshape(-1, LANES)
    pad = (-a.shape[0]) % 8
    return jnp.pad(a, ((0, pad), (0, 0))) if pad else a


def _pack(parts):
    return jnp.concatenate([_rows8(a) for a in parts], axis=0)


def _unpack(pack, like):
    out, row = [], 0
    for a in like:
        n = a.size // LANES
        out.append(pack[row:row + n].reshape(a.shape))
        row += n + (-n) % 8
    return out


def kernel(x, p, emb_ln_g, emb_ln_b, w_in, attn_out_g, w_pool, pool_scale, w_out, ln1_g, ln1_b, w_up, w_down, ln2_g, ln2_b, w_ple, w_ple_gate, ln3_g, ln3_b, loss_target, m_emb_ln_g, m_emb_ln_b, m_w_in, m_attn_out_g, m_w_pool, m_pool_scale, m_w_out, m_ln1_g, m_ln1_b, m_w_up, m_w_down, m_ln2_g, m_ln2_b, m_w_ple, m_w_ple_gate, m_ln3_g, m_ln3_b, v_emb_ln_g, v_emb_ln_b, v_w_in, v_attn_out_g, v_w_pool, v_pool_scale, v_w_out, v_ln1_g, v_ln1_b, v_w_up, v_w_down, v_ln2_g, v_ln2_b, v_w_ple, v_w_ple_gate, v_ln3_g, v_ln3_b):
    S = x.shape[1]
    tm = min(256, S)
    tq = min(256, S)
    xs = x[0]
    ps = p[0, 0]
    tgt = loss_target[0]
    row = lambda a: a.reshape(1, -1)
    g0, b0 = row(emb_ln_g), row(emb_ln_b)
    g1, b1, g2, b2, g3, b3 = ln1_g, ln1_b, ln2_g, ln2_b, ln3_g, ln3_b
    wp = w_pool[0]

    xi, yi, ci = _place()
    place = jnp.stack([ci, 2 * xi + yi]).astype(jnp.int32)
    names = ["w_in", "w_out", "w_up", "w_down", "w_ple", "w_ple_gate"]

    big = [w_in[0], w_out[0], w_up[0], w_down[0], w_ple[0], w_ple_gate[0]]
    stacked = [_cast_into_slot(w, place, "cast_" + n) for w, n in zip(big, names)]
    (w_in_s,) = _gather_rider(stacked[:1]).run("gather_w_in")

    xh0, rstd0, q, k, v, u = _embln_inproj(xs, g0, b0, w_in_s, tm)
    o_raw, on, w_out_s, w_up_s, w_down_s, w_ple_s, w_gate_s = _attn_fwd(
        q, k, v, attn_out_g, tq, _gather_rider(stacked[1:]))
    w_out_f = w_out_s.reshape(D_MODEL, D_MODEL)
    w_down_f = w_down_s.reshape(D_FF, D_MODEL)
    w_gate_f = w_gate_s.reshape(D_MODEL, D_MODEL)
    d_b, pooled = _pool_fwd(u, wp, pool_scale, tm)
    xh1, rstd1, x1b = _mix_ln1(on, pooled, xh0, g0, b0, w_out_f, g1, b1, tm)
    xh2, rstd2 = _mlp_ln2(xh1, x1b, g1, b1, w_up_s, w_down_f, tm)

    (dpre2, dhb, dw_ple, dw_gate, dg3, db3, dg2, db2, loss_row) = _ple_ln3_loss(
        xh2, rstd2, g2, b2, ps, w_ple_s, w_gate_f, g3, b3, tgt, tm)
    dx1m, da, h1 = _mlp_bwd(x1b, dhb, w_up_s, w_down_f, tm)
    dw_up = _tn_matmul(x1b, da, "grad_w_up", 512, min(512, S), stacked=True)
    dw_down = _tn_matmul(h1, dhb, "grad_w_down", 512, min(512, S), stacked=False)
    def halves_of(g):
        return g.reshape(N_CHIPS, 2, g.shape[1] // 2, g.shape[2])

    early_names = names[2:]
    early = [halves_of(g) for g in (dw_up, dw_down.reshape(N_CHIPS, D_FF // N_CHIPS, D_MODEL), dw_ple,
                                    dw_gate.reshape(N_CHIPS, D_MODEL // N_CHIPS, D_MODEL))]
    dpre1, don, dpooled, dw_out, dg1, db1, *early_pair = _mix_bwd(
        dpre2, dx1m, xh1, rstd1, g1, w_out_f, on, pooled, tm, _pair_swap_rider(early))
    early_sum = [_add_pair(g, r, place, "pair_sum_" + n) for g, r, n in zip(early, early_pair, early_names)]
    du, dwp, dsc = _pool_bwd(dpooled, d_b, wp, pool_scale, tm)
    dq, dk, dv, dga, *early_chips = _attn_bwd(
        q, k, v, don, o_raw, attn_out_g, tq, _chip_scatter_rider([b for _, b in early_sum]))
    grad_x, dw_in, dg0, db0 = _inproj_bwd(dq, dk, dv, du, dpre1, xh0, rstd0, g0, b0, w_in_s, tm)

    late_names = names[:2]
    late = [halves_of(dw_in), halves_of(dw_out.reshape(N_CHIPS, D_MODEL // N_CHIPS, D_MODEL))]
    late_pair = _pair_swap_rider(late).run("reduce_pair_late")
    late_sum = [_add_pair(g, r, place, "pair_sum_" + n) for g, r, n in zip(late, late_pair, late_names)]
    late_chips = _chip_scatter_rider([b for _, b in late_sum]).run("reduce_chips_late")
    mine = [_add_chips(s, r, place, "chip_sum_" + n)
            for (s, _), r, n in zip(late_sum + early_sum, list(late_chips) + early_chips, names)]
    theirs = _pair_send_rider(mine).run("gather_pair")

    ms = [m_w_in, m_w_out, m_w_up, m_w_down, m_w_ple, m_w_ple_gate]
    vs = [v_w_in, v_w_out, v_w_up, v_w_down, v_w_ple, v_w_ple_gate]
    big_out = {}
    for n, w, a, b, m, vv in zip(names, big, mine, theirs, ms, vs):
        res4 = _adamw(w, a, b, m[0], vv[0], place, "adamw_" + n)
        big_out[n] = tuple(r.reshape(m.shape) for r in res4)

    small_names = ["emb_ln_g", "emb_ln_b", "attn_out_g", "w_pool", "pool_scale", "ln1_g", "ln1_b",
                   "ln2_g", "ln2_b", "ln3_g", "ln3_b"]
    small_w = [emb_ln_g, emb_ln_b, attn_out_g, w_pool, pool_scale, ln1_g, ln1_b, ln2_g, ln2_b, ln3_g, ln3_b]
    small_m = [m_emb_ln_g, m_emb_ln_b, m_attn_out_g, m_w_pool, m_pool_scale, m_ln1_g, m_ln1_b, m_ln2_g,
               m_ln2_b, m_ln3_g, m_ln3_b]
    small_v = [v_emb_ln_g, v_emb_ln_b, v_attn_out_g, v_w_pool, v_pool_scale, v_ln1_g, v_ln1_b, v_ln2_g,
               v_ln2_b, v_ln3_g, v_ln3_b]
    small_g = [dg0, db0, dga, dwp, dsc, dg1, db1, dg2, db2, dg3, db3]
    loss_like = jnp.zeros((8, LANES), F32)
    gpack = _pack([jnp.broadcast_to(loss_row, (8, LANES))] + small_g)
    gs, ds, mos, vos = _small_allreduce_adamw(gpack, _pack([loss_like] + small_w), _pack([loss_like] + small_m),
                                              _pack([jnp.ones((8, LANES), F32)] + small_v))
    like = [loss_like] + small_w
    gs_u, ds_u, mos_u, vos_u = (_unpack(a, like) for a in (gs, ds, mos, vos))
    loss = gs_u[0][0, 0]
    small_out = {n: (gs_u[i + 1], ds_u[i + 1], mos_u[i + 1], vos_u[i + 1]) for i, n in enumerate(small_names)}

    order = ["emb_ln_g", "emb_ln_b", "w_in", "attn_out_g", "w_pool", "pool_scale", "w_out", "ln1_g", "ln1_b",
             "w_up", "w_down", "ln2_g", "ln2_b", "w_ple", "w_ple_gate", "ln3_g", "ln3_b"]
    res = {**big_out, **small_out}
    outs = [loss, grad_x.reshape(x.shape)]
    for kind in range(4):
        outs += [res[n][kind] for n in order]
    return tuple(outs)
```

```python
import functools

import jax
import jax.numpy as jnp
from jax import lax
from jax.experimental import pallas as pl
from jax.experimental.pallas import tpu as pltpu

F32 = jnp.float32
BF16 = jnp.bfloat16

D_MODEL = 1024
ATTN_WIDTH = 512
POOL_WIDTH = 512
HEAD_DIM = 64
PAIR = 2 * HEAD_DIM
N_PAIRS = ATTN_WIDTH // PAIR
N_POOL_GROUPS = 4
POOL_GROUP = 128
POOL_HALO = 16
D_FF = 4096
PLE_DIM = 256
N_CHIPS = 4
N_DEV = 8
LN_EPS = 1e-5
RMS_EPS = 1e-6
ALPHA = float(2.0 ** 0.25)
Q_SCALE = 0.125
ADAM_LR = 0.001
ADAM_B1 = 0.9
ADAM_B2 = 0.999
ADAM_EPS = 1e-08
ADAM_WD = 0.01
ADAM_STEP = 10
LANES = 128
MIB = 1024 * 1024

MESH = pl.DeviceIdType.MESH
HBM_SPEC = pl.BlockSpec(memory_space=pltpu.HBM)
VMEM_SPEC = pl.BlockSpec(memory_space=pltpu.VMEM)


def _cp(vmem_mib):
    return pltpu.CompilerParams(vmem_limit_bytes=vmem_mib * MIB)


def _dot(a, b):
    return jnp.dot(a, b, preferred_element_type=F32)


def _dot_nt(a, b):
    return lax.dot_general(a, b, (((1,), (1,)), ((), ())), preferred_element_type=F32)


def _dot_tn(a, b):
    return lax.dot_general(a, b, (((0,), (0,)), ((), ())), preferred_element_type=F32)


def _ln_fwd(pre):
    mu = jnp.mean(pre, axis=-1, keepdims=True)
    xc = pre - mu
    var = jnp.mean(xc * xc, axis=-1, keepdims=True)
    rstd = lax.rsqrt(var + LN_EPS)
    return xc * rstd, rstd


def _ln_bwd(dy, xh, rstd, g):
    dxh = dy * g
    m1 = jnp.mean(dxh, axis=-1, keepdims=True)
    m2 = jnp.mean(dxh * xh, axis=-1, keepdims=True)
    return rstd * (dxh - m1 - xh * m2)


def _colsum(a):
    return jnp.sum(a, axis=0, keepdims=True)


def _neg_softplus(z):
    return -(jnp.maximum(z, 0.0) + jnp.log(1.0 + jnp.exp(-jnp.abs(z))))


def _split_bf16(a):
    hi = a.astype(BF16)
    lo = (a - hi.astype(F32)).astype(BF16)
    return hi, lo


def _row_spec(tm, n):
    return pl.BlockSpec((tm, n), lambda i: (i, 0))


def _const_spec(shape):
    nd = len(shape)
    return pl.BlockSpec(shape, lambda *_: (0,) * nd)


def _sds(shape, dtype):
    return jax.ShapeDtypeStruct(shape, dtype)


def _embln_inproj(x, g0, b0, w_in_s, tm):
    S, D = x.shape

    def body(x_ref, g_ref, b_ref, w_ref, xh_ref, rstd_ref, q_ref, k_ref, v_ref, u_ref):
        xh, rstd = _ln_fwd(x_ref[...])
        xh_ref[...] = xh
        rstd_ref[...] = rstd
        xb = (xh * g_ref[...] + b_ref[...]).astype(BF16)
        q_ref[...] = (_dot(xb, w_ref[0]) * Q_SCALE).astype(BF16)
        k_ref[...] = _dot(xb, w_ref[1]).astype(BF16)
        v_ref[...] = _dot(xb, w_ref[2]).astype(BF16)
        u_ref[...] = _dot(xb, w_ref[3])

    return pl.pallas_call(
        body, grid=(S // tm,), name="embln_inproj",
        in_specs=[_row_spec(tm, D), _const_spec((1, D)), _const_spec((1, D)),
                  _const_spec((N_CHIPS, D, 512))],
        out_specs=[_row_spec(tm, D), _row_spec(tm, 1), _row_spec(tm, 512), _row_spec(tm, 512),
                   _row_spec(tm, 512), _row_spec(tm, 512)],
        out_shape=[_sds((S, D), F32), _sds((S, 1), F32), _sds((S, 512), BF16), _sds((S, 512), BF16),
                   _sds((S, 512), BF16), _sds((S, 512), F32)],
        compiler_params=_cp(40),
    )(x, g0, b0, w_in_s)


def _tri(n, upper):
    r = lax.broadcasted_iota(jnp.int32, (n, n), 0)
    c = lax.broadcasted_iota(jnp.int32, (n, n), 1)
    keep = (r < c) if upper else (r > c)
    return jnp.where(keep, 1.0, 0.0).astype(BF16)


def _strictly_causal(n):
    return lax.broadcasted_iota(jnp.int32, (n, n), 1) < lax.broadcasted_iota(jnp.int32, (n, n), 0)


LOG_WEIGHT_FLOOR = -110.0


def _sb_tile(qhs, kt, low, c_ls, valid):
    valids = valid if isinstance(valid, (list, tuple)) else [valid] * len(qhs)
    zs = [_dot_nt(qh, kt) for qh in qhs]
    lrs = [_neg_softplus(z) for z in zs]
    ls_ = [lr if m is None else jnp.where(m, lr, 0.0) for lr, m in zip(lrs, valids)]
    parts = [_split_bf16(l) for l in ls_]
    sfx = [_dot(hi, low) + _dot(lo, low) + c_l for (hi, lo), c_l in zip(parts, c_ls)]
    lss = [z + lr for z, lr in zip(zs, lrs)]
    ws = [jnp.exp(ls + s) for ls, s in zip(lss, sfx)]
    ws = [w if m is None else jnp.where(m, w, 0.0) for w, m in zip(ws, valids)]
    return lss, ls_, ws


def _attn_fwd(q, k, v, ga, tq, rider):
    S = q.shape[0]
    nq = S // tq

    def body(*refs):
        (q_ref, k_ref, v_ref, ga_ref), (o_ref, on_ref), _, ride = rider.split(refs, 4, 2, 0)
        p, i = pl.program_id(0), pl.program_id(1)

        @pl.when(jnp.logical_and(p == 0, i == 0))
        def _():
            rider.first(ride)

        @pl.when(jnp.logical_and(p == N_PAIRS - 1, i == 0))
        def _():
            rider.mid(ride)

        lane = lax.broadcasted_iota(jnp.int32, (1, PAIR), 1)
        m0 = lane < HEAD_DIM
        low = _tri(tq, upper=False)
        q2 = q_ref[...]
        qhs = [jnp.where(m0, q2, jnp.zeros_like(q2)), jnp.where(m0, jnp.zeros_like(q2), q2)]

        def tile(kb, c_ls, accs, valid):
            ks = pl.multiple_of(kb * tq, tq)
            kt = k_ref[pl.ds(ks, tq), :]
            vt = v_ref[pl.ds(ks, tq), :]
            _, ls_, ws = _sb_tile(qhs, kt, low, c_ls, valid)
            new_a = [acc + _dot(w.astype(BF16), vt) for acc, w in zip(accs, ws)]
            new_c = [c_l + jnp.sum(l, axis=1, keepdims=True) for c_l, l in zip(c_ls, ls_)]
            return new_c, new_a

        zc, za = jnp.zeros((tq, 1), F32), jnp.zeros((tq, PAIR), F32)

        def first_two():
            c_ls, accs = tile(i, [zc, zc], [za, za], _strictly_causal(tq))
            c_ls, accs = tile(i - 1, c_ls, accs, None)
            return (*c_ls, *accs)

        def first_one():
            c_ls, accs = tile(i, [zc, zc], [za, za], _strictly_causal(tq))
            return (*c_ls, *accs)

        st0 = lax.cond(i >= 1, first_two, first_one)

        def more(st):
            return jnp.logical_and(st[0] <= i, jnp.max(jnp.maximum(st[1], st[2])) > LOG_WEIGHT_FLOOR)

        def step(st):
            n, c0, c1, a0, a1 = st
            c_ls, accs = tile(i - n, [c0, c1], [a0, a1], None)
            return (n + 1, c_ls[0], c_ls[1], accs[0], accs[1])

        st = lax.while_loop(more, step, (jnp.int32(2), *st0))
        o = jnp.where(m0, st[3], st[4])
        o_ref[...] = o
        sq = o * o
        ms0 = jnp.sum(jnp.where(m0, sq, 0.0), axis=-1, keepdims=True) * (1.0 / HEAD_DIM)
        ms1 = jnp.sum(jnp.where(m0, 0.0, sq), axis=-1, keepdims=True) * (1.0 / HEAD_DIM)
        rs = jnp.where(m0, lax.rsqrt(ms0 + RMS_EPS), lax.rsqrt(ms1 + RMS_EPS))
        on_ref[...] = (o * rs * ga_ref[...]).astype(BF16)

        @pl.when(jnp.logical_and(p == N_PAIRS - 1, i == nq - 1))
        def _():
            rider.last(ride)

    return rider.call(
        body, [q, k, v, ga], grid=(N_PAIRS, nq), name="attn_fwd",
        in_specs=[pl.BlockSpec((tq, PAIR), lambda p, i: (i, p)),
                  pl.BlockSpec((S, PAIR), lambda p, i: (0, p)),
                  pl.BlockSpec((S, PAIR), lambda p, i: (0, p)),
                  pl.BlockSpec((1, PAIR), lambda p, i: (0, p))],
        out_specs=[pl.BlockSpec((tq, PAIR), lambda p, i: (i, p)),
                   pl.BlockSpec((tq, PAIR), lambda p, i: (i, p))],
        out_shape=[_sds((S, ATTN_WIDTH), F32), _sds((S, ATTN_WIDTH), BF16)],
        scratch_shapes=[], vmem_mib=40)


def _pool_fwd(u, w_pool, pscale, tm):
    S = u.shape[0]
    hb = tm // POOL_HALO

    def body(u_ref, uh_ref, wp_ref, sc_ref, d_ref, pooled_ref):
        i = pl.program_id(0)
        halo = jnp.where(i > 0, uh_ref[...], 0.0)
        pos = i * tm + lax.broadcasted_iota(jnp.int32, (tm, 1), 0)
        for g in range(N_POOL_GROUPS):
            win = 2 ** (g + 1)
            cols = slice(g * POOL_GROUP, (g + 1) * POOL_GROUP)
            ut = u_ref[:, cols]
            s = jnp.concatenate([halo[:, cols], ut], axis=0)
            for sh in (1, 2, 4, 8)[:g + 1]:
                s = s + pltpu.roll(s, sh, 0)
            cnt = jnp.minimum(pos + 1, win).astype(F32)
            db = (s[POOL_HALO:, :] / cnt - ut).astype(BF16)
            y = _dot(db, wp_ref[g].astype(BF16))
            d_ref[:, cols] = db
            pooled_ref[:, cols] = (y * sc_ref[:, cols]).astype(BF16)

    return pl.pallas_call(
        body, grid=(S // tm,), name="pool_fwd",
        in_specs=[_row_spec(tm, POOL_WIDTH),
                  pl.BlockSpec((POOL_HALO, POOL_WIDTH), lambda i: (jnp.maximum(i * hb - 1, 0), 0)),
                  _const_spec((N_POOL_GROUPS, POOL_GROUP, POOL_GROUP)), _const_spec((1, POOL_WIDTH))],
        out_specs=[_row_spec(tm, POOL_WIDTH), _row_spec(tm, POOL_WIDTH)],
        out_shape=[_sds((S, POOL_WIDTH), BF16), _sds((S, POOL_WIDTH), BF16)],
        compiler_params=_cp(32),
    )(u, u, w_pool, pscale)


def _mix_ln1(on, pooled, xh0, g0, b0, w_out, g1, b1, tm):
    S, D = xh0.shape

    def body(on_ref, po_ref, xh0_ref, g0_ref, b0_ref, w_ref, g1_ref, b1_ref, xh_ref, rstd_ref, xb_ref):
        mixed = _dot(on_ref[...], w_ref[:ATTN_WIDTH, :]) + _dot(po_ref[...], w_ref[ATTN_WIDTH:, :])
        x0 = xh0_ref[...] * g0_ref[...] + b0_ref[...]
        xh, rstd = _ln_fwd(ALPHA * x0 + mixed)
        xh_ref[...] = xh
        rstd_ref[...] = rstd
        xb_ref[...] = (xh * g1_ref[...] + b1_ref[...]).astype(BF16)

    return pl.pallas_call(
        body, grid=(S // tm,), name="mix_ln1",
        in_specs=[_row_spec(tm, ATTN_WIDTH), _row_spec(tm, POOL_WIDTH), _row_spec(tm, D),
                  _const_spec((1, D)), _const_spec((1, D)), _const_spec((D, D)),
                  _const_spec((1, D)), _const_spec((1, D))],
        out_specs=[_row_spec(tm, D), _row_spec(tm, 1), _row_spec(tm, D)],
        out_shape=[_sds((S, D), F32), _sds((S, 1), F32), _sds((S, D), BF16)],
        compiler_params=_cp(40),
    )(on, pooled, xh0, g0, b0, w_out, g1, b1)


def _mlp_ln2(xh1, x1b, g1, b1, w_up_s, w_down, tm):
    S, D = xh1.shape
    fc = D_FF // N_CHIPS

    def body(xh_ref, xb_ref, g_ref, b_ref, wu_ref, wd_ref, xh2_ref, rstd_ref, acc_ref):
        j = pl.program_id(1)

        @pl.when(j == 0)
        def _():
            acc_ref[...] = jnp.zeros_like(acc_ref)

        a = _dot(xb_ref[...], wu_ref[0])
        r = jnp.maximum(a, 0.0)
        acc_ref[...] += _dot((r * r).astype(BF16), wd_ref[...])

        @pl.when(j == N_CHIPS - 1)
        def _():
            x1 = xh_ref[...] * g_ref[...] + b_ref[...]
            xh, rstd = _ln_fwd(ALPHA * x1 + acc_ref[...])
            xh2_ref[...] = xh
            rstd_ref[...] = rstd

    return pl.pallas_call(
        body, grid=(S // tm, N_CHIPS), name="mlp_ln2",
        in_specs=[pl.BlockSpec((tm, D), lambda i, j: (i, 0)), pl.BlockSpec((tm, D), lambda i, j: (i, 0)),
                  pl.BlockSpec((1, D), lambda i, j: (0, 0)), pl.BlockSpec((1, D), lambda i, j: (0, 0)),
                  pl.BlockSpec((1, D, fc), lambda i, j: (j, 0, 0)),
                  pl.BlockSpec((fc, D), lambda i, j: (j, 0))],
        out_specs=[pl.BlockSpec((tm, D), lambda i, j: (i, 0)), pl.BlockSpec((tm, 1), lambda i, j: (i, 0))],
        out_shape=[_sds((S, D), F32), _sds((S, 1), F32)],
        scratch_shapes=[pltpu.VMEM((tm, D), F32)],
        compiler_params=_cp(40),
    )(xh1, x1b, g1, b1, w_up_s, w_down)


def _ple_ln3_loss(xh2, rstd2, g2, b2, p, w_ple_s, w_gate, g3, b3, target, tm):
    S, D = xh2.shape
    pc = D // N_CHIPS

    def body(xh2_ref, rstd2_ref, g2_ref, b2_ref, p_ref, wp_ref, wg_ref, g3_ref, b3_ref, t_ref,
             dpre2_ref, dhb_ref, dwp_ref, dwg_ref, dg3_ref, db3_ref, dg2_ref, db2_ref, loss_ref):
        i = pl.program_id(0)

        @pl.when(i == 0)
        def _():
            for r in (dwp_ref, dwg_ref, dg3_ref, db3_ref, dg2_ref, db2_ref, loss_ref):
                r[...] = jnp.zeros_like(r)

        xh2 = xh2_ref[...]
        x2 = xh2 * g2_ref[...] + b2_ref[...]
        x2b = x2.astype(BF16)
        gate = 1.0 / (1.0 + jnp.exp(-_dot(x2b, wg_ref[...])))
        pb = p_ref[...].astype(BF16)
        pe = jnp.concatenate([_dot(pb, wp_ref[c]) for c in range(N_CHIPS)], axis=1)
        xh3, rstd3 = _ln_fwd(ALPHA * x2 + pe * gate)
        diff = xh3 * g3_ref[...] + b3_ref[...] - t_ref[...]
        loss_ref[...] += (0.5 / D) * jnp.sum(diff * diff)
        dy = diff * (1.0 / D)
        dg3_ref[...] += _colsum(dy * xh3)
        db3_ref[...] += _colsum(dy)
        dpre3 = _ln_bwd(dy, xh3, rstd3, g3_ref[...])
        dpe_b = (dpre3 * gate).astype(BF16)
        dgp_b = (dpre3 * pe * gate * (1.0 - gate)).astype(BF16)
        dx2 = ALPHA * dpre3 + _dot_nt(dgp_b, wg_ref[...])
        dwg_ref[...] += _dot_tn(x2b, dgp_b)
        for c in range(N_CHIPS):
            dwp_ref[c] += _dot_tn(pb, dpe_b[:, c * pc:(c + 1) * pc])
        dg2_ref[...] += _colsum(dx2 * xh2)
        db2_ref[...] += _colsum(dx2)
        dpre2 = _ln_bwd(dx2, xh2, rstd2_ref[...], g2_ref[...])
        dpre2_ref[...] = dpre2
        dhb_ref[...] = dpre2.astype(BF16)

    vec = _const_spec((1, D))
    return pl.pallas_call(
        body, grid=(S // tm,), name="ple_ln3_loss",
        in_specs=[_row_spec(tm, D), _row_spec(tm, 1), vec, vec, _row_spec(tm, PLE_DIM),
                  _const_spec((N_CHIPS, PLE_DIM, pc)), _const_spec((D, D)), vec, vec, _row_spec(tm, D)],
        out_specs=[_row_spec(tm, D), _row_spec(tm, D), _const_spec((N_CHIPS, PLE_DIM, pc)),
                   _const_spec((D, D)), vec, vec, vec, vec, _const_spec((1, LANES))],
        out_shape=[_sds((S, D), F32), _sds((S, D), BF16), _sds((N_CHIPS, PLE_DIM, pc), F32),
                   _sds((D, D), F32), _sds((1, D), F32), _sds((1, D), F32), _sds((1, D), F32),
                   _sds((1, D), F32), _sds((1, LANES), F32)],
        compiler_params=_cp(48),
    )(xh2, rstd2, g2, b2, p, w_ple_s, w_gate, g3, b3, target)


def _mlp_bwd(x1b, dhb, w_up_s, w_down, tm):
    S, D = x1b.shape
    fc = D_FF // N_CHIPS

    def body(xb_ref, dh_ref, wu_ref, wd_ref, dx_ref, da_ref, h1_ref):
        j = pl.program_id(1)

        @pl.when(j == 0)
        def _():
            dx_ref[...] = jnp.zeros_like(dx_ref)

        a = _dot(xb_ref[...], wu_ref[0])
        r = jnp.maximum(a, 0.0)
        h1_ref[...] = (r * r).astype(BF16)
        da = (_dot_nt(dh_ref[...], wd_ref[...]) * (2.0 * r)).astype(BF16)
        da_ref[...] = da
        dx_ref[...] += _dot_nt(da, wu_ref[0])

    return pl.pallas_call(
        body, grid=(S // tm, N_CHIPS), name="mlp_bwd",
        in_specs=[pl.BlockSpec((tm, D), lambda i, j: (i, 0)), pl.BlockSpec((tm, D), lambda i, j: (i, 0)),
                  pl.BlockSpec((1, D, fc), lambda i, j: (j, 0, 0)),
                  pl.BlockSpec((fc, D), lambda i, j: (j, 0))],
        out_specs=[pl.BlockSpec((tm, D), lambda i, j: (i, 0)), pl.BlockSpec((tm, fc), lambda i, j: (i, j)),
                   pl.BlockSpec((tm, fc), lambda i, j: (i, j))],
        out_shape=[_sds((S, D), F32), _sds((S, D_FF), BF16), _sds((S, D_FF), BF16)],
        compiler_params=_cp(40),
    )(x1b, dhb, w_up_s, w_down)


def _tn_matmul(a, b, name, tk, tt, stacked):
    T, K = a.shape
    N = b.shape[1]
    tn = 1024

    def body(a_ref, b_ref, o_ref):
        @pl.when(pl.program_id(2) == 0)
        def _():
            o_ref[...] = jnp.zeros_like(o_ref)

        prod = _dot_tn(a_ref[...], b_ref[...])
        if stacked:
            o_ref[0] += prod
        else:
            o_ref[...] += prod

    if stacked:
        out_spec = pl.BlockSpec((1, tk, tn), lambda k, n, t: (n, k, 0))
        out_shape = _sds((N // tn, K, tn), F32)
    else:
        out_spec = pl.BlockSpec((tk, tn), lambda k, n, t: (k, n))
        out_shape = _sds((K, N), F32)
    return pl.pallas_call(
        body, grid=(K // tk, N // tn, T // tt), name=name,
        in_specs=[pl.BlockSpec((tt, tk), lambda k, n, t: (t, k)),
                  pl.BlockSpec((tt, tn), lambda k, n, t: (t, n))],
        out_specs=out_spec, out_shape=out_shape,
        compiler_params=_cp(40),
    )(a, b)


def _mix_bwd(dpre2, dx1m, xh1, rstd1, g1, w_out, on, pooled, tm, rider):
    S, D = xh1.shape
    n_t = S // tm

    def body(*refs):
        ((dp2_ref, dxm_ref, xh_ref, rstd_ref, g_ref, w_ref, on_ref, po_ref),
         (dpre1_ref, don_ref, dpo_ref, dw_ref, dg_ref, db_ref), _, ride) = rider.split(refs, 8, 6, 0)

        @pl.when(pl.program_id(0) == 0)
        def _():
            rider.first(ride)
            for r in (dw_ref, dg_ref, db_ref):
                r[...] = jnp.zeros_like(r)

        xh = xh_ref[...]
        dx1 = ALPHA * dp2_ref[...] + dxm_ref[...]
        dg_ref[...] += _colsum(dx1 * xh)
        db_ref[...] += _colsum(dx1)
        dpre1 = _ln_bwd(dx1, xh, rstd_ref[...], g_ref[...])
        dpre1_ref[...] = dpre1
        dmb = dpre1.astype(BF16)
        dcat = _dot_nt(dmb, w_ref[...])
        don_ref[...] = dcat[:, :ATTN_WIDTH]
        dpo_ref[...] = dcat[:, ATTN_WIDTH:]
        dw_ref[:ATTN_WIDTH, :] += _dot_tn(on_ref[...], dmb)
        dw_ref[ATTN_WIDTH:, :] += _dot_tn(po_ref[...], dmb)

        @pl.when(pl.program_id(0) == n_t - 1)
        def _():
            rider.last(ride)

    vec = _const_spec((1, D))
    return rider.call(
        body, [dpre2, dx1m, xh1, rstd1, g1, w_out, on, pooled], grid=(n_t,), name="mix_bwd",
        in_specs=[_row_spec(tm, D), _row_spec(tm, D), _row_spec(tm, D), _row_spec(tm, 1), vec,
                  _const_spec((D, D)), _row_spec(tm, ATTN_WIDTH), _row_spec(tm, POOL_WIDTH)],
        out_specs=[_row_spec(tm, D), _row_spec(tm, ATTN_WIDTH), _row_spec(tm, POOL_WIDTH),
                   _const_spec((D, D)), vec, vec],
        out_shape=[_sds((S, D), F32), _sds((S, ATTN_WIDTH), F32), _sds((S, POOL_WIDTH), F32),
                   _sds((D, D), F32), _sds((1, D), F32), _sds((1, D), F32)],
        scratch_shapes=[], vmem_mib=48)


def _pool_bwd(dpooled, d_b, w_pool, pscale, tm):
    S = dpooled.shape[0]
    hb = tm // POOL_HALO
    n_t = S // tm
    te = tm + POOL_HALO

    def body(dp_ref, dph_ref, d_ref, wp_ref, sc_ref, du_ref, dwp_ref, dsc_ref):
        i = pl.program_id(0)

        @pl.when(i == 0)
        def _():
            dwp_ref[...] = jnp.zeros_like(dwp_ref)
            dsc_ref[...] = jnp.zeros_like(dsc_ref)

        halo = jnp.where(i < n_t - 1, dph_ref[...], 0.0)
        pos = i * tm + lax.broadcasted_iota(jnp.int32, (te, 1), 0)
        for g in range(N_POOL_GROUPS):
            win = 2 ** (g + 1)
            cols = slice(g * POOL_GROUP, (g + 1) * POOL_GROUP)
            wpb = wp_ref[g].astype(BF16)
            dpt = dp_ref[:, cols]
            dpe = jnp.concatenate([dpt, halo[:, cols]], axis=0)
            dyb = (dpe * sc_ref[:, cols]).astype(BF16)
            dd = _dot_nt(dyb, wpb)
            s = dd / jnp.minimum(pos + 1, win).astype(F32)
            for sh in (1, 2, 4, 8)[:g + 1]:
                s = s + pltpu.roll(s, te - sh, 0)
            du_ref[:, cols] = s[:tm, :] - dd[:tm, :]
            db = d_ref[:, cols]
            dwp_ref[g] += _dot_tn(db, dyb[:tm, :])
            dsc_ref[:, cols] += _colsum(dpt * _dot(db, wpb))

    return pl.pallas_call(
        body, grid=(n_t,), name="pool_bwd",
        in_specs=[_row_spec(tm, POOL_WIDTH),
                  pl.BlockSpec((POOL_HALO, POOL_WIDTH),
                               lambda i: (jnp.minimum((i + 1) * hb, S // POOL_HALO - 1), 0)),
                  _row_spec(tm, POOL_WIDTH),
                  _const_spec((N_POOL_GROUPS, POOL_GROUP, POOL_GROUP)), _const_spec((1, POOL_WIDTH))],
        out_specs=[_row_spec(tm, POOL_WIDTH), _const_spec((N_POOL_GROUPS, POOL_GROUP, POOL_GROUP)),
                   _const_spec((1, POOL_WIDTH))],
        out_shape=[_sds((S, POOL_WIDTH), F32), _sds((N_POOL_GROUPS, POOL_GROUP, POOL_GROUP), F32),
                   _sds((1, POOL_WIDTH), F32)],
        compiler_params=_cp(32),
    )(dpooled, dpooled, d_b, w_pool, pscale)


def _attn_bwd(q, k, v, don, o_raw, ga, tq, rider):
    S = q.shape[0]
    nq = S // tq

    def body(*refs):
        ((q_ref, k_ref, v_ref, don_ref, o_ref, ga_ref), (dq_ref, dk_ref, dv_ref, dga_ref),
         (g_s, b_s), ride) = rider.split(refs, 6, 4, 2)
        p, i = pl.program_id(0), pl.program_id(1)

        @pl.when(jnp.logical_and(p == 0, i == 0))
        def _():
            rider.first(ride)

        @pl.when(i == 0)
        def _():
            for r in (dk_ref, dv_ref, dga_ref):
                r[...] = jnp.zeros_like(r)

        lane = lax.broadcasted_iota(jnp.int32, (1, PAIR), 1)
        m0 = lane < HEAD_DIM
        low = _tri(tq, upper=False)
        upp = _tri(tq, upper=True)

        def seg_mean(a):
            s0 = jnp.sum(jnp.where(m0, a, 0.0), axis=-1, keepdims=True)
            s1 = jnp.sum(jnp.where(m0, 0.0, a), axis=-1, keepdims=True)
            return jnp.where(m0, s0, s1) * (1.0 / HEAD_DIM)

        o = o_ref[...]
        rs = lax.rsqrt(seg_mean(o * o) + RMS_EPS)
        oh = o * rs
        don = don_ref[...]
        dga_ref[...] += _colsum(don * oh)
        doh = don * ga_ref[...]
        do = rs * (doh - oh * seg_mean(doh * oh))
        dob = do.astype(BF16)
        q2 = q_ref[...]
        qhs = [jnp.where(m0, q2, jnp.zeros_like(q2)), jnp.where(m0, jnp.zeros_like(q2), q2)]
        dhs = [jnp.where(m0, dob, jnp.zeros_like(dob)), jnp.where(m0, jnp.zeros_like(dob), dob)]
        causal = _strictly_causal(tq)

        def down(kb, c_ls, valid):
            ks = pl.multiple_of(kb * tq, tq)
            kt = k_ref[pl.ds(ks, tq), :]
            vt = v_ref[pl.ds(ks, tq), :]
            lss, ls_, ws = _sb_tile(qhs, kt, low, c_ls, valid)
            dws = [_dot_nt(dh, vt) for dh in dhs]
            for hh in range(2):
                g_s[hh, kb] = dws[hh] * ws[hh]
                b_s[hh, kb] = jnp.exp(lss[hh])
            dv_ref[pl.ds(ks, tq), :] += (_dot_tn(ws[0].astype(BF16), dhs[0])
                                         + _dot_tn(ws[1].astype(BF16), dhs[1]))
            return [c_l + jnp.sum(l, axis=1, keepdims=True) for c_l, l in zip(c_ls, ls_)]

        zc, za = jnp.zeros((tq, 1), F32), jnp.zeros((tq, PAIR), F32)
        c_ls = lax.cond(i >= 1, lambda: tuple(down(i - 1, down(i, [zc, zc], causal), None)),
                        lambda: tuple(down(i, [zc, zc], causal)))

        def more(st):
            return jnp.logical_and(st[0] <= i, jnp.max(jnp.maximum(st[1], st[2])) > LOG_WEIGHT_FLOOR)

        def down_step(st):
            c_ls = down(i - st[0], [st[1], st[2]], None)
            return (st[0] + 1, c_ls[0], c_ls[1])

        n_tiles = lax.while_loop(more, down_step, (jnp.int32(2), c_ls[0], c_ls[1]))[0]

        def up(kb, c_gs, accs, valid):
            ks = pl.multiple_of(kb * tq, tq)
            kt = k_ref[pl.ds(ks, tq), :]
            gs = [g_s[hh, kb] for hh in range(2)]
            parts = [_split_bf16(g) for g in gs]
            pres = [_dot(hi, upp) + _dot(lo, upp) + c_g for (hi, lo), c_g in zip(parts, c_gs)]
            dzs = []
            for hh in range(2):
                beta = b_s[hh, kb]
                dz = gs[hh] * (1.0 - beta) - beta * pres[hh]
                if valid is not None:
                    dz = jnp.where(valid, dz, 0.0)
                dzs.append(dz.astype(BF16))
            new_a = [acc + _dot(dzb, kt) for acc, dzb in zip(accs, dzs)]
            dk_ref[pl.ds(ks, tq), :] += _dot_tn(dzs[0], qhs[0]) + _dot_tn(dzs[1], qhs[1])
            new_c = [c_g + jnp.sum(g, axis=1, keepdims=True) for c_g, g in zip(c_gs, gs)]
            return new_c, new_a

        def up_step(kb, st):
            c_gs, accs = up(kb, [st[0], st[1]], [st[2], st[3]], None)
            return (c_gs[0], c_gs[1], accs[0], accs[1])

        st = lax.fori_loop(i - n_tiles + 1, i - 1, up_step, (zc, zc, za, za))

        def last_two():
            c_gs, accs = up(i - 1, [st[0], st[1]], [st[2], st[3]], None)
            return tuple(up(i, c_gs, accs, causal)[1])

        accs = lax.cond(i >= 1, last_two, lambda: tuple(up(i, [zc, zc], [za, za], causal)[1]))
        dq_ref[...] = jnp.where(m0, accs[0], accs[1]) * Q_SCALE

        @pl.when(jnp.logical_and(p == N_PAIRS - 1, i == nq - 1))
        def _():
            rider.last(ride)

    return rider.call(
        body, [q, k, v, don, o_raw, ga], grid=(N_PAIRS, nq), name="attn_bwd",
        in_specs=[pl.BlockSpec((tq, PAIR), lambda p, i: (i, p)),
                  pl.BlockSpec((S, PAIR), lambda p, i: (0, p)),
                  pl.BlockSpec((S, PAIR), lambda p, i: (0, p)),
                  pl.BlockSpec((tq, PAIR), lambda p, i: (i, p)),
                  pl.BlockSpec((tq, PAIR), lambda p, i: (i, p)),
                  pl.BlockSpec((1, PAIR), lambda p, i: (0, p))],
        out_specs=[pl.BlockSpec((tq, PAIR), lambda p, i: (i, p)),
                   pl.BlockSpec((S, PAIR), lambda p, i: (0, p)),
                   pl.BlockSpec((S, PAIR), lambda p, i: (0, p)),
                   pl.BlockSpec((1, PAIR), lambda p, i: (0, p))],
        out_shape=[_sds((S, ATTN_WIDTH), F32), _sds((S, ATTN_WIDTH), F32), _sds((S, ATTN_WIDTH), F32),
                   _sds((1, ATTN_WIDTH), F32)],
        scratch_shapes=[pltpu.VMEM((2, nq, tq, tq), F32), pltpu.VMEM((2, nq, tq, tq), F32)],
        vmem_mib=56)


def _inproj_bwd(dq, dk, dv, du, dpre1, xh0, rstd0, g0, b0, w_in_s, tm):
    S, D = xh0.shape

    def body(dq_ref, dk_ref, dv_ref, du_ref, dp1_ref, xh_ref, rstd_ref, g_ref, b_ref, w_ref,
             gx_ref, dw_ref, dg_ref, db_ref):
        @pl.when(pl.program_id(0) == 0)
        def _():
            for r in (dw_ref, dg_ref, db_ref):
                r[...] = jnp.zeros_like(r)

        xh = xh_ref[...]
        xb = (xh * g_ref[...] + b_ref[...]).astype(BF16)
        dx0 = ALPHA * dp1_ref[...]
        for c, r in enumerate((dq_ref, dk_ref, dv_ref, du_ref)):
            dpb = r[...].astype(BF16)
            dx0 = dx0 + _dot_nt(dpb, w_ref[c])
            dw_ref[c] += _dot_tn(xb, dpb)
        dg_ref[...] += _colsum(dx0 * xh)
        db_ref[...] += _colsum(dx0)
        gx_ref[...] = _ln_bwd(dx0, xh, rstd_ref[...], g_ref[...])

    vec = _const_spec((1, D))
    half = _row_spec(tm, 512)
    return pl.pallas_call(
        body, grid=(S // tm,), name="inproj_bwd",
        in_specs=[half, half, half, half, _row_spec(tm, D), _row_spec(tm, D), _row_spec(tm, 1), vec, vec,
                  _const_spec((N_CHIPS, D, 512))],
        out_specs=[_row_spec(tm, D), _const_spec((N_CHIPS, D, 512)), vec, vec],
        out_shape=[_sds((S, D), F32), _sds((N_CHIPS, D, 512), F32), _sds((1, D), F32), _sds((1, D), F32)],
        compiler_params=_cp(56),
    )(dq, dk, dv, du, dpre1, xh0, rstd0, g0, b0, w_in_s)


def _place():
    return lax.axis_index("x"), lax.axis_index("y"), lax.axis_index("c")


CHIP_FLIPS = ((0, 1), (1, 0), (1, 1))


class _Rider:
    def __init__(self, ins, out_shapes, n_sem, phases, aliases=None):
        self.ins, self.out_shapes, self.n_sem, self.phases = list(ins), list(out_shapes), n_sem, phases
        self.aliases = aliases or {}

    def split(self, refs, n_in, n_out, n_scratch):
        a = n_in + len(self.ins)
        b = a + n_out
        c = b + len(self.out_shapes)
        own = (refs[:n_in], refs[a:b], refs[c:c + n_scratch])
        return own + ((refs[n_in:a], refs[b:c]) + tuple(refs[c + n_scratch:]),)

    def first(self, ride):
        for make in self.phases(*ride)[0]:
            make().start()

    def mid(self, ride):
        ph = self.phases(*ride)
        if len(ph) == 2:
            for make in ph[0]:
                make().wait_recv()
            for make in ph[1]:
                make().start()

    def last(self, ride):
        ph = self.phases(*ride)
        if len(ph) == 2:
            for make in ph[0]:
                make().wait_send()
        for make in ph[-1]:
            make().wait()

    def call(self, body, args, *, grid, name, in_specs, out_specs, out_shape, scratch_shapes, vmem_mib):
        n_in, n_out = len(in_specs), len(out_specs)
        sems = [pltpu.SemaphoreType.DMA((self.n_sem,)), pltpu.SemaphoreType.DMA((self.n_sem,))]
        return pl.pallas_call(
            body, grid=grid, name=name,
            in_specs=list(in_specs) + [HBM_SPEC] * len(self.ins),
            out_specs=list(out_specs) + [HBM_SPEC] * len(self.out_shapes),
            out_shape=list(out_shape) + self.out_shapes,
            scratch_shapes=list(scratch_shapes) + sems,
            input_output_aliases={n_in + i: n_out + o for i, o in self.aliases.items()},
            compiler_params=_cp(vmem_mib),
        )(*args, *self.ins)

    def run(self, name):
        def body(*refs):
            ride = self.split(refs, 0, 0, 0)[3]
            self.first(ride)
            self.mid(ride)
            self.last(ride)

        return self.call(body, [], grid=(), name=name, in_specs=[], out_specs=[], out_shape=[],
                         scratch_shapes=[], vmem_mib=16)


def _remote(src, dst, ssem, rsem, n, dev):
    return functools.partial(pltpu.make_async_remote_copy, src_ref=src, dst_ref=dst, send_sem=ssem.at[n],
                             recv_sem=rsem.at[n], device_id=dev, device_id_type=MESH)


def _cast_into_slot(w, place, name):
    R, C = w.shape
    tr = min(R, 512)

    def body(pl_ref, w_ref, o_ref):
        o_ref[0] = w_ref[...].astype(BF16)

    return pl.pallas_call(
        body, name=name,
        grid_spec=pltpu.PrefetchScalarGridSpec(
            num_scalar_prefetch=1, grid=(R // tr,),
            in_specs=[pl.BlockSpec((tr, C), lambda r, pr: (r, 0))],
            out_specs=pl.BlockSpec((1, tr, C), lambda r, pr: (pr[1], r, 0))),
        out_shape=_sds((N_CHIPS, R, C), BF16),
    )(place, w)


def _gather_rider(stacked):
    n, nf = len(stacked), len(CHIP_FLIPS)

    def phases(ins, outs, ssem, rsem):
        x, y, c = _place()
        slot = 2 * x + y
        ici, d2d = [], []
        for w, (i_ref, o_ref) in enumerate(zip(ins, outs)):
            hh = o_ref.shape[1] // 2
            rows = pl.ds(c * hh, hh)
            for f, (fx, fy) in enumerate(CHIP_FLIPS):
                k = w * nf + f
                theirs = 2 * (x ^ fx) + (y ^ fy)
                ici.append(_remote(i_ref.at[slot, rows], o_ref.at[slot, rows], ssem, rsem, k, (x ^ fx, y ^ fy, c)))
                d2d.append(_remote(o_ref.at[theirs, rows], o_ref.at[theirs, rows], ssem, rsem, n * nf + k,
                                   (x, y, 1 - c)))
        return [ici, d2d]

    return _Rider(stacked, [_sds(s.shape, s.dtype) for s in stacked], 2 * n * nf, phases,
                  aliases={i: i for i in range(n)})


def _pair_swap_rider(grads):
    def phases(ins, outs, ssem, rsem):
        x, y, c = _place()
        return [[_remote(g.at[:, 1 - c], o, ssem, rsem, k, (x, y, 1 - c))
                 for k, (g, o) in enumerate(zip(ins, outs))]]

    return _Rider(grads, [_sds((N_CHIPS,) + g.shape[2:], g.dtype) for g in grads], len(grads), phases)


def _chip_scatter_rider(parts):
    nf = len(CHIP_FLIPS)

    def phases(ins, outs, ssem, rsem):
        x, y, c = _place()
        return [[_remote(r.at[2 * (x ^ fx) + (y ^ fy)], o.at[f], ssem, rsem, w * nf + f, (x ^ fx, y ^ fy, c))
                 for w, (r, o) in enumerate(zip(ins, outs)) for f, (fx, fy) in enumerate(CHIP_FLIPS)]]

    return _Rider(parts, [_sds((nf,) + r.shape[1:], r.dtype) for r in parts], len(parts) * nf, phases)


def _pair_send_rider(halves):
    def phases(ins, outs, ssem, rsem):
        x, y, c = _place()
        return [[_remote(h, o, ssem, rsem, k, (x, y, 1 - c)) for k, (h, o) in enumerate(zip(ins, outs))]]

    return _Rider(halves, [_sds(h.shape, h.dtype) for h in halves], len(halves), phases)


def _add_pair(grad, recv, place, name):
    _, _, H, C = grad.shape
    th = min(H, 256)

    def body(pl_ref, g_ref, r_ref, o_ref, ob_ref):
        s = g_ref[:, 0] + r_ref[...]
        o_ref[...] = s
        ob_ref[...] = s.astype(BF16)

    spec = pl.BlockSpec((1, th, C), lambda j, h, pr: (j, h, 0))
    return pl.pallas_call(
        body, name=name,
        grid_spec=pltpu.PrefetchScalarGridSpec(
            num_scalar_prefetch=1, grid=(N_CHIPS, H // th),
            in_specs=[pl.BlockSpec((1, 1, th, C), lambda j, h, pr: (j, pr[0], h, 0)), spec],
            out_specs=[spec, spec]),
        out_shape=[_sds((N_CHIPS, H, C), F32), _sds((N_CHIPS, H, C), BF16)],
    )(place, grad, recv)


def _add_chips(part, recv, place, name):
    _, H, C = part.shape
    th = min(H, 256)

    def body(pl_ref, p_ref, r_ref, o_ref):
        o_ref[...] = p_ref[0] + r_ref[0].astype(F32) + r_ref[1].astype(F32) + r_ref[2].astype(F32)

    return pl.pallas_call(
        body, name=name,
        grid_spec=pltpu.PrefetchScalarGridSpec(
            num_scalar_prefetch=1, grid=(H // th,),
            in_specs=[pl.BlockSpec((1, th, C), lambda h, pr: (pr[1], h, 0)),
                      pl.BlockSpec((len(CHIP_FLIPS), th, C), lambda h, pr: (0, h, 0))],
            out_specs=pl.BlockSpec((th, C), lambda h, pr: (h, 0))),
        out_shape=_sds((H, C), F32),
    )(place, part, recv)


def _adamw_math(w, g, m, v):
    m = ADAM_B1 * m + (1.0 - ADAM_B1) * g
    v = ADAM_B2 * v + (1.0 - ADAM_B2) * (g * g)
    m_hat = m / (1.0 - ADAM_B1 ** ADAM_STEP)
    v_hat = v / (1.0 - ADAM_B2 ** ADAM_STEP)
    delta = -ADAM_LR * (m_hat / (jnp.sqrt(v_hat) + ADAM_EPS) + ADAM_WD * w)
    return delta, m, v


def _adamw(w, mine, theirs, m, v, place, name):
    R, C = w.shape
    th = min(R // 2, 256)
    nb = (R // 2) // th

    def body(pl_ref, w_ref, a_ref, b_ref, m_ref, v_ref, g_ref, d_ref, mo_ref, vo_ref):
        g = jnp.where(pl.program_id(0) == pl_ref[0], a_ref[...], b_ref[...])
        g_ref[...] = g
        d, mo, vo = _adamw_math(w_ref[...], g, m_ref[...], v_ref[...])
        d_ref[...] = d
        mo_ref[...] = mo
        vo_ref[...] = vo

    whole = pl.BlockSpec((th, C), lambda h, r, pr: (h * nb + r, 0))
    half = pl.BlockSpec((th, C), lambda h, r, pr: (r, 0))
    return pl.pallas_call(
        body, name=name,
        grid_spec=pltpu.PrefetchScalarGridSpec(
            num_scalar_prefetch=1, grid=(2, nb),
            in_specs=[whole, half, half, whole, whole], out_specs=[whole] * 4),
        out_shape=[_sds((R, C), F32)] * 4,
    )(place, w, mine, theirs, m, v)


def _small_allreduce_adamw(gpack, wpack, mpack, vpack):
    R = gpack.shape[0]
    flips = [(fx, fy, fc) for fx in (0, 1) for fy in (0, 1) for fc in (0, 1)][1:]

    def body(g_ref, w_ref, m_ref, v_ref, gs_ref, d_ref, mo_ref, vo_ref, recv, lsem, ssem, rsem):
        x, y, c = _place()
        me = 4 * x + 2 * y + c
        own = pltpu.make_async_copy(g_ref, recv.at[me], lsem)
        sends = [pltpu.make_async_remote_copy(src_ref=g_ref, dst_ref=recv.at[me], send_sem=ssem.at[n],
                                              recv_sem=rsem.at[n], device_id=(x ^ fx, y ^ fy, c ^ fc),
                                              device_id_type=MESH)
                 for n, (fx, fy, fc) in enumerate(flips)]
        own.start()
        for cp in sends:
            cp.start()
        own.wait()
        for cp in sends:
            cp.wait()
        total = recv[0]
        for dev in range(1, N_DEV):
            total = total + recv[dev]
        gs_ref[...] = total
        d, mo, vo = _adamw_math(w_ref[...], total, m_ref[...], v_ref[...])
        d_ref[...] = d
        mo_ref[...] = mo
        vo_ref[...] = vo

    return pl.pallas_call(
        body, name="small_allreduce_adamw", in_specs=[VMEM_SPEC] * 4, out_specs=[VMEM_SPEC] * 4,
        out_shape=[_sds((R, LANES), F32)] * 4,
        scratch_shapes=[pltpu.VMEM((N_DEV, R, LANES), F32), pltpu.SemaphoreType.DMA,
                        pltpu.SemaphoreType.DMA((len(flips),)), pltpu.SemaphoreType.DMA((len(flips),))],
    )(gpack, wpack, mpack, vpack)


def _rows8(a):
    a = a.reshape(-1, LANES)
    pad = (-a.shape[0]) % 8
    return jnp.pad(a, ((0, pad), (0, 0))) if pad else a


def _pack(parts):
    return jnp.concatenate([_rows8(a) for a in parts], axis=0)


def _unpack(pack, like):
    out, row = [], 0
    for a in like:
        n = a.size // LANES
        out.append(pack[row:row + n].reshape(a.shape))
        row += n + (-n) % 8
    return out


def kernel(x, p, emb_ln_g, emb_ln_b, w_in, attn_out_g, w_pool, pool_scale, w_out, ln1_g, ln1_b, w_up, w_down, ln2_g, ln2_b, w_ple, w_ple_gate, ln3_g, ln3_b, loss_target, m_emb_ln_g, m_emb_ln_b, m_w_in, m_attn_out_g, m_w_pool, m_pool_scale, m_w_out, m_ln1_g, m_ln1_b, m_w_up, m_w_down, m_ln2_g, m_ln2_b, m_w_ple, m_w_ple_gate, m_ln3_g, m_ln3_b, v_emb_ln_g, v_emb_ln_b, v_w_in, v_attn_out_g, v_w_pool, v_pool_scale, v_w_out, v_ln1_g, v_ln1_b, v_w_up, v_w_down, v_ln2_g, v_ln2_b, v_w_ple, v_w_ple_gate, v_ln3_g, v_ln3_b):
    S = x.shape[1]
    tm = min(256, S)
    tq = min(256, S)
    tm_mlp = min(512, S)
    xs = x[0]
    ps = p[0, 0]
    tgt = loss_target[0]
    row = lambda a: a.reshape(1, -1)
    g0, b0 = row(emb_ln_g), row(emb_ln_b)
    g1, b1, g2, b2, g3, b3 = ln1_g, ln1_b, ln2_g, ln2_b, ln3_g, ln3_b
    wp = w_pool[0]

    xi, yi, ci = _place()
    place = jnp.stack([ci, 2 * xi + yi]).astype(jnp.int32)
    names = ["w_in", "w_out", "w_up", "w_down", "w_ple", "w_ple_gate"]

    big = [w_in[0], w_out[0], w_up[0], w_down[0], w_ple[0], w_ple_gate[0]]
    stacked = [_cast_into_slot(w, place, "cast_" + n) for w, n in zip(big, names)]
    (w_in_s,) = _gather_rider(stacked[:1]).run("gather_w_in")

    xh0, rstd0, q, k, v, u = _embln_inproj(xs, g0, b0, w_in_s, tm)
    o_raw, on, w_out_s, w_up_s, w_down_s, w_ple_s, w_gate_s = _attn_fwd(
        q, k, v, attn_out_g, tq, _gather_rider(stacked[1:]))
    w_out_f = w_out_s.reshape(D_MODEL, D_MODEL)
    w_down_f = w_down_s.reshape(D_FF, D_MODEL)
    w_gate_f = w_gate_s.reshape(D_MODEL, D_MODEL)
    d_b, pooled = _pool_fwd(u, wp, pool_scale, tm)
    xh1, rstd1, x1b = _mix_ln1(on, pooled, xh0, g0, b0, w_out_f, g1, b1, tm)
    xh2, rstd2 = _mlp_ln2(xh1, x1b, g1, b1, w_up_s, w_down_f, tm_mlp)

    (dpre2, dhb, dw_ple, dw_gate, dg3, db3, dg2, db2, loss_row) = _ple_ln3_loss(
        xh2, rstd2, g2, b2, ps, w_ple_s, w_gate_f, g3, b3, tgt, tm)
    dx1m, da, h1 = _mlp_bwd(x1b, dhb, w_up_s, w_down_f, tm_mlp)
    dw_up = _tn_matmul(x1b, da, "grad_w_up", 1024, min(512, S), stacked=True)
    dw_down = _tn_matmul(h1, dhb, "grad_w_down", 1024, min(512, S), stacked=False)
    def halves_of(g):
        return g.reshape(N_CHIPS, 2, g.shape[1] // 2, g.shape[2])

    early_names = names[2:]
    early = [halves_of(g) for g in (dw_up, dw_down.reshape(N_CHIPS, D_FF // N_CHIPS, D_MODEL), dw_ple,
                                    dw_gate.reshape(N_CHIPS, D_MODEL // N_CHIPS, D_MODEL))]
    dpre1, don, dpooled, dw_out, dg1, db1, *early_pair = _mix_bwd(
        dpre2, dx1m, xh1, rstd1, g1, w_out_f, on, pooled, tm, _pair_swap_rider(early))
    early_sum = [_add_pair(g, r, place, "pair_sum_" + n) for g, r, n in zip(early, early_pair, early_names)]
    du, dwp, dsc = _pool_bwd(dpooled, d_b, wp, pool_scale, tm)
    dq, dk, dv, dga, *early_chips = _attn_bwd(
        q, k, v, don, o_raw, attn_out_g, tq, _chip_scatter_rider([b for _, b in early_sum]))
    grad_x, dw_in, dg0, db0 = _inproj_bwd(dq, dk, dv, du, dpre1, xh0, rstd0, g0, b0, w_in_s, tm)

    late_names = names[:2]
    late = [halves_of(dw_in), halves_of(dw_out.reshape(N_CHIPS, D_MODEL // N_CHIPS, D_MODEL))]
    late_pair = _pair_swap_rider(late).run("reduce_pair_late")
    late_sum = [_add_pair(g, r, place, "pair_sum_" + n) for g, r, n in zip(late, late_pair, late_names)]
    late_chips = _chip_scatter_rider([b for _, b in late_sum]).run("reduce_chips_late")
    mine = [_add_chips(s, r, place, "chip_sum_" + n)
            for (s, _), r, n in zip(late_sum + early_sum, list(late_chips) + early_chips, names)]
    theirs = _pair_send_rider(mine).run("gather_pair")

    ms = [m_w_in, m_w_out, m_w_up, m_w_down, m_w_ple, m_w_ple_gate]
    vs = [v_w_in, v_w_out, v_w_up, v_w_down, v_w_ple, v_w_ple_gate]
    big_out = {}
    for n, w, a, b, m, vv in zip(names, big, mine, theirs, ms, vs):
        res4 = _adamw(w, a, b, m[0], vv[0], place, "adamw_" + n)
        big_out[n] = tuple(r.reshape(m.shape) for r in res4)

    small_names = ["emb_ln_g", "emb_ln_b", "attn_out_g", "w_pool", "pool_scale", "ln1_g", "ln1_b",
                   "ln2_g", "ln2_b", "ln3_g", "ln3_b"]
    small_w = [emb_ln_g, emb_ln_b, attn_out_g, w_pool, pool_scale, ln1_g, ln1_b, ln2_g, ln2_b, ln3_g, ln3_b]
    small_m = [m_emb_ln_g, m_emb_ln_b, m_attn_out_g, m_w_pool, m_pool_scale, m_ln1_g, m_ln1_b, m_ln2_g,
               m_ln2_b, m_ln3_g, m_ln3_b]
    small_v = [v_emb_ln_g, v_emb_ln_b, v_attn_out_g, v_w_pool, v_pool_scale, v_ln1_g, v_ln1_b, v_ln2_g,
               v_ln2_b, v_ln3_g, v_ln3_b]
    small_g = [dg0, db0, dga, dwp, dsc, dg1, db1, dg2, db2, dg3, db3]
    loss_like = jnp.zeros((8, LANES), F32)
    gpack = _pack([jnp.broadcast_to(loss_row, (8, LANES))] + small_g)
    gs, ds, mos, vos = _small_allreduce_adamw(gpack, _pack([loss_like] + small_w), _pack([loss_like] + small_m),
                                              _pack([jnp.ones((8, LANES), F32)] + small_v))
    like = [loss_like] + small_w
    gs_u, ds_u, mos_u, vos_u = (_unpack(a, like) for a in (gs, ds, mos, vos))
    loss = gs_u[0][0, 0]
    small_out = {n: (gs_u[i + 1], ds_u[i + 1], mos_u[i + 1], vos_u[i + 1]) for i, n in enumerate(small_names)}

    order = ["emb_ln_g", "emb_ln_b", "w_in", "attn_out_g", "w_pool", "pool_scale", "w_out", "ln1_g", "ln1_b",
             "w_up", "w_down", "ln2_g", "ln2_b", "w_ple", "w_ple_gate", "ln3_g", "ln3_b"]
    res = {**big_out, **small_out}
    outs = [loss, grad_x.reshape(x.shape)]
    for kind in range(4):
        outs += [res[n][kind] for n in order]
    return tuple(outs)
```

```python
import functools

import jax
import jax.numpy as jnp
from jax import lax
from jax.experimental import pallas as pl
from jax.experimental.pallas import tpu as pltpu

F32 = jnp.float32
BF16 = jnp.bfloat16

D_MODEL = 1024
ATTN_WIDTH = 512
POOL_WIDTH = 512
HEAD_DIM = 64
PAIR = 2 * HEAD_DIM
N_PAIRS = ATTN_WIDTH // PAIR
N_POOL_GROUPS = 4
POOL_GROUP = 128
POOL_HALO = 16
D_FF = 4096
PLE_DIM = 256
N_CHIPS = 4
N_DEV = 8
LN_EPS = 1e-5
RMS_EPS = 1e-6
ALPHA = float(2.0 ** 0.25)
Q_SCALE = 0.125
ADAM_LR = 0.001
ADAM_B1 = 0.9
ADAM_B2 = 0.999
ADAM_EPS = 1e-08
ADAM_WD = 0.01
ADAM_STEP = 10
LANES = 128
MIB = 1024 * 1024

MESH = pl.DeviceIdType.MESH
HBM_SPEC = pl.BlockSpec(memory_space=pltpu.HBM)
VMEM_SPEC = pl.BlockSpec(memory_space=pltpu.VMEM)


def _cp(vmem_mib):
    return pltpu.CompilerParams(vmem_limit_bytes=vmem_mib * MIB)


def _dot(a, b):
    return jnp.dot(a, b, preferred_element_type=F32)


def _dot_nt(a, b):
    return lax.dot_general(a, b, (((1,), (1,)), ((), ())), preferred_element_type=F32)


def _dot_tn(a, b):
    return lax.dot_general(a, b, (((0,), (0,)), ((), ())), preferred_element_type=F32)


def _ln_fwd(pre):
    mu = jnp.mean(pre, axis=-1, keepdims=True)
    xc = pre - mu
    var = jnp.mean(xc * xc, axis=-1, keepdims=True)
    rstd = lax.rsqrt(var + LN_EPS)
    return xc * rstd, rstd


def _ln_bwd(dy, xh, rstd, g):
    dxh = dy * g
    m1 = jnp.mean(dxh, axis=-1, keepdims=True)
    m2 = jnp.mean(dxh * xh, axis=-1, keepdims=True)
    return rstd * (dxh - m1 - xh * m2)


def _colsum(a):
    return jnp.sum(a, axis=0, keepdims=True)


def _neg_softplus(z):
    return -(jnp.maximum(z, 0.0) + jnp.log(1.0 + jnp.exp(-jnp.abs(z))))


def _split_bf16(a):
    hi = a.astype(BF16)
    lo = (a - hi.astype(F32)).astype(BF16)
    return hi, lo


def _row_spec(tm, n):
    return pl.BlockSpec((tm, n), lambda i: (i, 0))


def _const_spec(shape):
    nd = len(shape)
    return pl.BlockSpec(shape, lambda *_: (0,) * nd)


def _sds(shape, dtype):
    return jax.ShapeDtypeStruct(shape, dtype)


def _embln_inproj(x, g0, b0, w_in_s, tm, rider):
    S, D = x.shape
    n_t = S // tm

    def body(*refs):
        ((x_ref, g_ref, b_ref, w_ref), (xh_ref, rstd_ref, q_ref, k_ref, v_ref, u_ref), _,
         ride) = rider.split(refs, 4, 6, 0)
        i = pl.program_id(0)

        @pl.when(i == 0)
        def _():
            rider.first(ride)

        @pl.when(i == (3 * n_t) // 4)
        def _():
            rider.mid(ride)

        xh, rstd = _ln_fwd(x_ref[...])
        xh_ref[...] = xh
        rstd_ref[...] = rstd
        xb = (xh * g_ref[...] + b_ref[...]).astype(BF16)
        q_ref[...] = (_dot(xb, w_ref[0]) * Q_SCALE).astype(BF16)
        k_ref[...] = _dot(xb, w_ref[1]).astype(BF16)
        v_ref[...] = _dot(xb, w_ref[2]).astype(BF16)
        u_ref[...] = _dot(xb, w_ref[3])

        @pl.when(i == n_t - 1)
        def _():
            rider.last(ride)

    return rider.call(
        body, [x, g0, b0, w_in_s], grid=(n_t,), name="embln_inproj",
        in_specs=[_row_spec(tm, D), _const_spec((1, D)), _const_spec((1, D)),
                  _const_spec((N_CHIPS, D, 512))],
        out_specs=[_row_spec(tm, D), _row_spec(tm, 1), _row_spec(tm, 512), _row_spec(tm, 512),
                   _row_spec(tm, 512), _row_spec(tm, 512)],
        out_shape=[_sds((S, D), F32), _sds((S, 1), F32), _sds((S, 512), BF16), _sds((S, 512), BF16),
                   _sds((S, 512), BF16), _sds((S, 512), F32)],
        scratch_shapes=[], vmem_mib=40)


def _tri(n, upper):
    r = lax.broadcasted_iota(jnp.int32, (n, n), 0)
    c = lax.broadcasted_iota(jnp.int32, (n, n), 1)
    keep = (r < c) if upper else (r > c)
    return jnp.where(keep, 1.0, 0.0).astype(BF16)


def _strictly_causal(n):
    return lax.broadcasted_iota(jnp.int32, (n, n), 1) < lax.broadcasted_iota(jnp.int32, (n, n), 0)


LOG_WEIGHT_FLOOR = -110.0


def _sb_tile(qhs, kt, low, c_ls, valid):
    valids = valid if isinstance(valid, (list, tuple)) else [valid] * len(qhs)
    zs = [_dot_nt(qh, kt) for qh in qhs]
    lrs = [_neg_softplus(z) for z in zs]
    ls_ = [lr if m is None else jnp.where(m, lr, 0.0) for lr, m in zip(lrs, valids)]
    parts = [_split_bf16(l) for l in ls_]
    sfx = [_dot(hi, low) + _dot(lo, low) + c_l for (hi, lo), c_l in zip(parts, c_ls)]
    lss = [z + lr for z, lr in zip(zs, lrs)]
    ws = [jnp.exp(ls + s) for ls, s in zip(lss, sfx)]
    ws = [w if m is None else jnp.where(m, w, 0.0) for w, m in zip(ws, valids)]
    return lss, ls_, ws


def _attn_fwd(q, k, v, ga, tq, rider):
    S = q.shape[0]
    nq = S // tq

    def body(*refs):
        (q_ref, k_ref, v_ref, ga_ref), (o_ref, on_ref), _, ride = rider.split(refs, 4, 2, 0)
        p, i = pl.program_id(0), pl.program_id(1)

        @pl.when(jnp.logical_and(p == 0, i == 0))
        def _():
            rider.first(ride)

        @pl.when(jnp.logical_and(p == N_PAIRS - 1, i == 0))
        def _():
            rider.mid(ride)

        lane = lax.broadcasted_iota(jnp.int32, (1, PAIR), 1)
        m0 = lane < HEAD_DIM
        low = _tri(tq, upper=False)
        q2 = q_ref[...]
        qhs = [jnp.where(m0, q2, jnp.zeros_like(q2)), jnp.where(m0, jnp.zeros_like(q2), q2)]

        def tile(kb, c_ls, accs, valid):
            ks = pl.multiple_of(kb * tq, tq)
            kt = k_ref[pl.ds(ks, tq), :]
            vt = v_ref[pl.ds(ks, tq), :]
            _, ls_, ws = _sb_tile(qhs, kt, low, c_ls, valid)
            new_a = [acc + _dot(w.astype(BF16), vt) for acc, w in zip(accs, ws)]
            new_c = [c_l + jnp.sum(l, axis=1, keepdims=True) for c_l, l in zip(c_ls, ls_)]
            return new_c, new_a

        zc, za = jnp.zeros((tq, 1), F32), jnp.zeros((tq, PAIR), F32)

        def first_two():
            c_ls, accs = tile(i, [zc, zc], [za, za], _strictly_causal(tq))
            c_ls, accs = tile(i - 1, c_ls, accs, None)
            return (*c_ls, *accs)

        def first_one():
            c_ls, accs = tile(i, [zc, zc], [za, za], _strictly_causal(tq))
            return (*c_ls, *accs)

        st0 = lax.cond(i >= 1, first_two, first_one)

        def more(st):
            return jnp.logical_and(st[0] <= i, jnp.max(jnp.maximum(st[1], st[2])) > LOG_WEIGHT_FLOOR)

        def step(st):
            n, c0, c1, a0, a1 = st
            c_ls, accs = tile(i - n, [c0, c1], [a0, a1], None)
            return (n + 1, c_ls[0], c_ls[1], accs[0], accs[1])

        st = lax.while_loop(more, step, (jnp.int32(2), *st0))
        o = jnp.where(m0, st[3], st[4])
        o_ref[...] = o
        sq = o * o
        ms0 = jnp.sum(jnp.where(m0, sq, 0.0), axis=-1, keepdims=True) * (1.0 / HEAD_DIM)
        ms1 = jnp.sum(jnp.where(m0, 0.0, sq), axis=-1, keepdims=True) * (1.0 / HEAD_DIM)
        rs = jnp.where(m0, lax.rsqrt(ms0 + RMS_EPS), lax.rsqrt(ms1 + RMS_EPS))
        on_ref[...] = (o * rs * ga_ref[...]).astype(BF16)

        @pl.when(jnp.logical_and(p == N_PAIRS - 1, i == nq - 1))
        def _():
            rider.last(ride)

    return rider.call(
        body, [q, k, v, ga], grid=(N_PAIRS, nq), name="attn_fwd",
        in_specs=[pl.BlockSpec((tq, PAIR), lambda p, i: (i, p)),
                  pl.BlockSpec((S, PAIR), lambda p, i: (0, p)),
                  pl.BlockSpec((S, PAIR), lambda p, i: (0, p)),
                  pl.BlockSpec((1, PAIR), lambda p, i: (0, p))],
        out_specs=[pl.BlockSpec((tq, PAIR), lambda p, i: (i, p)),
                   pl.BlockSpec((tq, PAIR), lambda p, i: (i, p))],
        out_shape=[_sds((S, ATTN_WIDTH), F32), _sds((S, ATTN_WIDTH), BF16)],
        scratch_shapes=[], vmem_mib=40)


def _pool_fwd(u, w_pool, pscale, tm):
    S = u.shape[0]
    hb = tm // POOL_HALO

    def body(u_ref, uh_ref, wp_ref, sc_ref, d_ref, pooled_ref):
        i = pl.program_id(0)
        halo = jnp.where(i > 0, uh_ref[...], 0.0)
        pos = i * tm + lax.broadcasted_iota(jnp.int32, (tm, 1), 0)
        for g in range(N_POOL_GROUPS):
            win = 2 ** (g + 1)
            cols = slice(g * POOL_GROUP, (g + 1) * POOL_GROUP)
            ut = u_ref[:, cols]
            s = jnp.concatenate([halo[:, cols], ut], axis=0)
            for sh in (1, 2, 4, 8)[:g + 1]:
                s = s + pltpu.roll(s, sh, 0)
            cnt = jnp.minimum(pos + 1, win).astype(F32)
            db = (s[POOL_HALO:, :] / cnt - ut).astype(BF16)
            y = _dot(db, wp_ref[g].astype(BF16))
            d_ref[:, cols] = db
            pooled_ref[:, cols] = (y * sc_ref[:, cols]).astype(BF16)

    return pl.pallas_call(
        body, grid=(S // tm,), name="pool_fwd",
        in_specs=[_row_spec(tm, POOL_WIDTH),
                  pl.BlockSpec((POOL_HALO, POOL_WIDTH), lambda i: (jnp.maximum(i * hb - 1, 0), 0)),
                  _const_spec((N_POOL_GROUPS, POOL_GROUP, POOL_GROUP)), _const_spec((1, POOL_WIDTH))],
        out_specs=[_row_spec(tm, POOL_WIDTH), _row_spec(tm, POOL_WIDTH)],
        out_shape=[_sds((S, POOL_WIDTH), BF16), _sds((S, POOL_WIDTH), BF16)],
        compiler_params=_cp(32),
    )(u, u, w_pool, pscale)


def _mix_ln1(on, pooled, xh0, g0, b0, w_out, g1, b1, tm):
    S, D = xh0.shape

    def body(on_ref, po_ref, xh0_ref, g0_ref, b0_ref, w_ref, g1_ref, b1_ref, xh_ref, rstd_ref, xb_ref):
        mixed = _dot(on_ref[...], w_ref[:ATTN_WIDTH, :]) + _dot(po_ref[...], w_ref[ATTN_WIDTH:, :])
        x0 = xh0_ref[...] * g0_ref[...] + b0_ref[...]
        xh, rstd = _ln_fwd(ALPHA * x0 + mixed)
        xh_ref[...] = xh
        rstd_ref[...] = rstd
        xb_ref[...] = (xh * g1_ref[...] + b1_ref[...]).astype(BF16)

    return pl.pallas_call(
        body, grid=(S // tm,), name="mix_ln1",
        in_specs=[_row_spec(tm, ATTN_WIDTH), _row_spec(tm, POOL_WIDTH), _row_spec(tm, D),
                  _const_spec((1, D)), _const_spec((1, D)), _const_spec((D, D)),
                  _const_spec((1, D)), _const_spec((1, D))],
        out_specs=[_row_spec(tm, D), _row_spec(tm, 1), _row_spec(tm, D)],
        out_shape=[_sds((S, D), F32), _sds((S, 1), F32), _sds((S, D), BF16)],
        compiler_params=_cp(40),
    )(on, pooled, xh0, g0, b0, w_out, g1, b1)


def _mlp_ln2(xh1, x1b, g1, b1, w_up_s, w_down, tm):
    S, D = xh1.shape
    fc = D_FF // N_CHIPS

    def body(xh_ref, xb_ref, g_ref, b_ref, wu_ref, wd_ref, xh2_ref, rstd_ref, acc_ref):
        j = pl.program_id(1)

        @pl.when(j == 0)
        def _():
            acc_ref[...] = jnp.zeros_like(acc_ref)

        a = _dot(xb_ref[...], wu_ref[0])
        r = jnp.maximum(a, 0.0)
        acc_ref[...] += _dot((r * r).astype(BF16), wd_ref[...])

        @pl.when(j == N_CHIPS - 1)
        def _():
            x1 = xh_ref[...] * g_ref[...] + b_ref[...]
            xh, rstd = _ln_fwd(ALPHA * x1 + acc_ref[...])
            xh2_ref[...] = xh
            rstd_ref[...] = rstd

    return pl.pallas_call(
        body, grid=(S // tm, N_CHIPS), name="mlp_ln2",
        in_specs=[pl.BlockSpec((tm, D), lambda i, j: (i, 0)), pl.BlockSpec((tm, D), lambda i, j: (i, 0)),
                  pl.BlockSpec((1, D), lambda i, j: (0, 0)), pl.BlockSpec((1, D), lambda i, j: (0, 0)),
                  pl.BlockSpec((1, D, fc), lambda i, j: (j, 0, 0)),
                  pl.BlockSpec((fc, D), lambda i, j: (j, 0))],
        out_specs=[pl.BlockSpec((tm, D), lambda i, j: (i, 0)), pl.BlockSpec((tm, 1), lambda i, j: (i, 0))],
        out_shape=[_sds((S, D), F32), _sds((S, 1), F32)],
        scratch_shapes=[pltpu.VMEM((tm, D), F32)],
        compiler_params=_cp(40),
    )(xh1, x1b, g1, b1, w_up_s, w_down)


def _ple_ln3_loss(xh2, rstd2, g2, b2, p, w_ple_s, w_gate, g3, b3, target, tm):
    S, D = xh2.shape
    pc = D // N_CHIPS

    def body(xh2_ref, rstd2_ref, g2_ref, b2_ref, p_ref, wp_ref, wg_ref, g3_ref, b3_ref, t_ref,
             dpre2_ref, dhb_ref, dwp_ref, dwg_ref, dg3_ref, db3_ref, dg2_ref, db2_ref, loss_ref):
        i = pl.program_id(0)

        @pl.when(i == 0)
        def _():
            for r in (dwp_ref, dwg_ref, dg3_ref, db3_ref, dg2_ref, db2_ref, loss_ref):
                r[...] = jnp.zeros_like(r)

        xh2 = xh2_ref[...]
        x2 = xh2 * g2_ref[...] + b2_ref[...]
        x2b = x2.astype(BF16)
        gate = 1.0 / (1.0 + jnp.exp(-_dot(x2b, wg_ref[...])))
        pb = p_ref[...].astype(BF16)
        pe = jnp.concatenate([_dot(pb, wp_ref[c]) for c in range(N_CHIPS)], axis=1)
        xh3, rstd3 = _ln_fwd(ALPHA * x2 + pe * gate)
        diff = xh3 * g3_ref[...] + b3_ref[...] - t_ref[...]
        loss_ref[...] += (0.5 / D) * jnp.sum(diff * diff)
        dy = diff * (1.0 / D)
        dg3_ref[...] += _colsum(dy * xh3)
        db3_ref[...] += _colsum(dy)
        dpre3 = _ln_bwd(dy, xh3, rstd3, g3_ref[...])
        dpe_b = (dpre3 * gate).astype(BF16)
        dgp_b = (dpre3 * pe * gate * (1.0 - gate)).astype(BF16)
        dx2 = ALPHA * dpre3 + _dot_nt(dgp_b, wg_ref[...])
        dwg_ref[...] += _dot_tn(x2b, dgp_b)
        for c in range(N_CHIPS):
            dwp_ref[c] += _dot_tn(pb, dpe_b[:, c * pc:(c + 1) * pc])
        dg2_ref[...] += _colsum(dx2 * xh2)
        db2_ref[...] += _colsum(dx2)
        dpre2 = _ln_bwd(dx2, xh2, rstd2_ref[...], g2_ref[...])
        dpre2_ref[...] = dpre2
        dhb_ref[...] = dpre2.astype(BF16)

    vec = _const_spec((1, D))
    return pl.pallas_call(
        body, grid=(S // tm,), name="ple_ln3_loss",
        in_specs=[_row_spec(tm, D), _row_spec(tm, 1), vec, vec, _row_spec(tm, PLE_DIM),
                  _const_spec((N_CHIPS, PLE_DIM, pc)), _const_spec((D, D)), vec, vec, _row_spec(tm, D)],
        out_specs=[_row_spec(tm, D), _row_spec(tm, D), _const_spec((N_CHIPS, PLE_DIM, pc)),
                   _const_spec((D, D)), vec, vec, vec, vec, _const_spec((1, LANES))],
        out_shape=[_sds((S, D), F32), _sds((S, D), BF16), _sds((N_CHIPS, PLE_DIM, pc), F32),
                   _sds((D, D), F32), _sds((1, D), F32), _sds((1, D), F32), _sds((1, D), F32),
                   _sds((1, D), F32), _sds((1, LANES), F32)],
        compiler_params=_cp(48),
    )(xh2, rstd2, g2, b2, p, w_ple_s, w_gate, g3, b3, target)


def _mlp_bwd(x1b, dhb, w_up_s, w_down, tm):
    S, D = x1b.shape
    fc = D_FF // N_CHIPS

    def body(xb_ref, dh_ref, wu_ref, wd_ref, dx_ref, da_ref, h1_ref):
        j = pl.program_id(1)

        @pl.when(j == 0)
        def _():
            dx_ref[...] = jnp.zeros_like(dx_ref)

        a = _dot(xb_ref[...], wu_ref[0])
        r = jnp.maximum(a, 0.0)
        h1_ref[...] = (r * r).astype(BF16)
        da = (_dot_nt(dh_ref[...], wd_ref[...]) * (2.0 * r)).astype(BF16)
        da_ref[...] = da
        dx_ref[...] += _dot_nt(da, wu_ref[0])

    return pl.pallas_call(
        body, grid=(S // tm, N_CHIPS), name="mlp_bwd",
        in_specs=[pl.BlockSpec((tm, D), lambda i, j: (i, 0)), pl.BlockSpec((tm, D), lambda i, j: (i, 0)),
                  pl.BlockSpec((1, D, fc), lambda i, j: (j, 0, 0)),
                  pl.BlockSpec((fc, D), lambda i, j: (j, 0))],
        out_specs=[pl.BlockSpec((tm, D), lambda i, j: (i, 0)), pl.BlockSpec((tm, fc), lambda i, j: (i, j)),
                   pl.BlockSpec((tm, fc), lambda i, j: (i, j))],
        out_shape=[_sds((S, D), F32), _sds((S, D_FF), BF16), _sds((S, D_FF), BF16)],
        compiler_params=_cp(40),
    )(x1b, dhb, w_up_s, w_down)


def _tn_matmul(a, b, name, tk, tt, stacked):
    T, K = a.shape
    N = b.shape[1]
    tn = 1024

    def body(a_ref, b_ref, o_ref):
        @pl.when(pl.program_id(2) == 0)
        def _():
            o_ref[...] = jnp.zeros_like(o_ref)

        prod = _dot_tn(a_ref[...], b_ref[...])
        if stacked:
            o_ref[0] += prod
        else:
            o_ref[...] += prod

    if stacked:
        out_spec = pl.BlockSpec((1, tk, tn), lambda k, n, t: (n, k, 0))
        out_shape = _sds((N // tn, K, tn), F32)
    else:
        out_spec = pl.BlockSpec((tk, tn), lambda k, n, t: (k, n))
        out_shape = _sds((K, N), F32)
    return pl.pallas_call(
        body, grid=(K // tk, N // tn, T // tt), name=name,
        in_specs=[pl.BlockSpec((tt, tk), lambda k, n, t: (t, k)),
                  pl.BlockSpec((tt, tn), lambda k, n, t: (t, n))],
        out_specs=out_spec, out_shape=out_shape,
        compiler_params=_cp(40),
    )(a, b)


def _mix_bwd(dpre2, dx1m, xh1, rstd1, g1, w_out, on, pooled, tm, rider):
    S, D = xh1.shape
    n_t = S // tm

    def body(*refs):
        ((dp2_ref, dxm_ref, xh_ref, rstd_ref, g_ref, w_ref, on_ref, po_ref),
         (dpre1_ref, don_ref, dpo_ref, dw_ref, dg_ref, db_ref), _, ride) = rider.split(refs, 8, 6, 0)

        @pl.when(pl.program_id(0) == 0)
        def _():
            rider.first(ride)
            for r in (dw_ref, dg_ref, db_ref):
                r[...] = jnp.zeros_like(r)

        xh = xh_ref[...]
        dx1 = ALPHA * dp2_ref[...] + dxm_ref[...]
        dg_ref[...] += _colsum(dx1 * xh)
        db_ref[...] += _colsum(dx1)
        dpre1 = _ln_bwd(dx1, xh, rstd_ref[...], g_ref[...])
        dpre1_ref[...] = dpre1
        dmb = dpre1.astype(BF16)
        dcat = _dot_nt(dmb, w_ref[...])
        don_ref[...] = dcat[:, :ATTN_WIDTH]
        dpo_ref[...] = dcat[:, ATTN_WIDTH:]
        dw_ref[:ATTN_WIDTH, :] += _dot_tn(on_ref[...], dmb)
        dw_ref[ATTN_WIDTH:, :] += _dot_tn(po_ref[...], dmb)

        @pl.when(pl.program_id(0) == n_t - 1)
        def _():
            rider.last(ride)

    vec = _const_spec((1, D))
    return rider.call(
        body, [dpre2, dx1m, xh1, rstd1, g1, w_out, on, pooled], grid=(n_t,), name="mix_bwd",
        in_specs=[_row_spec(tm, D), _row_spec(tm, D), _row_spec(tm, D), _row_spec(tm, 1), vec,
                  _const_spec((D, D)), _row_spec(tm, ATTN_WIDTH), _row_spec(tm, POOL_WIDTH)],
        out_specs=[_row_spec(tm, D), _row_spec(tm, ATTN_WIDTH), _row_spec(tm, POOL_WIDTH),
                   _const_spec((D, D)), vec, vec],
        out_shape=[_sds((S, D), F32), _sds((S, ATTN_WIDTH), F32), _sds((S, POOL_WIDTH), F32),
                   _sds((D, D), F32), _sds((1, D), F32), _sds((1, D), F32)],
        scratch_shapes=[], vmem_mib=48)


def _pool_bwd(dpooled, d_b, w_pool, pscale, tm, rider):
    S = dpooled.shape[0]
    hb = tm // POOL_HALO
    n_t = S // tm
    te = tm + POOL_HALO

    def body(*refs):
        ((dp_ref, dph_ref, d_ref, wp_ref, sc_ref), (du_ref, dwp_ref, dsc_ref), _,
         ride) = rider.split(refs, 5, 3, 0)
        i = pl.program_id(0)

        @pl.when(i == 0)
        def _():
            rider.first(ride)
            dwp_ref[...] = jnp.zeros_like(dwp_ref)
            dsc_ref[...] = jnp.zeros_like(dsc_ref)

        halo = jnp.where(i < n_t - 1, dph_ref[...], 0.0)
        pos = i * tm + lax.broadcasted_iota(jnp.int32, (te, 1), 0)
        for g in range(N_POOL_GROUPS):
            win = 2 ** (g + 1)
            cols = slice(g * POOL_GROUP, (g + 1) * POOL_GROUP)
            wpb = wp_ref[g].astype(BF16)
            dpt = dp_ref[:, cols]
            dpe = jnp.concatenate([dpt, halo[:, cols]], axis=0)
            dyb = (dpe * sc_ref[:, cols]).astype(BF16)
            dd = _dot_nt(dyb, wpb)
            s = dd / jnp.minimum(pos + 1, win).astype(F32)
            for sh in (1, 2, 4, 8)[:g + 1]:
                s = s + pltpu.roll(s, te - sh, 0)
            du_ref[:, cols] = s[:tm, :] - dd[:tm, :]
            db = d_ref[:, cols]
            dwp_ref[g] += _dot_tn(db, dyb[:tm, :])
            dsc_ref[:, cols] += _colsum(dpt * _dot(db, wpb))

        @pl.when(i == n_t - 1)
        def _():
            rider.last(ride)

    return rider.call(
        body, [dpooled, dpooled, d_b, w_pool, pscale], grid=(n_t,), name="pool_bwd",
        in_specs=[_row_spec(tm, POOL_WIDTH),
                  pl.BlockSpec((POOL_HALO, POOL_WIDTH),
                               lambda i: (jnp.minimum((i + 1) * hb, S // POOL_HALO - 1), 0)),
                  _row_spec(tm, POOL_WIDTH),
                  _const_spec((N_POOL_GROUPS, POOL_GROUP, POOL_GROUP)), _const_spec((1, POOL_WIDTH))],
        out_specs=[_row_spec(tm, POOL_WIDTH), _const_spec((N_POOL_GROUPS, POOL_GROUP, POOL_GROUP)),
                   _const_spec((1, POOL_WIDTH))],
        out_shape=[_sds((S, POOL_WIDTH), F32), _sds((N_POOL_GROUPS, POOL_GROUP, POOL_GROUP), F32),
                   _sds((1, POOL_WIDTH), F32)],
        scratch_shapes=[], vmem_mib=32)


def _attn_bwd(q, k, v, don, o_raw, ga, tq, rider):
    S = q.shape[0]
    nq = S // tq

    def body(*refs):
        ((q_ref, k_ref, v_ref, don_ref, o_ref, ga_ref), (dq_ref, dk_ref, dv_ref, dga_ref),
         (g_s, b_s), ride) = rider.split(refs, 6, 4, 2)
        p, i = pl.program_id(0), pl.program_id(1)

        @pl.when(jnp.logical_and(p == 0, i == 0))
        def _():
            rider.first(ride)

        @pl.when(i == 0)
        def _():
            for r in (dk_ref, dv_ref, dga_ref):
                r[...] = jnp.zeros_like(r)

        lane = lax.broadcasted_iota(jnp.int32, (1, PAIR), 1)
        m0 = lane < HEAD_DIM
        low = _tri(tq, upper=False)
        upp = _tri(tq, upper=True)

        def seg_mean(a):
            s0 = jnp.sum(jnp.where(m0, a, 0.0), axis=-1, keepdims=True)
            s1 = jnp.sum(jnp.where(m0, 0.0, a), axis=-1, keepdims=True)
            return jnp.where(m0, s0, s1) * (1.0 / HEAD_DIM)

        o = o_ref[...]
        rs = lax.rsqrt(seg_mean(o * o) + RMS_EPS)
        oh = o * rs
        don = don_ref[...]
        dga_ref[...] += _colsum(don * oh)
        doh = don * ga_ref[...]
        do = rs * (doh - oh * seg_mean(doh * oh))
        dob = do.astype(BF16)
        q2 = q_ref[...]
        qhs = [jnp.where(m0, q2, jnp.zeros_like(q2)), jnp.where(m0, jnp.zeros_like(q2), q2)]
        dhs = [jnp.where(m0, dob, jnp.zeros_like(dob)), jnp.where(m0, jnp.zeros_like(dob), dob)]
        causal = _strictly_causal(tq)

        def down(kb, c_ls, valid):
            ks = pl.multiple_of(kb * tq, tq)
            kt = k_ref[pl.ds(ks, tq), :]
            vt = v_ref[pl.ds(ks, tq), :]
            lss, ls_, ws = _sb_tile(qhs, kt, low, c_ls, valid)
            dws = [_dot_nt(dh, vt) for dh in dhs]
            for hh in range(2):
                g_s[hh, kb] = dws[hh] * ws[hh]
                b_s[hh, kb] = jnp.exp(lss[hh])
            dv_ref[pl.ds(ks, tq), :] += (_dot_tn(ws[0].astype(BF16), dhs[0])
                                         + _dot_tn(ws[1].astype(BF16), dhs[1]))
            return [c_l + jnp.sum(l, axis=1, keepdims=True) for c_l, l in zip(c_ls, ls_)]

        zc, za = jnp.zeros((tq, 1), F32), jnp.zeros((tq, PAIR), F32)
        c_ls = lax.cond(i >= 1, lambda: tuple(down(i - 1, down(i, [zc, zc], causal), None)),
                        lambda: tuple(down(i, [zc, zc], causal)))

        def more(st):
            return jnp.logical_and(st[0] <= i, jnp.max(jnp.maximum(st[1], st[2])) > LOG_WEIGHT_FLOOR)

        def down_step(st):
            c_ls = down(i - st[0], [st[1], st[2]], None)
            return (st[0] + 1, c_ls[0], c_ls[1])

        n_tiles = lax.while_loop(more, down_step, (jnp.int32(2), c_ls[0], c_ls[1]))[0]

        def up(kb, c_gs, accs, valid):
            ks = pl.multiple_of(kb * tq, tq)
            kt = k_ref[pl.ds(ks, tq), :]
            gs = [g_s[hh, kb] for hh in range(2)]
            parts = [_split_bf16(g) for g in gs]
            pres = [_dot(hi, upp) + _dot(lo, upp) + c_g for (hi, lo), c_g in zip(parts, c_gs)]
            dzs = []
            for hh in range(2):
                beta = b_s[hh, kb]
                dz = gs[hh] * (1.0 - beta) - beta * pres[hh]
                if valid is not None:
                    dz = jnp.where(valid, dz, 0.0)
                dzs.append(dz.astype(BF16))
            new_a = [acc + _dot(dzb, kt) for acc, dzb in zip(accs, dzs)]
            dk_ref[pl.ds(ks, tq), :] += _dot_tn(dzs[0], qhs[0]) + _dot_tn(dzs[1], qhs[1])
            new_c = [c_g + jnp.sum(g, axis=1, keepdims=True) for c_g, g in zip(c_gs, gs)]
            return new_c, new_a

        def up_step(kb, st):
            c_gs, accs = up(kb, [st[0], st[1]], [st[2], st[3]], None)
            return (c_gs[0], c_gs[1], accs[0], accs[1])

        st = lax.fori_loop(i - n_tiles + 1, i - 1, up_step, (zc, zc, za, za))

        def last_two():
            c_gs, accs = up(i - 1, [st[0], st[1]], [st[2], st[3]], None)
            return tuple(up(i, c_gs, accs, causal)[1])

        accs = lax.cond(i >= 1, last_two, lambda: tuple(up(i, [zc, zc], [za, za], causal)[1]))
        dq_ref[...] = jnp.where(m0, accs[0], accs[1]) * Q_SCALE

        @pl.when(jnp.logical_and(p == N_PAIRS - 1, i == nq - 1))
        def _():
            rider.last(ride)

    return rider.call(
        body, [q, k, v, don, o_raw, ga], grid=(N_PAIRS, nq), name="attn_bwd",
        in_specs=[pl.BlockSpec((tq, PAIR), lambda p, i: (i, p)),
                  pl.BlockSpec((S, PAIR), lambda p, i: (0, p)),
                  pl.BlockSpec((S, PAIR), lambda p, i: (0, p)),
                  pl.BlockSpec((tq, PAIR), lambda p, i: (i, p)),
                  pl.BlockSpec((tq, PAIR), lambda p, i: (i, p)),
                  pl.BlockSpec((1, PAIR), lambda p, i: (0, p))],
        out_specs=[pl.BlockSpec((tq, PAIR), lambda p, i: (i, p)),
                   pl.BlockSpec((S, PAIR), lambda p, i: (0, p)),
                   pl.BlockSpec((S, PAIR), lambda p, i: (0, p)),
                   pl.BlockSpec((1, PAIR), lambda p, i: (0, p))],
        out_shape=[_sds((S, ATTN_WIDTH), F32), _sds((S, ATTN_WIDTH), F32), _sds((S, ATTN_WIDTH), F32),
                   _sds((1, ATTN_WIDTH), F32)],
        scratch_shapes=[pltpu.VMEM((2, nq, tq, tq), F32), pltpu.VMEM((2, nq, tq, tq), F32)],
        vmem_mib=56)


def _inproj_bwd(dq, dk, dv, du, dpre1, xh0, rstd0, g0, b0, w_in_s, tm):
    S, D = xh0.shape

    def body(dq_ref, dk_ref, dv_ref, du_ref, dp1_ref, xh_ref, rstd_ref, g_ref, b_ref, w_ref,
             gx_ref, dw_ref, dg_ref, db_ref):
        @pl.when(pl.program_id(0) == 0)
        def _():
            for r in (dw_ref, dg_ref, db_ref):
                r[...] = jnp.zeros_like(r)

        xh = xh_ref[...]
        xb = (xh * g_ref[...] + b_ref[...]).astype(BF16)
        dx0 = ALPHA * dp1_ref[...]
        for c, r in enumerate((dq_ref, dk_ref, dv_ref, du_ref)):
            dpb = r[...].astype(BF16)
            dx0 = dx0 + _dot_nt(dpb, w_ref[c])
            dw_ref[c] += _dot_tn(xb, dpb)
        dg_ref[...] += _colsum(dx0 * xh)
        db_ref[...] += _colsum(dx0)
        gx_ref[...] = _ln_bwd(dx0, xh, rstd_ref[...], g_ref[...])

    vec = _const_spec((1, D))
    half = _row_spec(tm, 512)
    return pl.pallas_call(
        body, grid=(S // tm,), name="inproj_bwd",
        in_specs=[half, half, half, half, _row_spec(tm, D), _row_spec(tm, D), _row_spec(tm, 1), vec, vec,
                  _const_spec((N_CHIPS, D, 512))],
        out_specs=[_row_spec(tm, D), _const_spec((N_CHIPS, D, 512)), vec, vec],
        out_shape=[_sds((S, D), F32), _sds((N_CHIPS, D, 512), F32), _sds((1, D), F32), _sds((1, D), F32)],
        compiler_params=_cp(56),
    )(dq, dk, dv, du, dpre1, xh0, rstd0, g0, b0, w_in_s)


def _place():
    return lax.axis_index("x"), lax.axis_index("y"), lax.axis_index("c")


CHIP_FLIPS = ((0, 1), (1, 0), (1, 1))


class _Rider:
    def __init__(self, ins, out_shapes, n_sem, phases, aliases=None):
        self.ins, self.out_shapes, self.n_sem, self.phases = list(ins), list(out_shapes), n_sem, phases
        self.aliases = aliases or {}

    def __add__(self, other):
        na, ma = len(self.ins), len(self.out_shapes)

        def phases(ins, outs, ssem, rsem):
            mine = self.phases(ins[:na], outs[:ma], ssem, rsem)
            rest = pl.ds(self.n_sem, other.n_sem)
            theirs = other.phases(ins[na:], outs[ma:], ssem.at[rest], rsem.at[rest])
            assert len(mine) == 1 and len(theirs) == 1
            return [mine[0] + theirs[0]]

        return _Rider(self.ins + other.ins, self.out_shapes + other.out_shapes, self.n_sem + other.n_sem, phases)

    def split(self, refs, n_in, n_out, n_scratch):
        a = n_in + len(self.ins)
        b = a + n_out
        c = b + len(self.out_shapes)
        own = (refs[:n_in], refs[a:b], refs[c:c + n_scratch])
        return own + ((refs[n_in:a], refs[b:c]) + tuple(refs[c + n_scratch:]),)

    def first(self, ride):
        for make in self.phases(*ride)[0]:
            make().start()

    def mid(self, ride):
        ph = self.phases(*ride)
        if len(ph) == 2:
            for make in ph[0]:
                make().wait_recv()
            for make in ph[1]:
                make().start()

    def last(self, ride):
        ph = self.phases(*ride)
        if len(ph) == 2:
            for make in ph[0]:
                make().wait_send()
        for make in ph[-1]:
            make().wait()

    def call(self, body, args, *, grid, name, in_specs, out_specs, out_shape, scratch_shapes, vmem_mib):
        n_in, n_out = len(in_specs), len(out_specs)
        sems = [pltpu.SemaphoreType.DMA((self.n_sem,)), pltpu.SemaphoreType.DMA((self.n_sem,))]
        return pl.pallas_call(
            body, grid=grid, name=name,
            in_specs=list(in_specs) + [HBM_SPEC] * len(self.ins),
            out_specs=list(out_specs) + [HBM_SPEC] * len(self.out_shapes),
            out_shape=list(out_shape) + self.out_shapes,
            scratch_shapes=list(scratch_shapes) + sems,
            input_output_aliases={n_in + i: n_out + o for i, o in self.aliases.items()},
            compiler_params=_cp(vmem_mib),
        )(*args, *self.ins)

    def run(self, name):
        def body(*refs):
            ride = self.split(refs, 0, 0, 0)[3]
            self.first(ride)
            self.mid(ride)
            self.last(ride)

        return self.call(body, [], grid=(), name=name, in_specs=[], out_specs=[], out_shape=[],
                         scratch_shapes=[], vmem_mib=16)


def _remote(src, dst, ssem, rsem, n, dev):
    return functools.partial(pltpu.make_async_remote_copy, src_ref=src, dst_ref=dst, send_sem=ssem.at[n],
                             recv_sem=rsem.at[n], device_id=dev, device_id_type=MESH)


def _cast_into_slot(w, place, name):
    R, C = w.shape
    tr = min(R, 512)

    def body(pl_ref, w_ref, o_ref):
        o_ref[0] = w_ref[...].astype(BF16)

    return pl.pallas_call(
        body, name=name,
        grid_spec=pltpu.PrefetchScalarGridSpec(
            num_scalar_prefetch=1, grid=(R // tr,),
            in_specs=[pl.BlockSpec((tr, C), lambda r, pr: (r, 0))],
            out_specs=pl.BlockSpec((1, tr, C), lambda r, pr: (pr[1], r, 0))),
        out_shape=_sds((N_CHIPS, R, C), BF16),
    )(place, w)


def _gather_rider(stacked):
    n, nf = len(stacked), len(CHIP_FLIPS)

    def phases(ins, outs, ssem, rsem):
        x, y, c = _place()
        slot = 2 * x + y
        ici, d2d = [], []
        for w, (i_ref, o_ref) in enumerate(zip(ins, outs)):
            hh = o_ref.shape[1] // 2
            rows = pl.ds(c * hh, hh)
            for f, (fx, fy) in enumerate(CHIP_FLIPS):
                k = w * nf + f
                theirs = 2 * (x ^ fx) + (y ^ fy)
                ici.append(_remote(i_ref.at[slot, rows], o_ref.at[slot, rows], ssem, rsem, k, (x ^ fx, y ^ fy, c)))
                d2d.append(_remote(o_ref.at[theirs, rows], o_ref.at[theirs, rows], ssem, rsem, n * nf + k,
                                   (x, y, 1 - c)))
        return [ici, d2d]

    return _Rider(stacked, [_sds(s.shape, s.dtype) for s in stacked], 2 * n * nf, phases,
                  aliases={i: i for i in range(n)})


def _pair_swap_rider(grads):
    def phases(ins, outs, ssem, rsem):
        x, y, c = _place()
        return [[_remote(g.at[:, 1 - c], o, ssem, rsem, k, (x, y, 1 - c))
                 for k, (g, o) in enumerate(zip(ins, outs))]]

    return _Rider(grads, [_sds((N_CHIPS,) + g.shape[2:], g.dtype) for g in grads], len(grads), phases)


def _chip_scatter_rider(parts):
    nf = len(CHIP_FLIPS)

    def phases(ins, outs, ssem, rsem):
        x, y, c = _place()
        return [[_remote(r.at[2 * (x ^ fx) + (y ^ fy)], o.at[f], ssem, rsem, w * nf + f, (x ^ fx, y ^ fy, c))
                 for w, (r, o) in enumerate(zip(ins, outs)) for f, (fx, fy) in enumerate(CHIP_FLIPS)]]

    return _Rider(parts, [_sds((nf,) + r.shape[1:], r.dtype) for r in parts], len(parts) * nf, phases)


def _pair_send_rider(halves):
    def phases(ins, outs, ssem, rsem):
        x, y, c = _place()
        return [[_remote(h, o, ssem, rsem, k, (x, y, 1 - c)) for k, (h, o) in enumerate(zip(ins, outs))]]

    return _Rider(halves, [_sds(h.shape, h.dtype) for h in halves], len(halves), phases)


def _add_pair(grad, recv, place, name):
    _, _, H, C = grad.shape
    th = min(H, 256)

    def body(pl_ref, g_ref, r_ref, o_ref, ob_ref):
        s = g_ref[:, 0] + r_ref[...]
        o_ref[...] = s
        ob_ref[...] = s.astype(BF16)

    spec = pl.BlockSpec((1, th, C), lambda j, h, pr: (j, h, 0))
    return pl.pallas_call(
        body, name=name,
        grid_spec=pltpu.PrefetchScalarGridSpec(
            num_scalar_prefetch=1, grid=(N_CHIPS, H // th),
            in_specs=[pl.BlockSpec((1, 1, th, C), lambda j, h, pr: (j, pr[0], h, 0)), spec],
            out_specs=[spec, spec]),
        out_shape=[_sds((N_CHIPS, H, C), F32), _sds((N_CHIPS, H, C), BF16)],
    )(place, grad, recv)


def _add_chips(part, recv, place, name):
    _, H, C = part.shape
    th = min(H, 256)

    def body(pl_ref, p_ref, r_ref, o_ref):
        o_ref[...] = p_ref[0] + r_ref[0].astype(F32) + r_ref[1].astype(F32) + r_ref[2].astype(F32)

    return pl.pallas_call(
        body, name=name,
        grid_spec=pltpu.PrefetchScalarGridSpec(
            num_scalar_prefetch=1, grid=(H // th,),
            in_specs=[pl.BlockSpec((1, th, C), lambda h, pr: (pr[1], h, 0)),
                      pl.BlockSpec((len(CHIP_FLIPS), th, C), lambda h, pr: (0, h, 0))],
            out_specs=pl.BlockSpec((th, C), lambda h, pr: (h, 0))),
        out_shape=_sds((H, C), F32),
    )(place, part, recv)


def _adamw_math(w, g, m, v):
    m = ADAM_B1 * m + (1.0 - ADAM_B1) * g
    v = ADAM_B2 * v + (1.0 - ADAM_B2) * (g * g)
    m_hat = m / (1.0 - ADAM_B1 ** ADAM_STEP)
    v_hat = v / (1.0 - ADAM_B2 ** ADAM_STEP)
    delta = -ADAM_LR * (m_hat / (jnp.sqrt(v_hat) + ADAM_EPS) + ADAM_WD * w)
    return delta, m, v


def _adamw(w, mine, theirs, m, v, place, name):
    R, C = w.shape
    th = min(R // 2, 256)
    nb = (R // 2) // th

    def body(pl_ref, w_ref, a_ref, b_ref, m_ref, v_ref, g_ref, d_ref, mo_ref, vo_ref):
        g = jnp.where(pl.program_id(0) == pl_ref[0], a_ref[...], b_ref[...])
        g_ref[...] = g
        d, mo, vo = _adamw_math(w_ref[...], g, m_ref[...], v_ref[...])
        d_ref[...] = d
        mo_ref[...] = mo
        vo_ref[...] = vo

    whole = pl.BlockSpec((th, C), lambda h, r, pr: (h * nb + r, 0))
    half = pl.BlockSpec((th, C), lambda h, r, pr: (r, 0))
    return pl.pallas_call(
        body, name=name,
        grid_spec=pltpu.PrefetchScalarGridSpec(
            num_scalar_prefetch=1, grid=(2, nb),
            in_specs=[whole, half, half, whole, whole], out_specs=[whole] * 4),
        out_shape=[_sds((R, C), F32)] * 4,
    )(place, w, mine, theirs, m, v)


DEVICE_FLIPS = tuple((fx, fy, fc) for fx in (0, 1) for fy in (0, 1) for fc in (0, 1))[1:]


def _pack_exchange_rider(pack):
    def phases(ins, outs, ssem, rsem):
        x, y, c = _place()
        mine = outs[0].at[4 * x + 2 * y + c]
        copies = [_remote(ins[0], mine, ssem, rsem, k, (x ^ fx, y ^ fy, c ^ fc))
                  for k, (fx, fy, fc) in enumerate(DEVICE_FLIPS)]
        copies.append(functools.partial(pltpu.make_async_copy, ins[0], mine, ssem.at[len(DEVICE_FLIPS)]))
        return [copies]

    return _Rider([pack], [_sds((N_DEV,) + pack.shape, pack.dtype)], len(DEVICE_FLIPS) + 1, phases)


def _small_sum_adamw(recv_a, recv_b, wpack, mpack, vpack):
    R = wpack.shape[0]

    def body(a_ref, b_ref, w_ref, m_ref, v_ref, gs_ref, d_ref, mo_ref, vo_ref):
        ta, tb = a_ref[0], b_ref[0]
        for dev in range(1, N_DEV):
            ta = ta + a_ref[dev]
            tb = tb + b_ref[dev]
        total = jnp.concatenate([ta, tb], axis=0)
        gs_ref[...] = total
        d, mo, vo = _adamw_math(w_ref[...], total, m_ref[...], v_ref[...])
        d_ref[...] = d
        mo_ref[...] = mo
        vo_ref[...] = vo

    return pl.pallas_call(
        body, name="small_sum_adamw", in_specs=[VMEM_SPEC] * 5, out_specs=[VMEM_SPEC] * 4,
        out_shape=[_sds((R, LANES), F32)] * 4,
    )(recv_a, recv_b, wpack, mpack, vpack)


def _rows8(a):
    a = a.reshape(-1, LANES)
    pad = (-a.shape[0]) % 8
    return jnp.pad(a, ((0, pad), (0, 0))) if pad else a


def _pack(parts):
    return jnp.concatenate([_rows8(a) for a in parts], axis=0)


def _unpack(pack, like):
    out, row = [], 0
    for a in like:
        n = a.size // LANES
        out.append(pack[row:row + n].reshape(a.shape))
        row += n + (-n) % 8
    return out


def kernel(x, p, emb_ln_g, emb_ln_b, w_in, attn_out_g, w_pool, pool_scale, w_out, ln1_g, ln1_b, w_up, w_down, ln2_g, ln2_b, w_ple, w_ple_gate, ln3_g, ln3_b, loss_target, m_emb_ln_g, m_emb_ln_b, m_w_in, m_attn_out_g, m_w_pool, m_pool_scale, m_w_out, m_ln1_g, m_ln1_b, m_w_up, m_w_down, m_ln2_g, m_ln2_b, m_w_ple, m_w_ple_gate, m_ln3_g, m_ln3_b, v_emb_ln_g, v_emb_ln_b, v_w_in, v_attn_out_g, v_w_pool, v_pool_scale, v_w_out, v_ln1_g, v_ln1_b, v_w_up, v_w_down, v_ln2_g, v_ln2_b, v_w_ple, v_w_ple_gate, v_ln3_g, v_ln3_b):
    S = x.shape[1]
    tm = min(256, S)
    tq = min(256, S)
    tm_mlp = min(512, S)
    xs = x[0]
    ps = p[0, 0]
    tgt = loss_target[0]
    row = lambda a: a.reshape(1, -1)
    g0, b0 = row(emb_ln_g), row(emb_ln_b)
    g1, b1, g2, b2, g3, b3 = ln1_g, ln1_b, ln2_g, ln2_b, ln3_g, ln3_b
    wp = w_pool[0]

    xi, yi, ci = _place()
    place = jnp.stack([ci, 2 * xi + yi]).astype(jnp.int32)
    names = ["w_in", "w_out", "w_up", "w_down", "w_ple", "w_ple_gate"]

    big = [w_in[0], w_out[0], w_up[0], w_down[0], w_ple[0], w_ple_gate[0]]
    stacked = dict(zip(names, [_cast_into_slot(w, place, "cast_" + n) for w, n in zip(big, names)]))
    (w_in_s,) = _gather_rider([stacked["w_in"]]).run("gather_w_in")

    xh0, rstd0, q, k, v, u, w_out_s, w_ple_s, w_gate_s = _embln_inproj(
        xs, g0, b0, w_in_s, tm, _gather_rider([stacked[n] for n in ("w_out", "w_ple", "w_ple_gate")]))
    o_raw, on, w_up_s, w_down_s = _attn_fwd(
        q, k, v, attn_out_g, tq, _gather_rider([stacked["w_up"], stacked["w_down"]]))
    w_out_f = w_out_s.reshape(D_MODEL, D_MODEL)
    w_down_f = w_down_s.reshape(D_FF, D_MODEL)
    w_gate_f = w_gate_s.reshape(D_MODEL, D_MODEL)
    d_b, pooled = _pool_fwd(u, wp, pool_scale, tm)
    xh1, rstd1, x1b = _mix_ln1(on, pooled, xh0, g0, b0, w_out_f, g1, b1, tm)
    xh2, rstd2 = _mlp_ln2(xh1, x1b, g1, b1, w_up_s, w_down_f, tm_mlp)

    (dpre2, dhb, dw_ple, dw_gate, dg3, db3, dg2, db2, loss_row) = _ple_ln3_loss(
        xh2, rstd2, g2, b2, ps, w_ple_s, w_gate_f, g3, b3, tgt, tm)
    dx1m, da, h1 = _mlp_bwd(x1b, dhb, w_up_s, w_down_f, tm_mlp)
    dw_up = _tn_matmul(x1b, da, "grad_w_up", 1024, min(512, S), stacked=True)
    dw_down = _tn_matmul(h1, dhb, "grad_w_down", 1024, min(512, S), stacked=False)
    def halves_of(g):
        return g.reshape(N_CHIPS, 2, g.shape[1] // 2, g.shape[2])

    early_names = names[2:]
    early = [halves_of(g) for g in (dw_up, dw_down.reshape(N_CHIPS, D_FF // N_CHIPS, D_MODEL), dw_ple,
                                    dw_gate.reshape(N_CHIPS, D_MODEL // N_CHIPS, D_MODEL))]
    dpre1, don, dpooled, dw_out, dg1, db1, *early_pair = _mix_bwd(
        dpre2, dx1m, xh1, rstd1, g1, w_out_f, on, pooled, tm, _pair_swap_rider(early))
    early_sum = [_add_pair(g, r, place, "pair_sum_" + n) for g, r, n in zip(early, early_pair, early_names)]
    out_halves = halves_of(dw_out.reshape(N_CHIPS, D_MODEL // N_CHIPS, D_MODEL))
    du, dwp, dsc, out_pair = _pool_bwd(dpooled, d_b, wp, pool_scale, tm, _pair_swap_rider([out_halves]))
    out_sum = _add_pair(out_halves, out_pair, place, "pair_sum_w_out")
    pack_a = _pack([jnp.broadcast_to(loss_row, (8, LANES)), dwp, dsc, dg1, db1, dg2, db2, dg3, db3])
    riding = _chip_scatter_rider([out_sum[1]] + [b for _, b in early_sum]) + _pack_exchange_rider(pack_a)
    dq, dk, dv, dga, *arrived = _attn_bwd(q, k, v, don, o_raw, attn_out_g, tq, riding)
    early_chips, recv_a = arrived[:-1], arrived[-1]
    grad_x, dw_in, dg0, db0 = _inproj_bwd(dq, dk, dv, du, dpre1, xh0, rstd0, g0, b0, w_in_s, tm)

    in_halves = halves_of(dw_in)
    pack_b = _pack([dg0, db0, dga])
    in_pair, recv_b = (_pair_swap_rider([in_halves]) + _pack_exchange_rider(pack_b)).run("reduce_pair_late")
    in_sum = _add_pair(in_halves, in_pair, place, "pair_sum_w_in")
    (in_chips,) = _chip_scatter_rider([in_sum[1]]).run("reduce_chips_late")
    mine = [_add_chips(s, r, place, "chip_sum_" + n)
            for (s, _), r, n in zip([in_sum, out_sum] + early_sum, [in_chips] + early_chips, names)]
    theirs = _pair_send_rider(mine).run("gather_pair")

    ms = [m_w_in, m_w_out, m_w_up, m_w_down, m_w_ple, m_w_ple_gate]
    vs = [v_w_in, v_w_out, v_w_up, v_w_down, v_w_ple, v_w_ple_gate]
    big_out = {}
    for n, w, a, b, m, vv in zip(names, big, mine, theirs, ms, vs):
        res4 = _adamw(w, a, b, m[0], vv[0], place, "adamw_" + n)
        big_out[n] = tuple(r.reshape(m.shape) for r in res4)

    small_names = ["w_pool", "pool_scale", "ln1_g", "ln1_b", "ln2_g", "ln2_b", "ln3_g", "ln3_b",
                   "emb_ln_g", "emb_ln_b", "attn_out_g"]
    small_w = [w_pool, pool_scale, ln1_g, ln1_b, ln2_g, ln2_b, ln3_g, ln3_b, emb_ln_g, emb_ln_b, attn_out_g]
    small_m = [m_w_pool, m_pool_scale, m_ln1_g, m_ln1_b, m_ln2_g, m_ln2_b, m_ln3_g, m_ln3_b,
               m_emb_ln_g, m_emb_ln_b, m_attn_out_g]
    small_v = [v_w_pool, v_pool_scale, v_ln1_g, v_ln1_b, v_ln2_g, v_ln2_b, v_ln3_g, v_ln3_b,
               v_emb_ln_g, v_emb_ln_b, v_attn_out_g]
    loss_like = jnp.zeros((8, LANES), F32)
    gs, ds, mos, vos = _small_sum_adamw(recv_a, recv_b, _pack([loss_like] + small_w), _pack([loss_like] + small_m),
                                        _pack([jnp.ones((8, LANES), F32)] + small_v))
    like = [loss_like] + small_w
    gs_u, ds_u, mos_u, vos_u = (_unpack(a, like) for a in (gs, ds, mos, vos))
    loss = gs_u[0][0, 0]
    small_out = {n: (gs_u[i + 1], ds_u[i + 1], mos_u[i + 1], vos_u[i + 1]) for i, n in enumerate(small_names)}

    order = ["emb_ln_g", "emb_ln_b", "w_in", "attn_out_g", "w_pool", "pool_scale", "w_out", "ln1_g", "ln1_b",
             "w_up", "w_down", "ln2_g", "ln2_b", "w_ple", "w_ple_gate", "ln3_g", "ln3_b"]
    res = {**big_out, **small_out}
    outs = [loss, grad_x.reshape(x.shape)]
    for kind in range(4):
        outs += [res[n][kind] for n in order]
    return tuple(outs)
```

```python
import functools

import jax
import jax.numpy as jnp
from jax import lax
from jax.experimental import pallas as pl
from jax.experimental.pallas import tpu as pltpu

F32 = jnp.float32
BF16 = jnp.bfloat16

D_MODEL = 1024
ATTN_WIDTH = 512
POOL_WIDTH = 512
HEAD_DIM = 64
PAIR = 2 * HEAD_DIM
N_PAIRS = ATTN_WIDTH // PAIR
N_POOL_GROUPS = 4
POOL_GROUP = 128
POOL_HALO = 16
D_FF = 4096
PLE_DIM = 256
N_CHIPS = 4
N_DEV = 8
LN_EPS = 1e-5
RMS_EPS = 1e-6
ALPHA = float(2.0 ** 0.25)
Q_SCALE = 0.125
ADAM_LR = 0.001
ADAM_B1 = 0.9
ADAM_B2 = 0.999
ADAM_EPS = 1e-08
ADAM_WD = 0.01
ADAM_STEP = 10
LANES = 128
MIB = 1024 * 1024

MESH = pl.DeviceIdType.MESH
HBM_SPEC = pl.BlockSpec(memory_space=pltpu.HBM)
VMEM_SPEC = pl.BlockSpec(memory_space=pltpu.VMEM)


def _cp(vmem_mib):
    return pltpu.CompilerParams(vmem_limit_bytes=vmem_mib * MIB)


def _dot(a, b):
    return jnp.dot(a, b, preferred_element_type=F32)


def _dot_nt(a, b):
    return lax.dot_general(a, b, (((1,), (1,)), ((), ())), preferred_element_type=F32)


def _dot_tn(a, b):
    return lax.dot_general(a, b, (((0,), (0,)), ((), ())), preferred_element_type=F32)


def _ln_fwd(pre):
    mu = jnp.mean(pre, axis=-1, keepdims=True)
    xc = pre - mu
    var = jnp.mean(xc * xc, axis=-1, keepdims=True)
    rstd = lax.rsqrt(var + LN_EPS)
    return xc * rstd, rstd


def _ln_bwd(dy, xh, rstd, g):
    dxh = dy * g
    m1 = jnp.mean(dxh, axis=-1, keepdims=True)
    m2 = jnp.mean(dxh * xh, axis=-1, keepdims=True)
    return rstd * (dxh - m1 - xh * m2)


def _colsum(a):
    return jnp.sum(a, axis=0, keepdims=True)


def _neg_softplus(z):
    return -(jnp.maximum(z, 0.0) + jnp.log(1.0 + jnp.exp(-jnp.abs(z))))


def _split_bf16(a):
    hi = a.astype(BF16)
    lo = (a - hi.astype(F32)).astype(BF16)
    return hi, lo


def _row_spec(tm, n):
    return pl.BlockSpec((tm, n), lambda i: (i, 0))


def _const_spec(shape):
    nd = len(shape)
    return pl.BlockSpec(shape, lambda *_: (0,) * nd)


def _hbm(*arrays):
    return [pltpu.with_memory_space_constraint(a, pltpu.HBM) for a in arrays]


def _sds(shape, dtype):
    return pltpu.HBM(shape, dtype)


def _embln_inproj(x, g0, b0, w_in_s, tm, rider):
    S, D = x.shape
    n_t = S // tm

    def body(*refs):
        ((x_ref, g_ref, b_ref, w_ref), (xh_ref, rstd_ref, q_ref, k_ref, v_ref, u_ref), _,
         ride) = rider.split(refs, 4, 6, 0)
        i = pl.program_id(0)

        @pl.when(i == 0)
        def _():
            rider.first(ride)

        @pl.when(i == (3 * n_t) // 4)
        def _():
            rider.mid(ride)

        xh, rstd = _ln_fwd(x_ref[...])
        xh_ref[...] = xh
        rstd_ref[...] = rstd
        xb = (xh * g_ref[...] + b_ref[...]).astype(BF16)
        q_ref[...] = (_dot(xb, w_ref[0]) * Q_SCALE).astype(BF16)
        k_ref[...] = _dot(xb, w_ref[1]).astype(BF16)
        v_ref[...] = _dot(xb, w_ref[2]).astype(BF16)
        u_ref[...] = _dot(xb, w_ref[3])

        @pl.when(i == n_t - 1)
        def _():
            rider.last(ride)

    return rider.call(
        body, [x, g0, b0, w_in_s], grid=(n_t,), name="embln_inproj",
        in_specs=[_row_spec(tm, D), _const_spec((1, D)), _const_spec((1, D)),
                  _const_spec((N_CHIPS, D, 512))],
        out_specs=[_row_spec(tm, D), _row_spec(tm, 1), _row_spec(tm, 512), _row_spec(tm, 512),
                   _row_spec(tm, 512), _row_spec(tm, 512)],
        out_shape=[_sds((S, D), F32), _sds((S, 1), F32), _sds((S, 512), BF16), _sds((S, 512), BF16),
                   _sds((S, 512), BF16), _sds((S, 512), F32)],
        scratch_shapes=[], vmem_mib=40)


def _tri(n, upper):
    r = lax.broadcasted_iota(jnp.int32, (n, n), 0)
    c = lax.broadcasted_iota(jnp.int32, (n, n), 1)
    keep = (r < c) if upper else (r > c)
    return jnp.where(keep, 1.0, 0.0).astype(BF16)


def _strictly_causal(n):
    return lax.broadcasted_iota(jnp.int32, (n, n), 1) < lax.broadcasted_iota(jnp.int32, (n, n), 0)


LOG_WEIGHT_FLOOR = -110.0


def _sb_tile(qhs, kt, low, c_ls, valid):
    valids = valid if isinstance(valid, (list, tuple)) else [valid] * len(qhs)
    zs = [_dot_nt(qh, kt) for qh in qhs]
    lrs = [_neg_softplus(z) for z in zs]
    ls_ = [lr if m is None else jnp.where(m, lr, 0.0) for lr, m in zip(lrs, valids)]
    parts = [_split_bf16(l) for l in ls_]
    sfx = [_dot(hi, low) + _dot(lo, low) + c_l for (hi, lo), c_l in zip(parts, c_ls)]
    lss = [z + lr for z, lr in zip(zs, lrs)]
    ws = [jnp.exp(ls + s) for ls, s in zip(lss, sfx)]
    ws = [w if m is None else jnp.where(m, w, 0.0) for w, m in zip(ws, valids)]
    return lss, ls_, ws


def _attn_fwd(q, k, v, ga, tq, rider):
    S = q.shape[0]
    nq = S // tq

    def body(*refs):
        (q_ref, k_ref, v_ref, ga_ref), (o_ref, on_ref), _, ride = rider.split(refs, 4, 2, 0)
        p, i = pl.program_id(0), pl.program_id(1)

        @pl.when(jnp.logical_and(p == 0, i == 0))
        def _():
            rider.first(ride)

        @pl.when(jnp.logical_and(p == N_PAIRS - 1, i == 0))
        def _():
            rider.mid(ride)

        lane = lax.broadcasted_iota(jnp.int32, (1, PAIR), 1)
        m0 = lane < HEAD_DIM
        low = _tri(tq, upper=False)
        q2 = q_ref[...]
        qhs = [jnp.where(m0, q2, jnp.zeros_like(q2)), jnp.where(m0, jnp.zeros_like(q2), q2)]

        def tile(kb, c_ls, accs, valid):
            ks = pl.multiple_of(kb * tq, tq)
            kt = k_ref[pl.ds(ks, tq), :]
            vt = v_ref[pl.ds(ks, tq), :]
            _, ls_, ws = _sb_tile(qhs, kt, low, c_ls, valid)
            new_a = [acc + _dot(w.astype(BF16), vt) for acc, w in zip(accs, ws)]
            new_c = [c_l + jnp.sum(l, axis=1, keepdims=True) for c_l, l in zip(c_ls, ls_)]
            return new_c, new_a

        zc, za = jnp.zeros((tq, 1), F32), jnp.zeros((tq, PAIR), F32)

        def first_two():
            c_ls, accs = tile(i, [zc, zc], [za, za], _strictly_causal(tq))
            c_ls, accs = tile(i - 1, c_ls, accs, None)
            return (*c_ls, *accs)

        def first_one():
            c_ls, accs = tile(i, [zc, zc], [za, za], _strictly_causal(tq))
            return (*c_ls, *accs)

        st0 = lax.cond(i >= 1, first_two, first_one)

        def more(st):
            return jnp.logical_and(st[0] <= i, jnp.max(jnp.maximum(st[1], st[2])) > LOG_WEIGHT_FLOOR)

        def step(st):
            n, c0, c1, a0, a1 = st
            c_ls, accs = tile(i - n, [c0, c1], [a0, a1], None)
            return (n + 1, c_ls[0], c_ls[1], accs[0], accs[1])

        st = lax.while_loop(more, step, (jnp.int32(2), *st0))
        o = jnp.where(m0, st[3], st[4])
        o_ref[...] = o
        sq = o * o
        ms0 = jnp.sum(jnp.where(m0, sq, 0.0), axis=-1, keepdims=True) * (1.0 / HEAD_DIM)
        ms1 = jnp.sum(jnp.where(m0, 0.0, sq), axis=-1, keepdims=True) * (1.0 / HEAD_DIM)
        rs = jnp.where(m0, lax.rsqrt(ms0 + RMS_EPS), lax.rsqrt(ms1 + RMS_EPS))
        on_ref[...] = (o * rs * ga_ref[...]).astype(BF16)

        @pl.when(jnp.logical_and(p == N_PAIRS - 1, i == nq - 1))
        def _():
            rider.last(ride)

    return rider.call(
        body, [q, k, v, ga], grid=(N_PAIRS, nq), name="attn_fwd",
        in_specs=[pl.BlockSpec((tq, PAIR), lambda p, i: (i, p)),
                  pl.BlockSpec((S, PAIR), lambda p, i: (0, p)),
                  pl.BlockSpec((S, PAIR), lambda p, i: (0, p)),
                  pl.BlockSpec((1, PAIR), lambda p, i: (0, p))],
        out_specs=[pl.BlockSpec((tq, PAIR), lambda p, i: (i, p)),
                   pl.BlockSpec((tq, PAIR), lambda p, i: (i, p))],
        out_shape=[_sds((S, ATTN_WIDTH), F32), _sds((S, ATTN_WIDTH), BF16)],
        scratch_shapes=[], vmem_mib=40)


def _pool_fwd(u, w_pool, pscale, tm):
    S = u.shape[0]
    hb = tm // POOL_HALO

    def body(u_ref, uh_ref, wp_ref, sc_ref, d_ref, pooled_ref):
        i = pl.program_id(0)
        halo = jnp.where(i > 0, uh_ref[...], 0.0)
        pos = i * tm + lax.broadcasted_iota(jnp.int32, (tm, 1), 0)
        for g in range(N_POOL_GROUPS):
            win = 2 ** (g + 1)
            cols = slice(g * POOL_GROUP, (g + 1) * POOL_GROUP)
            ut = u_ref[:, cols]
            s = jnp.concatenate([halo[:, cols], ut], axis=0)
            for sh in (1, 2, 4, 8)[:g + 1]:
                s = s + pltpu.roll(s, sh, 0)
            cnt = jnp.minimum(pos + 1, win).astype(F32)
            db = (s[POOL_HALO:, :] / cnt - ut).astype(BF16)
            y = _dot(db, wp_ref[g].astype(BF16))
            d_ref[:, cols] = db
            pooled_ref[:, cols] = (y * sc_ref[:, cols]).astype(BF16)

    return pl.pallas_call(
        body, grid=(S // tm,), name="pool_fwd",
        in_specs=[_row_spec(tm, POOL_WIDTH),
                  pl.BlockSpec((POOL_HALO, POOL_WIDTH), lambda i: (jnp.maximum(i * hb - 1, 0), 0)),
                  _const_spec((N_POOL_GROUPS, POOL_GROUP, POOL_GROUP)), _const_spec((1, POOL_WIDTH))],
        out_specs=[_row_spec(tm, POOL_WIDTH), _row_spec(tm, POOL_WIDTH)],
        out_shape=[_sds((S, POOL_WIDTH), BF16), _sds((S, POOL_WIDTH), BF16)],
        compiler_params=_cp(32),
    )(*_hbm(u, u, w_pool, pscale))


def _mix_ln1(on, pooled, xh0, g0, b0, w_out, g1, b1, tm):
    S, D = xh0.shape

    def body(on_ref, po_ref, xh0_ref, g0_ref, b0_ref, w_ref, g1_ref, b1_ref, xh_ref, rstd_ref, xb_ref):
        mixed = _dot(on_ref[...], w_ref[:ATTN_WIDTH, :]) + _dot(po_ref[...], w_ref[ATTN_WIDTH:, :])
        x0 = xh0_ref[...] * g0_ref[...] + b0_ref[...]
        xh, rstd = _ln_fwd(ALPHA * x0 + mixed)
        xh_ref[...] = xh
        rstd_ref[...] = rstd
        xb_ref[...] = (xh * g1_ref[...] + b1_ref[...]).astype(BF16)

    return pl.pallas_call(
        body, grid=(S // tm,), name="mix_ln1",
        in_specs=[_row_spec(tm, ATTN_WIDTH), _row_spec(tm, POOL_WIDTH), _row_spec(tm, D),
                  _const_spec((1, D)), _const_spec((1, D)), _const_spec((D, D)),
                  _const_spec((1, D)), _const_spec((1, D))],
        out_specs=[_row_spec(tm, D), _row_spec(tm, 1), _row_spec(tm, D)],
        out_shape=[_sds((S, D), F32), _sds((S, 1), F32), _sds((S, D), BF16)],
        compiler_params=_cp(40),
    )(*_hbm(on, pooled, xh0, g0, b0, w_out, g1, b1))


def _mlp_ln2(xh1, x1b, g1, b1, w_up_s, w_down, tm):
    S, D = xh1.shape
    fc = D_FF // N_CHIPS

    def body(xh_ref, xb_ref, g_ref, b_ref, wu_ref, wd_ref, xh2_ref, rstd_ref, acc_ref):
        j = pl.program_id(1)

        @pl.when(j == 0)
        def _():
            acc_ref[...] = jnp.zeros_like(acc_ref)

        a = _dot(xb_ref[...], wu_ref[0])
        r = jnp.maximum(a, 0.0)
        acc_ref[...] += _dot((r * r).astype(BF16), wd_ref[...])

        @pl.when(j == N_CHIPS - 1)
        def _():
            x1 = xh_ref[...] * g_ref[...] + b_ref[...]
            xh, rstd = _ln_fwd(ALPHA * x1 + acc_ref[...])
            xh2_ref[...] = xh
            rstd_ref[...] = rstd

    return pl.pallas_call(
        body, grid=(S // tm, N_CHIPS), name="mlp_ln2",
        in_specs=[pl.BlockSpec((tm, D), lambda i, j: (i, 0)), pl.BlockSpec((tm, D), lambda i, j: (i, 0)),
                  pl.BlockSpec((1, D), lambda i, j: (0, 0)), pl.BlockSpec((1, D), lambda i, j: (0, 0)),
                  pl.BlockSpec((1, D, fc), lambda i, j: (j, 0, 0)),
                  pl.BlockSpec((fc, D), lambda i, j: (j, 0))],
        out_specs=[pl.BlockSpec((tm, D), lambda i, j: (i, 0)), pl.BlockSpec((tm, 1), lambda i, j: (i, 0))],
        out_shape=[_sds((S, D), F32), _sds((S, 1), F32)],
        scratch_shapes=[pltpu.VMEM((tm, D), F32)],
        compiler_params=_cp(40),
    )(*_hbm(xh1, x1b, g1, b1, w_up_s, w_down))


def _ple_ln3_loss(xh2, rstd2, g2, b2, p, w_ple_s, w_gate, g3, b3, target, tm):
    S, D = xh2.shape
    pc = D // N_CHIPS

    def body(xh2_ref, rstd2_ref, g2_ref, b2_ref, p_ref, wp_ref, wg_ref, g3_ref, b3_ref, t_ref,
             dpre2_ref, dhb_ref, dwp_ref, dwg_ref, dg3_ref, db3_ref, dg2_ref, db2_ref, loss_ref):
        i = pl.program_id(0)

        @pl.when(i == 0)
        def _():
            for r in (dwp_ref, dwg_ref, dg3_ref, db3_ref, dg2_ref, db2_ref, loss_ref):
                r[...] = jnp.zeros_like(r)

        xh2 = xh2_ref[...]
        x2 = xh2 * g2_ref[...] + b2_ref[...]
        x2b = x2.astype(BF16)
        gate = 1.0 / (1.0 + jnp.exp(-_dot(x2b, wg_ref[...])))
        pb = p_ref[...].astype(BF16)
        pe = jnp.concatenate([_dot(pb, wp_ref[c]) for c in range(N_CHIPS)], axis=1)
        xh3, rstd3 = _ln_fwd(ALPHA * x2 + pe * gate)
        diff = xh3 * g3_ref[...] + b3_ref[...] - t_ref[...]
        loss_ref[...] += (0.5 / D) * jnp.sum(diff * diff)
        dy = diff * (1.0 / D)
        dg3_ref[...] += _colsum(dy * xh3)
        db3_ref[...] += _colsum(dy)
        dpre3 = _ln_bwd(dy, xh3, rstd3, g3_ref[...])
        dpe_b = (dpre3 * gate).astype(BF16)
        dgp_b = (dpre3 * pe * gate * (1.0 - gate)).astype(BF16)
        dx2 = ALPHA * dpre3 + _dot_nt(dgp_b, wg_ref[...])
        dwg_ref[...] += _dot_tn(x2b, dgp_b)
        for c in range(N_CHIPS):
            dwp_ref[c] += _dot_tn(pb, dpe_b[:, c * pc:(c + 1) * pc])
        dg2_ref[...] += _colsum(dx2 * xh2)
        db2_ref[...] += _colsum(dx2)
        dpre2 = _ln_bwd(dx2, xh2, rstd2_ref[...], g2_ref[...])
        dpre2_ref[...] = dpre2
        dhb_ref[...] = dpre2.astype(BF16)

    vec = _const_spec((1, D))
    return pl.pallas_call(
        body, grid=(S // tm,), name="ple_ln3_loss",
        in_specs=[_row_spec(tm, D), _row_spec(tm, 1), vec, vec, _row_spec(tm, PLE_DIM),
                  _const_spec((N_CHIPS, PLE_DIM, pc)), _const_spec((D, D)), vec, vec, _row_spec(tm, D)],
        out_specs=[_row_spec(tm, D), _row_spec(tm, D), _const_spec((N_CHIPS, PLE_DIM, pc)),
                   _const_spec((D, D)), vec, vec, vec, vec, _const_spec((1, LANES))],
        out_shape=[_sds((S, D), F32), _sds((S, D), BF16), _sds((N_CHIPS, PLE_DIM, pc), F32),
                   _sds((D, D), F32), _sds((1, D), F32), _sds((1, D), F32), _sds((1, D), F32),
                   _sds((1, D), F32), _sds((1, LANES), F32)],
        compiler_params=_cp(48),
    )(*_hbm(xh2, rstd2, g2, b2, p, w_ple_s, w_gate, g3, b3, target))


def _mlp_bwd(x1b, dhb, w_up_s, w_down, tm):
    S, D = x1b.shape
    fc = D_FF // N_CHIPS

    def body(xb_ref, dh_ref, wu_ref, wd_ref, dx_ref, da_ref, h1_ref):
        j = pl.program_id(1)

        @pl.when(j == 0)
        def _():
            dx_ref[...] = jnp.zeros_like(dx_ref)

        a = _dot(xb_ref[...], wu_ref[0])
        r = jnp.maximum(a, 0.0)
        h1_ref[...] = (r * r).astype(BF16)
        da = (_dot_nt(dh_ref[...], wd_ref[...]) * (2.0 * r)).astype(BF16)
        da_ref[...] = da
        dx_ref[...] += _dot_nt(da, wu_ref[0])

    return pl.pallas_call(
        body, grid=(S // tm, N_CHIPS), name="mlp_bwd",
        in_specs=[pl.BlockSpec((tm, D), lambda i, j: (i, 0)), pl.BlockSpec((tm, D), lambda i, j: (i, 0)),
                  pl.BlockSpec((1, D, fc), lambda i, j: (j, 0, 0)),
                  pl.BlockSpec((fc, D), lambda i, j: (j, 0))],
        out_specs=[pl.BlockSpec((tm, D), lambda i, j: (i, 0)), pl.BlockSpec((tm, fc), lambda i, j: (i, j)),
                   pl.BlockSpec((tm, fc), lambda i, j: (i, j))],
        out_shape=[_sds((S, D), F32), _sds((S, D_FF), BF16), _sds((S, D_FF), BF16)],
        compiler_params=_cp(40),
    )(*_hbm(x1b, dhb, w_up_s, w_down))


def _tn_matmul(a, b, name, tk, tt, stacked):
    T, K = a.shape
    N = b.shape[1]
    tn = 1024

    def body(a_ref, b_ref, o_ref):
        @pl.when(pl.program_id(2) == 0)
        def _():
            o_ref[...] = jnp.zeros_like(o_ref)

        prod = _dot_tn(a_ref[...], b_ref[...])
        if stacked:
            o_ref[0] += prod
        else:
            o_ref[...] += prod

    if stacked:
        out_spec = pl.BlockSpec((1, tk, tn), lambda k, n, t: (n, k, 0))
        out_shape = _sds((N // tn, K, tn), F32)
    else:
        out_spec = pl.BlockSpec((tk, tn), lambda k, n, t: (k, n))
        out_shape = _sds((K, N), F32)
    return pl.pallas_call(
        body, grid=(K // tk, N // tn, T // tt), name=name,
        in_specs=[pl.BlockSpec((tt, tk), lambda k, n, t: (t, k)),
                  pl.BlockSpec((tt, tn), lambda k, n, t: (t, n))],
        out_specs=out_spec, out_shape=out_shape,
        compiler_params=_cp(40),
    )(*_hbm(a, b))


def _mix_bwd(dpre2, dx1m, xh1, rstd1, g1, w_out, on, pooled, tm, rider):
    S, D = xh1.shape
    n_t = S // tm

    def body(*refs):
        ((dp2_ref, dxm_ref, xh_ref, rstd_ref, g_ref, w_ref, on_ref, po_ref),
         (dpre1_ref, don_ref, dpo_ref, dw_ref, dg_ref, db_ref), _, ride) = rider.split(refs, 8, 6, 0)

        @pl.when(pl.program_id(0) == 0)
        def _():
            rider.first(ride)
            for r in (dw_ref, dg_ref, db_ref):
                r[...] = jnp.zeros_like(r)

        xh = xh_ref[...]
        dx1 = ALPHA * dp2_ref[...] + dxm_ref[...]
        dg_ref[...] += _colsum(dx1 * xh)
        db_ref[...] += _colsum(dx1)
        dpre1 = _ln_bwd(dx1, xh, rstd_ref[...], g_ref[...])
        dpre1_ref[...] = dpre1
        dmb = dpre1.astype(BF16)
        dcat = _dot_nt(dmb, w_ref[...])
        don_ref[...] = dcat[:, :ATTN_WIDTH]
        dpo_ref[...] = dcat[:, ATTN_WIDTH:]
        dw_ref[:ATTN_WIDTH, :] += _dot_tn(on_ref[...], dmb)
        dw_ref[ATTN_WIDTH:, :] += _dot_tn(po_ref[...], dmb)

        @pl.when(pl.program_id(0) == n_t - 1)
        def _():
            rider.last(ride)

    vec = _const_spec((1, D))
    return rider.call(
        body, [dpre2, dx1m, xh1, rstd1, g1, w_out, on, pooled], grid=(n_t,), name="mix_bwd",
        in_specs=[_row_spec(tm, D), _row_spec(tm, D), _row_spec(tm, D), _row_spec(tm, 1), vec,
                  _const_spec((D, D)), _row_spec(tm, ATTN_WIDTH), _row_spec(tm, POOL_WIDTH)],
        out_specs=[_row_spec(tm, D), _row_spec(tm, ATTN_WIDTH), _row_spec(tm, POOL_WIDTH),
                   _const_spec((D, D)), vec, vec],
        out_shape=[_sds((S, D), F32), _sds((S, ATTN_WIDTH), F32), _sds((S, POOL_WIDTH), F32),
                   _sds((D, D), F32), _sds((1, D), F32), _sds((1, D), F32)],
        scratch_shapes=[], vmem_mib=48)


def _pool_bwd(dpooled, d_b, w_pool, pscale, tm, rider):
    S = dpooled.shape[0]
    hb = tm // POOL_HALO
    n_t = S // tm
    te = tm + POOL_HALO

    def body(*refs):
        ((dp_ref, dph_ref, d_ref, wp_ref, sc_ref), (du_ref, dwp_ref, dsc_ref), _,
         ride) = rider.split(refs, 5, 3, 0)
        i = pl.program_id(0)

        @pl.when(i == 0)
        def _():
            rider.first(ride)
            dwp_ref[...] = jnp.zeros_like(dwp_ref)
            dsc_ref[...] = jnp.zeros_like(dsc_ref)

        halo = jnp.where(i < n_t - 1, dph_ref[...], 0.0)
        pos = i * tm + lax.broadcasted_iota(jnp.int32, (te, 1), 0)
        for g in range(N_POOL_GROUPS):
            win = 2 ** (g + 1)
            cols = slice(g * POOL_GROUP, (g + 1) * POOL_GROUP)
            wpb = wp_ref[g].astype(BF16)
            dpt = dp_ref[:, cols]
            dpe = jnp.concatenate([dpt, halo[:, cols]], axis=0)
            dyb = (dpe * sc_ref[:, cols]).astype(BF16)
            dd = _dot_nt(dyb, wpb)
            s = dd / jnp.minimum(pos + 1, win).astype(F32)
            for sh in (1, 2, 4, 8)[:g + 1]:
                s = s + pltpu.roll(s, te - sh, 0)
            du_ref[:, cols] = s[:tm, :] - dd[:tm, :]
            db = d_ref[:, cols]
            dwp_ref[g] += _dot_tn(db, dyb[:tm, :])
            dsc_ref[:, cols] += _colsum(dpt * _dot(db, wpb))

        @pl.when(i == n_t - 1)
        def _():
            rider.last(ride)

    return rider.call(
        body, [dpooled, dpooled, d_b, w_pool, pscale], grid=(n_t,), name="pool_bwd",
        in_specs=[_row_spec(tm, POOL_WIDTH),
                  pl.BlockSpec((POOL_HALO, POOL_WIDTH),
                               lambda i: (jnp.minimum((i + 1) * hb, S // POOL_HALO - 1), 0)),
                  _row_spec(tm, POOL_WIDTH),
                  _const_spec((N_POOL_GROUPS, POOL_GROUP, POOL_GROUP)), _const_spec((1, POOL_WIDTH))],
        out_specs=[_row_spec(tm, POOL_WIDTH), _const_spec((N_POOL_GROUPS, POOL_GROUP, POOL_GROUP)),
                   _const_spec((1, POOL_WIDTH))],
        out_shape=[_sds((S, POOL_WIDTH), F32), _sds((N_POOL_GROUPS, POOL_GROUP, POOL_GROUP), F32),
                   _sds((1, POOL_WIDTH), F32)],
        scratch_shapes=[], vmem_mib=32)


def _attn_bwd(q, k, v, don, o_raw, ga, tq, rider):
    S = q.shape[0]
    nq = S // tq

    def body(*refs):
        ((q_ref, k_ref, v_ref, don_ref, o_ref, ga_ref), (dq_ref, dk_ref, dv_ref, dga_ref),
         (g_s, b_s), ride) = rider.split(refs, 6, 4, 2)
        p, i = pl.program_id(0), pl.program_id(1)

        @pl.when(jnp.logical_and(p == 0, i == 0))
        def _():
            rider.first(ride)

        @pl.when(i == 0)
        def _():
            for r in (dk_ref, dv_ref, dga_ref):
                r[...] = jnp.zeros_like(r)

        lane = lax.broadcasted_iota(jnp.int32, (1, PAIR), 1)
        m0 = lane < HEAD_DIM
        low = _tri(tq, upper=False)
        upp = _tri(tq, upper=True)

        def seg_mean(a):
            s0 = jnp.sum(jnp.where(m0, a, 0.0), axis=-1, keepdims=True)
            s1 = jnp.sum(jnp.where(m0, 0.0, a), axis=-1, keepdims=True)
            return jnp.where(m0, s0, s1) * (1.0 / HEAD_DIM)

        o = o_ref[...]
        rs = lax.rsqrt(seg_mean(o * o) + RMS_EPS)
        oh = o * rs
        don = don_ref[...]
        dga_ref[...] += _colsum(don * oh)
        doh = don * ga_ref[...]
        do = rs * (doh - oh * seg_mean(doh * oh))
        dob = do.astype(BF16)
        q2 = q_ref[...]
        qhs = [jnp.where(m0, q2, jnp.zeros_like(q2)), jnp.where(m0, jnp.zeros_like(q2), q2)]
        dhs = [jnp.where(m0, dob, jnp.zeros_like(dob)), jnp.where(m0, jnp.zeros_like(dob), dob)]
        causal = _strictly_causal(tq)

        def down(kb, c_ls, valid):
            ks = pl.multiple_of(kb * tq, tq)
            kt = k_ref[pl.ds(ks, tq), :]
            vt = v_ref[pl.ds(ks, tq), :]
            lss, ls_, ws = _sb_tile(qhs, kt, low, c_ls, valid)
            dws = [_dot_nt(dh, vt) for dh in dhs]
            for hh in range(2):
                g_s[hh, kb] = dws[hh] * ws[hh]
                b_s[hh, kb] = jnp.exp(lss[hh])
            dv_ref[pl.ds(ks, tq), :] += (_dot_tn(ws[0].astype(BF16), dhs[0])
                                         + _dot_tn(ws[1].astype(BF16), dhs[1]))
            return [c_l + jnp.sum(l, axis=1, keepdims=True) for c_l, l in zip(c_ls, ls_)]

        zc, za = jnp.zeros((tq, 1), F32), jnp.zeros((tq, PAIR), F32)
        c_ls = lax.cond(i >= 1, lambda: tuple(down(i - 1, down(i, [zc, zc], causal), None)),
                        lambda: tuple(down(i, [zc, zc], causal)))

        def more(st):
            return jnp.logical_and(st[0] <= i, jnp.max(jnp.maximum(st[1], st[2])) > LOG_WEIGHT_FLOOR)

        def down_step(st):
            c_ls = down(i - st[0], [st[1], st[2]], None)
            return (st[0] + 1, c_ls[0], c_ls[1])

        n_tiles = lax.while_loop(more, down_step, (jnp.int32(2), c_ls[0], c_ls[1]))[0]

        def up(kb, c_gs, accs, valid):
            ks = pl.multiple_of(kb * tq, tq)
            kt = k_ref[pl.ds(ks, tq), :]
            gs = [g_s[hh, kb] for hh in range(2)]
            parts = [_split_bf16(g) for g in gs]
            pres = [_dot(hi, upp) + _dot(lo, upp) + c_g for (hi, lo), c_g in zip(parts, c_gs)]
            dzs = []
            for hh in range(2):
                beta = b_s[hh, kb]
                dz = gs[hh] * (1.0 - beta) - beta * pres[hh]
                if valid is not None:
                    dz = jnp.where(valid, dz, 0.0)
                dzs.append(dz.astype(BF16))
            new_a = [acc + _dot(dzb, kt) for acc, dzb in zip(accs, dzs)]
            dk_ref[pl.ds(ks, tq), :] += _dot_tn(dzs[0], qhs[0]) + _dot_tn(dzs[1], qhs[1])
            new_c = [c_g + jnp.sum(g, axis=1, keepdims=True) for c_g, g in zip(c_gs, gs)]
            return new_c, new_a

        def up_step(kb, st):
            c_gs, accs = up(kb, [st[0], st[1]], [st[2], st[3]], None)
            return (c_gs[0], c_gs[1], accs[0], accs[1])

        st = lax.fori_loop(i - n_tiles + 1, i - 1, up_step, (zc, zc, za, za))

        def last_two():
            c_gs, accs = up(i - 1, [st[0], st[1]], [st[2], st[3]], None)
            return tuple(up(i, c_gs, accs, causal)[1])

        accs = lax.cond(i >= 1, last_two, lambda: tuple(up(i, [zc, zc], [za, za], causal)[1]))
        dq_ref[...] = jnp.where(m0, accs[0], accs[1]) * Q_SCALE

        @pl.when(jnp.logical_and(p == N_PAIRS - 1, i == nq - 1))
        def _():
            rider.last(ride)

    return rider.call(
        body, [q, k, v, don, o_raw, ga], grid=(N_PAIRS, nq), name="attn_bwd",
        in_specs=[pl.BlockSpec((tq, PAIR), lambda p, i: (i, p)),
                  pl.BlockSpec((S, PAIR), lambda p, i: (0, p)),
                  pl.BlockSpec((S, PAIR), lambda p, i: (0, p)),
                  pl.BlockSpec((tq, PAIR), lambda p, i: (i, p)),
                  pl.BlockSpec((tq, PAIR), lambda p, i: (i, p)),
                  pl.BlockSpec((1, PAIR), lambda p, i: (0, p))],
        out_specs=[pl.BlockSpec((tq, PAIR), lambda p, i: (i, p)),
                   pl.BlockSpec((S, PAIR), lambda p, i: (0, p)),
                   pl.BlockSpec((S, PAIR), lambda p, i: (0, p)),
                   pl.BlockSpec((1, PAIR), lambda p, i: (0, p))],
        out_shape=[_sds((S, ATTN_WIDTH), F32), _sds((S, ATTN_WIDTH), F32), _sds((S, ATTN_WIDTH), F32),
                   _sds((1, ATTN_WIDTH), F32)],
        scratch_shapes=[pltpu.VMEM((2, nq, tq, tq), F32), pltpu.VMEM((2, nq, tq, tq), F32)],
        vmem_mib=56)


def _inproj_bwd(dq, dk, dv, du, dpre1, xh0, rstd0, g0, b0, w_in_s, tm):
    S, D = xh0.shape

    def body(dq_ref, dk_ref, dv_ref, du_ref, dp1_ref, xh_ref, rstd_ref, g_ref, b_ref, w_ref,
             gx_ref, dw_ref, dg_ref, db_ref):
        @pl.when(pl.program_id(0) == 0)
        def _():
            for r in (dw_ref, dg_ref, db_ref):
                r[...] = jnp.zeros_like(r)

        xh = xh_ref[...]
        xb = (xh * g_ref[...] + b_ref[...]).astype(BF16)
        dx0 = ALPHA * dp1_ref[...]
        for c, r in enumerate((dq_ref, dk_ref, dv_ref, du_ref)):
            dpb = r[...].astype(BF16)
            dx0 = dx0 + _dot_nt(dpb, w_ref[c])
            dw_ref[c] += _dot_tn(xb, dpb)
        dg_ref[...] += _colsum(dx0 * xh)
        db_ref[...] += _colsum(dx0)
        gx_ref[...] = _ln_bwd(dx0, xh, rstd_ref[...], g_ref[...])

    vec = _const_spec((1, D))
    half = _row_spec(tm, 512)
    return pl.pallas_call(
        body, grid=(S // tm,), name="inproj_bwd",
        in_specs=[half, half, half, half, _row_spec(tm, D), _row_spec(tm, D), _row_spec(tm, 1), vec, vec,
                  _const_spec((N_CHIPS, D, 512))],
        out_specs=[_row_spec(tm, D), _const_spec((N_CHIPS, D, 512)), vec, vec],
        out_shape=[_sds((S, D), F32), _sds((N_CHIPS, D, 512), F32), _sds((1, D), F32), _sds((1, D), F32)],
        compiler_params=_cp(56),
    )(*_hbm(dq, dk, dv, du, dpre1, xh0, rstd0, g0, b0, w_in_s))


def _place():
    return lax.axis_index("x"), lax.axis_index("y"), lax.axis_index("c")


CHIP_FLIPS = ((0, 1), (1, 0), (1, 1))


class _Rider:
    def __init__(self, ins, out_shapes, n_sem, phases, aliases=None):
        self.ins, self.out_shapes, self.n_sem, self.phases = list(ins), list(out_shapes), n_sem, phases
        self.aliases = aliases or {}

    def __add__(self, other):
        na, ma = len(self.ins), len(self.out_shapes)

        def phases(ins, outs, ssem, rsem):
            mine = self.phases(ins[:na], outs[:ma], ssem, rsem)
            rest = pl.ds(self.n_sem, other.n_sem)
            theirs = other.phases(ins[na:], outs[ma:], ssem.at[rest], rsem.at[rest])
            assert len(mine) == 1 and len(theirs) == 1
            return [mine[0] + theirs[0]]

        return _Rider(self.ins + other.ins, self.out_shapes + other.out_shapes, self.n_sem + other.n_sem, phases)

    def split(self, refs, n_in, n_out, n_scratch):
        a = n_in + len(self.ins)
        b = a + n_out
        c = b + len(self.out_shapes)
        own = (refs[:n_in], refs[a:b], refs[c:c + n_scratch])
        return own + ((refs[n_in:a], refs[b:c]) + tuple(refs[c + n_scratch:]),)

    def first(self, ride):
        for make in self.phases(*ride)[0]:
            make().start()

    def mid(self, ride):
        ph = self.phases(*ride)
        if len(ph) == 2:
            for make in ph[0]:
                make().wait_recv()
            for make in ph[1]:
                make().start()

    def last(self, ride):
        ph = self.phases(*ride)
        if len(ph) == 2:
            for make in ph[0]:
                make().wait_send()
        for make in ph[-1]:
            make().wait()

    def call(self, body, args, *, grid, name, in_specs, out_specs, out_shape, scratch_shapes, vmem_mib):
        n_in, n_out = len(in_specs), len(out_specs)
        sems = [pltpu.SemaphoreType.DMA((self.n_sem,)), pltpu.SemaphoreType.DMA((self.n_sem,))]
        return pl.pallas_call(
            body, grid=grid, name=name,
            in_specs=list(in_specs) + [HBM_SPEC] * len(self.ins),
            out_specs=list(out_specs) + [HBM_SPEC] * len(self.out_shapes),
            out_shape=list(out_shape) + self.out_shapes,
            scratch_shapes=list(scratch_shapes) + sems,
            input_output_aliases={n_in + i: n_out + o for i, o in self.aliases.items()},
            compiler_params=_cp(vmem_mib),
        )(*_hbm(*args), *self.ins)

    def run(self, name):
        def body(*refs):
            ride = self.split(refs, 0, 0, 0)[3]
            self.first(ride)
            self.mid(ride)
            self.last(ride)

        return self.call(body, [], grid=(), name=name, in_specs=[], out_specs=[], out_shape=[],
                         scratch_shapes=[], vmem_mib=16)


def _remote(src, dst, ssem, rsem, n, dev):
    return functools.partial(pltpu.make_async_remote_copy, src_ref=src, dst_ref=dst, send_sem=ssem.at[n],
                             recv_sem=rsem.at[n], device_id=dev, device_id_type=MESH)


def _cast_into_slot(w, place, name):
    R, C = w.shape
    tr = min(R, 512)

    def body(pl_ref, w_ref, o_ref):
        o_ref[0] = w_ref[...].astype(BF16)

    return pl.pallas_call(
        body, name=name,
        grid_spec=pltpu.PrefetchScalarGridSpec(
            num_scalar_prefetch=1, grid=(R // tr,),
            in_specs=[pl.BlockSpec((tr, C), lambda r, pr: (r, 0))],
            out_specs=pl.BlockSpec((1, tr, C), lambda r, pr: (pr[1], r, 0))),
        out_shape=_sds((N_CHIPS, R, C), BF16),
    )(place, w)


def _gather_rider(stacked):
    n, nf = len(stacked), len(CHIP_FLIPS)

    def phases(ins, outs, ssem, rsem):
        x, y, c = _place()
        slot = 2 * x + y
        ici, d2d = [], []
        for w, (i_ref, o_ref) in enumerate(zip(ins, outs)):
            hh = o_ref.shape[1] // 2
            rows = pl.ds(c * hh, hh)
            for f, (fx, fy) in enumerate(CHIP_FLIPS):
                k = w * nf + f
                theirs = 2 * (x ^ fx) + (y ^ fy)
                ici.append(_remote(i_ref.at[slot, rows], o_ref.at[slot, rows], ssem, rsem, k, (x ^ fx, y ^ fy, c)))
                d2d.append(_remote(o_ref.at[theirs, rows], o_ref.at[theirs, rows], ssem, rsem, n * nf + k,
                                   (x, y, 1 - c)))
        return [ici, d2d]

    return _Rider(stacked, [_sds(s.shape, s.dtype) for s in stacked], 2 * n * nf, phases,
                  aliases={i: i for i in range(n)})


def _pair_swap_rider(grads):
    def phases(ins, outs, ssem, rsem):
        x, y, c = _place()
        return [[_remote(g.at[:, 1 - c], o, ssem, rsem, k, (x, y, 1 - c))
                 for k, (g, o) in enumerate(zip(ins, outs))]]

    return _Rider(grads, [_sds((N_CHIPS,) + g.shape[2:], g.dtype) for g in grads], len(grads), phases)


def _chip_scatter_rider(parts):
    nf = len(CHIP_FLIPS)

    def phases(ins, outs, ssem, rsem):
        x, y, c = _place()
        return [[_remote(r.at[2 * (x ^ fx) + (y ^ fy)], o.at[f], ssem, rsem, w * nf + f, (x ^ fx, y ^ fy, c))
                 for w, (r, o) in enumerate(zip(ins, outs)) for f, (fx, fy) in enumerate(CHIP_FLIPS)]]

    return _Rider(parts, [_sds((nf,) + r.shape[1:], r.dtype) for r in parts], len(parts) * nf, phases)


def _pair_send_rider(halves):
    def phases(ins, outs, ssem, rsem):
        x, y, c = _place()
        return [[_remote(h, o, ssem, rsem, k, (x, y, 1 - c)) for k, (h, o) in enumerate(zip(ins, outs))]]

    return _Rider(halves, [_sds(h.shape, h.dtype) for h in halves], len(halves), phases)


def _add_pair(grad, recv, place, name):
    _, _, H, C = grad.shape
    th = min(H, 256)

    def body(pl_ref, g_ref, r_ref, o_ref, ob_ref):
        s = g_ref[:, 0] + r_ref[...]
        o_ref[...] = s
        ob_ref[...] = s.astype(BF16)

    spec = pl.BlockSpec((1, th, C), lambda j, h, pr: (j, h, 0))
    return pl.pallas_call(
        body, name=name,
        grid_spec=pltpu.PrefetchScalarGridSpec(
            num_scalar_prefetch=1, grid=(N_CHIPS, H // th),
            in_specs=[pl.BlockSpec((1, 1, th, C), lambda j, h, pr: (j, pr[0], h, 0)), spec],
            out_specs=[spec, spec]),
        out_shape=[_sds((N_CHIPS, H, C), F32), _sds((N_CHIPS, H, C), BF16)],
    )(place, *_hbm(grad, recv))


def _add_chips(part, recv, place, name):
    _, H, C = part.shape
    th = min(H, 256)

    def body(pl_ref, p_ref, r_ref, o_ref):
        o_ref[...] = p_ref[0] + r_ref[0].astype(F32) + r_ref[1].astype(F32) + r_ref[2].astype(F32)

    return pl.pallas_call(
        body, name=name,
        grid_spec=pltpu.PrefetchScalarGridSpec(
            num_scalar_prefetch=1, grid=(H // th,),
            in_specs=[pl.BlockSpec((1, th, C), lambda h, pr: (pr[1], h, 0)),
                      pl.BlockSpec((len(CHIP_FLIPS), th, C), lambda h, pr: (0, h, 0))],
            out_specs=pl.BlockSpec((th, C), lambda h, pr: (h, 0))),
        out_shape=_sds((H, C), F32),
    )(place, *_hbm(part, recv))


def _adamw_math(w, g, m, v):
    m = ADAM_B1 * m + (1.0 - ADAM_B1) * g
    v = ADAM_B2 * v + (1.0 - ADAM_B2) * (g * g)
    m_hat = m / (1.0 - ADAM_B1 ** ADAM_STEP)
    v_hat = v / (1.0 - ADAM_B2 ** ADAM_STEP)
    delta = -ADAM_LR * (m_hat / (jnp.sqrt(v_hat) + ADAM_EPS) + ADAM_WD * w)
    return delta, m, v


def _adamw(w, mine, theirs, m, v, place, name):
    R, C = w.shape
    th = min(R // 2, 256)
    nb = (R // 2) // th

    def body(pl_ref, w_ref, a_ref, b_ref, m_ref, v_ref, g_ref, d_ref, mo_ref, vo_ref):
        g = jnp.where(pl.program_id(0) == pl_ref[0], a_ref[...], b_ref[...])
        g_ref[...] = g
        d, mo, vo = _adamw_math(w_ref[...], g, m_ref[...], v_ref[...])
        d_ref[...] = d
        mo_ref[...] = mo
        vo_ref[...] = vo

    whole = pl.BlockSpec((th, C), lambda h, r, pr: (h * nb + r, 0))
    half = pl.BlockSpec((th, C), lambda h, r, pr: (r, 0))
    return pl.pallas_call(
        body, name=name,
        grid_spec=pltpu.PrefetchScalarGridSpec(
            num_scalar_prefetch=1, grid=(2, nb),
            in_specs=[whole, half, half, whole, whole], out_specs=[whole] * 4),
        out_shape=[_sds((R, C), F32)] * 4,
    )(place, *_hbm(w, mine, theirs, m, v))


DEVICE_FLIPS = tuple((fx, fy, fc) for fx in (0, 1) for fy in (0, 1) for fc in (0, 1))[1:]


def _pack_exchange_rider(pack):
    def phases(ins, outs, ssem, rsem):
        x, y, c = _place()
        mine = outs[0].at[4 * x + 2 * y + c]
        copies = [_remote(ins[0], mine, ssem, rsem, k, (x ^ fx, y ^ fy, c ^ fc))
                  for k, (fx, fy, fc) in enumerate(DEVICE_FLIPS)]
        copies.append(functools.partial(pltpu.make_async_copy, ins[0], mine, ssem.at[len(DEVICE_FLIPS)]))
        return [copies]

    return _Rider([pack], [_sds((N_DEV,) + pack.shape, pack.dtype)], len(DEVICE_FLIPS) + 1, phases)


def _small_sum_adamw(recv_a, recv_b, wpack, mpack, vpack):
    R = wpack.shape[0]

    def body(a_ref, b_ref, w_ref, m_ref, v_ref, gs_ref, d_ref, mo_ref, vo_ref):
        ta, tb = a_ref[0], b_ref[0]
        for dev in range(1, N_DEV):
            ta = ta + a_ref[dev]
            tb = tb + b_ref[dev]
        total = jnp.concatenate([ta, tb], axis=0)
        gs_ref[...] = total
        d, mo, vo = _adamw_math(w_ref[...], total, m_ref[...], v_ref[...])
        d_ref[...] = d
        mo_ref[...] = mo
        vo_ref[...] = vo

    return pl.pallas_call(
        body, name="small_sum_adamw", in_specs=[VMEM_SPEC] * 5, out_specs=[VMEM_SPEC] * 4,
        out_shape=[_sds((R, LANES), F32)] * 4,
    )(recv_a, recv_b, wpack, mpack, vpack)


def _rows8(a):
    a = a.reshape(-1, LANES)
    pad = (-a.shape[0]) % 8
    return jnp.pad(a, ((0, pad), (0, 0))) if pad else a


def _pack(parts):
    return jnp.concatenate([_rows8(a) for a in parts], axis=0)


def _unpack(pack, like):
    out, row = [], 0
    for a in like:
        n = a.size // LANES
        out.append(pack[row:row + n].reshape(a.shape))
        row += n + (-n) % 8
    return out


def kernel(x, p, emb_ln_g, emb_ln_b, w_in, attn_out_g, w_pool, pool_scale, w_out, ln1_g, ln1_b, w_up, w_down, ln2_g, ln2_b, w_ple, w_ple_gate, ln3_g, ln3_b, loss_target, m_emb_ln_g, m_emb_ln_b, m_w_in, m_attn_out_g, m_w_pool, m_pool_scale, m_w_out, m_ln1_g, m_ln1_b, m_w_up, m_w_down, m_ln2_g, m_ln2_b, m_w_ple, m_w_ple_gate, m_ln3_g, m_ln3_b, v_emb_ln_g, v_emb_ln_b, v_w_in, v_attn_out_g, v_w_pool, v_pool_scale, v_w_out, v_ln1_g, v_ln1_b, v_w_up, v_w_down, v_ln2_g, v_ln2_b, v_w_ple, v_w_ple_gate, v_ln3_g, v_ln3_b):
    S = x.shape[1]
    tm = min(256, S)
    tq = min(256, S)
    tm_mlp = min(512, S)
    xs = x[0]
    ps = p[0, 0]
    tgt = loss_target[0]
    row = lambda a: a.reshape(1, -1)
    g0, b0 = row(emb_ln_g), row(emb_ln_b)
    g1, b1, g2, b2, g3, b3 = ln1_g, ln1_b, ln2_g, ln2_b, ln3_g, ln3_b
    wp = w_pool[0]

    xi, yi, ci = _place()
    place = jnp.stack([ci, 2 * xi + yi]).astype(jnp.int32)
    names = ["w_in", "w_out", "w_up", "w_down", "w_ple", "w_ple_gate"]

    big = [w_in[0], w_out[0], w_up[0], w_down[0], w_ple[0], w_ple_gate[0]]
    stacked = dict(zip(names, [_cast_into_slot(w, place, "cast_" + n) for w, n in zip(big, names)]))
    (w_in_s,) = _gather_rider([stacked["w_in"]]).run("gather_w_in")

    xh0, rstd0, q, k, v, u, w_out_s, w_ple_s, w_gate_s = _embln_inproj(
        xs, g0, b0, w_in_s, tm, _gather_rider([stacked[n] for n in ("w_out", "w_ple", "w_ple_gate")]))
    o_raw, on, w_up_s, w_down_s = _attn_fwd(
        q, k, v, attn_out_g, tq, _gather_rider([stacked["w_up"], stacked["w_down"]]))
    w_out_f = w_out_s.reshape(D_MODEL, D_MODEL)
    w_down_f = w_down_s.reshape(D_FF, D_MODEL)
    w_gate_f = w_gate_s.reshape(D_MODEL, D_MODEL)
    d_b, pooled = _pool_fwd(u, wp, pool_scale, tm)
    xh1, rstd1, x1b = _mix_ln1(on, pooled, xh0, g0, b0, w_out_f, g1, b1, tm)
    xh2, rstd2 = _mlp_ln2(xh1, x1b, g1, b1, w_up_s, w_down_f, tm_mlp)

    (dpre2, dhb, dw_ple, dw_gate, dg3, db3, dg2, db2, loss_row) = _ple_ln3_loss(
        xh2, rstd2, g2, b2, ps, w_ple_s, w_gate_f, g3, b3, tgt, tm)
    dx1m, da, h1 = _mlp_bwd(x1b, dhb, w_up_s, w_down_f, tm_mlp)
    dw_up = _tn_matmul(x1b, da, "grad_w_up", 1024, min(512, S), stacked=True)
    dw_down = _tn_matmul(h1, dhb, "grad_w_down", 1024, min(512, S), stacked=False)
    def halves_of(g):
        return g.reshape(N_CHIPS, 2, g.shape[1] // 2, g.shape[2])

    early_names = names[2:]
    early = [halves_of(g) for g in (dw_up, dw_down.reshape(N_CHIPS, D_FF // N_CHIPS, D_MODEL), dw_ple,
                                    dw_gate.reshape(N_CHIPS, D_MODEL // N_CHIPS, D_MODEL))]
    dpre1, don, dpooled, dw_out, dg1, db1, *early_pair = _mix_bwd(
        dpre2, dx1m, xh1, rstd1, g1, w_out_f, on, pooled, tm, _pair_swap_rider(early))
    early_sum = [_add_pair(g, r, place, "pair_sum_" + n) for g, r, n in zip(early, early_pair, early_names)]
    out_halves = halves_of(dw_out.reshape(N_CHIPS, D_MODEL // N_CHIPS, D_MODEL))
    du, dwp, dsc, out_pair = _pool_bwd(dpooled, d_b, wp, pool_scale, tm, _pair_swap_rider([out_halves]))
    out_sum = _add_pair(out_halves, out_pair, place, "pair_sum_w_out")
    pack_a = _pack([jnp.broadcast_to(loss_row, (8, LANES)), dwp, dsc, dg1, db1, dg2, db2, dg3, db3])
    riding = _chip_scatter_rider([out_sum[1]] + [b for _, b in early_sum]) + _pack_exchange_rider(pack_a)
    dq, dk, dv, dga, *arrived = _attn_bwd(q, k, v, don, o_raw, attn_out_g, tq, riding)
    early_chips, recv_a = arrived[:-1], arrived[-1]
    grad_x, dw_in, dg0, db0 = _inproj_bwd(dq, dk, dv, du, dpre1, xh0, rstd0, g0, b0, w_in_s, tm)

    in_halves = halves_of(dw_in)
    pack_b = _pack([dg0, db0, dga])
    in_pair, recv_b = (_pair_swap_rider([in_halves]) + _pack_exchange_rider(pack_b)).run("reduce_pair_late")
    in_sum = _add_pair(in_halves, in_pair, place, "pair_sum_w_in")
    (in_chips,) = _chip_scatter_rider([in_sum[1]]).run("reduce_chips_late")
    mine = [_add_chips(s, r, place, "chip_sum_" + n)
            for (s, _), r, n in zip([in_sum, out_sum] + early_sum, [in_chips] + early_chips, names)]
    theirs = _pair_send_rider(mine).run("gather_pair")

    ms = [m_w_in, m_w_out, m_w_up, m_w_down, m_w_ple, m_w_ple_gate]
    vs = [v_w_in, v_w_out, v_w_up, v_w_down, v_w_ple, v_w_ple_gate]
    big_out = {}
    for n, w, a, b, m, vv in zip(names, big, mine, theirs, ms, vs):
        res4 = _adamw(w, a, b, m[0], vv[0], place, "adamw_" + n)
        big_out[n] = tuple(r.reshape(m.shape) for r in res4)

    small_names = ["w_pool", "pool_scale", "ln1_g", "ln1_b", "ln2_g", "ln2_b", "ln3_g", "ln3_b",
                   "emb_ln_g", "emb_ln_b", "attn_out_g"]
    small_w = [w_pool, pool_scale, ln1_g, ln1_b, ln2_g, ln2_b, ln3_g, ln3_b, emb_ln_g, emb_ln_b, attn_out_g]
    small_m = [m_w_pool, m_pool_scale, m_ln1_g, m_ln1_b, m_ln2_g, m_ln2_b, m_ln3_g, m_ln3_b,
               m_emb_ln_g, m_emb_ln_b, m_attn_out_g]
    small_v = [v_w_pool, v_pool_scale, v_ln1_g, v_ln1_b, v_ln2_g, v_ln2_b, v_ln3_g, v_ln3_b,
               v_emb_ln_g, v_emb_ln_b, v_attn_out_g]
    loss_like = jnp.zeros((8, LANES), F32)
    gs, ds, mos, vos = _small_sum_adamw(recv_a, recv_b, _pack([loss_like] + small_w), _pack([loss_like] + small_m),
                                        _pack([jnp.ones((8, LANES), F32)] + small_v))
    like = [loss_like] + small_w
    gs_u, ds_u, mos_u, vos_u = (_unpack(a, like) for a in (gs, ds, mos, vos))
    loss = gs_u[0][0, 0]
    small_out = {n: (gs_u[i + 1], ds_u[i + 1], mos_u[i + 1], vos_u[i + 1]) for i, n in enumerate(small_names)}

    order = ["emb_ln_g", "emb_ln_b", "w_in", "attn_out_g", "w_pool", "pool_scale", "w_out", "ln1_g", "ln1_b",
             "w_up", "w_down", "ln2_g", "ln2_b", "w_ple", "w_ple_gate", "ln3_g", "ln3_b"]
    res = {**big_out, **small_out}
    outs = [loss, grad_x.reshape(x.shape)]
    for kind in range(4):
        outs += [res[n][kind] for n in order]
    return tuple(outs)
```

```python
import functools

import jax
import jax.numpy as jnp
from jax import lax
from jax.experimental import pallas as pl
from jax.experimental.pallas import tpu as pltpu

F32 = jnp.float32
BF16 = jnp.bfloat16

D_MODEL = 1024
ATTN_WIDTH = 512
POOL_WIDTH = 512
HEAD_DIM = 64
PAIR = 2 * HEAD_DIM
N_PAIRS = ATTN_WIDTH // PAIR
N_POOL_GROUPS = 4
POOL_GROUP = 128
POOL_HALO = 16
D_FF = 4096
PLE_DIM = 256
N_CHIPS = 4
N_DEV = 8
LN_EPS = 1e-5
RMS_EPS = 1e-6
ALPHA = float(2.0 ** 0.25)
Q_SCALE = 0.125
ADAM_LR = 0.001
ADAM_B1 = 0.9
ADAM_B2 = 0.999
ADAM_EPS = 1e-08
ADAM_WD = 0.01
ADAM_STEP = 10
LANES = 128
MIB = 1024 * 1024

MESH = pl.DeviceIdType.MESH
HBM_SPEC = pl.BlockSpec(memory_space=pltpu.HBM)
VMEM_SPEC = pl.BlockSpec(memory_space=pltpu.VMEM)


def _cp(vmem_mib):
    return pltpu.CompilerParams(vmem_limit_bytes=vmem_mib * MIB)


def _dot(a, b):
    return jnp.dot(a, b, preferred_element_type=F32)


def _dot_nt(a, b):
    return lax.dot_general(a, b, (((1,), (1,)), ((), ())), preferred_element_type=F32)


def _dot_tn(a, b):
    return lax.dot_general(a, b, (((0,), (0,)), ((), ())), preferred_element_type=F32)


def _ln_fwd(pre):
    mu = jnp.mean(pre, axis=-1, keepdims=True)
    xc = pre - mu
    var = jnp.mean(xc * xc, axis=-1, keepdims=True)
    rstd = lax.rsqrt(var + LN_EPS)
    return xc * rstd, rstd


def _ln_bwd(dy, xh, rstd, g):
    dxh = dy * g
    m1 = jnp.mean(dxh, axis=-1, keepdims=True)
    m2 = jnp.mean(dxh * xh, axis=-1, keepdims=True)
    return rstd * (dxh - m1 - xh * m2)


def _colsum(a):
    return jnp.sum(a, axis=0, keepdims=True)


def _neg_softplus(z):
    return -(jnp.maximum(z, 0.0) + jnp.log(1.0 + jnp.exp(-jnp.abs(z))))


def _split_bf16(a):
    hi = a.astype(BF16)
    lo = (a - hi.astype(F32)).astype(BF16)
    return hi, lo


def _row_spec(tm, n):
    return pl.BlockSpec((tm, n), lambda i: (i, 0))


def _const_spec(shape):
    nd = len(shape)
    return pl.BlockSpec(shape, lambda *_: (0,) * nd)


def _hbm(*arrays):
    return [pltpu.with_memory_space_constraint(a, pltpu.HBM) for a in arrays]


def _sds(shape, dtype):
    return pltpu.HBM(shape, dtype)


def _embln_inproj(x, g0, b0, w_in_s, tm, rider):
    S, D = x.shape
    n_t = S // tm

    def body(*refs):
        ((x_ref, g_ref, b_ref, w_ref), (xh_ref, rstd_ref, q_ref, k_ref, v_ref, u_ref), _,
         ride) = rider.split(refs, 4, 6, 0)
        i = pl.program_id(0)

        @pl.when(i == 0)
        def _():
            rider.first(ride)

        @pl.when(i == (3 * n_t) // 4)
        def _():
            rider.mid(ride)

        xh, rstd = _ln_fwd(x_ref[...])
        xh_ref[...] = xh
        rstd_ref[...] = rstd
        xb = (xh * g_ref[...] + b_ref[...]).astype(BF16)
        q_ref[...] = (_dot(xb, w_ref[0]) * Q_SCALE).astype(BF16)
        k_ref[...] = _dot(xb, w_ref[1]).astype(BF16)
        v_ref[...] = _dot(xb, w_ref[2]).astype(BF16)
        u_ref[...] = _dot(xb, w_ref[3])

        @pl.when(i == n_t - 1)
        def _():
            rider.last(ride)

    return rider.call(
        body, [x, g0, b0, w_in_s], grid=(n_t,), name="embln_inproj",
        in_specs=[_row_spec(tm, D), _const_spec((1, D)), _const_spec((1, D)),
                  _const_spec((N_CHIPS, D, 512))],
        out_specs=[_row_spec(tm, D), _row_spec(tm, 1), _row_spec(tm, 512), _row_spec(tm, 512),
                   _row_spec(tm, 512), _row_spec(tm, 512)],
        out_shape=[_sds((S, D), F32), _sds((S, 1), F32), _sds((S, 512), BF16), _sds((S, 512), BF16),
                   _sds((S, 512), BF16), _sds((S, 512), F32)],
        scratch_shapes=[], vmem_mib=40)


def _tri(n, upper):
    r = lax.broadcasted_iota(jnp.int32, (n, n), 0)
    c = lax.broadcasted_iota(jnp.int32, (n, n), 1)
    keep = (r < c) if upper else (r > c)
    return jnp.where(keep, 1.0, 0.0).astype(BF16)


def _strictly_causal(n):
    return lax.broadcasted_iota(jnp.int32, (n, n), 1) < lax.broadcasted_iota(jnp.int32, (n, n), 0)


LOG_WEIGHT_FLOOR = -110.0


def _sb_tile(qhs, kt, low, c_ls, valid):
    valids = valid if isinstance(valid, (list, tuple)) else [valid] * len(qhs)
    zs = [_dot_nt(qh, kt) for qh in qhs]
    lrs = [_neg_softplus(z) for z in zs]
    ls_ = [lr if m is None else jnp.where(m, lr, 0.0) for lr, m in zip(lrs, valids)]
    sfx = [_dot(l.astype(BF16), low) + c_l for l, c_l in zip(ls_, c_ls)]
    lss = [z + lr for z, lr in zip(zs, lrs)]
    ws = [jnp.exp(ls + s) for ls, s in zip(lss, sfx)]
    ws = [w if m is None else jnp.where(m, w, 0.0) for w, m in zip(ws, valids)]
    return lss, ls_, ws


def _attn_fwd(q, k, v, ga, tq, rider):
    S = q.shape[0]
    nq = S // tq

    def body(*refs):
        (q_ref, k_ref, v_ref, ga_ref), (o_ref, on_ref), _, ride = rider.split(refs, 4, 2, 0)
        p, i = pl.program_id(0), pl.program_id(1)

        @pl.when(jnp.logical_and(p == 0, i == 0))
        def _():
            rider.first(ride)

        @pl.when(jnp.logical_and(p == N_PAIRS - 1, i == 0))
        def _():
            rider.mid(ride)

        lane = lax.broadcasted_iota(jnp.int32, (1, PAIR), 1)
        m0 = lane < HEAD_DIM
        low = _tri(tq, upper=False)
        q2 = q_ref[...]
        qhs = [jnp.where(m0, q2, jnp.zeros_like(q2)), jnp.where(m0, jnp.zeros_like(q2), q2)]

        def tile(kb, c_ls, accs, valid):
            ks = pl.multiple_of(kb * tq, tq)
            kt = k_ref[pl.ds(ks, tq), :]
            vt = v_ref[pl.ds(ks, tq), :]
            _, ls_, ws = _sb_tile(qhs, kt, low, c_ls, valid)
            new_a = [acc + _dot(w.astype(BF16), vt) for acc, w in zip(accs, ws)]
            new_c = [c_l + jnp.sum(l, axis=1, keepdims=True) for c_l, l in zip(c_ls, ls_)]
            return new_c, new_a

        zc, za = jnp.zeros((tq, 1), F32), jnp.zeros((tq, PAIR), F32)

        def first_two():
            c_ls, accs = tile(i, [zc, zc], [za, za], _strictly_causal(tq))
            c_ls, accs = tile(i - 1, c_ls, accs, None)
            return (*c_ls, *accs)

        def first_one():
            c_ls, accs = tile(i, [zc, zc], [za, za], _strictly_causal(tq))
            return (*c_ls, *accs)

        st0 = lax.cond(i >= 1, first_two, first_one)

        def more(st):
            return jnp.logical_and(st[0] <= i, jnp.max(jnp.maximum(st[1], st[2])) > LOG_WEIGHT_FLOOR)

        def step(st):
            n, c0, c1, a0, a1 = st
            c_ls, accs = tile(i - n, [c0, c1], [a0, a1], None)
            return (n + 1, c_ls[0], c_ls[1], accs[0], accs[1])

        st = lax.while_loop(more, step, (jnp.int32(2), *st0))
        o = jnp.where(m0, st[3], st[4])
        o_ref[...] = o
        sq = o * o
        ms0 = jnp.sum(jnp.where(m0, sq, 0.0), axis=-1, keepdims=True) * (1.0 / HEAD_DIM)
        ms1 = jnp.sum(jnp.where(m0, 0.0, sq), axis=-1, keepdims=True) * (1.0 / HEAD_DIM)
        rs = jnp.where(m0, lax.rsqrt(ms0 + RMS_EPS), lax.rsqrt(ms1 + RMS_EPS))
        on_ref[...] = (o * rs * ga_ref[...]).astype(BF16)

        @pl.when(jnp.logical_and(p == N_PAIRS - 1, i == nq - 1))
        def _():
            rider.last(ride)

    return rider.call(
        body, [q, k, v, ga], grid=(N_PAIRS, nq), name="attn_fwd",
        in_specs=[pl.BlockSpec((tq, PAIR), lambda p, i: (i, p)),
                  pl.BlockSpec((S, PAIR), lambda p, i: (0, p)),
                  pl.BlockSpec((S, PAIR), lambda p, i: (0, p)),
                  pl.BlockSpec((1, PAIR), lambda p, i: (0, p))],
        out_specs=[pl.BlockSpec((tq, PAIR), lambda p, i: (i, p)),
                   pl.BlockSpec((tq, PAIR), lambda p, i: (i, p))],
        out_shape=[_sds((S, ATTN_WIDTH), F32), _sds((S, ATTN_WIDTH), BF16)],
        scratch_shapes=[], vmem_mib=40)


def _pool_fwd(u, w_pool, pscale, tm):
    S = u.shape[0]
    hb = tm // POOL_HALO

    def body(u_ref, uh_ref, wp_ref, sc_ref, d_ref, pooled_ref):
        i = pl.program_id(0)
        halo = jnp.where(i > 0, uh_ref[...], 0.0)
        pos = i * tm + lax.broadcasted_iota(jnp.int32, (tm, 1), 0)
        for g in range(N_POOL_GROUPS):
            win = 2 ** (g + 1)
            cols = slice(g * POOL_GROUP, (g + 1) * POOL_GROUP)
            ut = u_ref[:, cols]
            s = jnp.concatenate([halo[:, cols], ut], axis=0)
            for sh in (1, 2, 4, 8)[:g + 1]:
                s = s + pltpu.roll(s, sh, 0)
            cnt = jnp.minimum(pos + 1, win).astype(F32)
            db = (s[POOL_HALO:, :] / cnt - ut).astype(BF16)
            y = _dot(db, wp_ref[g].astype(BF16))
            d_ref[:, cols] = db
            pooled_ref[:, cols] = (y * sc_ref[:, cols]).astype(BF16)

    return pl.pallas_call(
        body, grid=(S // tm,), name="pool_fwd",
        in_specs=[_row_spec(tm, POOL_WIDTH),
                  pl.BlockSpec((POOL_HALO, POOL_WIDTH), lambda i: (jnp.maximum(i * hb - 1, 0), 0)),
                  _const_spec((N_POOL_GROUPS, POOL_GROUP, POOL_GROUP)), _const_spec((1, POOL_WIDTH))],
        out_specs=[_row_spec(tm, POOL_WIDTH), _row_spec(tm, POOL_WIDTH)],
        out_shape=[_sds((S, POOL_WIDTH), BF16), _sds((S, POOL_WIDTH), BF16)],
        compiler_params=_cp(32),
    )(*_hbm(u, u, w_pool, pscale))


def _mix_ln1(on, pooled, xh0, g0, b0, w_out, g1, b1, tm, rider):
    S, D = xh0.shape
    n_t = S // tm

    def body(*refs):
        ((on_ref, po_ref, xh0_ref, g0_ref, b0_ref, w_ref, g1_ref, b1_ref), (xh_ref, rstd_ref, xb_ref), _,
         ride) = rider.split(refs, 8, 3, 0)

        @pl.when(pl.program_id(0) == 0)
        def _():
            rider.first(ride)

        mixed = _dot(on_ref[...], w_ref[:ATTN_WIDTH, :]) + _dot(po_ref[...], w_ref[ATTN_WIDTH:, :])
        x0 = xh0_ref[...] * g0_ref[...] + b0_ref[...]
        xh, rstd = _ln_fwd(ALPHA * x0 + mixed)
        xh_ref[...] = xh
        rstd_ref[...] = rstd
        xb_ref[...] = (xh * g1_ref[...] + b1_ref[...]).astype(BF16)

        @pl.when(pl.program_id(0) == n_t - 1)
        def _():
            rider.last(ride)

    return rider.call(
        body, [on, pooled, xh0, g0, b0, w_out, g1, b1], grid=(n_t,), name="mix_ln1",
        in_specs=[_row_spec(tm, ATTN_WIDTH), _row_spec(tm, POOL_WIDTH), _row_spec(tm, D),
                  _const_spec((1, D)), _const_spec((1, D)), _const_spec((D, D)),
                  _const_spec((1, D)), _const_spec((1, D))],
        out_specs=[_row_spec(tm, D), _row_spec(tm, 1), _row_spec(tm, D)],
        out_shape=[_sds((S, D), F32), _sds((S, 1), F32), _sds((S, D), BF16)],
        scratch_shapes=[], vmem_mib=40)


def _mlp_ln2(xh1, x1b, g1, b1, w_up_s, w_down, tm):
    S, D = xh1.shape
    fc = D_FF // N_CHIPS

    def body(xh_ref, xb_ref, g_ref, b_ref, wu_ref, wd_ref, xh2_ref, rstd_ref, acc_ref):
        j = pl.program_id(1)

        @pl.when(j == 0)
        def _():
            acc_ref[...] = jnp.zeros_like(acc_ref)

        a = _dot(xb_ref[...], wu_ref[0])
        r = jnp.maximum(a, 0.0)
        acc_ref[...] += _dot((r * r).astype(BF16), wd_ref[...])

        @pl.when(j == N_CHIPS - 1)
        def _():
            x1 = xh_ref[...] * g_ref[...] + b_ref[...]
            xh, rstd = _ln_fwd(ALPHA * x1 + acc_ref[...])
            xh2_ref[...] = xh
            rstd_ref[...] = rstd

    return pl.pallas_call(
        body, grid=(S // tm, N_CHIPS), name="mlp_ln2",
        in_specs=[pl.BlockSpec((tm, D), lambda i, j: (i, 0)), pl.BlockSpec((tm, D), lambda i, j: (i, 0)),
                  pl.BlockSpec((1, D), lambda i, j: (0, 0)), pl.BlockSpec((1, D), lambda i, j: (0, 0)),
                  pl.BlockSpec((1, D, fc), lambda i, j: (j, 0, 0)),
                  pl.BlockSpec((fc, D), lambda i, j: (j, 0))],
        out_specs=[pl.BlockSpec((tm, D), lambda i, j: (i, 0)), pl.BlockSpec((tm, 1), lambda i, j: (i, 0))],
        out_shape=[_sds((S, D), F32), _sds((S, 1), F32)],
        scratch_shapes=[pltpu.VMEM((tm, D), F32)],
        compiler_params=_cp(40),
    )(*_hbm(xh1, x1b, g1, b1, w_up_s, w_down))


def _ple_ln3_loss(xh2, rstd2, g2, b2, p, w_ple_s, w_gate, g3, b3, target, tm):
    S, D = xh2.shape
    pc = D // N_CHIPS

    def body(xh2_ref, rstd2_ref, g2_ref, b2_ref, p_ref, wp_ref, wg_ref, g3_ref, b3_ref, t_ref,
             dpre2_ref, dhb_ref, dwp_ref, dwg_ref, dg3_ref, db3_ref, dg2_ref, db2_ref, loss_ref):
        i = pl.program_id(0)

        @pl.when(i == 0)
        def _():
            for r in (dwp_ref, dwg_ref, dg3_ref, db3_ref, dg2_ref, db2_ref, loss_ref):
                r[...] = jnp.zeros_like(r)

        xh2 = xh2_ref[...]
        x2 = xh2 * g2_ref[...] + b2_ref[...]
        x2b = x2.astype(BF16)
        gate = 1.0 / (1.0 + jnp.exp(-_dot(x2b, wg_ref[...])))
        pb = p_ref[...].astype(BF16)
        pe = jnp.concatenate([_dot(pb, wp_ref[c]) for c in range(N_CHIPS)], axis=1)
        xh3, rstd3 = _ln_fwd(ALPHA * x2 + pe * gate)
        diff = xh3 * g3_ref[...] + b3_ref[...] - t_ref[...]
        loss_ref[...] += (0.5 / D) * jnp.sum(diff * diff)
        dy = diff * (1.0 / D)
        dg3_ref[...] += _colsum(dy * xh3)
        db3_ref[...] += _colsum(dy)
        dpre3 = _ln_bwd(dy, xh3, rstd3, g3_ref[...])
        dpe_b = (dpre3 * gate).astype(BF16)
        dgp_b = (dpre3 * pe * gate * (1.0 - gate)).astype(BF16)
        dx2 = ALPHA * dpre3 + _dot_nt(dgp_b, wg_ref[...])
        dwg_ref[...] += _dot_tn(x2b, dgp_b)
        for c in range(N_CHIPS):
            dwp_ref[c] += _dot_tn(pb, dpe_b[:, c * pc:(c + 1) * pc])
        dg2_ref[...] += _colsum(dx2 * xh2)
        db2_ref[...] += _colsum(dx2)
        dpre2 = _ln_bwd(dx2, xh2, rstd2_ref[...], g2_ref[...])
        dpre2_ref[...] = dpre2
        dhb_ref[...] = dpre2.astype(BF16)

    vec = _const_spec((1, D))
    return pl.pallas_call(
        body, grid=(S // tm,), name="ple_ln3_loss",
        in_specs=[_row_spec(tm, D), _row_spec(tm, 1), vec, vec, _row_spec(tm, PLE_DIM),
                  _const_spec((N_CHIPS, PLE_DIM, pc)), _const_spec((D, D)), vec, vec, _row_spec(tm, D)],
        out_specs=[_row_spec(tm, D), _row_spec(tm, D), _const_spec((N_CHIPS, PLE_DIM, pc)),
                   _const_spec((D, D)), vec, vec, vec, vec, _const_spec((1, LANES))],
        out_shape=[_sds((S, D), F32), _sds((S, D), BF16), _sds((N_CHIPS, PLE_DIM, pc), F32),
                   _sds((D, D), F32), _sds((1, D), F32), _sds((1, D), F32), _sds((1, D), F32),
                   _sds((1, D), F32), _sds((1, LANES), F32)],
        compiler_params=_cp(48),
    )(*_hbm(xh2, rstd2, g2, b2, p, w_ple_s, w_gate, g3, b3, target))


def _mlp_bwd(x1b, dhb, w_up_s, w_down, tm):
    S, D = x1b.shape
    fc = D_FF // N_CHIPS

    def body(xb_ref, dh_ref, wu_ref, wd_ref, dx_ref, da_ref, h1_ref):
        j = pl.program_id(1)

        @pl.when(j == 0)
        def _():
            dx_ref[...] = jnp.zeros_like(dx_ref)

        a = _dot(xb_ref[...], wu_ref[0])
        r = jnp.maximum(a, 0.0)
        h1_ref[...] = (r * r).astype(BF16)
        da = (_dot_nt(dh_ref[...], wd_ref[...]) * (2.0 * r)).astype(BF16)
        da_ref[...] = da
        dx_ref[...] += _dot_nt(da, wu_ref[0])

    return pl.pallas_call(
        body, grid=(S // tm, N_CHIPS), name="mlp_bwd",
        in_specs=[pl.BlockSpec((tm, D), lambda i, j: (i, 0)), pl.BlockSpec((tm, D), lambda i, j: (i, 0)),
                  pl.BlockSpec((1, D, fc), lambda i, j: (j, 0, 0)),
                  pl.BlockSpec((fc, D), lambda i, j: (j, 0))],
        out_specs=[pl.BlockSpec((tm, D), lambda i, j: (i, 0)), pl.BlockSpec((tm, fc), lambda i, j: (i, j)),
                   pl.BlockSpec((tm, fc), lambda i, j: (i, j))],
        out_shape=[_sds((S, D), F32), _sds((S, D_FF), BF16), _sds((S, D_FF), BF16)],
        compiler_params=_cp(40),
    )(*_hbm(x1b, dhb, w_up_s, w_down))


def _tn_matmul(a, b, name, tk, tt, stacked):
    T, K = a.shape
    N = b.shape[1]
    tn = 1024

    def body(a_ref, b_ref, o_ref):
        @pl.when(pl.program_id(2) == 0)
        def _():
            o_ref[...] = jnp.zeros_like(o_ref)

        prod = _dot_tn(a_ref[...], b_ref[...])
        if stacked:
            o_ref[0] += prod
        else:
            o_ref[...] += prod

    if stacked:
        out_spec = pl.BlockSpec((1, tk, tn), lambda k, n, t: (n, k, 0))
        out_shape = _sds((N // tn, K, tn), F32)
    else:
        out_spec = pl.BlockSpec((tk, tn), lambda k, n, t: (k, n))
        out_shape = _sds((K, N), F32)
    return pl.pallas_call(
        body, grid=(K // tk, N // tn, T // tt), name=name,
        in_specs=[pl.BlockSpec((tt, tk), lambda k, n, t: (t, k)),
                  pl.BlockSpec((tt, tn), lambda k, n, t: (t, n))],
        out_specs=out_spec, out_shape=out_shape,
        compiler_params=_cp(40),
    )(*_hbm(a, b))


def _mix_bwd(dpre2, dx1m, xh1, rstd1, g1, w_out, on, pooled, tm, rider):
    S, D = xh1.shape
    n_t = S // tm

    def body(*refs):
        ((dp2_ref, dxm_ref, xh_ref, rstd_ref, g_ref, w_ref, on_ref, po_ref),
         (dpre1_ref, don_ref, dpo_ref, dw_ref, dg_ref, db_ref), _, ride) = rider.split(refs, 8, 6, 0)

        @pl.when(pl.program_id(0) == 0)
        def _():
            rider.first(ride)
            for r in (dw_ref, dg_ref, db_ref):
                r[...] = jnp.zeros_like(r)

        xh = xh_ref[...]
        dx1 = ALPHA * dp2_ref[...] + dxm_ref[...]
        dg_ref[...] += _colsum(dx1 * xh)
        db_ref[...] += _colsum(dx1)
        dpre1 = _ln_bwd(dx1, xh, rstd_ref[...], g_ref[...])
        dpre1_ref[...] = dpre1
        dmb = dpre1.astype(BF16)
        dcat = _dot_nt(dmb, w_ref[...])
        don_ref[...] = dcat[:, :ATTN_WIDTH]
        dpo_ref[...] = dcat[:, ATTN_WIDTH:]
        dw_ref[:ATTN_WIDTH, :] += _dot_tn(on_ref[...], dmb)
        dw_ref[ATTN_WIDTH:, :] += _dot_tn(po_ref[...], dmb)

        @pl.when(pl.program_id(0) == n_t - 1)
        def _():
            rider.last(ride)

    vec = _const_spec((1, D))
    return rider.call(
        body, [dpre2, dx1m, xh1, rstd1, g1, w_out, on, pooled], grid=(n_t,), name="mix_bwd",
        in_specs=[_row_spec(tm, D), _row_spec(tm, D), _row_spec(tm, D), _row_spec(tm, 1), vec,
                  _const_spec((D, D)), _row_spec(tm, ATTN_WIDTH), _row_spec(tm, POOL_WIDTH)],
        out_specs=[_row_spec(tm, D), _row_spec(tm, ATTN_WIDTH), _row_spec(tm, POOL_WIDTH),
                   _const_spec((D, D)), vec, vec],
        out_shape=[_sds((S, D), F32), _sds((S, ATTN_WIDTH), F32), _sds((S, POOL_WIDTH), F32),
                   _sds((D, D), F32), _sds((1, D), F32), _sds((1, D), F32)],
        scratch_shapes=[], vmem_mib=48)


def _pool_bwd(dpooled, d_b, w_pool, pscale, tm, rider):
    S = dpooled.shape[0]
    hb = tm // POOL_HALO
    n_t = S // tm
    te = tm + POOL_HALO

    def body(*refs):
        ((dp_ref, dph_ref, d_ref, wp_ref, sc_ref), (du_ref, dwp_ref, dsc_ref), _,
         ride) = rider.split(refs, 5, 3, 0)
        i = pl.program_id(0)

        @pl.when(i == 0)
        def _():
            rider.first(ride)
            dwp_ref[...] = jnp.zeros_like(dwp_ref)
            dsc_ref[...] = jnp.zeros_like(dsc_ref)

        halo = jnp.where(i < n_t - 1, dph_ref[...], 0.0)
        pos = i * tm + lax.broadcasted_iota(jnp.int32, (te, 1), 0)
        for g in range(N_POOL_GROUPS):
            win = 2 ** (g + 1)
            cols = slice(g * POOL_GROUP, (g + 1) * POOL_GROUP)
            wpb = wp_ref[g].astype(BF16)
            dpt = dp_ref[:, cols]
            dpe = jnp.concatenate([dpt, halo[:, cols]], axis=0)
            dyb = (dpe * sc_ref[:, cols]).astype(BF16)
            dd = _dot_nt(dyb, wpb)
            s = dd / jnp.minimum(pos + 1, win).astype(F32)
            for sh in (1, 2, 4, 8)[:g + 1]:
                s = s + pltpu.roll(s, te - sh, 0)
            du_ref[:, cols] = s[:tm, :] - dd[:tm, :]
            db = d_ref[:, cols]
            dwp_ref[g] += _dot_tn(db, dyb[:tm, :])
            dsc_ref[:, cols] += _colsum(dpt * _dot(db, wpb))

        @pl.when(i == n_t - 1)
        def _():
            rider.last(ride)

    return rider.call(
        body, [dpooled, dpooled, d_b, w_pool, pscale], grid=(n_t,), name="pool_bwd",
        in_specs=[_row_spec(tm, POOL_WIDTH),
                  pl.BlockSpec((POOL_HALO, POOL_WIDTH),
                               lambda i: (jnp.minimum((i + 1) * hb, S // POOL_HALO - 1), 0)),
                  _row_spec(tm, POOL_WIDTH),
                  _const_spec((N_POOL_GROUPS, POOL_GROUP, POOL_GROUP)), _const_spec((1, POOL_WIDTH))],
        out_specs=[_row_spec(tm, POOL_WIDTH), _const_spec((N_POOL_GROUPS, POOL_GROUP, POOL_GROUP)),
                   _const_spec((1, POOL_WIDTH))],
        out_shape=[_sds((S, POOL_WIDTH), F32), _sds((N_POOL_GROUPS, POOL_GROUP, POOL_GROUP), F32),
                   _sds((1, POOL_WIDTH), F32)],
        scratch_shapes=[], vmem_mib=32)


def _attn_bwd(q, k, v, don, o_raw, ga, tq, rider):
    S = q.shape[0]
    nq = S // tq

    def body(*refs):
        ((q_ref, k_ref, v_ref, don_ref, o_ref, ga_ref), (dq_ref, dk_ref, dv_ref, dga_ref),
         (g_s, b_s), ride) = rider.split(refs, 6, 4, 2)
        p, i = pl.program_id(0), pl.program_id(1)

        @pl.when(jnp.logical_and(p == 0, i == 0))
        def _():
            rider.first(ride)

        @pl.when(i == 0)
        def _():
            for r in (dk_ref, dv_ref, dga_ref):
                r[...] = jnp.zeros_like(r)

        lane = lax.broadcasted_iota(jnp.int32, (1, PAIR), 1)
        m0 = lane < HEAD_DIM
        low = _tri(tq, upper=False)
        upp = _tri(tq, upper=True)

        def seg_mean(a):
            s0 = jnp.sum(jnp.where(m0, a, 0.0), axis=-1, keepdims=True)
            s1 = jnp.sum(jnp.where(m0, 0.0, a), axis=-1, keepdims=True)
            return jnp.where(m0, s0, s1) * (1.0 / HEAD_DIM)

        o = o_ref[...]
        rs = lax.rsqrt(seg_mean(o * o) + RMS_EPS)
        oh = o * rs
        don = don_ref[...]
        dga_ref[...] += _colsum(don * oh)
        doh = don * ga_ref[...]
        do = rs * (doh - oh * seg_mean(doh * oh))
        dob = do.astype(BF16)
        q2 = q_ref[...]
        qhs = [jnp.where(m0, q2, jnp.zeros_like(q2)), jnp.where(m0, jnp.zeros_like(q2), q2)]
        dhs = [jnp.where(m0, dob, jnp.zeros_like(dob)), jnp.where(m0, jnp.zeros_like(dob), dob)]
        causal = _strictly_causal(tq)

        def down(kb, c_ls, valid):
            ks = pl.multiple_of(kb * tq, tq)
            kt = k_ref[pl.ds(ks, tq), :]
            vt = v_ref[pl.ds(ks, tq), :]
            lss, ls_, ws = _sb_tile(qhs, kt, low, c_ls, valid)
            dws = [_dot_nt(dh, vt) for dh in dhs]
            for hh in range(2):
                g_s[hh, kb] = dws[hh] * ws[hh]
                b_s[hh, kb] = jnp.exp(lss[hh])
            dv_ref[pl.ds(ks, tq), :] += (_dot_tn(ws[0].astype(BF16), dhs[0])
                                         + _dot_tn(ws[1].astype(BF16), dhs[1]))
            return [c_l + jnp.sum(l, axis=1, keepdims=True) for c_l, l in zip(c_ls, ls_)]

        zc, za = jnp.zeros((tq, 1), F32), jnp.zeros((tq, PAIR), F32)
        c_ls = lax.cond(i >= 1, lambda: tuple(down(i - 1, down(i, [zc, zc], causal), None)),
                        lambda: tuple(down(i, [zc, zc], causal)))

        def more(st):
            return jnp.logical_and(st[0] <= i, jnp.max(jnp.maximum(st[1], st[2])) > LOG_WEIGHT_FLOOR)

        def down_step(st):
            c_ls = down(i - st[0], [st[1], st[2]], None)
            return (st[0] + 1, c_ls[0], c_ls[1])

        n_tiles = lax.while_loop(more, down_step, (jnp.int32(2), c_ls[0], c_ls[1]))[0]

        def up(kb, c_gs, accs, valid):
            ks = pl.multiple_of(kb * tq, tq)
            kt = k_ref[pl.ds(ks, tq), :]
            gs = [g_s[hh, kb] for hh in range(2)]
            pres = [_dot(g.astype(BF16), upp) + c_g for g, c_g in zip(gs, c_gs)]
            dzs = []
            for hh in range(2):
                beta = b_s[hh, kb]
                dz = gs[hh] * (1.0 - beta) - beta * pres[hh]
                if valid is not None:
                    dz = jnp.where(valid, dz, 0.0)
                dzs.append(dz.astype(BF16))
            new_a = [acc + _dot(dzb, kt) for acc, dzb in zip(accs, dzs)]
            dk_ref[pl.ds(ks, tq), :] += _dot_tn(dzs[0], qhs[0]) + _dot_tn(dzs[1], qhs[1])
            new_c = [c_g + jnp.sum(g, axis=1, keepdims=True) for c_g, g in zip(c_gs, gs)]
            return new_c, new_a

        def up_step(kb, st):
            c_gs, accs = up(kb, [st[0], st[1]], [st[2], st[3]], None)
            return (c_gs[0], c_gs[1], accs[0], accs[1])

        st = lax.fori_loop(i - n_tiles + 1, i - 1, up_step, (zc, zc, za, za))

        def last_two():
            c_gs, accs = up(i - 1, [st[0], st[1]], [st[2], st[3]], None)
            return tuple(up(i, c_gs, accs, causal)[1])

        accs = lax.cond(i >= 1, last_two, lambda: tuple(up(i, [zc, zc], [za, za], causal)[1]))
        dq_ref[...] = jnp.where(m0, accs[0], accs[1]) * Q_SCALE

        @pl.when(jnp.logical_and(p == N_PAIRS - 1, i == nq - 1))
        def _():
            rider.last(ride)

    return rider.call(
        body, [q, k, v, don, o_raw, ga], grid=(N_PAIRS, nq), name="attn_bwd",
        in_specs=[pl.BlockSpec((tq, PAIR), lambda p, i: (i, p)),
                  pl.BlockSpec((S, PAIR), lambda p, i: (0, p)),
                  pl.BlockSpec((S, PAIR), lambda p, i: (0, p)),
                  pl.BlockSpec((tq, PAIR), lambda p, i: (i, p)),
                  pl.BlockSpec((tq, PAIR), lambda p, i: (i, p)),
                  pl.BlockSpec((1, PAIR), lambda p, i: (0, p))],
        out_specs=[pl.BlockSpec((tq, PAIR), lambda p, i: (i, p)),
                   pl.BlockSpec((S, PAIR), lambda p, i: (0, p)),
                   pl.BlockSpec((S, PAIR), lambda p, i: (0, p)),
                   pl.BlockSpec((1, PAIR), lambda p, i: (0, p))],
        out_shape=[_sds((S, ATTN_WIDTH), F32), _sds((S, ATTN_WIDTH), F32), _sds((S, ATTN_WIDTH), F32),
                   _sds((1, ATTN_WIDTH), F32)],
        scratch_shapes=[pltpu.VMEM((2, nq, tq, tq), F32), pltpu.VMEM((2, nq, tq, tq), F32)],
        vmem_mib=56)


def _inproj_bwd(dq, dk, dv, du, dpre1, xh0, rstd0, g0, b0, w_in_s, tm):
    S, D = xh0.shape

    def body(dq_ref, dk_ref, dv_ref, du_ref, dp1_ref, xh_ref, rstd_ref, g_ref, b_ref, w_ref,
             gx_ref, dw_ref, dg_ref, db_ref):
        @pl.when(pl.program_id(0) == 0)
        def _():
            for r in (dw_ref, dg_ref, db_ref):
                r[...] = jnp.zeros_like(r)

        xh = xh_ref[...]
        xb = (xh * g_ref[...] + b_ref[...]).astype(BF16)
        dx0 = ALPHA * dp1_ref[...]
        for c, r in enumerate((dq_ref, dk_ref, dv_ref, du_ref)):
            dpb = r[...].astype(BF16)
            dx0 = dx0 + _dot_nt(dpb, w_ref[c])
            dw_ref[c] += _dot_tn(xb, dpb)
        dg_ref[...] += _colsum(dx0 * xh)
        db_ref[...] += _colsum(dx0)
        gx_ref[...] = _ln_bwd(dx0, xh, rstd_ref[...], g_ref[...])

    vec = _const_spec((1, D))
    half = _row_spec(tm, 512)
    return pl.pallas_call(
        body, grid=(S // tm,), name="inproj_bwd",
        in_specs=[half, half, half, half, _row_spec(tm, D), _row_spec(tm, D), _row_spec(tm, 1), vec, vec,
                  _const_spec((N_CHIPS, D, 512))],
        out_specs=[_row_spec(tm, D), _const_spec((N_CHIPS, D, 512)), vec, vec],
        out_shape=[_sds((S, D), F32), _sds((N_CHIPS, D, 512), F32), _sds((1, D), F32), _sds((1, D), F32)],
        compiler_params=_cp(56),
    )(*_hbm(dq, dk, dv, du, dpre1, xh0, rstd0, g0, b0, w_in_s))


def _place():
    return lax.axis_index("x"), lax.axis_index("y"), lax.axis_index("c")


CHIP_FLIPS = ((0, 1), (1, 0), (1, 1))


class _Rider:
    def __init__(self, ins, out_shapes, n_sem, phases, aliases=None):
        self.ins, self.out_shapes, self.n_sem, self.phases = list(ins), list(out_shapes), n_sem, phases
        self.aliases = aliases or {}

    def __add__(self, other):
        na, ma = len(self.ins), len(self.out_shapes)

        def phases(ins, outs, ssem, rsem):
            mine = self.phases(ins[:na], outs[:ma], ssem, rsem)
            rest = pl.ds(self.n_sem, other.n_sem)
            theirs = other.phases(ins[na:], outs[ma:], ssem.at[rest], rsem.at[rest])
            assert len(mine) == 1 and len(theirs) == 1
            return [mine[0] + theirs[0]]

        aliases = {**self.aliases, **{na + i: ma + o for i, o in other.aliases.items()}}
        return _Rider(self.ins + other.ins, self.out_shapes + other.out_shapes, self.n_sem + other.n_sem, phases,
                      aliases)

    def split(self, refs, n_in, n_out, n_scratch):
        a = n_in + len(self.ins)
        b = a + n_out
        c = b + len(self.out_shapes)
        own = (refs[:n_in], refs[a:b], refs[c:c + n_scratch])
        return own + ((refs[n_in:a], refs[b:c]) + tuple(refs[c + n_scratch:]),)

    def first(self, ride):
        for make in self.phases(*ride)[0]:
            make().start()

    def mid(self, ride):
        ph = self.phases(*ride)
        if len(ph) == 2:
            for make in ph[0]:
                make().wait_recv()
            for make in ph[1]:
                make().start()

    def last(self, ride):
        ph = self.phases(*ride)
        if len(ph) == 2:
            for make in ph[0]:
                make().wait_send()
        for make in ph[-1]:
            make().wait()

    def call(self, body, args, *, grid, name, in_specs, out_specs, out_shape, scratch_shapes, vmem_mib,
             prefetch=None):
        n_in, n_out = len(in_specs), len(out_specs)
        sems = [pltpu.SemaphoreType.DMA((self.n_sem,)), pltpu.SemaphoreType.DMA((self.n_sem,))]
        n_pre = 0 if prefetch is None else 1
        grid_spec = pltpu.PrefetchScalarGridSpec(
            num_scalar_prefetch=n_pre, grid=grid,
            in_specs=list(in_specs) + [HBM_SPEC] * len(self.ins),
            out_specs=list(out_specs) + [HBM_SPEC] * len(self.out_shapes),
            scratch_shapes=list(scratch_shapes) + sems)
        return pl.pallas_call(
            body, name=name, grid_spec=grid_spec,
            out_shape=list(out_shape) + self.out_shapes,
            input_output_aliases={n_pre + n_in + i: n_out + o for i, o in self.aliases.items()},
            compiler_params=_cp(vmem_mib),
        )(*([] if prefetch is None else [prefetch]), *_hbm(*args), *self.ins)

    def run(self, name):
        def body(*refs):
            ride = self.split(refs, 0, 0, 0)[3]
            self.first(ride)
            self.mid(ride)
            self.last(ride)

        return self.call(body, [], grid=(), name=name, in_specs=[], out_specs=[], out_shape=[],
                         scratch_shapes=[], vmem_mib=16)


def _remote(src, dst, ssem, rsem, n, dev):
    return functools.partial(pltpu.make_async_remote_copy, src_ref=src, dst_ref=dst, send_sem=ssem.at[n],
                             recv_sem=rsem.at[n], device_id=dev, device_id_type=MESH)


def _cast_into_slot(w, place, name):
    R, C = w.shape
    tr = min(R, 512)

    def body(pl_ref, w_ref, o_ref):
        o_ref[0] = w_ref[...].astype(BF16)

    return pl.pallas_call(
        body, name=name,
        grid_spec=pltpu.PrefetchScalarGridSpec(
            num_scalar_prefetch=1, grid=(R // tr,),
            in_specs=[pl.BlockSpec((tr, C), lambda r, pr: (r, 0))],
            out_specs=pl.BlockSpec((1, tr, C), lambda r, pr: (pr[1], r, 0))),
        out_shape=_sds((N_CHIPS, R, C), BF16),
    )(place, w)


CAST_STEPS = 8


def _cast_rest(ws, place, rider):
    n = len(ws)

    def body(pl_ref, *refs):
        w_refs, o_refs, _, ride = rider.split(refs, n, n, 0)
        r = pl.program_id(0)

        @pl.when(r == 0)
        def _():
            rider.first(ride)

        @pl.when(r == CAST_STEPS // 2)
        def _():
            rider.mid(ride)

        for w_ref, o_ref in zip(w_refs, o_refs):
            o_ref[0] = w_ref[...].astype(BF16)

        @pl.when(r == CAST_STEPS - 1)
        def _():
            rider.last(ride)

    def rows(w):
        return w.shape[0] // CAST_STEPS

    return rider.call(
        body, ws, grid=(CAST_STEPS,), name="cast_weights", prefetch=place,
        in_specs=[pl.BlockSpec((rows(w), w.shape[1]), lambda r, pr: (r, 0)) for w in ws],
        out_specs=[pl.BlockSpec((1, rows(w), w.shape[1]), lambda r, pr: (pr[1], r, 0)) for w in ws],
        out_shape=[_sds((N_CHIPS,) + w.shape, BF16) for w in ws], scratch_shapes=[], vmem_mib=32)


def _gather_rider(stacked, part="both"):
    n, nf = len(stacked), len(CHIP_FLIPS)

    def phases(ins, outs, ssem, rsem):
        x, y, c = _place()
        slot = 2 * x + y
        ici, d2d = [], []
        for w, (i_ref, o_ref) in enumerate(zip(ins, outs)):
            hh = o_ref.shape[1] // 2
            rows = pl.ds(c * hh, hh)
            for f, (fx, fy) in enumerate(CHIP_FLIPS):
                k = w * nf + f
                theirs = 2 * (x ^ fx) + (y ^ fy)
                if part != "pair":
                    ici.append(_remote(i_ref.at[slot, rows], o_ref.at[slot, rows], ssem, rsem, k,
                                       (x ^ fx, y ^ fy, c)))
                if part != "chips":
                    d2d.append(_remote(o_ref.at[theirs, rows], o_ref.at[theirs, rows], ssem, rsem,
                                       (n * nf if part == "both" else 0) + k, (x, y, 1 - c)))
        return [ph for ph in (ici, d2d) if ph]

    return _Rider(stacked, [_sds(s.shape, s.dtype) for s in stacked], (2 if part == "both" else 1) * n * nf,
                  phases, aliases={i: i for i in range(n)})


def _pair_swap_rider(grads):
    def phases(ins, outs, ssem, rsem):
        x, y, c = _place()
        return [[_remote(g.at[:, 1 - c], o, ssem, rsem, k, (x, y, 1 - c))
                 for k, (g, o) in enumerate(zip(ins, outs))]]

    return _Rider(grads, [_sds((N_CHIPS,) + g.shape[2:], g.dtype) for g in grads], len(grads), phases)


def _chip_scatter_rider(parts):
    nf = len(CHIP_FLIPS)

    def phases(ins, outs, ssem, rsem):
        x, y, c = _place()
        return [[_remote(r.at[2 * (x ^ fx) + (y ^ fy)], o.at[f], ssem, rsem, w * nf + f, (x ^ fx, y ^ fy, c))
                 for w, (r, o) in enumerate(zip(ins, outs)) for f, (fx, fy) in enumerate(CHIP_FLIPS)]]

    return _Rider(parts, [_sds((nf,) + r.shape[1:], r.dtype) for r in parts], len(parts) * nf, phases)


def _pair_send_rider(halves):
    def phases(ins, outs, ssem, rsem):
        x, y, c = _place()
        return [[_remote(h, o, ssem, rsem, k, (x, y, 1 - c)) for k, (h, o) in enumerate(zip(ins, outs))]]

    return _Rider(halves, [_sds(h.shape, h.dtype) for h in halves], len(halves), phases)


def _add_pair(grad, recv, place, name):
    _, _, H, C = grad.shape
    th = min(H, 256)

    def body(pl_ref, g_ref, r_ref, o_ref, ob_ref):
        s = g_ref[:, 0] + r_ref[...]
        o_ref[...] = s
        ob_ref[...] = s.astype(BF16)

    spec = pl.BlockSpec((1, th, C), lambda j, h, pr: (j, h, 0))
    return pl.pallas_call(
        body, name=name,
        grid_spec=pltpu.PrefetchScalarGridSpec(
            num_scalar_prefetch=1, grid=(N_CHIPS, H // th),
            in_specs=[pl.BlockSpec((1, 1, th, C), lambda j, h, pr: (j, pr[0], h, 0)), spec],
            out_specs=[spec, spec]),
        out_shape=[_sds((N_CHIPS, H, C), F32), _sds((N_CHIPS, H, C), BF16)],
    )(place, *_hbm(grad, recv))


def _add_chips(part, recv, place, name):
    _, H, C = part.shape
    th = min(H, 256)

    def body(pl_ref, p_ref, r_ref, o_ref):
        o_ref[...] = p_ref[0] + r_ref[0].astype(F32) + r_ref[1].astype(F32) + r_ref[2].astype(F32)

    return pl.pallas_call(
        body, name=name,
        grid_spec=pltpu.PrefetchScalarGridSpec(
            num_scalar_prefetch=1, grid=(H // th,),
            in_specs=[pl.BlockSpec((1, th, C), lambda h, pr: (pr[1], h, 0)),
                      pl.BlockSpec((len(CHIP_FLIPS), th, C), lambda h, pr: (0, h, 0))],
            out_specs=pl.BlockSpec((th, C), lambda h, pr: (h, 0))),
        out_shape=_sds((H, C), F32),
    )(place, *_hbm(part, recv))


def _adamw_math(w, g, m, v):
    m = ADAM_B1 * m + (1.0 - ADAM_B1) * g
    v = ADAM_B2 * v + (1.0 - ADAM_B2) * (g * g)
    m_hat = m / (1.0 - ADAM_B1 ** ADAM_STEP)
    v_hat = v / (1.0 - ADAM_B2 ** ADAM_STEP)
    delta = -ADAM_LR * (m_hat / (jnp.sqrt(v_hat) + ADAM_EPS) + ADAM_WD * w)
    return delta, m, v


def _adamw(w, mine, theirs, m, v, place, name):
    R, C = w.shape
    th = min(R // 2, 256)
    nb = (R // 2) // th

    def body(pl_ref, w_ref, a_ref, b_ref, m_ref, v_ref, g_ref, d_ref, mo_ref, vo_ref):
        g = jnp.where(pl.program_id(0) == pl_ref[0], a_ref[...], b_ref[...])
        g_ref[...] = g
        d, mo, vo = _adamw_math(w_ref[...], g, m_ref[...], v_ref[...])
        d_ref[...] = d
        mo_ref[...] = mo
        vo_ref[...] = vo

    whole = pl.BlockSpec((th, C), lambda h, r, pr: (h * nb + r, 0))
    half = pl.BlockSpec((th, C), lambda h, r, pr: (r, 0))
    return pl.pallas_call(
        body, name=name,
        grid_spec=pltpu.PrefetchScalarGridSpec(
            num_scalar_prefetch=1, grid=(2, nb),
            in_specs=[whole, half, half, whole, whole], out_specs=[whole] * 4),
        out_shape=[_sds((R, C), F32)] * 4,
    )(place, *_hbm(w, mine, theirs, m, v))


DEVICE_FLIPS = tuple((fx, fy, fc) for fx in (0, 1) for fy in (0, 1) for fc in (0, 1))[1:]


def _pack_exchange_rider(pack):
    def phases(ins, outs, ssem, rsem):
        x, y, c = _place()
        mine = outs[0].at[4 * x + 2 * y + c]
        copies = [_remote(ins[0], mine, ssem, rsem, k, (x ^ fx, y ^ fy, c ^ fc))
                  for k, (fx, fy, fc) in enumerate(DEVICE_FLIPS)]
        copies.append(functools.partial(pltpu.make_async_copy, ins[0], mine, ssem.at[len(DEVICE_FLIPS)]))
        return [copies]

    return _Rider([pack], [_sds((N_DEV,) + pack.shape, pack.dtype)], len(DEVICE_FLIPS) + 1, phases)


def _small_sum_adamw(recv_a, recv_b, wpack, mpack, vpack):
    R = wpack.shape[0]

    def body(a_ref, b_ref, w_ref, m_ref, v_ref, gs_ref, d_ref, mo_ref, vo_ref):
        ta, tb = a_ref[0], b_ref[0]
        for dev in range(1, N_DEV):
            ta = ta + a_ref[dev]
            tb = tb + b_ref[dev]
        total = jnp.concatenate([ta, tb], axis=0)
        gs_ref[...] = total
        d, mo, vo = _adamw_math(w_ref[...], total, m_ref[...], v_ref[...])
        d_ref[...] = d
        mo_ref[...] = mo
        vo_ref[...] = vo

    return pl.pallas_call(
        body, name="small_sum_adamw", in_specs=[VMEM_SPEC] * 5, out_specs=[VMEM_SPEC] * 4,
        out_shape=[_sds((R, LANES), F32)] * 4,
    )(recv_a, recv_b, wpack, mpack, vpack)


def _rows8(a):
    a = a.reshape(-1, LANES)
    pad = (-a.shape[0]) % 8
    return jnp.pad(a, ((0, pad), (0, 0))) if pad else a


def _pack(parts):
    return jnp.concatenate([_rows8(a) for a in parts], axis=0)


def _unpack(pack, like):
    out, row = [], 0
    for a in like:
        n = a.size // LANES
        out.append(pack[row:row + n].reshape(a.shape))
        row += n + (-n) % 8
    return out


def kernel(x, p, emb_ln_g, emb_ln_b, w_in, attn_out_g, w_pool, pool_scale, w_out, ln1_g, ln1_b, w_up, w_down, ln2_g, ln2_b, w_ple, w_ple_gate, ln3_g, ln3_b, loss_target, m_emb_ln_g, m_emb_ln_b, m_w_in, m_attn_out_g, m_w_pool, m_pool_scale, m_w_out, m_ln1_g, m_ln1_b, m_w_up, m_w_down, m_ln2_g, m_ln2_b, m_w_ple, m_w_ple_gate, m_ln3_g, m_ln3_b, v_emb_ln_g, v_emb_ln_b, v_w_in, v_attn_out_g, v_w_pool, v_pool_scale, v_w_out, v_ln1_g, v_ln1_b, v_w_up, v_w_down, v_ln2_g, v_ln2_b, v_w_ple, v_w_ple_gate, v_ln3_g, v_ln3_b):
    S = x.shape[1]
    tm = min(256, S)
    tq = min(256, S)
    tm_mlp = min(512, S)
    xs = x[0]
    ps = p[0, 0]
    tgt = loss_target[0]
    row = lambda a: a.reshape(1, -1)
    g0, b0 = row(emb_ln_g), row(emb_ln_b)
    g1, b1, g2, b2, g3, b3 = ln1_g, ln1_b, ln2_g, ln2_b, ln3_g, ln3_b
    wp = w_pool[0]

    xi, yi, ci = _place()
    place = jnp.stack([ci, 2 * xi + yi]).astype(jnp.int32)
    names = ["w_in", "w_out", "w_up", "w_down", "w_ple", "w_ple_gate"]

    big = [w_in[0], w_out[0], w_up[0], w_down[0], w_ple[0], w_ple_gate[0]]
    s_in = _cast_into_slot(big[0], place, "cast_w_in")
    s_out, s_up, s_down, s_ple, s_gate, w_in_s = _cast_rest(big[1:], place, _gather_rider([s_in]))

    xh0, rstd0, q, k, v, u, s_out, s_ple, s_gate = _embln_inproj(
        xs, g0, b0, w_in_s, tm, _gather_rider([s_out, s_ple, s_gate], "chips"))
    o_raw, on, s_up, s_down, w_out_s, w_ple_s, w_gate_s = _attn_fwd(
        q, k, v, attn_out_g, tq, _gather_rider([s_up, s_down], "chips") + _gather_rider([s_out, s_ple, s_gate], "pair"))
    w_out_f = w_out_s.reshape(D_MODEL, D_MODEL)
    w_gate_f = w_gate_s.reshape(D_MODEL, D_MODEL)
    d_b, pooled = _pool_fwd(u, wp, pool_scale, tm)
    xh1, rstd1, x1b, w_up_s, w_down_s = _mix_ln1(on, pooled, xh0, g0, b0, w_out_f, g1, b1, tm,
                                                 _gather_rider([s_up, s_down], "pair"))
    w_down_f = w_down_s.reshape(D_FF, D_MODEL)
    xh2, rstd2 = _mlp_ln2(xh1, x1b, g1, b1, w_up_s, w_down_f, tm_mlp)

    (dpre2, dhb, dw_ple, dw_gate, dg3, db3, dg2, db2, loss_row) = _ple_ln3_loss(
        xh2, rstd2, g2, b2, ps, w_ple_s, w_gate_f, g3, b3, tgt, tm)
    dx1m, da, h1 = _mlp_bwd(x1b, dhb, w_up_s, w_down_f, tm_mlp)
    dw_up = _tn_matmul(x1b, da, "grad_w_up", 1024, min(512, S), stacked=True)
    dw_down = _tn_matmul(h1, dhb, "grad_w_down", 1024, min(512, S), stacked=False)
    def halves_of(g):
        return g.reshape(N_CHIPS, 2, g.shape[1] // 2, g.shape[2])

    early_names = names[2:]
    early = [halves_of(g) for g in (dw_up, dw_down.reshape(N_CHIPS, D_FF // N_CHIPS, D_MODEL), dw_ple,
                                    dw_gate.reshape(N_CHIPS, D_MODEL // N_CHIPS, D_MODEL))]
    dpre1, don, dpooled, dw_out, dg1, db1, *early_pair = _mix_bwd(
        dpre2, dx1m, xh1, rstd1, g1, w_out_f, on, pooled, tm, _pair_swap_rider(early))
    early_sum = [_add_pair(g, r, place, "pair_sum_" + n) for g, r, n in zip(early, early_pair, early_names)]
    out_halves = halves_of(dw_out.reshape(N_CHIPS, D_MODEL // N_CHIPS, D_MODEL))
    du, dwp, dsc, out_pair = _pool_bwd(dpooled, d_b, wp, pool_scale, tm, _pair_swap_rider([out_halves]))
    out_sum = _add_pair(out_halves, out_pair, place, "pair_sum_w_out")
    pack_a = _pack([jnp.broadcast_to(loss_row, (8, LANES)), dwp, dsc, dg1, db1, dg2, db2, dg3, db3])
    riding = _chip_scatter_rider([out_sum[1]] + [b for _, b in early_sum]) + _pack_exchange_rider(pack_a)
    dq, dk, dv, dga, *arrived = _attn_bwd(q, k, v, don, o_raw, attn_out_g, tq, riding)
    early_chips, recv_a = arrived[:-1], arrived[-1]
    grad_x, dw_in, dg0, db0 = _inproj_bwd(dq, dk, dv, du, dpre1, xh0, rstd0, g0, b0, w_in_s, tm)

    in_halves = halves_of(dw_in)
    pack_b = _pack([dg0, db0, dga])
    in_pair, recv_b = (_pair_swap_rider([in_halves]) + _pack_exchange_rider(pack_b)).run("reduce_pair_late")
    in_sum = _add_pair(in_halves, in_pair, place, "pair_sum_w_in")
    (in_chips,) = _chip_scatter_rider([in_sum[1]]).run("reduce_chips_late")
    mine = [_add_chips(s, r, place, "chip_sum_" + n)
            for (s, _), r, n in zip([in_sum, out_sum] + early_sum, [in_chips] + early_chips, names)]
    theirs = _pair_send_rider(mine).run("gather_pair")

    ms = [m_w_in, m_w_out, m_w_up, m_w_down, m_w_ple, m_w_ple_gate]
    vs = [v_w_in, v_w_out, v_w_up, v_w_down, v_w_ple, v_w_ple_gate]
    big_out = {}
    for n, w, a, b, m, vv in zip(names, big, mine, theirs, ms, vs):
        res4 = _adamw(w, a, b, m[0], vv[0], place, "adamw_" + n)
        big_out[n] = tuple(r.reshape(m.shape) for r in res4)

    small_names = ["w_pool", "pool_scale", "ln1_g", "ln1_b", "ln2_g", "ln2_b", "ln3_g", "ln3_b",
                   "emb_ln_g", "emb_ln_b", "attn_out_g"]
    small_w = [w_pool, pool_scale, ln1_g, ln1_b, ln2_g, ln2_b, ln3_g, ln3_b, emb_ln_g, emb_ln_b, attn_out_g]
    small_m = [m_w_pool, m_pool_scale, m_ln1_g, m_ln1_b, m_ln2_g, m_ln2_b, m_ln3_g, m_ln3_b,
               m_emb_ln_g, m_emb_ln_b, m_attn_out_g]
    small_v = [v_w_pool, v_pool_scale, v_ln1_g, v_ln1_b, v_ln2_g, v_ln2_b, v_ln3_g, v_ln3_b,
               v_emb_ln_g, v_emb_ln_b, v_attn_out_g]
    loss_like = jnp.zeros((8, LANES), F32)
    gs, ds, mos, vos = _small_sum_adamw(recv_a, recv_b, _pack([loss_like] + small_w), _pack([loss_like] + small_m),
                                        _pack([jnp.ones((8, LANES), F32)] + small_v))
    like = [loss_like] + small_w
    gs_u, ds_u, mos_u, vos_u = (_unpack(a, like) for a in (gs, ds, mos, vos))
    loss = gs_u[0][0, 0]
    small_out = {n: (gs_u[i + 1], ds_u[i + 1], mos_u[i + 1], vos_u[i + 1]) for i, n in enumerate(small_names)}

    order = ["emb_ln_g", "emb_ln_b", "w_in", "attn_out_g", "w_pool", "pool_scale", "w_out", "ln1_g", "ln1_b",
             "w_up", "w_down", "ln2_g", "ln2_b", "w_ple", "w_ple_gate", "ln3_g", "ln3_b"]
    res = {**big_out, **small_out}
    outs = [loss, grad_x.reshape(x.shape)]
    for kind in range(4):
        outs += [res[n][kind] for n in order]
    return tuple(outs)
```

```python
import functools

import jax
import jax.numpy as jnp
from jax import lax
from jax.experimental import pallas as pl
from jax.experimental.pallas import tpu as pltpu

F32 = jnp.float32
BF16 = jnp.bfloat16

D_MODEL = 1024
ATTN_WIDTH = 512
POOL_WIDTH = 512
HEAD_DIM = 64
PAIR = 2 * HEAD_DIM
N_PAIRS = ATTN_WIDTH // PAIR
N_POOL_GROUPS = 4
POOL_GROUP = 128
POOL_HALO = 16
D_FF = 4096
PLE_DIM = 256
N_CHIPS = 4
N_DEV = 8
LN_EPS = 1e-5
RMS_EPS = 1e-6
ALPHA = float(2.0 ** 0.25)
Q_SCALE = 0.125
ADAM_LR = 0.001
ADAM_B1 = 0.9
ADAM_B2 = 0.999
ADAM_EPS = 1e-08
ADAM_WD = 0.01
ADAM_STEP = 10
LANES = 128
MIB = 1024 * 1024

MESH = pl.DeviceIdType.MESH
HBM_SPEC = pl.BlockSpec(memory_space=pltpu.HBM)
VMEM_SPEC = pl.BlockSpec(memory_space=pltpu.VMEM)


def _cp(vmem_mib):
    return pltpu.CompilerParams(vmem_limit_bytes=vmem_mib * MIB)


def _dot(a, b):
    return jnp.dot(a, b, preferred_element_type=F32)


def _dot_nt(a, b):
    return lax.dot_general(a, b, (((1,), (1,)), ((), ())), preferred_element_type=F32)


def _dot_tn(a, b):
    return lax.dot_general(a, b, (((0,), (0,)), ((), ())), preferred_element_type=F32)


def _ln_fwd(pre):
    mu = jnp.mean(pre, axis=-1, keepdims=True)
    xc = pre - mu
    var = jnp.mean(xc * xc, axis=-1, keepdims=True)
    rstd = lax.rsqrt(var + LN_EPS)
    return xc * rstd, rstd


def _ln_bwd(dy, xh, rstd, g):
    dxh = dy * g
    m1 = jnp.mean(dxh, axis=-1, keepdims=True)
    m2 = jnp.mean(dxh * xh, axis=-1, keepdims=True)
    return rstd * (dxh - m1 - xh * m2)


def _colsum(a):
    return jnp.sum(a, axis=0, keepdims=True)


def _neg_softplus(z):
    return -(jnp.maximum(z, 0.0) + jnp.log(1.0 + jnp.exp(-jnp.abs(z))))


def _split_bf16(a):
    hi = a.astype(BF16)
    lo = (a - hi.astype(F32)).astype(BF16)
    return hi, lo


def _row_spec(tm, n):
    return pl.BlockSpec((tm, n), lambda i: (i, 0))


def _const_spec(shape):
    nd = len(shape)
    return pl.BlockSpec(shape, lambda *_: (0,) * nd)


def _hbm(*arrays):
    return [pltpu.with_memory_space_constraint(a, pltpu.HBM) for a in arrays]


def _sds(shape, dtype):
    return pltpu.HBM(shape, dtype)


def _embln_inproj(x, g0, b0, w_in_s, tm, rider):
    S, D = x.shape
    n_t = S // tm

    def body(*refs):
        ((x_ref, g_ref, b_ref, w_ref), (xh_ref, rstd_ref, q_ref, k_ref, v_ref, u_ref), _,
         ride) = rider.split(refs, 4, 6, 0)
        i = pl.program_id(0)

        @pl.when(i == 0)
        def _():
            rider.first(ride)

        @pl.when(i == (3 * n_t) // 4)
        def _():
            rider.mid(ride)

        xh, rstd = _ln_fwd(x_ref[...])
        xh_ref[...] = xh
        rstd_ref[...] = rstd
        xb = (xh * g_ref[...] + b_ref[...]).astype(BF16)
        q_ref[...] = (_dot(xb, w_ref[0]) * Q_SCALE).astype(BF16)
        k_ref[...] = _dot(xb, w_ref[1]).astype(BF16)
        v_ref[...] = _dot(xb, w_ref[2]).astype(BF16)
        u_ref[...] = _dot(xb, w_ref[3])

        @pl.when(i == n_t - 1)
        def _():
            rider.last(ride)

    return rider.call(
        body, [x, g0, b0, w_in_s], grid=(n_t,), name="embln_inproj",
        in_specs=[_row_spec(tm, D), _const_spec((1, D)), _const_spec((1, D)),
                  _const_spec((N_CHIPS, D, 512))],
        out_specs=[_row_spec(tm, D), _row_spec(tm, 1), _row_spec(tm, 512), _row_spec(tm, 512),
                   _row_spec(tm, 512), _row_spec(tm, 512)],
        out_shape=[_sds((S, D), F32), _sds((S, 1), F32), _sds((S, 512), BF16), _sds((S, 512), BF16),
                   _sds((S, 512), BF16), _sds((S, 512), F32)],
        scratch_shapes=[], vmem_mib=40)


def _tri(n, upper):
    r = lax.broadcasted_iota(jnp.int32, (n, n), 0)
    c = lax.broadcasted_iota(jnp.int32, (n, n), 1)
    keep = (r < c) if upper else (r > c)
    return jnp.where(keep, 1.0, 0.0).astype(BF16)


def _strictly_causal(n):
    return lax.broadcasted_iota(jnp.int32, (n, n), 1) < lax.broadcasted_iota(jnp.int32, (n, n), 0)


LOG_WEIGHT_FLOOR = -110.0


def _sb_tile(qhs, kt, low, c_ls, valid):
    valids = valid if isinstance(valid, (list, tuple)) else [valid] * len(qhs)
    zs = [_dot_nt(qh, kt) for qh in qhs]
    lrs = [_neg_softplus(z) for z in zs]
    ls_ = [lr if m is None else jnp.where(m, lr, 0.0) for lr, m in zip(lrs, valids)]
    sfx = [_dot(l.astype(BF16), low) + c_l for l, c_l in zip(ls_, c_ls)]
    lss = [z + lr for z, lr in zip(zs, lrs)]
    ws = [jnp.exp(ls + s) for ls, s in zip(lss, sfx)]
    ws = [w if m is None else jnp.where(m, w, 0.0) for w, m in zip(ws, valids)]
    return lss, ls_, ws


def _attn_fwd(q, k, v, ga, tq, rider):
    S = q.shape[0]
    nq = S // tq

    def body(*refs):
        (q_ref, k_ref, v_ref, ga_ref), (o_ref, on_ref), _, ride = rider.split(refs, 4, 2, 0)
        p, i = pl.program_id(0), pl.program_id(1)

        @pl.when(jnp.logical_and(p == 0, i == 0))
        def _():
            rider.first(ride)

        @pl.when(jnp.logical_and(p == N_PAIRS - 1, i == 0))
        def _():
            rider.mid(ride)

        lane = lax.broadcasted_iota(jnp.int32, (1, PAIR), 1)
        m0 = lane < HEAD_DIM
        low = _tri(tq, upper=False)
        q2 = q_ref[...]
        qhs = [jnp.where(m0, q2, jnp.zeros_like(q2)), jnp.where(m0, jnp.zeros_like(q2), q2)]

        def tile(kb, c_ls, accs, valid):
            ks = pl.multiple_of(kb * tq, tq)
            kt = k_ref[pl.ds(ks, tq), :]
            vt = v_ref[pl.ds(ks, tq), :]
            _, ls_, ws = _sb_tile(qhs, kt, low, c_ls, valid)
            new_a = [acc + _dot(w.astype(BF16), vt) for acc, w in zip(accs, ws)]
            new_c = [c_l + jnp.sum(l, axis=1, keepdims=True) for c_l, l in zip(c_ls, ls_)]
            return new_c, new_a

        zc, za = jnp.zeros((tq, 1), F32), jnp.zeros((tq, PAIR), F32)

        def first_two():
            c_ls, accs = tile(i, [zc, zc], [za, za], _strictly_causal(tq))
            c_ls, accs = tile(i - 1, c_ls, accs, None)
            return (*c_ls, *accs)

        def first_one():
            c_ls, accs = tile(i, [zc, zc], [za, za], _strictly_causal(tq))
            return (*c_ls, *accs)

        st0 = lax.cond(i >= 1, first_two, first_one)

        def more(st):
            return jnp.logical_and(st[0] <= i, jnp.max(jnp.maximum(st[1], st[2])) > LOG_WEIGHT_FLOOR)

        def step(st):
            n, c0, c1, a0, a1 = st
            c_ls, accs = tile(i - n, [c0, c1], [a0, a1], None)
            return (n + 1, c_ls[0], c_ls[1], accs[0], accs[1])

        st = lax.while_loop(more, step, (jnp.int32(2), *st0))
        o = jnp.where(m0, st[3], st[4])
        o_ref[...] = o
        sq = o * o
        ms0 = jnp.sum(jnp.where(m0, sq, 0.0), axis=-1, keepdims=True) * (1.0 / HEAD_DIM)
        ms1 = jnp.sum(jnp.where(m0, 0.0, sq), axis=-1, keepdims=True) * (1.0 / HEAD_DIM)
        rs = jnp.where(m0, lax.rsqrt(ms0 + RMS_EPS), lax.rsqrt(ms1 + RMS_EPS))
        on_ref[...] = (o * rs * ga_ref[...]).astype(BF16)

        @pl.when(jnp.logical_and(p == N_PAIRS - 1, i == nq - 1))
        def _():
            rider.last(ride)

    return rider.call(
        body, [q, k, v, ga], grid=(N_PAIRS, nq), name="attn_fwd",
        in_specs=[pl.BlockSpec((tq, PAIR), lambda p, i: (i, p)),
                  pl.BlockSpec((S, PAIR), lambda p, i: (0, p)),
                  pl.BlockSpec((S, PAIR), lambda p, i: (0, p)),
                  pl.BlockSpec((1, PAIR), lambda p, i: (0, p))],
        out_specs=[pl.BlockSpec((tq, PAIR), lambda p, i: (i, p)),
                   pl.BlockSpec((tq, PAIR), lambda p, i: (i, p))],
        out_shape=[_sds((S, ATTN_WIDTH), F32), _sds((S, ATTN_WIDTH), BF16)],
        scratch_shapes=[], vmem_mib=40)


def _pool_fwd(u, w_pool, pscale, tm):
    S = u.shape[0]
    hb = tm // POOL_HALO

    def body(u_ref, uh_ref, wp_ref, sc_ref, d_ref, pooled_ref):
        i = pl.program_id(0)
        halo = jnp.where(i > 0, uh_ref[...], 0.0)
        pos = i * tm + lax.broadcasted_iota(jnp.int32, (tm, 1), 0)
        for g in range(N_POOL_GROUPS):
            win = 2 ** (g + 1)
            cols = slice(g * POOL_GROUP, (g + 1) * POOL_GROUP)
            ut = u_ref[:, cols]
            s = jnp.concatenate([halo[:, cols], ut], axis=0)
            for sh in (1, 2, 4, 8)[:g + 1]:
                s = s + pltpu.roll(s, sh, 0)
            cnt = jnp.minimum(pos + 1, win).astype(F32)
            db = (s[POOL_HALO:, :] / cnt - ut).astype(BF16)
            y = _dot(db, wp_ref[g].astype(BF16))
            d_ref[:, cols] = db
            pooled_ref[:, cols] = (y * sc_ref[:, cols]).astype(BF16)

    return pl.pallas_call(
        body, grid=(S // tm,), name="pool_fwd",
        in_specs=[_row_spec(tm, POOL_WIDTH),
                  pl.BlockSpec((POOL_HALO, POOL_WIDTH), lambda i: (jnp.maximum(i * hb - 1, 0), 0)),
                  _const_spec((N_POOL_GROUPS, POOL_GROUP, POOL_GROUP)), _const_spec((1, POOL_WIDTH))],
        out_specs=[_row_spec(tm, POOL_WIDTH), _row_spec(tm, POOL_WIDTH)],
        out_shape=[_sds((S, POOL_WIDTH), BF16), _sds((S, POOL_WIDTH), BF16)],
        compiler_params=_cp(32),
    )(*_hbm(u, u, w_pool, pscale))


def _mix_ln1(on, pooled, xh0, g0, b0, w_out, g1, b1, tm, rider):
    S, D = xh0.shape
    n_t = S // tm

    def body(*refs):
        ((on_ref, po_ref, xh0_ref, g0_ref, b0_ref, w_ref, g1_ref, b1_ref), (xh_ref, rstd_ref, xb_ref), _,
         ride) = rider.split(refs, 8, 3, 0)

        @pl.when(pl.program_id(0) == 0)
        def _():
            rider.first(ride)

        mixed = _dot(on_ref[...], w_ref[:ATTN_WIDTH, :]) + _dot(po_ref[...], w_ref[ATTN_WIDTH:, :])
        x0 = xh0_ref[...] * g0_ref[...] + b0_ref[...]
        xh, rstd = _ln_fwd(ALPHA * x0 + mixed)
        xh_ref[...] = xh
        rstd_ref[...] = rstd
        xb_ref[...] = (xh * g1_ref[...] + b1_ref[...]).astype(BF16)

        @pl.when(pl.program_id(0) == n_t - 1)
        def _():
            rider.last(ride)

    return rider.call(
        body, [on, pooled, xh0, g0, b0, w_out, g1, b1], grid=(n_t,), name="mix_ln1",
        in_specs=[_row_spec(tm, ATTN_WIDTH), _row_spec(tm, POOL_WIDTH), _row_spec(tm, D),
                  _const_spec((1, D)), _const_spec((1, D)), _const_spec((D, D)),
                  _const_spec((1, D)), _const_spec((1, D))],
        out_specs=[_row_spec(tm, D), _row_spec(tm, 1), _row_spec(tm, D)],
        out_shape=[_sds((S, D), F32), _sds((S, 1), F32), _sds((S, D), BF16)],
        scratch_shapes=[], vmem_mib=40)


def _mlp_ln2(xh1, x1b, g1, b1, w_up_s, w_down, tm):
    S, D = xh1.shape
    fc = D_FF // N_CHIPS

    def body(xh_ref, xb_ref, g_ref, b_ref, wu_ref, wd_ref, xh2_ref, rstd_ref, r_ref, h1_ref, acc_ref):
        j = pl.program_id(1)

        @pl.when(j == 0)
        def _():
            acc_ref[...] = jnp.zeros_like(acc_ref)

        r = jnp.maximum(_dot(xb_ref[...], wu_ref[0]), 0.0)
        h1 = (r * r).astype(BF16)
        r_ref[...] = r.astype(BF16)
        h1_ref[...] = h1
        acc_ref[...] += _dot(h1, wd_ref[...])

        @pl.when(j == N_CHIPS - 1)
        def _():
            x1 = xh_ref[...] * g_ref[...] + b_ref[...]
            xh, rstd = _ln_fwd(ALPHA * x1 + acc_ref[...])
            xh2_ref[...] = xh
            rstd_ref[...] = rstd

    return pl.pallas_call(
        body, grid=(S // tm, N_CHIPS), name="mlp_ln2",
        in_specs=[pl.BlockSpec((tm, D), lambda i, j: (i, 0)), pl.BlockSpec((tm, D), lambda i, j: (i, 0)),
                  pl.BlockSpec((1, D), lambda i, j: (0, 0)), pl.BlockSpec((1, D), lambda i, j: (0, 0)),
                  pl.BlockSpec((1, D, fc), lambda i, j: (j, 0, 0)),
                  pl.BlockSpec((fc, D), lambda i, j: (j, 0))],
        out_specs=[pl.BlockSpec((tm, D), lambda i, j: (i, 0)), pl.BlockSpec((tm, 1), lambda i, j: (i, 0)),
                   pl.BlockSpec((tm, fc), lambda i, j: (i, j)), pl.BlockSpec((tm, fc), lambda i, j: (i, j))],
        out_shape=[_sds((S, D), F32), _sds((S, 1), F32), _sds((S, D_FF), BF16), _sds((S, D_FF), BF16)],
        scratch_shapes=[pltpu.VMEM((tm, D), F32)],
        compiler_params=_cp(40),
    )(*_hbm(xh1, x1b, g1, b1, w_up_s, w_down))


def _ple_ln3_loss(xh2, rstd2, g2, b2, p, w_ple_s, w_gate, g3, b3, target, tm):
    S, D = xh2.shape
    pc = D // N_CHIPS

    def body(xh2_ref, rstd2_ref, g2_ref, b2_ref, p_ref, wp_ref, wg_ref, g3_ref, b3_ref, t_ref,
             dpre2_ref, dhb_ref, dwp_ref, dwg_ref, dg3_ref, db3_ref, dg2_ref, db2_ref, loss_ref):
        i = pl.program_id(0)

        @pl.when(i == 0)
        def _():
            for r in (dwp_ref, dwg_ref, dg3_ref, db3_ref, dg2_ref, db2_ref, loss_ref):
                r[...] = jnp.zeros_like(r)

        xh2 = xh2_ref[...]
        x2 = xh2 * g2_ref[...] + b2_ref[...]
        x2b = x2.astype(BF16)
        gate = 1.0 / (1.0 + jnp.exp(-_dot(x2b, wg_ref[...])))
        pb = p_ref[...].astype(BF16)
        pe = jnp.concatenate([_dot(pb, wp_ref[c]) for c in range(N_CHIPS)], axis=1)
        xh3, rstd3 = _ln_fwd(ALPHA * x2 + pe * gate)
        diff = xh3 * g3_ref[...] + b3_ref[...] - t_ref[...]
        loss_ref[...] += (0.5 / D) * jnp.sum(diff * diff)
        dy = diff * (1.0 / D)
        dg3_ref[...] += _colsum(dy * xh3)
        db3_ref[...] += _colsum(dy)
        dpre3 = _ln_bwd(dy, xh3, rstd3, g3_ref[...])
        dpe_b = (dpre3 * gate).astype(BF16)
        dgp_b = (dpre3 * pe * gate * (1.0 - gate)).astype(BF16)
        dx2 = ALPHA * dpre3 + _dot_nt(dgp_b, wg_ref[...])
        dwg_ref[...] += _dot_tn(x2b, dgp_b)
        for c in range(N_CHIPS):
            dwp_ref[c] += _dot_tn(pb, dpe_b[:, c * pc:(c + 1) * pc])
        dg2_ref[...] += _colsum(dx2 * xh2)
        db2_ref[...] += _colsum(dx2)
        dpre2 = _ln_bwd(dx2, xh2, rstd2_ref[...], g2_ref[...])
        dpre2_ref[...] = dpre2
        dhb_ref[...] = dpre2.astype(BF16)

    vec = _const_spec((1, D))
    return pl.pallas_call(
        body, grid=(S // tm,), name="ple_ln3_loss",
        in_specs=[_row_spec(tm, D), _row_spec(tm, 1), vec, vec, _row_spec(tm, PLE_DIM),
                  _const_spec((N_CHIPS, PLE_DIM, pc)), _const_spec((D, D)), vec, vec, _row_spec(tm, D)],
        out_specs=[_row_spec(tm, D), _row_spec(tm, D), _const_spec((N_CHIPS, PLE_DIM, pc)),
                   _const_spec((D, D)), vec, vec, vec, vec, _const_spec((1, LANES))],
        out_shape=[_sds((S, D), F32), _sds((S, D), BF16), _sds((N_CHIPS, PLE_DIM, pc), F32),
                   _sds((D, D), F32), _sds((1, D), F32), _sds((1, D), F32), _sds((1, D), F32),
                   _sds((1, D), F32), _sds((1, LANES), F32)],
        compiler_params=_cp(48),
    )(*_hbm(xh2, rstd2, g2, b2, p, w_ple_s, w_gate, g3, b3, target))


def _mlp_bwd(rb, dhb, w_up_s, w_down, tm):
    S, D = dhb.shape
    fc = D_FF // N_CHIPS

    def body(r_ref, dh_ref, wu_ref, wd_ref, dx_ref, da_ref):
        da = (_dot_nt(dh_ref[...], wd_ref[...]) * (2.0 * r_ref[...].astype(F32))).astype(BF16)
        da_ref[...] = da
        dx = _dot_nt(da, wu_ref[0])

        @pl.when(pl.program_id(1) == 0)
        def _():
            dx_ref[...] = dx

        @pl.when(pl.program_id(1) > 0)
        def _():
            dx_ref[...] += dx

    return pl.pallas_call(
        body, grid=(S // tm, N_CHIPS), name="mlp_bwd",
        in_specs=[pl.BlockSpec((tm, fc), lambda i, j: (i, j)), pl.BlockSpec((tm, D), lambda i, j: (i, 0)),
                  pl.BlockSpec((1, D, fc), lambda i, j: (j, 0, 0)),
                  pl.BlockSpec((fc, D), lambda i, j: (j, 0))],
        out_specs=[pl.BlockSpec((tm, D), lambda i, j: (i, 0)), pl.BlockSpec((tm, fc), lambda i, j: (i, j))],
        out_shape=[_sds((S, D), F32), _sds((S, D_FF), BF16)],
        compiler_params=_cp(40),
    )(*_hbm(rb, dhb, w_up_s, w_down))


def _tn_matmul(a, b, name, tk, tt, stacked):
    T, K = a.shape
    N = b.shape[1]
    tn = 1024

    def body(a_ref, b_ref, o_ref):
        prod = _dot_tn(a_ref[...], b_ref[...])
        if stacked:
            prod = prod[None]

        @pl.when(pl.program_id(2) == 0)
        def _():
            o_ref[...] = prod

        @pl.when(pl.program_id(2) > 0)
        def _():
            o_ref[...] += prod

    if stacked:
        out_spec = pl.BlockSpec((1, tk, tn), lambda k, n, t: (n, k, 0))
        out_shape = _sds((N // tn, K, tn), F32)
    else:
        out_spec = pl.BlockSpec((tk, tn), lambda k, n, t: (k, n))
        out_shape = _sds((K, N), F32)
    return pl.pallas_call(
        body, grid=(K // tk, N // tn, T // tt), name=name,
        in_specs=[pl.BlockSpec((tt, tk), lambda k, n, t: (t, k)),
                  pl.BlockSpec((tt, tn), lambda k, n, t: (t, n))],
        out_specs=out_spec, out_shape=out_shape,
        compiler_params=_cp(40),
    )(*_hbm(a, b))


def _mix_bwd(dpre2, dx1m, xh1, rstd1, g1, w_out, on, pooled, tm, rider):
    S, D = xh1.shape
    n_t = S // tm

    def body(*refs):
        ((dp2_ref, dxm_ref, xh_ref, rstd_ref, g_ref, w_ref, on_ref, po_ref),
         (dpre1_ref, don_ref, dpo_ref, dw_ref, dg_ref, db_ref), _, ride) = rider.split(refs, 8, 6, 0)

        @pl.when(pl.program_id(0) == 0)
        def _():
            rider.first(ride)
            for r in (dw_ref, dg_ref, db_ref):
                r[...] = jnp.zeros_like(r)

        xh = xh_ref[...]
        dx1 = ALPHA * dp2_ref[...] + dxm_ref[...]
        dg_ref[...] += _colsum(dx1 * xh)
        db_ref[...] += _colsum(dx1)
        dpre1 = _ln_bwd(dx1, xh, rstd_ref[...], g_ref[...])
        dpre1_ref[...] = dpre1
        dmb = dpre1.astype(BF16)
        dcat = _dot_nt(dmb, w_ref[...])
        don_ref[...] = dcat[:, :ATTN_WIDTH]
        dpo_ref[...] = dcat[:, ATTN_WIDTH:]
        dw_ref[:ATTN_WIDTH, :] += _dot_tn(on_ref[...], dmb)
        dw_ref[ATTN_WIDTH:, :] += _dot_tn(po_ref[...], dmb)

        @pl.when(pl.program_id(0) == n_t - 1)
        def _():
            rider.last(ride)

    vec = _const_spec((1, D))
    return rider.call(
        body, [dpre2, dx1m, xh1, rstd1, g1, w_out, on, pooled], grid=(n_t,), name="mix_bwd",
        in_specs=[_row_spec(tm, D), _row_spec(tm, D), _row_spec(tm, D), _row_spec(tm, 1), vec,
                  _const_spec((D, D)), _row_spec(tm, ATTN_WIDTH), _row_spec(tm, POOL_WIDTH)],
        out_specs=[_row_spec(tm, D), _row_spec(tm, ATTN_WIDTH), _row_spec(tm, POOL_WIDTH),
                   _const_spec((D, D)), vec, vec],
        out_shape=[_sds((S, D), F32), _sds((S, ATTN_WIDTH), F32), _sds((S, POOL_WIDTH), F32),
                   _sds((D, D), F32), _sds((1, D), F32), _sds((1, D), F32)],
        scratch_shapes=[], vmem_mib=48)


def _pool_bwd(dpooled, d_b, w_pool, pscale, tm, rider):
    S = dpooled.shape[0]
    hb = tm // POOL_HALO
    n_t = S // tm
    te = tm + POOL_HALO

    def body(*refs):
        ((dp_ref, dph_ref, d_ref, wp_ref, sc_ref), (du_ref, dwp_ref, dsc_ref), _,
         ride) = rider.split(refs, 5, 3, 0)
        i = pl.program_id(0)

        @pl.when(i == 0)
        def _():
            rider.first(ride)
            dwp_ref[...] = jnp.zeros_like(dwp_ref)
            dsc_ref[...] = jnp.zeros_like(dsc_ref)

        halo = jnp.where(i < n_t - 1, dph_ref[...], 0.0)
        pos = i * tm + lax.broadcasted_iota(jnp.int32, (te, 1), 0)
        for g in range(N_POOL_GROUPS):
            win = 2 ** (g + 1)
            cols = slice(g * POOL_GROUP, (g + 1) * POOL_GROUP)
            wpb = wp_ref[g].astype(BF16)
            dpt = dp_ref[:, cols]
            dpe = jnp.concatenate([dpt, halo[:, cols]], axis=0)
            dyb = (dpe * sc_ref[:, cols]).astype(BF16)
            dd = _dot_nt(dyb, wpb)
            s = dd / jnp.minimum(pos + 1, win).astype(F32)
            for sh in (1, 2, 4, 8)[:g + 1]:
                s = s + pltpu.roll(s, te - sh, 0)
            du_ref[:, cols] = s[:tm, :] - dd[:tm, :]
            db = d_ref[:, cols]
            dwp_ref[g] += _dot_tn(db, dyb[:tm, :])
            dsc_ref[:, cols] += _colsum(dpt * _dot(db, wpb))

        @pl.when(i == n_t - 1)
        def _():
            rider.last(ride)

    return rider.call(
        body, [dpooled, dpooled, d_b, w_pool, pscale], grid=(n_t,), name="pool_bwd",
        in_specs=[_row_spec(tm, POOL_WIDTH),
                  pl.BlockSpec((POOL_HALO, POOL_WIDTH),
                               lambda i: (jnp.minimum((i + 1) * hb, S // POOL_HALO - 1), 0)),
                  _row_spec(tm, POOL_WIDTH),
                  _const_spec((N_POOL_GROUPS, POOL_GROUP, POOL_GROUP)), _const_spec((1, POOL_WIDTH))],
        out_specs=[_row_spec(tm, POOL_WIDTH), _const_spec((N_POOL_GROUPS, POOL_GROUP, POOL_GROUP)),
                   _const_spec((1, POOL_WIDTH))],
        out_shape=[_sds((S, POOL_WIDTH), F32), _sds((N_POOL_GROUPS, POOL_GROUP, POOL_GROUP), F32),
                   _sds((1, POOL_WIDTH), F32)],
        scratch_shapes=[], vmem_mib=32)


def _attn_bwd(q, k, v, don, o_raw, ga, tq, rider):
    S = q.shape[0]
    nq = S // tq

    def body(*refs):
        ((q_ref, k_ref, v_ref, don_ref, o_ref, ga_ref), (dq_ref, dk_ref, dv_ref, dga_ref),
         (g_s, b_s), ride) = rider.split(refs, 6, 4, 2)
        p, i = pl.program_id(0), pl.program_id(1)

        @pl.when(jnp.logical_and(p == 0, i == 0))
        def _():
            rider.first(ride)

        @pl.when(i == 0)
        def _():
            for r in (dk_ref, dv_ref, dga_ref):
                r[...] = jnp.zeros_like(r)

        lane = lax.broadcasted_iota(jnp.int32, (1, PAIR), 1)
        m0 = lane < HEAD_DIM
        low = _tri(tq, upper=False)
        upp = _tri(tq, upper=True)

        def seg_mean(a):
            s0 = jnp.sum(jnp.where(m0, a, 0.0), axis=-1, keepdims=True)
            s1 = jnp.sum(jnp.where(m0, 0.0, a), axis=-1, keepdims=True)
            return jnp.where(m0, s0, s1) * (1.0 / HEAD_DIM)

        o = o_ref[...]
        rs = lax.rsqrt(seg_mean(o * o) + RMS_EPS)
        oh = o * rs
        don = don_ref[...]
        dga_ref[...] += _colsum(don * oh)
        doh = don * ga_ref[...]
        do = rs * (doh - oh * seg_mean(doh * oh))
        dob = do.astype(BF16)
        q2 = q_ref[...]
        qhs = [jnp.where(m0, q2, jnp.zeros_like(q2)), jnp.where(m0, jnp.zeros_like(q2), q2)]
        dhs = [jnp.where(m0, dob, jnp.zeros_like(dob)), jnp.where(m0, jnp.zeros_like(dob), dob)]
        causal = _strictly_causal(tq)

        def down(kb, c_ls, valid):
            ks = pl.multiple_of(kb * tq, tq)
            kt = k_ref[pl.ds(ks, tq), :]
            vt = v_ref[pl.ds(ks, tq), :]
            lss, ls_, ws = _sb_tile(qhs, kt, low, c_ls, valid)
            dws = [_dot_nt(dh, vt) for dh in dhs]
            for hh in range(2):
                g_s[hh, kb] = dws[hh] * ws[hh]
                b_s[hh, kb] = jnp.exp(lss[hh])
            dv_ref[pl.ds(ks, tq), :] += (_dot_tn(ws[0].astype(BF16), dhs[0])
                                         + _dot_tn(ws[1].astype(BF16), dhs[1]))
            return [c_l + jnp.sum(l, axis=1, keepdims=True) for c_l, l in zip(c_ls, ls_)]

        zc, za = jnp.zeros((tq, 1), F32), jnp.zeros((tq, PAIR), F32)
        c_ls = lax.cond(i >= 1, lambda: tuple(down(i - 1, down(i, [zc, zc], causal), None)),
                        lambda: tuple(down(i, [zc, zc], causal)))

        def more(st):
            return jnp.logical_and(st[0] <= i, jnp.max(jnp.maximum(st[1], st[2])) > LOG_WEIGHT_FLOOR)

        def down_step(st):
            c_ls = down(i - st[0], [st[1], st[2]], None)
            return (st[0] + 1, c_ls[0], c_ls[1])

        n_tiles = lax.while_loop(more, down_step, (jnp.int32(2), c_ls[0], c_ls[1]))[0]

        def up(kb, c_gs, accs, valid):
            ks = pl.multiple_of(kb * tq, tq)
            kt = k_ref[pl.ds(ks, tq), :]
            gs = [g_s[hh, kb] for hh in range(2)]
            pres = [_dot(g.astype(BF16), upp) + c_g for g, c_g in zip(gs, c_gs)]
            dzs = []
            for hh in range(2):
                beta = b_s[hh, kb]
                dz = gs[hh] * (1.0 - beta) - beta * pres[hh]
                if valid is not None:
                    dz = jnp.where(valid, dz, 0.0)
                dzs.append(dz.astype(BF16))
            new_a = [acc + _dot(dzb, kt) for acc, dzb in zip(accs, dzs)]
            dk_ref[pl.ds(ks, tq), :] += _dot_tn(dzs[0], qhs[0]) + _dot_tn(dzs[1], qhs[1])
            new_c = [c_g + jnp.sum(g, axis=1, keepdims=True) for c_g, g in zip(c_gs, gs)]
            return new_c, new_a

        def up_step(kb, st):
            c_gs, accs = up(kb, [st[0], st[1]], [st[2], st[3]], None)
            return (c_gs[0], c_gs[1], accs[0], accs[1])

        st = lax.fori_loop(i - n_tiles + 1, i - 1, up_step, (zc, zc, za, za))

        def last_two():
            c_gs, accs = up(i - 1, [st[0], st[1]], [st[2], st[3]], None)
            return tuple(up(i, c_gs, accs, causal)[1])

        accs = lax.cond(i >= 1, last_two, lambda: tuple(up(i, [zc, zc], [za, za], causal)[1]))
        dq_ref[...] = jnp.where(m0, accs[0], accs[1]) * Q_SCALE

        @pl.when(jnp.logical_and(p == N_PAIRS - 1, i == nq - 1))
        def _():
            rider.last(ride)

    return rider.call(
        body, [q, k, v, don, o_raw, ga], grid=(N_PAIRS, nq), name="attn_bwd",
        in_specs=[pl.BlockSpec((tq, PAIR), lambda p, i: (i, p)),
                  pl.BlockSpec((S, PAIR), lambda p, i: (0, p)),
                  pl.BlockSpec((S, PAIR), lambda p, i: (0, p)),
                  pl.BlockSpec((tq, PAIR), lambda p, i: (i, p)),
                  pl.BlockSpec((tq, PAIR), lambda p, i: (i, p)),
                  pl.BlockSpec((1, PAIR), lambda p, i: (0, p))],
        out_specs=[pl.BlockSpec((tq, PAIR), lambda p, i: (i, p)),
                   pl.BlockSpec((S, PAIR), lambda p, i: (0, p)),
                   pl.BlockSpec((S, PAIR), lambda p, i: (0, p)),
                   pl.BlockSpec((1, PAIR), lambda p, i: (0, p))],
        out_shape=[_sds((S, ATTN_WIDTH), F32), _sds((S, ATTN_WIDTH), F32), _sds((S, ATTN_WIDTH), F32),
                   _sds((1, ATTN_WIDTH), F32)],
        scratch_shapes=[pltpu.VMEM((2, nq, tq, tq), F32), pltpu.VMEM((2, nq, tq, tq), F32)],
        vmem_mib=56)


def _inproj_bwd(dq, dk, dv, du, dpre1, xh0, rstd0, g0, b0, w_in_s, tm):
    S, D = xh0.shape

    def body(dq_ref, dk_ref, dv_ref, du_ref, dp1_ref, xh_ref, rstd_ref, g_ref, b_ref, w_ref,
             gx_ref, dw_ref, dg_ref, db_ref):
        @pl.when(pl.program_id(0) == 0)
        def _():
            for r in (dw_ref, dg_ref, db_ref):
                r[...] = jnp.zeros_like(r)

        xh = xh_ref[...]
        xb = (xh * g_ref[...] + b_ref[...]).astype(BF16)
        dx0 = ALPHA * dp1_ref[...]
        for c, r in enumerate((dq_ref, dk_ref, dv_ref, du_ref)):
            dpb = r[...].astype(BF16)
            dx0 = dx0 + _dot_nt(dpb, w_ref[c])
            dw_ref[c] += _dot_tn(xb, dpb)
        dg_ref[...] += _colsum(dx0 * xh)
        db_ref[...] += _colsum(dx0)
        gx_ref[...] = _ln_bwd(dx0, xh, rstd_ref[...], g_ref[...])

    vec = _const_spec((1, D))
    half = _row_spec(tm, 512)
    return pl.pallas_call(
        body, grid=(S // tm,), name="inproj_bwd",
        in_specs=[half, half, half, half, _row_spec(tm, D), _row_spec(tm, D), _row_spec(tm, 1), vec, vec,
                  _const_spec((N_CHIPS, D, 512))],
        out_specs=[_row_spec(tm, D), _const_spec((N_CHIPS, D, 512)), vec, vec],
        out_shape=[_sds((S, D), F32), _sds((N_CHIPS, D, 512), F32), _sds((1, D), F32), _sds((1, D), F32)],
        compiler_params=_cp(56),
    )(*_hbm(dq, dk, dv, du, dpre1, xh0, rstd0, g0, b0, w_in_s))


def _place():
    return lax.axis_index("x"), lax.axis_index("y"), lax.axis_index("c")


CHIP_FLIPS = ((0, 1), (1, 0), (1, 1))


class _Rider:
    def __init__(self, ins, out_shapes, n_sem, phases, aliases=None):
        self.ins, self.out_shapes, self.n_sem, self.phases = list(ins), list(out_shapes), n_sem, phases
        self.aliases = aliases or {}

    def __add__(self, other):
        na, ma = len(self.ins), len(self.out_shapes)

        def phases(ins, outs, ssem, rsem):
            mine = self.phases(ins[:na], outs[:ma], ssem, rsem)
            rest = pl.ds(self.n_sem, other.n_sem)
            theirs = other.phases(ins[na:], outs[ma:], ssem.at[rest], rsem.at[rest])
            assert len(mine) == 1 and len(theirs) == 1
            return [mine[0] + theirs[0]]

        aliases = {**self.aliases, **{na + i: ma + o for i, o in other.aliases.items()}}
        return _Rider(self.ins + other.ins, self.out_shapes + other.out_shapes, self.n_sem + other.n_sem, phases,
                      aliases)

    def split(self, refs, n_in, n_out, n_scratch):
        a = n_in + len(self.ins)
        b = a + n_out
        c = b + len(self.out_shapes)
        own = (refs[:n_in], refs[a:b], refs[c:c + n_scratch])
        return own + ((refs[n_in:a], refs[b:c]) + tuple(refs[c + n_scratch:]),)

    def first(self, ride):
        for make in self.phases(*ride)[0]:
            make().start()

    def mid(self, ride):
        ph = self.phases(*ride)
        if len(ph) == 2:
            for make in ph[0]:
                make().wait_recv()
            for make in ph[1]:
                make().start()

    def last(self, ride):
        ph = self.phases(*ride)
        if len(ph) == 2:
            for make in ph[0]:
                make().wait_send()
        for make in ph[-1]:
            make().wait()

    def call(self, body, args, *, grid, name, in_specs, out_specs, out_shape, scratch_shapes, vmem_mib,
             prefetch=None):
        n_in, n_out = len(in_specs), len(out_specs)
        sems = [pltpu.SemaphoreType.DMA((self.n_sem,)), pltpu.SemaphoreType.DMA((self.n_sem,))]
        n_pre = 0 if prefetch is None else 1
        grid_spec = pltpu.PrefetchScalarGridSpec(
            num_scalar_prefetch=n_pre, grid=grid,
            in_specs=list(in_specs) + [HBM_SPEC] * len(self.ins),
            out_specs=list(out_specs) + [HBM_SPEC] * len(self.out_shapes),
            scratch_shapes=list(scratch_shapes) + sems)
        return pl.pallas_call(
            body, name=name, grid_spec=grid_spec,
            out_shape=list(out_shape) + self.out_shapes,
            input_output_aliases={n_pre + n_in + i: n_out + o for i, o in self.aliases.items()},
            compiler_params=_cp(vmem_mib),
        )(*([] if prefetch is None else [prefetch]), *_hbm(*args), *self.ins)

    def run(self, name):
        def body(*refs):
            ride = self.split(refs, 0, 0, 0)[3]
            self.first(ride)
            self.mid(ride)
            self.last(ride)

        return self.call(body, [], grid=(), name=name, in_specs=[], out_specs=[], out_shape=[],
                         scratch_shapes=[], vmem_mib=16)


def _remote(src, dst, ssem, rsem, n, dev):
    return functools.partial(pltpu.make_async_remote_copy, src_ref=src, dst_ref=dst, send_sem=ssem.at[n],
                             recv_sem=rsem.at[n], device_id=dev, device_id_type=MESH)


def _cast_into_slot(w, place, name):
    R, C = w.shape
    tr = min(R, 512)

    def body(pl_ref, w_ref, o_ref):
        o_ref[0] = w_ref[...].astype(BF16)

    return pl.pallas_call(
        body, name=name,
        grid_spec=pltpu.PrefetchScalarGridSpec(
            num_scalar_prefetch=1, grid=(R // tr,),
            in_specs=[pl.BlockSpec((tr, C), lambda r, pr: (r, 0))],
            out_specs=pl.BlockSpec((1, tr, C), lambda r, pr: (pr[1], r, 0))),
        out_shape=_sds((N_CHIPS, R, C), BF16),
    )(place, w)


CAST_STEPS = 8


def _cast_rest(ws, place, rider):
    n = len(ws)

    def body(pl_ref, *refs):
        w_refs, o_refs, _, ride = rider.split(refs, n, n, 0)
        r = pl.program_id(0)

        @pl.when(r == 0)
        def _():
            rider.first(ride)

        @pl.when(r == CAST_STEPS // 2)
        def _():
            rider.mid(ride)

        for w_ref, o_ref in zip(w_refs, o_refs):
            o_ref[0] = w_ref[...].astype(BF16)

        @pl.when(r == CAST_STEPS - 1)
        def _():
            rider.last(ride)

    def rows(w):
        return w.shape[0] // CAST_STEPS

    return rider.call(
        body, ws, grid=(CAST_STEPS,), name="cast_weights", prefetch=place,
        in_specs=[pl.BlockSpec((rows(w), w.shape[1]), lambda r, pr: (r, 0)) for w in ws],
        out_specs=[pl.BlockSpec((1, rows(w), w.shape[1]), lambda r, pr: (pr[1], r, 0)) for w in ws],
        out_shape=[_sds((N_CHIPS,) + w.shape, BF16) for w in ws], scratch_shapes=[], vmem_mib=32)


def _gather_rider(stacked, part="both"):
    n, nf = len(stacked), len(CHIP_FLIPS)

    def phases(ins, outs, ssem, rsem):
        x, y, c = _place()
        slot = 2 * x + y
        ici, d2d = [], []
        for w, (i_ref, o_ref) in enumerate(zip(ins, outs)):
            hh = o_ref.shape[1] // 2
            rows = pl.ds(c * hh, hh)
            for f, (fx, fy) in enumerate(CHIP_FLIPS):
                k = w * nf + f
                theirs = 2 * (x ^ fx) + (y ^ fy)
                if part != "pair":
                    ici.append(_remote(i_ref.at[slot, rows], o_ref.at[slot, rows], ssem, rsem, k,
                                       (x ^ fx, y ^ fy, c)))
                if part != "chips":
                    d2d.append(_remote(o_ref.at[theirs, rows], o_ref.at[theirs, rows], ssem, rsem,
                                       (n * nf if part == "both" else 0) + k, (x, y, 1 - c)))
        return [ph for ph in (ici, d2d) if ph]

    return _Rider(stacked, [_sds(s.shape, s.dtype) for s in stacked], (2 if part == "both" else 1) * n * nf,
                  phases, aliases={i: i for i in range(n)})


def _pair_swap_rider(grads):
    def phases(ins, outs, ssem, rsem):
        x, y, c = _place()
        return [[_remote(g.at[:, 1 - c], o, ssem, rsem, k, (x, y, 1 - c))
                 for k, (g, o) in enumerate(zip(ins, outs))]]

    return _Rider(grads, [_sds((N_CHIPS,) + g.shape[2:], g.dtype) for g in grads], len(grads), phases)


def _chip_scatter_rider(parts):
    nf = len(CHIP_FLIPS)

    def phases(ins, outs, ssem, rsem):
        x, y, c = _place()
        return [[_remote(r.at[2 * (x ^ fx) + (y ^ fy)], o.at[f], ssem, rsem, w * nf + f, (x ^ fx, y ^ fy, c))
                 for w, (r, o) in enumerate(zip(ins, outs)) for f, (fx, fy) in enumerate(CHIP_FLIPS)]]

    return _Rider(parts, [_sds((nf,) + r.shape[1:], r.dtype) for r in parts], len(parts) * nf, phases)


def _pair_send_rider(halves):
    def phases(ins, outs, ssem, rsem):
        x, y, c = _place()
        return [[_remote(h, o, ssem, rsem, k, (x, y, 1 - c)) for k, (h, o) in enumerate(zip(ins, outs))]]

    return _Rider(halves, [_sds(h.shape, h.dtype) for h in halves], len(halves), phases)


def _add_pair(grad, recv, place, name):
    _, _, H, C = grad.shape
    th = min(H, 256)

    def body(pl_ref, g_ref, r_ref, o_ref, ob_ref):
        s = g_ref[:, 0] + r_ref[...]
        o_ref[...] = s
        ob_ref[...] = s.astype(BF16)

    spec = pl.BlockSpec((1, th, C), lambda j, h, pr: (j, h, 0))
    return pl.pallas_call(
        body, name=name,
        grid_spec=pltpu.PrefetchScalarGridSpec(
            num_scalar_prefetch=1, grid=(N_CHIPS, H // th),
            in_specs=[pl.BlockSpec((1, 1, th, C), lambda j, h, pr: (j, pr[0], h, 0)), spec],
            out_specs=[spec, spec]),
        out_shape=[_sds((N_CHIPS, H, C), F32), _sds((N_CHIPS, H, C), BF16)],
    )(place, *_hbm(grad, recv))


def _add_chips(part, recv, place, name):
    _, H, C = part.shape
    th = min(H, 256)

    def body(pl_ref, p_ref, r_ref, o_ref):
        o_ref[...] = p_ref[0] + r_ref[0].astype(F32) + r_ref[1].astype(F32) + r_ref[2].astype(F32)

    return pl.pallas_call(
        body, name=name,
        grid_spec=pltpu.PrefetchScalarGridSpec(
            num_scalar_prefetch=1, grid=(H // th,),
            in_specs=[pl.BlockSpec((1, th, C), lambda h, pr: (pr[1], h, 0)),
                      pl.BlockSpec((len(CHIP_FLIPS), th, C), lambda h, pr: (0, h, 0))],
            out_specs=pl.BlockSpec((th, C), lambda h, pr: (h, 0))),
        out_shape=_sds((H, C), F32),
    )(place, *_hbm(part, recv))


def _adamw_math(w, g, m, v):
    m = ADAM_B1 * m + (1.0 - ADAM_B1) * g
    v = ADAM_B2 * v + (1.0 - ADAM_B2) * (g * g)
    m_hat = m / (1.0 - ADAM_B1 ** ADAM_STEP)
    v_hat = v / (1.0 - ADAM_B2 ** ADAM_STEP)
    delta = -ADAM_LR * (m_hat / (jnp.sqrt(v_hat) + ADAM_EPS) + ADAM_WD * w)
    return delta, m, v


def _adamw(w, mine, theirs, m, v, place, name):
    R, C = w.shape
    th = min(R // 2, 256)
    nb = (R // 2) // th

    def body(pl_ref, w_ref, a_ref, b_ref, m_ref, v_ref, g_ref, d_ref, mo_ref, vo_ref):
        g = jnp.where(pl.program_id(0) == pl_ref[0], a_ref[...], b_ref[...])
        g_ref[...] = g
        d, mo, vo = _adamw_math(w_ref[...], g, m_ref[...], v_ref[...])
        d_ref[...] = d
        mo_ref[...] = mo
        vo_ref[...] = vo

    whole = pl.BlockSpec((th, C), lambda h, r, pr: (h * nb + r, 0))
    mine_spec = pl.BlockSpec((th, C), lambda h, r, pr: (jnp.where(h == pr[0], r, 0), 0))
    theirs_spec = pl.BlockSpec((th, C), lambda h, r, pr: (jnp.where(h == pr[0], 0, r), 0))
    return pl.pallas_call(
        body, name=name,
        grid_spec=pltpu.PrefetchScalarGridSpec(
            num_scalar_prefetch=1, grid=(2, nb),
            in_specs=[whole, mine_spec, theirs_spec, whole, whole], out_specs=[whole] * 4),
        out_shape=[_sds((R, C), F32)] * 4,
    )(place, *_hbm(w, mine, theirs, m, v))


DEVICE_FLIPS = tuple((fx, fy, fc) for fx in (0, 1) for fy in (0, 1) for fc in (0, 1))[1:]


def _pack_exchange_rider(pack):
    def phases(ins, outs, ssem, rsem):
        x, y, c = _place()
        mine = outs[0].at[4 * x + 2 * y + c]
        copies = [_remote(ins[0], mine, ssem, rsem, k, (x ^ fx, y ^ fy, c ^ fc))
                  for k, (fx, fy, fc) in enumerate(DEVICE_FLIPS)]
        copies.append(functools.partial(pltpu.make_async_copy, ins[0], mine, ssem.at[len(DEVICE_FLIPS)]))
        return [copies]

    return _Rider([pack], [_sds((N_DEV,) + pack.shape, pack.dtype)], len(DEVICE_FLIPS) + 1, phases)


def _small_sum_adamw(recv_a, recv_b, wpack, mpack, vpack):
    R = wpack.shape[0]

    def body(a_ref, b_ref, w_ref, m_ref, v_ref, gs_ref, d_ref, mo_ref, vo_ref):
        ta, tb = a_ref[0], b_ref[0]
        for dev in range(1, N_DEV):
            ta = ta + a_ref[dev]
            tb = tb + b_ref[dev]
        total = jnp.concatenate([ta, tb], axis=0)
        gs_ref[...] = total
        d, mo, vo = _adamw_math(w_ref[...], total, m_ref[...], v_ref[...])
        d_ref[...] = d
        mo_ref[...] = mo
        vo_ref[...] = vo

    return pl.pallas_call(
        body, name="small_sum_adamw", in_specs=[VMEM_SPEC] * 5, out_specs=[VMEM_SPEC] * 4,
        out_shape=[_sds((R, LANES), F32)] * 4,
    )(recv_a, recv_b, wpack, mpack, vpack)


def _rows8(a):
    a = a.reshape(-1, LANES)
    pad = (-a.shape[0]) % 8
    return jnp.pad(a, ((0, pad), (0, 0))) if pad else a


def _pack(parts):
    return jnp.concatenate([_rows8(a) for a in parts], axis=0)


def _unpack(pack, like):
    out, row = [], 0
    for a in like:
        n = a.size // LANES
        out.append(pack[row:row + n].reshape(a.shape))
        row += n + (-n) % 8
    return out


def kernel(x, p, emb_ln_g, emb_ln_b, w_in, attn_out_g, w_pool, pool_scale, w_out, ln1_g, ln1_b, w_up, w_down, ln2_g, ln2_b, w_ple, w_ple_gate, ln3_g, ln3_b, loss_target, m_emb_ln_g, m_emb_ln_b, m_w_in, m_attn_out_g, m_w_pool, m_pool_scale, m_w_out, m_ln1_g, m_ln1_b, m_w_up, m_w_down, m_ln2_g, m_ln2_b, m_w_ple, m_w_ple_gate, m_ln3_g, m_ln3_b, v_emb_ln_g, v_emb_ln_b, v_w_in, v_attn_out_g, v_w_pool, v_pool_scale, v_w_out, v_ln1_g, v_ln1_b, v_w_up, v_w_down, v_ln2_g, v_ln2_b, v_w_ple, v_w_ple_gate, v_ln3_g, v_ln3_b):
    S = x.shape[1]
    tm = min(256, S)
    tq = min(256, S)
    tm_mlp = min(512, S)
    xs = x[0]
    ps = p[0, 0]
    tgt = loss_target[0]
    row = lambda a: a.reshape(1, -1)
    g0, b0 = row(emb_ln_g), row(emb_ln_b)
    g1, b1, g2, b2, g3, b3 = ln1_g, ln1_b, ln2_g, ln2_b, ln3_g, ln3_b
    wp = w_pool[0]

    xi, yi, ci = _place()
    place = jnp.stack([ci, 2 * xi + yi]).astype(jnp.int32)
    names = ["w_in", "w_out", "w_up", "w_down", "w_ple", "w_ple_gate"]

    big = [w_in[0], w_out[0], w_up[0], w_down[0], w_ple[0], w_ple_gate[0]]
    s_in = _cast_into_slot(big[0], place, "cast_w_in")
    s_out, s_up, s_down, s_ple, s_gate, w_in_s = _cast_rest(big[1:], place, _gather_rider([s_in]))

    xh0, rstd0, q, k, v, u, s_out, s_ple, s_gate = _embln_inproj(
        xs, g0, b0, w_in_s, tm, _gather_rider([s_out, s_ple, s_gate], "chips"))
    o_raw, on, s_up, s_down, w_out_s, w_ple_s, w_gate_s = _attn_fwd(
        q, k, v, attn_out_g, tq, _gather_rider([s_up, s_down], "chips") + _gather_rider([s_out, s_ple, s_gate], "pair"))
    w_out_f = w_out_s.reshape(D_MODEL, D_MODEL)
    w_gate_f = w_gate_s.reshape(D_MODEL, D_MODEL)
    d_b, pooled = _pool_fwd(u, wp, pool_scale, tm)
    xh1, rstd1, x1b, w_up_s, w_down_s = _mix_ln1(on, pooled, xh0, g0, b0, w_out_f, g1, b1, tm,
                                                 _gather_rider([s_up, s_down], "pair"))
    w_down_f = w_down_s.reshape(D_FF, D_MODEL)
    xh2, rstd2, rb, h1 = _mlp_ln2(xh1, x1b, g1, b1, w_up_s, w_down_f, tm_mlp)

    (dpre2, dhb, dw_ple, dw_gate, dg3, db3, dg2, db2, loss_row) = _ple_ln3_loss(
        xh2, rstd2, g2, b2, ps, w_ple_s, w_gate_f, g3, b3, tgt, tm)
    dx1m, da = _mlp_bwd(rb, dhb, w_up_s, w_down_f, tm_mlp)
    dw_up = _tn_matmul(x1b, da, "grad_w_up", 1024, min(512, S), stacked=True)
    dw_down = _tn_matmul(h1, dhb, "grad_w_down", 1024, min(512, S), stacked=False)
    def halves_of(g):
        return g.reshape(N_CHIPS, 2, g.shape[1] // 2, g.shape[2])

    early_names = names[2:]
    early = [halves_of(g) for g in (dw_up, dw_down.reshape(N_CHIPS, D_FF // N_CHIPS, D_MODEL), dw_ple,
                                    dw_gate.reshape(N_CHIPS, D_MODEL // N_CHIPS, D_MODEL))]
    dpre1, don, dpooled, dw_out, dg1, db1, *early_pair = _mix_bwd(
        dpre2, dx1m, xh1, rstd1, g1, w_out_f, on, pooled, tm, _pair_swap_rider(early))
    early_sum = [_add_pair(g, r, place, "pair_sum_" + n) for g, r, n in zip(early, early_pair, early_names)]
    out_halves = halves_of(dw_out.reshape(N_CHIPS, D_MODEL // N_CHIPS, D_MODEL))
    du, dwp, dsc, out_pair = _pool_bwd(dpooled, d_b, wp, pool_scale, tm, _pair_swap_rider([out_halves]))
    out_sum = _add_pair(out_halves, out_pair, place, "pair_sum_w_out")
    pack_a = _pack([jnp.broadcast_to(loss_row, (8, LANES)), dwp, dsc, dg1, db1, dg2, db2, dg3, db3])
    riding = _chip_scatter_rider([out_sum[1]] + [b for _, b in early_sum]) + _pack_exchange_rider(pack_a)
    dq, dk, dv, dga, *arrived = _attn_bwd(q, k, v, don, o_raw, attn_out_g, tq, riding)
    early_chips, recv_a = arrived[:-1], arrived[-1]
    grad_x, dw_in, dg0, db0 = _inproj_bwd(dq, dk, dv, du, dpre1, xh0, rstd0, g0, b0, w_in_s, tm)

    in_halves = halves_of(dw_in)
    pack_b = _pack([dg0, db0, dga])
    in_pair, recv_b = (_pair_swap_rider([in_halves]) + _pack_exchange_rider(pack_b)).run("reduce_pair_late")
    in_sum = _add_pair(in_halves, in_pair, place, "pair_sum_w_in")
    (in_chips,) = _chip_scatter_rider([in_sum[1]]).run("reduce_chips_late")
    mine = [_add_chips(s, r, place, "chip_sum_" + n)
            for (s, _), r, n in zip([in_sum, out_sum] + early_sum, [in_chips] + early_chips, names)]
    theirs = _pair_send_rider(mine).run("gather_pair")

    ms = [m_w_in, m_w_out, m_w_up, m_w_down, m_w_ple, m_w_ple_gate]
    vs = [v_w_in, v_w_out, v_w_up, v_w_down, v_w_ple, v_w_ple_gate]
    big_out = {}
    for n, w, a, b, m, vv in zip(names, big, mine, theirs, ms, vs):
        res4 = _adamw(w, a, b, m[0], vv[0], place, "adamw_" + n)
        big_out[n] = tuple(r.reshape(m.shape) for r in res4)

    small_names = ["w_pool", "pool_scale", "ln1_g", "ln1_b", "ln2_g", "ln2_b", "ln3_g", "ln3_b",
                   "emb_ln_g", "emb_ln_b", "attn_out_g"]
    small_w = [w_pool, pool_scale, ln1_g, ln1_b, ln2_g, ln2_b, ln3_g, ln3_b, emb_ln_g, emb_ln_b, attn_out_g]
    small_m = [m_w_pool, m_pool_scale, m_ln1_g, m_ln1_b, m_ln2_g, m_ln2_b, m_ln3_g, m_ln3_b,
               m_emb_ln_g, m_emb_ln_b, m_attn_out_g]
    small_v = [v_w_pool, v_pool_scale, v_ln1_g, v_ln1_b, v_ln2_g, v_ln2_b, v_ln3_g, v_ln3_b,
               v_emb_ln_g, v_emb_ln_b, v_attn_out_g]
    loss_like = jnp.zeros((8, LANES), F32)
    gs, ds, mos, vos = _small_sum_adamw(recv_a, recv_b, _pack([loss_like] + small_w), _pack([loss_like] + small_m),
                                        _pack([jnp.ones((8, LANES), F32)] + small_v))
    like = [loss_like] + small_w
    gs_u, ds_u, mos_u, vos_u = (_unpack(a, like) for a in (gs, ds, mos, vos))
    loss = gs_u[0][0, 0]
    small_out = {n: (gs_u[i + 1], ds_u[i + 1], mos_u[i + 1], vos_u[i + 1]) for i, n in enumerate(small_names)}

    order = ["emb_ln_g", "emb_ln_b", "w_in", "attn_out_g", "w_pool", "pool_scale", "w_out", "ln1_g", "ln1_b",
             "w_up", "w_down", "ln2_g", "ln2_b", "w_ple", "w_ple_gate", "ln3_g", "ln3_b"]
    res = {**big_out, **small_out}
    outs = [loss, grad_x.reshape(x.shape)]
    for kind in range(4):
        outs += [res[n][kind] for n in order]
    return tuple(outs)
```

```python
import functools

import jax
import jax.numpy as jnp
from jax import lax
from jax.experimental import pallas as pl
from jax.experimental.pallas import tpu as pltpu

F32 = jnp.float32
BF16 = jnp.bfloat16

D_MODEL = 1024
ATTN_WIDTH = 512
POOL_WIDTH = 512
HEAD_DIM = 64
PAIR = 2 * HEAD_DIM
N_PAIRS = ATTN_WIDTH // PAIR
N_POOL_GROUPS = 4
POOL_GROUP = 128
POOL_HALO = 16
D_FF = 4096
PLE_DIM = 256
N_CHIPS = 4
N_DEV = 8
LN_EPS = 1e-5
RMS_EPS = 1e-6
ALPHA = float(2.0 ** 0.25)
Q_SCALE = 0.125
ADAM_LR = 0.001
ADAM_B1 = 0.9
ADAM_B2 = 0.999
ADAM_EPS = 1e-08
ADAM_WD = 0.01
ADAM_STEP = 10
LANES = 128
MIB = 1024 * 1024

MESH = pl.DeviceIdType.MESH
HBM_SPEC = pl.BlockSpec(memory_space=pltpu.HBM)
VMEM_SPEC = pl.BlockSpec(memory_space=pltpu.VMEM)


def _cp(vmem_mib):
    return pltpu.CompilerParams(vmem_limit_bytes=vmem_mib * MIB)


def _dot(a, b):
    return jnp.dot(a, b, preferred_element_type=F32)


def _dot_nt(a, b):
    return lax.dot_general(a, b, (((1,), (1,)), ((), ())), preferred_element_type=F32)


def _dot_tn(a, b):
    return lax.dot_general(a, b, (((0,), (0,)), ((), ())), preferred_element_type=F32)


def _ln_fwd(pre):
    mu = jnp.mean(pre, axis=-1, keepdims=True)
    xc = pre - mu
    var = jnp.mean(xc * xc, axis=-1, keepdims=True)
    rstd = lax.rsqrt(var + LN_EPS)
    return xc * rstd, rstd


def _ln_bwd(dy, xh, rstd, g):
    dxh = dy * g
    m1 = jnp.mean(dxh, axis=-1, keepdims=True)
    m2 = jnp.mean(dxh * xh, axis=-1, keepdims=True)
    return rstd * (dxh - m1 - xh * m2)


def _colsum(a):
    return jnp.sum(a, axis=0, keepdims=True)


def _neg_softplus(z):
    return -(jnp.maximum(z, 0.0) + jnp.log(1.0 + jnp.exp(-jnp.abs(z))))


def _split_bf16(a):
    hi = a.astype(BF16)
    lo = (a - hi.astype(F32)).astype(BF16)
    return hi, lo


def _row_spec(tm, n):
    return pl.BlockSpec((tm, n), lambda i: (i, 0))


def _const_spec(shape):
    nd = len(shape)
    return pl.BlockSpec(shape, lambda *_: (0,) * nd)


def _hbm(*arrays):
    return [pltpu.with_memory_space_constraint(a, pltpu.HBM) for a in arrays]


def _sds(shape, dtype):
    return pltpu.HBM(shape, dtype)


def _embln_inproj(x, g0, b0, w_in_s, tm, rider):
    S, D = x.shape
    n_t = S // tm

    def body(*refs):
        ((x_ref, g_ref, b_ref, w_ref), (xh_ref, rstd_ref, q_ref, k_ref, v_ref, u_ref), _,
         ride) = rider.split(refs, 4, 6, 0)
        i = pl.program_id(0)

        @pl.when(i == 0)
        def _():
            rider.first(ride)

        @pl.when(i == (3 * n_t) // 4)
        def _():
            rider.mid(ride)

        xh, rstd = _ln_fwd(x_ref[...])
        xh_ref[...] = xh
        rstd_ref[...] = rstd
        xb = (xh * g_ref[...] + b_ref[...]).astype(BF16)
        q_ref[...] = (_dot(xb, w_ref[0]) * Q_SCALE).astype(BF16)
        k_ref[...] = _dot(xb, w_ref[1]).astype(BF16)
        v_ref[...] = _dot(xb, w_ref[2]).astype(BF16)
        u_ref[...] = _dot(xb, w_ref[3])

        @pl.when(i == n_t - 1)
        def _():
            rider.last(ride)

    return rider.call(
        body, [x, g0, b0, w_in_s], grid=(n_t,), name="embln_inproj",
        in_specs=[_row_spec(tm, D), _const_spec((1, D)), _const_spec((1, D)),
                  _const_spec((N_CHIPS, D, 512))],
        out_specs=[_row_spec(tm, D), _row_spec(tm, 1), _row_spec(tm, 512), _row_spec(tm, 512),
                   _row_spec(tm, 512), _row_spec(tm, 512)],
        out_shape=[_sds((S, D), F32), _sds((S, 1), F32), _sds((S, 512), BF16), _sds((S, 512), BF16),
                   _sds((S, 512), BF16), _sds((S, 512), F32)],
        scratch_shapes=[], vmem_mib=40)


def _tri(n, upper):
    r = lax.broadcasted_iota(jnp.int32, (n, n), 0)
    c = lax.broadcasted_iota(jnp.int32, (n, n), 1)
    keep = (r < c) if upper else (r > c)
    return jnp.where(keep, 1.0, 0.0).astype(BF16)


def _strictly_causal(n):
    return lax.broadcasted_iota(jnp.int32, (n, n), 1) < lax.broadcasted_iota(jnp.int32, (n, n), 0)


LOG_WEIGHT_FLOOR = -110.0


def _sb_tile(qhs, kt, low, c_ls, valid):
    valids = valid if isinstance(valid, (list, tuple)) else [valid] * len(qhs)
    zs = [_dot_nt(qh, kt) for qh in qhs]
    lrs = [_neg_softplus(z) for z in zs]
    ls_ = [lr if m is None else jnp.where(m, lr, 0.0) for lr, m in zip(lrs, valids)]
    sfx = [_dot(l.astype(BF16), low) + c_l for l, c_l in zip(ls_, c_ls)]
    lss = [z + lr for z, lr in zip(zs, lrs)]
    ws = [jnp.exp(ls + s) for ls, s in zip(lss, sfx)]
    ws = [w if m is None else jnp.where(m, w, 0.0) for w, m in zip(ws, valids)]
    return lss, ls_, ws


def _attn_fwd(q, k, v, ga, tq, rider):
    S = q.shape[0]
    nq = S // tq

    def body(*refs):
        (q_ref, k_ref, v_ref, ga_ref), (o_ref, on_ref), _, ride = rider.split(refs, 4, 2, 0)
        p, i = pl.program_id(0), pl.program_id(1)

        @pl.when(jnp.logical_and(p == 0, i == 0))
        def _():
            rider.first(ride)

        @pl.when(jnp.logical_and(p == N_PAIRS - 1, i == 0))
        def _():
            rider.mid(ride)

        lane = lax.broadcasted_iota(jnp.int32, (1, PAIR), 1)
        m0 = lane < HEAD_DIM
        low = _tri(tq, upper=False)
        q2 = q_ref[...]
        qhs = [jnp.where(m0, q2, jnp.zeros_like(q2)), jnp.where(m0, jnp.zeros_like(q2), q2)]

        def tile(kb, c_ls, accs, valid):
            ks = pl.multiple_of(kb * tq, tq)
            kt = k_ref[pl.ds(ks, tq), :]
            vt = v_ref[pl.ds(ks, tq), :]
            _, ls_, ws = _sb_tile(qhs, kt, low, c_ls, valid)
            new_a = [acc + _dot(w.astype(BF16), vt) for acc, w in zip(accs, ws)]
            new_c = [c_l + jnp.sum(l, axis=1, keepdims=True) for c_l, l in zip(c_ls, ls_)]
            return new_c, new_a

        zc, za = jnp.zeros((tq, 1), F32), jnp.zeros((tq, PAIR), F32)

        def first_two():
            c_ls, accs = tile(i, [zc, zc], [za, za], _strictly_causal(tq))
            c_ls, accs = tile(i - 1, c_ls, accs, None)
            return (*c_ls, *accs)

        def first_one():
            c_ls, accs = tile(i, [zc, zc], [za, za], _strictly_causal(tq))
            return (*c_ls, *accs)

        st0 = lax.cond(i >= 1, first_two, first_one)

        def more(st):
            return jnp.logical_and(st[0] <= i, jnp.max(jnp.maximum(st[1], st[2])) > LOG_WEIGHT_FLOOR)

        def step(st):
            n, c0, c1, a0, a1 = st
            c_ls, accs = tile(i - n, [c0, c1], [a0, a1], None)
            return (n + 1, c_ls[0], c_ls[1], accs[0], accs[1])

        st = lax.while_loop(more, step, (jnp.int32(2), *st0))
        o = jnp.where(m0, st[3], st[4])
        o_ref[...] = o
        sq = o * o
        ms0 = jnp.sum(jnp.where(m0, sq, 0.0), axis=-1, keepdims=True) * (1.0 / HEAD_DIM)
        ms1 = jnp.sum(jnp.where(m0, 0.0, sq), axis=-1, keepdims=True) * (1.0 / HEAD_DIM)
        rs = jnp.where(m0, lax.rsqrt(ms0 + RMS_EPS), lax.rsqrt(ms1 + RMS_EPS))
        on_ref[...] = (o * rs * ga_ref[...]).astype(BF16)

        @pl.when(jnp.logical_and(p == N_PAIRS - 1, i == nq - 1))
        def _():
            rider.last(ride)

    return rider.call(
        body, [q, k, v, ga], grid=(N_PAIRS, nq), name="attn_fwd",
        in_specs=[pl.BlockSpec((tq, PAIR), lambda p, i: (i, p)),
                  pl.BlockSpec((S, PAIR), lambda p, i: (0, p)),
                  pl.BlockSpec((S, PAIR), lambda p, i: (0, p)),
                  pl.BlockSpec((1, PAIR), lambda p, i: (0, p))],
        out_specs=[pl.BlockSpec((tq, PAIR), lambda p, i: (i, p)),
                   pl.BlockSpec((tq, PAIR), lambda p, i: (i, p))],
        out_shape=[_sds((S, ATTN_WIDTH), F32), _sds((S, ATTN_WIDTH), BF16)],
        scratch_shapes=[], vmem_mib=40)


def _pool_fwd(u, w_pool, pscale, tm):
    S = u.shape[0]
    hb = tm // POOL_HALO

    def body(u_ref, uh_ref, wp_ref, sc_ref, d_ref, pooled_ref):
        i = pl.program_id(0)
        halo = jnp.where(i > 0, uh_ref[...], 0.0)
        pos = i * tm + lax.broadcasted_iota(jnp.int32, (tm, 1), 0)
        for g in range(N_POOL_GROUPS):
            win = 2 ** (g + 1)
            cols = slice(g * POOL_GROUP, (g + 1) * POOL_GROUP)
            ut = u_ref[:, cols]
            s = jnp.concatenate([halo[:, cols], ut], axis=0)
            for sh in (1, 2, 4, 8)[:g + 1]:
                s = s + pltpu.roll(s, sh, 0)
            cnt = jnp.minimum(pos + 1, win).astype(F32)
            db = (s[POOL_HALO:, :] / cnt - ut).astype(BF16)
            y = _dot(db, wp_ref[g].astype(BF16))
            d_ref[:, cols] = db
            pooled_ref[:, cols] = (y * sc_ref[:, cols]).astype(BF16)

    return pl.pallas_call(
        body, grid=(S // tm,), name="pool_fwd",
        in_specs=[_row_spec(tm, POOL_WIDTH),
                  pl.BlockSpec((POOL_HALO, POOL_WIDTH), lambda i: (jnp.maximum(i * hb - 1, 0), 0)),
                  _const_spec((N_POOL_GROUPS, POOL_GROUP, POOL_GROUP)), _const_spec((1, POOL_WIDTH))],
        out_specs=[_row_spec(tm, POOL_WIDTH), _row_spec(tm, POOL_WIDTH)],
        out_shape=[_sds((S, POOL_WIDTH), BF16), _sds((S, POOL_WIDTH), BF16)],
        compiler_params=_cp(32),
    )(*_hbm(u, u, w_pool, pscale))


def _mix_ln1(on, pooled, xh0, g0, b0, w_out, g1, b1, tm, rider):
    S, D = xh0.shape
    n_t = S // tm

    def body(*refs):
        ((on_ref, po_ref, xh0_ref, g0_ref, b0_ref, w_ref, g1_ref, b1_ref), (xh_ref, rstd_ref, xb_ref), _,
         ride) = rider.split(refs, 8, 3, 0)

        @pl.when(pl.program_id(0) == 0)
        def _():
            rider.first(ride)

        mixed = _dot(on_ref[...], w_ref[:ATTN_WIDTH, :]) + _dot(po_ref[...], w_ref[ATTN_WIDTH:, :])
        x0 = xh0_ref[...] * g0_ref[...] + b0_ref[...]
        xh, rstd = _ln_fwd(ALPHA * x0 + mixed)
        xh_ref[...] = xh
        rstd_ref[...] = rstd
        xb_ref[...] = (xh * g1_ref[...] + b1_ref[...]).astype(BF16)

        @pl.when(pl.program_id(0) == n_t - 1)
        def _():
            rider.last(ride)

    return rider.call(
        body, [on, pooled, xh0, g0, b0, w_out, g1, b1], grid=(n_t,), name="mix_ln1",
        in_specs=[_row_spec(tm, ATTN_WIDTH), _row_spec(tm, POOL_WIDTH), _row_spec(tm, D),
                  _const_spec((1, D)), _const_spec((1, D)), _const_spec((D, D)),
                  _const_spec((1, D)), _const_spec((1, D))],
        out_specs=[_row_spec(tm, D), _row_spec(tm, 1), _row_spec(tm, D)],
        out_shape=[_sds((S, D), F32), _sds((S, 1), F32), _sds((S, D), BF16)],
        scratch_shapes=[], vmem_mib=40)


def _mlp_ln2(xh1, x1b, g1, b1, w_up_s, w_down, tm):
    S, D = xh1.shape
    fc = D_FF // N_CHIPS

    def body(xh_ref, xb_ref, g_ref, b_ref, wu_ref, wd_ref, xh2_ref, rstd_ref, r_ref, acc_ref):
        j = pl.program_id(1)

        @pl.when(j == 0)
        def _():
            acc_ref[...] = jnp.zeros_like(acc_ref)

        r = jnp.maximum(_dot(xb_ref[...], wu_ref[0]), 0.0)
        r_ref[...] = r.astype(BF16)
        acc_ref[...] += _dot((r * r).astype(BF16), wd_ref[...])

        @pl.when(j == N_CHIPS - 1)
        def _():
            x1 = xh_ref[...] * g_ref[...] + b_ref[...]
            xh, rstd = _ln_fwd(ALPHA * x1 + acc_ref[...])
            xh2_ref[...] = xh
            rstd_ref[...] = rstd

    return pl.pallas_call(
        body, grid=(S // tm, N_CHIPS), name="mlp_ln2",
        in_specs=[pl.BlockSpec((tm, D), lambda i, j: (i, 0)), pl.BlockSpec((tm, D), lambda i, j: (i, 0)),
                  pl.BlockSpec((1, D), lambda i, j: (0, 0)), pl.BlockSpec((1, D), lambda i, j: (0, 0)),
                  pl.BlockSpec((1, D, fc), lambda i, j: (j, 0, 0)),
                  pl.BlockSpec((fc, D), lambda i, j: (j, 0))],
        out_specs=[pl.BlockSpec((tm, D), lambda i, j: (i, 0)), pl.BlockSpec((tm, 1), lambda i, j: (i, 0)),
                   pl.BlockSpec((tm, fc), lambda i, j: (i, j))],
        out_shape=[_sds((S, D), F32), _sds((S, 1), F32), _sds((S, D_FF), BF16)],
        scratch_shapes=[pltpu.VMEM((tm, D), F32)],
        compiler_params=_cp(40),
    )(*_hbm(xh1, x1b, g1, b1, w_up_s, w_down))


def _ple_ln3_loss(xh2, rstd2, g2, b2, p, w_ple_s, w_gate, g3, b3, target, tm):
    S, D = xh2.shape
    pc = D // N_CHIPS

    def body(xh2_ref, rstd2_ref, g2_ref, b2_ref, p_ref, wp_ref, wg_ref, g3_ref, b3_ref, t_ref,
             dpre2_ref, dhb_ref, dwp_ref, dwg_ref, dg3_ref, db3_ref, dg2_ref, db2_ref, loss_ref):
        i = pl.program_id(0)

        @pl.when(i == 0)
        def _():
            for r in (dwp_ref, dwg_ref, dg3_ref, db3_ref, dg2_ref, db2_ref, loss_ref):
                r[...] = jnp.zeros_like(r)

        xh2 = xh2_ref[...]
        x2 = xh2 * g2_ref[...] + b2_ref[...]
        x2b = x2.astype(BF16)
        gate = 1.0 / (1.0 + jnp.exp(-_dot(x2b, wg_ref[...])))
        pb = p_ref[...].astype(BF16)
        pe = jnp.concatenate([_dot(pb, wp_ref[c]) for c in range(N_CHIPS)], axis=1)
        xh3, rstd3 = _ln_fwd(ALPHA * x2 + pe * gate)
        diff = xh3 * g3_ref[...] + b3_ref[...] - t_ref[...]
        loss_ref[...] += (0.5 / D) * jnp.sum(diff * diff)
        dy = diff * (1.0 / D)
        dg3_ref[...] += _colsum(dy * xh3)
        db3_ref[...] += _colsum(dy)
        dpre3 = _ln_bwd(dy, xh3, rstd3, g3_ref[...])
        dpe_b = (dpre3 * gate).astype(BF16)
        dgp_b = (dpre3 * pe * gate * (1.0 - gate)).astype(BF16)
        dx2 = ALPHA * dpre3 + _dot_nt(dgp_b, wg_ref[...])
        dwg_ref[...] += _dot_tn(x2b, dgp_b)
        for c in range(N_CHIPS):
            dwp_ref[c] += _dot_tn(pb, dpe_b[:, c * pc:(c + 1) * pc])
        dg2_ref[...] += _colsum(dx2 * xh2)
        db2_ref[...] += _colsum(dx2)
        dpre2 = _ln_bwd(dx2, xh2, rstd2_ref[...], g2_ref[...])
        dpre2_ref[...] = dpre2
        dhb_ref[...] = dpre2.astype(BF16)

    vec = _const_spec((1, D))
    return pl.pallas_call(
        body, grid=(S // tm,), name="ple_ln3_loss",
        in_specs=[_row_spec(tm, D), _row_spec(tm, 1), vec, vec, _row_spec(tm, PLE_DIM),
                  _const_spec((N_CHIPS, PLE_DIM, pc)), _const_spec((D, D)), vec, vec, _row_spec(tm, D)],
        out_specs=[_row_spec(tm, D), _row_spec(tm, D), _const_spec((N_CHIPS, PLE_DIM, pc)),
                   _const_spec((D, D)), vec, vec, vec, vec, _const_spec((1, LANES))],
        out_shape=[_sds((S, D), F32), _sds((S, D), BF16), _sds((N_CHIPS, PLE_DIM, pc), F32),
                   _sds((D, D), F32), _sds((1, D), F32), _sds((1, D), F32), _sds((1, D), F32),
                   _sds((1, D), F32), _sds((1, LANES), F32)],
        compiler_params=_cp(48),
    )(*_hbm(xh2, rstd2, g2, b2, p, w_ple_s, w_gate, g3, b3, target))


def _mlp_bwd(rb, dhb, w_up_s, w_down, tm):
    S, D = dhb.shape
    fc = D_FF // N_CHIPS

    def body(r_ref, dh_ref, wu_ref, wd_ref, dx_ref, da_ref):
        @pl.when(pl.program_id(1) == 0)
        def _():
            dx_ref[...] = jnp.zeros_like(dx_ref)

        da = (_dot_nt(dh_ref[...], wd_ref[...]) * (2.0 * r_ref[...].astype(F32))).astype(BF16)
        da_ref[...] = da
        dx_ref[...] += _dot_nt(da, wu_ref[0])

    return pl.pallas_call(
        body, grid=(S // tm, N_CHIPS), name="mlp_bwd",
        in_specs=[pl.BlockSpec((tm, fc), lambda i, j: (i, j)), pl.BlockSpec((tm, D), lambda i, j: (i, 0)),
                  pl.BlockSpec((1, D, fc), lambda i, j: (j, 0, 0)),
                  pl.BlockSpec((fc, D), lambda i, j: (j, 0))],
        out_specs=[pl.BlockSpec((tm, D), lambda i, j: (i, 0)), pl.BlockSpec((tm, fc), lambda i, j: (i, j))],
        out_shape=[_sds((S, D), F32), _sds((S, D_FF), BF16)],
        compiler_params=_cp(40),
    )(*_hbm(rb, dhb, w_up_s, w_down))


def _tn_matmul(a, b, name, tk, tt, stacked, square_a=False):
    T, K = a.shape
    N = b.shape[1]
    tn = 1024

    def body(a_ref, b_ref, o_ref):
        @pl.when(pl.program_id(2) == 0)
        def _():
            o_ref[...] = jnp.zeros_like(o_ref)

        a_t = a_ref[...]
        if square_a:
            a_t = a_t * a_t
        prod = _dot_tn(a_t, b_ref[...])
        if stacked:
            o_ref[0] += prod
        else:
            o_ref[...] += prod

    if stacked:
        out_spec = pl.BlockSpec((1, tk, tn), lambda k, n, t: (n, k, 0))
        out_shape = _sds((N // tn, K, tn), F32)
    else:
        out_spec = pl.BlockSpec((tk, tn), lambda k, n, t: (k, n))
        out_shape = _sds((K, N), F32)
    return pl.pallas_call(
        body, grid=(K // tk, N // tn, T // tt), name=name,
        in_specs=[pl.BlockSpec((tt, tk), lambda k, n, t: (t, k)),
                  pl.BlockSpec((tt, tn), lambda k, n, t: (t, n))],
        out_specs=out_spec, out_shape=out_shape,
        compiler_params=_cp(40),
    )(*_hbm(a, b))


def _mix_bwd(dpre2, dx1m, xh1, rstd1, g1, w_out, on, pooled, tm, rider):
    S, D = xh1.shape
    n_t = S // tm

    def body(*refs):
        ((dp2_ref, dxm_ref, xh_ref, rstd_ref, g_ref, w_ref, on_ref, po_ref),
         (dpre1_ref, don_ref, dpo_ref, dw_ref, dg_ref, db_ref), _, ride) = rider.split(refs, 8, 6, 0)

        @pl.when(pl.program_id(0) == 0)
        def _():
            rider.first(ride)
            for r in (dw_ref, dg_ref, db_ref):
                r[...] = jnp.zeros_like(r)

        xh = xh_ref[...]
        dx1 = ALPHA * dp2_ref[...] + dxm_ref[...]
        dg_ref[...] += _colsum(dx1 * xh)
        db_ref[...] += _colsum(dx1)
        dpre1 = _ln_bwd(dx1, xh, rstd_ref[...], g_ref[...])
        dpre1_ref[...] = dpre1
        dmb = dpre1.astype(BF16)
        dcat = _dot_nt(dmb, w_ref[...])
        don_ref[...] = dcat[:, :ATTN_WIDTH]
        dpo_ref[...] = dcat[:, ATTN_WIDTH:]
        dw_ref[:ATTN_WIDTH, :] += _dot_tn(on_ref[...], dmb)
        dw_ref[ATTN_WIDTH:, :] += _dot_tn(po_ref[...], dmb)

        @pl.when(pl.program_id(0) == n_t - 1)
        def _():
            rider.last(ride)

    vec = _const_spec((1, D))
    return rider.call(
        body, [dpre2, dx1m, xh1, rstd1, g1, w_out, on, pooled], grid=(n_t,), name="mix_bwd",
        in_specs=[_row_spec(tm, D), _row_spec(tm, D), _row_spec(tm, D), _row_spec(tm, 1), vec,
                  _const_spec((D, D)), _row_spec(tm, ATTN_WIDTH), _row_spec(tm, POOL_WIDTH)],
        out_specs=[_row_spec(tm, D), _row_spec(tm, ATTN_WIDTH), _row_spec(tm, POOL_WIDTH),
                   _const_spec((D, D)), vec, vec],
        out_shape=[_sds((S, D), F32), _sds((S, ATTN_WIDTH), F32), _sds((S, POOL_WIDTH), F32),
                   _sds((D, D), F32), _sds((1, D), F32), _sds((1, D), F32)],
        scratch_shapes=[], vmem_mib=48)


def _pool_bwd(dpooled, d_b, w_pool, pscale, tm, rider):
    S = dpooled.shape[0]
    hb = tm // POOL_HALO
    n_t = S // tm
    te = tm + POOL_HALO

    def body(*refs):
        ((dp_ref, dph_ref, d_ref, wp_ref, sc_ref), (du_ref, dwp_ref, dsc_ref), _,
         ride) = rider.split(refs, 5, 3, 0)
        i = pl.program_id(0)

        @pl.when(i == 0)
        def _():
            rider.first(ride)
            dwp_ref[...] = jnp.zeros_like(dwp_ref)
            dsc_ref[...] = jnp.zeros_like(dsc_ref)

        halo = jnp.where(i < n_t - 1, dph_ref[...], 0.0)
        pos = i * tm + lax.broadcasted_iota(jnp.int32, (te, 1), 0)
        for g in range(N_POOL_GROUPS):
            win = 2 ** (g + 1)
            cols = slice(g * POOL_GROUP, (g + 1) * POOL_GROUP)
            wpb = wp_ref[g].astype(BF16)
            dpt = dp_ref[:, cols]
            dpe = jnp.concatenate([dpt, halo[:, cols]], axis=0)
            dyb = (dpe * sc_ref[:, cols]).astype(BF16)
            dd = _dot_nt(dyb, wpb)
            s = dd / jnp.minimum(pos + 1, win).astype(F32)
            for sh in (1, 2, 4, 8)[:g + 1]:
                s = s + pltpu.roll(s, te - sh, 0)
            du_ref[:, cols] = s[:tm, :] - dd[:tm, :]
            db = d_ref[:, cols]
            dwp_ref[g] += _dot_tn(db, dyb[:tm, :])
            dsc_ref[:, cols] += _colsum(dpt * _dot(db, wpb))

        @pl.when(i == n_t - 1)
        def _():
            rider.last(ride)

    return rider.call(
        body, [dpooled, dpooled, d_b, w_pool, pscale], grid=(n_t,), name="pool_bwd",
        in_specs=[_row_spec(tm, POOL_WIDTH),
                  pl.BlockSpec((POOL_HALO, POOL_WIDTH),
                               lambda i: (jnp.minimum((i + 1) * hb, S // POOL_HALO - 1), 0)),
                  _row_spec(tm, POOL_WIDTH),
                  _const_spec((N_POOL_GROUPS, POOL_GROUP, POOL_GROUP)), _const_spec((1, POOL_WIDTH))],
        out_specs=[_row_spec(tm, POOL_WIDTH), _const_spec((N_POOL_GROUPS, POOL_GROUP, POOL_GROUP)),
                   _const_spec((1, POOL_WIDTH))],
        out_shape=[_sds((S, POOL_WIDTH), F32), _sds((N_POOL_GROUPS, POOL_GROUP, POOL_GROUP), F32),
                   _sds((1, POOL_WIDTH), F32)],
        scratch_shapes=[], vmem_mib=32)


def _attn_bwd(q, k, v, don, o_raw, ga, tq, rider):
    S = q.shape[0]
    nq = S // tq

    def body(*refs):
        ((q_ref, k_ref, v_ref, don_ref, o_ref, ga_ref), (dq_ref, dk_ref, dv_ref, dga_ref),
         (g_s, b_s), ride) = rider.split(refs, 6, 4, 2)
        p, i = pl.program_id(0), pl.program_id(1)

        @pl.when(jnp.logical_and(p == 0, i == 0))
        def _():
            rider.first(ride)

        @pl.when(i == 0)
        def _():
            for r in (dk_ref, dv_ref, dga_ref):
                r[...] = jnp.zeros_like(r)

        lane = lax.broadcasted_iota(jnp.int32, (1, PAIR), 1)
        m0 = lane < HEAD_DIM
        low = _tri(tq, upper=False)
        upp = _tri(tq, upper=True)

        def seg_mean(a):
            s0 = jnp.sum(jnp.where(m0, a, 0.0), axis=-1, keepdims=True)
            s1 = jnp.sum(jnp.where(m0, 0.0, a), axis=-1, keepdims=True)
            return jnp.where(m0, s0, s1) * (1.0 / HEAD_DIM)

        o = o_ref[...]
        rs = lax.rsqrt(seg_mean(o * o) + RMS_EPS)
        oh = o * rs
        don = don_ref[...]
        dga_ref[...] += _colsum(don * oh)
        doh = don * ga_ref[...]
        do = rs * (doh - oh * seg_mean(doh * oh))
        dob = do.astype(BF16)
        q2 = q_ref[...]
        qhs = [jnp.where(m0, q2, jnp.zeros_like(q2)), jnp.where(m0, jnp.zeros_like(q2), q2)]
        dhs = [jnp.where(m0, dob, jnp.zeros_like(dob)), jnp.where(m0, jnp.zeros_like(dob), dob)]
        causal = _strictly_causal(tq)

        def down(kb, c_ls, valid):
            ks = pl.multiple_of(kb * tq, tq)
            kt = k_ref[pl.ds(ks, tq), :]
            vt = v_ref[pl.ds(ks, tq), :]
            lss, ls_, ws = _sb_tile(qhs, kt, low, c_ls, valid)
            dws = [_dot_nt(dh, vt) for dh in dhs]
            for hh in range(2):
                g_s[hh, kb] = dws[hh] * ws[hh]
                b_s[hh, kb] = jnp.exp(lss[hh])
            dv_ref[pl.ds(ks, tq), :] += (_dot_tn(ws[0].astype(BF16), dhs[0])
                                         + _dot_tn(ws[1].astype(BF16), dhs[1]))
            return [c_l + jnp.sum(l, axis=1, keepdims=True) for c_l, l in zip(c_ls, ls_)]

        zc, za = jnp.zeros((tq, 1), F32), jnp.zeros((tq, PAIR), F32)
        c_ls = lax.cond(i >= 1, lambda: tuple(down(i - 1, down(i, [zc, zc], causal), None)),
                        lambda: tuple(down(i, [zc, zc], causal)))

        def more(st):
            return jnp.logical_and(st[0] <= i, jnp.max(jnp.maximum(st[1], st[2])) > LOG_WEIGHT_FLOOR)

        def down_step(st):
            c_ls = down(i - st[0], [st[1], st[2]], None)
            return (st[0] + 1, c_ls[0], c_ls[1])

        n_tiles = lax.while_loop(more, down_step, (jnp.int32(2), c_ls[0], c_ls[1]))[0]

        def up(kb, c_gs, accs, valid):
            ks = pl.multiple_of(kb * tq, tq)
            kt = k_ref[pl.ds(ks, tq), :]
            gs = [g_s[hh, kb] for hh in range(2)]
            pres = [_dot(g.astype(BF16), upp) + c_g for g, c_g in zip(gs, c_gs)]
            dzs = []
            for hh in range(2):
                beta = b_s[hh, kb]
                dz = gs[hh] * (1.0 - beta) - beta * pres[hh]
                if valid is not None:
                    dz = jnp.where(valid, dz, 0.0)
                dzs.append(dz.astype(BF16))
            new_a = [acc + _dot(dzb, kt) for acc, dzb in zip(accs, dzs)]
            dk_ref[pl.ds(ks, tq), :] += _dot_tn(dzs[0], qhs[0]) + _dot_tn(dzs[1], qhs[1])
            new_c = [c_g + jnp.sum(g, axis=1, keepdims=True) for c_g, g in zip(c_gs, gs)]
            return new_c, new_a

        def up_step(kb, st):
            c_gs, accs = up(kb, [st[0], st[1]], [st[2], st[3]], None)
            return (c_gs[0], c_gs[1], accs[0], accs[1])

        st = lax.fori_loop(i - n_tiles + 1, i - 1, up_step, (zc, zc, za, za))

        def last_two():
            c_gs, accs = up(i - 1, [st[0], st[1]], [st[2], st[3]], None)
            return tuple(up(i, c_gs, accs, causal)[1])

        accs = lax.cond(i >= 1, last_two, lambda: tuple(up(i, [zc, zc], [za, za], causal)[1]))
        dq_ref[...] = jnp.where(m0, accs[0], accs[1]) * Q_SCALE

        @pl.when(jnp.logical_and(p == N_PAIRS - 1, i == nq - 1))
        def _():
            rider.last(ride)

    return rider.call(
        body, [q, k, v, don, o_raw, ga], grid=(N_PAIRS, nq), name="attn_bwd",
        in_specs=[pl.BlockSpec((tq, PAIR), lambda p, i: (i, p)),
                  pl.BlockSpec((S, PAIR), lambda p, i: (0, p)),
                  pl.BlockSpec((S, PAIR), lambda p, i: (0, p)),
                  pl.BlockSpec((tq, PAIR), lambda p, i: (i, p)),
                  pl.BlockSpec((tq, PAIR), lambda p, i: (i, p)),
                  pl.BlockSpec((1, PAIR), lambda p, i: (0, p))],
        out_specs=[pl.BlockSpec((tq, PAIR), lambda p, i: (i, p)),
                   pl.BlockSpec((S, PAIR), lambda p, i: (0, p)),
                   pl.BlockSpec((S, PAIR), lambda p, i: (0, p)),
                   pl.BlockSpec((1, PAIR), lambda p, i: (0, p))],
        out_shape=[_sds((S, ATTN_WIDTH), F32), _sds((S, ATTN_WIDTH), F32), _sds((S, ATTN_WIDTH), F32),
                   _sds((1, ATTN_WIDTH), F32)],
        scratch_shapes=[pltpu.VMEM((2, nq, tq, tq), F32), pltpu.VMEM((2, nq, tq, tq), F32)],
        vmem_mib=56)


def _inproj_bwd(dq, dk, dv, du, dpre1, xh0, rstd0, g0, b0, w_in_s, tm):
    S, D = xh0.shape

    def body(dq_ref, dk_ref, dv_ref, du_ref, dp1_ref, xh_ref, rstd_ref, g_ref, b_ref, w_ref,
             gx_ref, dw_ref, dg_ref, db_ref):
        @pl.when(pl.program_id(0) == 0)
        def _():
            for r in (dw_ref, dg_ref, db_ref):
                r[...] = jnp.zeros_like(r)

        xh = xh_ref[...]
        xb = (xh * g_ref[...] + b_ref[...]).astype(BF16)
        dx0 = ALPHA * dp1_ref[...]
        for c, r in enumerate((dq_ref, dk_ref, dv_ref, du_ref)):
            dpb = r[...].astype(BF16)
            dx0 = dx0 + _dot_nt(dpb, w_ref[c])
            dw_ref[c] += _dot_tn(xb, dpb)
        dg_ref[...] += _colsum(dx0 * xh)
        db_ref[...] += _colsum(dx0)
        gx_ref[...] = _ln_bwd(dx0, xh, rstd_ref[...], g_ref[...])

    vec = _const_spec((1, D))
    half = _row_spec(tm, 512)
    return pl.pallas_call(
        body, grid=(S // tm,), name="inproj_bwd",
        in_specs=[half, half, half, half, _row_spec(tm, D), _row_spec(tm, D), _row_spec(tm, 1), vec, vec,
                  _const_spec((N_CHIPS, D, 512))],
        out_specs=[_row_spec(tm, D), _const_spec((N_CHIPS, D, 512)), vec, vec],
        out_shape=[_sds((S, D), F32), _sds((N_CHIPS, D, 512), F32), _sds((1, D), F32), _sds((1, D), F32)],
        compiler_params=_cp(56),
    )(*_hbm(dq, dk, dv, du, dpre1, xh0, rstd0, g0, b0, w_in_s))


def _place():
    return lax.axis_index("x"), lax.axis_index("y"), lax.axis_index("c")


CHIP_FLIPS = ((0, 1), (1, 0), (1, 1))


class _Rider:
    def __init__(self, ins, out_shapes, n_sem, phases, aliases=None):
        self.ins, self.out_shapes, self.n_sem, self.phases = list(ins), list(out_shapes), n_sem, phases
        self.aliases = aliases or {}

    def __add__(self, other):
        na, ma = len(self.ins), len(self.out_shapes)

        def phases(ins, outs, ssem, rsem):
            mine = self.phases(ins[:na], outs[:ma], ssem, rsem)
            rest = pl.ds(self.n_sem, other.n_sem)
            theirs = other.phases(ins[na:], outs[ma:], ssem.at[rest], rsem.at[rest])
            assert len(mine) == 1 and len(theirs) == 1
            return [mine[0] + theirs[0]]

        aliases = {**self.aliases, **{na + i: ma + o for i, o in other.aliases.items()}}
        return _Rider(self.ins + other.ins, self.out_shapes + other.out_shapes, self.n_sem + other.n_sem, phases,
                      aliases)

    def split(self, refs, n_in, n_out, n_scratch):
        a = n_in + len(self.ins)
        b = a + n_out
        c = b + len(self.out_shapes)
        own = (refs[:n_in], refs[a:b], refs[c:c + n_scratch])
        return own + ((refs[n_in:a], refs[b:c]) + tuple(refs[c + n_scratch:]),)

    def first(self, ride):
        for make in self.phases(*ride)[0]:
            make().start()

    def mid(self, ride):
        ph = self.phases(*ride)
        if len(ph) == 2:
            for make in ph[0]:
                make().wait_recv()
            for make in ph[1]:
                make().start()

    def last(self, ride):
        ph = self.phases(*ride)
        if len(ph) == 2:
            for make in ph[0]:
                make().wait_send()
        for make in ph[-1]:
            make().wait()

    def call(self, body, args, *, grid, name, in_specs, out_specs, out_shape, scratch_shapes, vmem_mib,
             prefetch=None):
        n_in, n_out = len(in_specs), len(out_specs)
        sems = [pltpu.SemaphoreType.DMA((self.n_sem,)), pltpu.SemaphoreType.DMA((self.n_sem,))]
        n_pre = 0 if prefetch is None else 1
        grid_spec = pltpu.PrefetchScalarGridSpec(
            num_scalar_prefetch=n_pre, grid=grid,
            in_specs=list(in_specs) + [HBM_SPEC] * len(self.ins),
            out_specs=list(out_specs) + [HBM_SPEC] * len(self.out_shapes),
            scratch_shapes=list(scratch_shapes) + sems)
        return pl.pallas_call(
            body, name=name, grid_spec=grid_spec,
            out_shape=list(out_shape) + self.out_shapes,
            input_output_aliases={n_pre + n_in + i: n_out + o for i, o in self.aliases.items()},
            compiler_params=_cp(vmem_mib),
        )(*([] if prefetch is None else [prefetch]), *_hbm(*args), *self.ins)

    def run(self, name):
        def body(*refs):
            ride = self.split(refs, 0, 0, 0)[3]
            self.first(ride)
            self.mid(ride)
            self.last(ride)

        return self.call(body, [], grid=(), name=name, in_specs=[], out_specs=[], out_shape=[],
                         scratch_shapes=[], vmem_mib=16)


def _remote(src, dst, ssem, rsem, n, dev):
    return functools.partial(pltpu.make_async_remote_copy, src_ref=src, dst_ref=dst, send_sem=ssem.at[n],
                             recv_sem=rsem.at[n], device_id=dev, device_id_type=MESH)


def _cast_into_slot(w, place, name):
    R, C = w.shape
    tr = min(R, 512)

    def body(pl_ref, w_ref, o_ref):
        o_ref[0] = w_ref[...].astype(BF16)

    return pl.pallas_call(
        body, name=name,
        grid_spec=pltpu.PrefetchScalarGridSpec(
            num_scalar_prefetch=1, grid=(R // tr,),
            in_specs=[pl.BlockSpec((tr, C), lambda r, pr: (r, 0))],
            out_specs=pl.BlockSpec((1, tr, C), lambda r, pr: (pr[1], r, 0))),
        out_shape=_sds((N_CHIPS, R, C), BF16),
    )(place, w)


CAST_STEPS = 8


def _cast_rest(ws, place, rider):
    n = len(ws)

    def body(pl_ref, *refs):
        w_refs, o_refs, _, ride = rider.split(refs, n, n, 0)
        r = pl.program_id(0)

        @pl.when(r == 0)
        def _():
            rider.first(ride)

        @pl.when(r == CAST_STEPS // 2)
        def _():
            rider.mid(ride)

        for w_ref, o_ref in zip(w_refs, o_refs):
            o_ref[0] = w_ref[...].astype(BF16)

        @pl.when(r == CAST_STEPS - 1)
        def _():
            rider.last(ride)

    def rows(w):
        return w.shape[0] // CAST_STEPS

    return rider.call(
        body, ws, grid=(CAST_STEPS,), name="cast_weights", prefetch=place,
        in_specs=[pl.BlockSpec((rows(w), w.shape[1]), lambda r, pr: (r, 0)) for w in ws],
        out_specs=[pl.BlockSpec((1, rows(w), w.shape[1]), lambda r, pr: (pr[1], r, 0)) for w in ws],
        out_shape=[_sds((N_CHIPS,) + w.shape, BF16) for w in ws], scratch_shapes=[], vmem_mib=32)


def _gather_rider(stacked, part="both"):
    n, nf = len(stacked), len(CHIP_FLIPS)

    def phases(ins, outs, ssem, rsem):
        x, y, c = _place()
        slot = 2 * x + y
        ici, d2d = [], []
        for w, (i_ref, o_ref) in enumerate(zip(ins, outs)):
            hh = o_ref.shape[1] // 2
            rows = pl.ds(c * hh, hh)
            for f, (fx, fy) in enumerate(CHIP_FLIPS):
                k = w * nf + f
                theirs = 2 * (x ^ fx) + (y ^ fy)
                if part != "pair":
                    ici.append(_remote(i_ref.at[slot, rows], o_ref.at[slot, rows], ssem, rsem, k,
                                       (x ^ fx, y ^ fy, c)))
                if part != "chips":
                    d2d.append(_remote(o_ref.at[theirs, rows], o_ref.at[theirs, rows], ssem, rsem,
                                       (n * nf if part == "both" else 0) + k, (x, y, 1 - c)))
        return [ph for ph in (ici, d2d) if ph]

    return _Rider(stacked, [_sds(s.shape, s.dtype) for s in stacked], (2 if part == "both" else 1) * n * nf,
                  phases, aliases={i: i for i in range(n)})


def _pair_swap_rider(grads):
    def phases(ins, outs, ssem, rsem):
        x, y, c = _place()
        return [[_remote(g.at[:, 1 - c], o, ssem, rsem, k, (x, y, 1 - c))
                 for k, (g, o) in enumerate(zip(ins, outs))]]

    return _Rider(grads, [_sds((N_CHIPS,) + g.shape[2:], g.dtype) for g in grads], len(grads), phases)


def _chip_scatter_rider(parts):
    nf = len(CHIP_FLIPS)

    def phases(ins, outs, ssem, rsem):
        x, y, c = _place()
        return [[_remote(r.at[2 * (x ^ fx) + (y ^ fy)], o.at[f], ssem, rsem, w * nf + f, (x ^ fx, y ^ fy, c))
                 for w, (r, o) in enumerate(zip(ins, outs)) for f, (fx, fy) in enumerate(CHIP_FLIPS)]]

    return _Rider(parts, [_sds((nf,) + r.shape[1:], r.dtype) for r in parts], len(parts) * nf, phases)


def _pair_send_rider(halves):
    def phases(ins, outs, ssem, rsem):
        x, y, c = _place()
        return [[_remote(h, o, ssem, rsem, k, (x, y, 1 - c)) for k, (h, o) in enumerate(zip(ins, outs))]]

    return _Rider(halves, [_sds(h.shape, h.dtype) for h in halves], len(halves), phases)


def _add_pair(grad, recv, place, name):
    _, _, H, C = grad.shape
    th = min(H, 256)

    def body(pl_ref, g_ref, r_ref, o_ref, ob_ref):
        s = g_ref[:, 0] + r_ref[...]
        o_ref[...] = s
        ob_ref[...] = s.astype(BF16)

    spec = pl.BlockSpec((1, th, C), lambda j, h, pr: (j, h, 0))
    return pl.pallas_call(
        body, name=name,
        grid_spec=pltpu.PrefetchScalarGridSpec(
            num_scalar_prefetch=1, grid=(N_CHIPS, H // th),
            in_specs=[pl.BlockSpec((1, 1, th, C), lambda j, h, pr: (j, pr[0], h, 0)), spec],
            out_specs=[spec, spec]),
        out_shape=[_sds((N_CHIPS, H, C), F32), _sds((N_CHIPS, H, C), BF16)],
    )(place, *_hbm(grad, recv))


def _add_chips(part, recv, place, name):
    _, H, C = part.shape
    th = min(H, 256)

    def body(pl_ref, p_ref, r_ref, o_ref):
        o_ref[...] = p_ref[0] + r_ref[0].astype(F32) + r_ref[1].astype(F32) + r_ref[2].astype(F32)

    return pl.pallas_call(
        body, name=name,
        grid_spec=pltpu.PrefetchScalarGridSpec(
            num_scalar_prefetch=1, grid=(H // th,),
            in_specs=[pl.BlockSpec((1, th, C), lambda h, pr: (pr[1], h, 0)),
                      pl.BlockSpec((len(CHIP_FLIPS), th, C), lambda h, pr: (0, h, 0))],
            out_specs=pl.BlockSpec((th, C), lambda h, pr: (h, 0))),
        out_shape=_sds((H, C), F32),
    )(place, *_hbm(part, recv))


def _adamw_math(w, g, m, v):
    m = ADAM_B1 * m + (1.0 - ADAM_B1) * g
    v = ADAM_B2 * v + (1.0 - ADAM_B2) * (g * g)
    m_hat = m / (1.0 - ADAM_B1 ** ADAM_STEP)
    v_hat = v / (1.0 - ADAM_B2 ** ADAM_STEP)
    delta = -ADAM_LR * (m_hat / (jnp.sqrt(v_hat) + ADAM_EPS) + ADAM_WD * w)
    return delta, m, v


def _adamw(w, mine, theirs, m, v, place, name):
    R, C = w.shape
    th = min(R // 2, 256)
    nb = (R // 2) // th

    def body(pl_ref, w_ref, a_ref, b_ref, m_ref, v_ref, g_ref, d_ref, mo_ref, vo_ref):
        g = jnp.where(pl.program_id(0) == pl_ref[0], a_ref[...], b_ref[...])
        g_ref[...] = g
        d, mo, vo = _adamw_math(w_ref[...], g, m_ref[...], v_ref[...])
        d_ref[...] = d
        mo_ref[...] = mo
        vo_ref[...] = vo

    whole = pl.BlockSpec((th, C), lambda h, r, pr: (h * nb + r, 0))
    mine_spec = pl.BlockSpec((th, C), lambda h, r, pr: (jnp.where(h == pr[0], r, 0), 0))
    theirs_spec = pl.BlockSpec((th, C), lambda h, r, pr: (jnp.where(h == pr[0], 0, r), 0))
    return pl.pallas_call(
        body, name=name,
        grid_spec=pltpu.PrefetchScalarGridSpec(
            num_scalar_prefetch=1, grid=(2, nb),
            in_specs=[whole, mine_spec, theirs_spec, whole, whole], out_specs=[whole] * 4),
        out_shape=[_sds((R, C), F32)] * 4,
    )(place, *_hbm(w, mine, theirs, m, v))


DEVICE_FLIPS = tuple((fx, fy, fc) for fx in (0, 1) for fy in (0, 1) for fc in (0, 1))[1:]


def _pack_exchange_rider(pack):
    def phases(ins, outs, ssem, rsem):
        x, y, c = _place()
        mine = outs[0].at[4 * x + 2 * y + c]
        copies = [_remote(ins[0], mine, ssem, rsem, k, (x ^ fx, y ^ fy, c ^ fc))
                  for k, (fx, fy, fc) in enumerate(DEVICE_FLIPS)]
        copies.append(functools.partial(pltpu.make_async_copy, ins[0], mine, ssem.at[len(DEVICE_FLIPS)]))
        return [copies]

    return _Rider([pack], [_sds((N_DEV,) + pack.shape, pack.dtype)], len(DEVICE_FLIPS) + 1, phases)


def _small_sum_adamw(recv_a, recv_b, wpack, mpack, vpack):
    R = wpack.shape[0]

    def body(a_ref, b_ref, w_ref, m_ref, v_ref, gs_ref, d_ref, mo_ref, vo_ref):
        ta, tb = a_ref[0], b_ref[0]
        for dev in range(1, N_DEV):
            ta = ta + a_ref[dev]
            tb = tb + b_ref[dev]
        total = jnp.concatenate([ta, tb], axis=0)
        gs_ref[...] = total
        d, mo, vo = _adamw_math(w_ref[...], total, m_ref[...], v_ref[...])
        d_ref[...] = d
        mo_ref[...] = mo
        vo_ref[...] = vo

    return pl.pallas_call(
        body, name="small_sum_adamw", in_specs=[VMEM_SPEC] * 5, out_specs=[VMEM_SPEC] * 4,
        out_shape=[_sds((R, LANES), F32)] * 4,
    )(recv_a, recv_b, wpack, mpack, vpack)


def _rows8(a):
    a = a.reshape(-1, LANES)
    pad = (-a.shape[0]) % 8
    return jnp.pad(a, ((0, pad), (0, 0))) if pad else a


def _pack(parts):
    return jnp.concatenate([_rows8(a) for a in parts], axis=0)


def _unpack(pack, like):
    out, row = [], 0
    for a in like:
        n = a.size // LANES
        out.append(pack[row:row + n].reshape(a.shape))
        row += n + (-n) % 8
    return out


def kernel(x, p, emb_ln_g, emb_ln_b, w_in, attn_out_g, w_pool, pool_scale, w_out, ln1_g, ln1_b, w_up, w_down, ln2_g, ln2_b, w_ple, w_ple_gate, ln3_g, ln3_b, loss_target, m_emb_ln_g, m_emb_ln_b, m_w_in, m_attn_out_g, m_w_pool, m_pool_scale, m_w_out, m_ln1_g, m_ln1_b, m_w_up, m_w_down, m_ln2_g, m_ln2_b, m_w_ple, m_w_ple_gate, m_ln3_g, m_ln3_b, v_emb_ln_g, v_emb_ln_b, v_w_in, v_attn_out_g, v_w_pool, v_pool_scale, v_w_out, v_ln1_g, v_ln1_b, v_w_up, v_w_down, v_ln2_g, v_ln2_b, v_w_ple, v_w_ple_gate, v_ln3_g, v_ln3_b):
    S = x.shape[1]
    tm = min(256, S)
    tq = min(256, S)
    tm_mlp = min(512, S)
    xs = x[0]
    ps = p[0, 0]
    tgt = loss_target[0]
    row = lambda a: a.reshape(1, -1)
    g0, b0 = row(emb_ln_g), row(emb_ln_b)
    g1, b1, g2, b2, g3, b3 = ln1_g, ln1_b, ln2_g, ln2_b, ln3_g, ln3_b
    wp = w_pool[0]

    xi, yi, ci = _place()
    place = jnp.stack([ci, 2 * xi + yi]).astype(jnp.int32)
    names = ["w_in", "w_out", "w_up", "w_down", "w_ple", "w_ple_gate"]

    big = [w_in[0], w_out[0], w_up[0], w_down[0], w_ple[0], w_ple_gate[0]]
    s_in = _cast_into_slot(big[0], place, "cast_w_in")
    s_out, s_up, s_down, s_ple, s_gate, w_in_s = _cast_rest(big[1:], place, _gather_rider([s_in]))

    xh0, rstd0, q, k, v, u, s_out, s_ple, s_gate = _embln_inproj(
        xs, g0, b0, w_in_s, tm, _gather_rider([s_out, s_ple, s_gate], "chips"))
    o_raw, on, s_up, s_down, w_out_s, w_ple_s, w_gate_s = _attn_fwd(
        q, k, v, attn_out_g, tq, _gather_rider([s_up, s_down], "chips") + _gather_rider([s_out, s_ple, s_gate], "pair"))
    w_out_f = w_out_s.reshape(D_MODEL, D_MODEL)
    w_gate_f = w_gate_s.reshape(D_MODEL, D_MODEL)
    d_b, pooled = _pool_fwd(u, wp, pool_scale, tm)
    xh1, rstd1, x1b, w_up_s, w_down_s = _mix_ln1(on, pooled, xh0, g0, b0, w_out_f, g1, b1, tm,
                                                 _gather_rider([s_up, s_down], "pair"))
    w_down_f = w_down_s.reshape(D_FF, D_MODEL)
    xh2, rstd2, rb = _mlp_ln2(xh1, x1b, g1, b1, w_up_s, w_down_f, tm_mlp)

    (dpre2, dhb, dw_ple, dw_gate, dg3, db3, dg2, db2, loss_row) = _ple_ln3_loss(
        xh2, rstd2, g2, b2, ps, w_ple_s, w_gate_f, g3, b3, tgt, tm)
    dx1m, da = _mlp_bwd(rb, dhb, w_up_s, w_down_f, tm_mlp)
    dw_up = _tn_matmul(x1b, da, "grad_w_up", 1024, min(512, S), stacked=True)
    dw_down = _tn_matmul(rb, dhb, "grad_w_down", 1024, min(512, S), stacked=False, square_a=True)
    def halves_of(g):
        return g.reshape(N_CHIPS, 2, g.shape[1] // 2, g.shape[2])

    early_names = names[2:]
    early = [halves_of(g) for g in (dw_up, dw_down.reshape(N_CHIPS, D_FF // N_CHIPS, D_MODEL), dw_ple,
                                    dw_gate.reshape(N_CHIPS, D_MODEL // N_CHIPS, D_MODEL))]
    dpre1, don, dpooled, dw_out, dg1, db1, *early_pair = _mix_bwd(
        dpre2, dx1m, xh1, rstd1, g1, w_out_f, on, pooled, tm, _pair_swap_rider(early))
    early_sum = [_add_pair(g, r, place, "pair_sum_" + n) for g, r, n in zip(early, early_pair, early_names)]
    out_halves = halves_of(dw_out.reshape(N_CHIPS, D_MODEL // N_CHIPS, D_MODEL))
    du, dwp, dsc, out_pair = _pool_bwd(dpooled, d_b, wp, pool_scale, tm, _pair_swap_rider([out_halves]))
    out_sum = _add_pair(out_halves, out_pair, place, "pair_sum_w_out")
    pack_a = _pack([jnp.broadcast_to(loss_row, (8, LANES)), dwp, dsc, dg1, db1, dg2, db2, dg3, db3])
    riding = _chip_scatter_rider([out_sum[1]] + [b for _, b in early_sum]) + _pack_exchange_rider(pack_a)
    dq, dk, dv, dga, *arrived = _attn_bwd(q, k, v, don, o_raw, attn_out_g, tq, riding)
    early_chips, recv_a = arrived[:-1], arrived[-1]
    grad_x, dw_in, dg0, db0 = _inproj_bwd(dq, dk, dv, du, dpre1, xh0, rstd0, g0, b0, w_in_s, tm)

    in_halves = halves_of(dw_in)
    pack_b = _pack([dg0, db0, dga])
    in_pair, recv_b = (_pair_swap_rider([in_halves]) + _pack_exchange_rider(pack_b)).run("reduce_pair_late")
    in_sum = _add_pair(in_halves, in_pair, place, "pair_sum_w_in")
    (in_chips,) = _chip_scatter_rider([in_sum[1]]).run("reduce_chips_late")
    mine = [_add_chips(s, r, place, "chip_sum_" + n)
            for (s, _), r, n in zip([in_sum, out_sum] + early_sum, [in_chips] + early_chips, names)]
    theirs = _pair_send_rider(mine).run("gather_pair")

    ms = [m_w_in, m_w_out, m_w_up, m_w_down, m_w_ple, m_w_ple_gate]
    vs = [v_w_in, v_w_out, v_w_up, v_w_down, v_w_ple, v_w_ple_gate]
    big_out = {}
    for n, w, a, b, m, vv in zip(names, big, mine, theirs, ms, vs):
        res4 = _adamw(w, a, b, m[0], vv[0], place, "adamw_" + n)
        big_out[n] = tuple(r.reshape(m.shape) for r in res4)

    small_names = ["w_pool", "pool_scale", "ln1_g", "ln1_b", "ln2_g", "ln2_b", "ln3_g", "ln3_b",
                   "emb_ln_g", "emb_ln_b", "attn_out_g"]
    small_w = [w_pool, pool_scale, ln1_g, ln1_b, ln2_g, ln2_b, ln3_g, ln3_b, emb_ln_g, emb_ln_b, attn_out_g]
    small_m = [m_w_pool, m_pool_scale, m_ln1_g, m_ln1_b, m_ln2_g, m_ln2_b, m_ln3_g, m_ln3_b,
               m_emb_ln_g, m_emb_ln_b, m_attn_out_g]
    small_v = [v_w_pool, v_pool_scale, v_ln1_g, v_ln1_b, v_ln2_g, v_ln2_b, v_ln3_g, v_ln3_b,
               v_emb_ln_g, v_emb_ln_b, v_attn_out_g]
    loss_like = jnp.zeros((8, LANES), F32)
    gs, ds, mos, vos = _small_sum_adamw(recv_a, recv_b, _pack([loss_like] + small_w), _pack([loss_like] + small_m),
                                        _pack([jnp.ones((8, LANES), F32)] + small_v))
    like = [loss_like] + small_w
    gs_u, ds_u, mos_u, vos_u = (_unpack(a, like) for a in (gs, ds, mos, vos))
    loss = gs_u[0][0, 0]
    small_out = {n: (gs_u[i + 1], ds_u[i + 1], mos_u[i + 1], vos_u[i + 1]) for i, n in enumerate(small_names)}

    order = ["emb_ln_g", "emb_ln_b", "w_in", "attn_out_g", "w_pool", "pool_scale", "w_out", "ln1_g", "ln1_b",
             "w_up", "w_down", "ln2_g", "ln2_b", "w_ple", "w_ple_gate", "ln3_g", "ln3_b"]
    res = {**big_out, **small_out}
    outs = [loss, grad_x.reshape(x.shape)]
    for kind in range(4):
        outs += [res[n][kind] for n in order]
    return tuple(outs)
```

```python
import functools

import jax
import jax.numpy as jnp
from jax import lax
from jax.experimental import pallas as pl
from jax.experimental.pallas import tpu as pltpu

F32 = jnp.float32
BF16 = jnp.bfloat16

D_MODEL = 1024
ATTN_WIDTH = 512
POOL_WIDTH = 512
HEAD_DIM = 64
PAIR = 2 * HEAD_DIM
N_PAIRS = ATTN_WIDTH // PAIR
N_POOL_GROUPS = 4
POOL_GROUP = 128
POOL_HALO = 16
D_FF = 4096
PLE_DIM = 256
N_CHIPS = 4
N_DEV = 8
LN_EPS = 1e-5
RMS_EPS = 1e-6
ALPHA = float(2.0 ** 0.25)
Q_SCALE = 0.125
ADAM_LR = 0.001
ADAM_B1 = 0.9
ADAM_B2 = 0.999
ADAM_EPS = 1e-08
ADAM_WD = 0.01
ADAM_STEP = 10
LANES = 128
MIB = 1024 * 1024

MESH = pl.DeviceIdType.MESH
HBM_SPEC = pl.BlockSpec(memory_space=pltpu.HBM)
VMEM_SPEC = pl.BlockSpec(memory_space=pltpu.VMEM)


def _cp(vmem_mib):
    return pltpu.CompilerParams(vmem_limit_bytes=vmem_mib * MIB)


def _dot(a, b):
    return jnp.dot(a, b, preferred_element_type=F32)


def _dot_nt(a, b):
    return lax.dot_general(a, b, (((1,), (1,)), ((), ())), preferred_element_type=F32)


def _dot_tn(a, b):
    return lax.dot_general(a, b, (((0,), (0,)), ((), ())), preferred_element_type=F32)


def _ln_fwd(pre):
    mu = jnp.mean(pre, axis=-1, keepdims=True)
    xc = pre - mu
    var = jnp.mean(xc * xc, axis=-1, keepdims=True)
    rstd = lax.rsqrt(var + LN_EPS)
    return xc * rstd, rstd


def _ln_bwd(dy, xh, rstd, g):
    dxh = dy * g
    m1 = jnp.mean(dxh, axis=-1, keepdims=True)
    m2 = jnp.mean(dxh * xh, axis=-1, keepdims=True)
    return rstd * (dxh - m1 - xh * m2)


def _colsum(a):
    return jnp.sum(a, axis=0, keepdims=True)


def _neg_softplus(z):
    return -(jnp.maximum(z, 0.0) + jnp.log(1.0 + jnp.exp(-jnp.abs(z))))


def _split_bf16(a):
    hi = a.astype(BF16)
    lo = (a - hi.astype(F32)).astype(BF16)
    return hi, lo


def _row_spec(tm, n):
    return pl.BlockSpec((tm, n), lambda i: (i, 0))


def _const_spec(shape):
    nd = len(shape)
    return pl.BlockSpec(shape, lambda *_: (0,) * nd)


def _hbm(*arrays):
    return [pltpu.with_memory_space_constraint(a, pltpu.HBM) for a in arrays]


def _sds(shape, dtype):
    return pltpu.HBM(shape, dtype)


def _embln_inproj(x, g0, b0, w_in_s, tm, rider):
    S, D = x.shape
    n_t = S // tm

    def body(*refs):
        ((x_ref, g_ref, b_ref, w_ref), (xh_ref, rstd_ref, q_ref, k_ref, v_ref, u_ref), _,
         ride) = rider.split(refs, 4, 6, 0)
        i = pl.program_id(0)

        @pl.when(i == 0)
        def _():
            rider.first(ride)

        @pl.when(i == (3 * n_t) // 4)
        def _():
            rider.mid(ride)

        xh, rstd = _ln_fwd(x_ref[...])
        xh_ref[...] = xh
        rstd_ref[...] = rstd
        xb = (xh * g_ref[...] + b_ref[...]).astype(BF16)
        q_ref[...] = (_dot(xb, w_ref[0]) * Q_SCALE).astype(BF16)
        k_ref[...] = _dot(xb, w_ref[1]).astype(BF16)
        v_ref[...] = _dot(xb, w_ref[2]).astype(BF16)
        u_ref[...] = _dot(xb, w_ref[3])

        @pl.when(i == n_t - 1)
        def _():
            rider.last(ride)

    return rider.call(
        body, [x, g0, b0, w_in_s], grid=(n_t,), name="embln_inproj",
        in_specs=[_row_spec(tm, D), _const_spec((1, D)), _const_spec((1, D)),
                  _const_spec((N_CHIPS, D, 512))],
        out_specs=[_row_spec(tm, D), _row_spec(tm, 1), _row_spec(tm, 512), _row_spec(tm, 512),
                   _row_spec(tm, 512), _row_spec(tm, 512)],
        out_shape=[_sds((S, D), F32), _sds((S, 1), F32), _sds((S, 512), BF16), _sds((S, 512), BF16),
                   _sds((S, 512), BF16), _sds((S, 512), F32)],
        scratch_shapes=[], vmem_mib=40)


def _tri(n, upper):
    r = lax.broadcasted_iota(jnp.int32, (n, n), 0)
    c = lax.broadcasted_iota(jnp.int32, (n, n), 1)
    keep = (r < c) if upper else (r > c)
    return jnp.where(keep, 1.0, 0.0).astype(BF16)


def _strictly_causal(n):
    return lax.broadcasted_iota(jnp.int32, (n, n), 1) < lax.broadcasted_iota(jnp.int32, (n, n), 0)


LOG_WEIGHT_FLOOR = -110.0


def _sb_tile(qhs, kt, low, c_ls, valid):
    valids = valid if isinstance(valid, (list, tuple)) else [valid] * len(qhs)
    zs = [_dot_nt(qh, kt) for qh in qhs]
    lrs = [_neg_softplus(z) for z in zs]
    ls_ = [lr if m is None else jnp.where(m, lr, 0.0) for lr, m in zip(lrs, valids)]
    sfx = [_dot(l.astype(BF16), low) + c_l for l, c_l in zip(ls_, c_ls)]
    lss = [z + lr for z, lr in zip(zs, lrs)]
    ws = [jnp.exp(ls + s) for ls, s in zip(lss, sfx)]
    ws = [w if m is None else jnp.where(m, w, 0.0) for w, m in zip(ws, valids)]
    return lss, ls_, ws


def _attn_fwd(q, k, v, ga, tq, rider):
    S = q.shape[0]
    nq = S // tq

    def body(*refs):
        (q_ref, k_ref, v_ref, ga_ref), (o_ref, on_ref), _, ride = rider.split(refs, 4, 2, 0)
        p, i = pl.program_id(0), pl.program_id(1)

        @pl.when(jnp.logical_and(p == 0, i == 0))
        def _():
            rider.first(ride)

        @pl.when(jnp.logical_and(p == N_PAIRS - 1, i == 0))
        def _():
            rider.mid(ride)

        lane = lax.broadcasted_iota(jnp.int32, (1, PAIR), 1)
        m0 = lane < HEAD_DIM
        low = _tri(tq, upper=False)
        q2 = q_ref[...]
        qhs = [jnp.where(m0, q2, jnp.zeros_like(q2)), jnp.where(m0, jnp.zeros_like(q2), q2)]

        def tile(kb, c_ls, accs, valid):
            ks = pl.multiple_of(kb * tq, tq)
            kt = k_ref[pl.ds(ks, tq), :]
            vt = v_ref[pl.ds(ks, tq), :]
            _, ls_, ws = _sb_tile(qhs, kt, low, c_ls, valid)
            new_a = [acc + _dot(w.astype(BF16), vt) for acc, w in zip(accs, ws)]
            new_c = [c_l + jnp.sum(l, axis=1, keepdims=True) for c_l, l in zip(c_ls, ls_)]
            return new_c, new_a

        zc, za = jnp.zeros((tq, 1), F32), jnp.zeros((tq, PAIR), F32)

        def first_two():
            c_ls, accs = tile(i, [zc, zc], [za, za], _strictly_causal(tq))
            c_ls, accs = tile(i - 1, c_ls, accs, None)
            return (*c_ls, *accs)

        def first_one():
            c_ls, accs = tile(i, [zc, zc], [za, za], _strictly_causal(tq))
            return (*c_ls, *accs)

        st0 = lax.cond(i >= 1, first_two, first_one)

        def more(st):
            return jnp.logical_and(st[0] <= i, jnp.max(jnp.maximum(st[1], st[2])) > LOG_WEIGHT_FLOOR)

        def step(st):
            n, c0, c1, a0, a1 = st
            c_ls, accs = tile(i - n, [c0, c1], [a0, a1], None)
            return (n + 1, c_ls[0], c_ls[1], accs[0], accs[1])

        st = lax.while_loop(more, step, (jnp.int32(2), *st0))
        o = jnp.where(m0, st[3], st[4])
        o_ref[...] = o
        sq = o * o
        ms0 = jnp.sum(jnp.where(m0, sq, 0.0), axis=-1, keepdims=True) * (1.0 / HEAD_DIM)
        ms1 = jnp.sum(jnp.where(m0, 0.0, sq), axis=-1, keepdims=True) * (1.0 / HEAD_DIM)
        rs = jnp.where(m0, lax.rsqrt(ms0 + RMS_EPS), lax.rsqrt(ms1 + RMS_EPS))
        on_ref[...] = (o * rs * ga_ref[...]).astype(BF16)

        @pl.when(jnp.logical_and(p == N_PAIRS - 1, i == nq - 1))
        def _():
            rider.last(ride)

    return rider.call(
        body, [q, k, v, ga], grid=(N_PAIRS, nq), name="attn_fwd",
        in_specs=[pl.BlockSpec((tq, PAIR), lambda p, i: (i, p)),
                  pl.BlockSpec((S, PAIR), lambda p, i: (0, p)),
                  pl.BlockSpec((S, PAIR), lambda p, i: (0, p)),
                  pl.BlockSpec((1, PAIR), lambda p, i: (0, p))],
        out_specs=[pl.BlockSpec((tq, PAIR), lambda p, i: (i, p)),
                   pl.BlockSpec((tq, PAIR), lambda p, i: (i, p))],
        out_shape=[_sds((S, ATTN_WIDTH), F32), _sds((S, ATTN_WIDTH), BF16)],
        scratch_shapes=[], vmem_mib=40)


def _pool_fwd(u, w_pool, pscale, tm):
    S = u.shape[0]
    hb = tm // POOL_HALO

    def body(u_ref, uh_ref, wp_ref, sc_ref, d_ref, pooled_ref):
        i = pl.program_id(0)
        halo = jnp.where(i > 0, uh_ref[...], 0.0)
        pos = i * tm + lax.broadcasted_iota(jnp.int32, (tm, 1), 0)
        for g in range(N_POOL_GROUPS):
            win = 2 ** (g + 1)
            cols = slice(g * POOL_GROUP, (g + 1) * POOL_GROUP)
            ut = u_ref[:, cols]
            s = jnp.concatenate([halo[:, cols], ut], axis=0)
            for sh in (1, 2, 4, 8)[:g + 1]:
                s = s + pltpu.roll(s, sh, 0)
            cnt = jnp.minimum(pos + 1, win).astype(F32)
            db = (s[POOL_HALO:, :] / cnt - ut).astype(BF16)
            y = _dot(db, wp_ref[g].astype(BF16))
            d_ref[:, cols] = db
            pooled_ref[:, cols] = (y * sc_ref[:, cols]).astype(BF16)

    return pl.pallas_call(
        body, grid=(S // tm,), name="pool_fwd",
        in_specs=[_row_spec(tm, POOL_WIDTH),
                  pl.BlockSpec((POOL_HALO, POOL_WIDTH), lambda i: (jnp.maximum(i * hb - 1, 0), 0)),
                  _const_spec((N_POOL_GROUPS, POOL_GROUP, POOL_GROUP)), _const_spec((1, POOL_WIDTH))],
        out_specs=[_row_spec(tm, POOL_WIDTH), _row_spec(tm, POOL_WIDTH)],
        out_shape=[_sds((S, POOL_WIDTH), BF16), _sds((S, POOL_WIDTH), BF16)],
        compiler_params=_cp(32),
    )(*_hbm(u, u, w_pool, pscale))


def _mix_ln1(on, pooled, xh0, g0, b0, w_out, g1, b1, tm, rider):
    S, D = xh0.shape
    n_t = S // tm

    def body(*refs):
        ((on_ref, po_ref, xh0_ref, g0_ref, b0_ref, w_ref, g1_ref, b1_ref), (xh_ref, rstd_ref, xb_ref), _,
         ride) = rider.split(refs, 8, 3, 0)

        @pl.when(pl.program_id(0) == 0)
        def _():
            rider.first(ride)

        mixed = _dot(on_ref[...], w_ref[:ATTN_WIDTH, :]) + _dot(po_ref[...], w_ref[ATTN_WIDTH:, :])
        x0 = xh0_ref[...] * g0_ref[...] + b0_ref[...]
        xh, rstd = _ln_fwd(ALPHA * x0 + mixed)
        xh_ref[...] = xh
        rstd_ref[...] = rstd
        xb_ref[...] = (xh * g1_ref[...] + b1_ref[...]).astype(BF16)

        @pl.when(pl.program_id(0) == n_t - 1)
        def _():
            rider.last(ride)

    return rider.call(
        body, [on, pooled, xh0, g0, b0, w_out, g1, b1], grid=(n_t,), name="mix_ln1",
        in_specs=[_row_spec(tm, ATTN_WIDTH), _row_spec(tm, POOL_WIDTH), _row_spec(tm, D),
                  _const_spec((1, D)), _const_spec((1, D)), _const_spec((D, D)),
                  _const_spec((1, D)), _const_spec((1, D))],
        out_specs=[_row_spec(tm, D), _row_spec(tm, 1), _row_spec(tm, D)],
        out_shape=[_sds((S, D), F32), _sds((S, 1), F32), _sds((S, D), BF16)],
        scratch_shapes=[], vmem_mib=40)


def _mlp_ln2(xh1, x1b, g1, b1, w_up_s, w_down, tm):
    S, D = xh1.shape
    fc = D_FF // N_CHIPS

    def body(xh_ref, xb_ref, g_ref, b_ref, wu_ref, wd_ref, xh2_ref, rstd_ref, r_ref, acc_ref):
        j = pl.program_id(1)

        @pl.when(j == 0)
        def _():
            acc_ref[...] = jnp.zeros_like(acc_ref)

        r = jnp.maximum(_dot(xb_ref[...], wu_ref[0]), 0.0)
        r_ref[...] = r.astype(BF16)
        acc_ref[...] += _dot((r * r).astype(BF16), wd_ref[...])

        @pl.when(j == N_CHIPS - 1)
        def _():
            x1 = xh_ref[...] * g_ref[...] + b_ref[...]
            xh, rstd = _ln_fwd(ALPHA * x1 + acc_ref[...])
            xh2_ref[...] = xh
            rstd_ref[...] = rstd

    return pl.pallas_call(
        body, grid=(S // tm, N_CHIPS), name="mlp_ln2",
        in_specs=[pl.BlockSpec((tm, D), lambda i, j: (i, 0)), pl.BlockSpec((tm, D), lambda i, j: (i, 0)),
                  pl.BlockSpec((1, D), lambda i, j: (0, 0)), pl.BlockSpec((1, D), lambda i, j: (0, 0)),
                  pl.BlockSpec((1, D, fc), lambda i, j: (j, 0, 0)),
                  pl.BlockSpec((fc, D), lambda i, j: (j, 0))],
        out_specs=[pl.BlockSpec((tm, D), lambda i, j: (i, 0)), pl.BlockSpec((tm, 1), lambda i, j: (i, 0)),
                   pl.BlockSpec((tm, fc), lambda i, j: (i, j))],
        out_shape=[_sds((S, D), F32), _sds((S, 1), F32), _sds((S, D_FF), BF16)],
        scratch_shapes=[pltpu.VMEM((tm, D), F32)],
        compiler_params=_cp(40),
    )(*_hbm(xh1, x1b, g1, b1, w_up_s, w_down))


def _ple_ln3_loss(xh2, rstd2, g2, b2, p, w_ple_s, w_gate, g3, b3, target, tm):
    S, D = xh2.shape
    pc = D // N_CHIPS

    def body(xh2_ref, rstd2_ref, g2_ref, b2_ref, p_ref, wp_ref, wg_ref, g3_ref, b3_ref, t_ref,
             dpre2_ref, dhb_ref, dwp_ref, dwg_ref, dg3_ref, db3_ref, dg2_ref, db2_ref, loss_ref):
        i = pl.program_id(0)

        @pl.when(i == 0)
        def _():
            for r in (dwp_ref, dwg_ref, dg3_ref, db3_ref, dg2_ref, db2_ref, loss_ref):
                r[...] = jnp.zeros_like(r)

        xh2 = xh2_ref[...]
        x2 = xh2 * g2_ref[...] + b2_ref[...]
        x2b = x2.astype(BF16)
        gate = 1.0 / (1.0 + jnp.exp(-_dot(x2b, wg_ref[...])))
        pb = p_ref[...].astype(BF16)
        pe = jnp.concatenate([_dot(pb, wp_ref[c]) for c in range(N_CHIPS)], axis=1)
        xh3, rstd3 = _ln_fwd(ALPHA * x2 + pe * gate)
        diff = xh3 * g3_ref[...] + b3_ref[...] - t_ref[...]
        loss_ref[...] += (0.5 / D) * jnp.sum(diff * diff)
        dy = diff * (1.0 / D)
        dg3_ref[...] += _colsum(dy * xh3)
        db3_ref[...] += _colsum(dy)
        dpre3 = _ln_bwd(dy, xh3, rstd3, g3_ref[...])
        dpe_b = (dpre3 * gate).astype(BF16)
        dgp_b = (dpre3 * pe * gate * (1.0 - gate)).astype(BF16)
        dx2 = ALPHA * dpre3 + _dot_nt(dgp_b, wg_ref[...])
        dwg_ref[...] += _dot_tn(x2b, dgp_b)
        for c in range(N_CHIPS):
            dwp_ref[c] += _dot_tn(pb, dpe_b[:, c * pc:(c + 1) * pc])
        dg2_ref[...] += _colsum(dx2 * xh2)
        db2_ref[...] += _colsum(dx2)
        dpre2 = _ln_bwd(dx2, xh2, rstd2_ref[...], g2_ref[...])
        dpre2_ref[...] = dpre2
        dhb_ref[...] = dpre2.astype(BF16)

    vec = _const_spec((1, D))
    return pl.pallas_call(
        body, grid=(S // tm,), name="ple_ln3_loss",
        in_specs=[_row_spec(tm, D), _row_spec(tm, 1), vec, vec, _row_spec(tm, PLE_DIM),
                  _const_spec((N_CHIPS, PLE_DIM, pc)), _const_spec((D, D)), vec, vec, _row_spec(tm, D)],
        out_specs=[_row_spec(tm, D), _row_spec(tm, D), _const_spec((N_CHIPS, PLE_DIM, pc)),
                   _const_spec((D, D)), vec, vec, vec, vec, _const_spec((1, LANES))],
        out_shape=[_sds((S, D), F32), _sds((S, D), BF16), _sds((N_CHIPS, PLE_DIM, pc), F32),
                   _sds((D, D), F32), _sds((1, D), F32), _sds((1, D), F32), _sds((1, D), F32),
                   _sds((1, D), F32), _sds((1, LANES), F32)],
        compiler_params=_cp(48),
    )(*_hbm(xh2, rstd2, g2, b2, p, w_ple_s, w_gate, g3, b3, target))


def _mlp_bwd(rb, dhb, w_up_s, w_down, tm):
    S, D = dhb.shape
    fc = D_FF // N_CHIPS

    def body(r_ref, dh_ref, wu_ref, wd_ref, dx_ref, da_ref):
        @pl.when(pl.program_id(1) == 0)
        def _():
            dx_ref[...] = jnp.zeros_like(dx_ref)

        da = (_dot_nt(dh_ref[...], wd_ref[...]) * (2.0 * r_ref[...].astype(F32))).astype(BF16)
        da_ref[...] = da
        dx_ref[...] += _dot_nt(da, wu_ref[0])

    return pl.pallas_call(
        body, grid=(S // tm, N_CHIPS), name="mlp_bwd",
        in_specs=[pl.BlockSpec((tm, fc), lambda i, j: (i, j)), pl.BlockSpec((tm, D), lambda i, j: (i, 0)),
                  pl.BlockSpec((1, D, fc), lambda i, j: (j, 0, 0)),
                  pl.BlockSpec((fc, D), lambda i, j: (j, 0))],
        out_specs=[pl.BlockSpec((tm, D), lambda i, j: (i, 0)), pl.BlockSpec((tm, fc), lambda i, j: (i, j))],
        out_shape=[_sds((S, D), F32), _sds((S, D_FF), BF16)],
        compiler_params=_cp(40),
    )(*_hbm(rb, dhb, w_up_s, w_down))


def _tn_matmul(a, b, name, tk, tt, stacked, square_a=False):
    T, K = a.shape
    N = b.shape[1]
    tn = 1024

    def body(a_ref, b_ref, o_ref):
        @pl.when(pl.program_id(2) == 0)
        def _():
            o_ref[...] = jnp.zeros_like(o_ref)

        a_t = a_ref[...]
        if square_a:
            a_t = a_t * a_t
        prod = _dot_tn(a_t, b_ref[...])
        if stacked:
            o_ref[0] += prod
        else:
            o_ref[...] += prod

    if stacked:
        out_spec = pl.BlockSpec((1, tk, tn), lambda k, n, t: (n, k, 0))
        out_shape = _sds((N // tn, K, tn), F32)
    else:
        out_spec = pl.BlockSpec((tk, tn), lambda k, n, t: (k, n))
        out_shape = _sds((K, N), F32)
    return pl.pallas_call(
        body, grid=(K // tk, N // tn, T // tt), name=name,
        in_specs=[pl.BlockSpec((tt, tk), lambda k, n, t: (t, k)),
                  pl.BlockSpec((tt, tn), lambda k, n, t: (t, n))],
        out_specs=out_spec, out_shape=out_shape,
        compiler_params=_cp(40),
    )(*_hbm(a, b))


def _mix_bwd(dpre2, dx1m, xh1, rstd1, g1, w_out, on, pooled, tm, rider):
    S, D = xh1.shape
    n_t = S // tm

    def body(*refs):
        ((dp2_ref, dxm_ref, xh_ref, rstd_ref, g_ref, w_ref, on_ref, po_ref),
         (dpre1_ref, don_ref, dpo_ref, dw_ref, dg_ref, db_ref), _, ride) = rider.split(refs, 8, 6, 0)

        @pl.when(pl.program_id(0) == 0)
        def _():
            rider.first(ride)
            for r in (dw_ref, dg_ref, db_ref):
                r[...] = jnp.zeros_like(r)

        xh = xh_ref[...]
        dx1 = ALPHA * dp2_ref[...] + dxm_ref[...]
        dg_ref[...] += _colsum(dx1 * xh)
        db_ref[...] += _colsum(dx1)
        dpre1 = _ln_bwd(dx1, xh, rstd_ref[...], g_ref[...])
        dpre1_ref[...] = dpre1
        dmb = dpre1.astype(BF16)
        dcat = _dot_nt(dmb, w_ref[...])
        don_ref[...] = dcat[:, :ATTN_WIDTH]
        dpo_ref[...] = dcat[:, ATTN_WIDTH:]
        dw_ref[:ATTN_WIDTH, :] += _dot_tn(on_ref[...], dmb)
        dw_ref[ATTN_WIDTH:, :] += _dot_tn(po_ref[...], dmb)

        @pl.when(pl.program_id(0) == n_t - 1)
        def _():
            rider.last(ride)

    vec = _const_spec((1, D))
    return rider.call(
        body, [dpre2, dx1m, xh1, rstd1, g1, w_out, on, pooled], grid=(n_t,), name="mix_bwd",
        in_specs=[_row_spec(tm, D), _row_spec(tm, D), _row_spec(tm, D), _row_spec(tm, 1), vec,
                  _const_spec((D, D)), _row_spec(tm, ATTN_WIDTH), _row_spec(tm, POOL_WIDTH)],
        out_specs=[_row_spec(tm, D), _row_spec(tm, ATTN_WIDTH), _row_spec(tm, POOL_WIDTH),
                   _const_spec((D, D)), vec, vec],
        out_shape=[_sds((S, D), F32), _sds((S, ATTN_WIDTH), F32), _sds((S, POOL_WIDTH), F32),
                   _sds((D, D), F32), _sds((1, D), F32), _sds((1, D), F32)],
        scratch_shapes=[], vmem_mib=48)


def _pool_bwd(dpooled, d_b, w_pool, pscale, tm, rider):
    S = dpooled.shape[0]
    hb = tm // POOL_HALO
    n_t = S // tm
    te = tm + POOL_HALO

    def body(*refs):
        ((dp_ref, dph_ref, d_ref, wp_ref, sc_ref), (du_ref, dwp_ref, dsc_ref), _,
         ride) = rider.split(refs, 5, 3, 0)
        i = pl.program_id(0)

        @pl.when(i == 0)
        def _():
            rider.first(ride)
            dwp_ref[...] = jnp.zeros_like(dwp_ref)
            dsc_ref[...] = jnp.zeros_like(dsc_ref)

        halo = jnp.where(i < n_t - 1, dph_ref[...], 0.0)
        pos = i * tm + lax.broadcasted_iota(jnp.int32, (te, 1), 0)
        for g in range(N_POOL_GROUPS):
            win = 2 ** (g + 1)
            cols = slice(g * POOL_GROUP, (g + 1) * POOL_GROUP)
            wpb = wp_ref[g].astype(BF16)
            dpt = dp_ref[:, cols]
            dpe = jnp.concatenate([dpt, halo[:, cols]], axis=0)
            dyb = (dpe * sc_ref[:, cols]).astype(BF16)
            dd = _dot_nt(dyb, wpb)
            s = dd / jnp.minimum(pos + 1, win).astype(F32)
            for sh in (1, 2, 4, 8)[:g + 1]:
                s = s + pltpu.roll(s, te - sh, 0)
            du_ref[:, cols] = s[:tm, :] - dd[:tm, :]
            db = d_ref[:, cols]
            dwp_ref[g] += _dot_tn(db, dyb[:tm, :])
            dsc_ref[:, cols] += _colsum(dpt * _dot(db, wpb))

        @pl.when(i == n_t - 1)
        def _():
            rider.last(ride)

    return rider.call(
        body, [dpooled, dpooled, d_b, w_pool, pscale], grid=(n_t,), name="pool_bwd",
        in_specs=[_row_spec(tm, POOL_WIDTH),
                  pl.BlockSpec((POOL_HALO, POOL_WIDTH),
                               lambda i: (jnp.minimum((i + 1) * hb, S // POOL_HALO - 1), 0)),
                  _row_spec(tm, POOL_WIDTH),
                  _const_spec((N_POOL_GROUPS, POOL_GROUP, POOL_GROUP)), _const_spec((1, POOL_WIDTH))],
        out_specs=[_row_spec(tm, POOL_WIDTH), _const_spec((N_POOL_GROUPS, POOL_GROUP, POOL_GROUP)),
                   _const_spec((1, POOL_WIDTH))],
        out_shape=[_sds((S, POOL_WIDTH), F32), _sds((N_POOL_GROUPS, POOL_GROUP, POOL_GROUP), F32),
                   _sds((1, POOL_WIDTH), F32)],
        scratch_shapes=[], vmem_mib=32)


def _attn_bwd(q, k, v, don, o_raw, ga, tq, rider):
    S = q.shape[0]
    nq = S // tq

    def body(*refs):
        ((q_ref, k_ref, v_ref, don_ref, o_ref, ga_ref), (dq_ref, dk_ref, dv_ref, dga_ref),
         (g_s, b_s), ride) = rider.split(refs, 6, 4, 2)
        p, i = pl.program_id(0), pl.program_id(1)

        @pl.when(jnp.logical_and(p == 0, i == 0))
        def _():
            rider.first(ride)

        @pl.when(i == 0)
        def _():
            for r in (dk_ref, dv_ref, dga_ref):
                r[...] = jnp.zeros_like(r)

        lane = lax.broadcasted_iota(jnp.int32, (1, PAIR), 1)
        m0 = lane < HEAD_DIM
        low = _tri(tq, upper=False)
        upp = _tri(tq, upper=True)

        def seg_mean(a):
            s0 = jnp.sum(jnp.where(m0, a, 0.0), axis=-1, keepdims=True)
            s1 = jnp.sum(jnp.where(m0, 0.0, a), axis=-1, keepdims=True)
            return jnp.where(m0, s0, s1) * (1.0 / HEAD_DIM)

        o = o_ref[...]
        rs = lax.rsqrt(seg_mean(o * o) + RMS_EPS)
        oh = o * rs
        don = don_ref[...]
        dga_ref[...] += _colsum(don * oh)
        doh = don * ga_ref[...]
        do = rs * (doh - oh * seg_mean(doh * oh))
        dob = do.astype(BF16)
        q2 = q_ref[...]
        qhs = [jnp.where(m0, q2, jnp.zeros_like(q2)), jnp.where(m0, jnp.zeros_like(q2), q2)]
        dhs = [jnp.where(m0, dob, jnp.zeros_like(dob)), jnp.where(m0, jnp.zeros_like(dob), dob)]
        causal = _strictly_causal(tq)

        def down(kb, c_ls, valid):
            ks = pl.multiple_of(kb * tq, tq)
            kt = k_ref[pl.ds(ks, tq), :]
            vt = v_ref[pl.ds(ks, tq), :]
            lss, ls_, ws = _sb_tile(qhs, kt, low, c_ls, valid)
            dws = [_dot_nt(dh, vt) for dh in dhs]
            for hh in range(2):
                g_s[hh, kb] = dws[hh] * ws[hh]
                b_s[hh, kb] = jnp.exp(lss[hh])
            dv_ref[pl.ds(ks, tq), :] += (_dot_tn(ws[0].astype(BF16), dhs[0])
                                         + _dot_tn(ws[1].astype(BF16), dhs[1]))
            return [c_l + jnp.sum(l, axis=1, keepdims=True) for c_l, l in zip(c_ls, ls_)]

        zc, za = jnp.zeros((tq, 1), F32), jnp.zeros((tq, PAIR), F32)
        c_ls = lax.cond(i >= 1, lambda: tuple(down(i - 1, down(i, [zc, zc], causal), None)),
                        lambda: tuple(down(i, [zc, zc], causal)))

        def more(st):
            return jnp.logical_and(st[0] <= i, jnp.max(jnp.maximum(st[1], st[2])) > LOG_WEIGHT_FLOOR)

        def down_step(st):
            c_ls = down(i - st[0], [st[1], st[2]], None)
            return (st[0] + 1, c_ls[0], c_ls[1])

        n_tiles = lax.while_loop(more, down_step, (jnp.int32(2), c_ls[0], c_ls[1]))[0]

        def up(kb, c_gs, accs, valid):
            ks = pl.multiple_of(kb * tq, tq)
            kt = k_ref[pl.ds(ks, tq), :]
            gs = [g_s[hh, kb] for hh in range(2)]
            pres = [_dot(g.astype(BF16), upp) + c_g for g, c_g in zip(gs, c_gs)]
            dzs = []
            for hh in range(2):
                beta = b_s[hh, kb]
                dz = gs[hh] * (1.0 - beta) - beta * pres[hh]
                if valid is not None:
                    dz = jnp.where(valid, dz, 0.0)
                dzs.append(dz.astype(BF16))
            new_a = [acc + _dot(dzb, kt) for acc, dzb in zip(accs, dzs)]
            dk_ref[pl.ds(ks, tq), :] += _dot_tn(dzs[0], qhs[0]) + _dot_tn(dzs[1], qhs[1])
            new_c = [c_g + jnp.sum(g, axis=1, keepdims=True) for c_g, g in zip(c_gs, gs)]
            return new_c, new_a

        def up_step(kb, st):
            c_gs, accs = up(kb, [st[0], st[1]], [st[2], st[3]], None)
            return (c_gs[0], c_gs[1], accs[0], accs[1])

        st = lax.fori_loop(i - n_tiles + 1, i - 1, up_step, (zc, zc, za, za))

        def last_two():
            c_gs, accs = up(i - 1, [st[0], st[1]], [st[2], st[3]], None)
            return tuple(up(i, c_gs, accs, causal)[1])

        accs = lax.cond(i >= 1, last_two, lambda: tuple(up(i, [zc, zc], [za, za], causal)[1]))
        dq_ref[...] = jnp.where(m0, accs[0], accs[1]) * Q_SCALE

        @pl.when(jnp.logical_and(p == N_PAIRS - 1, i == nq - 1))
        def _():
            rider.last(ride)

    return rider.call(
        body, [q, k, v, don, o_raw, ga], grid=(N_PAIRS, nq), name="attn_bwd",
        in_specs=[pl.BlockSpec((tq, PAIR), lambda p, i: (i, p)),
                  pl.BlockSpec((S, PAIR), lambda p, i: (0, p)),
                  pl.BlockSpec((S, PAIR), lambda p, i: (0, p)),
                  pl.BlockSpec((tq, PAIR), lambda p, i: (i, p)),
                  pl.BlockSpec((tq, PAIR), lambda p, i: (i, p)),
                  pl.BlockSpec((1, PAIR), lambda p, i: (0, p))],
        out_specs=[pl.BlockSpec((tq, PAIR), lambda p, i: (i, p)),
                   pl.BlockSpec((S, PAIR), lambda p, i: (0, p)),
                   pl.BlockSpec((S, PAIR), lambda p, i: (0, p)),
                   pl.BlockSpec((1, PAIR), lambda p, i: (0, p))],
        out_shape=[_sds((S, ATTN_WIDTH), F32), _sds((S, ATTN_WIDTH), F32), _sds((S, ATTN_WIDTH), F32),
                   _sds((1, ATTN_WIDTH), F32)],
        scratch_shapes=[pltpu.VMEM((2, nq, tq, tq), F32), pltpu.VMEM((2, nq, tq, tq), F32)],
        vmem_mib=56)


def _inproj_bwd(dq, dk, dv, du, dpre1, xh0, rstd0, g0, b0, w_in_s, tm):
    S, D = xh0.shape

    def body(dq_ref, dk_ref, dv_ref, du_ref, dp1_ref, xh_ref, rstd_ref, g_ref, b_ref, w_ref,
             gx_ref, dw_ref, dg_ref, db_ref):
        @pl.when(pl.program_id(0) == 0)
        def _():
            for r in (dw_ref, dg_ref, db_ref):
                r[...] = jnp.zeros_like(r)

        xh = xh_ref[...]
        xb = (xh * g_ref[...] + b_ref[...]).astype(BF16)
        dx0 = ALPHA * dp1_ref[...]
        for c, r in enumerate((dq_ref, dk_ref, dv_ref, du_ref)):
            dpb = r[...].astype(BF16)
            dx0 = dx0 + _dot_nt(dpb, w_ref[c])
            dw_ref[c] += _dot_tn(xb, dpb)
        dg_ref[...] += _colsum(dx0 * xh)
        db_ref[...] += _colsum(dx0)
        gx_ref[...] = _ln_bwd(dx0, xh, rstd_ref[...], g_ref[...])

    vec = _const_spec((1, D))
    half = _row_spec(tm, 512)
    return pl.pallas_call(
        body, grid=(S // tm,), name="inproj_bwd",
        in_specs=[half, half, half, half, _row_spec(tm, D), _row_spec(tm, D), _row_spec(tm, 1), vec, vec,
                  _const_spec((N_CHIPS, D, 512))],
        out_specs=[_row_spec(tm, D), _const_spec((N_CHIPS, D, 512)), vec, vec],
        out_shape=[_sds((S, D), F32), _sds((N_CHIPS, D, 512), F32), _sds((1, D), F32), _sds((1, D), F32)],
        compiler_params=_cp(56),
    )(*_hbm(dq, dk, dv, du, dpre1, xh0, rstd0, g0, b0, w_in_s))


def _place():
    return lax.axis_index("x"), lax.axis_index("y"), lax.axis_index("c")


CHIP_FLIPS = ((0, 1), (1, 0), (1, 1))


class _Rider:
    def __init__(self, ins, out_shapes, n_sem, phases, aliases=None):
        self.ins, self.out_shapes, self.n_sem, self.phases = list(ins), list(out_shapes), n_sem, phases
        self.aliases = aliases or {}

    def __add__(self, other):
        na, ma = len(self.ins), len(self.out_shapes)

        def phases(ins, outs, ssem, rsem):
            mine = self.phases(ins[:na], outs[:ma], ssem, rsem)
            rest = pl.ds(self.n_sem, other.n_sem)
            theirs = other.phases(ins[na:], outs[ma:], ssem.at[rest], rsem.at[rest])
            assert len(mine) == 1 and len(theirs) == 1
            return [mine[0] + theirs[0]]

        aliases = {**self.aliases, **{na + i: ma + o for i, o in other.aliases.items()}}
        return _Rider(self.ins + other.ins, self.out_shapes + other.out_shapes, self.n_sem + other.n_sem, phases,
                      aliases)

    def split(self, refs, n_in, n_out, n_scratch):
        a = n_in + len(self.ins)
        b = a + n_out
        c = b + len(self.out_shapes)
        own = (refs[:n_in], refs[a:b], refs[c:c + n_scratch])
        return own + ((refs[n_in:a], refs[b:c]) + tuple(refs[c + n_scratch:]),)

    def first(self, ride):
        for make in self.phases(*ride)[0]:
            make().start()

    def mid(self, ride):
        ph = self.phases(*ride)
        if len(ph) == 2:
            for make in ph[0]:
                make().wait_recv()
            for make in ph[1]:
                make().start()

    def last(self, ride):
        ph = self.phases(*ride)
        if len(ph) == 2:
            for make in ph[0]:
                make().wait_send()
        for make in ph[-1]:
            make().wait()

    def call(self, body, args, *, grid, name, in_specs, out_specs, out_shape, scratch_shapes, vmem_mib,
             prefetch=None):
        n_in, n_out = len(in_specs), len(out_specs)
        sems = [pltpu.SemaphoreType.DMA((self.n_sem,)), pltpu.SemaphoreType.DMA((self.n_sem,))]
        n_pre = 0 if prefetch is None else 1
        grid_spec = pltpu.PrefetchScalarGridSpec(
            num_scalar_prefetch=n_pre, grid=grid,
            in_specs=list(in_specs) + [HBM_SPEC] * len(self.ins),
            out_specs=list(out_specs) + [HBM_SPEC] * len(self.out_shapes),
            scratch_shapes=list(scratch_shapes) + sems)
        return pl.pallas_call(
            body, name=name, grid_spec=grid_spec,
            out_shape=list(out_shape) + self.out_shapes,
            input_output_aliases={n_pre + n_in + i: n_out + o for i, o in self.aliases.items()},
            compiler_params=_cp(vmem_mib),
        )(*([] if prefetch is None else [prefetch]), *_hbm(*args), *self.ins)

    def run(self, name):
        def body(*refs):
            ride = self.split(refs, 0, 0, 0)[3]
            self.first(ride)
            self.mid(ride)
            self.last(ride)

        return self.call(body, [], grid=(), name=name, in_specs=[], out_specs=[], out_shape=[],
                         scratch_shapes=[], vmem_mib=16)


def _remote(src, dst, ssem, rsem, n, dev):
    return functools.partial(pltpu.make_async_remote_copy, src_ref=src, dst_ref=dst, send_sem=ssem.at[n],
                             recv_sem=rsem.at[n], device_id=dev, device_id_type=MESH)


def _cast_into_slot(w, place, name):
    R, C = w.shape
    tr = min(R, 512)

    def body(pl_ref, w_ref, o_ref):
        o_ref[0] = w_ref[...].astype(BF16)

    return pl.pallas_call(
        body, name=name,
        grid_spec=pltpu.PrefetchScalarGridSpec(
            num_scalar_prefetch=1, grid=(R // tr,),
            in_specs=[pl.BlockSpec((tr, C), lambda r, pr: (r, 0))],
            out_specs=pl.BlockSpec((1, tr, C), lambda r, pr: (pr[1], r, 0))),
        out_shape=_sds((N_CHIPS, R, C), BF16),
    )(place, w)


CAST_STEPS = 8


def _cast_rest(ws, place, rider):
    n = len(ws)

    def body(pl_ref, *refs):
        w_refs, o_refs, _, ride = rider.split(refs, n, n, 0)
        r = pl.program_id(0)

        @pl.when(r == 0)
        def _():
            rider.first(ride)

        @pl.when(r == CAST_STEPS // 2)
        def _():
            rider.mid(ride)

        for w_ref, o_ref in zip(w_refs, o_refs):
            o_ref[0] = w_ref[...].astype(BF16)

        @pl.when(r == CAST_STEPS - 1)
        def _():
            rider.last(ride)

    def rows(w):
        return w.shape[0] // CAST_STEPS

    return rider.call(
        body, ws, grid=(CAST_STEPS,), name="cast_weights", prefetch=place,
        in_specs=[pl.BlockSpec((rows(w), w.shape[1]), lambda r, pr: (r, 0)) for w in ws],
        out_specs=[pl.BlockSpec((1, rows(w), w.shape[1]), lambda r, pr: (pr[1], r, 0)) for w in ws],
        out_shape=[_sds((N_CHIPS,) + w.shape, BF16) for w in ws], scratch_shapes=[], vmem_mib=32)


def _gather_rider(stacked, part="both"):
    n, nf = len(stacked), len(CHIP_FLIPS)

    def phases(ins, outs, ssem, rsem):
        x, y, c = _place()
        slot = 2 * x + y
        ici, d2d = [], []
        for w, (i_ref, o_ref) in enumerate(zip(ins, outs)):
            hh = o_ref.shape[1] // 2
            rows = pl.ds(c * hh, hh)
            for f, (fx, fy) in enumerate(CHIP_FLIPS):
                k = w * nf + f
                theirs = 2 * (x ^ fx) + (y ^ fy)
                if part != "pair":
                    ici.append(_remote(i_ref.at[slot, rows], o_ref.at[slot, rows], ssem, rsem, k,
                                       (x ^ fx, y ^ fy, c)))
                if part != "chips":
                    d2d.append(_remote(o_ref.at[theirs, rows], o_ref.at[theirs, rows], ssem, rsem,
                                       (n * nf if part == "both" else 0) + k, (x, y, 1 - c)))
        return [ph for ph in (ici, d2d) if ph]

    return _Rider(stacked, [_sds(s.shape, s.dtype) for s in stacked], (2 if part == "both" else 1) * n * nf,
                  phases, aliases={i: i for i in range(n)})


def _pair_swap_rider(grads):
    def phases(ins, outs, ssem, rsem):
        x, y, c = _place()
        return [[_remote(g.at[:, 1 - c], o, ssem, rsem, k, (x, y, 1 - c))
                 for k, (g, o) in enumerate(zip(ins, outs))]]

    return _Rider(grads, [_sds((N_CHIPS,) + g.shape[2:], g.dtype) for g in grads], len(grads), phases)


def _chip_scatter_rider(parts):
    nf = len(CHIP_FLIPS)

    def phases(ins, outs, ssem, rsem):
        x, y, c = _place()
        return [[_remote(r.at[2 * (x ^ fx) + (y ^ fy)], o.at[f], ssem, rsem, w * nf + f, (x ^ fx, y ^ fy, c))
                 for w, (r, o) in enumerate(zip(ins, outs)) for f, (fx, fy) in enumerate(CHIP_FLIPS)]]

    return _Rider(parts, [_sds((nf,) + r.shape[1:], r.dtype) for r in parts], len(parts) * nf, phases)


def _pair_send_rider(halves):
    def phases(ins, outs, ssem, rsem):
        x, y, c = _place()
        return [[_remote(h, o, ssem, rsem, k, (x, y, 1 - c)) for k, (h, o) in enumerate(zip(ins, outs))]]

    return _Rider(halves, [_sds(h.shape, h.dtype) for h in halves], len(halves), phases)


TAIL_STEPS = 8


def _no_rider():
    return _Rider([], [], 1, lambda ins, outs, ssem, rsem: [[]])


def _add_pair(grads, recvs, place, name, rider):
    n = len(grads)

    def body(pl_ref, *refs):
        ins, outs, _, ride = rider.split(refs, 2 * n, 2 * n, 0)
        j, h = pl.program_id(0), pl.program_id(1)

        @pl.when(jnp.logical_and(j == 0, h == 0))
        def _():
            rider.first(ride)

        for w in range(n):
            s = ins[2 * w][:, 0] + ins[2 * w + 1][...]
            outs[2 * w][...] = s
            outs[2 * w + 1][...] = s.astype(BF16)

        @pl.when(jnp.logical_and(j == N_CHIPS - 1, h == TAIL_STEPS - 1))
        def _():
            rider.last(ride)

    in_specs, out_specs, out_shape, args = [], [], [], []
    for g, r in zip(grads, recvs):
        _, _, H, C = g.shape
        th = H // TAIL_STEPS
        spec = pl.BlockSpec((1, th, C), lambda j, h, pr: (j, h, 0))
        in_specs += [pl.BlockSpec((1, 1, th, C), lambda j, h, pr: (j, pr[0], h, 0)), spec]
        out_specs += [spec, spec]
        out_shape += [_sds((N_CHIPS, H, C), F32), _sds((N_CHIPS, H, C), BF16)]
        args += [g, r]
    res = rider.call(body, args, grid=(N_CHIPS, TAIL_STEPS), name=name, prefetch=place, in_specs=in_specs,
                     out_specs=out_specs, out_shape=out_shape, scratch_shapes=[], vmem_mib=32)
    return [(res[2 * w], res[2 * w + 1]) for w in range(n)], res[2 * n:]


def _add_chips(parts, recvs, place, name, rider):
    n = len(parts)

    def body(pl_ref, *refs):
        ins, outs, _, ride = rider.split(refs, 2 * n, n, 0)
        h = pl.program_id(0)

        @pl.when(h == 0)
        def _():
            rider.first(ride)

        for w in range(n):
            p_ref, r_ref = ins[2 * w], ins[2 * w + 1]
            outs[w][...] = p_ref[0] + r_ref[0].astype(F32) + r_ref[1].astype(F32) + r_ref[2].astype(F32)

        @pl.when(h == TAIL_STEPS - 1)
        def _():
            rider.last(ride)

    in_specs, out_specs, out_shape, args = [], [], [], []
    for p, r in zip(parts, recvs):
        _, H, C = p.shape
        th = H // TAIL_STEPS
        in_specs += [pl.BlockSpec((1, th, C), lambda h, pr: (pr[1], h, 0)),
                     pl.BlockSpec((len(CHIP_FLIPS), th, C), lambda h, pr: (0, h, 0))]
        out_specs.append(pl.BlockSpec((th, C), lambda h, pr: (h, 0)))
        out_shape.append(_sds((H, C), F32))
        args += [p, r]
    res = rider.call(body, args, grid=(TAIL_STEPS,), name=name, prefetch=place, in_specs=in_specs,
                     out_specs=out_specs, out_shape=out_shape, scratch_shapes=[], vmem_mib=32)
    return res[:n], res[n:]


def _adamw_math(w, g, m, v):
    m = ADAM_B1 * m + (1.0 - ADAM_B1) * g
    v = ADAM_B2 * v + (1.0 - ADAM_B2) * (g * g)
    m_hat = m / (1.0 - ADAM_B1 ** ADAM_STEP)
    v_hat = v / (1.0 - ADAM_B2 ** ADAM_STEP)
    delta = -ADAM_LR * (m_hat / (jnp.sqrt(v_hat) + ADAM_EPS) + ADAM_WD * w)
    return delta, m, v


def _adamw(ws, mines, theirs, ms, vs, place, name, rider):
    n = len(ws)

    def body(pl_ref, *refs):
        ins, outs, _, ride = rider.split(refs, 5 * n, 4 * n, 0)
        h, r = pl.program_id(0), pl.program_id(1)

        @pl.when(jnp.logical_and(h == 0, r == 0))
        def _():
            rider.first(ride)

        for k in range(n):
            w_ref, a_ref, b_ref, m_ref, v_ref = ins[5 * k:5 * k + 5]
            g = jnp.where(h == pl_ref[0], a_ref[...], b_ref[...])
            d, mo, vo = _adamw_math(w_ref[...], g, m_ref[...], v_ref[...])
            for o_ref, val in zip(outs[4 * k:4 * k + 4], (g, d, mo, vo)):
                o_ref[...] = val

        @pl.when(jnp.logical_and(h == 1, r == TAIL_STEPS - 1))
        def _():
            rider.last(ride)

    in_specs, out_specs, out_shape, args = [], [], [], []
    for w, a, b, m, v in zip(ws, mines, theirs, ms, vs):
        R, C = w.shape
        th = (R // 2) // TAIL_STEPS
        whole = pl.BlockSpec((th, C), lambda h, r, pr: (h * TAIL_STEPS + r, 0))
        mine_spec = pl.BlockSpec((th, C), lambda h, r, pr: (jnp.where(h == pr[0], r, 0), 0))
        theirs_spec = pl.BlockSpec((th, C), lambda h, r, pr: (jnp.where(h == pr[0], 0, r), 0))
        in_specs += [whole, mine_spec, theirs_spec, whole, whole]
        out_specs += [whole] * 4
        out_shape += [_sds((R, C), F32)] * 4
        args += [w, a, b, m, v]
    res = rider.call(body, args, grid=(2, TAIL_STEPS), name=name, prefetch=place, in_specs=in_specs,
                     out_specs=out_specs, out_shape=out_shape, scratch_shapes=[], vmem_mib=40)
    return [tuple(res[4 * k:4 * k + 4]) for k in range(n)], res[4 * n:]


DEVICE_FLIPS = tuple((fx, fy, fc) for fx in (0, 1) for fy in (0, 1) for fc in (0, 1))[1:]


def _pack_exchange_rider(pack):
    def phases(ins, outs, ssem, rsem):
        x, y, c = _place()
        mine = outs[0].at[4 * x + 2 * y + c]
        copies = [_remote(ins[0], mine, ssem, rsem, k, (x ^ fx, y ^ fy, c ^ fc))
                  for k, (fx, fy, fc) in enumerate(DEVICE_FLIPS)]
        copies.append(functools.partial(pltpu.make_async_copy, ins[0], mine, ssem.at[len(DEVICE_FLIPS)]))
        return [copies]

    return _Rider([pack], [_sds((N_DEV,) + pack.shape, pack.dtype)], len(DEVICE_FLIPS) + 1, phases)


def _small_sum_adamw(recv_a, recv_b, wpack, mpack, vpack):
    R = wpack.shape[0]

    def body(a_ref, b_ref, w_ref, m_ref, v_ref, gs_ref, d_ref, mo_ref, vo_ref):
        ta, tb = a_ref[0], b_ref[0]
        for dev in range(1, N_DEV):
            ta = ta + a_ref[dev]
            tb = tb + b_ref[dev]
        total = jnp.concatenate([ta, tb], axis=0)
        gs_ref[...] = total
        d, mo, vo = _adamw_math(w_ref[...], total, m_ref[...], v_ref[...])
        d_ref[...] = d
        mo_ref[...] = mo
        vo_ref[...] = vo

    return pl.pallas_call(
        body, name="small_sum_adamw", in_specs=[VMEM_SPEC] * 5, out_specs=[VMEM_SPEC] * 4,
        out_shape=[_sds((R, LANES), F32)] * 4,
    )(recv_a, recv_b, wpack, mpack, vpack)


def _rows8(a):
    a = a.reshape(-1, LANES)
    pad = (-a.shape[0]) % 8
    return jnp.pad(a, ((0, pad), (0, 0))) if pad else a


def _pack(parts):
    return jnp.concatenate([_rows8(a) for a in parts], axis=0)


def _unpack(pack, like):
    out, row = [], 0
    for a in like:
        n = a.size // LANES
        out.append(pack[row:row + n].reshape(a.shape))
        row += n + (-n) % 8
    return out


def kernel(x, p, emb_ln_g, emb_ln_b, w_in, attn_out_g, w_pool, pool_scale, w_out, ln1_g, ln1_b, w_up, w_down, ln2_g, ln2_b, w_ple, w_ple_gate, ln3_g, ln3_b, loss_target, m_emb_ln_g, m_emb_ln_b, m_w_in, m_attn_out_g, m_w_pool, m_pool_scale, m_w_out, m_ln1_g, m_ln1_b, m_w_up, m_w_down, m_ln2_g, m_ln2_b, m_w_ple, m_w_ple_gate, m_ln3_g, m_ln3_b, v_emb_ln_g, v_emb_ln_b, v_w_in, v_attn_out_g, v_w_pool, v_pool_scale, v_w_out, v_ln1_g, v_ln1_b, v_w_up, v_w_down, v_ln2_g, v_ln2_b, v_w_ple, v_w_ple_gate, v_ln3_g, v_ln3_b):
    S = x.shape[1]
    tm = min(256, S)
    tq = min(256, S)
    tm_mlp = min(512, S)
    xs = x[0]
    ps = p[0, 0]
    tgt = loss_target[0]
    row = lambda a: a.reshape(1, -1)
    g0, b0 = row(emb_ln_g), row(emb_ln_b)
    g1, b1, g2, b2, g3, b3 = ln1_g, ln1_b, ln2_g, ln2_b, ln3_g, ln3_b
    wp = w_pool[0]

    xi, yi, ci = _place()
    place = jnp.stack([ci, 2 * xi + yi]).astype(jnp.int32)
    names = ["w_in", "w_out", "w_up", "w_down", "w_ple", "w_ple_gate"]

    big = [w_in[0], w_out[0], w_up[0], w_down[0], w_ple[0], w_ple_gate[0]]
    s_in = _cast_into_slot(big[0], place, "cast_w_in")
    s_out, s_up, s_down, s_ple, s_gate, w_in_s = _cast_rest(big[1:], place, _gather_rider([s_in]))

    xh0, rstd0, q, k, v, u, s_out, s_ple, s_gate = _embln_inproj(
        xs, g0, b0, w_in_s, tm, _gather_rider([s_out, s_ple, s_gate], "chips"))
    o_raw, on, s_up, s_down, w_out_s, w_ple_s, w_gate_s = _attn_fwd(
        q, k, v, attn_out_g, tq, _gather_rider([s_up, s_down], "chips") + _gather_rider([s_out, s_ple, s_gate], "pair"))
    w_out_f = w_out_s.reshape(D_MODEL, D_MODEL)
    w_gate_f = w_gate_s.reshape(D_MODEL, D_MODEL)
    d_b, pooled = _pool_fwd(u, wp, pool_scale, tm)
    xh1, rstd1, x1b, w_up_s, w_down_s = _mix_ln1(on, pooled, xh0, g0, b0, w_out_f, g1, b1, tm,
                                                 _gather_rider([s_up, s_down], "pair"))
    w_down_f = w_down_s.reshape(D_FF, D_MODEL)
    xh2, rstd2, rb = _mlp_ln2(xh1, x1b, g1, b1, w_up_s, w_down_f, tm_mlp)

    (dpre2, dhb, dw_ple, dw_gate, dg3, db3, dg2, db2, loss_row) = _ple_ln3_loss(
        xh2, rstd2, g2, b2, ps, w_ple_s, w_gate_f, g3, b3, tgt, tm)
    dx1m, da = _mlp_bwd(rb, dhb, w_up_s, w_down_f, tm_mlp)
    dw_up = _tn_matmul(x1b, da, "grad_w_up", 1024, min(512, S), stacked=True)
    dw_down = _tn_matmul(rb, dhb, "grad_w_down", 1024, min(512, S), stacked=False, square_a=True)
    def halves_of(g):
        return g.reshape(N_CHIPS, 2, g.shape[1] // 2, g.shape[2])

    early_names = names[2:]
    early = [halves_of(g) for g in (dw_up, dw_down.reshape(N_CHIPS, D_FF // N_CHIPS, D_MODEL), dw_ple,
                                    dw_gate.reshape(N_CHIPS, D_MODEL // N_CHIPS, D_MODEL))]
    dpre1, don, dpooled, dw_out, dg1, db1, *early_pair = _mix_bwd(
        dpre2, dx1m, xh1, rstd1, g1, w_out_f, on, pooled, tm, _pair_swap_rider(early))
    early_sum, _ = _add_pair(early, early_pair, place, "pair_sum_mlp_ple", _no_rider())
    out_halves = halves_of(dw_out.reshape(N_CHIPS, D_MODEL // N_CHIPS, D_MODEL))
    du, dwp, dsc, out_pair = _pool_bwd(dpooled, d_b, wp, pool_scale, tm, _pair_swap_rider([out_halves]))
    (out_sum,), _ = _add_pair([out_halves], [out_pair], place, "pair_sum_w_out", _no_rider())
    pack_a = _pack([jnp.broadcast_to(loss_row, (8, LANES)), dwp, dsc, dg1, db1, dg2, db2, dg3, db3])
    early_sum = [out_sum] + early_sum
    riding = _chip_scatter_rider([b for _, b in early_sum]) + _pack_exchange_rider(pack_a)
    dq, dk, dv, dga, *arrived = _attn_bwd(q, k, v, don, o_raw, attn_out_g, tq, riding)
    early_chips, recv_a = arrived[:-1], arrived[-1]
    grad_x, dw_in, dg0, db0 = _inproj_bwd(dq, dk, dv, du, dpre1, xh0, rstd0, g0, b0, w_in_s, tm)

    in_halves = halves_of(dw_in)
    pack_b = _pack([dg0, db0, dga])
    early_mine, (in_pair, recv_b) = _add_chips(
        [s for s, _ in early_sum], early_chips, place, "chip_sum_early",
        _pair_swap_rider([in_halves]) + _pack_exchange_rider(pack_b))
    (in_sum,), early_theirs = _add_pair([in_halves], [in_pair], place, "pair_sum_w_in", _pair_send_rider(early_mine))
    ms = [m_w_in, m_w_out, m_w_up, m_w_down, m_w_ple, m_w_ple_gate]
    vs = [v_w_in, v_w_out, v_w_up, v_w_down, v_w_ple, v_w_ple_gate]
    early_res, (in_chips,) = _adamw(big[1:], early_mine, early_theirs, [m[0] for m in ms[1:]],
                                    [v[0] for v in vs[1:]], place, "adamw_early",
                                    _chip_scatter_rider([in_sum[1]]))
    (in_mine,), _ = _add_chips([in_sum[0]], [in_chips], place, "chip_sum_w_in", _no_rider())
    (in_theirs,) = _pair_send_rider([in_mine]).run("gather_pair_w_in")
    in_res, _ = _adamw(big[:1], [in_mine], [in_theirs], [ms[0][0]], [vs[0][0]], place, "adamw_w_in", _no_rider())
    big_out = {n: tuple(r.reshape(m.shape) for r in res4) for n, res4, m in zip(names, in_res + early_res, ms)}

    small_names = ["w_pool", "pool_scale", "ln1_g", "ln1_b", "ln2_g", "ln2_b", "ln3_g", "ln3_b",
                   "emb_ln_g", "emb_ln_b", "attn_out_g"]
    small_w = [w_pool, pool_scale, ln1_g, ln1_b, ln2_g, ln2_b, ln3_g, ln3_b, emb_ln_g, emb_ln_b, attn_out_g]
    small_m = [m_w_pool, m_pool_scale, m_ln1_g, m_ln1_b, m_ln2_g, m_ln2_b, m_ln3_g, m_ln3_b,
               m_emb_ln_g, m_emb_ln_b, m_attn_out_g]
    small_v = [v_w_pool, v_pool_scale, v_ln1_g, v_ln1_b, v_ln2_g, v_ln2_b, v_ln3_g, v_ln3_b,
               v_emb_ln_g, v_emb_ln_b, v_attn_out_g]
    loss_like = jnp.zeros((8, LANES), F32)
    gs, ds, mos, vos = _small_sum_adamw(recv_a, recv_b, _pack([loss_like] + small_w), _pack([loss_like] + small_m),
                                        _pack([jnp.ones((8, LANES), F32)] + small_v))
    like = [loss_like] + small_w
    gs_u, ds_u, mos_u, vos_u = (_unpack(a, like) for a in (gs, ds, mos, vos))
    loss = gs_u[0][0, 0]
    small_out = {n: (gs_u[i + 1], ds_u[i + 1], mos_u[i + 1], vos_u[i + 1]) for i, n in enumerate(small_names)}

    order = ["emb_ln_g", "emb_ln_b", "w_in", "attn_out_g", "w_pool", "pool_scale", "w_out", "ln1_g", "ln1_b",
             "w_up", "w_down", "ln2_g", "ln2_b", "w_ple", "w_ple_gate", "ln3_g", "ln3_b"]
    res = {**big_out, **small_out}
    outs = [loss, grad_x.reshape(x.shape)]
    for kind in range(4):
        outs += [res[n][kind] for n in order]
    return tuple(outs)
```

```python
import functools

import jax
import jax.numpy as jnp
from jax import lax
from jax.experimental import pallas as pl
from jax.experimental.pallas import tpu as pltpu

F32 = jnp.float32
BF16 = jnp.bfloat16

D_MODEL = 1024
ATTN_WIDTH = 512
POOL_WIDTH = 512
HEAD_DIM = 64
PAIR = 2 * HEAD_DIM
N_PAIRS = ATTN_WIDTH // PAIR
N_POOL_GROUPS = 4
POOL_GROUP = 128
POOL_HALO = 16
D_FF = 4096
PLE_DIM = 256
N_CHIPS = 4
N_DEV = 8
LN_EPS = 1e-5
RMS_EPS = 1e-6
ALPHA = float(2.0 ** 0.25)
Q_SCALE = 0.125
ADAM_LR = 0.001
ADAM_B1 = 0.9
ADAM_B2 = 0.999
ADAM_EPS = 1e-08
ADAM_WD = 0.01
ADAM_STEP = 10
LANES = 128
MIB = 1024 * 1024

MESH = pl.DeviceIdType.MESH
HBM_SPEC = pl.BlockSpec(memory_space=pltpu.HBM)
VMEM_SPEC = pl.BlockSpec(memory_space=pltpu.VMEM)


def _cp(vmem_mib):
    return pltpu.CompilerParams(vmem_limit_bytes=vmem_mib * MIB)


def _dot(a, b):
    return jnp.dot(a, b, preferred_element_type=F32)


def _dot_nt(a, b):
    return lax.dot_general(a, b, (((1,), (1,)), ((), ())), preferred_element_type=F32)


def _dot_tn(a, b):
    return lax.dot_general(a, b, (((0,), (0,)), ((), ())), preferred_element_type=F32)


def _ln_fwd(pre):
    mu = jnp.mean(pre, axis=-1, keepdims=True)
    xc = pre - mu
    var = jnp.mean(xc * xc, axis=-1, keepdims=True)
    rstd = lax.rsqrt(var + LN_EPS)
    return xc * rstd, rstd


def _ln_bwd(dy, xh, rstd, g):
    dxh = dy * g
    m1 = jnp.mean(dxh, axis=-1, keepdims=True)
    m2 = jnp.mean(dxh * xh, axis=-1, keepdims=True)
    return rstd * (dxh - m1 - xh * m2)


def _colsum(a):
    return jnp.sum(a, axis=0, keepdims=True)


def _neg_softplus(z):
    return -(jnp.maximum(z, 0.0) + jnp.log(1.0 + jnp.exp(-jnp.abs(z))))


def _split_bf16(a):
    hi = a.astype(BF16)
    lo = (a - hi.astype(F32)).astype(BF16)
    return hi, lo


def _row_spec(tm, n):
    return pl.BlockSpec((tm, n), lambda i: (i, 0))


def _const_spec(shape):
    nd = len(shape)
    return pl.BlockSpec(shape, lambda *_: (0,) * nd)


def _hbm(*arrays):
    return [pltpu.with_memory_space_constraint(a, pltpu.HBM) for a in arrays]


def _sds(shape, dtype):
    return pltpu.HBM(shape, dtype)


def _embln_inproj(x, g0, b0, w_in_s, tm, rider):
    S, D = x.shape
    n_t = S // tm

    def body(*refs):
        ((x_ref, g_ref, b_ref, w_ref), (xh_ref, rstd_ref, q_ref, k_ref, v_ref, u_ref), _,
         ride) = rider.split(refs, 4, 6, 0)
        i = pl.program_id(0)

        @pl.when(i == 0)
        def _():
            rider.first(ride)

        @pl.when(i == (3 * n_t) // 4)
        def _():
            rider.mid(ride)

        xh, rstd = _ln_fwd(x_ref[...])
        xh_ref[...] = xh
        rstd_ref[...] = rstd
        xb = (xh * g_ref[...] + b_ref[...]).astype(BF16)
        q_ref[...] = (_dot(xb, w_ref[0]) * Q_SCALE).astype(BF16)
        k_ref[...] = _dot(xb, w_ref[1]).astype(BF16)
        v_ref[...] = _dot(xb, w_ref[2]).astype(BF16)
        u_ref[...] = _dot(xb, w_ref[3])

        @pl.when(i == n_t - 1)
        def _():
            rider.last(ride)

    return rider.call(
        body, [x, g0, b0, w_in_s], grid=(n_t,), name="embln_inproj",
        in_specs=[_row_spec(tm, D), _const_spec((1, D)), _const_spec((1, D)),
                  _const_spec((N_CHIPS, D, 512))],
        out_specs=[_row_spec(tm, D), _row_spec(tm, 1), _row_spec(tm, 512), _row_spec(tm, 512),
                   _row_spec(tm, 512), _row_spec(tm, 512)],
        out_shape=[_sds((S, D), F32), _sds((S, 1), F32), _sds((S, 512), BF16), _sds((S, 512), BF16),
                   _sds((S, 512), BF16), _sds((S, 512), F32)],
        scratch_shapes=[], vmem_mib=40)


def _tri(n, upper):
    r = lax.broadcasted_iota(jnp.int32, (n, n), 0)
    c = lax.broadcasted_iota(jnp.int32, (n, n), 1)
    keep = (r < c) if upper else (r > c)
    return jnp.where(keep, 1.0, 0.0).astype(BF16)


def _strictly_causal(n):
    return lax.broadcasted_iota(jnp.int32, (n, n), 1) < lax.broadcasted_iota(jnp.int32, (n, n), 0)


LOG_WEIGHT_FLOOR = -110.0


def _sb_tile(qhs, kt, low, c_ls, valid):
    valids = valid if isinstance(valid, (list, tuple)) else [valid] * len(qhs)
    zs = [_dot_nt(qh, kt) for qh in qhs]
    lrs = [_neg_softplus(z) for z in zs]
    ls_ = [lr if m is None else jnp.where(m, lr, 0.0) for lr, m in zip(lrs, valids)]
    sfx = [_dot(l.astype(BF16), low) + c_l for l, c_l in zip(ls_, c_ls)]
    lss = [z + lr for z, lr in zip(zs, lrs)]
    ws = [jnp.exp(ls + s) for ls, s in zip(lss, sfx)]
    ws = [w if m is None else jnp.where(m, w, 0.0) for w, m in zip(ws, valids)]
    return lss, ls_, ws


def _attn_fwd(q, k, v, ga, tq, rider):
    S = q.shape[0]
    nq = S // tq

    def body(*refs):
        (q_ref, k_ref, v_ref, ga_ref), (o_ref, on_ref), _, ride = rider.split(refs, 4, 2, 0)
        p, i = pl.program_id(0), pl.program_id(1)

        @pl.when(jnp.logical_and(p == 0, i == 0))
        def _():
            rider.first(ride)

        @pl.when(jnp.logical_and(p == N_PAIRS - 1, i == 0))
        def _():
            rider.mid(ride)

        lane = lax.broadcasted_iota(jnp.int32, (1, PAIR), 1)
        m0 = lane < HEAD_DIM
        low = _tri(tq, upper=False)
        q2 = q_ref[...]
        qhs = [jnp.where(m0, q2, jnp.zeros_like(q2)), jnp.where(m0, jnp.zeros_like(q2), q2)]

        def tile(kb, c_ls, accs, valid):
            ks = pl.multiple_of(kb * tq, tq)
            kt = k_ref[pl.ds(ks, tq), :]
            vt = v_ref[pl.ds(ks, tq), :]
            _, ls_, ws = _sb_tile(qhs, kt, low, c_ls, valid)
            new_a = [acc + _dot(w.astype(BF16), vt) for acc, w in zip(accs, ws)]
            new_c = [c_l + jnp.sum(l, axis=1, keepdims=True) for c_l, l in zip(c_ls, ls_)]
            return new_c, new_a

        zc, za = jnp.zeros((tq, 1), F32), jnp.zeros((tq, PAIR), F32)

        def first_two():
            c_ls, accs = tile(i, [zc, zc], [za, za], _strictly_causal(tq))
            c_ls, accs = tile(i - 1, c_ls, accs, None)
            return (*c_ls, *accs)

        def first_one():
            c_ls, accs = tile(i, [zc, zc], [za, za], _strictly_causal(tq))
            return (*c_ls, *accs)

        st0 = lax.cond(i >= 1, first_two, first_one)

        def more(st):
            return jnp.logical_and(st[0] <= i, jnp.max(jnp.maximum(st[1], st[2])) > LOG_WEIGHT_FLOOR)

        def step(st):
            n, c0, c1, a0, a1 = st
            c_ls, accs = tile(i - n, [c0, c1], [a0, a1], None)
            return (n + 1, c_ls[0], c_ls[1], accs[0], accs[1])

        st = lax.while_loop(more, step, (jnp.int32(2), *st0))
        o = jnp.where(m0, st[3], st[4])
        o_ref[...] = o
        sq = o * o
        ms0 = jnp.sum(jnp.where(m0, sq, 0.0), axis=-1, keepdims=True) * (1.0 / HEAD_DIM)
        ms1 = jnp.sum(jnp.where(m0, 0.0, sq), axis=-1, keepdims=True) * (1.0 / HEAD_DIM)
        rs = jnp.where(m0, lax.rsqrt(ms0 + RMS_EPS), lax.rsqrt(ms1 + RMS_EPS))
        on_ref[...] = (o * rs * ga_ref[...]).astype(BF16)

        @pl.when(jnp.logical_and(p == N_PAIRS - 1, i == nq - 1))
        def _():
            rider.last(ride)

    return rider.call(
        body, [q, k, v, ga], grid=(N_PAIRS, nq), name="attn_fwd",
        in_specs=[pl.BlockSpec((tq, PAIR), lambda p, i: (i, p)),
                  pl.BlockSpec((S, PAIR), lambda p, i: (0, p)),
                  pl.BlockSpec((S, PAIR), lambda p, i: (0, p)),
                  pl.BlockSpec((1, PAIR), lambda p, i: (0, p))],
        out_specs=[pl.BlockSpec((tq, PAIR), lambda p, i: (i, p)),
                   pl.BlockSpec((tq, PAIR), lambda p, i: (i, p))],
        out_shape=[_sds((S, ATTN_WIDTH), F32), _sds((S, ATTN_WIDTH), BF16)],
        scratch_shapes=[], vmem_mib=40)


def _pool_fwd(u, w_pool, pscale, tm):
    S = u.shape[0]
    hb = tm // POOL_HALO

    def body(u_ref, uh_ref, wp_ref, sc_ref, d_ref, pooled_ref):
        i = pl.program_id(0)
        halo = jnp.where(i > 0, uh_ref[...], 0.0)
        pos = i * tm + lax.broadcasted_iota(jnp.int32, (tm, 1), 0)
        for g in range(N_POOL_GROUPS):
            win = 2 ** (g + 1)
            cols = slice(g * POOL_GROUP, (g + 1) * POOL_GROUP)
            ut = u_ref[:, cols]
            s = jnp.concatenate([halo[:, cols], ut], axis=0)
            for sh in (1, 2, 4, 8)[:g + 1]:
                s = s + pltpu.roll(s, sh, 0)
            cnt = jnp.minimum(pos + 1, win).astype(F32)
            db = (s[POOL_HALO:, :] / cnt - ut).astype(BF16)
            y = _dot(db, wp_ref[g].astype(BF16))
            d_ref[:, cols] = db
            pooled_ref[:, cols] = (y * sc_ref[:, cols]).astype(BF16)

    return pl.pallas_call(
        body, grid=(S // tm,), name="pool_fwd",
        in_specs=[_row_spec(tm, POOL_WIDTH),
                  pl.BlockSpec((POOL_HALO, POOL_WIDTH), lambda i: (jnp.maximum(i * hb - 1, 0), 0)),
                  _const_spec((N_POOL_GROUPS, POOL_GROUP, POOL_GROUP)), _const_spec((1, POOL_WIDTH))],
        out_specs=[_row_spec(tm, POOL_WIDTH), _row_spec(tm, POOL_WIDTH)],
        out_shape=[_sds((S, POOL_WIDTH), BF16), _sds((S, POOL_WIDTH), BF16)],
        compiler_params=_cp(32),
    )(*_hbm(u, u, w_pool, pscale))


def _mix_ln1(on, pooled, xh0, g0, b0, w_out, g1, b1, tm, rider):
    S, D = xh0.shape
    n_t = S // tm

    def body(*refs):
        ((on_ref, po_ref, xh0_ref, g0_ref, b0_ref, w_ref, g1_ref, b1_ref), (xh_ref, rstd_ref, xb_ref), _,
         ride) = rider.split(refs, 8, 3, 0)

        @pl.when(pl.program_id(0) == 0)
        def _():
            rider.first(ride)

        mixed = _dot(on_ref[...], w_ref[:ATTN_WIDTH, :]) + _dot(po_ref[...], w_ref[ATTN_WIDTH:, :])
        x0 = xh0_ref[...] * g0_ref[...] + b0_ref[...]
        xh, rstd = _ln_fwd(ALPHA * x0 + mixed)
        xh_ref[...] = xh
        rstd_ref[...] = rstd
        xb_ref[...] = (xh * g1_ref[...] + b1_ref[...]).astype(BF16)

        @pl.when(pl.program_id(0) == n_t - 1)
        def _():
            rider.last(ride)

    return rider.call(
        body, [on, pooled, xh0, g0, b0, w_out, g1, b1], grid=(n_t,), name="mix_ln1",
        in_specs=[_row_spec(tm, ATTN_WIDTH), _row_spec(tm, POOL_WIDTH), _row_spec(tm, D),
                  _const_spec((1, D)), _const_spec((1, D)), _const_spec((D, D)),
                  _const_spec((1, D)), _const_spec((1, D))],
        out_specs=[_row_spec(tm, D), _row_spec(tm, 1), _row_spec(tm, D)],
        out_shape=[_sds((S, D), F32), _sds((S, 1), F32), _sds((S, D), BF16)],
        scratch_shapes=[], vmem_mib=40)


def _mlp_ln2(xh1, x1b, g1, b1, w_up_s, w_down, tm):
    S, D = xh1.shape
    fc = D_FF // N_CHIPS

    def body(xh_ref, xb_ref, g_ref, b_ref, wu_ref, wd_ref, xh2_ref, rstd_ref, r_ref, acc_ref):
        j = pl.program_id(1)

        @pl.when(j == 0)
        def _():
            acc_ref[...] = jnp.zeros_like(acc_ref)

        r = jnp.maximum(_dot(xb_ref[...], wu_ref[0]), 0.0)
        r_ref[...] = r.astype(BF16)
        acc_ref[...] += _dot((r * r).astype(BF16), wd_ref[...])

        @pl.when(j == N_CHIPS - 1)
        def _():
            x1 = xh_ref[...] * g_ref[...] + b_ref[...]
            xh, rstd = _ln_fwd(ALPHA * x1 + acc_ref[...])
            xh2_ref[...] = xh
            rstd_ref[...] = rstd

    return pl.pallas_call(
        body, grid=(S // tm, N_CHIPS), name="mlp_ln2",
        in_specs=[pl.BlockSpec((tm, D), lambda i, j: (i, 0)), pl.BlockSpec((tm, D), lambda i, j: (i, 0)),
                  pl.BlockSpec((1, D), lambda i, j: (0, 0)), pl.BlockSpec((1, D), lambda i, j: (0, 0)),
                  pl.BlockSpec((1, D, fc), lambda i, j: (j, 0, 0)),
                  pl.BlockSpec((fc, D), lambda i, j: (j, 0))],
        out_specs=[pl.BlockSpec((tm, D), lambda i, j: (i, 0)), pl.BlockSpec((tm, 1), lambda i, j: (i, 0)),
                   pl.BlockSpec((tm, fc), lambda i, j: (i, j))],
        out_shape=[_sds((S, D), F32), _sds((S, 1), F32), _sds((S, D_FF), BF16)],
        scratch_shapes=[pltpu.VMEM((tm, D), F32)],
        compiler_params=_cp(40),
    )(*_hbm(xh1, x1b, g1, b1, w_up_s, w_down))


def _ple_ln3_loss(xh2, rstd2, g2, b2, p, w_ple_s, w_gate, g3, b3, target, tm):
    S, D = xh2.shape
    pc = D // N_CHIPS

    def body(xh2_ref, rstd2_ref, g2_ref, b2_ref, p_ref, wp_ref, wg_ref, g3_ref, b3_ref, t_ref,
             dpre2_ref, dhb_ref, dwp_ref, dwg_ref, dg3_ref, db3_ref, dg2_ref, db2_ref, loss_ref):
        i = pl.program_id(0)

        @pl.when(i == 0)
        def _():
            for r in (dwp_ref, dwg_ref, dg3_ref, db3_ref, dg2_ref, db2_ref, loss_ref):
                r[...] = jnp.zeros_like(r)

        xh2 = xh2_ref[...]
        x2 = xh2 * g2_ref[...] + b2_ref[...]
        x2b = x2.astype(BF16)
        gate = 1.0 / (1.0 + jnp.exp(-_dot(x2b, wg_ref[...])))
        pb = p_ref[...].astype(BF16)
        pe = jnp.concatenate([_dot(pb, wp_ref[c]) for c in range(N_CHIPS)], axis=1)
        xh3, rstd3 = _ln_fwd(ALPHA * x2 + pe * gate)
        diff = xh3 * g3_ref[...] + b3_ref[...] - t_ref[...]
        loss_ref[...] += (0.5 / D) * jnp.sum(diff * diff)
        dy = diff * (1.0 / D)
        dg3_ref[...] += _colsum(dy * xh3)
        db3_ref[...] += _colsum(dy)
        dpre3 = _ln_bwd(dy, xh3, rstd3, g3_ref[...])
        dpe_b = (dpre3 * gate).astype(BF16)
        dgp_b = (dpre3 * pe * gate * (1.0 - gate)).astype(BF16)
        dx2 = ALPHA * dpre3 + _dot_nt(dgp_b, wg_ref[...])
        dwg_ref[...] += _dot_tn(x2b, dgp_b)
        for c in range(N_CHIPS):
            dwp_ref[c] += _dot_tn(pb, dpe_b[:, c * pc:(c + 1) * pc])
        dg2_ref[...] += _colsum(dx2 * xh2)
        db2_ref[...] += _colsum(dx2)
        dpre2 = _ln_bwd(dx2, xh2, rstd2_ref[...], g2_ref[...])
        dpre2_ref[...] = dpre2
        dhb_ref[...] = dpre2.astype(BF16)

    vec = _const_spec((1, D))
    return pl.pallas_call(
        body, grid=(S // tm,), name="ple_ln3_loss",
        in_specs=[_row_spec(tm, D), _row_spec(tm, 1), vec, vec, _row_spec(tm, PLE_DIM),
                  _const_spec((N_CHIPS, PLE_DIM, pc)), _const_spec((D, D)), vec, vec, _row_spec(tm, D)],
        out_specs=[_row_spec(tm, D), _row_spec(tm, D), _const_spec((N_CHIPS, PLE_DIM, pc)),
                   _const_spec((D, D)), vec, vec, vec, vec, _const_spec((1, LANES))],
        out_shape=[_sds((S, D), F32), _sds((S, D), BF16), _sds((N_CHIPS, PLE_DIM, pc), F32),
                   _sds((D, D), F32), _sds((1, D), F32), _sds((1, D), F32), _sds((1, D), F32),
                   _sds((1, D), F32), _sds((1, LANES), F32)],
        compiler_params=_cp(48),
    )(*_hbm(xh2, rstd2, g2, b2, p, w_ple_s, w_gate, g3, b3, target))


def _mlp_bwd(rb, dhb, w_up_s, w_down, tm):
    S, D = dhb.shape
    fc = D_FF // N_CHIPS

    def body(r_ref, dh_ref, wu_ref, wd_ref, dx_ref, da_ref):
        @pl.when(pl.program_id(1) == 0)
        def _():
            dx_ref[...] = jnp.zeros_like(dx_ref)

        da = (_dot_nt(dh_ref[...], wd_ref[...]) * (2.0 * r_ref[...].astype(F32))).astype(BF16)
        da_ref[...] = da
        dx_ref[...] += _dot_nt(da, wu_ref[0])

    return pl.pallas_call(
        body, grid=(S // tm, N_CHIPS), name="mlp_bwd",
        in_specs=[pl.BlockSpec((tm, fc), lambda i, j: (i, j)), pl.BlockSpec((tm, D), lambda i, j: (i, 0)),
                  pl.BlockSpec((1, D, fc), lambda i, j: (j, 0, 0)),
                  pl.BlockSpec((fc, D), lambda i, j: (j, 0))],
        out_specs=[pl.BlockSpec((tm, D), lambda i, j: (i, 0)), pl.BlockSpec((tm, fc), lambda i, j: (i, j))],
        out_shape=[_sds((S, D), F32), _sds((S, D_FF), BF16)],
        compiler_params=_cp(40),
    )(*_hbm(rb, dhb, w_up_s, w_down))


def _tn_matmul(a, b, name, tk, tt, stacked, square_a=False):
    T, K = a.shape
    N = b.shape[1]
    tn = 1024

    def body(a_ref, b_ref, o_ref):
        @pl.when(pl.program_id(2) == 0)
        def _():
            o_ref[...] = jnp.zeros_like(o_ref)

        a_t = a_ref[...]
        if square_a:
            a_t = a_t * a_t
        prod = _dot_tn(a_t, b_ref[...])
        if stacked:
            o_ref[0] += prod
        else:
            o_ref[...] += prod

    if stacked:
        out_spec = pl.BlockSpec((1, tk, tn), lambda k, n, t: (n, k, 0))
        out_shape = _sds((N // tn, K, tn), F32)
    else:
        out_spec = pl.BlockSpec((tk, tn), lambda k, n, t: (k, n))
        out_shape = _sds((K, N), F32)
    return pl.pallas_call(
        body, grid=(K // tk, N // tn, T // tt), name=name,
        in_specs=[pl.BlockSpec((tt, tk), lambda k, n, t: (t, k)),
                  pl.BlockSpec((tt, tn), lambda k, n, t: (t, n))],
        out_specs=out_spec, out_shape=out_shape,
        compiler_params=_cp(40),
    )(*_hbm(a, b))


def _mix_bwd(dpre2, dx1m, xh1, rstd1, g1, w_out, on, pooled, tm, rider):
    S, D = xh1.shape
    n_t = S // tm

    def body(*refs):
        ((dp2_ref, dxm_ref, xh_ref, rstd_ref, g_ref, w_ref, on_ref, po_ref),
         (dpre1_ref, don_ref, dpo_ref, dw_ref, dg_ref, db_ref), _, ride) = rider.split(refs, 8, 6, 0)

        @pl.when(pl.program_id(0) == 0)
        def _():
            rider.first(ride)
            for r in (dw_ref, dg_ref, db_ref):
                r[...] = jnp.zeros_like(r)

        xh = xh_ref[...]
        dx1 = ALPHA * dp2_ref[...] + dxm_ref[...]
        dg_ref[...] += _colsum(dx1 * xh)
        db_ref[...] += _colsum(dx1)
        dpre1 = _ln_bwd(dx1, xh, rstd_ref[...], g_ref[...])
        dpre1_ref[...] = dpre1
        dmb = dpre1.astype(BF16)
        dcat = _dot_nt(dmb, w_ref[...])
        don_ref[...] = dcat[:, :ATTN_WIDTH]
        dpo_ref[...] = dcat[:, ATTN_WIDTH:]
        dw_ref[:ATTN_WIDTH, :] += _dot_tn(on_ref[...], dmb)
        dw_ref[ATTN_WIDTH:, :] += _dot_tn(po_ref[...], dmb)

        @pl.when(pl.program_id(0) == n_t - 1)
        def _():
            rider.last(ride)

    vec = _const_spec((1, D))
    return rider.call(
        body, [dpre2, dx1m, xh1, rstd1, g1, w_out, on, pooled], grid=(n_t,), name="mix_bwd",
        in_specs=[_row_spec(tm, D), _row_spec(tm, D), _row_spec(tm, D), _row_spec(tm, 1), vec,
                  _const_spec((D, D)), _row_spec(tm, ATTN_WIDTH), _row_spec(tm, POOL_WIDTH)],
        out_specs=[_row_spec(tm, D), _row_spec(tm, ATTN_WIDTH), _row_spec(tm, POOL_WIDTH),
                   _const_spec((D, D)), vec, vec],
        out_shape=[_sds((S, D), F32), _sds((S, ATTN_WIDTH), F32), _sds((S, POOL_WIDTH), F32),
                   _sds((D, D), F32), _sds((1, D), F32), _sds((1, D), F32)],
        scratch_shapes=[], vmem_mib=48)


def _pool_bwd(dpooled, d_b, w_pool, pscale, tm, rider):
    S = dpooled.shape[0]
    hb = tm // POOL_HALO
    n_t = S // tm
    te = tm + POOL_HALO

    def body(*refs):
        ((dp_ref, dph_ref, d_ref, wp_ref, sc_ref), (du_ref, dwp_ref, dsc_ref), _,
         ride) = rider.split(refs, 5, 3, 0)
        i = pl.program_id(0)

        @pl.when(i == 0)
        def _():
            rider.first(ride)
            dwp_ref[...] = jnp.zeros_like(dwp_ref)
            dsc_ref[...] = jnp.zeros_like(dsc_ref)

        halo = jnp.where(i < n_t - 1, dph_ref[...], 0.0)
        pos = i * tm + lax.broadcasted_iota(jnp.int32, (te, 1), 0)
        for g in range(N_POOL_GROUPS):
            win = 2 ** (g + 1)
            cols = slice(g * POOL_GROUP, (g + 1) * POOL_GROUP)
            wpb = wp_ref[g].astype(BF16)
            dpt = dp_ref[:, cols]
            dpe = jnp.concatenate([dpt, halo[:, cols]], axis=0)
            dyb = (dpe * sc_ref[:, cols]).astype(BF16)
            dd = _dot_nt(dyb, wpb)
            s = dd / jnp.minimum(pos + 1, win).astype(F32)
            for sh in (1, 2, 4, 8)[:g + 1]:
                s = s + pltpu.roll(s, te - sh, 0)
            du_ref[:, cols] = s[:tm, :] - dd[:tm, :]
            db = d_ref[:, cols]
            dwp_ref[g] += _dot_tn(db, dyb[:tm, :])
            dsc_ref[:, cols] += _colsum(dpt * _dot(db, wpb))

        @pl.when(i == n_t - 1)
        def _():
            rider.last(ride)

    return rider.call(
        body, [dpooled, dpooled, d_b, w_pool, pscale], grid=(n_t,), name="pool_bwd",
        in_specs=[_row_spec(tm, POOL_WIDTH),
                  pl.BlockSpec((POOL_HALO, POOL_WIDTH),
                               lambda i: (jnp.minimum((i + 1) * hb, S // POOL_HALO - 1), 0)),
                  _row_spec(tm, POOL_WIDTH),
                  _const_spec((N_POOL_GROUPS, POOL_GROUP, POOL_GROUP)), _const_spec((1, POOL_WIDTH))],
        out_specs=[_row_spec(tm, POOL_WIDTH), _const_spec((N_POOL_GROUPS, POOL_GROUP, POOL_GROUP)),
                   _const_spec((1, POOL_WIDTH))],
        out_shape=[_sds((S, POOL_WIDTH), F32), _sds((N_POOL_GROUPS, POOL_GROUP, POOL_GROUP), F32),
                   _sds((1, POOL_WIDTH), F32)],
        scratch_shapes=[], vmem_mib=32)


def _attn_bwd(q, k, v, don, o_raw, ga, tq, rider):
    S = q.shape[0]
    nq = S // tq

    def body(*refs):
        ((q_ref, k_ref, v_ref, don_ref, o_ref, ga_ref), (dq_ref, dk_ref, dv_ref, dga_ref),
         (g_s, b_s), ride) = rider.split(refs, 6, 4, 2)
        p, i = pl.program_id(0), pl.program_id(1)

        @pl.when(jnp.logical_and(p == 0, i == 0))
        def _():
            rider.first(ride)

        @pl.when(i == 0)
        def _():
            for r in (dk_ref, dv_ref, dga_ref):
                r[...] = jnp.zeros_like(r)

        lane = lax.broadcasted_iota(jnp.int32, (1, PAIR), 1)
        m0 = lane < HEAD_DIM
        low = _tri(tq, upper=False)
        upp = _tri(tq, upper=True)

        def seg_mean(a):
            s0 = jnp.sum(jnp.where(m0, a, 0.0), axis=-1, keepdims=True)
            s1 = jnp.sum(jnp.where(m0, 0.0, a), axis=-1, keepdims=True)
            return jnp.where(m0, s0, s1) * (1.0 / HEAD_DIM)

        o = o_ref[...]
        rs = lax.rsqrt(seg_mean(o * o) + RMS_EPS)
        oh = o * rs
        don = don_ref[...]
        dga_ref[...] += _colsum(don * oh)
        doh = don * ga_ref[...]
        do = rs * (doh - oh * seg_mean(doh * oh))
        dob = do.astype(BF16)
        q2 = q_ref[...]
        qhs = [jnp.where(m0, q2, jnp.zeros_like(q2)), jnp.where(m0, jnp.zeros_like(q2), q2)]
        dhs = [jnp.where(m0, dob, jnp.zeros_like(dob)), jnp.where(m0, jnp.zeros_like(dob), dob)]
        causal = _strictly_causal(tq)

        def down(kb, c_ls, valid):
            ks = pl.multiple_of(kb * tq, tq)
            kt = k_ref[pl.ds(ks, tq), :]
            vt = v_ref[pl.ds(ks, tq), :]
            lss, ls_, ws = _sb_tile(qhs, kt, low, c_ls, valid)
            dws = [_dot_nt(dh, vt) for dh in dhs]
            for hh in range(2):
                g_s[hh, kb] = dws[hh] * ws[hh]
                b_s[hh, kb] = jnp.exp(lss[hh])
            dv_ref[pl.ds(ks, tq), :] += (_dot_tn(ws[0].astype(BF16), dhs[0])
                                         + _dot_tn(ws[1].astype(BF16), dhs[1]))
            return [c_l + jnp.sum(l, axis=1, keepdims=True) for c_l, l in zip(c_ls, ls_)]

        zc, za = jnp.zeros((tq, 1), F32), jnp.zeros((tq, PAIR), F32)
        c_ls = lax.cond(i >= 1, lambda: tuple(down(i - 1, down(i, [zc, zc], causal), None)),
                        lambda: tuple(down(i, [zc, zc], causal)))

        def more(st):
            return jnp.logical_and(st[0] <= i, jnp.max(jnp.maximum(st[1], st[2])) > LOG_WEIGHT_FLOOR)

        def down_step(st):
            c_ls = down(i - st[0], [st[1], st[2]], None)
            return (st[0] + 1, c_ls[0], c_ls[1])

        n_tiles = lax.while_loop(more, down_step, (jnp.int32(2), c_ls[0], c_ls[1]))[0]

        def up(kb, c_gs, accs, valid):
            ks = pl.multiple_of(kb * tq, tq)
            kt = k_ref[pl.ds(ks, tq), :]
            gs = [g_s[hh, kb] for hh in range(2)]
            pres = [_dot(g.astype(BF16), upp) + c_g for g, c_g in zip(gs, c_gs)]
            dzs = []
            for hh in range(2):
                beta = b_s[hh, kb]
                dz = gs[hh] * (1.0 - beta) - beta * pres[hh]
                if valid is not None:
                    dz = jnp.where(valid, dz, 0.0)
                dzs.append(dz.astype(BF16))
            new_a = [acc + _dot(dzb, kt) for acc, dzb in zip(accs, dzs)]
            dk_ref[pl.ds(ks, tq), :] += _dot_tn(dzs[0], qhs[0]) + _dot_tn(dzs[1], qhs[1])
            new_c = [c_g + jnp.sum(g, axis=1, keepdims=True) for c_g, g in zip(c_gs, gs)]
            return new_c, new_a

        def up_step(kb, st):
            c_gs, accs = up(kb, [st[0], st[1]], [st[2], st[3]], None)
            return (c_gs[0], c_gs[1], accs[0], accs[1])

        st = lax.fori_loop(i - n_tiles + 1, i - 1, up_step, (zc, zc, za, za))

        def last_two():
            c_gs, accs = up(i - 1, [st[0], st[1]], [st[2], st[3]], None)
            return tuple(up(i, c_gs, accs, causal)[1])

        accs = lax.cond(i >= 1, last_two, lambda: tuple(up(i, [zc, zc], [za, za], causal)[1]))
        dq_ref[...] = jnp.where(m0, accs[0], accs[1]) * Q_SCALE

        @pl.when(jnp.logical_and(p == N_PAIRS - 1, i == nq - 1))
        def _():
            rider.last(ride)

    return rider.call(
        body, [q, k, v, don, o_raw, ga], grid=(N_PAIRS, nq), name="attn_bwd",
        in_specs=[pl.BlockSpec((tq, PAIR), lambda p, i: (i, p)),
                  pl.BlockSpec((S, PAIR), lambda p, i: (0, p)),
                  pl.BlockSpec((S, PAIR), lambda p, i: (0, p)),
                  pl.BlockSpec((tq, PAIR), lambda p, i: (i, p)),
                  pl.BlockSpec((tq, PAIR), lambda p, i: (i, p)),
                  pl.BlockSpec((1, PAIR), lambda p, i: (0, p))],
        out_specs=[pl.BlockSpec((tq, PAIR), lambda p, i: (i, p)),
                   pl.BlockSpec((S, PAIR), lambda p, i: (0, p)),
                   pl.BlockSpec((S, PAIR), lambda p, i: (0, p)),
                   pl.BlockSpec((1, PAIR), lambda p, i: (0, p))],
        out_shape=[_sds((S, ATTN_WIDTH), F32), _sds((S, ATTN_WIDTH), F32), _sds((S, ATTN_WIDTH), F32),
                   _sds((1, ATTN_WIDTH), F32)],
        scratch_shapes=[pltpu.VMEM((2, nq, tq, tq), F32), pltpu.VMEM((2, nq, tq, tq), F32)],
        vmem_mib=56)


def _inproj_bwd(dq, dk, dv, du, dpre1, xh0, rstd0, g0, b0, w_in_s, tm):
    S, D = xh0.shape

    def body(dq_ref, dk_ref, dv_ref, du_ref, dp1_ref, xh_ref, rstd_ref, g_ref, b_ref, w_ref,
             gx_ref, dw_ref, dg_ref, db_ref):
        @pl.when(pl.program_id(0) == 0)
        def _():
            for r in (dw_ref, dg_ref, db_ref):
                r[...] = jnp.zeros_like(r)

        xh = xh_ref[...]
        xb = (xh * g_ref[...] + b_ref[...]).astype(BF16)
        dx0 = ALPHA * dp1_ref[...]
        for c, r in enumerate((dq_ref, dk_ref, dv_ref, du_ref)):
            dpb = r[...].astype(BF16)
            dx0 = dx0 + _dot_nt(dpb, w_ref[c])
            dw_ref[c] += _dot_tn(xb, dpb)
        dg_ref[...] += _colsum(dx0 * xh)
        db_ref[...] += _colsum(dx0)
        gx_ref[...] = _ln_bwd(dx0, xh, rstd_ref[...], g_ref[...])

    vec = _const_spec((1, D))
    half = _row_spec(tm, 512)
    return pl.pallas_call(
        body, grid=(S // tm,), name="inproj_bwd",
        in_specs=[half, half, half, half, _row_spec(tm, D), _row_spec(tm, D), _row_spec(tm, 1), vec, vec,
                  _const_spec((N_CHIPS, D, 512))],
        out_specs=[_row_spec(tm, D), _const_spec((N_CHIPS, D, 512)), vec, vec],
        out_shape=[_sds((S, D), F32), _sds((N_CHIPS, D, 512), F32), _sds((1, D), F32), _sds((1, D), F32)],
        compiler_params=_cp(56),
    )(*_hbm(dq, dk, dv, du, dpre1, xh0, rstd0, g0, b0, w_in_s))


def _place():
    return lax.axis_index("x"), lax.axis_index("y"), lax.axis_index("c")


CHIP_FLIPS = ((0, 1), (1, 0), (1, 1))


class _Rider:
    def __init__(self, ins, out_shapes, n_sem, phases, aliases=None):
        self.ins, self.out_shapes, self.n_sem, self.phases = list(ins), list(out_shapes), n_sem, phases
        self.aliases = aliases or {}

    def __add__(self, other):
        na, ma = len(self.ins), len(self.out_shapes)

        def phases(ins, outs, ssem, rsem):
            mine = self.phases(ins[:na], outs[:ma], ssem, rsem)
            rest = pl.ds(self.n_sem, other.n_sem)
            theirs = other.phases(ins[na:], outs[ma:], ssem.at[rest], rsem.at[rest])
            assert len(mine) == 1 and len(theirs) == 1
            return [mine[0] + theirs[0]]

        aliases = {**self.aliases, **{na + i: ma + o for i, o in other.aliases.items()}}
        return _Rider(self.ins + other.ins, self.out_shapes + other.out_shapes, self.n_sem + other.n_sem, phases,
                      aliases)

    def split(self, refs, n_in, n_out, n_scratch):
        a = n_in + len(self.ins)
        b = a + n_out
        c = b + len(self.out_shapes)
        own = (refs[:n_in], refs[a:b], refs[c:c + n_scratch])
        return own + ((refs[n_in:a], refs[b:c]) + tuple(refs[c + n_scratch:]),)

    def first(self, ride):
        for make in self.phases(*ride)[0]:
            make().start()

    def mid(self, ride):
        ph = self.phases(*ride)
        if len(ph) == 2:
            for make in ph[0]:
                make().wait_recv()
            for make in ph[1]:
                make().start()

    def last(self, ride):
        ph = self.phases(*ride)
        if len(ph) == 2:
            for make in ph[0]:
                make().wait_send()
        for make in ph[-1]:
            make().wait()

    def call(self, body, args, *, grid, name, in_specs, out_specs, out_shape, scratch_shapes, vmem_mib,
             prefetch=None):
        n_in, n_out = len(in_specs), len(out_specs)
        sems = [pltpu.SemaphoreType.DMA((self.n_sem,)), pltpu.SemaphoreType.DMA((self.n_sem,))]
        n_pre = 0 if prefetch is None else 1
        grid_spec = pltpu.PrefetchScalarGridSpec(
            num_scalar_prefetch=n_pre, grid=grid,
            in_specs=list(in_specs) + [HBM_SPEC] * len(self.ins),
            out_specs=list(out_specs) + [HBM_SPEC] * len(self.out_shapes),
            scratch_shapes=list(scratch_shapes) + sems)
        return pl.pallas_call(
            body, name=name, grid_spec=grid_spec,
            out_shape=list(out_shape) + self.out_shapes,
            input_output_aliases={n_pre + n_in + i: n_out + o for i, o in self.aliases.items()},
            compiler_params=_cp(vmem_mib),
        )(*([] if prefetch is None else [prefetch]), *_hbm(*args), *self.ins)

    def run(self, name):
        def body(*refs):
            ride = self.split(refs, 0, 0, 0)[3]
            self.first(ride)
            self.mid(ride)
            self.last(ride)

        return self.call(body, [], grid=(), name=name, in_specs=[], out_specs=[], out_shape=[],
                         scratch_shapes=[], vmem_mib=16)


def _remote(src, dst, ssem, rsem, n, dev):
    return functools.partial(pltpu.make_async_remote_copy, src_ref=src, dst_ref=dst, send_sem=ssem.at[n],
                             recv_sem=rsem.at[n], device_id=dev, device_id_type=MESH)


def _cast_into_slot(w, place, name):
    R, C = w.shape
    tr = min(R, 512)

    def body(pl_ref, w_ref, o_ref):
        o_ref[0] = w_ref[...].astype(BF16)

    return pl.pallas_call(
        body, name=name,
        grid_spec=pltpu.PrefetchScalarGridSpec(
            num_scalar_prefetch=1, grid=(R // tr,),
            in_specs=[pl.BlockSpec((tr, C), lambda r, pr: (r, 0))],
            out_specs=pl.BlockSpec((1, tr, C), lambda r, pr: (pr[1], r, 0))),
        out_shape=_sds((N_CHIPS, R, C), BF16),
    )(place, w)


CAST_STEPS = 8


def _cast_rest(ws, place, rider):
    n = len(ws)

    def body(pl_ref, *refs):
        w_refs, o_refs, _, ride = rider.split(refs, n, n, 0)
        r = pl.program_id(0)

        @pl.when(r == 0)
        def _():
            rider.first(ride)

        @pl.when(r == CAST_STEPS // 2)
        def _():
            rider.mid(ride)

        for w_ref, o_ref in zip(w_refs, o_refs):
            o_ref[0] = w_ref[...].astype(BF16)

        @pl.when(r == CAST_STEPS - 1)
        def _():
            rider.last(ride)

    def rows(w):
        return w.shape[0] // CAST_STEPS

    return rider.call(
        body, ws, grid=(CAST_STEPS,), name="cast_weights", prefetch=place,
        in_specs=[pl.BlockSpec((rows(w), w.shape[1]), lambda r, pr: (r, 0)) for w in ws],
        out_specs=[pl.BlockSpec((1, rows(w), w.shape[1]), lambda r, pr: (pr[1], r, 0)) for w in ws],
        out_shape=[_sds((N_CHIPS,) + w.shape, BF16) for w in ws], scratch_shapes=[], vmem_mib=32)


def _gather_rider(stacked, part="both"):
    n, nf = len(stacked), len(CHIP_FLIPS)

    def phases(ins, outs, ssem, rsem):
        x, y, c = _place()
        slot = 2 * x + y
        ici, d2d = [], []
        for w, (i_ref, o_ref) in enumerate(zip(ins, outs)):
            hh = o_ref.shape[1] // 2
            rows = pl.ds(c * hh, hh)
            for f, (fx, fy) in enumerate(CHIP_FLIPS):
                k = w * nf + f
                theirs = 2 * (x ^ fx) + (y ^ fy)
                if part != "pair":
                    ici.append(_remote(i_ref.at[slot, rows], o_ref.at[slot, rows], ssem, rsem, k,
                                       (x ^ fx, y ^ fy, c)))
                if part != "chips":
                    d2d.append(_remote(o_ref.at[theirs, rows], o_ref.at[theirs, rows], ssem, rsem,
                                       (n * nf if part == "both" else 0) + k, (x, y, 1 - c)))
        return [ph for ph in (ici, d2d) if ph]

    return _Rider(stacked, [_sds(s.shape, s.dtype) for s in stacked], (2 if part == "both" else 1) * n * nf,
                  phases, aliases={i: i for i in range(n)})


def _pair_swap_rider(grads):
    def phases(ins, outs, ssem, rsem):
        x, y, c = _place()
        return [[_remote(g.at[:, 1 - c], o, ssem, rsem, k, (x, y, 1 - c))
                 for k, (g, o) in enumerate(zip(ins, outs))]]

    return _Rider(grads, [_sds((N_CHIPS,) + g.shape[2:], g.dtype) for g in grads], len(grads), phases)


def _chip_scatter_rider(parts):
    nf = len(CHIP_FLIPS)

    def phases(ins, outs, ssem, rsem):
        x, y, c = _place()
        return [[_remote(r.at[2 * (x ^ fx) + (y ^ fy)], o.at[f], ssem, rsem, w * nf + f, (x ^ fx, y ^ fy, c))
                 for w, (r, o) in enumerate(zip(ins, outs)) for f, (fx, fy) in enumerate(CHIP_FLIPS)]]

    return _Rider(parts, [_sds((nf,) + r.shape[1:], r.dtype) for r in parts], len(parts) * nf, phases)


def _pair_send_rider(halves):
    def phases(ins, outs, ssem, rsem):
        x, y, c = _place()
        return [[_remote(h, o, ssem, rsem, k, (x, y, 1 - c)) for k, (h, o) in enumerate(zip(ins, outs))]]

    return _Rider(halves, [_sds(h.shape, h.dtype) for h in halves], len(halves), phases)


PAIR_SUM_STEPS = 2
CHIP_SUM_STEPS = 4
ADAMW_STEPS = 4


def _no_rider():
    return _Rider([], [], 1, lambda ins, outs, ssem, rsem: [[]])


def _add_pair(grads, recvs, place, name, rider):
    n = len(grads)

    def body(pl_ref, *refs):
        ins, outs, _, ride = rider.split(refs, 2 * n, 2 * n, 0)
        j, h = pl.program_id(0), pl.program_id(1)

        @pl.when(jnp.logical_and(j == 0, h == 0))
        def _():
            rider.first(ride)

        for w in range(n):
            s = ins[2 * w][:, 0] + ins[2 * w + 1][...]
            outs[2 * w][...] = s
            outs[2 * w + 1][...] = s.astype(BF16)

        @pl.when(jnp.logical_and(j == N_CHIPS - 1, h == PAIR_SUM_STEPS - 1))
        def _():
            rider.last(ride)

    in_specs, out_specs, out_shape, args = [], [], [], []
    for g, r in zip(grads, recvs):
        _, _, H, C = g.shape
        th = H // PAIR_SUM_STEPS
        spec = pl.BlockSpec((1, th, C), lambda j, h, pr: (j, h, 0))
        in_specs += [pl.BlockSpec((1, 1, th, C), lambda j, h, pr: (j, pr[0], h, 0)), spec]
        out_specs += [spec, spec]
        out_shape += [_sds((N_CHIPS, H, C), F32), _sds((N_CHIPS, H, C), BF16)]
        args += [g, r]
    res = rider.call(body, args, grid=(N_CHIPS, PAIR_SUM_STEPS), name=name, prefetch=place, in_specs=in_specs,
                     out_specs=out_specs, out_shape=out_shape, scratch_shapes=[], vmem_mib=32)
    return [(res[2 * w], res[2 * w + 1]) for w in range(n)], res[2 * n:]


def _add_chips(parts, recvs, place, name, rider):
    n = len(parts)

    def body(pl_ref, *refs):
        ins, outs, _, ride = rider.split(refs, 2 * n, n, 0)
        h = pl.program_id(0)

        @pl.when(h == 0)
        def _():
            rider.first(ride)

        for w in range(n):
            p_ref, r_ref = ins[2 * w], ins[2 * w + 1]
            outs[w][...] = p_ref[0] + r_ref[0].astype(F32) + r_ref[1].astype(F32) + r_ref[2].astype(F32)

        @pl.when(h == CHIP_SUM_STEPS - 1)
        def _():
            rider.last(ride)

    in_specs, out_specs, out_shape, args = [], [], [], []
    for p, r in zip(parts, recvs):
        _, H, C = p.shape
        th = H // CHIP_SUM_STEPS
        in_specs += [pl.BlockSpec((1, th, C), lambda h, pr: (pr[1], h, 0)),
                     pl.BlockSpec((len(CHIP_FLIPS), th, C), lambda h, pr: (0, h, 0))]
        out_specs.append(pl.BlockSpec((th, C), lambda h, pr: (h, 0)))
        out_shape.append(_sds((H, C), F32))
        args += [p, r]
    res = rider.call(body, args, grid=(CHIP_SUM_STEPS,), name=name, prefetch=place, in_specs=in_specs,
                     out_specs=out_specs, out_shape=out_shape, scratch_shapes=[], vmem_mib=32)
    return res[:n], res[n:]


def _adamw_math(w, g, m, v):
    m = ADAM_B1 * m + (1.0 - ADAM_B1) * g
    v = ADAM_B2 * v + (1.0 - ADAM_B2) * (g * g)
    m_hat = m / (1.0 - ADAM_B1 ** ADAM_STEP)
    v_hat = v / (1.0 - ADAM_B2 ** ADAM_STEP)
    delta = -ADAM_LR * (m_hat / (jnp.sqrt(v_hat) + ADAM_EPS) + ADAM_WD * w)
    return delta, m, v


def _adamw(ws, mines, theirs, ms, vs, place, name, rider):
    n = len(ws)

    def body(pl_ref, *refs):
        ins, outs, _, ride = rider.split(refs, 5 * n, 4 * n, 0)
        h, r = pl.program_id(0), pl.program_id(1)

        @pl.when(jnp.logical_and(h == 0, r == 0))
        def _():
            rider.first(ride)

        for k in range(n):
            w_ref, a_ref, b_ref, m_ref, v_ref = ins[5 * k:5 * k + 5]
            g = jnp.where(h == pl_ref[0], a_ref[...], b_ref[...])
            d, mo, vo = _adamw_math(w_ref[...], g, m_ref[...], v_ref[...])
            for o_ref, val in zip(outs[4 * k:4 * k + 4], (g, d, mo, vo)):
                o_ref[...] = val

        @pl.when(jnp.logical_and(h == 1, r == ADAMW_STEPS - 1))
        def _():
            rider.last(ride)

    in_specs, out_specs, out_shape, args = [], [], [], []
    for w, a, b, m, v in zip(ws, mines, theirs, ms, vs):
        R, C = w.shape
        th = (R // 2) // ADAMW_STEPS
        whole = pl.BlockSpec((th, C), lambda h, r, pr: (h * ADAMW_STEPS + r, 0))
        mine_spec = pl.BlockSpec((th, C), lambda h, r, pr: (jnp.where(h == pr[0], r, 0), 0))
        theirs_spec = pl.BlockSpec((th, C), lambda h, r, pr: (jnp.where(h == pr[0], 0, r), 0))
        in_specs += [whole, mine_spec, theirs_spec, whole, whole]
        out_specs += [whole] * 4
        out_shape += [_sds((R, C), F32)] * 4
        args += [w, a, b, m, v]
    res = rider.call(body, args, grid=(2, ADAMW_STEPS), name=name, prefetch=place, in_specs=in_specs,
                     out_specs=out_specs, out_shape=out_shape, scratch_shapes=[], vmem_mib=40)
    return [tuple(res[4 * k:4 * k + 4]) for k in range(n)], res[4 * n:]


DEVICE_FLIPS = tuple((fx, fy, fc) for fx in (0, 1) for fy in (0, 1) for fc in (0, 1))[1:]


def _pack_exchange_rider(pack):
    def phases(ins, outs, ssem, rsem):
        x, y, c = _place()
        mine = outs[0].at[4 * x + 2 * y + c]
        copies = [_remote(ins[0], mine, ssem, rsem, k, (x ^ fx, y ^ fy, c ^ fc))
                  for k, (fx, fy, fc) in enumerate(DEVICE_FLIPS)]
        copies.append(functools.partial(pltpu.make_async_copy, ins[0], mine, ssem.at[len(DEVICE_FLIPS)]))
        return [copies]

    return _Rider([pack], [_sds((N_DEV,) + pack.shape, pack.dtype)], len(DEVICE_FLIPS) + 1, phases)


def _small_sum_adamw(recv_a, recv_b, wpack, mpack, vpack):
    R = wpack.shape[0]

    def body(a_ref, b_ref, w_ref, m_ref, v_ref, gs_ref, d_ref, mo_ref, vo_ref):
        ta, tb = a_ref[0], b_ref[0]
        for dev in range(1, N_DEV):
            ta = ta + a_ref[dev]
            tb = tb + b_ref[dev]
        total = jnp.concatenate([ta, tb], axis=0)
        gs_ref[...] = total
        d, mo, vo = _adamw_math(w_ref[...], total, m_ref[...], v_ref[...])
        d_ref[...] = d
        mo_ref[...] = mo
        vo_ref[...] = vo

    return pl.pallas_call(
        body, name="small_sum_adamw", in_specs=[VMEM_SPEC] * 5, out_specs=[VMEM_SPEC] * 4,
        out_shape=[_sds((R, LANES), F32)] * 4,
    )(recv_a, recv_b, wpack, mpack, vpack)


def _rows8(a):
    a = a.reshape(-1, LANES)
    pad = (-a.shape[0]) % 8
    return jnp.pad(a, ((0, pad), (0, 0))) if pad else a


def _pack(parts):
    return jnp.concatenate([_rows8(a) for a in parts], axis=0)


def _unpack(pack, like):
    out, row = [], 0
    for a in like:
        n = a.size // LANES
        out.append(pack[row:row + n].reshape(a.shape))
        row += n + (-n) % 8
    return out


def kernel(x, p, emb_ln_g, emb_ln_b, w_in, attn_out_g, w_pool, pool_scale, w_out, ln1_g, ln1_b, w_up, w_down, ln2_g, ln2_b, w_ple, w_ple_gate, ln3_g, ln3_b, loss_target, m_emb_ln_g, m_emb_ln_b, m_w_in, m_attn_out_g, m_w_pool, m_pool_scale, m_w_out, m_ln1_g, m_ln1_b, m_w_up, m_w_down, m_ln2_g, m_ln2_b, m_w_ple, m_w_ple_gate, m_ln3_g, m_ln3_b, v_emb_ln_g, v_emb_ln_b, v_w_in, v_attn_out_g, v_w_pool, v_pool_scale, v_w_out, v_ln1_g, v_ln1_b, v_w_up, v_w_down, v_ln2_g, v_ln2_b, v_w_ple, v_w_ple_gate, v_ln3_g, v_ln3_b):
    S = x.shape[1]
    tm = min(256, S)
    tq = min(256, S)
    tm_mlp = min(512, S)
    xs = x[0]
    ps = p[0, 0]
    tgt = loss_target[0]
    row = lambda a: a.reshape(1, -1)
    g0, b0 = row(emb_ln_g), row(emb_ln_b)
    g1, b1, g2, b2, g3, b3 = ln1_g, ln1_b, ln2_g, ln2_b, ln3_g, ln3_b
    wp = w_pool[0]

    xi, yi, ci = _place()
    place = jnp.stack([ci, 2 * xi + yi]).astype(jnp.int32)
    names = ["w_in", "w_out", "w_up", "w_down", "w_ple", "w_ple_gate"]

    big = [w_in[0], w_out[0], w_up[0], w_down[0], w_ple[0], w_ple_gate[0]]
    s_in = _cast_into_slot(big[0], place, "cast_w_in")
    s_out, s_up, s_down, s_ple, s_gate, w_in_s = _cast_rest(big[1:], place, _gather_rider([s_in]))

    xh0, rstd0, q, k, v, u, s_out, s_ple, s_gate = _embln_inproj(
        xs, g0, b0, w_in_s, tm, _gather_rider([s_out, s_ple, s_gate], "chips"))
    o_raw, on, s_up, s_down, w_out_s, w_ple_s, w_gate_s = _attn_fwd(
        q, k, v, attn_out_g, tq, _gather_rider([s_up, s_down], "chips") + _gather_rider([s_out, s_ple, s_gate], "pair"))
    w_out_f = w_out_s.reshape(D_MODEL, D_MODEL)
    w_gate_f = w_gate_s.reshape(D_MODEL, D_MODEL)
    d_b, pooled = _pool_fwd(u, wp, pool_scale, tm)
    xh1, rstd1, x1b, w_up_s, w_down_s = _mix_ln1(on, pooled, xh0, g0, b0, w_out_f, g1, b1, tm,
                                                 _gather_rider([s_up, s_down], "pair"))
    w_down_f = w_down_s.reshape(D_FF, D_MODEL)
    xh2, rstd2, rb = _mlp_ln2(xh1, x1b, g1, b1, w_up_s, w_down_f, tm_mlp)

    (dpre2, dhb, dw_ple, dw_gate, dg3, db3, dg2, db2, loss_row) = _ple_ln3_loss(
        xh2, rstd2, g2, b2, ps, w_ple_s, w_gate_f, g3, b3, tgt, tm)
    dx1m, da = _mlp_bwd(rb, dhb, w_up_s, w_down_f, tm_mlp)
    dw_up = _tn_matmul(x1b, da, "grad_w_up", 1024, min(512, S), stacked=True)
    dw_down = _tn_matmul(rb, dhb, "grad_w_down", 1024, min(512, S), stacked=False, square_a=True)
    def halves_of(g):
        return g.reshape(N_CHIPS, 2, g.shape[1] // 2, g.shape[2])

    early = [halves_of(g) for g in (dw_up, dw_down.reshape(N_CHIPS, D_FF // N_CHIPS, D_MODEL), dw_ple,
                                    dw_gate.reshape(N_CHIPS, D_MODEL // N_CHIPS, D_MODEL))]
    dpre1, don, dpooled, dw_out, dg1, db1, *early_pair = _mix_bwd(
        dpre2, dx1m, xh1, rstd1, g1, w_out_f, on, pooled, tm, _pair_swap_rider(early))
    early_sum, _ = _add_pair(early, early_pair, place, "pair_sum_mlp_ple", _no_rider())
    out_halves = halves_of(dw_out.reshape(N_CHIPS, D_MODEL // N_CHIPS, D_MODEL))
    du, dwp, dsc, out_pair = _pool_bwd(dpooled, d_b, wp, pool_scale, tm, _pair_swap_rider([out_halves]))
    (out_sum,), _ = _add_pair([out_halves], [out_pair], place, "pair_sum_w_out", _no_rider())
    pack_a = _pack([jnp.broadcast_to(loss_row, (8, LANES)), dwp, dsc, dg1, db1, dg2, db2, dg3, db3])
    early_sum = [out_sum] + early_sum
    riding = _chip_scatter_rider([b for _, b in early_sum]) + _pack_exchange_rider(pack_a)
    dq, dk, dv, dga, *arrived = _attn_bwd(q, k, v, don, o_raw, attn_out_g, tq, riding)
    early_chips, recv_a = arrived[:-1], arrived[-1]
    grad_x, dw_in, dg0, db0 = _inproj_bwd(dq, dk, dv, du, dpre1, xh0, rstd0, g0, b0, w_in_s, tm)

    in_halves = halves_of(dw_in)
    pack_b = _pack([dg0, db0, dga])
    early_mine, (in_pair, recv_b) = _add_chips(
        [s for s, _ in early_sum], early_chips, place, "chip_sum_early",
        _pair_swap_rider([in_halves]) + _pack_exchange_rider(pack_b))
    (in_sum,), early_theirs = _add_pair([in_halves], [in_pair], place, "pair_sum_w_in", _pair_send_rider(early_mine))
    ms = [m_w_in, m_w_out, m_w_up, m_w_down, m_w_ple, m_w_ple_gate]
    vs = [v_w_in, v_w_out, v_w_up, v_w_down, v_w_ple, v_w_ple_gate]
    early_res, _ = _adamw(big[1:], early_mine, early_theirs, [m[0] for m in ms[1:]], [v[0] for v in vs[1:]],
                          place, "adamw_early", _no_rider())
    (in_chips,) = _chip_scatter_rider([in_sum[1]]).run("reduce_chips_late")
    (in_mine,), _ = _add_chips([in_sum[0]], [in_chips], place, "chip_sum_w_in", _no_rider())
    (in_theirs,) = _pair_send_rider([in_mine]).run("gather_pair_w_in")
    in_res, _ = _adamw(big[:1], [in_mine], [in_theirs], [ms[0][0]], [vs[0][0]], place, "adamw_w_in", _no_rider())
    big_out = {n: tuple(r.reshape(m.shape) for r in res4) for n, res4, m in zip(names, in_res + early_res, ms)}

    small_names = ["w_pool", "pool_scale", "ln1_g", "ln1_b", "ln2_g", "ln2_b", "ln3_g", "ln3_b",
                   "emb_ln_g", "emb_ln_b", "attn_out_g"]
    small_w = [w_pool, pool_scale, ln1_g, ln1_b, ln2_g, ln2_b, ln3_g, ln3_b, emb_ln_g, emb_ln_b, attn_out_g]
    small_m = [m_w_pool, m_pool_scale, m_ln1_g, m_ln1_b, m_ln2_g, m_ln2_b, m_ln3_g, m_ln3_b,
               m_emb_ln_g, m_emb_ln_b, m_attn_out_g]
    small_v = [v_w_pool, v_pool_scale, v_ln1_g, v_ln1_b, v_ln2_g, v_ln2_b, v_ln3_g, v_ln3_b,
               v_emb_ln_g, v_emb_ln_b, v_attn_out_g]
    loss_like = jnp.zeros((8, LANES), F32)
    gs, ds, mos, vos = _small_sum_adamw(recv_a, recv_b, _pack([loss_like] + small_w), _pack([loss_like] + small_m),
                                        _pack([jnp.ones((8, LANES), F32)] + small_v))
    like = [loss_like] + small_w
    gs_u, ds_u, mos_u, vos_u = (_unpack(a, like) for a in (gs, ds, mos, vos))
    loss = gs_u[0][0, 0]
    small_out = {n: (gs_u[i + 1], ds_u[i + 1], mos_u[i + 1], vos_u[i + 1]) for i, n in enumerate(small_names)}

    order = ["emb_ln_g", "emb_ln_b", "w_in", "attn_out_g", "w_pool", "pool_scale", "w_out", "ln1_g", "ln1_b",
             "w_up", "w_down", "ln2_g", "ln2_b", "w_ple", "w_ple_gate", "ln3_g", "ln3_b"]
    res = {**big_out, **small_out}
    outs = [loss, grad_x.reshape(x.shape)]
    for kind in range(4):
        outs += [res[n][kind] for n in order]
    return tuple(outs)
```

```python
import functools

import jax
import jax.numpy as jnp
from jax import lax
from jax.experimental import pallas as pl
from jax.experimental.pallas import tpu as pltpu

F32 = jnp.float32
BF16 = jnp.bfloat16

D_MODEL = 1024
ATTN_WIDTH = 512
POOL_WIDTH = 512
HEAD_DIM = 64
PAIR = 2 * HEAD_DIM
N_PAIRS = ATTN_WIDTH // PAIR
N_POOL_GROUPS = 4
POOL_GROUP = 128
POOL_HALO = 16
D_FF = 4096
PLE_DIM = 256
N_CHIPS = 4
N_DEV = 8
LN_EPS = 1e-5
RMS_EPS = 1e-6
ALPHA = float(2.0 ** 0.25)
Q_SCALE = 0.125
ADAM_LR = 0.001
ADAM_B1 = 0.9
ADAM_B2 = 0.999
ADAM_EPS = 1e-08
ADAM_WD = 0.01
ADAM_STEP = 10
LANES = 128
MIB = 1024 * 1024

MESH = pl.DeviceIdType.MESH
HBM_SPEC = pl.BlockSpec(memory_space=pltpu.HBM)
VMEM_SPEC = pl.BlockSpec(memory_space=pltpu.VMEM)


def _cp(vmem_mib):
    return pltpu.CompilerParams(vmem_limit_bytes=vmem_mib * MIB)


def _dot(a, b):
    return jnp.dot(a, b, preferred_element_type=F32)


def _dot_nt(a, b):
    return lax.dot_general(a, b, (((1,), (1,)), ((), ())), preferred_element_type=F32)


def _dot_tn(a, b):
    return lax.dot_general(a, b, (((0,), (0,)), ((), ())), preferred_element_type=F32)


def _ln_fwd(pre):
    mu = jnp.mean(pre, axis=-1, keepdims=True)
    xc = pre - mu
    var = jnp.mean(xc * xc, axis=-1, keepdims=True)
    rstd = lax.rsqrt(var + LN_EPS)
    return xc * rstd, rstd


def _ln_bwd(dy, xh, rstd, g):
    dxh = dy * g
    m1 = jnp.mean(dxh, axis=-1, keepdims=True)
    m2 = jnp.mean(dxh * xh, axis=-1, keepdims=True)
    return rstd * (dxh - m1 - xh * m2)


def _colsum(a):
    return jnp.sum(a, axis=0, keepdims=True)


def _neg_softplus(z):
    return -(jnp.maximum(z, 0.0) + jnp.log(1.0 + jnp.exp(-jnp.abs(z))))


def _split_bf16(a):
    hi = a.astype(BF16)
    lo = (a - hi.astype(F32)).astype(BF16)
    return hi, lo


def _row_spec(tm, n):
    return pl.BlockSpec((tm, n), lambda i: (i, 0))


def _const_spec(shape):
    nd = len(shape)
    return pl.BlockSpec(shape, lambda *_: (0,) * nd)


def _hbm(*arrays):
    return [pltpu.with_memory_space_constraint(a, pltpu.HBM) for a in arrays]


def _sds(shape, dtype):
    return pltpu.HBM(shape, dtype)


def _embln_inproj(x, g0, b0, w_in_s, tm, rider):
    S, D = x.shape
    n_t = S // tm

    def body(*refs):
        ((x_ref, g_ref, b_ref, w_ref), (xh_ref, rstd_ref, q_ref, k_ref, v_ref, u_ref), _,
         ride) = rider.split(refs, 4, 6, 0)
        i = pl.program_id(0)

        @pl.when(i == 0)
        def _():
            rider.first(ride)

        @pl.when(i == (3 * n_t) // 4)
        def _():
            rider.mid(ride)

        xh, rstd = _ln_fwd(x_ref[...])
        xh_ref[...] = xh
        rstd_ref[...] = rstd
        xb = (xh * g_ref[...] + b_ref[...]).astype(BF16)
        q_ref[...] = (_dot(xb, w_ref[0]) * Q_SCALE).astype(BF16)
        k_ref[...] = _dot(xb, w_ref[1]).astype(BF16)
        v_ref[...] = _dot(xb, w_ref[2]).astype(BF16)
        u_ref[...] = _dot(xb, w_ref[3])

        @pl.when(i == n_t - 1)
        def _():
            rider.last(ride)

    return rider.call(
        body, [x, g0, b0, w_in_s], grid=(n_t,), name="embln_inproj",
        in_specs=[_row_spec(tm, D), _const_spec((1, D)), _const_spec((1, D)),
                  _const_spec((N_CHIPS, D, 512))],
        out_specs=[_row_spec(tm, D), _row_spec(tm, 1), _row_spec(tm, 512), _row_spec(tm, 512),
                   _row_spec(tm, 512), _row_spec(tm, 512)],
        out_shape=[_sds((S, D), F32), _sds((S, 1), F32), _sds((S, 512), BF16), _sds((S, 512), BF16),
                   _sds((S, 512), BF16), _sds((S, 512), F32)],
        scratch_shapes=[], vmem_mib=40)


def _tri(n, upper):
    r = lax.broadcasted_iota(jnp.int32, (n, n), 0)
    c = lax.broadcasted_iota(jnp.int32, (n, n), 1)
    keep = (r < c) if upper else (r > c)
    return jnp.where(keep, 1.0, 0.0).astype(BF16)


def _strictly_causal(n):
    return lax.broadcasted_iota(jnp.int32, (n, n), 1) < lax.broadcasted_iota(jnp.int32, (n, n), 0)


LOG_WEIGHT_FLOOR = -110.0


def _sb_tile(qhs, kt, low, c_ls, valid):
    valids = valid if isinstance(valid, (list, tuple)) else [valid] * len(qhs)
    zs = [_dot_nt(qh, kt) for qh in qhs]
    lrs = [_neg_softplus(z) for z in zs]
    ls_ = [lr if m is None else jnp.where(m, lr, 0.0) for lr, m in zip(lrs, valids)]
    sfx = [_dot(l.astype(BF16), low) + c_l for l, c_l in zip(ls_, c_ls)]
    lss = [z + lr for z, lr in zip(zs, lrs)]
    ws = [jnp.exp(ls + s) for ls, s in zip(lss, sfx)]
    ws = [w if m is None else jnp.where(m, w, 0.0) for w, m in zip(ws, valids)]
    return lss, ls_, ws


def _attn_fwd(q, k, v, ga, tq, rider):
    S = q.shape[0]
    nq = S // tq

    def body(*refs):
        (q_ref, k_ref, v_ref, ga_ref), (o_ref, on_ref), _, ride = rider.split(refs, 4, 2, 0)
        p, i = pl.program_id(0), pl.program_id(1)

        @pl.when(jnp.logical_and(p == 0, i == 0))
        def _():
            rider.first(ride)

        @pl.when(jnp.logical_and(p == N_PAIRS - 1, i == 0))
        def _():
            rider.mid(ride)

        lane = lax.broadcasted_iota(jnp.int32, (1, PAIR), 1)
        m0 = lane < HEAD_DIM
        low = _tri(tq, upper=False)
        q2 = q_ref[...]
        qhs = [jnp.where(m0, q2, jnp.zeros_like(q2)), jnp.where(m0, jnp.zeros_like(q2), q2)]

        def tile(kb, c_ls, accs, valid):
            ks = pl.multiple_of(kb * tq, tq)
            kt = k_ref[pl.ds(ks, tq), :]
            vt = v_ref[pl.ds(ks, tq), :]
            _, ls_, ws = _sb_tile(qhs, kt, low, c_ls, valid)
            new_a = [acc + _dot(w.astype(BF16), vt) for acc, w in zip(accs, ws)]
            new_c = [c_l + jnp.sum(l, axis=1, keepdims=True) for c_l, l in zip(c_ls, ls_)]
            return new_c, new_a

        zc, za = jnp.zeros((tq, 1), F32), jnp.zeros((tq, PAIR), F32)

        def first_two():
            c_ls, accs = tile(i, [zc, zc], [za, za], _strictly_causal(tq))
            c_ls, accs = tile(i - 1, c_ls, accs, None)
            return (*c_ls, *accs)

        def first_one():
            c_ls, accs = tile(i, [zc, zc], [za, za], _strictly_causal(tq))
            return (*c_ls, *accs)

        st0 = lax.cond(i >= 1, first_two, first_one)

        def more(st):
            return jnp.logical_and(st[0] <= i, jnp.max(jnp.maximum(st[1], st[2])) > LOG_WEIGHT_FLOOR)

        def step(st):
            n, c0, c1, a0, a1 = st
            c_ls, accs = tile(i - n, [c0, c1], [a0, a1], None)
            return (n + 1, c_ls[0], c_ls[1], accs[0], accs[1])

        st = lax.while_loop(more, step, (jnp.int32(2), *st0))
        o = jnp.where(m0, st[3], st[4])
        o_ref[...] = o
        sq = o * o
        ms0 = jnp.sum(jnp.where(m0, sq, 0.0), axis=-1, keepdims=True) * (1.0 / HEAD_DIM)
        ms1 = jnp.sum(jnp.where(m0, 0.0, sq), axis=-1, keepdims=True) * (1.0 / HEAD_DIM)
        rs = jnp.where(m0, lax.rsqrt(ms0 + RMS_EPS), lax.rsqrt(ms1 + RMS_EPS))
        on_ref[...] = (o * rs * ga_ref[...]).astype(BF16)

        @pl.when(jnp.logical_and(p == N_PAIRS - 1, i == nq - 1))
        def _():
            rider.last(ride)

    return rider.call(
        body, [q, k, v, ga], grid=(N_PAIRS, nq), name="attn_fwd",
        in_specs=[pl.BlockSpec((tq, PAIR), lambda p, i: (i, p)),
                  pl.BlockSpec((S, PAIR), lambda p, i: (0, p)),
                  pl.BlockSpec((S, PAIR), lambda p, i: (0, p)),
                  pl.BlockSpec((1, PAIR), lambda p, i: (0, p))],
        out_specs=[pl.BlockSpec((tq, PAIR), lambda p, i: (i, p)),
                   pl.BlockSpec((tq, PAIR), lambda p, i: (i, p))],
        out_shape=[_sds((S, ATTN_WIDTH), F32), _sds((S, ATTN_WIDTH), BF16)],
        scratch_shapes=[], vmem_mib=40)


def _pool_fwd(u, w_pool, pscale, tm):
    S = u.shape[0]
    hb = tm // POOL_HALO

    def body(u_ref, uh_ref, wp_ref, sc_ref, d_ref, pooled_ref):
        i = pl.program_id(0)
        halo = jnp.where(i > 0, uh_ref[...], 0.0)
        pos = i * tm + lax.broadcasted_iota(jnp.int32, (tm, 1), 0)
        for g in range(N_POOL_GROUPS):
            win = 2 ** (g + 1)
            cols = slice(g * POOL_GROUP, (g + 1) * POOL_GROUP)
            ut = u_ref[:, cols]
            s = jnp.concatenate([halo[:, cols], ut], axis=0)
            for sh in (1, 2, 4, 8)[:g + 1]:
                s = s + pltpu.roll(s, sh, 0)
            cnt = jnp.minimum(pos + 1, win).astype(F32)
            db = (s[POOL_HALO:, :] / cnt - ut).astype(BF16)
            y = _dot(db, wp_ref[g].astype(BF16))
            d_ref[:, cols] = db
            pooled_ref[:, cols] = (y * sc_ref[:, cols]).astype(BF16)

    return pl.pallas_call(
        body, grid=(S // tm,), name="pool_fwd",
        in_specs=[_row_spec(tm, POOL_WIDTH),
                  pl.BlockSpec((POOL_HALO, POOL_WIDTH), lambda i: (jnp.maximum(i * hb - 1, 0), 0)),
                  _const_spec((N_POOL_GROUPS, POOL_GROUP, POOL_GROUP)), _const_spec((1, POOL_WIDTH))],
        out_specs=[_row_spec(tm, POOL_WIDTH), _row_spec(tm, POOL_WIDTH)],
        out_shape=[_sds((S, POOL_WIDTH), BF16), _sds((S, POOL_WIDTH), BF16)],
        compiler_params=_cp(32),
    )(*_hbm(u, u, w_pool, pscale))


def _mix_ln1(on, pooled, xh0, g0, b0, w_out, g1, b1, tm, rider):
    S, D = xh0.shape
    n_t = S // tm

    def body(*refs):
        ((on_ref, po_ref, xh0_ref, g0_ref, b0_ref, w_ref, g1_ref, b1_ref), (xh_ref, rstd_ref, xb_ref), _,
         ride) = rider.split(refs, 8, 3, 0)

        @pl.when(pl.program_id(0) == 0)
        def _():
            rider.first(ride)

        mixed = _dot(on_ref[...], w_ref[:ATTN_WIDTH, :]) + _dot(po_ref[...], w_ref[ATTN_WIDTH:, :])
        x0 = xh0_ref[...] * g0_ref[...] + b0_ref[...]
        xh, rstd = _ln_fwd(ALPHA * x0 + mixed)
        xh_ref[...] = xh
        rstd_ref[...] = rstd
        xb_ref[...] = (xh * g1_ref[...] + b1_ref[...]).astype(BF16)

        @pl.when(pl.program_id(0) == n_t - 1)
        def _():
            rider.last(ride)

    return rider.call(
        body, [on, pooled, xh0, g0, b0, w_out, g1, b1], grid=(n_t,), name="mix_ln1",
        in_specs=[_row_spec(tm, ATTN_WIDTH), _row_spec(tm, POOL_WIDTH), _row_spec(tm, D),
                  _const_spec((1, D)), _const_spec((1, D)), _const_spec((D, D)),
                  _const_spec((1, D)), _const_spec((1, D))],
        out_specs=[_row_spec(tm, D), _row_spec(tm, 1), _row_spec(tm, D)],
        out_shape=[_sds((S, D), F32), _sds((S, 1), F32), _sds((S, D), BF16)],
        scratch_shapes=[], vmem_mib=40)


def _mlp_ln2(xh1, x1b, g1, b1, w_up_s, w_down, tm):
    S, D = xh1.shape
    fc = D_FF // N_CHIPS

    def body(xh_ref, xb_ref, g_ref, b_ref, wu_ref, wd_ref, xh2_ref, rstd_ref, r_ref, acc_ref):
        j = pl.program_id(1)

        @pl.when(j == 0)
        def _():
            acc_ref[...] = jnp.zeros_like(acc_ref)

        r = jnp.maximum(_dot(xb_ref[...], wu_ref[0]), 0.0)
        r_ref[...] = r.astype(BF16)
        acc_ref[...] += _dot((r * r).astype(BF16), wd_ref[...])

        @pl.when(j == N_CHIPS - 1)
        def _():
            x1 = xh_ref[...] * g_ref[...] + b_ref[...]
            xh, rstd = _ln_fwd(ALPHA * x1 + acc_ref[...])
            xh2_ref[...] = xh
            rstd_ref[...] = rstd

    return pl.pallas_call(
        body, grid=(S // tm, N_CHIPS), name="mlp_ln2",
        in_specs=[pl.BlockSpec((tm, D), lambda i, j: (i, 0)), pl.BlockSpec((tm, D), lambda i, j: (i, 0)),
                  pl.BlockSpec((1, D), lambda i, j: (0, 0)), pl.BlockSpec((1, D), lambda i, j: (0, 0)),
                  pl.BlockSpec((1, D, fc), lambda i, j: (j, 0, 0)),
                  pl.BlockSpec((fc, D), lambda i, j: (j, 0))],
        out_specs=[pl.BlockSpec((tm, D), lambda i, j: (i, 0)), pl.BlockSpec((tm, 1), lambda i, j: (i, 0)),
                   pl.BlockSpec((tm, fc), lambda i, j: (i, j))],
        out_shape=[_sds((S, D), F32), _sds((S, 1), F32), _sds((S, D_FF), BF16)],
        scratch_shapes=[pltpu.VMEM((tm, D), F32)],
        compiler_params=_cp(56),
    )(*_hbm(xh1, x1b, g1, b1, w_up_s, w_down))


def _ple_ln3_loss(xh2, rstd2, g2, b2, p, w_ple_s, w_gate, g3, b3, target, tm):
    S, D = xh2.shape
    pc = D // N_CHIPS

    def body(xh2_ref, rstd2_ref, g2_ref, b2_ref, p_ref, wp_ref, wg_ref, g3_ref, b3_ref, t_ref,
             dpre2_ref, dhb_ref, dwp_ref, dwg_ref, dg3_ref, db3_ref, dg2_ref, db2_ref, loss_ref):
        i = pl.program_id(0)

        @pl.when(i == 0)
        def _():
            for r in (dwp_ref, dwg_ref, dg3_ref, db3_ref, dg2_ref, db2_ref, loss_ref):
                r[...] = jnp.zeros_like(r)

        xh2 = xh2_ref[...]
        x2 = xh2 * g2_ref[...] + b2_ref[...]
        x2b = x2.astype(BF16)
        gate = 1.0 / (1.0 + jnp.exp(-_dot(x2b, wg_ref[...])))
        pb = p_ref[...].astype(BF16)
        pe = jnp.concatenate([_dot(pb, wp_ref[c]) for c in range(N_CHIPS)], axis=1)
        xh3, rstd3 = _ln_fwd(ALPHA * x2 + pe * gate)
        diff = xh3 * g3_ref[...] + b3_ref[...] - t_ref[...]
        loss_ref[...] += (0.5 / D) * jnp.sum(diff * diff)
        dy = diff * (1.0 / D)
        dg3_ref[...] += _colsum(dy * xh3)
        db3_ref[...] += _colsum(dy)
        dpre3 = _ln_bwd(dy, xh3, rstd3, g3_ref[...])
        dpe_b = (dpre3 * gate).astype(BF16)
        dgp_b = (dpre3 * pe * gate * (1.0 - gate)).astype(BF16)
        dx2 = ALPHA * dpre3 + _dot_nt(dgp_b, wg_ref[...])
        dwg_ref[...] += _dot_tn(x2b, dgp_b)
        for c in range(N_CHIPS):
            dwp_ref[c] += _dot_tn(pb, dpe_b[:, c * pc:(c + 1) * pc])
        dg2_ref[...] += _colsum(dx2 * xh2)
        db2_ref[...] += _colsum(dx2)
        dpre2 = _ln_bwd(dx2, xh2, rstd2_ref[...], g2_ref[...])
        dpre2_ref[...] = dpre2
        dhb_ref[...] = dpre2.astype(BF16)

    vec = _const_spec((1, D))
    return pl.pallas_call(
        body, grid=(S // tm,), name="ple_ln3_loss",
        in_specs=[_row_spec(tm, D), _row_spec(tm, 1), vec, vec, _row_spec(tm, PLE_DIM),
                  _const_spec((N_CHIPS, PLE_DIM, pc)), _const_spec((D, D)), vec, vec, _row_spec(tm, D)],
        out_specs=[_row_spec(tm, D), _row_spec(tm, D), _const_spec((N_CHIPS, PLE_DIM, pc)),
                   _const_spec((D, D)), vec, vec, vec, vec, _const_spec((1, LANES))],
        out_shape=[_sds((S, D), F32), _sds((S, D), BF16), _sds((N_CHIPS, PLE_DIM, pc), F32),
                   _sds((D, D), F32), _sds((1, D), F32), _sds((1, D), F32), _sds((1, D), F32),
                   _sds((1, D), F32), _sds((1, LANES), F32)],
        compiler_params=_cp(48),
    )(*_hbm(xh2, rstd2, g2, b2, p, w_ple_s, w_gate, g3, b3, target))


def _mlp_bwd(rb, dhb, w_up_s, w_down, tm):
    S, D = dhb.shape
    fc = D_FF // N_CHIPS

    def body(r_ref, dh_ref, wu_ref, wd_ref, dx_ref, da_ref):
        @pl.when(pl.program_id(1) == 0)
        def _():
            dx_ref[...] = jnp.zeros_like(dx_ref)

        da = (_dot_nt(dh_ref[...], wd_ref[...]) * (2.0 * r_ref[...].astype(F32))).astype(BF16)
        da_ref[...] = da
        dx_ref[...] += _dot_nt(da, wu_ref[0])

    return pl.pallas_call(
        body, grid=(S // tm, N_CHIPS), name="mlp_bwd",
        in_specs=[pl.BlockSpec((tm, fc), lambda i, j: (i, j)), pl.BlockSpec((tm, D), lambda i, j: (i, 0)),
                  pl.BlockSpec((1, D, fc), lambda i, j: (j, 0, 0)),
                  pl.BlockSpec((fc, D), lambda i, j: (j, 0))],
        out_specs=[pl.BlockSpec((tm, D), lambda i, j: (i, 0)), pl.BlockSpec((tm, fc), lambda i, j: (i, j))],
        out_shape=[_sds((S, D), F32), _sds((S, D_FF), BF16)],
        compiler_params=_cp(56),
    )(*_hbm(rb, dhb, w_up_s, w_down))


def _tn_matmul(a, b, name, tk, tt, stacked, square_a=False):
    T, K = a.shape
    N = b.shape[1]
    tn = 1024

    def body(a_ref, b_ref, o_ref):
        @pl.when(pl.program_id(2) == 0)
        def _():
            o_ref[...] = jnp.zeros_like(o_ref)

        a_t = a_ref[...]
        if square_a:
            a_t = a_t * a_t
        prod = _dot_tn(a_t, b_ref[...])
        if stacked:
            o_ref[0] += prod
        else:
            o_ref[...] += prod

    if stacked:
        out_spec = pl.BlockSpec((1, tk, tn), lambda k, n, t: (n, k, 0))
        out_shape = _sds((N // tn, K, tn), F32)
    else:
        out_spec = pl.BlockSpec((tk, tn), lambda k, n, t: (k, n))
        out_shape = _sds((K, N), F32)
    return pl.pallas_call(
        body, grid=(K // tk, N // tn, T // tt), name=name,
        in_specs=[pl.BlockSpec((tt, tk), lambda k, n, t: (t, k)),
                  pl.BlockSpec((tt, tn), lambda k, n, t: (t, n))],
        out_specs=out_spec, out_shape=out_shape,
        compiler_params=_cp(40),
    )(*_hbm(a, b))


def _mix_bwd(dpre2, dx1m, xh1, rstd1, g1, w_out, on, pooled, tm, rider):
    S, D = xh1.shape
    n_t = S // tm

    def body(*refs):
        ((dp2_ref, dxm_ref, xh_ref, rstd_ref, g_ref, w_ref, on_ref, po_ref),
         (dpre1_ref, don_ref, dpo_ref, dw_ref, dg_ref, db_ref), _, ride) = rider.split(refs, 8, 6, 0)

        @pl.when(pl.program_id(0) == 0)
        def _():
            rider.first(ride)
            for r in (dw_ref, dg_ref, db_ref):
                r[...] = jnp.zeros_like(r)

        xh = xh_ref[...]
        dx1 = ALPHA * dp2_ref[...] + dxm_ref[...]
        dg_ref[...] += _colsum(dx1 * xh)
        db_ref[...] += _colsum(dx1)
        dpre1 = _ln_bwd(dx1, xh, rstd_ref[...], g_ref[...])
        dpre1_ref[...] = dpre1
        dmb = dpre1.astype(BF16)
        dcat = _dot_nt(dmb, w_ref[...])
        don_ref[...] = dcat[:, :ATTN_WIDTH]
        dpo_ref[...] = dcat[:, ATTN_WIDTH:]
        dw_ref[:ATTN_WIDTH, :] += _dot_tn(on_ref[...], dmb)
        dw_ref[ATTN_WIDTH:, :] += _dot_tn(po_ref[...], dmb)

        @pl.when(pl.program_id(0) == n_t - 1)
        def _():
            rider.last(ride)

    vec = _const_spec((1, D))
    return rider.call(
        body, [dpre2, dx1m, xh1, rstd1, g1, w_out, on, pooled], grid=(n_t,), name="mix_bwd",
        in_specs=[_row_spec(tm, D), _row_spec(tm, D), _row_spec(tm, D), _row_spec(tm, 1), vec,
                  _const_spec((D, D)), _row_spec(tm, ATTN_WIDTH), _row_spec(tm, POOL_WIDTH)],
        out_specs=[_row_spec(tm, D), _row_spec(tm, ATTN_WIDTH), _row_spec(tm, POOL_WIDTH),
                   _const_spec((D, D)), vec, vec],
        out_shape=[_sds((S, D), F32), _sds((S, ATTN_WIDTH), F32), _sds((S, POOL_WIDTH), F32),
                   _sds((D, D), F32), _sds((1, D), F32), _sds((1, D), F32)],
        scratch_shapes=[], vmem_mib=48)


def _pool_bwd(dpooled, d_b, w_pool, pscale, tm, rider):
    S = dpooled.shape[0]
    hb = tm // POOL_HALO
    n_t = S // tm
    te = tm + POOL_HALO

    def body(*refs):
        ((dp_ref, dph_ref, d_ref, wp_ref, sc_ref), (du_ref, dwp_ref, dsc_ref), _,
         ride) = rider.split(refs, 5, 3, 0)
        i = pl.program_id(0)

        @pl.when(i == 0)
        def _():
            rider.first(ride)
            dwp_ref[...] = jnp.zeros_like(dwp_ref)
            dsc_ref[...] = jnp.zeros_like(dsc_ref)

        halo = jnp.where(i < n_t - 1, dph_ref[...], 0.0)
        pos = i * tm + lax.broadcasted_iota(jnp.int32, (te, 1), 0)
        for g in range(N_POOL_GROUPS):
            win = 2 ** (g + 1)
            cols = slice(g * POOL_GROUP, (g + 1) * POOL_GROUP)
            wpb = wp_ref[g].astype(BF16)
            dpt = dp_ref[:, cols]
            dpe = jnp.concatenate([dpt, halo[:, cols]], axis=0)
            dyb = (dpe * sc_ref[:, cols]).astype(BF16)
            dd = _dot_nt(dyb, wpb)
            s = dd / jnp.minimum(pos + 1, win).astype(F32)
            for sh in (1, 2, 4, 8)[:g + 1]:
                s = s + pltpu.roll(s, te - sh, 0)
            du_ref[:, cols] = s[:tm, :] - dd[:tm, :]
            db = d_ref[:, cols]
            dwp_ref[g] += _dot_tn(db, dyb[:tm, :])
            dsc_ref[:, cols] += _colsum(dpt * _dot(db, wpb))

        @pl.when(i == n_t - 1)
        def _():
            rider.last(ride)

    return rider.call(
        body, [dpooled, dpooled, d_b, w_pool, pscale], grid=(n_t,), name="pool_bwd",
        in_specs=[_row_spec(tm, POOL_WIDTH),
                  pl.BlockSpec((POOL_HALO, POOL_WIDTH),
                               lambda i: (jnp.minimum((i + 1) * hb, S // POOL_HALO - 1), 0)),
                  _row_spec(tm, POOL_WIDTH),
                  _const_spec((N_POOL_GROUPS, POOL_GROUP, POOL_GROUP)), _const_spec((1, POOL_WIDTH))],
        out_specs=[_row_spec(tm, POOL_WIDTH), _const_spec((N_POOL_GROUPS, POOL_GROUP, POOL_GROUP)),
                   _const_spec((1, POOL_WIDTH))],
        out_shape=[_sds((S, POOL_WIDTH), F32), _sds((N_POOL_GROUPS, POOL_GROUP, POOL_GROUP), F32),
                   _sds((1, POOL_WIDTH), F32)],
        scratch_shapes=[], vmem_mib=32)


def _attn_bwd(q, k, v, don, o_raw, ga, tq, rider):
    S = q.shape[0]
    nq = S // tq

    def body(*refs):
        ((q_ref, k_ref, v_ref, don_ref, o_ref, ga_ref), (dq_ref, dk_ref, dv_ref, dga_ref),
         (g_s, b_s), ride) = rider.split(refs, 6, 4, 2)
        p, i = pl.program_id(0), pl.program_id(1)

        @pl.when(jnp.logical_and(p == 0, i == 0))
        def _():
            rider.first(ride)

        @pl.when(i == 0)
        def _():
            for r in (dk_ref, dv_ref, dga_ref):
                r[...] = jnp.zeros_like(r)

        lane = lax.broadcasted_iota(jnp.int32, (1, PAIR), 1)
        m0 = lane < HEAD_DIM
        low = _tri(tq, upper=False)
        upp = _tri(tq, upper=True)

        def seg_mean(a):
            s0 = jnp.sum(jnp.where(m0, a, 0.0), axis=-1, keepdims=True)
            s1 = jnp.sum(jnp.where(m0, 0.0, a), axis=-1, keepdims=True)
            return jnp.where(m0, s0, s1) * (1.0 / HEAD_DIM)

        o = o_ref[...]
        rs = lax.rsqrt(seg_mean(o * o) + RMS_EPS)
        oh = o * rs
        don = don_ref[...]
        dga_ref[...] += _colsum(don * oh)
        doh = don * ga_ref[...]
        do = rs * (doh - oh * seg_mean(doh * oh))
        dob = do.astype(BF16)
        q2 = q_ref[...]
        qhs = [jnp.where(m0, q2, jnp.zeros_like(q2)), jnp.where(m0, jnp.zeros_like(q2), q2)]
        dhs = [jnp.where(m0, dob, jnp.zeros_like(dob)), jnp.where(m0, jnp.zeros_like(dob), dob)]
        causal = _strictly_causal(tq)

        def down(kb, c_ls, valid):
            ks = pl.multiple_of(kb * tq, tq)
            kt = k_ref[pl.ds(ks, tq), :]
            vt = v_ref[pl.ds(ks, tq), :]
            lss, ls_, ws = _sb_tile(qhs, kt, low, c_ls, valid)
            dws = [_dot_nt(dh, vt) for dh in dhs]
            for hh in range(2):
                g_s[hh, kb] = dws[hh] * ws[hh]
                b_s[hh, kb] = jnp.exp(lss[hh])
            dv_ref[pl.ds(ks, tq), :] += (_dot_tn(ws[0].astype(BF16), dhs[0])
                                         + _dot_tn(ws[1].astype(BF16), dhs[1]))
            return [c_l + jnp.sum(l, axis=1, keepdims=True) for c_l, l in zip(c_ls, ls_)]

        zc, za = jnp.zeros((tq, 1), F32), jnp.zeros((tq, PAIR), F32)
        c_ls = lax.cond(i >= 1, lambda: tuple(down(i - 1, down(i, [zc, zc], causal), None)),
                        lambda: tuple(down(i, [zc, zc], causal)))

        def more(st):
            return jnp.logical_and(st[0] <= i, jnp.max(jnp.maximum(st[1], st[2])) > LOG_WEIGHT_FLOOR)

        def down_step(st):
            c_ls = down(i - st[0], [st[1], st[2]], None)
            return (st[0] + 1, c_ls[0], c_ls[1])

        n_tiles = lax.while_loop(more, down_step, (jnp.int32(2), c_ls[0], c_ls[1]))[0]

        def up(kb, c_gs, accs, valid):
            ks = pl.multiple_of(kb * tq, tq)
            kt = k_ref[pl.ds(ks, tq), :]
            gs = [g_s[hh, kb] for hh in range(2)]
            pres = [_dot(g.astype(BF16), upp) + c_g for g, c_g in zip(gs, c_gs)]
            dzs = []
            for hh in range(2):
                beta = b_s[hh, kb]
                dz = gs[hh] - beta * (gs[hh] + pres[hh])
                if valid is not None:
                    dz = jnp.where(valid, dz, 0.0)
                dzs.append(dz.astype(BF16))
            new_a = [acc + _dot(dzb, kt) for acc, dzb in zip(accs, dzs)]
            dk_ref[pl.ds(ks, tq), :] += _dot_tn(dzs[0], qhs[0]) + _dot_tn(dzs[1], qhs[1])
            new_c = [c_g + jnp.sum(g, axis=1, keepdims=True) for c_g, g in zip(c_gs, gs)]
            return new_c, new_a

        def up_step(kb, st):
            c_gs, accs = up(kb, [st[0], st[1]], [st[2], st[3]], None)
            return (c_gs[0], c_gs[1], accs[0], accs[1])

        st = lax.fori_loop(i - n_tiles + 1, i - 1, up_step, (zc, zc, za, za))

        def last_two():
            c_gs, accs = up(i - 1, [st[0], st[1]], [st[2], st[3]], None)
            return tuple(up(i, c_gs, accs, causal)[1])

        accs = lax.cond(i >= 1, last_two, lambda: tuple(up(i, [zc, zc], [za, za], causal)[1]))
        dq_ref[...] = jnp.where(m0, accs[0], accs[1]) * Q_SCALE

        @pl.when(jnp.logical_and(p == N_PAIRS - 1, i == nq - 1))
        def _():
            rider.last(ride)

    return rider.call(
        body, [q, k, v, don, o_raw, ga], grid=(N_PAIRS, nq), name="attn_bwd",
        in_specs=[pl.BlockSpec((tq, PAIR), lambda p, i: (i, p)),
                  pl.BlockSpec((S, PAIR), lambda p, i: (0, p)),
                  pl.BlockSpec((S, PAIR), lambda p, i: (0, p)),
                  pl.BlockSpec((tq, PAIR), lambda p, i: (i, p)),
                  pl.BlockSpec((tq, PAIR), lambda p, i: (i, p)),
                  pl.BlockSpec((1, PAIR), lambda p, i: (0, p))],
        out_specs=[pl.BlockSpec((tq, PAIR), lambda p, i: (i, p)),
                   pl.BlockSpec((S, PAIR), lambda p, i: (0, p)),
                   pl.BlockSpec((S, PAIR), lambda p, i: (0, p)),
                   pl.BlockSpec((1, PAIR), lambda p, i: (0, p))],
        out_shape=[_sds((S, ATTN_WIDTH), F32), _sds((S, ATTN_WIDTH), F32), _sds((S, ATTN_WIDTH), F32),
                   _sds((1, ATTN_WIDTH), F32)],
        scratch_shapes=[pltpu.VMEM((2, nq, tq, tq), F32), pltpu.VMEM((2, nq, tq, tq), F32)],
        vmem_mib=56)


def _inproj_bwd(dq, dk, dv, du, dpre1, xh0, rstd0, g0, b0, w_in_s, tm):
    S, D = xh0.shape

    def body(dq_ref, dk_ref, dv_ref, du_ref, dp1_ref, xh_ref, rstd_ref, g_ref, b_ref, w_ref,
             gx_ref, dw_ref, dg_ref, db_ref):
        @pl.when(pl.program_id(0) == 0)
        def _():
            for r in (dw_ref, dg_ref, db_ref):
                r[...] = jnp.zeros_like(r)

        xh = xh_ref[...]
        xb = (xh * g_ref[...] + b_ref[...]).astype(BF16)
        dx0 = ALPHA * dp1_ref[...]
        for c, r in enumerate((dq_ref, dk_ref, dv_ref, du_ref)):
            dpb = r[...].astype(BF16)
            dx0 = dx0 + _dot_nt(dpb, w_ref[c])
            dw_ref[c] += _dot_tn(xb, dpb)
        dg_ref[...] += _colsum(dx0 * xh)
        db_ref[...] += _colsum(dx0)
        gx_ref[...] = _ln_bwd(dx0, xh, rstd_ref[...], g_ref[...])

    vec = _const_spec((1, D))
    half = _row_spec(tm, 512)
    return pl.pallas_call(
        body, grid=(S // tm,), name="inproj_bwd",
        in_specs=[half, half, half, half, _row_spec(tm, D), _row_spec(tm, D), _row_spec(tm, 1), vec, vec,
                  _const_spec((N_CHIPS, D, 512))],
        out_specs=[_row_spec(tm, D), _const_spec((N_CHIPS, D, 512)), vec, vec],
        out_shape=[_sds((S, D), F32), _sds((N_CHIPS, D, 512), F32), _sds((1, D), F32), _sds((1, D), F32)],
        compiler_params=_cp(56),
    )(*_hbm(dq, dk, dv, du, dpre1, xh0, rstd0, g0, b0, w_in_s))


def _place():
    return lax.axis_index("x"), lax.axis_index("y"), lax.axis_index("c")


CHIP_FLIPS = ((0, 1), (1, 0), (1, 1))


class _Rider:
    def __init__(self, ins, out_shapes, n_sem, phases, aliases=None):
        self.ins, self.out_shapes, self.n_sem, self.phases = list(ins), list(out_shapes), n_sem, phases
        self.aliases = aliases or {}

    def __add__(self, other):
        na, ma = len(self.ins), len(self.out_shapes)

        def phases(ins, outs, ssem, rsem):
            mine = self.phases(ins[:na], outs[:ma], ssem, rsem)
            rest = pl.ds(self.n_sem, other.n_sem)
            theirs = other.phases(ins[na:], outs[ma:], ssem.at[rest], rsem.at[rest])
            assert len(mine) == 1 and len(theirs) == 1
            return [mine[0] + theirs[0]]

        aliases = {**self.aliases, **{na + i: ma + o for i, o in other.aliases.items()}}
        return _Rider(self.ins + other.ins, self.out_shapes + other.out_shapes, self.n_sem + other.n_sem, phases,
                      aliases)

    def split(self, refs, n_in, n_out, n_scratch):
        a = n_in + len(self.ins)
        b = a + n_out
        c = b + len(self.out_shapes)
        own = (refs[:n_in], refs[a:b], refs[c:c + n_scratch])
        return own + ((refs[n_in:a], refs[b:c]) + tuple(refs[c + n_scratch:]),)

    def first(self, ride):
        for make in self.phases(*ride)[0]:
            make().start()

    def mid(self, ride):
        ph = self.phases(*ride)
        if len(ph) == 2:
            for make in ph[0]:
                make().wait_recv()
            for make in ph[1]:
                make().start()

    def last(self, ride):
        ph = self.phases(*ride)
        if len(ph) == 2:
            for make in ph[0]:
                make().wait_send()
        for make in ph[-1]:
            make().wait()

    def call(self, body, args, *, grid, name, in_specs, out_specs, out_shape, scratch_shapes, vmem_mib,
             prefetch=None):
        n_in, n_out = len(in_specs), len(out_specs)
        sems = [pltpu.SemaphoreType.DMA((self.n_sem,)), pltpu.SemaphoreType.DMA((self.n_sem,))]
        n_pre = 0 if prefetch is None else 1
        grid_spec = pltpu.PrefetchScalarGridSpec(
            num_scalar_prefetch=n_pre, grid=grid,
            in_specs=list(in_specs) + [HBM_SPEC] * len(self.ins),
            out_specs=list(out_specs) + [HBM_SPEC] * len(self.out_shapes),
            scratch_shapes=list(scratch_shapes) + sems)
        return pl.pallas_call(
            body, name=name, grid_spec=grid_spec,
            out_shape=list(out_shape) + self.out_shapes,
            input_output_aliases={n_pre + n_in + i: n_out + o for i, o in self.aliases.items()},
            compiler_params=_cp(vmem_mib),
        )(*([] if prefetch is None else [prefetch]), *_hbm(*args), *self.ins)

    def run(self, name):
        def body(*refs):
            ride = self.split(refs, 0, 0, 0)[3]
            self.first(ride)
            self.mid(ride)
            self.last(ride)

        return self.call(body, [], grid=(), name=name, in_specs=[], out_specs=[], out_shape=[],
                         scratch_shapes=[], vmem_mib=16)


def _remote(src, dst, ssem, rsem, n, dev):
    return functools.partial(pltpu.make_async_remote_copy, src_ref=src, dst_ref=dst, send_sem=ssem.at[n],
                             recv_sem=rsem.at[n], device_id=dev, device_id_type=MESH)


def _cast_into_slot(w, place, name):
    R, C = w.shape
    tr = min(R, 512)

    def body(pl_ref, w_ref, o_ref):
        o_ref[0] = w_ref[...].astype(BF16)

    return pl.pallas_call(
        body, name=name,
        grid_spec=pltpu.PrefetchScalarGridSpec(
            num_scalar_prefetch=1, grid=(R // tr,),
            in_specs=[pl.BlockSpec((tr, C), lambda r, pr: (r, 0))],
            out_specs=pl.BlockSpec((1, tr, C), lambda r, pr: (pr[1], r, 0))),
        out_shape=_sds((N_CHIPS, R, C), BF16),
    )(place, w)


CAST_STEPS = 8


def _cast_rest(ws, place, rider):
    n = len(ws)

    def body(pl_ref, *refs):
        w_refs, o_refs, _, ride = rider.split(refs, n, n, 0)
        r = pl.program_id(0)

        @pl.when(r == 0)
        def _():
            rider.first(ride)

        @pl.when(r == CAST_STEPS // 2)
        def _():
            rider.mid(ride)

        for w_ref, o_ref in zip(w_refs, o_refs):
            o_ref[0] = w_ref[...].astype(BF16)

        @pl.when(r == CAST_STEPS - 1)
        def _():
            rider.last(ride)

    def rows(w):
        return w.shape[0] // CAST_STEPS

    return rider.call(
        body, ws, grid=(CAST_STEPS,), name="cast_weights", prefetch=place,
        in_specs=[pl.BlockSpec((rows(w), w.shape[1]), lambda r, pr: (r, 0)) for w in ws],
        out_specs=[pl.BlockSpec((1, rows(w), w.shape[1]), lambda r, pr: (pr[1], r, 0)) for w in ws],
        out_shape=[_sds((N_CHIPS,) + w.shape, BF16) for w in ws], scratch_shapes=[], vmem_mib=32)


def _gather_rider(stacked, part="both"):
    n, nf = len(stacked), len(CHIP_FLIPS)

    def phases(ins, outs, ssem, rsem):
        x, y, c = _place()
        slot = 2 * x + y
        ici, d2d = [], []
        for w, (i_ref, o_ref) in enumerate(zip(ins, outs)):
            hh = o_ref.shape[1] // 2
            rows = pl.ds(c * hh, hh)
            for f, (fx, fy) in enumerate(CHIP_FLIPS):
                k = w * nf + f
                theirs = 2 * (x ^ fx) + (y ^ fy)
                if part != "pair":
                    ici.append(_remote(i_ref.at[slot, rows], o_ref.at[slot, rows], ssem, rsem, k,
                                       (x ^ fx, y ^ fy, c)))
                if part != "chips":
                    d2d.append(_remote(o_ref.at[theirs, rows], o_ref.at[theirs, rows], ssem, rsem,
                                       (n * nf if part == "both" else 0) + k, (x, y, 1 - c)))
        return [ph for ph in (ici, d2d) if ph]

    return _Rider(stacked, [_sds(s.shape, s.dtype) for s in stacked], (2 if part == "both" else 1) * n * nf,
                  phases, aliases={i: i for i in range(n)})


def _pair_swap_rider(grads):
    def phases(ins, outs, ssem, rsem):
        x, y, c = _place()
        return [[_remote(g.at[:, 1 - c], o, ssem, rsem, k, (x, y, 1 - c))
                 for k, (g, o) in enumerate(zip(ins, outs))]]

    return _Rider(grads, [_sds((N_CHIPS,) + g.shape[2:], g.dtype) for g in grads], len(grads), phases)


def _chip_scatter_rider(parts):
    nf = len(CHIP_FLIPS)

    def phases(ins, outs, ssem, rsem):
        x, y, c = _place()
        return [[_remote(r.at[2 * (x ^ fx) + (y ^ fy)], o.at[f], ssem, rsem, w * nf + f, (x ^ fx, y ^ fy, c))
                 for w, (r, o) in enumerate(zip(ins, outs)) for f, (fx, fy) in enumerate(CHIP_FLIPS)]]

    return _Rider(parts, [_sds((nf,) + r.shape[1:], r.dtype) for r in parts], len(parts) * nf, phases)


def _pair_send_rider(halves):
    def phases(ins, outs, ssem, rsem):
        x, y, c = _place()
        return [[_remote(h, o, ssem, rsem, k, (x, y, 1 - c)) for k, (h, o) in enumerate(zip(ins, outs))]]

    return _Rider(halves, [_sds(h.shape, h.dtype) for h in halves], len(halves), phases)


PAIR_SUM_STEPS = 2
CHIP_SUM_STEPS = 4
ADAMW_STEPS = 4


def _no_rider():
    return _Rider([], [], 1, lambda ins, outs, ssem, rsem: [[]])


def _add_pair(grads, recvs, place, name, rider):
    n = len(grads)

    def body(pl_ref, *refs):
        ins, outs, _, ride = rider.split(refs, 2 * n, 2 * n, 0)
        j, h = pl.program_id(0), pl.program_id(1)

        @pl.when(jnp.logical_and(j == 0, h == 0))
        def _():
            rider.first(ride)

        for w in range(n):
            s = ins[2 * w][:, 0] + ins[2 * w + 1][...]
            outs[2 * w][...] = s
            outs[2 * w + 1][...] = s.astype(BF16)

        @pl.when(jnp.logical_and(j == N_CHIPS - 1, h == PAIR_SUM_STEPS - 1))
        def _():
            rider.last(ride)

    in_specs, out_specs, out_shape, args = [], [], [], []
    for g, r in zip(grads, recvs):
        _, _, H, C = g.shape
        th = H // PAIR_SUM_STEPS
        spec = pl.BlockSpec((1, th, C), lambda j, h, pr: (j, h, 0))
        in_specs += [pl.BlockSpec((1, 1, th, C), lambda j, h, pr: (j, pr[0], h, 0)), spec]
        out_specs += [spec, spec]
        out_shape += [_sds((N_CHIPS, H, C), F32), _sds((N_CHIPS, H, C), BF16)]
        args += [g, r]
    res = rider.call(body, args, grid=(N_CHIPS, PAIR_SUM_STEPS), name=name, prefetch=place, in_specs=in_specs,
                     out_specs=out_specs, out_shape=out_shape, scratch_shapes=[], vmem_mib=32)
    return [(res[2 * w], res[2 * w + 1]) for w in range(n)], res[2 * n:]


def _add_chips(parts, recvs, place, name, rider):
    n = len(parts)

    def body(pl_ref, *refs):
        ins, outs, _, ride = rider.split(refs, 2 * n, n, 0)
        h = pl.program_id(0)

        @pl.when(h == 0)
        def _():
            rider.first(ride)

        for w in range(n):
            p_ref, r_ref = ins[2 * w], ins[2 * w + 1]
            outs[w][...] = p_ref[0] + r_ref[0].astype(F32) + r_ref[1].astype(F32) + r_ref[2].astype(F32)

        @pl.when(h == CHIP_SUM_STEPS - 1)
        def _():
            rider.last(ride)

    in_specs, out_specs, out_shape, args = [], [], [], []
    for p, r in zip(parts, recvs):
        _, H, C = p.shape
        th = H // CHIP_SUM_STEPS
        in_specs += [pl.BlockSpec((1, th, C), lambda h, pr: (pr[1], h, 0)),
                     pl.BlockSpec((len(CHIP_FLIPS), th, C), lambda h, pr: (0, h, 0))]
        out_specs.append(pl.BlockSpec((th, C), lambda h, pr: (h, 0)))
        out_shape.append(_sds((H, C), F32))
        args += [p, r]
    res = rider.call(body, args, grid=(CHIP_SUM_STEPS,), name=name, prefetch=place, in_specs=in_specs,
                     out_specs=out_specs, out_shape=out_shape, scratch_shapes=[], vmem_mib=32)
    return res[:n], res[n:]


def _adamw_math(w, g, m, v):
    m = ADAM_B1 * m + (1.0 - ADAM_B1) * g
    v = ADAM_B2 * v + (1.0 - ADAM_B2) * (g * g)
    m_hat = m / (1.0 - ADAM_B1 ** ADAM_STEP)
    v_hat = v / (1.0 - ADAM_B2 ** ADAM_STEP)
    delta = -ADAM_LR * (m_hat / (jnp.sqrt(v_hat) + ADAM_EPS) + ADAM_WD * w)
    return delta, m, v


def _adamw(ws, mines, theirs, ms, vs, place, name, rider):
    n = len(ws)

    def body(pl_ref, *refs):
        ins, outs, _, ride = rider.split(refs, 5 * n, 4 * n, 0)
        h, r = pl.program_id(0), pl.program_id(1)

        @pl.when(jnp.logical_and(h == 0, r == 0))
        def _():
            rider.first(ride)

        for k in range(n):
            w_ref, a_ref, b_ref, m_ref, v_ref = ins[5 * k:5 * k + 5]
            g = jnp.where(h == pl_ref[0], a_ref[...], b_ref[...])
            d, mo, vo = _adamw_math(w_ref[...], g, m_ref[...], v_ref[...])
            for o_ref, val in zip(outs[4 * k:4 * k + 4], (g, d, mo, vo)):
                o_ref[...] = val

        @pl.when(jnp.logical_and(h == 1, r == ADAMW_STEPS - 1))
        def _():
            rider.last(ride)

    in_specs, out_specs, out_shape, args = [], [], [], []
    for w, a, b, m, v in zip(ws, mines, theirs, ms, vs):
        R, C = w.shape
        th = (R // 2) // ADAMW_STEPS
        whole = pl.BlockSpec((th, C), lambda h, r, pr: (h * ADAMW_STEPS + r, 0))
        mine_spec = pl.BlockSpec((th, C), lambda h, r, pr: (jnp.where(h == pr[0], r, 0), 0))
        theirs_spec = pl.BlockSpec((th, C), lambda h, r, pr: (jnp.where(h == pr[0], 0, r), 0))
        in_specs += [whole, mine_spec, theirs_spec, whole, whole]
        out_specs += [whole] * 4
        out_shape += [_sds((R, C), F32)] * 4
        args += [w, a, b, m, v]
    res = rider.call(body, args, grid=(2, ADAMW_STEPS), name=name, prefetch=place, in_specs=in_specs,
                     out_specs=out_specs, out_shape=out_shape, scratch_shapes=[], vmem_mib=40)
    return [tuple(res[4 * k:4 * k + 4]) for k in range(n)], res[4 * n:]


DEVICE_FLIPS = tuple((fx, fy, fc) for fx in (0, 1) for fy in (0, 1) for fc in (0, 1))[1:]


def _pack_exchange_rider(pack):
    def phases(ins, outs, ssem, rsem):
        x, y, c = _place()
        mine = outs[0].at[4 * x + 2 * y + c]
        copies = [_remote(ins[0], mine, ssem, rsem, k, (x ^ fx, y ^ fy, c ^ fc))
                  for k, (fx, fy, fc) in enumerate(DEVICE_FLIPS)]
        copies.append(functools.partial(pltpu.make_async_copy, ins[0], mine, ssem.at[len(DEVICE_FLIPS)]))
        return [copies]

    return _Rider([pack], [_sds((N_DEV,) + pack.shape, pack.dtype)], len(DEVICE_FLIPS) + 1, phases)


def _small_sum_adamw(recv_a, recv_b, wpack, mpack, vpack):
    R = wpack.shape[0]

    def body(a_ref, b_ref, w_ref, m_ref, v_ref, gs_ref, d_ref, mo_ref, vo_ref):
        ta, tb = a_ref[0], b_ref[0]
        for dev in range(1, N_DEV):
            ta = ta + a_ref[dev]
            tb = tb + b_ref[dev]
        total = jnp.concatenate([ta, tb], axis=0)
        gs_ref[...] = total
        d, mo, vo = _adamw_math(w_ref[...], total, m_ref[...], v_ref[...])
        d_ref[...] = d
        mo_ref[...] = mo
        vo_ref[...] = vo

    return pl.pallas_call(
        body, name="small_sum_adamw", in_specs=[VMEM_SPEC] * 5, out_specs=[VMEM_SPEC] * 4,
        out_shape=[_sds((R, LANES), F32)] * 4,
    )(recv_a, recv_b, wpack, mpack, vpack)


def _rows8(a):
    a = a.reshape(-1, LANES)
    pad = (-a.shape[0]) % 8
    return jnp.pad(a, ((0, pad), (0, 0))) if pad else a


def _pack(parts):
    return jnp.concatenate([_rows8(a) for a in parts], axis=0)


def _unpack(pack, like):
    out, row = [], 0
    for a in like:
        n = a.size // LANES
        out.append(pack[row:row + n].reshape(a.shape))
        row += n + (-n) % 8
    return out


def kernel(x, p, emb_ln_g, emb_ln_b, w_in, attn_out_g, w_pool, pool_scale, w_out, ln1_g, ln1_b, w_up, w_down, ln2_g, ln2_b, w_ple, w_ple_gate, ln3_g, ln3_b, loss_target, m_emb_ln_g, m_emb_ln_b, m_w_in, m_attn_out_g, m_w_pool, m_pool_scale, m_w_out, m_ln1_g, m_ln1_b, m_w_up, m_w_down, m_ln2_g, m_ln2_b, m_w_ple, m_w_ple_gate, m_ln3_g, m_ln3_b, v_emb_ln_g, v_emb_ln_b, v_w_in, v_attn_out_g, v_w_pool, v_pool_scale, v_w_out, v_ln1_g, v_ln1_b, v_w_up, v_w_down, v_ln2_g, v_ln2_b, v_w_ple, v_w_ple_gate, v_ln3_g, v_ln3_b):
    S = x.shape[1]
    tm = min(256, S)
    tq = min(256, S)
    tm_mlp = min(1024, S)
    tm_pool = min(1024, S)
    xs = x[0]
    ps = p[0, 0]
    tgt = loss_target[0]
    row = lambda a: a.reshape(1, -1)
    g0, b0 = row(emb_ln_g), row(emb_ln_b)
    g1, b1, g2, b2, g3, b3 = ln1_g, ln1_b, ln2_g, ln2_b, ln3_g, ln3_b
    wp = w_pool[0]

    xi, yi, ci = _place()
    place = jnp.stack([ci, 2 * xi + yi]).astype(jnp.int32)
    names = ["w_in", "w_out", "w_up", "w_down", "w_ple", "w_ple_gate"]

    big = [w_in[0], w_out[0], w_up[0], w_down[0], w_ple[0], w_ple_gate[0]]
    s_in = _cast_into_slot(big[0], place, "cast_w_in")
    s_out, s_up, s_down, s_ple, s_gate, w_in_s = _cast_rest(big[1:], place, _gather_rider([s_in]))

    xh0, rstd0, q, k, v, u, s_out, s_ple, s_gate = _embln_inproj(
        xs, g0, b0, w_in_s, tm, _gather_rider([s_out, s_ple, s_gate], "chips"))
    o_raw, on, s_up, s_down, w_out_s, w_ple_s, w_gate_s = _attn_fwd(
        q, k, v, attn_out_g, tq, _gather_rider([s_up, s_down], "chips") + _gather_rider([s_out, s_ple, s_gate], "pair"))
    w_out_f = w_out_s.reshape(D_MODEL, D_MODEL)
    w_gate_f = w_gate_s.reshape(D_MODEL, D_MODEL)
    d_b, pooled = _pool_fwd(u, wp, pool_scale, tm_pool)
    xh1, rstd1, x1b, w_up_s, w_down_s = _mix_ln1(on, pooled, xh0, g0, b0, w_out_f, g1, b1, tm,
                                                 _gather_rider([s_up, s_down], "pair"))
    w_down_f = w_down_s.reshape(D_FF, D_MODEL)
    xh2, rstd2, rb = _mlp_ln2(xh1, x1b, g1, b1, w_up_s, w_down_f, tm_mlp)

    (dpre2, dhb, dw_ple, dw_gate, dg3, db3, dg2, db2, loss_row) = _ple_ln3_loss(
        xh2, rstd2, g2, b2, ps, w_ple_s, w_gate_f, g3, b3, tgt, tm)
    dx1m, da = _mlp_bwd(rb, dhb, w_up_s, w_down_f, tm_mlp)
    dw_up = _tn_matmul(x1b, da, "grad_w_up", 1024, min(512, S), stacked=True)
    dw_down = _tn_matmul(rb, dhb, "grad_w_down", 1024, min(512, S), stacked=False, square_a=True)
    def halves_of(g):
        return g.reshape(N_CHIPS, 2, g.shape[1] // 2, g.shape[2])

    early = [halves_of(g) for g in (dw_up, dw_down.reshape(N_CHIPS, D_FF // N_CHIPS, D_MODEL), dw_ple,
                                    dw_gate.reshape(N_CHIPS, D_MODEL // N_CHIPS, D_MODEL))]
    dpre1, don, dpooled, dw_out, dg1, db1, *early_pair = _mix_bwd(
        dpre2, dx1m, xh1, rstd1, g1, w_out_f, on, pooled, tm, _pair_swap_rider(early))
    early_sum, _ = _add_pair(early, early_pair, place, "pair_sum_mlp_ple", _no_rider())
    out_halves = halves_of(dw_out.reshape(N_CHIPS, D_MODEL // N_CHIPS, D_MODEL))
    du, dwp, dsc, out_pair = _pool_bwd(dpooled, d_b, wp, pool_scale, tm_pool, _pair_swap_rider([out_halves]))
    (out_sum,), _ = _add_pair([out_halves], [out_pair], place, "pair_sum_w_out", _no_rider())
    pack_a = _pack([jnp.broadcast_to(loss_row, (8, LANES)), dwp, dsc, dg1, db1, dg2, db2, dg3, db3])
    early_sum = [out_sum] + early_sum
    riding = _chip_scatter_rider([b for _, b in early_sum]) + _pack_exchange_rider(pack_a)
    dq, dk, dv, dga, *arrived = _attn_bwd(q, k, v, don, o_raw, attn_out_g, tq, riding)
    early_chips, recv_a = arrived[:-1], arrived[-1]
    grad_x, dw_in, dg0, db0 = _inproj_bwd(dq, dk, dv, du, dpre1, xh0, rstd0, g0, b0, w_in_s, tm)

    in_halves = halves_of(dw_in)
    pack_b = _pack([dg0, db0, dga])
    early_mine, (in_pair, recv_b) = _add_chips(
        [s for s, _ in early_sum], early_chips, place, "chip_sum_early",
        _pair_swap_rider([in_halves]) + _pack_exchange_rider(pack_b))
    (in_sum,), early_theirs = _add_pair([in_halves], [in_pair], place, "pair_sum_w_in", _pair_send_rider(early_mine))
    ms = [m_w_in, m_w_out, m_w_up, m_w_down, m_w_ple, m_w_ple_gate]
    vs = [v_w_in, v_w_out, v_w_up, v_w_down, v_w_ple, v_w_ple_gate]
    early_res, _ = _adamw(big[1:], early_mine, early_theirs, [m[0] for m in ms[1:]], [v[0] for v in vs[1:]],
                          place, "adamw_early", _no_rider())
    (in_chips,) = _chip_scatter_rider([in_sum[1]]).run("reduce_chips_late")
    (in_mine,), _ = _add_chips([in_sum[0]], [in_chips], place, "chip_sum_w_in", _no_rider())
    (in_theirs,) = _pair_send_rider([in_mine]).run("gather_pair_w_in")
    in_res, _ = _adamw(big[:1], [in_mine], [in_theirs], [ms[0][0]], [vs[0][0]], place, "adamw_w_in", _no_rider())
    big_out = {n: tuple(r.reshape(m.shape) for r in res4) for n, res4, m in zip(names, in_res + early_res, ms)}

    small_names = ["w_pool", "pool_scale", "ln1_g", "ln1_b", "ln2_g", "ln2_b", "ln3_g", "ln3_b",
                   "emb_ln_g", "emb_ln_b", "attn_out_g"]
    small_w = [w_pool, pool_scale, ln1_g, ln1_b, ln2_g, ln2_b, ln3_g, ln3_b, emb_ln_g, emb_ln_b, attn_out_g]
    small_m = [m_w_pool, m_pool_scale, m_ln1_g, m_ln1_b, m_ln2_g, m_ln2_b, m_ln3_g, m_ln3_b,
               m_emb_ln_g, m_emb_ln_b, m_attn_out_g]
    small_v = [v_w_pool, v_pool_scale, v_ln1_g, v_ln1_b, v_ln2_g, v_ln2_b, v_ln3_g, v_ln3_b,
               v_emb_ln_g, v_emb_ln_b, v_attn_out_g]
    loss_like = jnp.zeros((8, LANES), F32)
    gs, ds, mos, vos = _small_sum_adamw(recv_a, recv_b, _pack([loss_like] + small_w), _pack([loss_like] + small_m),
                                        _pack([jnp.ones((8, LANES), F32)] + small_v))
    like = [loss_like] + small_w
    gs_u, ds_u, mos_u, vos_u = (_unpack(a, like) for a in (gs, ds, mos, vos))
    loss = gs_u[0][0, 0]
    small_out = {n: (gs_u[i + 1], ds_u[i + 1], mos_u[i + 1], vos_u[i + 1]) for i, n in enumerate(small_names)}

    order = ["emb_ln_g", "emb_ln_b", "w_in", "attn_out_g", "w_pool", "pool_scale", "w_out", "ln1_g", "ln1_b",
             "w_up", "w_down", "ln2_g", "ln2_b", "w_ple", "w_ple_gate", "ln3_g", "ln3_b"]
    res = {**big_out, **small_out}
    outs = [loss, grad_x.reshape(x.shape)]
    for kind in range(4):
        outs += [res[n][kind] for n in order]
    return tuple(outs)
```

```python
import functools

import jax
import jax.numpy as jnp
from jax import lax
from jax.experimental import pallas as pl
from jax.experimental.pallas import tpu as pltpu

F32 = jnp.float32
BF16 = jnp.bfloat16

D_MODEL = 1024
ATTN_WIDTH = 512
POOL_WIDTH = 512
HEAD_DIM = 64
PAIR = 2 * HEAD_DIM
N_PAIRS = ATTN_WIDTH // PAIR
N_POOL_GROUPS = 4
POOL_GROUP = 128
POOL_HALO = 16
D_FF = 4096
PLE_DIM = 256
N_CHIPS = 4
N_DEV = 8
LN_EPS = 1e-5
RMS_EPS = 1e-6
ALPHA = float(2.0 ** 0.25)
Q_SCALE = 0.125
ADAM_LR = 0.001
ADAM_B1 = 0.9
ADAM_B2 = 0.999
ADAM_EPS = 1e-08
ADAM_WD = 0.01
ADAM_STEP = 10
LANES = 128
MIB = 1024 * 1024

MESH = pl.DeviceIdType.MESH
HBM_SPEC = pl.BlockSpec(memory_space=pltpu.HBM)
VMEM_SPEC = pl.BlockSpec(memory_space=pltpu.VMEM)


def _cp(vmem_mib):
    return pltpu.CompilerParams(vmem_limit_bytes=vmem_mib * MIB)


def _dot(a, b):
    return jnp.dot(a, b, preferred_element_type=F32)


def _dot_nt(a, b):
    return lax.dot_general(a, b, (((1,), (1,)), ((), ())), preferred_element_type=F32)


def _dot_tn(a, b):
    return lax.dot_general(a, b, (((0,), (0,)), ((), ())), preferred_element_type=F32)


def _ln_fwd(pre):
    mu = jnp.mean(pre, axis=-1, keepdims=True)
    xc = pre - mu
    var = jnp.mean(xc * xc, axis=-1, keepdims=True)
    rstd = lax.rsqrt(var + LN_EPS)
    return xc * rstd, rstd


def _ln_bwd(dy, xh, rstd, g):
    dxh = dy * g
    m1 = jnp.mean(dxh, axis=-1, keepdims=True)
    m2 = jnp.mean(dxh * xh, axis=-1, keepdims=True)
    return rstd * (dxh - m1 - xh * m2)


def _colsum(a):
    return jnp.sum(a, axis=0, keepdims=True)


def _neg_softplus(z):
    return -(jnp.maximum(z, 0.0) + jnp.log(1.0 + jnp.exp(-jnp.abs(z))))


def _split_bf16(a):
    hi = a.astype(BF16)
    lo = (a - hi.astype(F32)).astype(BF16)
    return hi, lo


def _row_spec(tm, n):
    return pl.BlockSpec((tm, n), lambda i: (i, 0))


def _const_spec(shape):
    nd = len(shape)
    return pl.BlockSpec(shape, lambda *_: (0,) * nd)


def _hbm(*arrays):
    return [pltpu.with_memory_space_constraint(a, pltpu.HBM) for a in arrays]


def _sds(shape, dtype):
    return pltpu.HBM(shape, dtype)


def _embln_inproj(x, g0, b0, w_in_s, tm, rider):
    S, D = x.shape
    n_t = S // tm

    def body(*refs):
        ((x_ref, g_ref, b_ref, w_ref), (xh_ref, rstd_ref, q_ref, k_ref, v_ref, u_ref), _,
         ride) = rider.split(refs, 4, 6, 0)
        i = pl.program_id(0)

        @pl.when(i == 0)
        def _():
            rider.first(ride)

        @pl.when(i == (3 * n_t) // 4)
        def _():
            rider.mid(ride)

        xh, rstd = _ln_fwd(x_ref[...])
        xh_ref[...] = xh
        rstd_ref[...] = rstd
        xb = (xh * g_ref[...] + b_ref[...]).astype(BF16)
        q_ref[...] = (_dot(xb, w_ref[0]) * Q_SCALE).astype(BF16)
        k_ref[...] = _dot(xb, w_ref[1]).astype(BF16)
        v_ref[...] = _dot(xb, w_ref[2]).astype(BF16)
        u_ref[...] = _dot(xb, w_ref[3])

        @pl.when(i == n_t - 1)
        def _():
            rider.last(ride)

    return rider.call(
        body, [x, g0, b0, w_in_s], grid=(n_t,), name="embln_inproj",
        in_specs=[_row_spec(tm, D), _const_spec((1, D)), _const_spec((1, D)),
                  _const_spec((N_CHIPS, D, 512))],
        out_specs=[_row_spec(tm, D), _row_spec(tm, 1), _row_spec(tm, 512), _row_spec(tm, 512),
                   _row_spec(tm, 512), _row_spec(tm, 512)],
        out_shape=[_sds((S, D), F32), _sds((S, 1), F32), _sds((S, 512), BF16), _sds((S, 512), BF16),
                   _sds((S, 512), BF16), _sds((S, 512), F32)],
        scratch_shapes=[], vmem_mib=40)


def _tri(n, upper):
    r = lax.broadcasted_iota(jnp.int32, (n, n), 0)
    c = lax.broadcasted_iota(jnp.int32, (n, n), 1)
    keep = (r < c) if upper else (r > c)
    return jnp.where(keep, 1.0, 0.0).astype(BF16)


def _strictly_causal(n):
    return lax.broadcasted_iota(jnp.int32, (n, n), 1) < lax.broadcasted_iota(jnp.int32, (n, n), 0)


LOG_WEIGHT_FLOOR = -110.0


def _sb_tile(qhs, kt, low, c_ls, valid):
    valids = valid if isinstance(valid, (list, tuple)) else [valid] * len(qhs)
    zs = [_dot_nt(qh, kt) for qh in qhs]
    lrs = [_neg_softplus(z) for z in zs]
    ls_ = [lr if m is None else jnp.where(m, lr, 0.0) for lr, m in zip(lrs, valids)]
    sfx = [_dot(l.astype(BF16), low) + c_l for l, c_l in zip(ls_, c_ls)]
    lss = [z + lr for z, lr in zip(zs, lrs)]
    ws = [jnp.exp(ls + s) for ls, s in zip(lss, sfx)]
    ws = [w if m is None else jnp.where(m, w, 0.0) for w, m in zip(ws, valids)]
    return lss, ls_, ws


def _attn_fwd(q, k, v, ga, tq, rider):
    S = q.shape[0]
    nq = S // tq

    def body(*refs):
        (q_ref, k_ref, v_ref, ga_ref), (o_ref, on_ref), _, ride = rider.split(refs, 4, 2, 0)
        p, i = pl.program_id(0), pl.program_id(1)

        @pl.when(jnp.logical_and(p == 0, i == 0))
        def _():
            rider.first(ride)

        @pl.when(jnp.logical_and(p == N_PAIRS - 1, i == 0))
        def _():
            rider.mid(ride)

        lane = lax.broadcasted_iota(jnp.int32, (1, PAIR), 1)
        m0 = lane < HEAD_DIM
        low = _tri(tq, upper=False)
        q2 = q_ref[...]
        qhs = [jnp.where(m0, q2, jnp.zeros_like(q2)), jnp.where(m0, jnp.zeros_like(q2), q2)]

        def tile(kb, c_ls, accs, valid):
            ks = pl.multiple_of(kb * tq, tq)
            kt = k_ref[pl.ds(ks, tq), :]
            vt = v_ref[pl.ds(ks, tq), :]
            _, ls_, ws = _sb_tile(qhs, kt, low, c_ls, valid)
            new_a = [acc + _dot(w.astype(BF16), vt) for acc, w in zip(accs, ws)]
            new_c = [c_l + jnp.sum(l, axis=1, keepdims=True) for c_l, l in zip(c_ls, ls_)]
            return new_c, new_a

        zc, za = jnp.zeros((tq, 1), F32), jnp.zeros((tq, PAIR), F32)

        def first_two():
            c_ls, accs = tile(i, [zc, zc], [za, za], _strictly_causal(tq))
            c_ls, accs = tile(i - 1, c_ls, accs, None)
            return (*c_ls, *accs)

        def first_one():
            c_ls, accs = tile(i, [zc, zc], [za, za], _strictly_causal(tq))
            return (*c_ls, *accs)

        st0 = lax.cond(i >= 1, first_two, first_one)

        def more(st):
            return jnp.logical_and(st[0] <= i, jnp.max(jnp.maximum(st[1], st[2])) > LOG_WEIGHT_FLOOR)

        def step(st):
            n, c0, c1, a0, a1 = st
            c_ls, accs = tile(i - n, [c0, c1], [a0, a1], None)
            return (n + 1, c_ls[0], c_ls[1], accs[0], accs[1])

        st = lax.while_loop(more, step, (jnp.int32(2), *st0))
        o = jnp.where(m0, st[3], st[4])
        o_ref[...] = o
        sq = o * o
        ms0 = jnp.sum(jnp.where(m0, sq, 0.0), axis=-1, keepdims=True) * (1.0 / HEAD_DIM)
        ms1 = jnp.sum(jnp.where(m0, 0.0, sq), axis=-1, keepdims=True) * (1.0 / HEAD_DIM)
        rs = jnp.where(m0, lax.rsqrt(ms0 + RMS_EPS), lax.rsqrt(ms1 + RMS_EPS))
        on_ref[...] = (o * rs * ga_ref[...]).astype(BF16)

        @pl.when(jnp.logical_and(p == N_PAIRS - 1, i == nq - 1))
        def _():
            rider.last(ride)

    return rider.call(
        body, [q, k, v, ga], grid=(N_PAIRS, nq), name="attn_fwd",
        in_specs=[pl.BlockSpec((tq, PAIR), lambda p, i: (i, p)),
                  pl.BlockSpec((S, PAIR), lambda p, i: (0, p)),
                  pl.BlockSpec((S, PAIR), lambda p, i: (0, p)),
                  pl.BlockSpec((1, PAIR), lambda p, i: (0, p))],
        out_specs=[pl.BlockSpec((tq, PAIR), lambda p, i: (i, p)),
                   pl.BlockSpec((tq, PAIR), lambda p, i: (i, p))],
        out_shape=[_sds((S, ATTN_WIDTH), F32), _sds((S, ATTN_WIDTH), BF16)],
        scratch_shapes=[], vmem_mib=40)


def _pool_fwd(u, w_pool, pscale, tm):
    S = u.shape[0]
    hb = tm // POOL_HALO

    def body(u_ref, uh_ref, wp_ref, sc_ref, d_ref, pooled_ref):
        i = pl.program_id(0)
        halo = jnp.where(i > 0, uh_ref[...], 0.0)
        pos = i * tm + lax.broadcasted_iota(jnp.int32, (tm, 1), 0)
        for g in range(N_POOL_GROUPS):
            win = 2 ** (g + 1)
            cols = slice(g * POOL_GROUP, (g + 1) * POOL_GROUP)
            ut = u_ref[:, cols]
            s = jnp.concatenate([halo[:, cols], ut], axis=0)
            for sh in (1, 2, 4, 8)[:g + 1]:
                s = s + pltpu.roll(s, sh, 0)
            cnt = jnp.minimum(pos + 1, win).astype(F32)
            db = (s[POOL_HALO:, :] / cnt - ut).astype(BF16)
            y = _dot(db, wp_ref[g].astype(BF16))
            d_ref[:, cols] = db
            pooled_ref[:, cols] = (y * sc_ref[:, cols]).astype(BF16)

    return pl.pallas_call(
        body, grid=(S // tm,), name="pool_fwd",
        in_specs=[_row_spec(tm, POOL_WIDTH),
                  pl.BlockSpec((POOL_HALO, POOL_WIDTH), lambda i: (jnp.maximum(i * hb - 1, 0), 0)),
                  _const_spec((N_POOL_GROUPS, POOL_GROUP, POOL_GROUP)), _const_spec((1, POOL_WIDTH))],
        out_specs=[_row_spec(tm, POOL_WIDTH), _row_spec(tm, POOL_WIDTH)],
        out_shape=[_sds((S, POOL_WIDTH), BF16), _sds((S, POOL_WIDTH), BF16)],
        compiler_params=_cp(32),
    )(*_hbm(u, u, w_pool, pscale))


def _mix_ln1(on, pooled, xh0, g0, b0, w_out, g1, b1, tm, rider):
    S, D = xh0.shape
    n_t = S // tm

    def body(*refs):
        ((on_ref, po_ref, xh0_ref, g0_ref, b0_ref, w_ref, g1_ref, b1_ref), (xh_ref, rstd_ref, xb_ref), _,
         ride) = rider.split(refs, 8, 3, 0)

        @pl.when(pl.program_id(0) == 0)
        def _():
            rider.first(ride)

        mixed = _dot(on_ref[...], w_ref[:ATTN_WIDTH, :]) + _dot(po_ref[...], w_ref[ATTN_WIDTH:, :])
        x0 = xh0_ref[...] * g0_ref[...] + b0_ref[...]
        xh, rstd = _ln_fwd(ALPHA * x0 + mixed)
        xh_ref[...] = xh
        rstd_ref[...] = rstd
        xb_ref[...] = (xh * g1_ref[...] + b1_ref[...]).astype(BF16)

        @pl.when(pl.program_id(0) == n_t - 1)
        def _():
            rider.last(ride)

    return rider.call(
        body, [on, pooled, xh0, g0, b0, w_out, g1, b1], grid=(n_t,), name="mix_ln1",
        in_specs=[_row_spec(tm, ATTN_WIDTH), _row_spec(tm, POOL_WIDTH), _row_spec(tm, D),
                  _const_spec((1, D)), _const_spec((1, D)), _const_spec((D, D)),
                  _const_spec((1, D)), _const_spec((1, D))],
        out_specs=[_row_spec(tm, D), _row_spec(tm, 1), _row_spec(tm, D)],
        out_shape=[_sds((S, D), F32), _sds((S, 1), F32), _sds((S, D), BF16)],
        scratch_shapes=[], vmem_mib=40)


def _mlp_ln2(xh1, x1b, g1, b1, w_up_s, w_down, tm):
    S, D = xh1.shape
    fc = D_FF // N_CHIPS

    def body(xh_ref, xb_ref, g_ref, b_ref, wu_ref, wd_ref, xh2_ref, rstd_ref, r_ref, acc_ref):
        j = pl.program_id(1)

        @pl.when(j == 0)
        def _():
            acc_ref[...] = jnp.zeros_like(acc_ref)

        r = jnp.maximum(_dot(xb_ref[...], wu_ref[0]), 0.0)
        r_ref[...] = r.astype(BF16)
        acc_ref[...] += _dot((r * r).astype(BF16), wd_ref[...])

        @pl.when(j == N_CHIPS - 1)
        def _():
            x1 = xh_ref[...] * g_ref[...] + b_ref[...]
            xh, rstd = _ln_fwd(ALPHA * x1 + acc_ref[...])
            xh2_ref[...] = xh
            rstd_ref[...] = rstd

    return pl.pallas_call(
        body, grid=(S // tm, N_CHIPS), name="mlp_ln2",
        in_specs=[pl.BlockSpec((tm, D), lambda i, j: (i, 0)), pl.BlockSpec((tm, D), lambda i, j: (i, 0)),
                  pl.BlockSpec((1, D), lambda i, j: (0, 0)), pl.BlockSpec((1, D), lambda i, j: (0, 0)),
                  pl.BlockSpec((1, D, fc), lambda i, j: (j, 0, 0)),
                  pl.BlockSpec((fc, D), lambda i, j: (j, 0))],
        out_specs=[pl.BlockSpec((tm, D), lambda i, j: (i, 0)), pl.BlockSpec((tm, 1), lambda i, j: (i, 0)),
                   pl.BlockSpec((tm, fc), lambda i, j: (i, j))],
        out_shape=[_sds((S, D), F32), _sds((S, 1), F32), _sds((S, D_FF), BF16)],
        scratch_shapes=[pltpu.VMEM((tm, D), F32)],
        compiler_params=_cp(56),
    )(*_hbm(xh1, x1b, g1, b1, w_up_s, w_down))


def _ple_ln3_loss(xh2, rstd2, g2, b2, p, w_ple_s, w_gate, g3, b3, target, tm):
    S, D = xh2.shape
    pc = D // N_CHIPS

    def body(xh2_ref, rstd2_ref, g2_ref, b2_ref, p_ref, wp_ref, wg_ref, g3_ref, b3_ref, t_ref,
             dpre2_ref, dhb_ref, dwp_ref, dwg_ref, dg3_ref, db3_ref, dg2_ref, db2_ref, loss_ref):
        i = pl.program_id(0)

        @pl.when(i == 0)
        def _():
            for r in (dwp_ref, dwg_ref, dg3_ref, db3_ref, dg2_ref, db2_ref, loss_ref):
                r[...] = jnp.zeros_like(r)

        xh2 = xh2_ref[...]
        x2 = xh2 * g2_ref[...] + b2_ref[...]
        x2b = x2.astype(BF16)
        gate = 1.0 / (1.0 + jnp.exp(-_dot(x2b, wg_ref[...])))
        pb = p_ref[...].astype(BF16)
        pe = jnp.concatenate([_dot(pb, wp_ref[c]) for c in range(N_CHIPS)], axis=1)
        xh3, rstd3 = _ln_fwd(ALPHA * x2 + pe * gate)
        diff = xh3 * g3_ref[...] + b3_ref[...] - t_ref[...]
        loss_ref[...] += (0.5 / D) * jnp.sum(diff * diff)
        dy = diff * (1.0 / D)
        dg3_ref[...] += _colsum(dy * xh3)
        db3_ref[...] += _colsum(dy)
        dpre3 = _ln_bwd(dy, xh3, rstd3, g3_ref[...])
        dpe_b = (dpre3 * gate).astype(BF16)
        dgp_b = (dpre3 * pe * gate * (1.0 - gate)).astype(BF16)
        dx2 = ALPHA * dpre3 + _dot_nt(dgp_b, wg_ref[...])
        dwg_ref[...] += _dot_tn(x2b, dgp_b)
        for c in range(N_CHIPS):
            dwp_ref[c] += _dot_tn(pb, dpe_b[:, c * pc:(c + 1) * pc])
        dg2_ref[...] += _colsum(dx2 * xh2)
        db2_ref[...] += _colsum(dx2)
        dpre2 = _ln_bwd(dx2, xh2, rstd2_ref[...], g2_ref[...])
        dpre2_ref[...] = dpre2
        dhb_ref[...] = dpre2.astype(BF16)

    vec = _const_spec((1, D))
    return pl.pallas_call(
        body, grid=(S // tm,), name="ple_ln3_loss",
        in_specs=[_row_spec(tm, D), _row_spec(tm, 1), vec, vec, _row_spec(tm, PLE_DIM),
                  _const_spec((N_CHIPS, PLE_DIM, pc)), _const_spec((D, D)), vec, vec, _row_spec(tm, D)],
        out_specs=[_row_spec(tm, D), _row_spec(tm, D), _const_spec((N_CHIPS, PLE_DIM, pc)),
                   _const_spec((D, D)), vec, vec, vec, vec, _const_spec((1, LANES))],
        out_shape=[_sds((S, D), F32), _sds((S, D), BF16), _sds((N_CHIPS, PLE_DIM, pc), F32),
                   _sds((D, D), F32), _sds((1, D), F32), _sds((1, D), F32), _sds((1, D), F32),
                   _sds((1, D), F32), _sds((1, LANES), F32)],
        compiler_params=_cp(48),
    )(*_hbm(xh2, rstd2, g2, b2, p, w_ple_s, w_gate, g3, b3, target))


def _mlp_bwd(rb, dhb, w_up_s, w_down, tm, rider):
    S, D = dhb.shape
    fc = D_FF // N_CHIPS
    n_t = S // tm

    def body(*refs):
        (r_ref, dh_ref, wu_ref, wd_ref), (dx_ref, da_ref), _, ride = rider.split(refs, 4, 2, 0)
        i, j = pl.program_id(0), pl.program_id(1)

        @pl.when(jnp.logical_and(i == 0, j == 0))
        def _():
            rider.first(ride)

        @pl.when(j == 0)
        def _():
            dx_ref[...] = jnp.zeros_like(dx_ref)

        da = (_dot_nt(dh_ref[...], wd_ref[...]) * (2.0 * r_ref[...].astype(F32))).astype(BF16)
        da_ref[...] = da
        dx_ref[...] += _dot_nt(da, wu_ref[0])

        @pl.when(jnp.logical_and(i == n_t - 1, j == N_CHIPS - 1))
        def _():
            rider.last(ride)

    return rider.call(
        body, [rb, dhb, w_up_s, w_down], grid=(n_t, N_CHIPS), name="mlp_bwd",
        in_specs=[pl.BlockSpec((tm, fc), lambda i, j: (i, j)), pl.BlockSpec((tm, D), lambda i, j: (i, 0)),
                  pl.BlockSpec((1, D, fc), lambda i, j: (j, 0, 0)),
                  pl.BlockSpec((fc, D), lambda i, j: (j, 0))],
        out_specs=[pl.BlockSpec((tm, D), lambda i, j: (i, 0)), pl.BlockSpec((tm, fc), lambda i, j: (i, j))],
        out_shape=[_sds((S, D), F32), _sds((S, D_FF), BF16)],
        scratch_shapes=[], vmem_mib=56)


def _tn_matmul(a, b, name, tk, tt, stacked, rider, square_a=False):
    T, K = a.shape
    N = b.shape[1]
    tn = 1024
    grid = (K // tk, N // tn, T // tt)

    def body(*refs):
        (a_ref, b_ref), (o_ref,), _, ride = rider.split(refs, 2, 1, 0)
        at = [pl.program_id(d) for d in range(3)]

        @pl.when(jnp.logical_and(jnp.logical_and(at[0] == 0, at[1] == 0), at[2] == 0))
        def _():
            rider.first(ride)

        @pl.when(at[2] == 0)
        def _():
            o_ref[...] = jnp.zeros_like(o_ref)

        a_t = a_ref[...]
        if square_a:
            a_t = a_t * a_t
        prod = _dot_tn(a_t, b_ref[...])
        if stacked:
            o_ref[0] += prod
        else:
            o_ref[...] += prod

        @pl.when(jnp.logical_and(jnp.logical_and(at[0] == grid[0] - 1, at[1] == grid[1] - 1),
                                 at[2] == grid[2] - 1))
        def _():
            rider.last(ride)

    if stacked:
        out_spec = pl.BlockSpec((1, tk, tn), lambda k, n, t: (n, k, 0))
        out_shape = _sds((N // tn, K, tn), F32)
    else:
        out_spec = pl.BlockSpec((tk, tn), lambda k, n, t: (k, n))
        out_shape = _sds((K, N), F32)
    return rider.call(
        body, [a, b], grid=grid, name=name,
        in_specs=[pl.BlockSpec((tt, tk), lambda k, n, t: (t, k)),
                  pl.BlockSpec((tt, tn), lambda k, n, t: (t, n))],
        out_specs=[out_spec], out_shape=[out_shape], scratch_shapes=[], vmem_mib=40)


def _mix_bwd(dpre2, dx1m, xh1, rstd1, g1, w_out, on, pooled, tm, rider):
    S, D = xh1.shape
    n_t = S // tm

    def body(*refs):
        ((dp2_ref, dxm_ref, xh_ref, rstd_ref, g_ref, w_ref, on_ref, po_ref),
         (dpre1_ref, don_ref, dpo_ref, dw_ref, dg_ref, db_ref), _, ride) = rider.split(refs, 8, 6, 0)

        @pl.when(pl.program_id(0) == 0)
        def _():
            rider.first(ride)
            for r in (dw_ref, dg_ref, db_ref):
                r[...] = jnp.zeros_like(r)

        xh = xh_ref[...]
        dx1 = ALPHA * dp2_ref[...] + dxm_ref[...]
        dg_ref[...] += _colsum(dx1 * xh)
        db_ref[...] += _colsum(dx1)
        dpre1 = _ln_bwd(dx1, xh, rstd_ref[...], g_ref[...])
        dpre1_ref[...] = dpre1
        dmb = dpre1.astype(BF16)
        dcat = _dot_nt(dmb, w_ref[...])
        don_ref[...] = dcat[:, :ATTN_WIDTH]
        dpo_ref[...] = dcat[:, ATTN_WIDTH:]
        dw_ref[:ATTN_WIDTH, :] += _dot_tn(on_ref[...], dmb)
        dw_ref[ATTN_WIDTH:, :] += _dot_tn(po_ref[...], dmb)

        @pl.when(pl.program_id(0) == n_t - 1)
        def _():
            rider.last(ride)

    vec = _const_spec((1, D))
    return rider.call(
        body, [dpre2, dx1m, xh1, rstd1, g1, w_out, on, pooled], grid=(n_t,), name="mix_bwd",
        in_specs=[_row_spec(tm, D), _row_spec(tm, D), _row_spec(tm, D), _row_spec(tm, 1), vec,
                  _const_spec((D, D)), _row_spec(tm, ATTN_WIDTH), _row_spec(tm, POOL_WIDTH)],
        out_specs=[_row_spec(tm, D), _row_spec(tm, ATTN_WIDTH), _row_spec(tm, POOL_WIDTH),
                   _const_spec((D, D)), vec, vec],
        out_shape=[_sds((S, D), F32), _sds((S, ATTN_WIDTH), F32), _sds((S, POOL_WIDTH), F32),
                   _sds((D, D), F32), _sds((1, D), F32), _sds((1, D), F32)],
        scratch_shapes=[], vmem_mib=48)


def _pool_bwd(dpooled, d_b, w_pool, pscale, tm, rider):
    S = dpooled.shape[0]
    hb = tm // POOL_HALO
    n_t = S // tm
    te = tm + POOL_HALO

    def body(*refs):
        ((dp_ref, dph_ref, d_ref, wp_ref, sc_ref), (du_ref, dwp_ref, dsc_ref), _,
         ride) = rider.split(refs, 5, 3, 0)
        i = pl.program_id(0)

        @pl.when(i == 0)
        def _():
            rider.first(ride)
            dwp_ref[...] = jnp.zeros_like(dwp_ref)
            dsc_ref[...] = jnp.zeros_like(dsc_ref)

        halo = jnp.where(i < n_t - 1, dph_ref[...], 0.0)
        pos = i * tm + lax.broadcasted_iota(jnp.int32, (te, 1), 0)
        for g in range(N_POOL_GROUPS):
            win = 2 ** (g + 1)
            cols = slice(g * POOL_GROUP, (g + 1) * POOL_GROUP)
            wpb = wp_ref[g].astype(BF16)
            dpt = dp_ref[:, cols]
            dpe = jnp.concatenate([dpt, halo[:, cols]], axis=0)
            dyb = (dpe * sc_ref[:, cols]).astype(BF16)
            dd = _dot_nt(dyb, wpb)
            s = dd / jnp.minimum(pos + 1, win).astype(F32)
            for sh in (1, 2, 4, 8)[:g + 1]:
                s = s + pltpu.roll(s, te - sh, 0)
            du_ref[:, cols] = s[:tm, :] - dd[:tm, :]
            db = d_ref[:, cols]
            dwp_ref[g] += _dot_tn(db, dyb[:tm, :])
            dsc_ref[:, cols] += _colsum(dpt * _dot(db, wpb))

        @pl.when(i == n_t - 1)
        def _():
            rider.last(ride)

    return rider.call(
        body, [dpooled, dpooled, d_b, w_pool, pscale], grid=(n_t,), name="pool_bwd",
        in_specs=[_row_spec(tm, POOL_WIDTH),
                  pl.BlockSpec((POOL_HALO, POOL_WIDTH),
                               lambda i: (jnp.minimum((i + 1) * hb, S // POOL_HALO - 1), 0)),
                  _row_spec(tm, POOL_WIDTH),
                  _const_spec((N_POOL_GROUPS, POOL_GROUP, POOL_GROUP)), _const_spec((1, POOL_WIDTH))],
        out_specs=[_row_spec(tm, POOL_WIDTH), _const_spec((N_POOL_GROUPS, POOL_GROUP, POOL_GROUP)),
                   _const_spec((1, POOL_WIDTH))],
        out_shape=[_sds((S, POOL_WIDTH), F32), _sds((N_POOL_GROUPS, POOL_GROUP, POOL_GROUP), F32),
                   _sds((1, POOL_WIDTH), F32)],
        scratch_shapes=[], vmem_mib=32)


def _attn_bwd(q, k, v, don, o_raw, ga, tq, rider):
    S = q.shape[0]
    nq = S // tq

    def body(*refs):
        ((q_ref, k_ref, v_ref, don_ref, o_ref, ga_ref), (dq_ref, dk_ref, dv_ref, dga_ref),
         (g_s, b_s), ride) = rider.split(refs, 6, 4, 2)
        p, i = pl.program_id(0), pl.program_id(1)

        @pl.when(jnp.logical_and(p == 0, i == 0))
        def _():
            rider.first(ride)

        @pl.when(i == 0)
        def _():
            for r in (dk_ref, dv_ref, dga_ref):
                r[...] = jnp.zeros_like(r)

        lane = lax.broadcasted_iota(jnp.int32, (1, PAIR), 1)
        m0 = lane < HEAD_DIM
        low = _tri(tq, upper=False)
        upp = _tri(tq, upper=True)

        def seg_mean(a):
            s0 = jnp.sum(jnp.where(m0, a, 0.0), axis=-1, keepdims=True)
            s1 = jnp.sum(jnp.where(m0, 0.0, a), axis=-1, keepdims=True)
            return jnp.where(m0, s0, s1) * (1.0 / HEAD_DIM)

        o = o_ref[...]
        rs = lax.rsqrt(seg_mean(o * o) + RMS_EPS)
        oh = o * rs
        don = don_ref[...]
        dga_ref[...] += _colsum(don * oh)
        doh = don * ga_ref[...]
        do = rs * (doh - oh * seg_mean(doh * oh))
        dob = do.astype(BF16)
        q2 = q_ref[...]
        qhs = [jnp.where(m0, q2, jnp.zeros_like(q2)), jnp.where(m0, jnp.zeros_like(q2), q2)]
        dhs = [jnp.where(m0, dob, jnp.zeros_like(dob)), jnp.where(m0, jnp.zeros_like(dob), dob)]
        causal = _strictly_causal(tq)

        def down(kb, c_ls, valid):
            ks = pl.multiple_of(kb * tq, tq)
            kt = k_ref[pl.ds(ks, tq), :]
            vt = v_ref[pl.ds(ks, tq), :]
            lss, ls_, ws = _sb_tile(qhs, kt, low, c_ls, valid)
            dws = [_dot_nt(dh, vt) for dh in dhs]
            for hh in range(2):
                g_s[hh, kb] = dws[hh] * ws[hh]
                b_s[hh, kb] = jnp.exp(lss[hh])
            dv_ref[pl.ds(ks, tq), :] += (_dot_tn(ws[0].astype(BF16), dhs[0])
                                         + _dot_tn(ws[1].astype(BF16), dhs[1]))
            return [c_l + jnp.sum(l, axis=1, keepdims=True) for c_l, l in zip(c_ls, ls_)]

        zc, za = jnp.zeros((tq, 1), F32), jnp.zeros((tq, PAIR), F32)
        c_ls = lax.cond(i >= 1, lambda: tuple(down(i - 1, down(i, [zc, zc], causal), None)),
                        lambda: tuple(down(i, [zc, zc], causal)))

        def more(st):
            return jnp.logical_and(st[0] <= i, jnp.max(jnp.maximum(st[1], st[2])) > LOG_WEIGHT_FLOOR)

        def down_step(st):
            c_ls = down(i - st[0], [st[1], st[2]], None)
            return (st[0] + 1, c_ls[0], c_ls[1])

        n_tiles = lax.while_loop(more, down_step, (jnp.int32(2), c_ls[0], c_ls[1]))[0]

        def up(kb, c_gs, accs, valid):
            ks = pl.multiple_of(kb * tq, tq)
            kt = k_ref[pl.ds(ks, tq), :]
            gs = [g_s[hh, kb] for hh in range(2)]
            pres = [_dot(g.astype(BF16), upp) + c_g for g, c_g in zip(gs, c_gs)]
            dzs = []
            for hh in range(2):
                beta = b_s[hh, kb]
                dz = gs[hh] - beta * (gs[hh] + pres[hh])
                if valid is not None:
                    dz = jnp.where(valid, dz, 0.0)
                dzs.append(dz.astype(BF16))
            new_a = [acc + _dot(dzb, kt) for acc, dzb in zip(accs, dzs)]
            dk_ref[pl.ds(ks, tq), :] += _dot_tn(dzs[0], qhs[0]) + _dot_tn(dzs[1], qhs[1])
            new_c = [c_g + jnp.sum(g, axis=1, keepdims=True) for c_g, g in zip(c_gs, gs)]
            return new_c, new_a

        def up_step(kb, st):
            c_gs, accs = up(kb, [st[0], st[1]], [st[2], st[3]], None)
            return (c_gs[0], c_gs[1], accs[0], accs[1])

        st = lax.fori_loop(i - n_tiles + 1, i - 1, up_step, (zc, zc, za, za))

        def last_two():
            c_gs, accs = up(i - 1, [st[0], st[1]], [st[2], st[3]], None)
            return tuple(up(i, c_gs, accs, causal)[1])

        accs = lax.cond(i >= 1, last_two, lambda: tuple(up(i, [zc, zc], [za, za], causal)[1]))
        dq_ref[...] = jnp.where(m0, accs[0], accs[1]) * Q_SCALE

        @pl.when(jnp.logical_and(p == N_PAIRS - 1, i == nq - 1))
        def _():
            rider.last(ride)

    return rider.call(
        body, [q, k, v, don, o_raw, ga], grid=(N_PAIRS, nq), name="attn_bwd",
        in_specs=[pl.BlockSpec((tq, PAIR), lambda p, i: (i, p)),
                  pl.BlockSpec((S, PAIR), lambda p, i: (0, p)),
                  pl.BlockSpec((S, PAIR), lambda p, i: (0, p)),
                  pl.BlockSpec((tq, PAIR), lambda p, i: (i, p)),
                  pl.BlockSpec((tq, PAIR), lambda p, i: (i, p)),
                  pl.BlockSpec((1, PAIR), lambda p, i: (0, p))],
        out_specs=[pl.BlockSpec((tq, PAIR), lambda p, i: (i, p)),
                   pl.BlockSpec((S, PAIR), lambda p, i: (0, p)),
                   pl.BlockSpec((S, PAIR), lambda p, i: (0, p)),
                   pl.BlockSpec((1, PAIR), lambda p, i: (0, p))],
        out_shape=[_sds((S, ATTN_WIDTH), F32), _sds((S, ATTN_WIDTH), F32), _sds((S, ATTN_WIDTH), F32),
                   _sds((1, ATTN_WIDTH), F32)],
        scratch_shapes=[pltpu.VMEM((2, nq, tq, tq), F32), pltpu.VMEM((2, nq, tq, tq), F32)],
        vmem_mib=56)


def _inproj_bwd(dq, dk, dv, du, dpre1, xh0, rstd0, g0, b0, w_in_s, tm):
    S, D = xh0.shape

    def body(dq_ref, dk_ref, dv_ref, du_ref, dp1_ref, xh_ref, rstd_ref, g_ref, b_ref, w_ref,
             gx_ref, dw_ref, dg_ref, db_ref):
        @pl.when(pl.program_id(0) == 0)
        def _():
            for r in (dw_ref, dg_ref, db_ref):
                r[...] = jnp.zeros_like(r)

        xh = xh_ref[...]
        xb = (xh * g_ref[...] + b_ref[...]).astype(BF16)
        dx0 = ALPHA * dp1_ref[...]
        for c, r in enumerate((dq_ref, dk_ref, dv_ref, du_ref)):
            dpb = r[...].astype(BF16)
            dx0 = dx0 + _dot_nt(dpb, w_ref[c])
            dw_ref[c] += _dot_tn(xb, dpb)
        dg_ref[...] += _colsum(dx0 * xh)
        db_ref[...] += _colsum(dx0)
        gx_ref[...] = _ln_bwd(dx0, xh, rstd_ref[...], g_ref[...])

    vec = _const_spec((1, D))
    half = _row_spec(tm, 512)
    return pl.pallas_call(
        body, grid=(S // tm,), name="inproj_bwd",
        in_specs=[half, half, half, half, _row_spec(tm, D), _row_spec(tm, D), _row_spec(tm, 1), vec, vec,
                  _const_spec((N_CHIPS, D, 512))],
        out_specs=[_row_spec(tm, D), _const_spec((N_CHIPS, D, 512)), vec, vec],
        out_shape=[_sds((S, D), F32), _sds((N_CHIPS, D, 512), F32), _sds((1, D), F32), _sds((1, D), F32)],
        compiler_params=_cp(56),
    )(*_hbm(dq, dk, dv, du, dpre1, xh0, rstd0, g0, b0, w_in_s))


def _place():
    return lax.axis_index("x"), lax.axis_index("y"), lax.axis_index("c")


CHIP_FLIPS = ((0, 1), (1, 0), (1, 1))


class _Rider:
    def __init__(self, ins, out_shapes, n_sem, phases, aliases=None):
        self.ins, self.out_shapes, self.n_sem, self.phases = list(ins), list(out_shapes), n_sem, phases
        self.aliases = aliases or {}

    def __add__(self, other):
        na, ma = len(self.ins), len(self.out_shapes)

        def phases(ins, outs, ssem, rsem):
            mine = self.phases(ins[:na], outs[:ma], ssem, rsem)
            rest = pl.ds(self.n_sem, other.n_sem)
            theirs = other.phases(ins[na:], outs[ma:], ssem.at[rest], rsem.at[rest])
            assert len(mine) == 1 and len(theirs) == 1
            return [mine[0] + theirs[0]]

        aliases = {**self.aliases, **{na + i: ma + o for i, o in other.aliases.items()}}
        return _Rider(self.ins + other.ins, self.out_shapes + other.out_shapes, self.n_sem + other.n_sem, phases,
                      aliases)

    def split(self, refs, n_in, n_out, n_scratch):
        a = n_in + len(self.ins)
        b = a + n_out
        c = b + len(self.out_shapes)
        own = (refs[:n_in], refs[a:b], refs[c:c + n_scratch])
        return own + ((refs[n_in:a], refs[b:c]) + tuple(refs[c + n_scratch:]),)

    def first(self, ride):
        for make in self.phases(*ride)[0]:
            make().start()

    def mid(self, ride):
        ph = self.phases(*ride)
        if len(ph) == 2:
            for make in ph[0]:
                make().wait_recv()
            for make in ph[1]:
                make().start()

    def last(self, ride):
        ph = self.phases(*ride)
        if len(ph) == 2:
            for make in ph[0]:
                make().wait_send()
        for make in ph[-1]:
            make().wait()

    def call(self, body, args, *, grid, name, in_specs, out_specs, out_shape, scratch_shapes, vmem_mib,
             prefetch=None):
        n_in, n_out = len(in_specs), len(out_specs)
        sems = [pltpu.SemaphoreType.DMA((self.n_sem,)), pltpu.SemaphoreType.DMA((self.n_sem,))]
        n_pre = 0 if prefetch is None else 1
        grid_spec = pltpu.PrefetchScalarGridSpec(
            num_scalar_prefetch=n_pre, grid=grid,
            in_specs=list(in_specs) + [HBM_SPEC] * len(self.ins),
            out_specs=list(out_specs) + [HBM_SPEC] * len(self.out_shapes),
            scratch_shapes=list(scratch_shapes) + sems)
        return pl.pallas_call(
            body, name=name, grid_spec=grid_spec,
            out_shape=list(out_shape) + self.out_shapes,
            input_output_aliases={n_pre + n_in + i: n_out + o for i, o in self.aliases.items()},
            compiler_params=_cp(vmem_mib),
        )(*([] if prefetch is None else [prefetch]), *_hbm(*args), *self.ins)

    def run(self, name):
        def body(*refs):
            ride = self.split(refs, 0, 0, 0)[3]
            self.first(ride)
            self.mid(ride)
            self.last(ride)

        return self.call(body, [], grid=(), name=name, in_specs=[], out_specs=[], out_shape=[],
                         scratch_shapes=[], vmem_mib=16)


def _remote(src, dst, ssem, rsem, n, dev):
    return functools.partial(pltpu.make_async_remote_copy, src_ref=src, dst_ref=dst, send_sem=ssem.at[n],
                             recv_sem=rsem.at[n], device_id=dev, device_id_type=MESH)


def _cast_into_slot(w, place, name):
    R, C = w.shape
    tr = min(R, 512)

    def body(pl_ref, w_ref, o_ref):
        o_ref[0] = w_ref[...].astype(BF16)

    return pl.pallas_call(
        body, name=name,
        grid_spec=pltpu.PrefetchScalarGridSpec(
            num_scalar_prefetch=1, grid=(R // tr,),
            in_specs=[pl.BlockSpec((tr, C), lambda r, pr: (r, 0))],
            out_specs=pl.BlockSpec((1, tr, C), lambda r, pr: (pr[1], r, 0))),
        out_shape=_sds((N_CHIPS, R, C), BF16),
    )(place, w)


CAST_STEPS = 8


def _cast_rest(ws, place, rider):
    n = len(ws)

    def body(pl_ref, *refs):
        w_refs, o_refs, _, ride = rider.split(refs, n, n, 0)
        r = pl.program_id(0)

        @pl.when(r == 0)
        def _():
            rider.first(ride)

        @pl.when(r == CAST_STEPS // 2)
        def _():
            rider.mid(ride)

        for w_ref, o_ref in zip(w_refs, o_refs):
            o_ref[0] = w_ref[...].astype(BF16)

        @pl.when(r == CAST_STEPS - 1)
        def _():
            rider.last(ride)

    def rows(w):
        return w.shape[0] // CAST_STEPS

    return rider.call(
        body, ws, grid=(CAST_STEPS,), name="cast_weights", prefetch=place,
        in_specs=[pl.BlockSpec((rows(w), w.shape[1]), lambda r, pr: (r, 0)) for w in ws],
        out_specs=[pl.BlockSpec((1, rows(w), w.shape[1]), lambda r, pr: (pr[1], r, 0)) for w in ws],
        out_shape=[_sds((N_CHIPS,) + w.shape, BF16) for w in ws], scratch_shapes=[], vmem_mib=32)


def _gather_rider(stacked, part="both"):
    n, nf = len(stacked), len(CHIP_FLIPS)

    def phases(ins, outs, ssem, rsem):
        x, y, c = _place()
        slot = 2 * x + y
        ici, d2d = [], []
        for w, (i_ref, o_ref) in enumerate(zip(ins, outs)):
            hh = o_ref.shape[1] // 2
            rows = pl.ds(c * hh, hh)
            for f, (fx, fy) in enumerate(CHIP_FLIPS):
                k = w * nf + f
                theirs = 2 * (x ^ fx) + (y ^ fy)
                if part != "pair":
                    ici.append(_remote(i_ref.at[slot, rows], o_ref.at[slot, rows], ssem, rsem, k,
                                       (x ^ fx, y ^ fy, c)))
                if part != "chips":
                    d2d.append(_remote(o_ref.at[theirs, rows], o_ref.at[theirs, rows], ssem, rsem,
                                       (n * nf if part == "both" else 0) + k, (x, y, 1 - c)))
        return [ph for ph in (ici, d2d) if ph]

    return _Rider(stacked, [_sds(s.shape, s.dtype) for s in stacked], (2 if part == "both" else 1) * n * nf,
                  phases, aliases={i: i for i in range(n)})


def _pair_swap_rider(grads):
    def phases(ins, outs, ssem, rsem):
        x, y, c = _place()
        return [[_remote(g.at[:, 1 - c], o, ssem, rsem, k, (x, y, 1 - c))
                 for k, (g, o) in enumerate(zip(ins, outs))]]

    return _Rider(grads, [_sds((N_CHIPS,) + g.shape[2:], g.dtype) for g in grads], len(grads), phases)


def _chip_scatter_rider(parts):
    nf = len(CHIP_FLIPS)

    def phases(ins, outs, ssem, rsem):
        x, y, c = _place()
        return [[_remote(r.at[2 * (x ^ fx) + (y ^ fy)], o.at[f], ssem, rsem, w * nf + f, (x ^ fx, y ^ fy, c))
                 for w, (r, o) in enumerate(zip(ins, outs)) for f, (fx, fy) in enumerate(CHIP_FLIPS)]]

    return _Rider(parts, [_sds((nf,) + r.shape[1:], r.dtype) for r in parts], len(parts) * nf, phases)


def _pair_send_rider(halves):
    def phases(ins, outs, ssem, rsem):
        x, y, c = _place()
        return [[_remote(h, o, ssem, rsem, k, (x, y, 1 - c)) for k, (h, o) in enumerate(zip(ins, outs))]]

    return _Rider(halves, [_sds(h.shape, h.dtype) for h in halves], len(halves), phases)


PAIR_SUM_STEPS = 2
CHIP_SUM_STEPS = 4
ADAMW_STEPS = 4


def _no_rider():
    return _Rider([], [], 1, lambda ins, outs, ssem, rsem: [[]])


def _add_pair(grads, recvs, place, name, rider):
    n = len(grads)

    def body(pl_ref, *refs):
        ins, outs, _, ride = rider.split(refs, 2 * n, 2 * n, 0)
        j, h = pl.program_id(0), pl.program_id(1)

        @pl.when(jnp.logical_and(j == 0, h == 0))
        def _():
            rider.first(ride)

        for w in range(n):
            s = ins[2 * w][:, 0] + ins[2 * w + 1][...]
            outs[2 * w][...] = s
            outs[2 * w + 1][...] = s.astype(BF16)

        @pl.when(jnp.logical_and(j == N_CHIPS - 1, h == PAIR_SUM_STEPS - 1))
        def _():
            rider.last(ride)

    in_specs, out_specs, out_shape, args = [], [], [], []
    for g, r in zip(grads, recvs):
        _, _, H, C = g.shape
        th = H // PAIR_SUM_STEPS
        spec = pl.BlockSpec((1, th, C), lambda j, h, pr: (j, h, 0))
        in_specs += [pl.BlockSpec((1, 1, th, C), lambda j, h, pr: (j, pr[0], h, 0)), spec]
        out_specs += [spec, spec]
        out_shape += [_sds((N_CHIPS, H, C), F32), _sds((N_CHIPS, H, C), BF16)]
        args += [g, r]
    res = rider.call(body, args, grid=(N_CHIPS, PAIR_SUM_STEPS), name=name, prefetch=place, in_specs=in_specs,
                     out_specs=out_specs, out_shape=out_shape, scratch_shapes=[], vmem_mib=32)
    return [(res[2 * w], res[2 * w + 1]) for w in range(n)], res[2 * n:]


def _add_chips(parts, recvs, place, name, rider):
    n = len(parts)

    def body(pl_ref, *refs):
        ins, outs, _, ride = rider.split(refs, 2 * n, n, 0)
        h = pl.program_id(0)

        @pl.when(h == 0)
        def _():
            rider.first(ride)

        for w in range(n):
            p_ref, r_ref = ins[2 * w], ins[2 * w + 1]
            outs[w][...] = p_ref[0] + r_ref[0].astype(F32) + r_ref[1].astype(F32) + r_ref[2].astype(F32)

        @pl.when(h == CHIP_SUM_STEPS - 1)
        def _():
            rider.last(ride)

    in_specs, out_specs, out_shape, args = [], [], [], []
    for p, r in zip(parts, recvs):
        _, H, C = p.shape
        th = H // CHIP_SUM_STEPS
        in_specs += [pl.BlockSpec((1, th, C), lambda h, pr: (pr[1], h, 0)),
                     pl.BlockSpec((len(CHIP_FLIPS), th, C), lambda h, pr: (0, h, 0))]
        out_specs.append(pl.BlockSpec((th, C), lambda h, pr: (h, 0)))
        out_shape.append(_sds((H, C), F32))
        args += [p, r]
    res = rider.call(body, args, grid=(CHIP_SUM_STEPS,), name=name, prefetch=place, in_specs=in_specs,
                     out_specs=out_specs, out_shape=out_shape, scratch_shapes=[], vmem_mib=32)
    return res[:n], res[n:]


def _adamw_math(w, g, m, v):
    m = ADAM_B1 * m + (1.0 - ADAM_B1) * g
    v = ADAM_B2 * v + (1.0 - ADAM_B2) * (g * g)
    m_hat = m / (1.0 - ADAM_B1 ** ADAM_STEP)
    v_hat = v / (1.0 - ADAM_B2 ** ADAM_STEP)
    delta = -ADAM_LR * (m_hat / (jnp.sqrt(v_hat) + ADAM_EPS) + ADAM_WD * w)
    return delta, m, v


def _adamw(ws, mines, theirs, ms, vs, place, name, rider):
    n = len(ws)

    def body(pl_ref, *refs):
        ins, outs, _, ride = rider.split(refs, 5 * n, 4 * n, 0)
        h, r = pl.program_id(0), pl.program_id(1)

        @pl.when(jnp.logical_and(h == 0, r == 0))
        def _():
            rider.first(ride)

        for k in range(n):
            w_ref, a_ref, b_ref, m_ref, v_ref = ins[5 * k:5 * k + 5]
            g = jnp.where(h == pl_ref[0], a_ref[...], b_ref[...])
            d, mo, vo = _adamw_math(w_ref[...], g, m_ref[...], v_ref[...])
            for o_ref, val in zip(outs[4 * k:4 * k + 4], (g, d, mo, vo)):
                o_ref[...] = val

        @pl.when(jnp.logical_and(h == 1, r == ADAMW_STEPS - 1))
        def _():
            rider.last(ride)

    in_specs, out_specs, out_shape, args = [], [], [], []
    for w, a, b, m, v in zip(ws, mines, theirs, ms, vs):
        R, C = w.shape
        th = (R // 2) // ADAMW_STEPS
        whole = pl.BlockSpec((th, C), lambda h, r, pr: (h * ADAMW_STEPS + r, 0))
        mine_spec = pl.BlockSpec((th, C), lambda h, r, pr: (jnp.where(h == pr[0], r, 0), 0))
        theirs_spec = pl.BlockSpec((th, C), lambda h, r, pr: (jnp.where(h == pr[0], 0, r), 0))
        in_specs += [whole, mine_spec, theirs_spec, whole, whole]
        out_specs += [whole] * 4
        out_shape += [_sds((R, C), F32)] * 4
        args += [w, a, b, m, v]
    res = rider.call(body, args, grid=(2, ADAMW_STEPS), name=name, prefetch=place, in_specs=in_specs,
                     out_specs=out_specs, out_shape=out_shape, scratch_shapes=[], vmem_mib=40)
    return [tuple(res[4 * k:4 * k + 4]) for k in range(n)], res[4 * n:]


DEVICE_FLIPS = tuple((fx, fy, fc) for fx in (0, 1) for fy in (0, 1) for fc in (0, 1))[1:]


def _pack_exchange_rider(pack):
    def phases(ins, outs, ssem, rsem):
        x, y, c = _place()
        mine = outs[0].at[4 * x + 2 * y + c]
        copies = [_remote(ins[0], mine, ssem, rsem, k, (x ^ fx, y ^ fy, c ^ fc))
                  for k, (fx, fy, fc) in enumerate(DEVICE_FLIPS)]
        copies.append(functools.partial(pltpu.make_async_copy, ins[0], mine, ssem.at[len(DEVICE_FLIPS)]))
        return [copies]

    return _Rider([pack], [_sds((N_DEV,) + pack.shape, pack.dtype)], len(DEVICE_FLIPS) + 1, phases)


def _small_sum_adamw(recv_a, recv_b, wpack, mpack, vpack):
    R = wpack.shape[0]

    def body(a_ref, b_ref, w_ref, m_ref, v_ref, gs_ref, d_ref, mo_ref, vo_ref):
        ta, tb = a_ref[0], b_ref[0]
        for dev in range(1, N_DEV):
            ta = ta + a_ref[dev]
            tb = tb + b_ref[dev]
        total = jnp.concatenate([ta, tb], axis=0)
        gs_ref[...] = total
        d, mo, vo = _adamw_math(w_ref[...], total, m_ref[...], v_ref[...])
        d_ref[...] = d
        mo_ref[...] = mo
        vo_ref[...] = vo

    return pl.pallas_call(
        body, name="small_sum_adamw", in_specs=[VMEM_SPEC] * 5, out_specs=[VMEM_SPEC] * 4,
        out_shape=[_sds((R, LANES), F32)] * 4,
    )(recv_a, recv_b, wpack, mpack, vpack)


def _rows8(a):
    a = a.reshape(-1, LANES)
    pad = (-a.shape[0]) % 8
    return jnp.pad(a, ((0, pad), (0, 0))) if pad else a


def _pack(parts):
    return jnp.concatenate([_rows8(a) for a in parts], axis=0)


def _unpack(pack, like):
    out, row = [], 0
    for a in like:
        n = a.size // LANES
        out.append(pack[row:row + n].reshape(a.shape))
        row += n + (-n) % 8
    return out


def kernel(x, p, emb_ln_g, emb_ln_b, w_in, attn_out_g, w_pool, pool_scale, w_out, ln1_g, ln1_b, w_up, w_down, ln2_g, ln2_b, w_ple, w_ple_gate, ln3_g, ln3_b, loss_target, m_emb_ln_g, m_emb_ln_b, m_w_in, m_attn_out_g, m_w_pool, m_pool_scale, m_w_out, m_ln1_g, m_ln1_b, m_w_up, m_w_down, m_ln2_g, m_ln2_b, m_w_ple, m_w_ple_gate, m_ln3_g, m_ln3_b, v_emb_ln_g, v_emb_ln_b, v_w_in, v_attn_out_g, v_w_pool, v_pool_scale, v_w_out, v_ln1_g, v_ln1_b, v_w_up, v_w_down, v_ln2_g, v_ln2_b, v_w_ple, v_w_ple_gate, v_ln3_g, v_ln3_b):
    S = x.shape[1]
    tm = min(256, S)
    tq = min(256, S)
    tm_mlp = min(1024, S)
    tm_pool = min(1024, S)
    xs = x[0]
    ps = p[0, 0]
    tgt = loss_target[0]
    row = lambda a: a.reshape(1, -1)
    g0, b0 = row(emb_ln_g), row(emb_ln_b)
    g1, b1, g2, b2, g3, b3 = ln1_g, ln1_b, ln2_g, ln2_b, ln3_g, ln3_b
    wp = w_pool[0]

    xi, yi, ci = _place()
    place = jnp.stack([ci, 2 * xi + yi]).astype(jnp.int32)
    names = ["w_in", "w_out", "w_up", "w_down", "w_ple", "w_ple_gate"]

    big = [w_in[0], w_out[0], w_up[0], w_down[0], w_ple[0], w_ple_gate[0]]
    s_in = _cast_into_slot(big[0], place, "cast_w_in")
    s_out, s_up, s_down, s_ple, s_gate, w_in_s = _cast_rest(big[1:], place, _gather_rider([s_in]))

    xh0, rstd0, q, k, v, u, s_out, s_ple, s_gate = _embln_inproj(
        xs, g0, b0, w_in_s, tm, _gather_rider([s_out, s_ple, s_gate], "chips"))
    o_raw, on, s_up, s_down, w_out_s, w_ple_s, w_gate_s = _attn_fwd(
        q, k, v, attn_out_g, tq, _gather_rider([s_up, s_down], "chips") + _gather_rider([s_out, s_ple, s_gate], "pair"))
    w_out_f = w_out_s.reshape(D_MODEL, D_MODEL)
    w_gate_f = w_gate_s.reshape(D_MODEL, D_MODEL)
    d_b, pooled = _pool_fwd(u, wp, pool_scale, tm_pool)
    xh1, rstd1, x1b, w_up_s, w_down_s = _mix_ln1(on, pooled, xh0, g0, b0, w_out_f, g1, b1, tm,
                                                 _gather_rider([s_up, s_down], "pair"))
    w_down_f = w_down_s.reshape(D_FF, D_MODEL)
    xh2, rstd2, rb = _mlp_ln2(xh1, x1b, g1, b1, w_up_s, w_down_f, tm_mlp)

    (dpre2, dhb, dw_ple, dw_gate, dg3, db3, dg2, db2, loss_row) = _ple_ln3_loss(
        xh2, rstd2, g2, b2, ps, w_ple_s, w_gate_f, g3, b3, tgt, tm)
    def halves_of(g):
        return g.reshape(N_CHIPS, 2, g.shape[1] // 2, g.shape[2])

    ple_halves = [halves_of(dw_ple), halves_of(dw_gate.reshape(N_CHIPS, D_MODEL // N_CHIPS, D_MODEL))]
    dx1m, da, *ple_pair = _mlp_bwd(rb, dhb, w_up_s, w_down_f, tm_mlp, _pair_swap_rider(ple_halves))
    (dw_up,) = _tn_matmul(x1b, da, "grad_w_up", 1024, min(512, S), True, _no_rider())
    up_halves = halves_of(dw_up)
    dw_down, up_pair = _tn_matmul(rb, dhb, "grad_w_down", 1024, min(512, S), False,
                                  _pair_swap_rider([up_halves]), square_a=True)
    down_halves = halves_of(dw_down.reshape(N_CHIPS, D_FF // N_CHIPS, D_MODEL))
    dpre1, don, dpooled, dw_out, dg1, db1, down_pair = _mix_bwd(
        dpre2, dx1m, xh1, rstd1, g1, w_out_f, on, pooled, tm, _pair_swap_rider([down_halves]))
    early_sum, _ = _add_pair([up_halves, down_halves] + ple_halves, [up_pair, down_pair] + ple_pair, place,
                             "pair_sum_mlp_ple", _no_rider())
    out_halves = halves_of(dw_out.reshape(N_CHIPS, D_MODEL // N_CHIPS, D_MODEL))
    du, dwp, dsc, out_pair = _pool_bwd(dpooled, d_b, wp, pool_scale, tm_pool, _pair_swap_rider([out_halves]))
    (out_sum,), _ = _add_pair([out_halves], [out_pair], place, "pair_sum_w_out", _no_rider())
    pack_a = _pack([jnp.broadcast_to(loss_row, (8, LANES)), dwp, dsc, dg1, db1, dg2, db2, dg3, db3])
    early_sum = [out_sum] + early_sum
    riding = _chip_scatter_rider([b for _, b in early_sum]) + _pack_exchange_rider(pack_a)
    dq, dk, dv, dga, *arrived = _attn_bwd(q, k, v, don, o_raw, attn_out_g, tq, riding)
    early_chips, recv_a = arrived[:-1], arrived[-1]
    grad_x, dw_in, dg0, db0 = _inproj_bwd(dq, dk, dv, du, dpre1, xh0, rstd0, g0, b0, w_in_s, tm)

    in_halves = halves_of(dw_in)
    pack_b = _pack([dg0, db0, dga])
    early_mine, (in_pair, recv_b) = _add_chips(
        [s for s, _ in early_sum], early_chips, place, "chip_sum_early",
        _pair_swap_rider([in_halves]) + _pack_exchange_rider(pack_b))
    (in_sum,), early_theirs = _add_pair([in_halves], [in_pair], place, "pair_sum_w_in", _pair_send_rider(early_mine))
    ms = [m_w_in, m_w_out, m_w_up, m_w_down, m_w_ple, m_w_ple_gate]
    vs = [v_w_in, v_w_out, v_w_up, v_w_down, v_w_ple, v_w_ple_gate]
    early_res, _ = _adamw(big[1:], early_mine, early_theirs, [m[0] for m in ms[1:]], [v[0] for v in vs[1:]],
                          place, "adamw_early", _no_rider())
    (in_chips,) = _chip_scatter_rider([in_sum[1]]).run("reduce_chips_late")
    (in_mine,), _ = _add_chips([in_sum[0]], [in_chips], place, "chip_sum_w_in", _no_rider())
    (in_theirs,) = _pair_send_rider([in_mine]).run("gather_pair_w_in")
    in_res, _ = _adamw(big[:1], [in_mine], [in_theirs], [ms[0][0]], [vs[0][0]], place, "adamw_w_in", _no_rider())
    big_out = {n: tuple(r.reshape(m.shape) for r in res4) for n, res4, m in zip(names, in_res + early_res, ms)}

    small_names = ["w_pool", "pool_scale", "ln1_g", "ln1_b", "ln2_g", "ln2_b", "ln3_g", "ln3_b",
                   "emb_ln_g", "emb_ln_b", "attn_out_g"]
    small_w = [w_pool, pool_scale, ln1_g, ln1_b, ln2_g, ln2_b, ln3_g, ln3_b, emb_ln_g, emb_ln_b, attn_out_g]
    small_m = [m_w_pool, m_pool_scale, m_ln1_g, m_ln1_b, m_ln2_g, m_ln2_b, m_ln3_g, m_ln3_b,
               m_emb_ln_g, m_emb_ln_b, m_attn_out_g]
    small_v = [v_w_pool, v_pool_scale, v_ln1_g, v_ln1_b, v_ln2_g, v_ln2_b, v_ln3_g, v_ln3_b,
               v_emb_ln_g, v_emb_ln_b, v_attn_out_g]
    loss_like = jnp.zeros((8, LANES), F32)
    gs, ds, mos, vos = _small_sum_adamw(recv_a, recv_b, _pack([loss_like] + small_w), _pack([loss_like] + small_m),
                                        _pack([jnp.ones((8, LANES), F32)] + small_v))
    like = [loss_like] + small_w
    gs_u, ds_u, mos_u, vos_u = (_unpack(a, like) for a in (gs, ds, mos, vos))
    loss = gs_u[0][0, 0]
    small_out = {n: (gs_u[i + 1], ds_u[i + 1], mos_u[i + 1], vos_u[i + 1]) for i, n in enumerate(small_names)}

    order = ["emb_ln_g", "emb_ln_b", "w_in", "attn_out_g", "w_pool", "pool_scale", "w_out", "ln1_g", "ln1_b",
             "w_up", "w_down", "ln2_g", "ln2_b", "w_ple", "w_ple_gate", "ln3_g", "ln3_b"]
    res = {**big_out, **small_out}
    outs = [loss, grad_x.reshape(x.shape)]
    for kind in range(4):
        outs += [res[n][kind] for n in order]
    return tuple(outs)
```

```python
import functools

import jax
import jax.numpy as jnp
from jax import lax
from jax.experimental import pallas as pl
from jax.experimental.pallas import tpu as pltpu

F32 = jnp.float32
BF16 = jnp.bfloat16

D_MODEL = 1024
ATTN_WIDTH = 512
POOL_WIDTH = 512
HEAD_DIM = 64
PAIR = 2 * HEAD_DIM
N_PAIRS = ATTN_WIDTH // PAIR
N_POOL_GROUPS = 4
POOL_GROUP = 128
POOL_HALO = 16
D_FF = 4096
PLE_DIM = 256
N_CHIPS = 4
N_DEV = 8
LN_EPS = 1e-5
RMS_EPS = 1e-6
ALPHA = float(2.0 ** 0.25)
Q_SCALE = 0.125
ADAM_LR = 0.001
ADAM_B1 = 0.9
ADAM_B2 = 0.999
ADAM_EPS = 1e-08
ADAM_WD = 0.01
ADAM_STEP = 10
LANES = 128
MIB = 1024 * 1024

MESH = pl.DeviceIdType.MESH
HBM_SPEC = pl.BlockSpec(memory_space=pltpu.HBM)
VMEM_SPEC = pl.BlockSpec(memory_space=pltpu.VMEM)


def _cp(vmem_mib):
    return pltpu.CompilerParams(vmem_limit_bytes=vmem_mib * MIB)


def _dot(a, b):
    return jnp.dot(a, b, preferred_element_type=F32)


def _dot_nt(a, b):
    return lax.dot_general(a, b, (((1,), (1,)), ((), ())), preferred_element_type=F32)


def _dot_tn(a, b):
    return lax.dot_general(a, b, (((0,), (0,)), ((), ())), preferred_element_type=F32)


def _ln_fwd(pre):
    mu = jnp.mean(pre, axis=-1, keepdims=True)
    xc = pre - mu
    var = jnp.mean(xc * xc, axis=-1, keepdims=True)
    rstd = lax.rsqrt(var + LN_EPS)
    return xc * rstd, rstd


def _ln_bwd(dy, xh, rstd, g):
    dxh = dy * g
    m1 = jnp.mean(dxh, axis=-1, keepdims=True)
    m2 = jnp.mean(dxh * xh, axis=-1, keepdims=True)
    return rstd * (dxh - m1 - xh * m2)


def _colsum(a):
    return jnp.sum(a, axis=0, keepdims=True)


def _neg_softplus(z):
    return -(jnp.maximum(z, 0.0) + jnp.log(1.0 + jnp.exp(-jnp.abs(z))))


def _split_bf16(a):
    hi = a.astype(BF16)
    lo = (a - hi.astype(F32)).astype(BF16)
    return hi, lo


def _row_spec(tm, n):
    return pl.BlockSpec((tm, n), lambda i: (i, 0))


def _const_spec(shape):
    nd = len(shape)
    return pl.BlockSpec(shape, lambda *_: (0,) * nd)


def _hbm(*arrays):
    return [pltpu.with_memory_space_constraint(a, pltpu.HBM) for a in arrays]


def _sds(shape, dtype):
    return pltpu.HBM(shape, dtype)


def _embln_inproj(x, g0, b0, w_in_s, tm, rider):
    S, D = x.shape
    n_t = S // tm

    def body(*refs):
        ((x_ref, g_ref, b_ref, w_ref), (xh_ref, rstd_ref, q_ref, k_ref, v_ref, u_ref), _,
         ride) = rider.split(refs, 4, 6, 0)
        i = pl.program_id(0)

        @pl.when(i == 0)
        def _():
            rider.first(ride)

        @pl.when(i == (3 * n_t) // 4)
        def _():
            rider.mid(ride)

        xh, rstd = _ln_fwd(x_ref[...])
        xh_ref[...] = xh
        rstd_ref[...] = rstd
        xb = (xh * g_ref[...] + b_ref[...]).astype(BF16)
        q_ref[...] = (_dot(xb, w_ref[0]) * Q_SCALE).astype(BF16)
        k_ref[...] = _dot(xb, w_ref[1]).astype(BF16)
        v_ref[...] = _dot(xb, w_ref[2]).astype(BF16)
        u_ref[...] = _dot(xb, w_ref[3])

        @pl.when(i == n_t - 1)
        def _():
            rider.last(ride)

    return rider.call(
        body, [x, g0, b0, w_in_s], grid=(n_t,), name="embln_inproj",
        in_specs=[_row_spec(tm, D), _const_spec((1, D)), _const_spec((1, D)),
                  _const_spec((N_CHIPS, D, 512))],
        out_specs=[_row_spec(tm, D), _row_spec(tm, 1), _row_spec(tm, 512), _row_spec(tm, 512),
                   _row_spec(tm, 512), _row_spec(tm, 512)],
        out_shape=[_sds((S, D), F32), _sds((S, 1), F32), _sds((S, 512), BF16), _sds((S, 512), BF16),
                   _sds((S, 512), BF16), _sds((S, 512), F32)],
        scratch_shapes=[], vmem_mib=40)


def _tri(n, upper):
    r = lax.broadcasted_iota(jnp.int32, (n, n), 0)
    c = lax.broadcasted_iota(jnp.int32, (n, n), 1)
    keep = (r < c) if upper else (r > c)
    return jnp.where(keep, 1.0, 0.0).astype(BF16)


def _strictly_causal(n):
    return lax.broadcasted_iota(jnp.int32, (n, n), 1) < lax.broadcasted_iota(jnp.int32, (n, n), 0)


LOG_WEIGHT_FLOOR = -110.0


def _sb_tile(qhs, kt, low, c_ls, valid):
    valids = valid if isinstance(valid, (list, tuple)) else [valid] * len(qhs)
    zs = [_dot_nt(qh, kt) for qh in qhs]
    lrs = [_neg_softplus(z) for z in zs]
    ls_ = [lr if m is None else jnp.where(m, lr, 0.0) for lr, m in zip(lrs, valids)]
    sfx = [_dot(l.astype(BF16), low) + c_l for l, c_l in zip(ls_, c_ls)]
    lss = [z + lr for z, lr in zip(zs, lrs)]
    ws = [jnp.exp(ls + s) for ls, s in zip(lss, sfx)]
    ws = [w if m is None else jnp.where(m, w, 0.0) for w, m in zip(ws, valids)]
    return lss, ls_, ws


def _attn_fwd(q, k, v, ga, tq, rider):
    S = q.shape[0]
    nq = S // tq

    def body(*refs):
        (q_ref, k_ref, v_ref, ga_ref), (o_ref, on_ref), (low_s,), ride = rider.split(refs, 4, 2, 1)
        p, i = pl.program_id(0), pl.program_id(1)

        @pl.when(jnp.logical_and(p == 0, i == 0))
        def _():
            rider.first(ride)
            low_s[...] = _tri(tq, upper=False)

        @pl.when(jnp.logical_and(p == N_PAIRS - 1, i == 0))
        def _():
            rider.mid(ride)

        lane = lax.broadcasted_iota(jnp.int32, (1, PAIR), 1)
        m0 = lane < HEAD_DIM
        low = low_s[...]
        q2 = q_ref[...]
        qhs = [jnp.where(m0, q2, jnp.zeros_like(q2)), jnp.where(m0, jnp.zeros_like(q2), q2)]

        def tile(kb, c_ls, accs, valid):
            ks = pl.multiple_of(kb * tq, tq)
            kt = k_ref[pl.ds(ks, tq), :]
            vt = v_ref[pl.ds(ks, tq), :]
            _, ls_, ws = _sb_tile(qhs, kt, low, c_ls, valid)
            new_a = [acc + _dot(w.astype(BF16), vt) for acc, w in zip(accs, ws)]
            new_c = [c_l + jnp.sum(l, axis=1, keepdims=True) for c_l, l in zip(c_ls, ls_)]
            return new_c, new_a

        zc, za = jnp.zeros((tq, 1), F32), jnp.zeros((tq, PAIR), F32)

        def first_two():
            c_ls, accs = tile(i, [zc, zc], [za, za], _strictly_causal(tq))
            c_ls, accs = tile(i - 1, c_ls, accs, None)
            return (*c_ls, *accs)

        def first_one():
            c_ls, accs = tile(i, [zc, zc], [za, za], _strictly_causal(tq))
            return (*c_ls, *accs)

        st0 = lax.cond(i >= 1, first_two, first_one)

        def more(st):
            return jnp.logical_and(st[0] <= i, jnp.max(jnp.maximum(st[1], st[2])) > LOG_WEIGHT_FLOOR)

        def step(st):
            n, c0, c1, a0, a1 = st
            c_ls, accs = tile(i - n, [c0, c1], [a0, a1], None)
            return (n + 1, c_ls[0], c_ls[1], accs[0], accs[1])

        st = lax.while_loop(more, step, (jnp.int32(2), *st0))
        o = jnp.where(m0, st[3], st[4])
        o_ref[...] = o
        sq = o * o
        ms0 = jnp.sum(jnp.where(m0, sq, 0.0), axis=-1, keepdims=True) * (1.0 / HEAD_DIM)
        ms1 = jnp.sum(jnp.where(m0, 0.0, sq), axis=-1, keepdims=True) * (1.0 / HEAD_DIM)
        rs = jnp.where(m0, lax.rsqrt(ms0 + RMS_EPS), lax.rsqrt(ms1 + RMS_EPS))
        on_ref[...] = (o * rs * ga_ref[...]).astype(BF16)

        @pl.when(jnp.logical_and(p == N_PAIRS - 1, i == nq - 1))
        def _():
            rider.last(ride)

    return rider.call(
        body, [q, k, v, ga], grid=(N_PAIRS, nq), name="attn_fwd",
        in_specs=[pl.BlockSpec((tq, PAIR), lambda p, i: (i, p)),
                  pl.BlockSpec((S, PAIR), lambda p, i: (0, p)),
                  pl.BlockSpec((S, PAIR), lambda p, i: (0, p)),
                  pl.BlockSpec((1, PAIR), lambda p, i: (0, p))],
        out_specs=[pl.BlockSpec((tq, PAIR), lambda p, i: (i, p)),
                   pl.BlockSpec((tq, PAIR), lambda p, i: (i, p))],
        out_shape=[_sds((S, ATTN_WIDTH), F32), _sds((S, ATTN_WIDTH), BF16)],
        scratch_shapes=[pltpu.VMEM((tq, tq), BF16)], vmem_mib=40)


def _pool_fwd(u, w_pool, pscale, tm):
    S = u.shape[0]
    hb = tm // POOL_HALO

    def body(u_ref, uh_ref, wp_ref, sc_ref, d_ref, pooled_ref):
        i = pl.program_id(0)
        halo = jnp.where(i > 0, uh_ref[...], 0.0)
        pos = i * tm + lax.broadcasted_iota(jnp.int32, (tm, 1), 0)
        for g in range(N_POOL_GROUPS):
            win = 2 ** (g + 1)
            cols = slice(g * POOL_GROUP, (g + 1) * POOL_GROUP)
            ut = u_ref[:, cols]
            s = jnp.concatenate([halo[:, cols], ut], axis=0)
            for sh in (1, 2, 4, 8)[:g + 1]:
                s = s + pltpu.roll(s, sh, 0)
            cnt = jnp.minimum(pos + 1, win).astype(F32)
            db = (s[POOL_HALO:, :] / cnt - ut).astype(BF16)
            y = _dot(db, wp_ref[g].astype(BF16))
            d_ref[:, cols] = db
            pooled_ref[:, cols] = (y * sc_ref[:, cols]).astype(BF16)

    return pl.pallas_call(
        body, grid=(S // tm,), name="pool_fwd",
        in_specs=[_row_spec(tm, POOL_WIDTH),
                  pl.BlockSpec((POOL_HALO, POOL_WIDTH), lambda i: (jnp.maximum(i * hb - 1, 0), 0)),
                  _const_spec((N_POOL_GROUPS, POOL_GROUP, POOL_GROUP)), _const_spec((1, POOL_WIDTH))],
        out_specs=[_row_spec(tm, POOL_WIDTH), _row_spec(tm, POOL_WIDTH)],
        out_shape=[_sds((S, POOL_WIDTH), BF16), _sds((S, POOL_WIDTH), BF16)],
        compiler_params=_cp(32),
    )(*_hbm(u, u, w_pool, pscale))


def _mix_ln1(on, pooled, xh0, g0, b0, w_out, g1, b1, tm, rider):
    S, D = xh0.shape
    n_t = S // tm

    def body(*refs):
        ((on_ref, po_ref, xh0_ref, g0_ref, b0_ref, w_ref, g1_ref, b1_ref), (xh_ref, rstd_ref, xb_ref), _,
         ride) = rider.split(refs, 8, 3, 0)

        @pl.when(pl.program_id(0) == 0)
        def _():
            rider.first(ride)

        mixed = _dot(on_ref[...], w_ref[:ATTN_WIDTH, :]) + _dot(po_ref[...], w_ref[ATTN_WIDTH:, :])
        x0 = xh0_ref[...] * g0_ref[...] + b0_ref[...]
        xh, rstd = _ln_fwd(ALPHA * x0 + mixed)
        xh_ref[...] = xh
        rstd_ref[...] = rstd
        xb_ref[...] = (xh * g1_ref[...] + b1_ref[...]).astype(BF16)

        @pl.when(pl.program_id(0) == n_t - 1)
        def _():
            rider.last(ride)

    return rider.call(
        body, [on, pooled, xh0, g0, b0, w_out, g1, b1], grid=(n_t,), name="mix_ln1",
        in_specs=[_row_spec(tm, ATTN_WIDTH), _row_spec(tm, POOL_WIDTH), _row_spec(tm, D),
                  _const_spec((1, D)), _const_spec((1, D)), _const_spec((D, D)),
                  _const_spec((1, D)), _const_spec((1, D))],
        out_specs=[_row_spec(tm, D), _row_spec(tm, 1), _row_spec(tm, D)],
        out_shape=[_sds((S, D), F32), _sds((S, 1), F32), _sds((S, D), BF16)],
        scratch_shapes=[], vmem_mib=40)


def _mlp_ln2(xh1, x1b, g1, b1, w_up_s, w_down, tm):
    S, D = xh1.shape
    fc = D_FF // N_CHIPS

    def body(xh_ref, xb_ref, g_ref, b_ref, wu_ref, wd_ref, xh2_ref, rstd_ref, r_ref, acc_ref):
        j = pl.program_id(1)

        @pl.when(j == 0)
        def _():
            acc_ref[...] = jnp.zeros_like(acc_ref)

        r = jnp.maximum(_dot(xb_ref[...], wu_ref[0]), 0.0)
        r_ref[...] = r.astype(BF16)
        acc_ref[...] += _dot((r * r).astype(BF16), wd_ref[...])

        @pl.when(j == N_CHIPS - 1)
        def _():
            x1 = xh_ref[...] * g_ref[...] + b_ref[...]
            xh, rstd = _ln_fwd(ALPHA * x1 + acc_ref[...])
            xh2_ref[...] = xh
            rstd_ref[...] = rstd

    return pl.pallas_call(
        body, grid=(S // tm, N_CHIPS), name="mlp_ln2",
        in_specs=[pl.BlockSpec((tm, D), lambda i, j: (i, 0)), pl.BlockSpec((tm, D), lambda i, j: (i, 0)),
                  pl.BlockSpec((1, D), lambda i, j: (0, 0)), pl.BlockSpec((1, D), lambda i, j: (0, 0)),
                  pl.BlockSpec((1, D, fc), lambda i, j: (j, 0, 0)),
                  pl.BlockSpec((fc, D), lambda i, j: (j, 0))],
        out_specs=[pl.BlockSpec((tm, D), lambda i, j: (i, 0)), pl.BlockSpec((tm, 1), lambda i, j: (i, 0)),
                   pl.BlockSpec((tm, fc), lambda i, j: (i, j))],
        out_shape=[_sds((S, D), F32), _sds((S, 1), F32), _sds((S, D_FF), BF16)],
        scratch_shapes=[pltpu.VMEM((tm, D), F32)],
        compiler_params=_cp(56),
    )(*_hbm(xh1, x1b, g1, b1, w_up_s, w_down))


def _ple_ln3_loss(xh2, rstd2, g2, b2, p, w_ple_s, w_gate, g3, b3, target, tm):
    S, D = xh2.shape
    pc = D // N_CHIPS

    def body(xh2_ref, rstd2_ref, g2_ref, b2_ref, p_ref, wp_ref, wg_ref, g3_ref, b3_ref, t_ref,
             dpre2_ref, dhb_ref, dwp_ref, dwg_ref, dg3_ref, db3_ref, dg2_ref, db2_ref, loss_ref):
        i = pl.program_id(0)

        @pl.when(i == 0)
        def _():
            for r in (dwp_ref, dwg_ref, dg3_ref, db3_ref, dg2_ref, db2_ref, loss_ref):
                r[...] = jnp.zeros_like(r)

        xh2 = xh2_ref[...]
        x2 = xh2 * g2_ref[...] + b2_ref[...]
        x2b = x2.astype(BF16)
        gate = 1.0 / (1.0 + jnp.exp(-_dot(x2b, wg_ref[...])))
        pb = p_ref[...].astype(BF16)
        pe = jnp.concatenate([_dot(pb, wp_ref[c]) for c in range(N_CHIPS)], axis=1)
        xh3, rstd3 = _ln_fwd(ALPHA * x2 + pe * gate)
        diff = xh3 * g3_ref[...] + b3_ref[...] - t_ref[...]
        loss_ref[...] += (0.5 / D) * jnp.sum(diff * diff)
        dy = diff * (1.0 / D)
        dg3_ref[...] += _colsum(dy * xh3)
        db3_ref[...] += _colsum(dy)
        dpre3 = _ln_bwd(dy, xh3, rstd3, g3_ref[...])
        dpe_b = (dpre3 * gate).astype(BF16)
        dgp_b = (dpre3 * pe * gate * (1.0 - gate)).astype(BF16)
        dx2 = ALPHA * dpre3 + _dot_nt(dgp_b, wg_ref[...])
        dwg_ref[...] += _dot_tn(x2b, dgp_b)
        for c in range(N_CHIPS):
            dwp_ref[c] += _dot_tn(pb, dpe_b[:, c * pc:(c + 1) * pc])
        dg2_ref[...] += _colsum(dx2 * xh2)
        db2_ref[...] += _colsum(dx2)
        dpre2 = _ln_bwd(dx2, xh2, rstd2_ref[...], g2_ref[...])
        dpre2_ref[...] = dpre2
        dhb_ref[...] = dpre2.astype(BF16)

    vec = _const_spec((1, D))
    return pl.pallas_call(
        body, grid=(S // tm,), name="ple_ln3_loss",
        in_specs=[_row_spec(tm, D), _row_spec(tm, 1), vec, vec, _row_spec(tm, PLE_DIM),
                  _const_spec((N_CHIPS, PLE_DIM, pc)), _const_spec((D, D)), vec, vec, _row_spec(tm, D)],
        out_specs=[_row_spec(tm, D), _row_spec(tm, D), _const_spec((N_CHIPS, PLE_DIM, pc)),
                   _const_spec((D, D)), vec, vec, vec, vec, _const_spec((1, LANES))],
        out_shape=[_sds((S, D), F32), _sds((S, D), BF16), _sds((N_CHIPS, PLE_DIM, pc), F32),
                   _sds((D, D), F32), _sds((1, D), F32), _sds((1, D), F32), _sds((1, D), F32),
                   _sds((1, D), F32), _sds((1, LANES), F32)],
        compiler_params=_cp(48),
    )(*_hbm(xh2, rstd2, g2, b2, p, w_ple_s, w_gate, g3, b3, target))


def _mlp_bwd(rb, dhb, w_up_s, w_down, tm, rider):
    S, D = dhb.shape
    fc = D_FF // N_CHIPS
    n_t = S // tm

    def body(*refs):
        (r_ref, dh_ref, wu_ref, wd_ref), (dx_ref, da_ref), _, ride = rider.split(refs, 4, 2, 0)
        i, j = pl.program_id(0), pl.program_id(1)

        @pl.when(jnp.logical_and(i == 0, j == 0))
        def _():
            rider.first(ride)

        @pl.when(j == 0)
        def _():
            dx_ref[...] = jnp.zeros_like(dx_ref)

        da = (_dot_nt(dh_ref[...], wd_ref[...]) * (2.0 * r_ref[...].astype(F32))).astype(BF16)
        da_ref[...] = da
        dx_ref[...] += _dot_nt(da, wu_ref[0])

        @pl.when(jnp.logical_and(i == n_t - 1, j == N_CHIPS - 1))
        def _():
            rider.last(ride)

    return rider.call(
        body, [rb, dhb, w_up_s, w_down], grid=(n_t, N_CHIPS), name="mlp_bwd",
        in_specs=[pl.BlockSpec((tm, fc), lambda i, j: (i, j)), pl.BlockSpec((tm, D), lambda i, j: (i, 0)),
                  pl.BlockSpec((1, D, fc), lambda i, j: (j, 0, 0)),
                  pl.BlockSpec((fc, D), lambda i, j: (j, 0))],
        out_specs=[pl.BlockSpec((tm, D), lambda i, j: (i, 0)), pl.BlockSpec((tm, fc), lambda i, j: (i, j))],
        out_shape=[_sds((S, D), F32), _sds((S, D_FF), BF16)],
        scratch_shapes=[], vmem_mib=56)


def _tn_matmul(a, b, name, tk, tt, stacked, rider, square_a=False):
    T, K = a.shape
    N = b.shape[1]
    tn = 1024
    grid = (K // tk, N // tn, T // tt)

    def body(*refs):
        (a_ref, b_ref), (o_ref,), _, ride = rider.split(refs, 2, 1, 0)
        at = [pl.program_id(d) for d in range(3)]

        @pl.when(jnp.logical_and(jnp.logical_and(at[0] == 0, at[1] == 0), at[2] == 0))
        def _():
            rider.first(ride)

        @pl.when(at[2] == 0)
        def _():
            o_ref[...] = jnp.zeros_like(o_ref)

        a_t = a_ref[...]
        if square_a:
            a_t = a_t * a_t
        prod = _dot_tn(a_t, b_ref[...])
        if stacked:
            o_ref[0] += prod
        else:
            o_ref[...] += prod

        @pl.when(jnp.logical_and(jnp.logical_and(at[0] == grid[0] - 1, at[1] == grid[1] - 1),
                                 at[2] == grid[2] - 1))
        def _():
            rider.last(ride)

    if stacked:
        out_spec = pl.BlockSpec((1, tk, tn), lambda k, n, t: (n, k, 0))
        out_shape = _sds((N // tn, K, tn), F32)
    else:
        out_spec = pl.BlockSpec((tk, tn), lambda k, n, t: (k, n))
        out_shape = _sds((K, N), F32)
    return rider.call(
        body, [a, b], grid=grid, name=name,
        in_specs=[pl.BlockSpec((tt, tk), lambda k, n, t: (t, k)),
                  pl.BlockSpec((tt, tn), lambda k, n, t: (t, n))],
        out_specs=[out_spec], out_shape=[out_shape], scratch_shapes=[], vmem_mib=40)


def _mix_bwd(dpre2, dx1m, xh1, rstd1, g1, w_out, on, pooled, tm, rider):
    S, D = xh1.shape
    n_t = S // tm

    def body(*refs):
        ((dp2_ref, dxm_ref, xh_ref, rstd_ref, g_ref, w_ref, on_ref, po_ref),
         (dpre1_ref, don_ref, dpo_ref, dw_ref, dg_ref, db_ref), _, ride) = rider.split(refs, 8, 6, 0)

        @pl.when(pl.program_id(0) == 0)
        def _():
            rider.first(ride)
            for r in (dw_ref, dg_ref, db_ref):
                r[...] = jnp.zeros_like(r)

        xh = xh_ref[...]
        dx1 = ALPHA * dp2_ref[...] + dxm_ref[...]
        dg_ref[...] += _colsum(dx1 * xh)
        db_ref[...] += _colsum(dx1)
        dpre1 = _ln_bwd(dx1, xh, rstd_ref[...], g_ref[...])
        dpre1_ref[...] = dpre1
        dmb = dpre1.astype(BF16)
        dcat = _dot_nt(dmb, w_ref[...])
        don_ref[...] = dcat[:, :ATTN_WIDTH]
        dpo_ref[...] = dcat[:, ATTN_WIDTH:]
        dw_ref[:ATTN_WIDTH, :] += _dot_tn(on_ref[...], dmb)
        dw_ref[ATTN_WIDTH:, :] += _dot_tn(po_ref[...], dmb)

        @pl.when(pl.program_id(0) == n_t - 1)
        def _():
            rider.last(ride)

    vec = _const_spec((1, D))
    return rider.call(
        body, [dpre2, dx1m, xh1, rstd1, g1, w_out, on, pooled], grid=(n_t,), name="mix_bwd",
        in_specs=[_row_spec(tm, D), _row_spec(tm, D), _row_spec(tm, D), _row_spec(tm, 1), vec,
                  _const_spec((D, D)), _row_spec(tm, ATTN_WIDTH), _row_spec(tm, POOL_WIDTH)],
        out_specs=[_row_spec(tm, D), _row_spec(tm, ATTN_WIDTH), _row_spec(tm, POOL_WIDTH),
                   _const_spec((D, D)), vec, vec],
        out_shape=[_sds((S, D), F32), _sds((S, ATTN_WIDTH), F32), _sds((S, POOL_WIDTH), F32),
                   _sds((D, D), F32), _sds((1, D), F32), _sds((1, D), F32)],
        scratch_shapes=[], vmem_mib=48)


def _pool_bwd(dpooled, d_b, w_pool, pscale, tm, rider):
    S = dpooled.shape[0]
    hb = tm // POOL_HALO
    n_t = S // tm
    te = tm + POOL_HALO

    def body(*refs):
        ((dp_ref, dph_ref, d_ref, wp_ref, sc_ref), (du_ref, dwp_ref, dsc_ref), _,
         ride) = rider.split(refs, 5, 3, 0)
        i = pl.program_id(0)

        @pl.when(i == 0)
        def _():
            rider.first(ride)
            dwp_ref[...] = jnp.zeros_like(dwp_ref)
            dsc_ref[...] = jnp.zeros_like(dsc_ref)

        halo = jnp.where(i < n_t - 1, dph_ref[...], 0.0)
        pos = i * tm + lax.broadcasted_iota(jnp.int32, (te, 1), 0)
        for g in range(N_POOL_GROUPS):
            win = 2 ** (g + 1)
            cols = slice(g * POOL_GROUP, (g + 1) * POOL_GROUP)
            wpb = wp_ref[g].astype(BF16)
            dpt = dp_ref[:, cols]
            dpe = jnp.concatenate([dpt, halo[:, cols]], axis=0)
            dyb = (dpe * sc_ref[:, cols]).astype(BF16)
            dd = _dot_nt(dyb, wpb)
            s = dd / jnp.minimum(pos + 1, win).astype(F32)
            for sh in (1, 2, 4, 8)[:g + 1]:
                s = s + pltpu.roll(s, te - sh, 0)
            du_ref[:, cols] = s[:tm, :] - dd[:tm, :]
            db = d_ref[:, cols]
            dwp_ref[g] += _dot_tn(db, dyb[:tm, :])
            dsc_ref[:, cols] += _colsum(dpt * _dot(db, wpb))

        @pl.when(i == n_t - 1)
        def _():
            rider.last(ride)

    return rider.call(
        body, [dpooled, dpooled, d_b, w_pool, pscale], grid=(n_t,), name="pool_bwd",
        in_specs=[_row_spec(tm, POOL_WIDTH),
                  pl.BlockSpec((POOL_HALO, POOL_WIDTH),
                               lambda i: (jnp.minimum((i + 1) * hb, S // POOL_HALO - 1), 0)),
                  _row_spec(tm, POOL_WIDTH),
                  _const_spec((N_POOL_GROUPS, POOL_GROUP, POOL_GROUP)), _const_spec((1, POOL_WIDTH))],
        out_specs=[_row_spec(tm, POOL_WIDTH), _const_spec((N_POOL_GROUPS, POOL_GROUP, POOL_GROUP)),
                   _const_spec((1, POOL_WIDTH))],
        out_shape=[_sds((S, POOL_WIDTH), F32), _sds((N_POOL_GROUPS, POOL_GROUP, POOL_GROUP), F32),
                   _sds((1, POOL_WIDTH), F32)],
        scratch_shapes=[], vmem_mib=32)


def _attn_bwd(q, k, v, don, o_raw, ga, tq, rider):
    S = q.shape[0]
    nq = S // tq

    def body(*refs):
        ((q_ref, k_ref, v_ref, don_ref, o_ref, ga_ref), (dq_ref, dk_ref, dv_ref, dga_ref),
         (g_s, b_s, low_s, upp_s), ride) = rider.split(refs, 6, 4, 4)
        p, i = pl.program_id(0), pl.program_id(1)

        @pl.when(jnp.logical_and(p == 0, i == 0))
        def _():
            rider.first(ride)
            low_s[...] = _tri(tq, upper=False)
            upp_s[...] = _tri(tq, upper=True)

        @pl.when(i == 0)
        def _():
            for r in (dk_ref, dv_ref, dga_ref):
                r[...] = jnp.zeros_like(r)

        lane = lax.broadcasted_iota(jnp.int32, (1, PAIR), 1)
        m0 = lane < HEAD_DIM
        low = low_s[...]
        upp = upp_s[...]

        def seg_mean(a):
            s0 = jnp.sum(jnp.where(m0, a, 0.0), axis=-1, keepdims=True)
            s1 = jnp.sum(jnp.where(m0, 0.0, a), axis=-1, keepdims=True)
            return jnp.where(m0, s0, s1) * (1.0 / HEAD_DIM)

        o = o_ref[...]
        rs = lax.rsqrt(seg_mean(o * o) + RMS_EPS)
        oh = o * rs
        don = don_ref[...]
        dga_ref[...] += _colsum(don * oh)
        doh = don * ga_ref[...]
        do = rs * (doh - oh * seg_mean(doh * oh))
        dob = do.astype(BF16)
        q2 = q_ref[...]
        qhs = [jnp.where(m0, q2, jnp.zeros_like(q2)), jnp.where(m0, jnp.zeros_like(q2), q2)]
        dhs = [jnp.where(m0, dob, jnp.zeros_like(dob)), jnp.where(m0, jnp.zeros_like(dob), dob)]
        causal = _strictly_causal(tq)

        def down(kb, c_ls, valid):
            ks = pl.multiple_of(kb * tq, tq)
            kt = k_ref[pl.ds(ks, tq), :]
            vt = v_ref[pl.ds(ks, tq), :]
            lss, ls_, ws = _sb_tile(qhs, kt, low, c_ls, valid)
            dws = [_dot_nt(dh, vt) for dh in dhs]
            for hh in range(2):
                g_s[hh, kb] = dws[hh] * ws[hh]
                b_s[hh, kb] = jnp.exp(lss[hh])
            dv_ref[pl.ds(ks, tq), :] += (_dot_tn(ws[0].astype(BF16), dhs[0])
                                         + _dot_tn(ws[1].astype(BF16), dhs[1]))
            return [c_l + jnp.sum(l, axis=1, keepdims=True) for c_l, l in zip(c_ls, ls_)]

        zc, za = jnp.zeros((tq, 1), F32), jnp.zeros((tq, PAIR), F32)
        c_ls = lax.cond(i >= 1, lambda: tuple(down(i - 1, down(i, [zc, zc], causal), None)),
                        lambda: tuple(down(i, [zc, zc], causal)))

        def more(st):
            return jnp.logical_and(st[0] <= i, jnp.max(jnp.maximum(st[1], st[2])) > LOG_WEIGHT_FLOOR)

        def down_step(st):
            c_ls = down(i - st[0], [st[1], st[2]], None)
            return (st[0] + 1, c_ls[0], c_ls[1])

        n_tiles = lax.while_loop(more, down_step, (jnp.int32(2), c_ls[0], c_ls[1]))[0]

        def up(kb, c_gs, accs, valid):
            ks = pl.multiple_of(kb * tq, tq)
            kt = k_ref[pl.ds(ks, tq), :]
            gs = [g_s[hh, kb] for hh in range(2)]
            pres = [_dot(g.astype(BF16), upp) + c_g for g, c_g in zip(gs, c_gs)]
            dzs = []
            for hh in range(2):
                beta = b_s[hh, kb]
                dz = gs[hh] - beta * (gs[hh] + pres[hh])
                if valid is not None:
                    dz = jnp.where(valid, dz, 0.0)
                dzs.append(dz.astype(BF16))
            new_a = [acc + _dot(dzb, kt) for acc, dzb in zip(accs, dzs)]
            dk_ref[pl.ds(ks, tq), :] += _dot_tn(dzs[0], qhs[0]) + _dot_tn(dzs[1], qhs[1])
            new_c = [c_g + jnp.sum(g, axis=1, keepdims=True) for c_g, g in zip(c_gs, gs)]
            return new_c, new_a

        def up_step(kb, st):
            c_gs, accs = up(kb, [st[0], st[1]], [st[2], st[3]], None)
            return (c_gs[0], c_gs[1], accs[0], accs[1])

        st = lax.fori_loop(i - n_tiles + 1, i - 1, up_step, (zc, zc, za, za))

        def last_two():
            c_gs, accs = up(i - 1, [st[0], st[1]], [st[2], st[3]], None)
            return tuple(up(i, c_gs, accs, causal)[1])

        accs = lax.cond(i >= 1, last_two, lambda: tuple(up(i, [zc, zc], [za, za], causal)[1]))
        dq_ref[...] = jnp.where(m0, accs[0], accs[1]) * Q_SCALE

        @pl.when(jnp.logical_and(p == N_PAIRS - 1, i == nq - 1))
        def _():
            rider.last(ride)

    return rider.call(
        body, [q, k, v, don, o_raw, ga], grid=(N_PAIRS, nq), name="attn_bwd",
        in_specs=[pl.BlockSpec((tq, PAIR), lambda p, i: (i, p)),
                  pl.BlockSpec((S, PAIR), lambda p, i: (0, p)),
                  pl.BlockSpec((S, PAIR), lambda p, i: (0, p)),
                  pl.BlockSpec((tq, PAIR), lambda p, i: (i, p)),
                  pl.BlockSpec((tq, PAIR), lambda p, i: (i, p)),
                  pl.BlockSpec((1, PAIR), lambda p, i: (0, p))],
        out_specs=[pl.BlockSpec((tq, PAIR), lambda p, i: (i, p)),
                   pl.BlockSpec((S, PAIR), lambda p, i: (0, p)),
                   pl.BlockSpec((S, PAIR), lambda p, i: (0, p)),
                   pl.BlockSpec((1, PAIR), lambda p, i: (0, p))],
        out_shape=[_sds((S, ATTN_WIDTH), F32), _sds((S, ATTN_WIDTH), F32), _sds((S, ATTN_WIDTH), F32),
                   _sds((1, ATTN_WIDTH), F32)],
        scratch_shapes=[pltpu.VMEM((2, nq, tq, tq), F32), pltpu.VMEM((2, nq, tq, tq), F32),
                        pltpu.VMEM((tq, tq), BF16), pltpu.VMEM((tq, tq), BF16)],
        vmem_mib=56)


def _inproj_bwd(dq, dk, dv, du, dpre1, xh0, rstd0, g0, b0, w_in_s, tm):
    S, D = xh0.shape

    def body(dq_ref, dk_ref, dv_ref, du_ref, dp1_ref, xh_ref, rstd_ref, g_ref, b_ref, w_ref,
             gx_ref, dw_ref, dg_ref, db_ref):
        @pl.when(pl.program_id(0) == 0)
        def _():
            for r in (dw_ref, dg_ref, db_ref):
                r[...] = jnp.zeros_like(r)

        xh = xh_ref[...]
        xb = (xh * g_ref[...] + b_ref[...]).astype(BF16)
        dx0 = ALPHA * dp1_ref[...]
        for c, r in enumerate((dq_ref, dk_ref, dv_ref, du_ref)):
            dpb = r[...].astype(BF16)
            dx0 = dx0 + _dot_nt(dpb, w_ref[c])
            dw_ref[c] += _dot_tn(xb, dpb)
        dg_ref[...] += _colsum(dx0 * xh)
        db_ref[...] += _colsum(dx0)
        gx_ref[...] = _ln_bwd(dx0, xh, rstd_ref[...], g_ref[...])

    vec = _const_spec((1, D))
    half = _row_spec(tm, 512)
    return pl.pallas_call(
        body, grid=(S // tm,), name="inproj_bwd",
        in_specs=[half, half, half, half, _row_spec(tm, D), _row_spec(tm, D), _row_spec(tm, 1), vec, vec,
                  _const_spec((N_CHIPS, D, 512))],
        out_specs=[_row_spec(tm, D), _const_spec((N_CHIPS, D, 512)), vec, vec],
        out_shape=[_sds((S, D), F32), _sds((N_CHIPS, D, 512), F32), _sds((1, D), F32), _sds((1, D), F32)],
        compiler_params=_cp(56),
    )(*_hbm(dq, dk, dv, du, dpre1, xh0, rstd0, g0, b0, w_in_s))


def _place():
    return lax.axis_index("x"), lax.axis_index("y"), lax.axis_index("c")


CHIP_FLIPS = ((0, 1), (1, 0), (1, 1))


class _Rider:
    def __init__(self, ins, out_shapes, n_sem, phases, aliases=None):
        self.ins, self.out_shapes, self.n_sem, self.phases = list(ins), list(out_shapes), n_sem, phases
        self.aliases = aliases or {}

    def __add__(self, other):
        na, ma = len(self.ins), len(self.out_shapes)

        def phases(ins, outs, ssem, rsem):
            mine = self.phases(ins[:na], outs[:ma], ssem, rsem)
            rest = pl.ds(self.n_sem, other.n_sem)
            theirs = other.phases(ins[na:], outs[ma:], ssem.at[rest], rsem.at[rest])
            assert len(mine) == 1 and len(theirs) == 1
            return [mine[0] + theirs[0]]

        aliases = {**self.aliases, **{na + i: ma + o for i, o in other.aliases.items()}}
        return _Rider(self.ins + other.ins, self.out_shapes + other.out_shapes, self.n_sem + other.n_sem, phases,
                      aliases)

    def split(self, refs, n_in, n_out, n_scratch):
        a = n_in + len(self.ins)
        b = a + n_out
        c = b + len(self.out_shapes)
        own = (refs[:n_in], refs[a:b], refs[c:c + n_scratch])
        return own + ((refs[n_in:a], refs[b:c]) + tuple(refs[c + n_scratch:]),)

    def first(self, ride):
        for make in self.phases(*ride)[0]:
            make().start()

    def mid(self, ride):
        ph = self.phases(*ride)
        if len(ph) == 2:
            for make in ph[0]:
                make().wait_recv()
            for make in ph[1]:
                make().start()

    def last(self, ride):
        ph = self.phases(*ride)
        if len(ph) == 2:
            for make in ph[0]:
                make().wait_send()
        for make in ph[-1]:
            make().wait()

    def call(self, body, args, *, grid, name, in_specs, out_specs, out_shape, scratch_shapes, vmem_mib,
             prefetch=None):
        n_in, n_out = len(in_specs), len(out_specs)
        sems = [pltpu.SemaphoreType.DMA((self.n_sem,)), pltpu.SemaphoreType.DMA((self.n_sem,))]
        n_pre = 0 if prefetch is None else 1
        grid_spec = pltpu.PrefetchScalarGridSpec(
            num_scalar_prefetch=n_pre, grid=grid,
            in_specs=list(in_specs) + [HBM_SPEC] * len(self.ins),
            out_specs=list(out_specs) + [HBM_SPEC] * len(self.out_shapes),
            scratch_shapes=list(scratch_shapes) + sems)
        return pl.pallas_call(
            body, name=name, grid_spec=grid_spec,
            out_shape=list(out_shape) + self.out_shapes,
            input_output_aliases={n_pre + n_in + i: n_out + o for i, o in self.aliases.items()},
            compiler_params=_cp(vmem_mib),
        )(*([] if prefetch is None else [prefetch]), *_hbm(*args), *self.ins)

    def run(self, name):
        def body(*refs):
            ride = self.split(refs, 0, 0, 0)[3]
            self.first(ride)
            self.mid(ride)
            self.last(ride)

        return self.call(body, [], grid=(), name=name, in_specs=[], out_specs=[], out_shape=[],
                         scratch_shapes=[], vmem_mib=16)


def _remote(src, dst, ssem, rsem, n, dev):
    return functools.partial(pltpu.make_async_remote_copy, src_ref=src, dst_ref=dst, send_sem=ssem.at[n],
                             recv_sem=rsem.at[n], device_id=dev, device_id_type=MESH)


def _cast_into_slot(w, place, name):
    R, C = w.shape
    tr = min(R, 512)

    def body(pl_ref, w_ref, o_ref):
        o_ref[0] = w_ref[...].astype(BF16)

    return pl.pallas_call(
        body, name=name,
        grid_spec=pltpu.PrefetchScalarGridSpec(
            num_scalar_prefetch=1, grid=(R // tr,),
            in_specs=[pl.BlockSpec((tr, C), lambda r, pr: (r, 0))],
            out_specs=pl.BlockSpec((1, tr, C), lambda r, pr: (pr[1], r, 0))),
        out_shape=_sds((N_CHIPS, R, C), BF16),
    )(place, w)


CAST_STEPS = 8


def _cast_rest(ws, place, rider):
    n = len(ws)

    def body(pl_ref, *refs):
        w_refs, o_refs, _, ride = rider.split(refs, n, n, 0)
        r = pl.program_id(0)

        @pl.when(r == 0)
        def _():
            rider.first(ride)

        @pl.when(r == CAST_STEPS // 2)
        def _():
            rider.mid(ride)

        for w_ref, o_ref in zip(w_refs, o_refs):
            o_ref[0] = w_ref[...].astype(BF16)

        @pl.when(r == CAST_STEPS - 1)
        def _():
            rider.last(ride)

    def rows(w):
        return w.shape[0] // CAST_STEPS

    return rider.call(
        body, ws, grid=(CAST_STEPS,), name="cast_weights", prefetch=place,
        in_specs=[pl.BlockSpec((rows(w), w.shape[1]), lambda r, pr: (r, 0)) for w in ws],
        out_specs=[pl.BlockSpec((1, rows(w), w.shape[1]), lambda r, pr: (pr[1], r, 0)) for w in ws],
        out_shape=[_sds((N_CHIPS,) + w.shape, BF16) for w in ws], scratch_shapes=[], vmem_mib=32)


def _gather_rider(stacked, part="both"):
    n, nf = len(stacked), len(CHIP_FLIPS)

    def phases(ins, outs, ssem, rsem):
        x, y, c = _place()
        slot = 2 * x + y
        ici, d2d = [], []
        for w, (i_ref, o_ref) in enumerate(zip(ins, outs)):
            hh = o_ref.shape[1] // 2
            rows = pl.ds(c * hh, hh)
            for f, (fx, fy) in enumerate(CHIP_FLIPS):
                k = w * nf + f
                theirs = 2 * (x ^ fx) + (y ^ fy)
                if part != "pair":
                    ici.append(_remote(i_ref.at[slot, rows], o_ref.at[slot, rows], ssem, rsem, k,
                                       (x ^ fx, y ^ fy, c)))
                if part != "chips":
                    d2d.append(_remote(o_ref.at[theirs, rows], o_ref.at[theirs, rows], ssem, rsem,
                                       (n * nf if part == "both" else 0) + k, (x, y, 1 - c)))
        return [ph for ph in (ici, d2d) if ph]

    return _Rider(stacked, [_sds(s.shape, s.dtype) for s in stacked], (2 if part == "both" else 1) * n * nf,
                  phases, aliases={i: i for i in range(n)})


def _pair_swap_rider(grads):
    def phases(ins, outs, ssem, rsem):
        x, y, c = _place()
        return [[_remote(g.at[:, 1 - c], o, ssem, rsem, k, (x, y, 1 - c))
                 for k, (g, o) in enumerate(zip(ins, outs))]]

    return _Rider(grads, [_sds((N_CHIPS,) + g.shape[2:], g.dtype) for g in grads], len(grads), phases)


def _chip_scatter_rider(parts):
    nf = len(CHIP_FLIPS)

    def phases(ins, outs, ssem, rsem):
        x, y, c = _place()
        return [[_remote(r.at[2 * (x ^ fx) + (y ^ fy)], o.at[f], ssem, rsem, w * nf + f, (x ^ fx, y ^ fy, c))
                 for w, (r, o) in enumerate(zip(ins, outs)) for f, (fx, fy) in enumerate(CHIP_FLIPS)]]

    return _Rider(parts, [_sds((nf,) + r.shape[1:], r.dtype) for r in parts], len(parts) * nf, phases)


def _pair_send_rider(halves):
    def phases(ins, outs, ssem, rsem):
        x, y, c = _place()
        return [[_remote(h, o, ssem, rsem, k, (x, y, 1 - c)) for k, (h, o) in enumerate(zip(ins, outs))]]

    return _Rider(halves, [_sds(h.shape, h.dtype) for h in halves], len(halves), phases)


PAIR_SUM_STEPS = 2
CHIP_SUM_STEPS = 4
ADAMW_STEPS = 4


def _no_rider():
    return _Rider([], [], 1, lambda ins, outs, ssem, rsem: [[]])


def _add_pair(grads, recvs, place, name, rider):
    n = len(grads)

    def body(pl_ref, *refs):
        ins, outs, _, ride = rider.split(refs, 2 * n, 2 * n, 0)
        j, h = pl.program_id(0), pl.program_id(1)

        @pl.when(jnp.logical_and(j == 0, h == 0))
        def _():
            rider.first(ride)

        for w in range(n):
            s = ins[2 * w][:, 0] + ins[2 * w + 1][...]
            outs[2 * w][...] = s
            outs[2 * w + 1][...] = s.astype(BF16)

        @pl.when(jnp.logical_and(j == N_CHIPS - 1, h == PAIR_SUM_STEPS - 1))
        def _():
            rider.last(ride)

    in_specs, out_specs, out_shape, args = [], [], [], []
    for g, r in zip(grads, recvs):
        _, _, H, C = g.shape
        th = H // PAIR_SUM_STEPS
        spec = pl.BlockSpec((1, th, C), lambda j, h, pr: (j, h, 0))
        in_specs += [pl.BlockSpec((1, 1, th, C), lambda j, h, pr: (j, pr[0], h, 0)), spec]
        out_specs += [spec, spec]
        out_shape += [_sds((N_CHIPS, H, C), F32), _sds((N_CHIPS, H, C), BF16)]
        args += [g, r]
    res = rider.call(body, args, grid=(N_CHIPS, PAIR_SUM_STEPS), name=name, prefetch=place, in_specs=in_specs,
                     out_specs=out_specs, out_shape=out_shape, scratch_shapes=[], vmem_mib=32)
    return [(res[2 * w], res[2 * w + 1]) for w in range(n)], res[2 * n:]


def _add_chips(parts, recvs, place, name, rider):
    n = len(parts)

    def body(pl_ref, *refs):
        ins, outs, _, ride = rider.split(refs, 2 * n, n, 0)
        h = pl.program_id(0)

        @pl.when(h == 0)
        def _():
            rider.first(ride)

        for w in range(n):
            p_ref, r_ref = ins[2 * w], ins[2 * w + 1]
            outs[w][...] = p_ref[0] + r_ref[0].astype(F32) + r_ref[1].astype(F32) + r_ref[2].astype(F32)

        @pl.when(h == CHIP_SUM_STEPS - 1)
        def _():
            rider.last(ride)

    in_specs, out_specs, out_shape, args = [], [], [], []
    for p, r in zip(parts, recvs):
        _, H, C = p.shape
        th = H // CHIP_SUM_STEPS
        in_specs += [pl.BlockSpec((1, th, C), lambda h, pr: (pr[1], h, 0)),
                     pl.BlockSpec((len(CHIP_FLIPS), th, C), lambda h, pr: (0, h, 0))]
        out_specs.append(pl.BlockSpec((th, C), lambda h, pr: (h, 0)))
        out_shape.append(_sds((H, C), F32))
        args += [p, r]
    res = rider.call(body, args, grid=(CHIP_SUM_STEPS,), name=name, prefetch=place, in_specs=in_specs,
                     out_specs=out_specs, out_shape=out_shape, scratch_shapes=[], vmem_mib=32)
    return res[:n], res[n:]


def _adamw_math(w, g, m, v):
    m = ADAM_B1 * m + (1.0 - ADAM_B1) * g
    v = ADAM_B2 * v + (1.0 - ADAM_B2) * (g * g)
    m_hat = m / (1.0 - ADAM_B1 ** ADAM_STEP)
    v_hat = v / (1.0 - ADAM_B2 ** ADAM_STEP)
    delta = -ADAM_LR * (m_hat / (jnp.sqrt(v_hat) + ADAM_EPS) + ADAM_WD * w)
    return delta, m, v


def _adamw(ws, mines, theirs, ms, vs, place, name, rider):
    n = len(ws)

    def body(pl_ref, *refs):
        ins, outs, _, ride = rider.split(refs, 5 * n, 4 * n, 0)
        h, r = pl.program_id(0), pl.program_id(1)

        @pl.when(jnp.logical_and(h == 0, r == 0))
        def _():
            rider.first(ride)

        for k in range(n):
            w_ref, a_ref, b_ref, m_ref, v_ref = ins[5 * k:5 * k + 5]
            g = jnp.where(h == pl_ref[0], a_ref[...], b_ref[...])
            d, mo, vo = _adamw_math(w_ref[...], g, m_ref[...], v_ref[...])
            for o_ref, val in zip(outs[4 * k:4 * k + 4], (g, d, mo, vo)):
                o_ref[...] = val

        @pl.when(jnp.logical_and(h == 1, r == ADAMW_STEPS - 1))
        def _():
            rider.last(ride)

    in_specs, out_specs, out_shape, args = [], [], [], []
    for w, a, b, m, v in zip(ws, mines, theirs, ms, vs):
        R, C = w.shape
        th = (R // 2) // ADAMW_STEPS
        whole = pl.BlockSpec((th, C), lambda h, r, pr: (h * ADAMW_STEPS + r, 0))
        mine_spec = pl.BlockSpec((th, C), lambda h, r, pr: (jnp.where(h == pr[0], r, 0), 0))
        theirs_spec = pl.BlockSpec((th, C), lambda h, r, pr: (jnp.where(h == pr[0], 0, r), 0))
        in_specs += [whole, mine_spec, theirs_spec, whole, whole]
        out_specs += [whole] * 4
        out_shape += [_sds((R, C), F32)] * 4
        args += [w, a, b, m, v]
    res = rider.call(body, args, grid=(2, ADAMW_STEPS), name=name, prefetch=place, in_specs=in_specs,
                     out_specs=out_specs, out_shape=out_shape, scratch_shapes=[], vmem_mib=40)
    return [tuple(res[4 * k:4 * k + 4]) for k in range(n)], res[4 * n:]


DEVICE_FLIPS = tuple((fx, fy, fc) for fx in (0, 1) for fy in (0, 1) for fc in (0, 1))[1:]


def _pack_exchange_rider(pack):
    def phases(ins, outs, ssem, rsem):
        x, y, c = _place()
        mine = outs[0].at[4 * x + 2 * y + c]
        copies = [_remote(ins[0], mine, ssem, rsem, k, (x ^ fx, y ^ fy, c ^ fc))
                  for k, (fx, fy, fc) in enumerate(DEVICE_FLIPS)]
        copies.append(functools.partial(pltpu.make_async_copy, ins[0], mine, ssem.at[len(DEVICE_FLIPS)]))
        return [copies]

    return _Rider([pack], [_sds((N_DEV,) + pack.shape, pack.dtype)], len(DEVICE_FLIPS) + 1, phases)


def _small_sum_adamw(recv_a, recv_b, wpack, mpack, vpack):
    R = wpack.shape[0]

    def body(a_ref, b_ref, w_ref, m_ref, v_ref, gs_ref, d_ref, mo_ref, vo_ref):
        ta, tb = a_ref[0], b_ref[0]
        for dev in range(1, N_DEV):
            ta = ta + a_ref[dev]
            tb = tb + b_ref[dev]
        total = jnp.concatenate([ta, tb], axis=0)
        gs_ref[...] = total
        d, mo, vo = _adamw_math(w_ref[...], total, m_ref[...], v_ref[...])
        d_ref[...] = d
        mo_ref[...] = mo
        vo_ref[...] = vo

    return pl.pallas_call(
        body, name="small_sum_adamw", in_specs=[VMEM_SPEC] * 5, out_specs=[VMEM_SPEC] * 4,
        out_shape=[_sds((R, LANES), F32)] * 4,
    )(recv_a, recv_b, wpack, mpack, vpack)


def _rows8(a):
    a = a.reshape(-1, LANES)
    pad = (-a.shape[0]) % 8
    return jnp.pad(a, ((0, pad), (0, 0))) if pad else a


def _pack(parts):
    return jnp.concatenate([_rows8(a) for a in parts], axis=0)


def _unpack(pack, like):
    out, row = [], 0
    for a in like:
        n = a.size // LANES
        out.append(pack[row:row + n].reshape(a.shape))
        row += n + (-n) % 8
    return out


def kernel(x, p, emb_ln_g, emb_ln_b, w_in, attn_out_g, w_pool, pool_scale, w_out, ln1_g, ln1_b, w_up, w_down, ln2_g, ln2_b, w_ple, w_ple_gate, ln3_g, ln3_b, loss_target, m_emb_ln_g, m_emb_ln_b, m_w_in, m_attn_out_g, m_w_pool, m_pool_scale, m_w_out, m_ln1_g, m_ln1_b, m_w_up, m_w_down, m_ln2_g, m_ln2_b, m_w_ple, m_w_ple_gate, m_ln3_g, m_ln3_b, v_emb_ln_g, v_emb_ln_b, v_w_in, v_attn_out_g, v_w_pool, v_pool_scale, v_w_out, v_ln1_g, v_ln1_b, v_w_up, v_w_down, v_ln2_g, v_ln2_b, v_w_ple, v_w_ple_gate, v_ln3_g, v_ln3_b):
    S = x.shape[1]
    tm = min(256, S)
    tq = min(256, S)
    tm_mlp = min(1024, S)
    tm_pool = min(1024, S)
    xs = x[0]
    ps = p[0, 0]
    tgt = loss_target[0]
    row = lambda a: a.reshape(1, -1)
    g0, b0 = row(emb_ln_g), row(emb_ln_b)
    g1, b1, g2, b2, g3, b3 = ln1_g, ln1_b, ln2_g, ln2_b, ln3_g, ln3_b
    wp = w_pool[0]

    xi, yi, ci = _place()
    place = jnp.stack([ci, 2 * xi + yi]).astype(jnp.int32)
    names = ["w_in", "w_out", "w_up", "w_down", "w_ple", "w_ple_gate"]

    big = [w_in[0], w_out[0], w_up[0], w_down[0], w_ple[0], w_ple_gate[0]]
    s_in = _cast_into_slot(big[0], place, "cast_w_in")
    s_out, s_up, s_down, s_ple, s_gate, w_in_s = _cast_rest(big[1:], place, _gather_rider([s_in]))

    xh0, rstd0, q, k, v, u, s_out, s_ple, s_gate = _embln_inproj(
        xs, g0, b0, w_in_s, tm, _gather_rider([s_out, s_ple, s_gate], "chips"))
    o_raw, on, s_up, s_down, w_out_s, w_ple_s, w_gate_s = _attn_fwd(
        q, k, v, attn_out_g, tq, _gather_rider([s_up, s_down], "chips") + _gather_rider([s_out, s_ple, s_gate], "pair"))
    w_out_f = w_out_s.reshape(D_MODEL, D_MODEL)
    w_gate_f = w_gate_s.reshape(D_MODEL, D_MODEL)
    d_b, pooled = _pool_fwd(u, wp, pool_scale, tm_pool)
    xh1, rstd1, x1b, w_up_s, w_down_s = _mix_ln1(on, pooled, xh0, g0, b0, w_out_f, g1, b1, tm,
                                                 _gather_rider([s_up, s_down], "pair"))
    w_down_f = w_down_s.reshape(D_FF, D_MODEL)
    xh2, rstd2, rb = _mlp_ln2(xh1, x1b, g1, b1, w_up_s, w_down_f, tm_mlp)

    (dpre2, dhb, dw_ple, dw_gate, dg3, db3, dg2, db2, loss_row) = _ple_ln3_loss(
        xh2, rstd2, g2, b2, ps, w_ple_s, w_gate_f, g3, b3, tgt, tm)
    def halves_of(g):
        return g.reshape(N_CHIPS, 2, g.shape[1] // 2, g.shape[2])

    ple_halves = [halves_of(dw_ple), halves_of(dw_gate.reshape(N_CHIPS, D_MODEL // N_CHIPS, D_MODEL))]
    dx1m, da, *ple_pair = _mlp_bwd(rb, dhb, w_up_s, w_down_f, tm_mlp, _pair_swap_rider(ple_halves))
    (dw_up,) = _tn_matmul(x1b, da, "grad_w_up", 1024, min(512, S), True, _no_rider())
    up_halves = halves_of(dw_up)
    dw_down, up_pair = _tn_matmul(rb, dhb, "grad_w_down", 1024, min(512, S), False,
                                  _pair_swap_rider([up_halves]), square_a=True)
    down_halves = halves_of(dw_down.reshape(N_CHIPS, D_FF // N_CHIPS, D_MODEL))
    dpre1, don, dpooled, dw_out, dg1, db1, down_pair = _mix_bwd(
        dpre2, dx1m, xh1, rstd1, g1, w_out_f, on, pooled, tm, _pair_swap_rider([down_halves]))
    early_sum, _ = _add_pair([up_halves, down_halves] + ple_halves, [up_pair, down_pair] + ple_pair, place,
                             "pair_sum_mlp_ple", _no_rider())
    out_halves = halves_of(dw_out.reshape(N_CHIPS, D_MODEL // N_CHIPS, D_MODEL))
    du, dwp, dsc, out_pair = _pool_bwd(dpooled, d_b, wp, pool_scale, tm_pool, _pair_swap_rider([out_halves]))
    (out_sum,), _ = _add_pair([out_halves], [out_pair], place, "pair_sum_w_out", _no_rider())
    pack_a = _pack([jnp.broadcast_to(loss_row, (8, LANES)), dwp, dsc, dg1, db1, dg2, db2, dg3, db3])
    early_sum = [out_sum] + early_sum
    riding = _chip_scatter_rider([b for _, b in early_sum]) + _pack_exchange_rider(pack_a)
    dq, dk, dv, dga, *arrived = _attn_bwd(q, k, v, don, o_raw, attn_out_g, tq, riding)
    early_chips, recv_a = arrived[:-1], arrived[-1]
    grad_x, dw_in, dg0, db0 = _inproj_bwd(dq, dk, dv, du, dpre1, xh0, rstd0, g0, b0, w_in_s, tm)

    in_halves = halves_of(dw_in)
    pack_b = _pack([dg0, db0, dga])
    early_mine, (in_pair, recv_b) = _add_chips(
        [s for s, _ in early_sum], early_chips, place, "chip_sum_early",
        _pair_swap_rider([in_halves]) + _pack_exchange_rider(pack_b))
    (in_sum,), early_theirs = _add_pair([in_halves], [in_pair], place, "pair_sum_w_in", _pair_send_rider(early_mine))
    ms = [m_w_in, m_w_out, m_w_up, m_w_down, m_w_ple, m_w_ple_gate]
    vs = [v_w_in, v_w_out, v_w_up, v_w_down, v_w_ple, v_w_ple_gate]
    early_res, _ = _adamw(big[1:], early_mine, early_theirs, [m[0] for m in ms[1:]], [v[0] for v in vs[1:]],
                          place, "adamw_early", _no_rider())
    (in_chips,) = _chip_scatter_rider([in_sum[1]]).run("reduce_chips_late")
    (in_mine,), _ = _add_chips([in_sum[0]], [in_chips], place, "chip_sum_w_in", _no_rider())
    (in_theirs,) = _pair_send_rider([in_mine]).run("gather_pair_w_in")
    in_res, _ = _adamw(big[:1], [in_mine], [in_theirs], [ms[0][0]], [vs[0][0]], place, "adamw_w_in", _no_rider())
    big_out = {n: tuple(r.reshape(m.shape) for r in res4) for n, res4, m in zip(names, in_res + early_res, ms)}

    small_names = ["w_pool", "pool_scale", "ln1_g", "ln1_b", "ln2_g", "ln2_b", "ln3_g", "ln3_b",
                   "emb_ln_g", "emb_ln_b", "attn_out_g"]
    small_w = [w_pool, pool_scale, ln1_g, ln1_b, ln2_g, ln2_b, ln3_g, ln3_b, emb_ln_g, emb_ln_b, attn_out_g]
    small_m = [m_w_pool, m_pool_scale, m_ln1_g, m_ln1_b, m_ln2_g, m_ln2_b, m_ln3_g, m_ln3_b,
               m_emb_ln_g, m_emb_ln_b, m_attn_out_g]
    small_v = [v_w_pool, v_pool_scale, v_ln1_g, v_ln1_b, v_ln2_g, v_ln2_b, v_ln3_g, v_ln3_b,
               v_emb_ln_g, v_emb_ln_b, v_attn_out_g]
    loss_like = jnp.zeros((8, LANES), F32)
    gs, ds, mos, vos = _small_sum_adamw(recv_a, recv_b, _pack([loss_like] + small_w), _pack([loss_like] + small_m),
                                        _pack([jnp.ones((8, LANES), F32)] + small_v))
    like = [loss_like] + small_w
    gs_u, ds_u, mos_u, vos_u = (_unpack(a, like) for a in (gs, ds, mos, vos))
    loss = gs_u[0][0, 0]
    small_out = {n: (gs_u[i + 1], ds_u[i + 1], mos_u[i + 1], vos_u[i + 1]) for i, n in enumerate(small_names)}

    order = ["emb_ln_g", "emb_ln_b", "w_in", "attn_out_g", "w_pool", "pool_scale", "w_out", "ln1_g", "ln1_b",
             "w_up", "w_down", "ln2_g", "ln2_b", "w_ple", "w_ple_gate", "ln3_g", "ln3_b"]
    res = {**big_out, **small_out}
    outs = [loss, grad_x.reshape(x.shape)]
    for kind in range(4):
        outs += [res[n][kind] for n in order]
    return tuple(outs)
```

```python
import functools

import jax
import jax.numpy as jnp
from jax import lax
from jax.experimental import pallas as pl
from jax.experimental.pallas import tpu as pltpu

F32 = jnp.float32
BF16 = jnp.bfloat16

D_MODEL = 1024
ATTN_WIDTH = 512
POOL_WIDTH = 512
HEAD_DIM = 64
PAIR = 2 * HEAD_DIM
N_PAIRS = ATTN_WIDTH // PAIR
N_POOL_GROUPS = 4
POOL_GROUP = 128
POOL_HALO = 16
D_FF = 4096
PLE_DIM = 256
N_CHIPS = 4
N_DEV = 8
LN_EPS = 1e-5
RMS_EPS = 1e-6
ALPHA = float(2.0 ** 0.25)
Q_SCALE = 0.125
ADAM_LR = 0.001
ADAM_B1 = 0.9
ADAM_B2 = 0.999
ADAM_EPS = 1e-08
ADAM_WD = 0.01
ADAM_STEP = 10
LANES = 128
MIB = 1024 * 1024

MESH = pl.DeviceIdType.MESH
HBM_SPEC = pl.BlockSpec(memory_space=pltpu.HBM)
VMEM_SPEC = pl.BlockSpec(memory_space=pltpu.VMEM)


def _cp(vmem_mib):
    return pltpu.CompilerParams(vmem_limit_bytes=vmem_mib * MIB)


def _dot(a, b):
    return jnp.dot(a, b, preferred_element_type=F32)


def _dot_nt(a, b):
    return lax.dot_general(a, b, (((1,), (1,)), ((), ())), preferred_element_type=F32)


def _dot_tn(a, b):
    return lax.dot_general(a, b, (((0,), (0,)), ((), ())), preferred_element_type=F32)


def _ln_fwd(pre):
    mu = jnp.mean(pre, axis=-1, keepdims=True)
    xc = pre - mu
    var = jnp.mean(xc * xc, axis=-1, keepdims=True)
    rstd = lax.rsqrt(var + LN_EPS)
    return xc * rstd, rstd


def _ln_bwd(dy, xh, rstd, g):
    dxh = dy * g
    m1 = jnp.mean(dxh, axis=-1, keepdims=True)
    m2 = jnp.mean(dxh * xh, axis=-1, keepdims=True)
    return rstd * (dxh - m1 - xh * m2)


def _colsum(a):
    return jnp.sum(a, axis=0, keepdims=True)


def _neg_softplus(z):
    return -(jnp.maximum(z, 0.0) + jnp.log(1.0 + jnp.exp(-jnp.abs(z))))


def _split_bf16(a):
    hi = a.astype(BF16)
    lo = (a - hi.astype(F32)).astype(BF16)
    return hi, lo


def _row_spec(tm, n):
    return pl.BlockSpec((tm, n), lambda i: (i, 0))


def _const_spec(shape):
    nd = len(shape)
    return pl.BlockSpec(shape, lambda *_: (0,) * nd)


def _hbm(*arrays):
    return [pltpu.with_memory_space_constraint(a, pltpu.HBM) for a in arrays]


def _sds(shape, dtype):
    return pltpu.HBM(shape, dtype)


def _embln_inproj(x, g0, b0, w_in_s, tm, rider):
    S, D = x.shape
    n_t = S // tm

    def body(*refs):
        ((x_ref, g_ref, b_ref, w_ref), (xh_ref, rstd_ref, q_ref, k_ref, v_ref, u_ref), _,
         ride) = rider.split(refs, 4, 6, 0)
        i = pl.program_id(0)

        @pl.when(i == 0)
        def _():
            rider.first(ride)

        @pl.when(i == (3 * n_t) // 4)
        def _():
            rider.mid(ride)

        xh, rstd = _ln_fwd(x_ref[...])
        xh_ref[...] = xh
        rstd_ref[...] = rstd
        xb = (xh * g_ref[...] + b_ref[...]).astype(BF16)
        q_ref[...] = (_dot(xb, w_ref[0]) * Q_SCALE).astype(BF16)
        k_ref[...] = _dot(xb, w_ref[1]).astype(BF16)
        v_ref[...] = _dot(xb, w_ref[2]).astype(BF16)
        u_ref[...] = _dot(xb, w_ref[3])

        @pl.when(i == n_t - 1)
        def _():
            rider.last(ride)

    return rider.call(
        body, [x, g0, b0, w_in_s], grid=(n_t,), name="embln_inproj",
        in_specs=[_row_spec(tm, D), _const_spec((1, D)), _const_spec((1, D)),
                  _const_spec((N_CHIPS, D, 512))],
        out_specs=[_row_spec(tm, D), _row_spec(tm, 1), _row_spec(tm, 512), _row_spec(tm, 512),
                   _row_spec(tm, 512), _row_spec(tm, 512)],
        out_shape=[_sds((S, D), F32), _sds((S, 1), F32), _sds((S, 512), BF16), _sds((S, 512), BF16),
                   _sds((S, 512), BF16), _sds((S, 512), F32)],
        scratch_shapes=[], vmem_mib=40)


def _tri(n, upper):
    r = lax.broadcasted_iota(jnp.int32, (n, n), 0)
    c = lax.broadcasted_iota(jnp.int32, (n, n), 1)
    keep = (r < c) if upper else (r > c)
    return jnp.where(keep, 1.0, 0.0).astype(BF16)


def _strictly_causal(n):
    return lax.broadcasted_iota(jnp.int32, (n, n), 1) < lax.broadcasted_iota(jnp.int32, (n, n), 0)


LOG_WEIGHT_FLOOR = -110.0


def _sb_tile(qhs, kt, low, c_ls, valid):
    valids = valid if isinstance(valid, (list, tuple)) else [valid] * len(qhs)
    zs = [_dot_nt(qh, kt) for qh in qhs]
    lrs = [_neg_softplus(z) for z in zs]
    ls_ = [lr if m is None else jnp.where(m, lr, 0.0) for lr, m in zip(lrs, valids)]
    sfx = [_dot(l.astype(BF16), low) + c_l for l, c_l in zip(ls_, c_ls)]
    lss = [z + lr for z, lr in zip(zs, lrs)]
    ws = [jnp.exp(ls + s) for ls, s in zip(lss, sfx)]
    ws = [w if m is None else jnp.where(m, w, 0.0) for w, m in zip(ws, valids)]
    return lss, ls_, ws


def _attn_fwd(q, k, v, ga, tq, rider):
    S = q.shape[0]
    nq = S // tq

    def body(*refs):
        (q_ref, k_ref, v_ref, ga_ref), (o_ref, on_ref), (low_s,), ride = rider.split(refs, 4, 2, 1)
        p, i = pl.program_id(0), pl.program_id(1)

        @pl.when(jnp.logical_and(p == 0, i == 0))
        def _():
            rider.first(ride)
            low_s[...] = _tri(tq, upper=False)

        @pl.when(jnp.logical_and(p == N_PAIRS - 1, i == 0))
        def _():
            rider.mid(ride)

        lane = lax.broadcasted_iota(jnp.int32, (1, PAIR), 1)
        m0 = lane < HEAD_DIM
        low = low_s[...]
        q2 = q_ref[...]
        qhs = [jnp.where(m0, q2, jnp.zeros_like(q2)), jnp.where(m0, jnp.zeros_like(q2), q2)]

        def tile(kb, c_ls, accs, valid):
            ks = pl.multiple_of(kb * tq, tq)
            kt = k_ref[pl.ds(ks, tq), :]
            vt = v_ref[pl.ds(ks, tq), :]
            _, ls_, ws = _sb_tile(qhs, kt, low, c_ls, valid)
            new_a = [acc + _dot(w.astype(BF16), vt) for acc, w in zip(accs, ws)]
            new_c = [c_l + jnp.sum(l, axis=1, keepdims=True) for c_l, l in zip(c_ls, ls_)]
            return new_c, new_a

        zc, za = jnp.zeros((tq, 1), F32), jnp.zeros((tq, PAIR), F32)

        def first_two():
            c_ls, accs = tile(i, [zc, zc], [za, za], _strictly_causal(tq))
            c_ls, accs = tile(i - 1, c_ls, accs, None)
            return (*c_ls, *accs)

        def first_one():
            c_ls, accs = tile(i, [zc, zc], [za, za], _strictly_causal(tq))
            return (*c_ls, *accs)

        st0 = lax.cond(i >= 1, first_two, first_one)

        def more(st):
            return jnp.logical_and(st[0] <= i, jnp.max(jnp.maximum(st[1], st[2])) > LOG_WEIGHT_FLOOR)

        def step(st):
            n, c0, c1, a0, a1 = st
            c_ls, accs = tile(i - n, [c0, c1], [a0, a1], None)
            return (n + 1, c_ls[0], c_ls[1], accs[0], accs[1])

        st = lax.while_loop(more, step, (jnp.int32(2), *st0))
        o = jnp.where(m0, st[3], st[4])
        o_ref[...] = o
        sq = o * o
        ms0 = jnp.sum(jnp.where(m0, sq, 0.0), axis=-1, keepdims=True) * (1.0 / HEAD_DIM)
        ms1 = jnp.sum(jnp.where(m0, 0.0, sq), axis=-1, keepdims=True) * (1.0 / HEAD_DIM)
        rs = jnp.where(m0, lax.rsqrt(ms0 + RMS_EPS), lax.rsqrt(ms1 + RMS_EPS))
        on_ref[...] = (o * rs * ga_ref[...]).astype(BF16)

        @pl.when(jnp.logical_and(p == N_PAIRS - 1, i == nq - 1))
        def _():
            rider.last(ride)

    return rider.call(
        body, [q, k, v, ga], grid=(N_PAIRS, nq), name="attn_fwd",
        in_specs=[pl.BlockSpec((tq, PAIR), lambda p, i: (i, p)),
                  pl.BlockSpec((S, PAIR), lambda p, i: (0, p)),
                  pl.BlockSpec((S, PAIR), lambda p, i: (0, p)),
                  pl.BlockSpec((1, PAIR), lambda p, i: (0, p))],
        out_specs=[pl.BlockSpec((tq, PAIR), lambda p, i: (i, p)),
                   pl.BlockSpec((tq, PAIR), lambda p, i: (i, p))],
        out_shape=[_sds((S, ATTN_WIDTH), F32), _sds((S, ATTN_WIDTH), BF16)],
        scratch_shapes=[pltpu.VMEM((tq, tq), BF16)], vmem_mib=40)


def _pool_fwd(u, w_pool, pscale, tm):
    S = u.shape[0]
    hb = tm // POOL_HALO

    def body(u_ref, uh_ref, wp_ref, sc_ref, d_ref, pooled_ref):
        i = pl.program_id(0)
        halo = jnp.where(i > 0, uh_ref[...], 0.0)
        pos = i * tm + lax.broadcasted_iota(jnp.int32, (tm, 1), 0)
        for g in range(N_POOL_GROUPS):
            win = 2 ** (g + 1)
            cols = slice(g * POOL_GROUP, (g + 1) * POOL_GROUP)
            ut = u_ref[:, cols]
            s = jnp.concatenate([halo[:, cols], ut], axis=0)
            for sh in (1, 2, 4, 8)[:g + 1]:
                s = s + pltpu.roll(s, sh, 0)
            cnt = jnp.minimum(pos + 1, win).astype(F32)
            db = (s[POOL_HALO:, :] / cnt - ut).astype(BF16)
            y = _dot(db, wp_ref[g].astype(BF16))
            d_ref[:, cols] = db
            pooled_ref[:, cols] = (y * sc_ref[:, cols]).astype(BF16)

    return pl.pallas_call(
        body, grid=(S // tm,), name="pool_fwd",
        in_specs=[_row_spec(tm, POOL_WIDTH),
                  pl.BlockSpec((POOL_HALO, POOL_WIDTH), lambda i: (jnp.maximum(i * hb - 1, 0), 0)),
                  _const_spec((N_POOL_GROUPS, POOL_GROUP, POOL_GROUP)), _const_spec((1, POOL_WIDTH))],
        out_specs=[_row_spec(tm, POOL_WIDTH), _row_spec(tm, POOL_WIDTH)],
        out_shape=[_sds((S, POOL_WIDTH), BF16), _sds((S, POOL_WIDTH), BF16)],
        compiler_params=_cp(32),
    )(*_hbm(u, u, w_pool, pscale))


def _mix_ln1(on, pooled, xh0, g0, b0, w_out, g1, b1, tm, rider):
    S, D = xh0.shape
    n_t = S // tm

    def body(*refs):
        ((on_ref, po_ref, xh0_ref, g0_ref, b0_ref, w_ref, g1_ref, b1_ref), (xh_ref, rstd_ref, xb_ref), _,
         ride) = rider.split(refs, 8, 3, 0)

        @pl.when(pl.program_id(0) == 0)
        def _():
            rider.first(ride)

        mixed = _dot(on_ref[...], w_ref[:ATTN_WIDTH, :]) + _dot(po_ref[...], w_ref[ATTN_WIDTH:, :])
        x0 = xh0_ref[...] * g0_ref[...] + b0_ref[...]
        xh, rstd = _ln_fwd(ALPHA * x0 + mixed)
        xh_ref[...] = xh
        rstd_ref[...] = rstd
        xb_ref[...] = (xh * g1_ref[...] + b1_ref[...]).astype(BF16)

        @pl.when(pl.program_id(0) == n_t - 1)
        def _():
            rider.last(ride)

    return rider.call(
        body, [on, pooled, xh0, g0, b0, w_out, g1, b1], grid=(n_t,), name="mix_ln1",
        in_specs=[_row_spec(tm, ATTN_WIDTH), _row_spec(tm, POOL_WIDTH), _row_spec(tm, D),
                  _const_spec((1, D)), _const_spec((1, D)), _const_spec((D, D)),
                  _const_spec((1, D)), _const_spec((1, D))],
        out_specs=[_row_spec(tm, D), _row_spec(tm, 1), _row_spec(tm, D)],
        out_shape=[_sds((S, D), F32), _sds((S, 1), F32), _sds((S, D), BF16)],
        scratch_shapes=[], vmem_mib=40)


def _mlp_ln2(xh1, x1b, g1, b1, tm, place, rider):
    S, D = xh1.shape
    fc = D_FF // N_CHIPS
    n_t = S // tm
    i_mid, i_last = min(2, n_t - 1), min(5, n_t - 1)

    def body(pl_ref, *refs):
        ((xh_ref, xb_ref, g_ref, b_ref), (xh2_ref, rstd_ref, r_ref), (acc_ref, wu_buf, wd_buf, wsem),
         ride) = rider.split(refs, 4, 3, 4)
        j, i = pl.program_id(0), pl.program_id(1)
        wu_hbm, wd_hbm = ride[1]

        def fetch(jj):
            chunk = pl_ref[1] ^ jj
            return [pltpu.make_async_copy(wu_hbm.at[chunk], wu_buf.at[jj % 2], wsem.at[0, jj % 2]),
                    pltpu.make_async_copy(wd_hbm.at[chunk], wd_buf.at[jj % 2], wsem.at[1, jj % 2])]

        @pl.when(jnp.logical_and(j == 0, i == 0))
        def _():
            rider.first(ride)
            for cp in fetch(0):
                cp.start()

        for jj in range(N_CHIPS):
            @pl.when(jnp.logical_and(j == jj, i == 0))
            def _(jj=jj):
                for cp in fetch(jj):
                    cp.wait()
                if jj + 1 < N_CHIPS - 1:
                    for cp in fetch(jj + 1):
                        cp.start()

        @pl.when(jnp.logical_and(j == N_CHIPS - 2, i == i_mid))
        def _():
            rider.mid(ride)

        @pl.when(jnp.logical_and(j == N_CHIPS - 2, i == i_last))
        def _():
            rider.last(ride)
            for cp in fetch(N_CHIPS - 1):
                cp.start()

        rows = pl.ds(pl.multiple_of(i * tm, tm), tm)

        @pl.when(j == 0)
        def _():
            acc_ref[rows, :] = jnp.zeros((tm, D), F32)

        r = jnp.maximum(_dot(xb_ref[...], wu_buf[j % 2]), 0.0)
        r_ref[...] = r.astype(BF16)
        acc_ref[rows, :] += _dot((r * r).astype(BF16), wd_buf[j % 2])

        @pl.when(j == N_CHIPS - 1)
        def _():
            x1 = xh_ref[...] * g_ref[...] + b_ref[...]
            xh, rstd = _ln_fwd(ALPHA * x1 + acc_ref[rows, :])
            xh2_ref[...] = xh
            rstd_ref[...] = rstd

    def at_end(j, i, pr):
        return (jnp.where(j == N_CHIPS - 1, i, 0), 0)

    return rider.call(
        body, [xh1, x1b, g1, b1], grid=(N_CHIPS, n_t), name="mlp_ln2", prefetch=place,
        in_specs=[pl.BlockSpec((tm, D), at_end), pl.BlockSpec((tm, D), lambda j, i, pr: (i, 0)),
                  pl.BlockSpec((1, D), lambda j, i, pr: (0, 0)), pl.BlockSpec((1, D), lambda j, i, pr: (0, 0))],
        out_specs=[pl.BlockSpec((tm, D), at_end), pl.BlockSpec((tm, 1), at_end),
                   pl.BlockSpec((tm, fc), lambda j, i, pr: (i, pr[1] ^ j))],
        out_shape=[_sds((S, D), F32), _sds((S, 1), F32), _sds((S, D_FF), BF16)],
        scratch_shapes=[pltpu.VMEM((S, D), F32), pltpu.VMEM((2, D, fc), BF16), pltpu.VMEM((2, fc, D), BF16),
                        pltpu.SemaphoreType.DMA((2, 2))],
        vmem_mib=56)


def _ple_ln3_loss(xh2, rstd2, g2, b2, p, w_ple_s, w_gate, g3, b3, target, tm):
    S, D = xh2.shape
    pc = D // N_CHIPS

    def body(xh2_ref, rstd2_ref, g2_ref, b2_ref, p_ref, wp_ref, wg_ref, g3_ref, b3_ref, t_ref,
             dpre2_ref, dhb_ref, dwp_ref, dwg_ref, dg3_ref, db3_ref, dg2_ref, db2_ref, loss_ref):
        i = pl.program_id(0)

        @pl.when(i == 0)
        def _():
            for r in (dwp_ref, dwg_ref, dg3_ref, db3_ref, dg2_ref, db2_ref, loss_ref):
                r[...] = jnp.zeros_like(r)

        xh2 = xh2_ref[...]
        x2 = xh2 * g2_ref[...] + b2_ref[...]
        x2b = x2.astype(BF16)
        gate = 1.0 / (1.0 + jnp.exp(-_dot(x2b, wg_ref[...])))
        pb = p_ref[...].astype(BF16)
        pe = jnp.concatenate([_dot(pb, wp_ref[c]) for c in range(N_CHIPS)], axis=1)
        xh3, rstd3 = _ln_fwd(ALPHA * x2 + pe * gate)
        diff = xh3 * g3_ref[...] + b3_ref[...] - t_ref[...]
        loss_ref[...] += (0.5 / D) * jnp.sum(diff * diff)
        dy = diff * (1.0 / D)
        dg3_ref[...] += _colsum(dy * xh3)
        db3_ref[...] += _colsum(dy)
        dpre3 = _ln_bwd(dy, xh3, rstd3, g3_ref[...])
        dpe_b = (dpre3 * gate).astype(BF16)
        dgp_b = (dpre3 * pe * gate * (1.0 - gate)).astype(BF16)
        dx2 = ALPHA * dpre3 + _dot_nt(dgp_b, wg_ref[...])
        dwg_ref[...] += _dot_tn(x2b, dgp_b)
        for c in range(N_CHIPS):
            dwp_ref[c] += _dot_tn(pb, dpe_b[:, c * pc:(c + 1) * pc])
        dg2_ref[...] += _colsum(dx2 * xh2)
        db2_ref[...] += _colsum(dx2)
        dpre2 = _ln_bwd(dx2, xh2, rstd2_ref[...], g2_ref[...])
        dpre2_ref[...] = dpre2
        dhb_ref[...] = dpre2.astype(BF16)

    vec = _const_spec((1, D))
    return pl.pallas_call(
        body, grid=(S // tm,), name="ple_ln3_loss",
        in_specs=[_row_spec(tm, D), _row_spec(tm, 1), vec, vec, _row_spec(tm, PLE_DIM),
                  _const_spec((N_CHIPS, PLE_DIM, pc)), _const_spec((D, D)), vec, vec, _row_spec(tm, D)],
        out_specs=[_row_spec(tm, D), _row_spec(tm, D), _const_spec((N_CHIPS, PLE_DIM, pc)),
                   _const_spec((D, D)), vec, vec, vec, vec, _const_spec((1, LANES))],
        out_shape=[_sds((S, D), F32), _sds((S, D), BF16), _sds((N_CHIPS, PLE_DIM, pc), F32),
                   _sds((D, D), F32), _sds((1, D), F32), _sds((1, D), F32), _sds((1, D), F32),
                   _sds((1, D), F32), _sds((1, LANES), F32)],
        compiler_params=_cp(48),
    )(*_hbm(xh2, rstd2, g2, b2, p, w_ple_s, w_gate, g3, b3, target))


def _mlp_bwd(rb, dhb, w_up_s, w_down, tm, rider):
    S, D = dhb.shape
    fc = D_FF // N_CHIPS
    n_t = S // tm

    def body(*refs):
        (r_ref, dh_ref, wu_ref, wd_ref), (dx_ref, da_ref), _, ride = rider.split(refs, 4, 2, 0)
        i, j = pl.program_id(0), pl.program_id(1)

        @pl.when(jnp.logical_and(i == 0, j == 0))
        def _():
            rider.first(ride)

        @pl.when(j == 0)
        def _():
            dx_ref[...] = jnp.zeros_like(dx_ref)

        da = (_dot_nt(dh_ref[...], wd_ref[...]) * (2.0 * r_ref[...].astype(F32))).astype(BF16)
        da_ref[...] = da
        dx_ref[...] += _dot_nt(da, wu_ref[0])

        @pl.when(jnp.logical_and(i == n_t - 1, j == N_CHIPS - 1))
        def _():
            rider.last(ride)

    return rider.call(
        body, [rb, dhb, w_up_s, w_down], grid=(n_t, N_CHIPS), name="mlp_bwd",
        in_specs=[pl.BlockSpec((tm, fc), lambda i, j: (i, j)), pl.BlockSpec((tm, D), lambda i, j: (i, 0)),
                  pl.BlockSpec((1, D, fc), lambda i, j: (j, 0, 0)),
                  pl.BlockSpec((fc, D), lambda i, j: (j, 0))],
        out_specs=[pl.BlockSpec((tm, D), lambda i, j: (i, 0)), pl.BlockSpec((tm, fc), lambda i, j: (i, j))],
        out_shape=[_sds((S, D), F32), _sds((S, D_FF), BF16)],
        scratch_shapes=[], vmem_mib=56)


def _tn_matmul(a, b, name, tk, tt, stacked, rider, square_a=False):
    T, K = a.shape
    N = b.shape[1]
    tn = 1024
    grid = (K // tk, N // tn, T // tt)

    def body(*refs):
        (a_ref, b_ref), (o_ref,), _, ride = rider.split(refs, 2, 1, 0)
        at = [pl.program_id(d) for d in range(3)]

        @pl.when(jnp.logical_and(jnp.logical_and(at[0] == 0, at[1] == 0), at[2] == 0))
        def _():
            rider.first(ride)

        @pl.when(at[2] == 0)
        def _():
            o_ref[...] = jnp.zeros_like(o_ref)

        a_t = a_ref[...]
        if square_a:
            a_t = a_t * a_t
        prod = _dot_tn(a_t, b_ref[...])
        if stacked:
            o_ref[0] += prod
        else:
            o_ref[...] += prod

        @pl.when(jnp.logical_and(jnp.logical_and(at[0] == grid[0] - 1, at[1] == grid[1] - 1),
                                 at[2] == grid[2] - 1))
        def _():
            rider.last(ride)

    if stacked:
        out_spec = pl.BlockSpec((1, tk, tn), lambda k, n, t: (n, k, 0))
        out_shape = _sds((N // tn, K, tn), F32)
    else:
        out_spec = pl.BlockSpec((tk, tn), lambda k, n, t: (k, n))
        out_shape = _sds((K, N), F32)
    return rider.call(
        body, [a, b], grid=grid, name=name,
        in_specs=[pl.BlockSpec((tt, tk), lambda k, n, t: (t, k)),
                  pl.BlockSpec((tt, tn), lambda k, n, t: (t, n))],
        out_specs=[out_spec], out_shape=[out_shape], scratch_shapes=[], vmem_mib=40)


def _mix_bwd(dpre2, dx1m, xh1, rstd1, g1, w_out, on, pooled, tm, rider):
    S, D = xh1.shape
    n_t = S // tm

    def body(*refs):
        ((dp2_ref, dxm_ref, xh_ref, rstd_ref, g_ref, w_ref, on_ref, po_ref),
         (dpre1_ref, don_ref, dpo_ref, dw_ref, dg_ref, db_ref), _, ride) = rider.split(refs, 8, 6, 0)

        @pl.when(pl.program_id(0) == 0)
        def _():
            rider.first(ride)
            for r in (dw_ref, dg_ref, db_ref):
                r[...] = jnp.zeros_like(r)

        xh = xh_ref[...]
        dx1 = ALPHA * dp2_ref[...] + dxm_ref[...]
        dg_ref[...] += _colsum(dx1 * xh)
        db_ref[...] += _colsum(dx1)
        dpre1 = _ln_bwd(dx1, xh, rstd_ref[...], g_ref[...])
        dpre1_ref[...] = dpre1
        dmb = dpre1.astype(BF16)
        dcat = _dot_nt(dmb, w_ref[...])
        don_ref[...] = dcat[:, :ATTN_WIDTH]
        dpo_ref[...] = dcat[:, ATTN_WIDTH:]
        dw_ref[:ATTN_WIDTH, :] += _dot_tn(on_ref[...], dmb)
        dw_ref[ATTN_WIDTH:, :] += _dot_tn(po_ref[...], dmb)

        @pl.when(pl.program_id(0) == n_t - 1)
        def _():
            rider.last(ride)

    vec = _const_spec((1, D))
    return rider.call(
        body, [dpre2, dx1m, xh1, rstd1, g1, w_out, on, pooled], grid=(n_t,), name="mix_bwd",
        in_specs=[_row_spec(tm, D), _row_spec(tm, D), _row_spec(tm, D), _row_spec(tm, 1), vec,
                  _const_spec((D, D)), _row_spec(tm, ATTN_WIDTH), _row_spec(tm, POOL_WIDTH)],
        out_specs=[_row_spec(tm, D), _row_spec(tm, ATTN_WIDTH), _row_spec(tm, POOL_WIDTH),
                   _const_spec((D, D)), vec, vec],
        out_shape=[_sds((S, D), F32), _sds((S, ATTN_WIDTH), F32), _sds((S, POOL_WIDTH), F32),
                   _sds((D, D), F32), _sds((1, D), F32), _sds((1, D), F32)],
        scratch_shapes=[], vmem_mib=48)


def _pool_bwd(dpooled, d_b, w_pool, pscale, tm, rider):
    S = dpooled.shape[0]
    hb = tm // POOL_HALO
    n_t = S // tm
    te = tm + POOL_HALO

    def body(*refs):
        ((dp_ref, dph_ref, d_ref, wp_ref, sc_ref), (du_ref, dwp_ref, dsc_ref), _,
         ride) = rider.split(refs, 5, 3, 0)
        i = pl.program_id(0)

        @pl.when(i == 0)
        def _():
            rider.first(ride)
            dwp_ref[...] = jnp.zeros_like(dwp_ref)
            dsc_ref[...] = jnp.zeros_like(dsc_ref)

        halo = jnp.where(i < n_t - 1, dph_ref[...], 0.0)
        pos = i * tm + lax.broadcasted_iota(jnp.int32, (te, 1), 0)
        for g in range(N_POOL_GROUPS):
            win = 2 ** (g + 1)
            cols = slice(g * POOL_GROUP, (g + 1) * POOL_GROUP)
            wpb = wp_ref[g].astype(BF16)
            dpt = dp_ref[:, cols]
            dpe = jnp.concatenate([dpt, halo[:, cols]], axis=0)
            dyb = (dpe * sc_ref[:, cols]).astype(BF16)
            dd = _dot_nt(dyb, wpb)
            s = dd / jnp.minimum(pos + 1, win).astype(F32)
            for sh in (1, 2, 4, 8)[:g + 1]:
                s = s + pltpu.roll(s, te - sh, 0)
            du_ref[:, cols] = s[:tm, :] - dd[:tm, :]
            db = d_ref[:, cols]
            dwp_ref[g] += _dot_tn(db, dyb[:tm, :])
            dsc_ref[:, cols] += _colsum(dpt * _dot(db, wpb))

        @pl.when(i == n_t - 1)
        def _():
            rider.last(ride)

    return rider.call(
        body, [dpooled, dpooled, d_b, w_pool, pscale], grid=(n_t,), name="pool_bwd",
        in_specs=[_row_spec(tm, POOL_WIDTH),
                  pl.BlockSpec((POOL_HALO, POOL_WIDTH),
                               lambda i: (jnp.minimum((i + 1) * hb, S // POOL_HALO - 1), 0)),
                  _row_spec(tm, POOL_WIDTH),
                  _const_spec((N_POOL_GROUPS, POOL_GROUP, POOL_GROUP)), _const_spec((1, POOL_WIDTH))],
        out_specs=[_row_spec(tm, POOL_WIDTH), _const_spec((N_POOL_GROUPS, POOL_GROUP, POOL_GROUP)),
                   _const_spec((1, POOL_WIDTH))],
        out_shape=[_sds((S, POOL_WIDTH), F32), _sds((N_POOL_GROUPS, POOL_GROUP, POOL_GROUP), F32),
                   _sds((1, POOL_WIDTH), F32)],
        scratch_shapes=[], vmem_mib=32)


def _attn_bwd(q, k, v, don, o_raw, ga, tq, rider):
    S = q.shape[0]
    nq = S // tq

    def body(*refs):
        ((q_ref, k_ref, v_ref, don_ref, o_ref, ga_ref), (dq_ref, dk_ref, dv_ref, dga_ref),
         (g_s, b_s, low_s, upp_s), ride) = rider.split(refs, 6, 4, 4)
        p, i = pl.program_id(0), pl.program_id(1)

        @pl.when(jnp.logical_and(p == 0, i == 0))
        def _():
            rider.first(ride)
            low_s[...] = _tri(tq, upper=False)
            upp_s[...] = _tri(tq, upper=True)

        @pl.when(i == 0)
        def _():
            for r in (dk_ref, dv_ref, dga_ref):
                r[...] = jnp.zeros_like(r)

        lane = lax.broadcasted_iota(jnp.int32, (1, PAIR), 1)
        m0 = lane < HEAD_DIM
        low = low_s[...]
        upp = upp_s[...]

        def seg_mean(a):
            s0 = jnp.sum(jnp.where(m0, a, 0.0), axis=-1, keepdims=True)
            s1 = jnp.sum(jnp.where(m0, 0.0, a), axis=-1, keepdims=True)
            return jnp.where(m0, s0, s1) * (1.0 / HEAD_DIM)

        o = o_ref[...]
        rs = lax.rsqrt(seg_mean(o * o) + RMS_EPS)
        oh = o * rs
        don = don_ref[...]
        dga_ref[...] += _colsum(don * oh)
        doh = don * ga_ref[...]
        do = rs * (doh - oh * seg_mean(doh * oh))
        dob = do.astype(BF16)
        q2 = q_ref[...]
        qhs = [jnp.where(m0, q2, jnp.zeros_like(q2)), jnp.where(m0, jnp.zeros_like(q2), q2)]
        dhs = [jnp.where(m0, dob, jnp.zeros_like(dob)), jnp.where(m0, jnp.zeros_like(dob), dob)]
        causal = _strictly_causal(tq)

        def down(kb, c_ls, valid):
            ks = pl.multiple_of(kb * tq, tq)
            kt = k_ref[pl.ds(ks, tq), :]
            vt = v_ref[pl.ds(ks, tq), :]
            lss, ls_, ws = _sb_tile(qhs, kt, low, c_ls, valid)
            dws = [_dot_nt(dh, vt) for dh in dhs]
            for hh in range(2):
                g_s[hh, kb] = dws[hh] * ws[hh]
                b_s[hh, kb] = jnp.exp(lss[hh])
            dv_ref[pl.ds(ks, tq), :] += (_dot_tn(ws[0].astype(BF16), dhs[0])
                                         + _dot_tn(ws[1].astype(BF16), dhs[1]))
            return [c_l + jnp.sum(l, axis=1, keepdims=True) for c_l, l in zip(c_ls, ls_)]

        zc, za = jnp.zeros((tq, 1), F32), jnp.zeros((tq, PAIR), F32)
        c_ls = lax.cond(i >= 1, lambda: tuple(down(i - 1, down(i, [zc, zc], causal), None)),
                        lambda: tuple(down(i, [zc, zc], causal)))

        def more(st):
            return jnp.logical_and(st[0] <= i, jnp.max(jnp.maximum(st[1], st[2])) > LOG_WEIGHT_FLOOR)

        def down_step(st):
            c_ls = down(i - st[0], [st[1], st[2]], None)
            return (st[0] + 1, c_ls[0], c_ls[1])

        n_tiles = lax.while_loop(more, down_step, (jnp.int32(2), c_ls[0], c_ls[1]))[0]

        def up(kb, c_gs, accs, valid):
            ks = pl.multiple_of(kb * tq, tq)
            kt = k_ref[pl.ds(ks, tq), :]
            gs = [g_s[hh, kb] for hh in range(2)]
            pres = [_dot(g.astype(BF16), upp) + c_g for g, c_g in zip(gs, c_gs)]
            dzs = []
            for hh in range(2):
                beta = b_s[hh, kb]
                dz = gs[hh] - beta * (gs[hh] + pres[hh])
                if valid is not None:
                    dz = jnp.where(valid, dz, 0.0)
                dzs.append(dz.astype(BF16))
            new_a = [acc + _dot(dzb, kt) for acc, dzb in zip(accs, dzs)]
            dk_ref[pl.ds(ks, tq), :] += _dot_tn(dzs[0], qhs[0]) + _dot_tn(dzs[1], qhs[1])
            new_c = [c_g + jnp.sum(g, axis=1, keepdims=True) for c_g, g in zip(c_gs, gs)]
            return new_c, new_a

        def up_step(kb, st):
            c_gs, accs = up(kb, [st[0], st[1]], [st[2], st[3]], None)
            return (c_gs[0], c_gs[1], accs[0], accs[1])

        st = lax.fori_loop(i - n_tiles + 1, i - 1, up_step, (zc, zc, za, za))

        def last_two():
            c_gs, accs = up(i - 1, [st[0], st[1]], [st[2], st[3]], None)
            return tuple(up(i, c_gs, accs, causal)[1])

        accs = lax.cond(i >= 1, last_two, lambda: tuple(up(i, [zc, zc], [za, za], causal)[1]))
        dq_ref[...] = jnp.where(m0, accs[0], accs[1]) * Q_SCALE

        @pl.when(jnp.logical_and(p == N_PAIRS - 1, i == nq - 1))
        def _():
            rider.last(ride)

    return rider.call(
        body, [q, k, v, don, o_raw, ga], grid=(N_PAIRS, nq), name="attn_bwd",
        in_specs=[pl.BlockSpec((tq, PAIR), lambda p, i: (i, p)),
                  pl.BlockSpec((S, PAIR), lambda p, i: (0, p)),
                  pl.BlockSpec((S, PAIR), lambda p, i: (0, p)),
                  pl.BlockSpec((tq, PAIR), lambda p, i: (i, p)),
                  pl.BlockSpec((tq, PAIR), lambda p, i: (i, p)),
                  pl.BlockSpec((1, PAIR), lambda p, i: (0, p))],
        out_specs=[pl.BlockSpec((tq, PAIR), lambda p, i: (i, p)),
                   pl.BlockSpec((S, PAIR), lambda p, i: (0, p)),
                   pl.BlockSpec((S, PAIR), lambda p, i: (0, p)),
                   pl.BlockSpec((1, PAIR), lambda p, i: (0, p))],
        out_shape=[_sds((S, ATTN_WIDTH), F32), _sds((S, ATTN_WIDTH), F32), _sds((S, ATTN_WIDTH), F32),
                   _sds((1, ATTN_WIDTH), F32)],
        scratch_shapes=[pltpu.VMEM((2, nq, tq, tq), F32), pltpu.VMEM((2, nq, tq, tq), F32),
                        pltpu.VMEM((tq, tq), BF16), pltpu.VMEM((tq, tq), BF16)],
        vmem_mib=56)


def _inproj_bwd(dq, dk, dv, du, dpre1, xh0, rstd0, g0, b0, w_in_s, tm):
    S, D = xh0.shape

    def body(dq_ref, dk_ref, dv_ref, du_ref, dp1_ref, xh_ref, rstd_ref, g_ref, b_ref, w_ref,
             gx_ref, dw_ref, dg_ref, db_ref):
        @pl.when(pl.program_id(0) == 0)
        def _():
            for r in (dw_ref, dg_ref, db_ref):
                r[...] = jnp.zeros_like(r)

        xh = xh_ref[...]
        xb = (xh * g_ref[...] + b_ref[...]).astype(BF16)
        dx0 = ALPHA * dp1_ref[...]
        for c, r in enumerate((dq_ref, dk_ref, dv_ref, du_ref)):
            dpb = r[...].astype(BF16)
            dx0 = dx0 + _dot_nt(dpb, w_ref[c])
            dw_ref[c] += _dot_tn(xb, dpb)
        dg_ref[...] += _colsum(dx0 * xh)
        db_ref[...] += _colsum(dx0)
        gx_ref[...] = _ln_bwd(dx0, xh, rstd_ref[...], g_ref[...])

    vec = _const_spec((1, D))
    half = _row_spec(tm, 512)
    return pl.pallas_call(
        body, grid=(S // tm,), name="inproj_bwd",
        in_specs=[half, half, half, half, _row_spec(tm, D), _row_spec(tm, D), _row_spec(tm, 1), vec, vec,
                  _const_spec((N_CHIPS, D, 512))],
        out_specs=[_row_spec(tm, D), _const_spec((N_CHIPS, D, 512)), vec, vec],
        out_shape=[_sds((S, D), F32), _sds((N_CHIPS, D, 512), F32), _sds((1, D), F32), _sds((1, D), F32)],
        compiler_params=_cp(56),
    )(*_hbm(dq, dk, dv, du, dpre1, xh0, rstd0, g0, b0, w_in_s))


def _place():
    return lax.axis_index("x"), lax.axis_index("y"), lax.axis_index("c")


CHIP_FLIPS = ((0, 1), (1, 0), (1, 1))


class _Rider:
    def __init__(self, ins, out_shapes, n_sem, phases, aliases=None):
        self.ins, self.out_shapes, self.n_sem, self.phases = list(ins), list(out_shapes), n_sem, phases
        self.aliases = aliases or {}

    def __add__(self, other):
        na, ma = len(self.ins), len(self.out_shapes)

        def phases(ins, outs, ssem, rsem):
            mine = self.phases(ins[:na], outs[:ma], ssem, rsem)
            rest = pl.ds(self.n_sem, other.n_sem)
            theirs = other.phases(ins[na:], outs[ma:], ssem.at[rest], rsem.at[rest])
            assert len(mine) == 1 and len(theirs) == 1
            return [mine[0] + theirs[0]]

        aliases = {**self.aliases, **{na + i: ma + o for i, o in other.aliases.items()}}
        return _Rider(self.ins + other.ins, self.out_shapes + other.out_shapes, self.n_sem + other.n_sem, phases,
                      aliases)

    def split(self, refs, n_in, n_out, n_scratch):
        a = n_in + len(self.ins)
        b = a + n_out
        c = b + len(self.out_shapes)
        own = (refs[:n_in], refs[a:b], refs[c:c + n_scratch])
        return own + ((refs[n_in:a], refs[b:c]) + tuple(refs[c + n_scratch:]),)

    def first(self, ride):
        for make in self.phases(*ride)[0]:
            make().start()

    def mid(self, ride):
        ph = self.phases(*ride)
        if len(ph) == 2:
            for make in ph[0]:
                make().wait_recv()
            for make in ph[1]:
                make().start()

    def last(self, ride):
        ph = self.phases(*ride)
        if len(ph) == 2:
            for make in ph[0]:
                make().wait_send()
        for make in ph[-1]:
            make().wait()

    def call(self, body, args, *, grid, name, in_specs, out_specs, out_shape, scratch_shapes, vmem_mib,
             prefetch=None):
        n_in, n_out = len(in_specs), len(out_specs)
        sems = [pltpu.SemaphoreType.DMA((self.n_sem,)), pltpu.SemaphoreType.DMA((self.n_sem,))]
        n_pre = 0 if prefetch is None else 1
        grid_spec = pltpu.PrefetchScalarGridSpec(
            num_scalar_prefetch=n_pre, grid=grid,
            in_specs=list(in_specs) + [HBM_SPEC] * len(self.ins),
            out_specs=list(out_specs) + [HBM_SPEC] * len(self.out_shapes),
            scratch_shapes=list(scratch_shapes) + sems)
        return pl.pallas_call(
            body, name=name, grid_spec=grid_spec,
            out_shape=list(out_shape) + self.out_shapes,
            input_output_aliases={n_pre + n_in + i: n_out + o for i, o in self.aliases.items()},
            compiler_params=_cp(vmem_mib),
        )(*([] if prefetch is None else [prefetch]), *_hbm(*args), *self.ins)

    def run(self, name):
        def body(*refs):
            ride = self.split(refs, 0, 0, 0)[3]
            self.first(ride)
            self.mid(ride)
            self.last(ride)

        return self.call(body, [], grid=(), name=name, in_specs=[], out_specs=[], out_shape=[],
                         scratch_shapes=[], vmem_mib=16)


def _remote(src, dst, ssem, rsem, n, dev):
    return functools.partial(pltpu.make_async_remote_copy, src_ref=src, dst_ref=dst, send_sem=ssem.at[n],
                             recv_sem=rsem.at[n], device_id=dev, device_id_type=MESH)


def _cast_into_slot(w, place, name):
    R, C = w.shape
    tr = min(R, 512)

    def body(pl_ref, w_ref, o_ref):
        o_ref[0] = w_ref[...].astype(BF16)

    return pl.pallas_call(
        body, name=name,
        grid_spec=pltpu.PrefetchScalarGridSpec(
            num_scalar_prefetch=1, grid=(R // tr,),
            in_specs=[pl.BlockSpec((tr, C), lambda r, pr: (r, 0))],
            out_specs=pl.BlockSpec((1, tr, C), lambda r, pr: (pr[1], r, 0))),
        out_shape=_sds((N_CHIPS, R, C), BF16),
    )(place, w)


CAST_STEPS = 8


def _cast_rest(ws, place, rider):
    n = len(ws)

    def body(pl_ref, *refs):
        w_refs, o_refs, _, ride = rider.split(refs, n, n, 0)
        r = pl.program_id(0)

        @pl.when(r == 0)
        def _():
            rider.first(ride)

        @pl.when(r == CAST_STEPS // 2)
        def _():
            rider.mid(ride)

        for w_ref, o_ref in zip(w_refs, o_refs):
            o_ref[0] = w_ref[...].astype(BF16)

        @pl.when(r == CAST_STEPS - 1)
        def _():
            rider.last(ride)

    def rows(w):
        return w.shape[0] // CAST_STEPS

    return rider.call(
        body, ws, grid=(CAST_STEPS,), name="cast_weights", prefetch=place,
        in_specs=[pl.BlockSpec((rows(w), w.shape[1]), lambda r, pr: (r, 0)) for w in ws],
        out_specs=[pl.BlockSpec((1, rows(w), w.shape[1]), lambda r, pr: (pr[1], r, 0)) for w in ws],
        out_shape=[_sds((N_CHIPS,) + w.shape, BF16) for w in ws], scratch_shapes=[], vmem_mib=32)


def _gather_rider(stacked, part="both", flips=CHIP_FLIPS):
    n, nf = len(stacked), len(flips)

    def phases(ins, outs, ssem, rsem):
        x, y, c = _place()
        slot = 2 * x + y
        ici, d2d = [], []
        for w, (i_ref, o_ref) in enumerate(zip(ins, outs)):
            hh = o_ref.shape[1] // 2
            rows = pl.ds(c * hh, hh)
            for f, (fx, fy) in enumerate(flips):
                k = w * nf + f
                theirs = 2 * (x ^ fx) + (y ^ fy)
                if part != "pair":
                    ici.append(_remote(i_ref.at[slot, rows], o_ref.at[slot, rows], ssem, rsem, k,
                                       (x ^ fx, y ^ fy, c)))
                if part != "chips":
                    d2d.append(_remote(o_ref.at[theirs, rows], o_ref.at[theirs, rows], ssem, rsem,
                                       (n * nf if part == "both" else 0) + k, (x, y, 1 - c)))
        return [ph for ph in (ici, d2d) if ph]

    return _Rider(stacked, [_sds(s.shape, s.dtype) for s in stacked], (2 if part == "both" else 1) * n * nf,
                  phases, aliases={i: i for i in range(n)})


def _pair_swap_rider(grads):
    def phases(ins, outs, ssem, rsem):
        x, y, c = _place()
        return [[_remote(g.at[:, 1 - c], o, ssem, rsem, k, (x, y, 1 - c))
                 for k, (g, o) in enumerate(zip(ins, outs))]]

    return _Rider(grads, [_sds((N_CHIPS,) + g.shape[2:], g.dtype) for g in grads], len(grads), phases)


def _chip_scatter_rider(parts):
    nf = len(CHIP_FLIPS)

    def phases(ins, outs, ssem, rsem):
        x, y, c = _place()
        return [[_remote(r.at[2 * (x ^ fx) + (y ^ fy)], o.at[f], ssem, rsem, w * nf + f, (x ^ fx, y ^ fy, c))
                 for w, (r, o) in enumerate(zip(ins, outs)) for f, (fx, fy) in enumerate(CHIP_FLIPS)]]

    return _Rider(parts, [_sds((nf,) + r.shape[1:], r.dtype) for r in parts], len(parts) * nf, phases)


def _pair_send_rider(halves):
    def phases(ins, outs, ssem, rsem):
        x, y, c = _place()
        return [[_remote(h, o, ssem, rsem, k, (x, y, 1 - c)) for k, (h, o) in enumerate(zip(ins, outs))]]

    return _Rider(halves, [_sds(h.shape, h.dtype) for h in halves], len(halves), phases)


PAIR_SUM_STEPS = 2
CHIP_SUM_STEPS = 4
ADAMW_STEPS = 4


def _no_rider():
    return _Rider([], [], 1, lambda ins, outs, ssem, rsem: [[]])


def _add_pair(grads, recvs, place, name, rider):
    n = len(grads)

    def body(pl_ref, *refs):
        ins, outs, _, ride = rider.split(refs, 2 * n, 2 * n, 0)
        j, h = pl.program_id(0), pl.program_id(1)

        @pl.when(jnp.logical_and(j == 0, h == 0))
        def _():
            rider.first(ride)

        for w in range(n):
            s = ins[2 * w][:, 0] + ins[2 * w + 1][...]
            outs[2 * w][...] = s
            outs[2 * w + 1][...] = s.astype(BF16)

        @pl.when(jnp.logical_and(j == N_CHIPS - 1, h == PAIR_SUM_STEPS - 1))
        def _():
            rider.last(ride)

    in_specs, out_specs, out_shape, args = [], [], [], []
    for g, r in zip(grads, recvs):
        _, _, H, C = g.shape
        th = H // PAIR_SUM_STEPS
        spec = pl.BlockSpec((1, th, C), lambda j, h, pr: (j, h, 0))
        in_specs += [pl.BlockSpec((1, 1, th, C), lambda j, h, pr: (j, pr[0], h, 0)), spec]
        out_specs += [spec, spec]
        out_shape += [_sds((N_CHIPS, H, C), F32), _sds((N_CHIPS, H, C), BF16)]
        args += [g, r]
    res = rider.call(body, args, grid=(N_CHIPS, PAIR_SUM_STEPS), name=name, prefetch=place, in_specs=in_specs,
                     out_specs=out_specs, out_shape=out_shape, scratch_shapes=[], vmem_mib=32)
    return [(res[2 * w], res[2 * w + 1]) for w in range(n)], res[2 * n:]


def _add_chips(parts, recvs, place, name, rider):
    n = len(parts)

    def body(pl_ref, *refs):
        ins, outs, _, ride = rider.split(refs, 2 * n, n, 0)
        h = pl.program_id(0)

        @pl.when(h == 0)
        def _():
            rider.first(ride)

        for w in range(n):
            p_ref, r_ref = ins[2 * w], ins[2 * w + 1]
            outs[w][...] = p_ref[0] + r_ref[0].astype(F32) + r_ref[1].astype(F32) + r_ref[2].astype(F32)

        @pl.when(h == CHIP_SUM_STEPS - 1)
        def _():
            rider.last(ride)

    in_specs, out_specs, out_shape, args = [], [], [], []
    for p, r in zip(parts, recvs):
        _, H, C = p.shape
        th = H // CHIP_SUM_STEPS
        in_specs += [pl.BlockSpec((1, th, C), lambda h, pr: (pr[1], h, 0)),
                     pl.BlockSpec((len(CHIP_FLIPS), th, C), lambda h, pr: (0, h, 0))]
        out_specs.append(pl.BlockSpec((th, C), lambda h, pr: (h, 0)))
        out_shape.append(_sds((H, C), F32))
        args += [p, r]
    res = rider.call(body, args, grid=(CHIP_SUM_STEPS,), name=name, prefetch=place, in_specs=in_specs,
                     out_specs=out_specs, out_shape=out_shape, scratch_shapes=[], vmem_mib=32)
    return res[:n], res[n:]


def _adamw_math(w, g, m, v):
    m = ADAM_B1 * m + (1.0 - ADAM_B1) * g
    v = ADAM_B2 * v + (1.0 - ADAM_B2) * (g * g)
    m_hat = m / (1.0 - ADAM_B1 ** ADAM_STEP)
    v_hat = v / (1.0 - ADAM_B2 ** ADAM_STEP)
    delta = -ADAM_LR * (m_hat / (jnp.sqrt(v_hat) + ADAM_EPS) + ADAM_WD * w)
    return delta, m, v


def _adamw(ws, mines, theirs, ms, vs, place, name, rider):
    n = len(ws)

    def body(pl_ref, *refs):
        ins, outs, _, ride = rider.split(refs, 5 * n, 4 * n, 0)
        h, r = pl.program_id(0), pl.program_id(1)

        @pl.when(jnp.logical_and(h == 0, r == 0))
        def _():
            rider.first(ride)

        for k in range(n):
            w_ref, a_ref, b_ref, m_ref, v_ref = ins[5 * k:5 * k + 5]
            g = jnp.where(h == pl_ref[0], a_ref[...], b_ref[...])
            d, mo, vo = _adamw_math(w_ref[...], g, m_ref[...], v_ref[...])
            for o_ref, val in zip(outs[4 * k:4 * k + 4], (g, d, mo, vo)):
                o_ref[...] = val

        @pl.when(jnp.logical_and(h == 1, r == ADAMW_STEPS - 1))
        def _():
            rider.last(ride)

    in_specs, out_specs, out_shape, args = [], [], [], []
    for w, a, b, m, v in zip(ws, mines, theirs, ms, vs):
        R, C = w.shape
        th = (R // 2) // ADAMW_STEPS
        whole = pl.BlockSpec((th, C), lambda h, r, pr: (h * ADAMW_STEPS + r, 0))
        mine_spec = pl.BlockSpec((th, C), lambda h, r, pr: (jnp.where(h == pr[0], r, 0), 0))
        theirs_spec = pl.BlockSpec((th, C), lambda h, r, pr: (jnp.where(h == pr[0], 0, r), 0))
        in_specs += [whole, mine_spec, theirs_spec, whole, whole]
        out_specs += [whole] * 4
        out_shape += [_sds((R, C), F32)] * 4
        args += [w, a, b, m, v]
    res = rider.call(body, args, grid=(2, ADAMW_STEPS), name=name, prefetch=place, in_specs=in_specs,
                     out_specs=out_specs, out_shape=out_shape, scratch_shapes=[], vmem_mib=40)
    return [tuple(res[4 * k:4 * k + 4]) for k in range(n)], res[4 * n:]


DEVICE_FLIPS = tuple((fx, fy, fc) for fx in (0, 1) for fy in (0, 1) for fc in (0, 1))[1:]


def _pack_exchange_rider(pack):
    def phases(ins, outs, ssem, rsem):
        x, y, c = _place()
        mine = outs[0].at[4 * x + 2 * y + c]
        copies = [_remote(ins[0], mine, ssem, rsem, k, (x ^ fx, y ^ fy, c ^ fc))
                  for k, (fx, fy, fc) in enumerate(DEVICE_FLIPS)]
        copies.append(functools.partial(pltpu.make_async_copy, ins[0], mine, ssem.at[len(DEVICE_FLIPS)]))
        return [copies]

    return _Rider([pack], [_sds((N_DEV,) + pack.shape, pack.dtype)], len(DEVICE_FLIPS) + 1, phases)


def _small_sum_adamw(recv_a, recv_b, wpack, mpack, vpack):
    R = wpack.shape[0]

    def body(a_ref, b_ref, w_ref, m_ref, v_ref, gs_ref, d_ref, mo_ref, vo_ref):
        ta, tb = a_ref[0], b_ref[0]
        for dev in range(1, N_DEV):
            ta = ta + a_ref[dev]
            tb = tb + b_ref[dev]
        total = jnp.concatenate([ta, tb], axis=0)
        gs_ref[...] = total
        d, mo, vo = _adamw_math(w_ref[...], total, m_ref[...], v_ref[...])
        d_ref[...] = d
        mo_ref[...] = mo
        vo_ref[...] = vo

    return pl.pallas_call(
        body, name="small_sum_adamw", in_specs=[VMEM_SPEC] * 5, out_specs=[VMEM_SPEC] * 4,
        out_shape=[_sds((R, LANES), F32)] * 4,
    )(recv_a, recv_b, wpack, mpack, vpack)


def _rows8(a):
    a = a.reshape(-1, LANES)
    pad = (-a.shape[0]) % 8
    return jnp.pad(a, ((0, pad), (0, 0))) if pad else a


def _pack(parts):
    return jnp.concatenate([_rows8(a) for a in parts], axis=0)


def _unpack(pack, like):
    out, row = [], 0
    for a in like:
        n = a.size // LANES
        out.append(pack[row:row + n].reshape(a.shape))
        row += n + (-n) % 8
    return out


def kernel(x, p, emb_ln_g, emb_ln_b, w_in, attn_out_g, w_pool, pool_scale, w_out, ln1_g, ln1_b, w_up, w_down, ln2_g, ln2_b, w_ple, w_ple_gate, ln3_g, ln3_b, loss_target, m_emb_ln_g, m_emb_ln_b, m_w_in, m_attn_out_g, m_w_pool, m_pool_scale, m_w_out, m_ln1_g, m_ln1_b, m_w_up, m_w_down, m_ln2_g, m_ln2_b, m_w_ple, m_w_ple_gate, m_ln3_g, m_ln3_b, v_emb_ln_g, v_emb_ln_b, v_w_in, v_attn_out_g, v_w_pool, v_pool_scale, v_w_out, v_ln1_g, v_ln1_b, v_w_up, v_w_down, v_ln2_g, v_ln2_b, v_w_ple, v_w_ple_gate, v_ln3_g, v_ln3_b):
    S = x.shape[1]
    tm = min(256, S)
    tq = min(256, S)
    tm_mlp = min(1024, S)
    tm_pool = min(1024, S)
    xs = x[0]
    ps = p[0, 0]
    tgt = loss_target[0]
    row = lambda a: a.reshape(1, -1)
    g0, b0 = row(emb_ln_g), row(emb_ln_b)
    g1, b1, g2, b2, g3, b3 = ln1_g, ln1_b, ln2_g, ln2_b, ln3_g, ln3_b
    wp = w_pool[0]

    xi, yi, ci = _place()
    place = jnp.stack([ci, 2 * xi + yi]).astype(jnp.int32)
    names = ["w_in", "w_out", "w_up", "w_down", "w_ple", "w_ple_gate"]

    big = [w_in[0], w_out[0], w_up[0], w_down[0], w_ple[0], w_ple_gate[0]]
    s_in = _cast_into_slot(big[0], place, "cast_w_in")
    s_out, s_up, s_down, s_ple, s_gate, w_in_s = _cast_rest(big[1:], place, _gather_rider([s_in]))

    xh0, rstd0, q, k, v, u, s_out, s_ple, s_gate = _embln_inproj(
        xs, g0, b0, w_in_s, tm, _gather_rider([s_out, s_ple, s_gate], "chips"))
    near, far = CHIP_FLIPS[:2], CHIP_FLIPS[2:]
    o_raw, on, s_up, s_down, w_out_s, w_ple_s, w_gate_s = _attn_fwd(
        q, k, v, attn_out_g, tq,
        _gather_rider([s_up, s_down], "chips", near) + _gather_rider([s_out, s_ple, s_gate], "pair"))
    w_out_f = w_out_s.reshape(D_MODEL, D_MODEL)
    w_gate_f = w_gate_s.reshape(D_MODEL, D_MODEL)
    d_b, pooled = _pool_fwd(u, wp, pool_scale, tm_pool)
    xh1, rstd1, x1b, s_up, s_down = _mix_ln1(on, pooled, xh0, g0, b0, w_out_f, g1, b1, tm,
                                             _gather_rider([s_up, s_down], "pair", near))
    xh2, rstd2, rb, w_up_s, w_down_s = _mlp_ln2(xh1, x1b, g1, b1, min(512, S), place,
                                                _gather_rider([s_up, s_down], "both", far))
    w_down_f = w_down_s.reshape(D_FF, D_MODEL)

    (dpre2, dhb, dw_ple, dw_gate, dg3, db3, dg2, db2, loss_row) = _ple_ln3_loss(
        xh2, rstd2, g2, b2, ps, w_ple_s, w_gate_f, g3, b3, tgt, tm)
    def halves_of(g):
        return g.reshape(N_CHIPS, 2, g.shape[1] // 2, g.shape[2])

    ple_halves = [halves_of(dw_ple), halves_of(dw_gate.reshape(N_CHIPS, D_MODEL // N_CHIPS, D_MODEL))]
    dx1m, da, *ple_pair = _mlp_bwd(rb, dhb, w_up_s, w_down_f, tm_mlp, _pair_swap_rider(ple_halves))
    (dw_up,) = _tn_matmul(x1b, da, "grad_w_up", 1024, min(512, S), True, _no_rider())
    up_halves = halves_of(dw_up)
    dw_down, up_pair = _tn_matmul(rb, dhb, "grad_w_down", 1024, min(512, S), False,
                                  _pair_swap_rider([up_halves]), square_a=True)
    down_halves = halves_of(dw_down.reshape(N_CHIPS, D_FF // N_CHIPS, D_MODEL))
    dpre1, don, dpooled, dw_out, dg1, db1, down_pair = _mix_bwd(
        dpre2, dx1m, xh1, rstd1, g1, w_out_f, on, pooled, tm, _pair_swap_rider([down_halves]))
    early_sum, _ = _add_pair([up_halves, down_halves] + ple_halves, [up_pair, down_pair] + ple_pair, place,
                             "pair_sum_mlp_ple", _no_rider())
    out_halves = halves_of(dw_out.reshape(N_CHIPS, D_MODEL // N_CHIPS, D_MODEL))
    du, dwp, dsc, out_pair = _pool_bwd(dpooled, d_b, wp, pool_scale, tm_pool, _pair_swap_rider([out_halves]))
    (out_sum,), _ = _add_pair([out_halves], [out_pair], place, "pair_sum_w_out", _no_rider())
    pack_a = _pack([jnp.broadcast_to(loss_row, (8, LANES)), dwp, dsc, dg1, db1, dg2, db2, dg3, db3])
    early_sum = [out_sum] + early_sum
    riding = _chip_scatter_rider([b for _, b in early_sum]) + _pack_exchange_rider(pack_a)
    dq, dk, dv, dga, *arrived = _attn_bwd(q, k, v, don, o_raw, attn_out_g, tq, riding)
    early_chips, recv_a = arrived[:-1], arrived[-1]
    grad_x, dw_in, dg0, db0 = _inproj_bwd(dq, dk, dv, du, dpre1, xh0, rstd0, g0, b0, w_in_s, tm)

    in_halves = halves_of(dw_in)
    pack_b = _pack([dg0, db0, dga])
    early_mine, (in_pair, recv_b) = _add_chips(
        [s for s, _ in early_sum], early_chips, place, "chip_sum_early",
        _pair_swap_rider([in_halves]) + _pack_exchange_rider(pack_b))
    (in_sum,), early_theirs = _add_pair([in_halves], [in_pair], place, "pair_sum_w_in", _pair_send_rider(early_mine))
    ms = [m_w_in, m_w_out, m_w_up, m_w_down, m_w_ple, m_w_ple_gate]
    vs = [v_w_in, v_w_out, v_w_up, v_w_down, v_w_ple, v_w_ple_gate]
    early_res, _ = _adamw(big[1:], early_mine, early_theirs, [m[0] for m in ms[1:]], [v[0] for v in vs[1:]],
                          place, "adamw_early", _no_rider())
    (in_chips,) = _chip_scatter_rider([in_sum[1]]).run("reduce_chips_late")
    (in_mine,), _ = _add_chips([in_sum[0]], [in_chips], place, "chip_sum_w_in", _no_rider())
    (in_theirs,) = _pair_send_rider([in_mine]).run("gather_pair_w_in")
    in_res, _ = _adamw(big[:1], [in_mine], [in_theirs], [ms[0][0]], [vs[0][0]], place, "adamw_w_in", _no_rider())
    big_out = {n: tuple(r.reshape(m.shape) for r in res4) for n, res4, m in zip(names, in_res + early_res, ms)}

    small_names = ["w_pool", "pool_scale", "ln1_g", "ln1_b", "ln2_g", "ln2_b", "ln3_g", "ln3_b",
                   "emb_ln_g", "emb_ln_b", "attn_out_g"]
    small_w = [w_pool, pool_scale, ln1_g, ln1_b, ln2_g, ln2_b, ln3_g, ln3_b, emb_ln_g, emb_ln_b, attn_out_g]
    small_m = [m_w_pool, m_pool_scale, m_ln1_g, m_ln1_b, m_ln2_g, m_ln2_b, m_ln3_g, m_ln3_b,
               m_emb_ln_g, m_emb_ln_b, m_attn_out_g]
    small_v = [v_w_pool, v_pool_scale, v_ln1_g, v_ln1_b, v_ln2_g, v_ln2_b, v_ln3_g, v_ln3_b,
               v_emb_ln_g, v_emb_ln_b, v_attn_out_g]
    loss_like = jnp.zeros((8, LANES), F32)
    gs, ds, mos, vos = _small_sum_adamw(recv_a, recv_b, _pack([loss_like] + small_w), _pack([loss_like] + small_m),
                                        _pack([jnp.ones((8, LANES), F32)] + small_v))
    like = [loss_like] + small_w
    gs_u, ds_u, mos_u, vos_u = (_unpack(a, like) for a in (gs, ds, mos, vos))
    loss = gs_u[0][0, 0]
    small_out = {n: (gs_u[i + 1], ds_u[i + 1], mos_u[i + 1], vos_u[i + 1]) for i, n in enumerate(small_names)}

    order = ["emb_ln_g", "emb_ln_b", "w_in", "attn_out_g", "w_pool", "pool_scale", "w_out", "ln1_g", "ln1_b",
             "w_up", "w_down", "ln2_g", "ln2_b", "w_ple", "w_ple_gate", "ln3_g", "ln3_b"]
    res = {**big_out, **small_out}
    outs = [loss, grad_x.reshape(x.shape)]
    for kind in range(4):
        outs += [res[n][kind] for n in order]
    return tuple(outs)
```

```python
import functools

import jax
import jax.numpy as jnp
from jax import lax
from jax.experimental import pallas as pl
from jax.experimental.pallas import tpu as pltpu

F32 = jnp.float32
BF16 = jnp.bfloat16

D_MODEL = 1024
ATTN_WIDTH = 512
POOL_WIDTH = 512
HEAD_DIM = 64
PAIR = 2 * HEAD_DIM
N_PAIRS = ATTN_WIDTH // PAIR
N_POOL_GROUPS = 4
POOL_GROUP = 128
POOL_HALO = 16
D_FF = 4096
PLE_DIM = 256
N_CHIPS = 4
N_DEV = 8
LN_EPS = 1e-5
RMS_EPS = 1e-6
ALPHA = float(2.0 ** 0.25)
Q_SCALE = 0.125
ADAM_LR = 0.001
ADAM_B1 = 0.9
ADAM_B2 = 0.999
ADAM_EPS = 1e-08
ADAM_WD = 0.01
ADAM_STEP = 10
LANES = 128
MIB = 1024 * 1024

MESH = pl.DeviceIdType.MESH
HBM_SPEC = pl.BlockSpec(memory_space=pltpu.HBM)
VMEM_SPEC = pl.BlockSpec(memory_space=pltpu.VMEM)


def _cp(vmem_mib):
    return pltpu.CompilerParams(vmem_limit_bytes=vmem_mib * MIB)


def _dot(a, b):
    return jnp.dot(a, b, preferred_element_type=F32)


def _dot_nt(a, b):
    return lax.dot_general(a, b, (((1,), (1,)), ((), ())), preferred_element_type=F32)


def _dot_tn(a, b):
    return lax.dot_general(a, b, (((0,), (0,)), ((), ())), preferred_element_type=F32)


def _ln_fwd(pre):
    mu = jnp.mean(pre, axis=-1, keepdims=True)
    xc = pre - mu
    var = jnp.mean(xc * xc, axis=-1, keepdims=True)
    rstd = lax.rsqrt(var + LN_EPS)
    return xc * rstd, rstd


def _ln_bwd(dy, xh, rstd, g):
    dxh = dy * g
    m1 = jnp.mean(dxh, axis=-1, keepdims=True)
    m2 = jnp.mean(dxh * xh, axis=-1, keepdims=True)
    return rstd * (dxh - m1 - xh * m2)


def _colsum(a):
    return jnp.sum(a, axis=0, keepdims=True)


def _neg_softplus(z):
    return -(jnp.maximum(z, 0.0) + jnp.log(1.0 + jnp.exp(-jnp.abs(z))))


def _split_bf16(a):
    hi = a.astype(BF16)
    lo = (a - hi.astype(F32)).astype(BF16)
    return hi, lo


def _row_spec(tm, n):
    return pl.BlockSpec((tm, n), lambda i: (i, 0))


def _const_spec(shape):
    nd = len(shape)
    return pl.BlockSpec(shape, lambda *_: (0,) * nd)


def _hbm(*arrays):
    return [pltpu.with_memory_space_constraint(a, pltpu.HBM) for a in arrays]


def _sds(shape, dtype):
    return pltpu.HBM(shape, dtype)


def _embln_inproj(x, g0, b0, w_in_s, tm, rider):
    S, D = x.shape
    n_t = S // tm

    def body(*refs):
        ((x_ref, g_ref, b_ref, w_ref), (xh_ref, rstd_ref, q_ref, k_ref, v_ref, u_ref), _,
         ride) = rider.split(refs, 4, 6, 0)
        i = pl.program_id(0)

        @pl.when(i == 0)
        def _():
            rider.first(ride)

        @pl.when(i == (3 * n_t) // 4)
        def _():
            rider.mid(ride)

        xh, rstd = _ln_fwd(x_ref[...])
        xh_ref[...] = xh
        rstd_ref[...] = rstd
        xb = (xh * g_ref[...] + b_ref[...]).astype(BF16)
        q_ref[...] = (_dot(xb, w_ref[0]) * Q_SCALE).astype(BF16)
        k_ref[...] = _dot(xb, w_ref[1]).astype(BF16)
        v_ref[...] = _dot(xb, w_ref[2]).astype(BF16)
        u_ref[...] = _dot(xb, w_ref[3])

        @pl.when(i == n_t - 1)
        def _():
            rider.last(ride)

    return rider.call(
        body, [x, g0, b0, w_in_s], grid=(n_t,), name="embln_inproj",
        in_specs=[_row_spec(tm, D), _const_spec((1, D)), _const_spec((1, D)),
                  _const_spec((N_CHIPS, D, 512))],
        out_specs=[_row_spec(tm, D), _row_spec(tm, 1), _row_spec(tm, 512), _row_spec(tm, 512),
                   _row_spec(tm, 512), _row_spec(tm, 512)],
        out_shape=[_sds((S, D), F32), _sds((S, 1), F32), _sds((S, 512), BF16), _sds((S, 512), BF16),
                   _sds((S, 512), BF16), _sds((S, 512), F32)],
        scratch_shapes=[], vmem_mib=40)


def _tri(n, upper):
    r = lax.broadcasted_iota(jnp.int32, (n, n), 0)
    c = lax.broadcasted_iota(jnp.int32, (n, n), 1)
    keep = (r < c) if upper else (r > c)
    return jnp.where(keep, 1.0, 0.0).astype(BF16)


def _strictly_causal(n):
    return lax.broadcasted_iota(jnp.int32, (n, n), 1) < lax.broadcasted_iota(jnp.int32, (n, n), 0)


LOG_WEIGHT_FLOOR = -110.0


def _weights_left(c_ls):
    return (jnp.max(jnp.maximum(c_ls[0], c_ls[1])) > LOG_WEIGHT_FLOOR).astype(jnp.int32)


def _sb_tile(qhs, kt, low, c_ls, valid):
    valids = valid if isinstance(valid, (list, tuple)) else [valid] * len(qhs)
    zs = [_dot_nt(qh, kt) for qh in qhs]
    lrs = [_neg_softplus(z) for z in zs]
    ls_ = [lr if m is None else jnp.where(m, lr, 0.0) for lr, m in zip(lrs, valids)]
    sfx = [_dot(l.astype(BF16), low) + c_l for l, c_l in zip(ls_, c_ls)]
    lss = [z + lr for z, lr in zip(zs, lrs)]
    ws = [jnp.exp(ls + s) for ls, s in zip(lss, sfx)]
    ws = [w if m is None else jnp.where(m, w, 0.0) for w, m in zip(ws, valids)]
    return lss, ls_, ws


def _attn_fwd(q, k, v, ga, tq, rider):
    S = q.shape[0]
    nq = S // tq

    def body(*refs):
        (q_ref, k_ref, v_ref, ga_ref), (o_ref, on_ref), (low_s,), ride = rider.split(refs, 4, 2, 1)
        p, i = pl.program_id(0), pl.program_id(1)

        @pl.when(jnp.logical_and(p == 0, i == 0))
        def _():
            rider.first(ride)
            low_s[...] = _tri(tq, upper=False)

        @pl.when(jnp.logical_and(p == N_PAIRS - 1, i == 0))
        def _():
            rider.mid(ride)

        lane = lax.broadcasted_iota(jnp.int32, (1, PAIR), 1)
        m0 = lane < HEAD_DIM
        low = low_s[...]
        q2 = q_ref[...]
        qhs = [jnp.where(m0, q2, jnp.zeros_like(q2)), jnp.where(m0, jnp.zeros_like(q2), q2)]

        def tile(kb, c_ls, accs, valid):
            ks = pl.multiple_of(kb * tq, tq)
            kt = k_ref[pl.ds(ks, tq), :]
            vt = v_ref[pl.ds(ks, tq), :]
            _, ls_, ws = _sb_tile(qhs, kt, low, c_ls, valid)
            new_a = [acc + _dot(w.astype(BF16), vt) for acc, w in zip(accs, ws)]
            new_c = [c_l + jnp.sum(l, axis=1, keepdims=True) for c_l, l in zip(c_ls, ls_)]
            return new_c, new_a

        zc, za = jnp.zeros((tq, 1), F32), jnp.zeros((tq, PAIR), F32)

        def first_two():
            c_ls, accs = tile(i, [zc, zc], [za, za], _strictly_causal(tq))
            c_ls, accs = tile(i - 1, c_ls, accs, None)
            return (_weights_left(c_ls), *c_ls, *accs)

        def first_one():
            c_ls, accs = tile(i, [zc, zc], [za, za], _strictly_causal(tq))
            return (jnp.int32(0), *c_ls, *accs)

        st0 = lax.cond(i >= 1, first_two, first_one)

        def more(st):
            return jnp.logical_and(st[0] <= i, st[1] > 0)

        def step(st):
            n, _, c0, c1, a0, a1 = st
            c_ls, accs = tile(i - n, [c0, c1], [a0, a1], None)
            return (n + 1, _weights_left(c_ls), c_ls[0], c_ls[1], accs[0], accs[1])

        st = lax.while_loop(more, step, (jnp.int32(2), *st0))
        o = jnp.where(m0, st[4], st[5])
        o_ref[...] = o
        sq = o * o
        ms0 = jnp.sum(jnp.where(m0, sq, 0.0), axis=-1, keepdims=True) * (1.0 / HEAD_DIM)
        ms1 = jnp.sum(jnp.where(m0, 0.0, sq), axis=-1, keepdims=True) * (1.0 / HEAD_DIM)
        rs = jnp.where(m0, lax.rsqrt(ms0 + RMS_EPS), lax.rsqrt(ms1 + RMS_EPS))
        on_ref[...] = (o * rs * ga_ref[...]).astype(BF16)

        @pl.when(jnp.logical_and(p == N_PAIRS - 1, i == nq - 1))
        def _():
            rider.last(ride)

    return rider.call(
        body, [q, k, v, ga], grid=(N_PAIRS, nq), name="attn_fwd",
        in_specs=[pl.BlockSpec((tq, PAIR), lambda p, i: (i, p)),
                  pl.BlockSpec((S, PAIR), lambda p, i: (0, p)),
                  pl.BlockSpec((S, PAIR), lambda p, i: (0, p)),
                  pl.BlockSpec((1, PAIR), lambda p, i: (0, p))],
        out_specs=[pl.BlockSpec((tq, PAIR), lambda p, i: (i, p)),
                   pl.BlockSpec((tq, PAIR), lambda p, i: (i, p))],
        out_shape=[_sds((S, ATTN_WIDTH), F32), _sds((S, ATTN_WIDTH), BF16)],
        scratch_shapes=[pltpu.VMEM((tq, tq), BF16)], vmem_mib=40)


def _pool_fwd(u, w_pool, pscale, tm):
    S = u.shape[0]
    hb = tm // POOL_HALO

    def body(u_ref, uh_ref, wp_ref, sc_ref, d_ref, pooled_ref):
        i = pl.program_id(0)
        halo = jnp.where(i > 0, uh_ref[...], 0.0)
        pos = i * tm + lax.broadcasted_iota(jnp.int32, (tm, 1), 0)
        for g in range(N_POOL_GROUPS):
            win = 2 ** (g + 1)
            cols = slice(g * POOL_GROUP, (g + 1) * POOL_GROUP)
            ut = u_ref[:, cols]
            s = jnp.concatenate([halo[:, cols], ut], axis=0)
            for sh in (1, 2, 4, 8)[:g + 1]:
                s = s + pltpu.roll(s, sh, 0)
            cnt = jnp.minimum(pos + 1, win).astype(F32)
            db = (s[POOL_HALO:, :] / cnt - ut).astype(BF16)
            y = _dot(db, wp_ref[g].astype(BF16))
            d_ref[:, cols] = db
            pooled_ref[:, cols] = (y * sc_ref[:, cols]).astype(BF16)

    return pl.pallas_call(
        body, grid=(S // tm,), name="pool_fwd",
        in_specs=[_row_spec(tm, POOL_WIDTH),
                  pl.BlockSpec((POOL_HALO, POOL_WIDTH), lambda i: (jnp.maximum(i * hb - 1, 0), 0)),
                  _const_spec((N_POOL_GROUPS, POOL_GROUP, POOL_GROUP)), _const_spec((1, POOL_WIDTH))],
        out_specs=[_row_spec(tm, POOL_WIDTH), _row_spec(tm, POOL_WIDTH)],
        out_shape=[_sds((S, POOL_WIDTH), BF16), _sds((S, POOL_WIDTH), BF16)],
        compiler_params=_cp(32),
    )(*_hbm(u, u, w_pool, pscale))


def _mix_ln1(on, pooled, xh0, g0, b0, w_out, g1, b1, tm, rider):
    S, D = xh0.shape
    n_t = S // tm

    def body(*refs):
        ((on_ref, po_ref, xh0_ref, g0_ref, b0_ref, w_ref, g1_ref, b1_ref), (xh_ref, rstd_ref, xb_ref), _,
         ride) = rider.split(refs, 8, 3, 0)

        @pl.when(pl.program_id(0) == 0)
        def _():
            rider.first(ride)

        mixed = _dot(on_ref[...], w_ref[:ATTN_WIDTH, :]) + _dot(po_ref[...], w_ref[ATTN_WIDTH:, :])
        x0 = xh0_ref[...] * g0_ref[...] + b0_ref[...]
        xh, rstd = _ln_fwd(ALPHA * x0 + mixed)
        xh_ref[...] = xh
        rstd_ref[...] = rstd
        xb_ref[...] = (xh * g1_ref[...] + b1_ref[...]).astype(BF16)

        @pl.when(pl.program_id(0) == n_t - 1)
        def _():
            rider.last(ride)

    return rider.call(
        body, [on, pooled, xh0, g0, b0, w_out, g1, b1], grid=(n_t,), name="mix_ln1",
        in_specs=[_row_spec(tm, ATTN_WIDTH), _row_spec(tm, POOL_WIDTH), _row_spec(tm, D),
                  _const_spec((1, D)), _const_spec((1, D)), _const_spec((D, D)),
                  _const_spec((1, D)), _const_spec((1, D))],
        out_specs=[_row_spec(tm, D), _row_spec(tm, 1), _row_spec(tm, D)],
        out_shape=[_sds((S, D), F32), _sds((S, 1), F32), _sds((S, D), BF16)],
        scratch_shapes=[], vmem_mib=40)


def _mlp_ln2(xh1, x1b, g1, b1, w_up_s, w_down, tm):
    S, D = xh1.shape
    fc = D_FF // N_CHIPS

    def body(xh_ref, xb_ref, g_ref, b_ref, wu_ref, wd_ref, xh2_ref, rstd_ref, r_ref, acc_ref):
        j = pl.program_id(1)

        @pl.when(j == 0)
        def _():
            acc_ref[...] = jnp.zeros_like(acc_ref)

        r = jnp.maximum(_dot(xb_ref[...], wu_ref[0]), 0.0)
        r_ref[...] = r.astype(BF16)
        acc_ref[...] += _dot((r * r).astype(BF16), wd_ref[...])

        @pl.when(j == N_CHIPS - 1)
        def _():
            x1 = xh_ref[...] * g_ref[...] + b_ref[...]
            xh, rstd = _ln_fwd(ALPHA * x1 + acc_ref[...])
            xh2_ref[...] = xh
            rstd_ref[...] = rstd

    return pl.pallas_call(
        body, grid=(S // tm, N_CHIPS), name="mlp_ln2",
        in_specs=[pl.BlockSpec((tm, D), lambda i, j: (i, 0)), pl.BlockSpec((tm, D), lambda i, j: (i, 0)),
                  pl.BlockSpec((1, D), lambda i, j: (0, 0)), pl.BlockSpec((1, D), lambda i, j: (0, 0)),
                  pl.BlockSpec((1, D, fc), lambda i, j: (j, 0, 0)),
                  pl.BlockSpec((fc, D), lambda i, j: (j, 0))],
        out_specs=[pl.BlockSpec((tm, D), lambda i, j: (i, 0)), pl.BlockSpec((tm, 1), lambda i, j: (i, 0)),
                   pl.BlockSpec((tm, fc), lambda i, j: (i, j))],
        out_shape=[_sds((S, D), F32), _sds((S, 1), F32), _sds((S, D_FF), BF16)],
        scratch_shapes=[pltpu.VMEM((tm, D), F32)],
        compiler_params=_cp(56),
    )(*_hbm(xh1, x1b, g1, b1, w_up_s, w_down))


def _ple_ln3_loss(xh2, rstd2, g2, b2, p, w_ple_s, w_gate, g3, b3, target, tm):
    S, D = xh2.shape
    pc = D // N_CHIPS

    def body(xh2_ref, rstd2_ref, g2_ref, b2_ref, p_ref, wp_ref, wg_ref, g3_ref, b3_ref, t_ref,
             dpre2_ref, dhb_ref, dwp_ref, dwg_ref, dg3_ref, db3_ref, dg2_ref, db2_ref, loss_ref):
        i = pl.program_id(0)

        @pl.when(i == 0)
        def _():
            for r in (dwp_ref, dwg_ref, dg3_ref, db3_ref, dg2_ref, db2_ref, loss_ref):
                r[...] = jnp.zeros_like(r)

        xh2 = xh2_ref[...]
        x2 = xh2 * g2_ref[...] + b2_ref[...]
        x2b = x2.astype(BF16)
        gate = 1.0 / (1.0 + jnp.exp(-_dot(x2b, wg_ref[...])))
        pb = p_ref[...].astype(BF16)
        pe = jnp.concatenate([_dot(pb, wp_ref[c]) for c in range(N_CHIPS)], axis=1)
        xh3, rstd3 = _ln_fwd(ALPHA * x2 + pe * gate)
        diff = xh3 * g3_ref[...] + b3_ref[...] - t_ref[...]
        loss_ref[...] += (0.5 / D) * jnp.sum(diff * diff)
        dy = diff * (1.0 / D)
        dg3_ref[...] += _colsum(dy * xh3)
        db3_ref[...] += _colsum(dy)
        dpre3 = _ln_bwd(dy, xh3, rstd3, g3_ref[...])
        dpe_b = (dpre3 * gate).astype(BF16)
        dgp_b = (dpre3 * pe * gate * (1.0 - gate)).astype(BF16)
        dx2 = ALPHA * dpre3 + _dot_nt(dgp_b, wg_ref[...])
        dwg_ref[...] += _dot_tn(x2b, dgp_b)
        for c in range(N_CHIPS):
            dwp_ref[c] += _dot_tn(pb, dpe_b[:, c * pc:(c + 1) * pc])
        dg2_ref[...] += _colsum(dx2 * xh2)
        db2_ref[...] += _colsum(dx2)
        dpre2 = _ln_bwd(dx2, xh2, rstd2_ref[...], g2_ref[...])
        dpre2_ref[...] = dpre2
        dhb_ref[...] = dpre2.astype(BF16)

    vec = _const_spec((1, D))
    return pl.pallas_call(
        body, grid=(S // tm,), name="ple_ln3_loss",
        in_specs=[_row_spec(tm, D), _row_spec(tm, 1), vec, vec, _row_spec(tm, PLE_DIM),
                  _const_spec((N_CHIPS, PLE_DIM, pc)), _const_spec((D, D)), vec, vec, _row_spec(tm, D)],
        out_specs=[_row_spec(tm, D), _row_spec(tm, D), _const_spec((N_CHIPS, PLE_DIM, pc)),
                   _const_spec((D, D)), vec, vec, vec, vec, _const_spec((1, LANES))],
        out_shape=[_sds((S, D), F32), _sds((S, D), BF16), _sds((N_CHIPS, PLE_DIM, pc), F32),
                   _sds((D, D), F32), _sds((1, D), F32), _sds((1, D), F32), _sds((1, D), F32),
                   _sds((1, D), F32), _sds((1, LANES), F32)],
        compiler_params=_cp(48),
    )(*_hbm(xh2, rstd2, g2, b2, p, w_ple_s, w_gate, g3, b3, target))


def _mlp_bwd(rb, dhb, w_up_s, w_down, tm, rider):
    S, D = dhb.shape
    fc = D_FF // N_CHIPS
    n_t = S // tm

    def body(*refs):
        (r_ref, dh_ref, wu_ref, wd_ref), (dx_ref, da_ref), _, ride = rider.split(refs, 4, 2, 0)
        i, j = pl.program_id(0), pl.program_id(1)

        @pl.when(jnp.logical_and(i == 0, j == 0))
        def _():
            rider.first(ride)

        @pl.when(j == 0)
        def _():
            dx_ref[...] = jnp.zeros_like(dx_ref)

        da = (_dot_nt(dh_ref[...], wd_ref[...]) * (2.0 * r_ref[...].astype(F32))).astype(BF16)
        da_ref[...] = da
        dx_ref[...] += _dot_nt(da, wu_ref[0])

        @pl.when(jnp.logical_and(i == n_t - 1, j == N_CHIPS - 1))
        def _():
            rider.last(ride)

    return rider.call(
        body, [rb, dhb, w_up_s, w_down], grid=(n_t, N_CHIPS), name="mlp_bwd",
        in_specs=[pl.BlockSpec((tm, fc), lambda i, j: (i, j)), pl.BlockSpec((tm, D), lambda i, j: (i, 0)),
                  pl.BlockSpec((1, D, fc), lambda i, j: (j, 0, 0)),
                  pl.BlockSpec((fc, D), lambda i, j: (j, 0))],
        out_specs=[pl.BlockSpec((tm, D), lambda i, j: (i, 0)), pl.BlockSpec((tm, fc), lambda i, j: (i, j))],
        out_shape=[_sds((S, D), F32), _sds((S, D_FF), BF16)],
        scratch_shapes=[], vmem_mib=56)


def _tn_matmul(a, b, name, tk, tt, stacked, rider, square_a=False):
    T, K = a.shape
    N = b.shape[1]
    tn = 1024
    grid = (K // tk, N // tn, T // tt)

    def body(*refs):
        (a_ref, b_ref), (o_ref,), _, ride = rider.split(refs, 2, 1, 0)
        at = [pl.program_id(d) for d in range(3)]

        @pl.when(jnp.logical_and(jnp.logical_and(at[0] == 0, at[1] == 0), at[2] == 0))
        def _():
            rider.first(ride)

        @pl.when(at[2] == 0)
        def _():
            o_ref[...] = jnp.zeros_like(o_ref)

        a_t = a_ref[...]
        if square_a:
            a_t = a_t * a_t
        prod = _dot_tn(a_t, b_ref[...])
        if stacked:
            o_ref[0] += prod
        else:
            o_ref[...] += prod

        @pl.when(jnp.logical_and(jnp.logical_and(at[0] == grid[0] - 1, at[1] == grid[1] - 1),
                                 at[2] == grid[2] - 1))
        def _():
            rider.last(ride)

    if stacked:
        out_spec = pl.BlockSpec((1, tk, tn), lambda k, n, t: (n, k, 0))
        out_shape = _sds((N // tn, K, tn), F32)
    else:
        out_spec = pl.BlockSpec((tk, tn), lambda k, n, t: (k, n))
        out_shape = _sds((K, N), F32)
    return rider.call(
        body, [a, b], grid=grid, name=name,
        in_specs=[pl.BlockSpec((tt, tk), lambda k, n, t: (t, k)),
                  pl.BlockSpec((tt, tn), lambda k, n, t: (t, n))],
        out_specs=[out_spec], out_shape=[out_shape], scratch_shapes=[], vmem_mib=40)


def _mix_bwd(dpre2, dx1m, xh1, rstd1, g1, w_out, on, pooled, tm, rider):
    S, D = xh1.shape
    n_t = S // tm

    def body(*refs):
        ((dp2_ref, dxm_ref, xh_ref, rstd_ref, g_ref, w_ref, on_ref, po_ref),
         (dpre1_ref, don_ref, dpo_ref, dw_ref, dg_ref, db_ref), _, ride) = rider.split(refs, 8, 6, 0)

        @pl.when(pl.program_id(0) == 0)
        def _():
            rider.first(ride)
            for r in (dw_ref, dg_ref, db_ref):
                r[...] = jnp.zeros_like(r)

        xh = xh_ref[...]
        dx1 = ALPHA * dp2_ref[...] + dxm_ref[...]
        dg_ref[...] += _colsum(dx1 * xh)
        db_ref[...] += _colsum(dx1)
        dpre1 = _ln_bwd(dx1, xh, rstd_ref[...], g_ref[...])
        dpre1_ref[...] = dpre1
        dmb = dpre1.astype(BF16)
        dcat = _dot_nt(dmb, w_ref[...])
        don_ref[...] = dcat[:, :ATTN_WIDTH]
        dpo_ref[...] = dcat[:, ATTN_WIDTH:]
        dw_ref[:ATTN_WIDTH, :] += _dot_tn(on_ref[...], dmb)
        dw_ref[ATTN_WIDTH:, :] += _dot_tn(po_ref[...], dmb)

        @pl.when(pl.program_id(0) == n_t - 1)
        def _():
            rider.last(ride)

    vec = _const_spec((1, D))
    return rider.call(
        body, [dpre2, dx1m, xh1, rstd1, g1, w_out, on, pooled], grid=(n_t,), name="mix_bwd",
        in_specs=[_row_spec(tm, D), _row_spec(tm, D), _row_spec(tm, D), _row_spec(tm, 1), vec,
                  _const_spec((D, D)), _row_spec(tm, ATTN_WIDTH), _row_spec(tm, POOL_WIDTH)],
        out_specs=[_row_spec(tm, D), _row_spec(tm, ATTN_WIDTH), _row_spec(tm, POOL_WIDTH),
                   _const_spec((D, D)), vec, vec],
        out_shape=[_sds((S, D), F32), _sds((S, ATTN_WIDTH), F32), _sds((S, POOL_WIDTH), F32),
                   _sds((D, D), F32), _sds((1, D), F32), _sds((1, D), F32)],
        scratch_shapes=[], vmem_mib=48)


def _pool_bwd(dpooled, d_b, w_pool, pscale, tm, rider):
    S = dpooled.shape[0]
    hb = tm // POOL_HALO
    n_t = S // tm
    te = tm + POOL_HALO

    def body(*refs):
        ((dp_ref, dph_ref, d_ref, wp_ref, sc_ref), (du_ref, dwp_ref, dsc_ref), _,
         ride) = rider.split(refs, 5, 3, 0)
        i = pl.program_id(0)

        @pl.when(i == 0)
        def _():
            rider.first(ride)
            dwp_ref[...] = jnp.zeros_like(dwp_ref)
            dsc_ref[...] = jnp.zeros_like(dsc_ref)

        halo = jnp.where(i < n_t - 1, dph_ref[...], 0.0)
        pos = i * tm + lax.broadcasted_iota(jnp.int32, (te, 1), 0)
        for g in range(N_POOL_GROUPS):
            win = 2 ** (g + 1)
            cols = slice(g * POOL_GROUP, (g + 1) * POOL_GROUP)
            wpb = wp_ref[g].astype(BF16)
            dpt = dp_ref[:, cols]
            dpe = jnp.concatenate([dpt, halo[:, cols]], axis=0)
            dyb = (dpe * sc_ref[:, cols]).astype(BF16)
            dd = _dot_nt(dyb, wpb)
            s = dd / jnp.minimum(pos + 1, win).astype(F32)
            for sh in (1, 2, 4, 8)[:g + 1]:
                s = s + pltpu.roll(s, te - sh, 0)
            du_ref[:, cols] = s[:tm, :] - dd[:tm, :]
            db = d_ref[:, cols]
            dwp_ref[g] += _dot_tn(db, dyb[:tm, :])
            dsc_ref[:, cols] += _colsum(dpt * _dot(db, wpb))

        @pl.when(i == n_t - 1)
        def _():
            rider.last(ride)

    return rider.call(
        body, [dpooled, dpooled, d_b, w_pool, pscale], grid=(n_t,), name="pool_bwd",
        in_specs=[_row_spec(tm, POOL_WIDTH),
                  pl.BlockSpec((POOL_HALO, POOL_WIDTH),
                               lambda i: (jnp.minimum((i + 1) * hb, S // POOL_HALO - 1), 0)),
                  _row_spec(tm, POOL_WIDTH),
                  _const_spec((N_POOL_GROUPS, POOL_GROUP, POOL_GROUP)), _const_spec((1, POOL_WIDTH))],
        out_specs=[_row_spec(tm, POOL_WIDTH), _const_spec((N_POOL_GROUPS, POOL_GROUP, POOL_GROUP)),
                   _const_spec((1, POOL_WIDTH))],
        out_shape=[_sds((S, POOL_WIDTH), F32), _sds((N_POOL_GROUPS, POOL_GROUP, POOL_GROUP), F32),
                   _sds((1, POOL_WIDTH), F32)],
        scratch_shapes=[], vmem_mib=32)


def _attn_bwd(q, k, v, don, o_raw, ga, tq, rider):
    S = q.shape[0]
    nq = S // tq

    def body(*refs):
        ((q_ref, k_ref, v_ref, don_ref, o_ref, ga_ref), (dq_ref, dk_ref, dv_ref, dga_ref),
         (g_s, b_s, low_s, upp_s), ride) = rider.split(refs, 6, 4, 4)
        p, i = pl.program_id(0), pl.program_id(1)

        @pl.when(jnp.logical_and(p == 0, i == 0))
        def _():
            rider.first(ride)
            low_s[...] = _tri(tq, upper=False)
            upp_s[...] = _tri(tq, upper=True)

        @pl.when(i == 0)
        def _():
            for r in (dk_ref, dv_ref, dga_ref):
                r[...] = jnp.zeros_like(r)

        lane = lax.broadcasted_iota(jnp.int32, (1, PAIR), 1)
        m0 = lane < HEAD_DIM
        low = low_s[...]
        upp = upp_s[...]

        def seg_mean(a):
            s0 = jnp.sum(jnp.where(m0, a, 0.0), axis=-1, keepdims=True)
            s1 = jnp.sum(jnp.where(m0, 0.0, a), axis=-1, keepdims=True)
            return jnp.where(m0, s0, s1) * (1.0 / HEAD_DIM)

        o = o_ref[...]
        rs = lax.rsqrt(seg_mean(o * o) + RMS_EPS)
        oh = o * rs
        don = don_ref[...]
        dga_ref[...] += _colsum(don * oh)
        doh = don * ga_ref[...]
        do = rs * (doh - oh * seg_mean(doh * oh))
        dob = do.astype(BF16)
        q2 = q_ref[...]
        qhs = [jnp.where(m0, q2, jnp.zeros_like(q2)), jnp.where(m0, jnp.zeros_like(q2), q2)]
        dhs = [jnp.where(m0, dob, jnp.zeros_like(dob)), jnp.where(m0, jnp.zeros_like(dob), dob)]
        causal = _strictly_causal(tq)

        def down(kb, c_ls, valid):
            ks = pl.multiple_of(kb * tq, tq)
            kt = k_ref[pl.ds(ks, tq), :]
            vt = v_ref[pl.ds(ks, tq), :]
            lss, ls_, ws = _sb_tile(qhs, kt, low, c_ls, valid)
            dws = [_dot_nt(dh, vt) for dh in dhs]
            for hh in range(2):
                g_s[hh, kb] = dws[hh] * ws[hh]
                b_s[hh, kb] = jnp.exp(lss[hh])
            dv_ref[pl.ds(ks, tq), :] += (_dot_tn(ws[0].astype(BF16), dhs[0])
                                         + _dot_tn(ws[1].astype(BF16), dhs[1]))
            return [c_l + jnp.sum(l, axis=1, keepdims=True) for c_l, l in zip(c_ls, ls_)]

        zc, za = jnp.zeros((tq, 1), F32), jnp.zeros((tq, PAIR), F32)
        def first_two():
            c_ls = down(i - 1, down(i, [zc, zc], causal), None)
            return (_weights_left(c_ls), *c_ls)

        st0 = lax.cond(i >= 1, first_two, lambda: (jnp.int32(0), *down(i, [zc, zc], causal)))

        def more(st):
            return jnp.logical_and(st[0] <= i, st[1] > 0)

        def down_step(st):
            c_ls = down(i - st[0], [st[2], st[3]], None)
            return (st[0] + 1, _weights_left(c_ls), c_ls[0], c_ls[1])

        n_tiles = lax.while_loop(more, down_step, (jnp.int32(2), *st0))[0]

        def up(kb, c_gs, accs, valid):
            ks = pl.multiple_of(kb * tq, tq)
            kt = k_ref[pl.ds(ks, tq), :]
            gs = [g_s[hh, kb] for hh in range(2)]
            pres = [_dot(g.astype(BF16), upp) + c_g for g, c_g in zip(gs, c_gs)]
            dzs = []
            for hh in range(2):
                beta = b_s[hh, kb]
                dz = gs[hh] - beta * (gs[hh] + pres[hh])
                if valid is not None:
                    dz = jnp.where(valid, dz, 0.0)
                dzs.append(dz.astype(BF16))
            new_a = [acc + _dot(dzb, kt) for acc, dzb in zip(accs, dzs)]
            dk_ref[pl.ds(ks, tq), :] += _dot_tn(dzs[0], qhs[0]) + _dot_tn(dzs[1], qhs[1])
            new_c = [c_g + jnp.sum(g, axis=1, keepdims=True) for c_g, g in zip(c_gs, gs)]
            return new_c, new_a

        def up_step(kb, st):
            c_gs, accs = up(kb, [st[0], st[1]], [st[2], st[3]], None)
            return (c_gs[0], c_gs[1], accs[0], accs[1])

        st = lax.fori_loop(i - n_tiles + 1, i - 1, up_step, (zc, zc, za, za))

        def last_two():
            c_gs, accs = up(i - 1, [st[0], st[1]], [st[2], st[3]], None)
            return tuple(up(i, c_gs, accs, causal)[1])

        accs = lax.cond(i >= 1, last_two, lambda: tuple(up(i, [zc, zc], [za, za], causal)[1]))
        dq_ref[...] = jnp.where(m0, accs[0], accs[1]) * Q_SCALE

        @pl.when(jnp.logical_and(p == N_PAIRS - 1, i == nq - 1))
        def _():
            rider.last(ride)

    return rider.call(
        body, [q, k, v, don, o_raw, ga], grid=(N_PAIRS, nq), name="attn_bwd",
        in_specs=[pl.BlockSpec((tq, PAIR), lambda p, i: (i, p)),
                  pl.BlockSpec((S, PAIR), lambda p, i: (0, p)),
                  pl.BlockSpec((S, PAIR), lambda p, i: (0, p)),
                  pl.BlockSpec((tq, PAIR), lambda p, i: (i, p)),
                  pl.BlockSpec((tq, PAIR), lambda p, i: (i, p)),
                  pl.BlockSpec((1, PAIR), lambda p, i: (0, p))],
        out_specs=[pl.BlockSpec((tq, PAIR), lambda p, i: (i, p)),
                   pl.BlockSpec((S, PAIR), lambda p, i: (0, p)),
                   pl.BlockSpec((S, PAIR), lambda p, i: (0, p)),
                   pl.BlockSpec((1, PAIR), lambda p, i: (0, p))],
        out_shape=[_sds((S, ATTN_WIDTH), F32), _sds((S, ATTN_WIDTH), F32), _sds((S, ATTN_WIDTH), F32),
                   _sds((1, ATTN_WIDTH), F32)],
        scratch_shapes=[pltpu.VMEM((2, nq, tq, tq), F32), pltpu.VMEM((2, nq, tq, tq), F32),
                        pltpu.VMEM((tq, tq), BF16), pltpu.VMEM((tq, tq), BF16)],
        vmem_mib=56)


def _inproj_bwd(dq, dk, dv, du, dpre1, xh0, rstd0, g0, b0, w_in_s, tm):
    S, D = xh0.shape

    def body(dq_ref, dk_ref, dv_ref, du_ref, dp1_ref, xh_ref, rstd_ref, g_ref, b_ref, w_ref,
             gx_ref, dw_ref, dg_ref, db_ref):
        @pl.when(pl.program_id(0) == 0)
        def _():
            for r in (dw_ref, dg_ref, db_ref):
                r[...] = jnp.zeros_like(r)

        xh = xh_ref[...]
        xb = (xh * g_ref[...] + b_ref[...]).astype(BF16)
        dx0 = ALPHA * dp1_ref[...]
        for c, r in enumerate((dq_ref, dk_ref, dv_ref, du_ref)):
            dpb = r[...].astype(BF16)
            dx0 = dx0 + _dot_nt(dpb, w_ref[c])
            dw_ref[c] += _dot_tn(xb, dpb)
        dg_ref[...] += _colsum(dx0 * xh)
        db_ref[...] += _colsum(dx0)
        gx_ref[...] = _ln_bwd(dx0, xh, rstd_ref[...], g_ref[...])

    vec = _const_spec((1, D))
    half = _row_spec(tm, 512)
    return pl.pallas_call(
        body, grid=(S // tm,), name="inproj_bwd",
        in_specs=[half, half, half, half, _row_spec(tm, D), _row_spec(tm, D), _row_spec(tm, 1), vec, vec,
                  _const_spec((N_CHIPS, D, 512))],
        out_specs=[_row_spec(tm, D), _const_spec((N_CHIPS, D, 512)), vec, vec],
        out_shape=[_sds((S, D), F32), _sds((N_CHIPS, D, 512), F32), _sds((1, D), F32), _sds((1, D), F32)],
        compiler_params=_cp(56),
    )(*_hbm(dq, dk, dv, du, dpre1, xh0, rstd0, g0, b0, w_in_s))


def _place():
    return lax.axis_index("x"), lax.axis_index("y"), lax.axis_index("c")


CHIP_FLIPS = ((0, 1), (1, 0), (1, 1))


class _Rider:
    def __init__(self, ins, out_shapes, n_sem, phases, aliases=None):
        self.ins, self.out_shapes, self.n_sem, self.phases = list(ins), list(out_shapes), n_sem, phases
        self.aliases = aliases or {}

    def __add__(self, other):
        na, ma = len(self.ins), len(self.out_shapes)

        def phases(ins, outs, ssem, rsem):
            mine = self.phases(ins[:na], outs[:ma], ssem, rsem)
            rest = pl.ds(self.n_sem, other.n_sem)
            theirs = other.phases(ins[na:], outs[ma:], ssem.at[rest], rsem.at[rest])
            assert len(mine) == 1 and len(theirs) == 1
            return [mine[0] + theirs[0]]

        aliases = {**self.aliases, **{na + i: ma + o for i, o in other.aliases.items()}}
        return _Rider(self.ins + other.ins, self.out_shapes + other.out_shapes, self.n_sem + other.n_sem, phases,
                      aliases)

    def split(self, refs, n_in, n_out, n_scratch):
        a = n_in + len(self.ins)
        b = a + n_out
        c = b + len(self.out_shapes)
        own = (refs[:n_in], refs[a:b], refs[c:c + n_scratch])
        return own + ((refs[n_in:a], refs[b:c]) + tuple(refs[c + n_scratch:]),)

    def first(self, ride):
        for make in self.phases(*ride)[0]:
            make().start()

    def mid(self, ride):
        ph = self.phases(*ride)
        if len(ph) == 2:
            for make in ph[0]:
                make().wait_recv()
            for make in ph[1]:
                make().start()

    def last(self, ride):
        ph = self.phases(*ride)
        if len(ph) == 2:
            for make in ph[0]:
                make().wait_send()
        for make in ph[-1]:
            make().wait()

    def call(self, body, args, *, grid, name, in_specs, out_specs, out_shape, scratch_shapes, vmem_mib,
             prefetch=None):
        n_in, n_out = len(in_specs), len(out_specs)
        sems = [pltpu.SemaphoreType.DMA((self.n_sem,)), pltpu.SemaphoreType.DMA((self.n_sem,))]
        n_pre = 0 if prefetch is None else 1
        grid_spec = pltpu.PrefetchScalarGridSpec(
            num_scalar_prefetch=n_pre, grid=grid,
            in_specs=list(in_specs) + [HBM_SPEC] * len(self.ins),
            out_specs=list(out_specs) + [HBM_SPEC] * len(self.out_shapes),
            scratch_shapes=list(scratch_shapes) + sems)
        return pl.pallas_call(
            body, name=name, grid_spec=grid_spec,
            out_shape=list(out_shape) + self.out_shapes,
            input_output_aliases={n_pre + n_in + i: n_out + o for i, o in self.aliases.items()},
            compiler_params=_cp(vmem_mib),
        )(*([] if prefetch is None else [prefetch]), *_hbm(*args), *self.ins)

    def run(self, name):
        def body(*refs):
            ride = self.split(refs, 0, 0, 0)[3]
            self.first(ride)
            self.mid(ride)
            self.last(ride)

        return self.call(body, [], grid=(), name=name, in_specs=[], out_specs=[], out_shape=[],
                         scratch_shapes=[], vmem_mib=16)


def _remote(src, dst, ssem, rsem, n, dev):
    return functools.partial(pltpu.make_async_remote_copy, src_ref=src, dst_ref=dst, send_sem=ssem.at[n],
                             recv_sem=rsem.at[n], device_id=dev, device_id_type=MESH)


def _cast_into_slot(w, place, name):
    R, C = w.shape
    tr = min(R, 512)

    def body(pl_ref, w_ref, o_ref):
        o_ref[0] = w_ref[...].astype(BF16)

    return pl.pallas_call(
        body, name=name,
        grid_spec=pltpu.PrefetchScalarGridSpec(
            num_scalar_prefetch=1, grid=(R // tr,),
            in_specs=[pl.BlockSpec((tr, C), lambda r, pr: (r, 0))],
            out_specs=pl.BlockSpec((1, tr, C), lambda r, pr: (pr[1], r, 0))),
        out_shape=_sds((N_CHIPS, R, C), BF16),
    )(place, w)


CAST_STEPS = 8


def _cast_rest(ws, place, rider):
    n = len(ws)

    def body(pl_ref, *refs):
        w_refs, o_refs, _, ride = rider.split(refs, n, n, 0)
        r = pl.program_id(0)

        @pl.when(r == 0)
        def _():
            rider.first(ride)

        @pl.when(r == CAST_STEPS // 2)
        def _():
            rider.mid(ride)

        for w_ref, o_ref in zip(w_refs, o_refs):
            o_ref[0] = w_ref[...].astype(BF16)

        @pl.when(r == CAST_STEPS - 1)
        def _():
            rider.last(ride)

    def rows(w):
        return w.shape[0] // CAST_STEPS

    return rider.call(
        body, ws, grid=(CAST_STEPS,), name="cast_weights", prefetch=place,
        in_specs=[pl.BlockSpec((rows(w), w.shape[1]), lambda r, pr: (r, 0)) for w in ws],
        out_specs=[pl.BlockSpec((1, rows(w), w.shape[1]), lambda r, pr: (pr[1], r, 0)) for w in ws],
        out_shape=[_sds((N_CHIPS,) + w.shape, BF16) for w in ws], scratch_shapes=[], vmem_mib=32)


def _gather_rider(stacked, part="both"):
    n, nf = len(stacked), len(CHIP_FLIPS)

    def phases(ins, outs, ssem, rsem):
        x, y, c = _place()
        slot = 2 * x + y
        ici, d2d = [], []
        for w, (i_ref, o_ref) in enumerate(zip(ins, outs)):
            hh = o_ref.shape[1] // 2
            rows = pl.ds(c * hh, hh)
            for f, (fx, fy) in enumerate(CHIP_FLIPS):
                k = w * nf + f
                theirs = 2 * (x ^ fx) + (y ^ fy)
                if part != "pair":
                    ici.append(_remote(i_ref.at[slot, rows], o_ref.at[slot, rows], ssem, rsem, k,
                                       (x ^ fx, y ^ fy, c)))
                if part != "chips":
                    d2d.append(_remote(o_ref.at[theirs, rows], o_ref.at[theirs, rows], ssem, rsem,
                                       (n * nf if part == "both" else 0) + k, (x, y, 1 - c)))
        return [ph for ph in (ici, d2d) if ph]

    return _Rider(stacked, [_sds(s.shape, s.dtype) for s in stacked], (2 if part == "both" else 1) * n * nf,
                  phases, aliases={i: i for i in range(n)})


def _pair_swap_rider(grads):
    def phases(ins, outs, ssem, rsem):
        x, y, c = _place()
        return [[_remote(g.at[:, 1 - c], o, ssem, rsem, k, (x, y, 1 - c))
                 for k, (g, o) in enumerate(zip(ins, outs))]]

    return _Rider(grads, [_sds((N_CHIPS,) + g.shape[2:], g.dtype) for g in grads], len(grads), phases)


def _chip_scatter_rider(parts):
    nf = len(CHIP_FLIPS)

    def phases(ins, outs, ssem, rsem):
        x, y, c = _place()
        return [[_remote(r.at[2 * (x ^ fx) + (y ^ fy)], o.at[f], ssem, rsem, w * nf + f, (x ^ fx, y ^ fy, c))
                 for w, (r, o) in enumerate(zip(ins, outs)) for f, (fx, fy) in enumerate(CHIP_FLIPS)]]

    return _Rider(parts, [_sds((nf,) + r.shape[1:], r.dtype) for r in parts], len(parts) * nf, phases)


def _pair_send_rider(halves):
    def phases(ins, outs, ssem, rsem):
        x, y, c = _place()
        return [[_remote(h, o, ssem, rsem, k, (x, y, 1 - c)) for k, (h, o) in enumerate(zip(ins, outs))]]

    return _Rider(halves, [_sds(h.shape, h.dtype) for h in halves], len(halves), phases)


PAIR_SUM_STEPS = 2
CHIP_SUM_STEPS = 4
ADAMW_STEPS = 4


def _no_rider():
    return _Rider([], [], 1, lambda ins, outs, ssem, rsem: [[]])


def _add_pair(grads, recvs, place, name, rider):
    n = len(grads)

    def body(pl_ref, *refs):
        ins, outs, _, ride = rider.split(refs, 2 * n, 2 * n, 0)
        j, h = pl.program_id(0), pl.program_id(1)

        @pl.when(jnp.logical_and(j == 0, h == 0))
        def _():
            rider.first(ride)

        for w in range(n):
            s = ins[2 * w][:, 0] + ins[2 * w + 1][...]
            outs[2 * w][...] = s
            outs[2 * w + 1][...] = s.astype(BF16)

        @pl.when(jnp.logical_and(j == N_CHIPS - 1, h == PAIR_SUM_STEPS - 1))
        def _():
            rider.last(ride)

    in_specs, out_specs, out_shape, args = [], [], [], []
    for g, r in zip(grads, recvs):
        _, _, H, C = g.shape
        th = H // PAIR_SUM_STEPS
        spec = pl.BlockSpec((1, th, C), lambda j, h, pr: (j, h, 0))
        in_specs += [pl.BlockSpec((1, 1, th, C), lambda j, h, pr: (j, pr[0], h, 0)), spec]
        out_specs += [spec, spec]
        out_shape += [_sds((N_CHIPS, H, C), F32), _sds((N_CHIPS, H, C), BF16)]
        args += [g, r]
    res = rider.call(body, args, grid=(N_CHIPS, PAIR_SUM_STEPS), name=name, prefetch=place, in_specs=in_specs,
                     out_specs=out_specs, out_shape=out_shape, scratch_shapes=[], vmem_mib=32)
    return [(res[2 * w], res[2 * w + 1]) for w in range(n)], res[2 * n:]


def _add_chips(parts, recvs, place, name, rider):
    n = len(parts)

    def body(pl_ref, *refs):
        ins, outs, _, ride = rider.split(refs, 2 * n, n, 0)
        h = pl.program_id(0)

        @pl.when(h == 0)
        def _():
            rider.first(ride)

        for w in range(n):
            p_ref, r_ref = ins[2 * w], ins[2 * w + 1]
            outs[w][...] = p_ref[0] + r_ref[0].astype(F32) + r_ref[1].astype(F32) + r_ref[2].astype(F32)

        @pl.when(h == CHIP_SUM_STEPS - 1)
        def _():
            rider.last(ride)

    in_specs, out_specs, out_shape, args = [], [], [], []
    for p, r in zip(parts, recvs):
        _, H, C = p.shape
        th = H // CHIP_SUM_STEPS
        in_specs += [pl.BlockSpec((1, th, C), lambda h, pr: (pr[1], h, 0)),
                     pl.BlockSpec((len(CHIP_FLIPS), th, C), lambda h, pr: (0, h, 0))]
        out_specs.append(pl.BlockSpec((th, C), lambda h, pr: (h, 0)))
        out_shape.append(_sds((H, C), F32))
        args += [p, r]
    res = rider.call(body, args, grid=(CHIP_SUM_STEPS,), name=name, prefetch=place, in_specs=in_specs,
                     out_specs=out_specs, out_shape=out_shape, scratch_shapes=[], vmem_mib=32)
    return res[:n], res[n:]


def _adamw_math(w, g, m, v):
    m = ADAM_B1 * m + (1.0 - ADAM_B1) * g
    v = ADAM_B2 * v + (1.0 - ADAM_B2) * (g * g)
    m_hat = m / (1.0 - ADAM_B1 ** ADAM_STEP)
    v_hat = v / (1.0 - ADAM_B2 ** ADAM_STEP)
    delta = -ADAM_LR * (m_hat / (jnp.sqrt(v_hat) + ADAM_EPS) + ADAM_WD * w)
    return delta, m, v


def _adamw(ws, mines, theirs, ms, vs, place, name, rider):
    n = len(ws)

    def body(pl_ref, *refs):
        ins, outs, _, ride = rider.split(refs, 5 * n, 4 * n, 0)
        h, r = pl.program_id(0), pl.program_id(1)

        @pl.when(jnp.logical_and(h == 0, r == 0))
        def _():
            rider.first(ride)

        for k in range(n):
            w_ref, a_ref, b_ref, m_ref, v_ref = ins[5 * k:5 * k + 5]
            g = jnp.where(h == pl_ref[0], a_ref[...], b_ref[...])
            d, mo, vo = _adamw_math(w_ref[...], g, m_ref[...], v_ref[...])
            for o_ref, val in zip(outs[4 * k:4 * k + 4], (g, d, mo, vo)):
                o_ref[...] = val

        @pl.when(jnp.logical_and(h == 1, r == ADAMW_STEPS - 1))
        def _():
            rider.last(ride)

    in_specs, out_specs, out_shape, args = [], [], [], []
    for w, a, b, m, v in zip(ws, mines, theirs, ms, vs):
        R, C = w.shape
        th = (R // 2) // ADAMW_STEPS
        whole = pl.BlockSpec((th, C), lambda h, r, pr: (h * ADAMW_STEPS + r, 0))
        mine_spec = pl.BlockSpec((th, C), lambda h, r, pr: (jnp.where(h == pr[0], r, 0), 0))
        theirs_spec = pl.BlockSpec((th, C), lambda h, r, pr: (jnp.where(h == pr[0], 0, r), 0))
        in_specs += [whole, mine_spec, theirs_spec, whole, whole]
        out_specs += [whole] * 4
        out_shape += [_sds((R, C), F32)] * 4
        args += [w, a, b, m, v]
    res = rider.call(body, args, grid=(2, ADAMW_STEPS), name=name, prefetch=place, in_specs=in_specs,
                     out_specs=out_specs, out_shape=out_shape, scratch_shapes=[], vmem_mib=40)
    return [tuple(res[4 * k:4 * k + 4]) for k in range(n)], res[4 * n:]


DEVICE_FLIPS = tuple((fx, fy, fc) for fx in (0, 1) for fy in (0, 1) for fc in (0, 1))[1:]


def _pack_exchange_rider(pack):
    def phases(ins, outs, ssem, rsem):
        x, y, c = _place()
        mine = outs[0].at[4 * x + 2 * y + c]
        copies = [_remote(ins[0], mine, ssem, rsem, k, (x ^ fx, y ^ fy, c ^ fc))
                  for k, (fx, fy, fc) in enumerate(DEVICE_FLIPS)]
        copies.append(functools.partial(pltpu.make_async_copy, ins[0], mine, ssem.at[len(DEVICE_FLIPS)]))
        return [copies]

    return _Rider([pack], [_sds((N_DEV,) + pack.shape, pack.dtype)], len(DEVICE_FLIPS) + 1, phases)


def _small_sum_adamw(recv_a, recv_b, wpack, mpack, vpack):
    R = wpack.shape[0]

    def body(a_ref, b_ref, w_ref, m_ref, v_ref, gs_ref, d_ref, mo_ref, vo_ref):
        ta, tb = a_ref[0], b_ref[0]
        for dev in range(1, N_DEV):
            ta = ta + a_ref[dev]
            tb = tb + b_ref[dev]
        total = jnp.concatenate([ta, tb], axis=0)
        gs_ref[...] = total
        d, mo, vo = _adamw_math(w_ref[...], total, m_ref[...], v_ref[...])
        d_ref[...] = d
        mo_ref[...] = mo
        vo_ref[...] = vo

    return pl.pallas_call(
        body, name="small_sum_adamw", in_specs=[VMEM_SPEC] * 5, out_specs=[VMEM_SPEC] * 4,
        out_shape=[_sds((R, LANES), F32)] * 4,
    )(recv_a, recv_b, wpack, mpack, vpack)


def _rows8(a):
    a = a.reshape(-1, LANES)
    pad = (-a.shape[0]) % 8
    return jnp.pad(a, ((0, pad), (0, 0))) if pad else a


def _pack(parts):
    return jnp.concatenate([_rows8(a) for a in parts], axis=0)


def _unpack(pack, like):
    out, row = [], 0
    for a in like:
        n = a.size // LANES
        out.append(pack[row:row + n].reshape(a.shape))
        row += n + (-n) % 8
    return out


def kernel(x, p, emb_ln_g, emb_ln_b, w_in, attn_out_g, w_pool, pool_scale, w_out, ln1_g, ln1_b, w_up, w_down, ln2_g, ln2_b, w_ple, w_ple_gate, ln3_g, ln3_b, loss_target, m_emb_ln_g, m_emb_ln_b, m_w_in, m_attn_out_g, m_w_pool, m_pool_scale, m_w_out, m_ln1_g, m_ln1_b, m_w_up, m_w_down, m_ln2_g, m_ln2_b, m_w_ple, m_w_ple_gate, m_ln3_g, m_ln3_b, v_emb_ln_g, v_emb_ln_b, v_w_in, v_attn_out_g, v_w_pool, v_pool_scale, v_w_out, v_ln1_g, v_ln1_b, v_w_up, v_w_down, v_ln2_g, v_ln2_b, v_w_ple, v_w_ple_gate, v_ln3_g, v_ln3_b):
    S = x.shape[1]
    tm = min(256, S)
    tq = min(256, S)
    tm_mlp = min(1024, S)
    tm_pool = min(1024, S)
    xs = x[0]
    ps = p[0, 0]
    tgt = loss_target[0]
    row = lambda a: a.reshape(1, -1)
    g0, b0 = row(emb_ln_g), row(emb_ln_b)
    g1, b1, g2, b2, g3, b3 = ln1_g, ln1_b, ln2_g, ln2_b, ln3_g, ln3_b
    wp = w_pool[0]

    xi, yi, ci = _place()
    place = jnp.stack([ci, 2 * xi + yi]).astype(jnp.int32)
    names = ["w_in", "w_out", "w_up", "w_down", "w_ple", "w_ple_gate"]

    big = [w_in[0], w_out[0], w_up[0], w_down[0], w_ple[0], w_ple_gate[0]]
    s_in = _cast_into_slot(big[0], place, "cast_w_in")
    s_out, s_up, s_down, s_ple, s_gate, w_in_s = _cast_rest(big[1:], place, _gather_rider([s_in]))

    xh0, rstd0, q, k, v, u, s_out, s_ple, s_gate = _embln_inproj(
        xs, g0, b0, w_in_s, tm, _gather_rider([s_out, s_ple, s_gate], "chips"))
    o_raw, on, s_up, s_down, w_out_s, w_ple_s, w_gate_s = _attn_fwd(
        q, k, v, attn_out_g, tq, _gather_rider([s_up, s_down], "chips") + _gather_rider([s_out, s_ple, s_gate], "pair"))
    w_out_f = w_out_s.reshape(D_MODEL, D_MODEL)
    w_gate_f = w_gate_s.reshape(D_MODEL, D_MODEL)
    d_b, pooled = _pool_fwd(u, wp, pool_scale, tm_pool)
    xh1, rstd1, x1b, w_up_s, w_down_s = _mix_ln1(on, pooled, xh0, g0, b0, w_out_f, g1, b1, tm,
                                                 _gather_rider([s_up, s_down], "pair"))
    w_down_f = w_down_s.reshape(D_FF, D_MODEL)
    xh2, rstd2, rb = _mlp_ln2(xh1, x1b, g1, b1, w_up_s, w_down_f, tm_mlp)

    (dpre2, dhb, dw_ple, dw_gate, dg3, db3, dg2, db2, loss_row) = _ple_ln3_loss(
        xh2, rstd2, g2, b2, ps, w_ple_s, w_gate_f, g3, b3, tgt, tm)
    def halves_of(g):
        return g.reshape(N_CHIPS, 2, g.shape[1] // 2, g.shape[2])

    ple_halves = [halves_of(dw_ple), halves_of(dw_gate.reshape(N_CHIPS, D_MODEL // N_CHIPS, D_MODEL))]
    dx1m, da, *ple_pair = _mlp_bwd(rb, dhb, w_up_s, w_down_f, tm_mlp, _pair_swap_rider(ple_halves))
    (dw_up,) = _tn_matmul(x1b, da, "grad_w_up", 1024, min(512, S), True, _no_rider())
    up_halves = halves_of(dw_up)
    dw_down, up_pair = _tn_matmul(rb, dhb, "grad_w_down", 1024, min(512, S), False,
                                  _pair_swap_rider([up_halves]), square_a=True)
    down_halves = halves_of(dw_down.reshape(N_CHIPS, D_FF // N_CHIPS, D_MODEL))
    dpre1, don, dpooled, dw_out, dg1, db1, down_pair = _mix_bwd(
        dpre2, dx1m, xh1, rstd1, g1, w_out_f, on, pooled, tm, _pair_swap_rider([down_halves]))
    early_sum, _ = _add_pair([up_halves, down_halves] + ple_halves, [up_pair, down_pair] + ple_pair, place,
                             "pair_sum_mlp_ple", _no_rider())
    out_halves = halves_of(dw_out.reshape(N_CHIPS, D_MODEL // N_CHIPS, D_MODEL))
    du, dwp, dsc, out_pair = _pool_bwd(dpooled, d_b, wp, pool_scale, tm_pool, _pair_swap_rider([out_halves]))
    (out_sum,), _ = _add_pair([out_halves], [out_pair], place, "pair_sum_w_out", _no_rider())
    pack_a = _pack([jnp.broadcast_to(loss_row, (8, LANES)), dwp, dsc, dg1, db1, dg2, db2, dg3, db3])
    early_sum = [out_sum] + early_sum
    riding = _chip_scatter_rider([b for _, b in early_sum]) + _pack_exchange_rider(pack_a)
    dq, dk, dv, dga, *arrived = _attn_bwd(q, k, v, don, o_raw, attn_out_g, tq, riding)
    early_chips, recv_a = arrived[:-1], arrived[-1]
    grad_x, dw_in, dg0, db0 = _inproj_bwd(dq, dk, dv, du, dpre1, xh0, rstd0, g0, b0, w_in_s, tm)

    in_halves = halves_of(dw_in)
    pack_b = _pack([dg0, db0, dga])
    early_mine, (in_pair, recv_b) = _add_chips(
        [s for s, _ in early_sum], early_chips, place, "chip_sum_early",
        _pair_swap_rider([in_halves]) + _pack_exchange_rider(pack_b))
    (in_sum,), early_theirs = _add_pair([in_halves], [in_pair], place, "pair_sum_w_in", _pair_send_rider(early_mine))
    ms = [m_w_in, m_w_out, m_w_up, m_w_down, m_w_ple, m_w_ple_gate]
    vs = [v_w_in, v_w_out, v_w_up, v_w_down, v_w_ple, v_w_ple_gate]
    early_res, _ = _adamw(big[1:], early_mine, early_theirs, [m[0] for m in ms[1:]], [v[0] for v in vs[1:]],
                          place, "adamw_early", _no_rider())
    (in_chips,) = _chip_scatter_rider([in_sum[1]]).run("reduce_chips_late")
    (in_mine,), _ = _add_chips([in_sum[0]], [in_chips], place, "chip_sum_w_in", _no_rider())
    (in_theirs,) = _pair_send_rider([in_mine]).run("gather_pair_w_in")
    in_res, _ = _adamw(big[:1], [in_mine], [in_theirs], [ms[0][0]], [vs[0][0]], place, "adamw_w_in", _no_rider())
    big_out = {n: tuple(r.reshape(m.shape) for r in res4) for n, res4, m in zip(names, in_res + early_res, ms)}

    small_names = ["w_pool", "pool_scale", "ln1_g", "ln1_b", "ln2_g", "ln2_b", "ln3_g", "ln3_b",
                   "emb_ln_g", "emb_ln_b", "attn_out_g"]
    small_w = [w_pool, pool_scale, ln1_g, ln1_b, ln2_g, ln2_b, ln3_g, ln3_b, emb_ln_g, emb_ln_b, attn_out_g]
    small_m = [m_w_pool, m_pool_scale, m_ln1_g, m_ln1_b, m_ln2_g, m_ln2_b, m_ln3_g, m_ln3_b,
               m_emb_ln_g, m_emb_ln_b, m_attn_out_g]
    small_v = [v_w_pool, v_pool_scale, v_ln1_g, v_ln1_b, v_ln2_g, v_ln2_b, v_ln3_g, v_ln3_b,
               v_emb_ln_g, v_emb_ln_b, v_attn_out_g]
    loss_like = jnp.zeros((8, LANES), F32)
    gs, ds, mos, vos = _small_sum_adamw(recv_a, recv_b, _pack([loss_like] + small_w), _pack([loss_like] + small_m),
                                        _pack([jnp.ones((8, LANES), F32)] + small_v))
    like = [loss_like] + small_w
    gs_u, ds_u, mos_u, vos_u = (_unpack(a, like) for a in (gs, ds, mos, vos))
    loss = gs_u[0][0, 0]
    small_out = {n: (gs_u[i + 1], ds_u[i + 1], mos_u[i + 1], vos_u[i + 1]) for i, n in enumerate(small_names)}

    order = ["emb_ln_g", "emb_ln_b", "w_in", "attn_out_g", "w_pool", "pool_scale", "w_out", "ln1_g", "ln1_b",
             "w_up", "w_down", "ln2_g", "ln2_b", "w_ple", "w_ple_gate", "ln3_g", "ln3_b"]
    res = {**big_out, **small_out}
    outs = [loss, grad_x.reshape(x.shape)]
    for kind in range(4):
        outs += [res[n][kind] for n in order]
    return tuple(outs)
```

```python
import functools

import jax
import jax.numpy as jnp
from jax import lax
from jax.experimental import pallas as pl
from jax.experimental.pallas import tpu as pltpu

F32 = jnp.float32
BF16 = jnp.bfloat16

D_MODEL = 1024
ATTN_WIDTH = 512
POOL_WIDTH = 512
HEAD_DIM = 64
PAIR = 2 * HEAD_DIM
N_PAIRS = ATTN_WIDTH // PAIR
N_POOL_GROUPS = 4
POOL_GROUP = 128
POOL_HALO = 16
D_FF = 4096
PLE_DIM = 256
N_CHIPS = 4
N_DEV = 8
LN_EPS = 1e-5
RMS_EPS = 1e-6
ALPHA = float(2.0 ** 0.25)
Q_SCALE = 0.125
ADAM_LR = 0.001
ADAM_B1 = 0.9
ADAM_B2 = 0.999
ADAM_EPS = 1e-08
ADAM_WD = 0.01
ADAM_STEP = 10
LANES = 128
MIB = 1024 * 1024

MESH = pl.DeviceIdType.MESH
HBM_SPEC = pl.BlockSpec(memory_space=pltpu.HBM)
VMEM_SPEC = pl.BlockSpec(memory_space=pltpu.VMEM)


def _cp(vmem_mib):
    return pltpu.CompilerParams(vmem_limit_bytes=vmem_mib * MIB)


def _dot(a, b):
    return jnp.dot(a, b, preferred_element_type=F32)


def _dot_nt(a, b):
    return lax.dot_general(a, b, (((1,), (1,)), ((), ())), preferred_element_type=F32)


def _dot_tn(a, b):
    return lax.dot_general(a, b, (((0,), (0,)), ((), ())), preferred_element_type=F32)


def _ln_fwd(pre):
    mu = jnp.mean(pre, axis=-1, keepdims=True)
    xc = pre - mu
    var = jnp.mean(xc * xc, axis=-1, keepdims=True)
    rstd = lax.rsqrt(var + LN_EPS)
    return xc * rstd, rstd


def _ln_bwd(dy, xh, rstd, g):
    dxh = dy * g
    m1 = jnp.mean(dxh, axis=-1, keepdims=True)
    m2 = jnp.mean(dxh * xh, axis=-1, keepdims=True)
    return rstd * (dxh - m1 - xh * m2)


def _colsum(a):
    return jnp.sum(a, axis=0, keepdims=True)


def _neg_softplus(z):
    return -(jnp.maximum(z, 0.0) + jnp.log(1.0 + jnp.exp(-jnp.abs(z))))


def _row_spec(tm, n):
    return pl.BlockSpec((tm, n), lambda i: (i, 0))


def _const_spec(shape):
    nd = len(shape)
    return pl.BlockSpec(shape, lambda *_: (0,) * nd)


def _hbm(*arrays):
    return [pltpu.with_memory_space_constraint(a, pltpu.HBM) for a in arrays]


def _sds(shape, dtype):
    return pltpu.HBM(shape, dtype)


def _embln_inproj(x, g0, b0, w_in_s, tm, rider):
    S, D = x.shape
    n_t = S // tm

    def body(*refs):
        ((x_ref, g_ref, b_ref, w_ref), (xh_ref, rstd_ref, q_ref, k_ref, v_ref, u_ref), _,
         ride) = rider.split(refs, 4, 6, 0)
        i = pl.program_id(0)

        @pl.when(i == 0)
        def _():
            rider.first(ride)

        @pl.when(i == (3 * n_t) // 4)
        def _():
            rider.mid(ride)

        xh, rstd = _ln_fwd(x_ref[...])
        xh_ref[...] = xh
        rstd_ref[...] = rstd
        xb = (xh * g_ref[...] + b_ref[...]).astype(BF16)
        q_ref[...] = (_dot(xb, w_ref[0]) * Q_SCALE).astype(BF16)
        k_ref[...] = _dot(xb, w_ref[1]).astype(BF16)
        v_ref[...] = _dot(xb, w_ref[2]).astype(BF16)
        u_ref[...] = _dot(xb, w_ref[3])

        @pl.when(i == n_t - 1)
        def _():
            rider.last(ride)

    return rider.call(
        body, [x, g0, b0, w_in_s], grid=(n_t,), name="embln_inproj",
        in_specs=[_row_spec(tm, D), _const_spec((1, D)), _const_spec((1, D)),
                  _const_spec((N_CHIPS, D, 512))],
        out_specs=[_row_spec(tm, D), _row_spec(tm, 1), _row_spec(tm, 512), _row_spec(tm, 512),
                   _row_spec(tm, 512), _row_spec(tm, 512)],
        out_shape=[_sds((S, D), F32), _sds((S, 1), F32), _sds((S, 512), BF16), _sds((S, 512), BF16),
                   _sds((S, 512), BF16), _sds((S, 512), F32)],
        scratch_shapes=[], vmem_mib=40)


def _tri(n, upper):
    r = lax.broadcasted_iota(jnp.int32, (n, n), 0)
    c = lax.broadcasted_iota(jnp.int32, (n, n), 1)
    keep = (r < c) if upper else (r > c)
    return jnp.where(keep, 1.0, 0.0).astype(BF16)


def _strictly_causal(n):
    return lax.broadcasted_iota(jnp.int32, (n, n), 1) < lax.broadcasted_iota(jnp.int32, (n, n), 0)


LOG_WEIGHT_FLOOR = -110.0


def _weights_left(c_ls):
    return (jnp.max(jnp.maximum(c_ls[0], c_ls[1])) > LOG_WEIGHT_FLOOR).astype(jnp.int32)


def _sb_tile(qhs, kt, low, c_ls, valid):
    valids = valid if isinstance(valid, (list, tuple)) else [valid] * len(qhs)
    zs = [_dot_nt(qh, kt) for qh in qhs]
    lrs = [_neg_softplus(z) for z in zs]
    ls_ = [lr if m is None else jnp.where(m, lr, 0.0) for lr, m in zip(lrs, valids)]
    sfx = [_dot(l.astype(BF16), low) + c_l for l, c_l in zip(ls_, c_ls)]
    lss = [z + lr for z, lr in zip(zs, lrs)]
    ws = [jnp.exp(ls + s) for ls, s in zip(lss, sfx)]
    ws = [w if m is None else jnp.where(m, w, 0.0) for w, m in zip(ws, valids)]
    return lss, ls_, ws


def _attn_fwd(q, k, v, ga, tq, rider):
    S = q.shape[0]
    nq = S // tq

    def body(*refs):
        (q_ref, k_ref, v_ref, ga_ref), (o_ref, on_ref), (low_s,), ride = rider.split(refs, 4, 2, 1)
        p, i = pl.program_id(0), pl.program_id(1)

        @pl.when(jnp.logical_and(p == 0, i == 0))
        def _():
            rider.first(ride)
            low_s[...] = _tri(tq, upper=False)

        @pl.when(jnp.logical_and(p == N_PAIRS - 1, i == 0))
        def _():
            rider.mid(ride)

        lane = lax.broadcasted_iota(jnp.int32, (1, PAIR), 1)
        m0 = lane < HEAD_DIM
        low = low_s[...]
        q2 = q_ref[...]
        qhs = [jnp.where(m0, q2, jnp.zeros_like(q2)), jnp.where(m0, jnp.zeros_like(q2), q2)]

        def tile(kb, c_ls, accs, valid):
            ks = pl.multiple_of(kb * tq, tq)
            kt = k_ref[pl.ds(ks, tq), :]
            vt = v_ref[pl.ds(ks, tq), :]
            _, ls_, ws = _sb_tile(qhs, kt, low, c_ls, valid)
            new_a = [acc + _dot(w.astype(BF16), vt) for acc, w in zip(accs, ws)]
            new_c = [c_l + jnp.sum(l, axis=1, keepdims=True) for c_l, l in zip(c_ls, ls_)]
            return new_c, new_a

        zc, za = jnp.zeros((tq, 1), F32), jnp.zeros((tq, PAIR), F32)

        def first_two():
            c_ls, accs = tile(i, [zc, zc], [za, za], _strictly_causal(tq))
            c_ls, accs = tile(i - 1, c_ls, accs, None)
            return (_weights_left(c_ls), *c_ls, *accs)

        def first_one():
            c_ls, accs = tile(i, [zc, zc], [za, za], _strictly_causal(tq))
            return (jnp.int32(0), *c_ls, *accs)

        st0 = lax.cond(i >= 1, first_two, first_one)

        def more(st):
            return jnp.logical_and(st[0] <= i, st[1] > 0)

        def step(st):
            n, _, c0, c1, a0, a1 = st
            c_ls, accs = tile(i - n, [c0, c1], [a0, a1], None)
            return (n + 1, _weights_left(c_ls), c_ls[0], c_ls[1], accs[0], accs[1])

        st = lax.while_loop(more, step, (jnp.int32(2), *st0))
        o = jnp.where(m0, st[4], st[5])
        o_ref[...] = o
        sq = o * o
        ms0 = jnp.sum(jnp.where(m0, sq, 0.0), axis=-1, keepdims=True) * (1.0 / HEAD_DIM)
        ms1 = jnp.sum(jnp.where(m0, 0.0, sq), axis=-1, keepdims=True) * (1.0 / HEAD_DIM)
        rs = jnp.where(m0, lax.rsqrt(ms0 + RMS_EPS), lax.rsqrt(ms1 + RMS_EPS))
        on_ref[...] = (o * rs * ga_ref[...]).astype(BF16)

        @pl.when(jnp.logical_and(p == N_PAIRS - 1, i == nq - 1))
        def _():
            rider.last(ride)

    return rider.call(
        body, [q, k, v, ga], grid=(N_PAIRS, nq), name="attn_fwd",
        in_specs=[pl.BlockSpec((tq, PAIR), lambda p, i: (i, p)),
                  pl.BlockSpec((S, PAIR), lambda p, i: (0, p)),
                  pl.BlockSpec((S, PAIR), lambda p, i: (0, p)),
                  pl.BlockSpec((1, PAIR), lambda p, i: (0, p))],
        out_specs=[pl.BlockSpec((tq, PAIR), lambda p, i: (i, p)),
                   pl.BlockSpec((tq, PAIR), lambda p, i: (i, p))],
        out_shape=[_sds((S, ATTN_WIDTH), F32), _sds((S, ATTN_WIDTH), BF16)],
        scratch_shapes=[pltpu.VMEM((tq, tq), BF16)], vmem_mib=40)


def _pool_fwd(u, w_pool, pscale, tm):
    S = u.shape[0]
    hb = tm // POOL_HALO

    def body(u_ref, uh_ref, wp_ref, sc_ref, d_ref, pooled_ref):
        i = pl.program_id(0)
        halo = jnp.where(i > 0, uh_ref[...], 0.0)
        pos = i * tm + lax.broadcasted_iota(jnp.int32, (tm, 1), 0)
        for g in range(N_POOL_GROUPS):
            win = 2 ** (g + 1)
            cols = slice(g * POOL_GROUP, (g + 1) * POOL_GROUP)
            ut = u_ref[:, cols]
            s = jnp.concatenate([halo[:, cols], ut], axis=0)
            for sh in (1, 2, 4, 8)[:g + 1]:
                s = s + pltpu.roll(s, sh, 0)
            cnt = jnp.minimum(pos + 1, win).astype(F32)
            db = (s[POOL_HALO:, :] / cnt - ut).astype(BF16)
            y = _dot(db, wp_ref[g].astype(BF16))
            d_ref[:, cols] = db
            pooled_ref[:, cols] = (y * sc_ref[:, cols]).astype(BF16)

    return pl.pallas_call(
        body, grid=(S // tm,), name="pool_fwd",
        in_specs=[_row_spec(tm, POOL_WIDTH),
                  pl.BlockSpec((POOL_HALO, POOL_WIDTH), lambda i: (jnp.maximum(i * hb - 1, 0), 0)),
                  _const_spec((N_POOL_GROUPS, POOL_GROUP, POOL_GROUP)), _const_spec((1, POOL_WIDTH))],
        out_specs=[_row_spec(tm, POOL_WIDTH), _row_spec(tm, POOL_WIDTH)],
        out_shape=[_sds((S, POOL_WIDTH), BF16), _sds((S, POOL_WIDTH), BF16)],
        compiler_params=_cp(32),
    )(*_hbm(u, u, w_pool, pscale))


def _mix_ln1(on, pooled, xh0, g0, b0, w_out, g1, b1, tm, rider):
    S, D = xh0.shape
    n_t = S // tm

    def body(*refs):
        ((on_ref, po_ref, xh0_ref, g0_ref, b0_ref, w_ref, g1_ref, b1_ref), (xh_ref, rstd_ref, xb_ref), _,
         ride) = rider.split(refs, 8, 3, 0)

        @pl.when(pl.program_id(0) == 0)
        def _():
            rider.first(ride)

        mixed = _dot(on_ref[...], w_ref[:ATTN_WIDTH, :]) + _dot(po_ref[...], w_ref[ATTN_WIDTH:, :])
        x0 = xh0_ref[...] * g0_ref[...] + b0_ref[...]
        xh, rstd = _ln_fwd(ALPHA * x0 + mixed)
        xh_ref[...] = xh
        rstd_ref[...] = rstd
        xb_ref[...] = (xh * g1_ref[...] + b1_ref[...]).astype(BF16)

        @pl.when(pl.program_id(0) == n_t - 1)
        def _():
            rider.last(ride)

    return rider.call(
        body, [on, pooled, xh0, g0, b0, w_out, g1, b1], grid=(n_t,), name="mix_ln1",
        in_specs=[_row_spec(tm, ATTN_WIDTH), _row_spec(tm, POOL_WIDTH), _row_spec(tm, D),
                  _const_spec((1, D)), _const_spec((1, D)), _const_spec((D, D)),
                  _const_spec((1, D)), _const_spec((1, D))],
        out_specs=[_row_spec(tm, D), _row_spec(tm, 1), _row_spec(tm, D)],
        out_shape=[_sds((S, D), F32), _sds((S, 1), F32), _sds((S, D), BF16)],
        scratch_shapes=[], vmem_mib=40)


def _mlp_ln2(xh1, x1b, g1, b1, w_up_s, w_down, tm):
    S, D = xh1.shape
    fc = D_FF // N_CHIPS

    def body(xh_ref, xb_ref, g_ref, b_ref, wu_ref, wd_ref, xh2_ref, rstd_ref, r_ref, acc_ref):
        j = pl.program_id(1)

        @pl.when(j == 0)
        def _():
            acc_ref[...] = jnp.zeros_like(acc_ref)

        r = jnp.maximum(_dot(xb_ref[...], wu_ref[0]), 0.0)
        r_ref[...] = r.astype(BF16)
        acc_ref[...] += _dot((r * r).astype(BF16), wd_ref[...])

        @pl.when(j == N_CHIPS - 1)
        def _():
            x1 = xh_ref[...] * g_ref[...] + b_ref[...]
            xh, rstd = _ln_fwd(ALPHA * x1 + acc_ref[...])
            xh2_ref[...] = xh
            rstd_ref[...] = rstd

    return pl.pallas_call(
        body, grid=(S // tm, N_CHIPS), name="mlp_ln2",
        in_specs=[pl.BlockSpec((tm, D), lambda i, j: (i, 0)), pl.BlockSpec((tm, D), lambda i, j: (i, 0)),
                  pl.BlockSpec((1, D), lambda i, j: (0, 0)), pl.BlockSpec((1, D), lambda i, j: (0, 0)),
                  pl.BlockSpec((1, D, fc), lambda i, j: (j, 0, 0)),
                  pl.BlockSpec((fc, D), lambda i, j: (j, 0))],
        out_specs=[pl.BlockSpec((tm, D), lambda i, j: (i, 0)), pl.BlockSpec((tm, 1), lambda i, j: (i, 0)),
                   pl.BlockSpec((tm, fc), lambda i, j: (i, j))],
        out_shape=[_sds((S, D), F32), _sds((S, 1), F32), _sds((S, D_FF), BF16)],
        scratch_shapes=[pltpu.VMEM((tm, D), F32)],
        compiler_params=_cp(56),
    )(*_hbm(xh1, x1b, g1, b1, w_up_s, w_down))


def _ple_ln3_loss(xh2, rstd2, g2, b2, p, w_ple_s, w_gate, g3, b3, target, tm):
    S, D = xh2.shape
    pc = D // N_CHIPS

    def body(xh2_ref, rstd2_ref, g2_ref, b2_ref, p_ref, wp_ref, wg_ref, g3_ref, b3_ref, t_ref,
             dpre2_ref, dhb_ref, dwp_ref, dwg_ref, dg3_ref, db3_ref, dg2_ref, db2_ref, loss_ref):
        i = pl.program_id(0)

        @pl.when(i == 0)
        def _():
            for r in (dwp_ref, dwg_ref, dg3_ref, db3_ref, dg2_ref, db2_ref, loss_ref):
                r[...] = jnp.zeros_like(r)

        xh2 = xh2_ref[...]
        x2 = xh2 * g2_ref[...] + b2_ref[...]
        x2b = x2.astype(BF16)
        gate = 1.0 / (1.0 + jnp.exp(-_dot(x2b, wg_ref[...])))
        pb = p_ref[...].astype(BF16)
        pe = jnp.concatenate([_dot(pb, wp_ref[c]) for c in range(N_CHIPS)], axis=1)
        xh3, rstd3 = _ln_fwd(ALPHA * x2 + pe * gate)
        diff = xh3 * g3_ref[...] + b3_ref[...] - t_ref[...]
        loss_ref[...] += (0.5 / D) * jnp.sum(diff * diff)
        dy = diff * (1.0 / D)
        dg3_ref[...] += _colsum(dy * xh3)
        db3_ref[...] += _colsum(dy)
        dpre3 = _ln_bwd(dy, xh3, rstd3, g3_ref[...])
        dpe_b = (dpre3 * gate).astype(BF16)
        dgp_b = (dpre3 * pe * gate * (1.0 - gate)).astype(BF16)
        dx2 = ALPHA * dpre3 + _dot_nt(dgp_b, wg_ref[...])
        dwg_ref[...] += _dot_tn(x2b, dgp_b)
        for c in range(N_CHIPS):
            dwp_ref[c] += _dot_tn(pb, dpe_b[:, c * pc:(c + 1) * pc])
        dg2_ref[...] += _colsum(dx2 * xh2)
        db2_ref[...] += _colsum(dx2)
        dpre2 = _ln_bwd(dx2, xh2, rstd2_ref[...], g2_ref[...])
        dpre2_ref[...] = dpre2
        dhb_ref[...] = dpre2.astype(BF16)

    vec = _const_spec((1, D))
    return pl.pallas_call(
        body, grid=(S // tm,), name="ple_ln3_loss",
        in_specs=[_row_spec(tm, D), _row_spec(tm, 1), vec, vec, _row_spec(tm, PLE_DIM),
                  _const_spec((N_CHIPS, PLE_DIM, pc)), _const_spec((D, D)), vec, vec, _row_spec(tm, D)],
        out_specs=[_row_spec(tm, D), _row_spec(tm, D), _const_spec((N_CHIPS, PLE_DIM, pc)),
                   _const_spec((D, D)), vec, vec, vec, vec, _const_spec((1, LANES))],
        out_shape=[_sds((S, D), F32), _sds((S, D), BF16), _sds((N_CHIPS, PLE_DIM, pc), F32),
                   _sds((D, D), F32), _sds((1, D), F32), _sds((1, D), F32), _sds((1, D), F32),
                   _sds((1, D), F32), _sds((1, LANES), F32)],
        compiler_params=_cp(48),
    )(*_hbm(xh2, rstd2, g2, b2, p, w_ple_s, w_gate, g3, b3, target))


def _mlp_bwd(rb, dhb, w_up_s, w_down, tm, rider):
    S, D = dhb.shape
    fc = D_FF // N_CHIPS
    n_t = S // tm

    def body(*refs):
        (r_ref, dh_ref, wu_ref, wd_ref), (dx_ref, da_ref), _, ride = rider.split(refs, 4, 2, 0)
        i, j = pl.program_id(0), pl.program_id(1)

        @pl.when(jnp.logical_and(i == 0, j == 0))
        def _():
            rider.first(ride)

        @pl.when(j == 0)
        def _():
            dx_ref[...] = jnp.zeros_like(dx_ref)

        da = (_dot_nt(dh_ref[...], wd_ref[...]) * (2.0 * r_ref[...].astype(F32))).astype(BF16)
        da_ref[...] = da
        dx_ref[...] += _dot_nt(da, wu_ref[0])

        @pl.when(jnp.logical_and(i == n_t - 1, j == N_CHIPS - 1))
        def _():
            rider.last(ride)

    return rider.call(
        body, [rb, dhb, w_up_s, w_down], grid=(n_t, N_CHIPS), name="mlp_bwd",
        in_specs=[pl.BlockSpec((tm, fc), lambda i, j: (i, j)), pl.BlockSpec((tm, D), lambda i, j: (i, 0)),
                  pl.BlockSpec((1, D, fc), lambda i, j: (j, 0, 0)),
                  pl.BlockSpec((fc, D), lambda i, j: (j, 0))],
        out_specs=[pl.BlockSpec((tm, D), lambda i, j: (i, 0)), pl.BlockSpec((tm, fc), lambda i, j: (i, j))],
        out_shape=[_sds((S, D), F32), _sds((S, D_FF), BF16)],
        scratch_shapes=[], vmem_mib=56)


def _tn_matmul(a, b, name, tk, tt, stacked, rider, square_a=False):
    T, K = a.shape
    N = b.shape[1]
    tn = 1024
    grid = (K // tk, N // tn, T // tt)

    def body(*refs):
        (a_ref, b_ref), (o_ref,), _, ride = rider.split(refs, 2, 1, 0)
        at = [pl.program_id(d) for d in range(3)]

        @pl.when(jnp.logical_and(jnp.logical_and(at[0] == 0, at[1] == 0), at[2] == 0))
        def _():
            rider.first(ride)

        @pl.when(at[2] == 0)
        def _():
            o_ref[...] = jnp.zeros_like(o_ref)

        a_t = a_ref[...]
        if square_a:
            a_t = a_t * a_t
        prod = _dot_tn(a_t, b_ref[...])
        if stacked:
            o_ref[0] += prod
        else:
            o_ref[...] += prod

        @pl.when(jnp.logical_and(jnp.logical_and(at[0] == grid[0] - 1, at[1] == grid[1] - 1),
                                 at[2] == grid[2] - 1))
        def _():
            rider.last(ride)

    if stacked:
        out_spec = pl.BlockSpec((1, tk, tn), lambda k, n, t: (n, k, 0))
        out_shape = _sds((N // tn, K, tn), F32)
    else:
        out_spec = pl.BlockSpec((tk, tn), lambda k, n, t: (k, n))
        out_shape = _sds((K, N), F32)
    return rider.call(
        body, [a, b], grid=grid, name=name,
        in_specs=[pl.BlockSpec((tt, tk), lambda k, n, t: (t, k)),
                  pl.BlockSpec((tt, tn), lambda k, n, t: (t, n))],
        out_specs=[out_spec], out_shape=[out_shape], scratch_shapes=[], vmem_mib=40)


def _mix_bwd(dpre2, dx1m, xh1, rstd1, g1, w_out, on, pooled, tm, rider):
    S, D = xh1.shape
    n_t = S // tm

    def body(*refs):
        ((dp2_ref, dxm_ref, xh_ref, rstd_ref, g_ref, w_ref, on_ref, po_ref),
         (dpre1_ref, don_ref, dpo_ref, dw_ref, dg_ref, db_ref), _, ride) = rider.split(refs, 8, 6, 0)

        @pl.when(pl.program_id(0) == 0)
        def _():
            rider.first(ride)
            for r in (dw_ref, dg_ref, db_ref):
                r[...] = jnp.zeros_like(r)

        xh = xh_ref[...]
        dx1 = ALPHA * dp2_ref[...] + dxm_ref[...]
        dg_ref[...] += _colsum(dx1 * xh)
        db_ref[...] += _colsum(dx1)
        dpre1 = _ln_bwd(dx1, xh, rstd_ref[...], g_ref[...])
        dpre1_ref[...] = dpre1
        dmb = dpre1.astype(BF16)
        dcat = _dot_nt(dmb, w_ref[...])
        don_ref[...] = dcat[:, :ATTN_WIDTH]
        dpo_ref[...] = dcat[:, ATTN_WIDTH:]
        dw_ref[:ATTN_WIDTH, :] += _dot_tn(on_ref[...], dmb)
        dw_ref[ATTN_WIDTH:, :] += _dot_tn(po_ref[...], dmb)

        @pl.when(pl.program_id(0) == n_t - 1)
        def _():
            rider.last(ride)

    vec = _const_spec((1, D))
    return rider.call(
        body, [dpre2, dx1m, xh1, rstd1, g1, w_out, on, pooled], grid=(n_t,), name="mix_bwd",
        in_specs=[_row_spec(tm, D), _row_spec(tm, D), _row_spec(tm, D), _row_spec(tm, 1), vec,
                  _const_spec((D, D)), _row_spec(tm, ATTN_WIDTH), _row_spec(tm, POOL_WIDTH)],
        out_specs=[_row_spec(tm, D), _row_spec(tm, ATTN_WIDTH), _row_spec(tm, POOL_WIDTH),
                   _const_spec((D, D)), vec, vec],
        out_shape=[_sds((S, D), F32), _sds((S, ATTN_WIDTH), F32), _sds((S, POOL_WIDTH), F32),
                   _sds((D, D), F32), _sds((1, D), F32), _sds((1, D), F32)],
        scratch_shapes=[], vmem_mib=48)


def _pool_bwd(dpooled, d_b, w_pool, pscale, tm, rider):
    S = dpooled.shape[0]
    hb = tm // POOL_HALO
    n_t = S // tm
    te = tm + POOL_HALO

    def body(*refs):
        ((dp_ref, dph_ref, d_ref, wp_ref, sc_ref), (du_ref, dwp_ref, dsc_ref), _,
         ride) = rider.split(refs, 5, 3, 0)
        i = pl.program_id(0)

        @pl.when(i == 0)
        def _():
            rider.first(ride)
            dwp_ref[...] = jnp.zeros_like(dwp_ref)
            dsc_ref[...] = jnp.zeros_like(dsc_ref)

        halo = jnp.where(i < n_t - 1, dph_ref[...], 0.0)
        pos = i * tm + lax.broadcasted_iota(jnp.int32, (te, 1), 0)
        for g in range(N_POOL_GROUPS):
            win = 2 ** (g + 1)
            cols = slice(g * POOL_GROUP, (g + 1) * POOL_GROUP)
            wpb = wp_ref[g].astype(BF16)
            dpt = dp_ref[:, cols]
            dpe = jnp.concatenate([dpt, halo[:, cols]], axis=0)
            dyb = (dpe * sc_ref[:, cols]).astype(BF16)
            dd = _dot_nt(dyb, wpb)
            s = dd / jnp.minimum(pos + 1, win).astype(F32)
            for sh in (1, 2, 4, 8)[:g + 1]:
                s = s + pltpu.roll(s, te - sh, 0)
            du_ref[:, cols] = s[:tm, :] - dd[:tm, :]
            db = d_ref[:, cols]
            dwp_ref[g] += _dot_tn(db, dyb[:tm, :])
            dsc_ref[:, cols] += _colsum(dpt * _dot(db, wpb))

        @pl.when(i == n_t - 1)
        def _():
            rider.last(ride)

    return rider.call(
        body, [dpooled, dpooled, d_b, w_pool, pscale], grid=(n_t,), name="pool_bwd",
        in_specs=[_row_spec(tm, POOL_WIDTH),
                  pl.BlockSpec((POOL_HALO, POOL_WIDTH),
                               lambda i: (jnp.minimum((i + 1) * hb, S // POOL_HALO - 1), 0)),
                  _row_spec(tm, POOL_WIDTH),
                  _const_spec((N_POOL_GROUPS, POOL_GROUP, POOL_GROUP)), _const_spec((1, POOL_WIDTH))],
        out_specs=[_row_spec(tm, POOL_WIDTH), _const_spec((N_POOL_GROUPS, POOL_GROUP, POOL_GROUP)),
                   _const_spec((1, POOL_WIDTH))],
        out_shape=[_sds((S, POOL_WIDTH), F32), _sds((N_POOL_GROUPS, POOL_GROUP, POOL_GROUP), F32),
                   _sds((1, POOL_WIDTH), F32)],
        scratch_shapes=[], vmem_mib=32)


def _attn_bwd(q, k, v, don, o_raw, ga, tq, rider):
    S = q.shape[0]
    nq = S // tq

    def body(*refs):
        ((q_ref, k_ref, v_ref, don_ref, o_ref, ga_ref), (dq_ref, dk_ref, dv_ref, dga_ref),
         (g_s, b_s, low_s, upp_s), ride) = rider.split(refs, 6, 4, 4)
        p, i = pl.program_id(0), pl.program_id(1)

        @pl.when(jnp.logical_and(p == 0, i == 0))
        def _():
            rider.first(ride)
            low_s[...] = _tri(tq, upper=False)
            upp_s[...] = _tri(tq, upper=True)

        @pl.when(i == 0)
        def _():
            for r in (dk_ref, dv_ref, dga_ref):
                r[...] = jnp.zeros_like(r)

        lane = lax.broadcasted_iota(jnp.int32, (1, PAIR), 1)
        m0 = lane < HEAD_DIM
        low = low_s[...]
        upp = upp_s[...]

        def seg_mean(a):
            s0 = jnp.sum(jnp.where(m0, a, 0.0), axis=-1, keepdims=True)
            s1 = jnp.sum(jnp.where(m0, 0.0, a), axis=-1, keepdims=True)
            return jnp.where(m0, s0, s1) * (1.0 / HEAD_DIM)

        o = o_ref[...]
        rs = lax.rsqrt(seg_mean(o * o) + RMS_EPS)
        oh = o * rs
        don = don_ref[...]
        dga_ref[...] += _colsum(don * oh)
        doh = don * ga_ref[...]
        do = rs * (doh - oh * seg_mean(doh * oh))
        dob = do.astype(BF16)
        q2 = q_ref[...]
        qhs = [jnp.where(m0, q2, jnp.zeros_like(q2)), jnp.where(m0, jnp.zeros_like(q2), q2)]
        dhs = [jnp.where(m0, dob, jnp.zeros_like(dob)), jnp.where(m0, jnp.zeros_like(dob), dob)]
        causal = _strictly_causal(tq)

        def down(kb, c_ls, valid):
            ks = pl.multiple_of(kb * tq, tq)
            kt = k_ref[pl.ds(ks, tq), :]
            vt = v_ref[pl.ds(ks, tq), :]
            lss, ls_, ws = _sb_tile(qhs, kt, low, c_ls, valid)
            dws = [_dot_nt(dh, vt) for dh in dhs]
            for hh in range(2):
                g_s[hh, kb] = dws[hh] * ws[hh]
                b_s[hh, kb] = jnp.exp(lss[hh])
            dv_ref[pl.ds(ks, tq), :] += (_dot_tn(ws[0].astype(BF16), dhs[0])
                                         + _dot_tn(ws[1].astype(BF16), dhs[1]))
            return [c_l + jnp.sum(l, axis=1, keepdims=True) for c_l, l in zip(c_ls, ls_)]

        zc, za = jnp.zeros((tq, 1), F32), jnp.zeros((tq, PAIR), F32)
        def first_two():
            c_ls = down(i - 1, down(i, [zc, zc], causal), None)
            return (_weights_left(c_ls), *c_ls)

        st0 = lax.cond(i >= 1, first_two, lambda: (jnp.int32(0), *down(i, [zc, zc], causal)))

        def more(st):
            return jnp.logical_and(st[0] <= i, st[1] > 0)

        def down_step(st):
            c_ls = down(i - st[0], [st[2], st[3]], None)
            return (st[0] + 1, _weights_left(c_ls), c_ls[0], c_ls[1])

        n_tiles = lax.while_loop(more, down_step, (jnp.int32(2), *st0))[0]

        def up(kb, c_gs, accs, valid):
            ks = pl.multiple_of(kb * tq, tq)
            kt = k_ref[pl.ds(ks, tq), :]
            gs = [g_s[hh, kb] for hh in range(2)]
            pres = [_dot(g.astype(BF16), upp) + c_g for g, c_g in zip(gs, c_gs)]
            dzs = []
            for hh in range(2):
                beta = b_s[hh, kb]
                dz = gs[hh] - beta * (gs[hh] + pres[hh])
                if valid is not None:
                    dz = jnp.where(valid, dz, 0.0)
                dzs.append(dz.astype(BF16))
            new_a = [acc + _dot(dzb, kt) for acc, dzb in zip(accs, dzs)]
            dk_ref[pl.ds(ks, tq), :] += _dot_tn(dzs[0], qhs[0]) + _dot_tn(dzs[1], qhs[1])
            new_c = [c_g + jnp.sum(g, axis=1, keepdims=True) for c_g, g in zip(c_gs, gs)]
            return new_c, new_a

        def up_step(kb, st):
            c_gs, accs = up(kb, [st[0], st[1]], [st[2], st[3]], None)
            return (c_gs[0], c_gs[1], accs[0], accs[1])

        st = lax.fori_loop(i - n_tiles + 1, i - 1, up_step, (zc, zc, za, za))

        def last_two():
            c_gs, accs = up(i - 1, [st[0], st[1]], [st[2], st[3]], None)
            return tuple(up(i, c_gs, accs, causal)[1])

        accs = lax.cond(i >= 1, last_two, lambda: tuple(up(i, [zc, zc], [za, za], causal)[1]))
        dq_ref[...] = jnp.where(m0, accs[0], accs[1]) * Q_SCALE

        @pl.when(jnp.logical_and(p == N_PAIRS - 1, i == nq - 1))
        def _():
            rider.last(ride)

    return rider.call(
        body, [q, k, v, don, o_raw, ga], grid=(N_PAIRS, nq), name="attn_bwd",
        in_specs=[pl.BlockSpec((tq, PAIR), lambda p, i: (i, p)),
                  pl.BlockSpec((S, PAIR), lambda p, i: (0, p)),
                  pl.BlockSpec((S, PAIR), lambda p, i: (0, p)),
                  pl.BlockSpec((tq, PAIR), lambda p, i: (i, p)),
                  pl.BlockSpec((tq, PAIR), lambda p, i: (i, p)),
                  pl.BlockSpec((1, PAIR), lambda p, i: (0, p))],
        out_specs=[pl.BlockSpec((tq, PAIR), lambda p, i: (i, p)),
                   pl.BlockSpec((S, PAIR), lambda p, i: (0, p)),
                   pl.BlockSpec((S, PAIR), lambda p, i: (0, p)),
                   pl.BlockSpec((1, PAIR), lambda p, i: (0, p))],
        out_shape=[_sds((S, ATTN_WIDTH), F32), _sds((S, ATTN_WIDTH), F32), _sds((S, ATTN_WIDTH), F32),
                   _sds((1, ATTN_WIDTH), F32)],
        scratch_shapes=[pltpu.VMEM((2, nq, tq, tq), F32), pltpu.VMEM((2, nq, tq, tq), F32),
                        pltpu.VMEM((tq, tq), BF16), pltpu.VMEM((tq, tq), BF16)],
        vmem_mib=56)


def _inproj_bwd(dq, dk, dv, du, dpre1, xh0, rstd0, g0, b0, w_in_s, tm):
    S, D = xh0.shape

    def body(dq_ref, dk_ref, dv_ref, du_ref, dp1_ref, xh_ref, rstd_ref, g_ref, b_ref, w_ref,
             gx_ref, dw_ref, dg_ref, db_ref):
        @pl.when(pl.program_id(0) == 0)
        def _():
            for r in (dw_ref, dg_ref, db_ref):
                r[...] = jnp.zeros_like(r)

        xh = xh_ref[...]
        xb = (xh * g_ref[...] + b_ref[...]).astype(BF16)
        dx0 = ALPHA * dp1_ref[...]
        for c, r in enumerate((dq_ref, dk_ref, dv_ref, du_ref)):
            dpb = r[...].astype(BF16)
            dx0 = dx0 + _dot_nt(dpb, w_ref[c])
            dw_ref[c] += _dot_tn(xb, dpb)
        dg_ref[...] += _colsum(dx0 * xh)
        db_ref[...] += _colsum(dx0)
        gx_ref[...] = _ln_bwd(dx0, xh, rstd_ref[...], g_ref[...])

    vec = _const_spec((1, D))
    half = _row_spec(tm, 512)
    return pl.pallas_call(
        body, grid=(S // tm,), name="inproj_bwd",
        in_specs=[half, half, half, half, _row_spec(tm, D), _row_spec(tm, D), _row_spec(tm, 1), vec, vec,
                  _const_spec((N_CHIPS, D, 512))],
        out_specs=[_row_spec(tm, D), _const_spec((N_CHIPS, D, 512)), vec, vec],
        out_shape=[_sds((S, D), F32), _sds((N_CHIPS, D, 512), F32), _sds((1, D), F32), _sds((1, D), F32)],
        compiler_params=_cp(56),
    )(*_hbm(dq, dk, dv, du, dpre1, xh0, rstd0, g0, b0, w_in_s))


def _place():
    return lax.axis_index("x"), lax.axis_index("y"), lax.axis_index("c")


CHIP_FLIPS = ((0, 1), (1, 0), (1, 1))


class _Rider:
    def __init__(self, ins, out_shapes, n_sem, phases, aliases=None):
        self.ins, self.out_shapes, self.n_sem, self.phases = list(ins), list(out_shapes), n_sem, phases
        self.aliases = aliases or {}

    def __add__(self, other):
        na, ma = len(self.ins), len(self.out_shapes)

        def phases(ins, outs, ssem, rsem):
            mine = self.phases(ins[:na], outs[:ma], ssem, rsem)
            rest = pl.ds(self.n_sem, other.n_sem)
            theirs = other.phases(ins[na:], outs[ma:], ssem.at[rest], rsem.at[rest])
            assert len(mine) == 1 and len(theirs) == 1
            return [mine[0] + theirs[0]]

        aliases = {**self.aliases, **{na + i: ma + o for i, o in other.aliases.items()}}
        return _Rider(self.ins + other.ins, self.out_shapes + other.out_shapes, self.n_sem + other.n_sem, phases,
                      aliases)

    def split(self, refs, n_in, n_out, n_scratch):
        a = n_in + len(self.ins)
        b = a + n_out
        c = b + len(self.out_shapes)
        own = (refs[:n_in], refs[a:b], refs[c:c + n_scratch])
        return own + ((refs[n_in:a], refs[b:c]) + tuple(refs[c + n_scratch:]),)

    def first(self, ride):
        for make in self.phases(*ride)[0]:
            make().start()

    def mid(self, ride):
        ph = self.phases(*ride)
        if len(ph) == 2:
            for make in ph[0]:
                make().wait_recv()
            for make in ph[1]:
                make().start()

    def last(self, ride):
        ph = self.phases(*ride)
        if len(ph) == 2:
            for make in ph[0]:
                make().wait_send()
        for make in ph[-1]:
            make().wait()

    def call(self, body, args, *, grid, name, in_specs, out_specs, out_shape, scratch_shapes, vmem_mib,
             prefetch=None):
        n_in, n_out = len(in_specs), len(out_specs)
        sems = [pltpu.SemaphoreType.DMA((self.n_sem,)), pltpu.SemaphoreType.DMA((self.n_sem,))]
        n_pre = 0 if prefetch is None else 1
        grid_spec = pltpu.PrefetchScalarGridSpec(
            num_scalar_prefetch=n_pre, grid=grid,
            in_specs=list(in_specs) + [HBM_SPEC] * len(self.ins),
            out_specs=list(out_specs) + [HBM_SPEC] * len(self.out_shapes),
            scratch_shapes=list(scratch_shapes) + sems)
        return pl.pallas_call(
            body, name=name, grid_spec=grid_spec,
            out_shape=list(out_shape) + self.out_shapes,
            input_output_aliases={n_pre + n_in + i: n_out + o for i, o in self.aliases.items()},
            compiler_params=_cp(vmem_mib),
        )(*([] if prefetch is None else [prefetch]), *_hbm(*args), *self.ins)

    def run(self, name):
        def body(*refs):
            ride = self.split(refs, 0, 0, 0)[3]
            self.first(ride)
            self.mid(ride)
            self.last(ride)

        return self.call(body, [], grid=(), name=name, in_specs=[], out_specs=[], out_shape=[],
                         scratch_shapes=[], vmem_mib=16)


def _remote(src, dst, ssem, rsem, n, dev):
    return functools.partial(pltpu.make_async_remote_copy, src_ref=src, dst_ref=dst, send_sem=ssem.at[n],
                             recv_sem=rsem.at[n], device_id=dev, device_id_type=MESH)


def _cast_into_slot(w, place, name):
    R, C = w.shape
    tr = min(R, 512)

    def body(pl_ref, w_ref, o_ref):
        o_ref[0] = w_ref[...].astype(BF16)

    return pl.pallas_call(
        body, name=name,
        grid_spec=pltpu.PrefetchScalarGridSpec(
            num_scalar_prefetch=1, grid=(R // tr,),
            in_specs=[pl.BlockSpec((tr, C), lambda r, pr: (r, 0))],
            out_specs=pl.BlockSpec((1, tr, C), lambda r, pr: (pr[1], r, 0))),
        out_shape=_sds((N_CHIPS, R, C), BF16),
    )(place, w)


CAST_STEPS = 8


def _cast_rest(ws, place, rider):
    n = len(ws)

    def body(pl_ref, *refs):
        w_refs, o_refs, _, ride = rider.split(refs, n, n, 0)
        r = pl.program_id(0)

        @pl.when(r == 0)
        def _():
            rider.first(ride)

        @pl.when(r == CAST_STEPS // 2)
        def _():
            rider.mid(ride)

        for w_ref, o_ref in zip(w_refs, o_refs):
            o_ref[0] = w_ref[...].astype(BF16)

        @pl.when(r == CAST_STEPS - 1)
        def _():
            rider.last(ride)

    def rows(w):
        return w.shape[0] // CAST_STEPS

    return rider.call(
        body, ws, grid=(CAST_STEPS,), name="cast_weights", prefetch=place,
        in_specs=[pl.BlockSpec((rows(w), w.shape[1]), lambda r, pr: (r, 0)) for w in ws],
        out_specs=[pl.BlockSpec((1, rows(w), w.shape[1]), lambda r, pr: (pr[1], r, 0)) for w in ws],
        out_shape=[_sds((N_CHIPS,) + w.shape, BF16) for w in ws], scratch_shapes=[], vmem_mib=32)


def _gather_rider(stacked, part="both"):
    n, nf = len(stacked), len(CHIP_FLIPS)

    def phases(ins, outs, ssem, rsem):
        x, y, c = _place()
        slot = 2 * x + y
        ici, d2d = [], []
        for w, (i_ref, o_ref) in enumerate(zip(ins, outs)):
            hh = o_ref.shape[1] // 2
            rows = pl.ds(c * hh, hh)
            for f, (fx, fy) in enumerate(CHIP_FLIPS):
                k = w * nf + f
                theirs = 2 * (x ^ fx) + (y ^ fy)
                if part != "pair":
                    ici.append(_remote(i_ref.at[slot, rows], o_ref.at[slot, rows], ssem, rsem, k,
                                       (x ^ fx, y ^ fy, c)))
                if part != "chips":
                    d2d.append(_remote(o_ref.at[theirs, rows], o_ref.at[theirs, rows], ssem, rsem,
                                       (n * nf if part == "both" else 0) + k, (x, y, 1 - c)))
        return [ph for ph in (ici, d2d) if ph]

    return _Rider(stacked, [_sds(s.shape, s.dtype) for s in stacked], (2 if part == "both" else 1) * n * nf,
                  phases, aliases={i: i for i in range(n)})


def _pair_swap_rider(grads):
    def phases(ins, outs, ssem, rsem):
        x, y, c = _place()
        return [[_remote(g.at[:, 1 - c], o, ssem, rsem, k, (x, y, 1 - c))
                 for k, (g, o) in enumerate(zip(ins, outs))]]

    return _Rider(grads, [_sds((N_CHIPS,) + g.shape[2:], g.dtype) for g in grads], len(grads), phases)


def _chip_scatter_rider(parts):
    nf = len(CHIP_FLIPS)

    def phases(ins, outs, ssem, rsem):
        x, y, c = _place()
        return [[_remote(r.at[2 * (x ^ fx) + (y ^ fy)], o.at[f], ssem, rsem, w * nf + f, (x ^ fx, y ^ fy, c))
                 for w, (r, o) in enumerate(zip(ins, outs)) for f, (fx, fy) in enumerate(CHIP_FLIPS)]]

    return _Rider(parts, [_sds((nf,) + r.shape[1:], r.dtype) for r in parts], len(parts) * nf, phases)


def _pair_send_rider(halves):
    def phases(ins, outs, ssem, rsem):
        x, y, c = _place()
        return [[_remote(h, o, ssem, rsem, k, (x, y, 1 - c)) for k, (h, o) in enumerate(zip(ins, outs))]]

    return _Rider(halves, [_sds(h.shape, h.dtype) for h in halves], len(halves), phases)


PAIR_SUM_STEPS = 2
CHIP_SUM_STEPS = 4
ADAMW_STEPS = 4


def _no_rider():
    return _Rider([], [], 1, lambda ins, outs, ssem, rsem: [[]])


def _add_pair(grads, recvs, place, name, rider):
    n = len(grads)

    def body(pl_ref, *refs):
        ins, outs, _, ride = rider.split(refs, 2 * n, 2 * n, 0)
        j, h = pl.program_id(0), pl.program_id(1)

        @pl.when(jnp.logical_and(j == 0, h == 0))
        def _():
            rider.first(ride)

        for w in range(n):
            s = ins[2 * w][:, 0] + ins[2 * w + 1][...]
            outs[2 * w][...] = s
            outs[2 * w + 1][...] = s.astype(BF16)

        @pl.when(jnp.logical_and(j == N_CHIPS - 1, h == PAIR_SUM_STEPS - 1))
        def _():
            rider.last(ride)

    in_specs, out_specs, out_shape, args = [], [], [], []
    for g, r in zip(grads, recvs):
        _, _, H, C = g.shape
        th = H // PAIR_SUM_STEPS
        spec = pl.BlockSpec((1, th, C), lambda j, h, pr: (j, h, 0))
        in_specs += [pl.BlockSpec((1, 1, th, C), lambda j, h, pr: (j, pr[0], h, 0)), spec]
        out_specs += [spec, spec]
        out_shape += [_sds((N_CHIPS, H, C), F32), _sds((N_CHIPS, H, C), BF16)]
        args += [g, r]
    res = rider.call(body, args, grid=(N_CHIPS, PAIR_SUM_STEPS), name=name, prefetch=place, in_specs=in_specs,
                     out_specs=out_specs, out_shape=out_shape, scratch_shapes=[], vmem_mib=32)
    return [(res[2 * w], res[2 * w + 1]) for w in range(n)], res[2 * n:]


def _add_chips(parts, recvs, place, name, rider):
    n = len(parts)

    def body(pl_ref, *refs):
        ins, outs, _, ride = rider.split(refs, 2 * n, n, 0)
        h = pl.program_id(0)

        @pl.when(h == 0)
        def _():
            rider.first(ride)

        for w in range(n):
            p_ref, r_ref = ins[2 * w], ins[2 * w + 1]
            outs[w][...] = p_ref[0] + r_ref[0].astype(F32) + r_ref[1].astype(F32) + r_ref[2].astype(F32)

        @pl.when(h == CHIP_SUM_STEPS - 1)
        def _():
            rider.last(ride)

    in_specs, out_specs, out_shape, args = [], [], [], []
    for p, r in zip(parts, recvs):
        _, H, C = p.shape
        th = H // CHIP_SUM_STEPS
        in_specs += [pl.BlockSpec((1, th, C), lambda h, pr: (pr[1], h, 0)),
                     pl.BlockSpec((len(CHIP_FLIPS), th, C), lambda h, pr: (0, h, 0))]
        out_specs.append(pl.BlockSpec((th, C), lambda h, pr: (h, 0)))
        out_shape.append(_sds((H, C), F32))
        args += [p, r]
    res = rider.call(body, args, grid=(CHIP_SUM_STEPS,), name=name, prefetch=place, in_specs=in_specs,
                     out_specs=out_specs, out_shape=out_shape, scratch_shapes=[], vmem_mib=32)
    return res[:n], res[n:]


def _adamw_math(w, g, m, v):
    m = ADAM_B1 * m + (1.0 - ADAM_B1) * g
    v = ADAM_B2 * v + (1.0 - ADAM_B2) * (g * g)
    m_hat = m / (1.0 - ADAM_B1 ** ADAM_STEP)
    v_hat = v / (1.0 - ADAM_B2 ** ADAM_STEP)
    delta = -ADAM_LR * (m_hat / (jnp.sqrt(v_hat) + ADAM_EPS) + ADAM_WD * w)
    return delta, m, v


def _adamw(ws, mines, theirs, ms, vs, place, name, rider):
    n = len(ws)

    def body(pl_ref, *refs):
        ins, outs, _, ride = rider.split(refs, 5 * n, 4 * n, 0)
        h, r = pl.program_id(0), pl.program_id(1)

        @pl.when(jnp.logical_and(h == 0, r == 0))
        def _():
            rider.first(ride)

        for k in range(n):
            w_ref, a_ref, b_ref, m_ref, v_ref = ins[5 * k:5 * k + 5]
            g = jnp.where(h == pl_ref[0], a_ref[...], b_ref[...])
            d, mo, vo = _adamw_math(w_ref[...], g, m_ref[...], v_ref[...])
            for o_ref, val in zip(outs[4 * k:4 * k + 4], (g, d, mo, vo)):
                o_ref[...] = val

        @pl.when(jnp.logical_and(h == 1, r == ADAMW_STEPS - 1))
        def _():
            rider.last(ride)

    in_specs, out_specs, out_shape, args = [], [], [], []
    for w, a, b, m, v in zip(ws, mines, theirs, ms, vs):
        R, C = w.shape
        th = (R // 2) // ADAMW_STEPS
        whole = pl.BlockSpec((th, C), lambda h, r, pr: (h * ADAMW_STEPS + r, 0))
        mine_spec = pl.BlockSpec((th, C), lambda h, r, pr: (jnp.where(h == pr[0], r, 0), 0))
        theirs_spec = pl.BlockSpec((th, C), lambda h, r, pr: (jnp.where(h == pr[0], 0, r), 0))
        in_specs += [whole, mine_spec, theirs_spec, whole, whole]
        out_specs += [whole] * 4
        out_shape += [_sds((R, C), F32)] * 4
        args += [w, a, b, m, v]
    res = rider.call(body, args, grid=(2, ADAMW_STEPS), name=name, prefetch=place, in_specs=in_specs,
                     out_specs=out_specs, out_shape=out_shape, scratch_shapes=[], vmem_mib=40)
    return [tuple(res[4 * k:4 * k + 4]) for k in range(n)], res[4 * n:]


DEVICE_FLIPS = tuple((fx, fy, fc) for fx in (0, 1) for fy in (0, 1) for fc in (0, 1))[1:]


def _pack_exchange_rider(pack):
    def phases(ins, outs, ssem, rsem):
        x, y, c = _place()
        mine = outs[0].at[4 * x + 2 * y + c]
        copies = [_remote(ins[0], mine, ssem, rsem, k, (x ^ fx, y ^ fy, c ^ fc))
                  for k, (fx, fy, fc) in enumerate(DEVICE_FLIPS)]
        copies.append(functools.partial(pltpu.make_async_copy, ins[0], mine, ssem.at[len(DEVICE_FLIPS)]))
        return [copies]

    return _Rider([pack], [_sds((N_DEV,) + pack.shape, pack.dtype)], len(DEVICE_FLIPS) + 1, phases)


def _small_sum_adamw(recv_a, recv_b, wpack, mpack, vpack):
    R = wpack.shape[0]

    def body(a_ref, b_ref, w_ref, m_ref, v_ref, gs_ref, d_ref, mo_ref, vo_ref):
        ta, tb = a_ref[0], b_ref[0]
        for dev in range(1, N_DEV):
            ta = ta + a_ref[dev]
            tb = tb + b_ref[dev]
        total = jnp.concatenate([ta, tb], axis=0)
        gs_ref[...] = total
        d, mo, vo = _adamw_math(w_ref[...], total, m_ref[...], v_ref[...])
        d_ref[...] = d
        mo_ref[...] = mo
        vo_ref[...] = vo

    return pl.pallas_call(
        body, name="small_sum_adamw", in_specs=[VMEM_SPEC] * 5, out_specs=[VMEM_SPEC] * 4,
        out_shape=[_sds((R, LANES), F32)] * 4,
    )(recv_a, recv_b, wpack, mpack, vpack)


def _rows8(a):
    a = a.reshape(-1, LANES)
    pad = (-a.shape[0]) % 8
    return jnp.pad(a, ((0, pad), (0, 0))) if pad else a


def _pack(parts):
    return jnp.concatenate([_rows8(a) for a in parts], axis=0)


def _unpack(pack, like):
    out, row = [], 0
    for a in like:
        n = a.size // LANES
        out.append(pack[row:row + n].reshape(a.shape))
        row += n + (-n) % 8
    return out


def kernel(x, p, emb_ln_g, emb_ln_b, w_in, attn_out_g, w_pool, pool_scale, w_out, ln1_g, ln1_b, w_up, w_down, ln2_g, ln2_b, w_ple, w_ple_gate, ln3_g, ln3_b, loss_target, m_emb_ln_g, m_emb_ln_b, m_w_in, m_attn_out_g, m_w_pool, m_pool_scale, m_w_out, m_ln1_g, m_ln1_b, m_w_up, m_w_down, m_ln2_g, m_ln2_b, m_w_ple, m_w_ple_gate, m_ln3_g, m_ln3_b, v_emb_ln_g, v_emb_ln_b, v_w_in, v_attn_out_g, v_w_pool, v_pool_scale, v_w_out, v_ln1_g, v_ln1_b, v_w_up, v_w_down, v_ln2_g, v_ln2_b, v_w_ple, v_w_ple_gate, v_ln3_g, v_ln3_b):
    S = x.shape[1]
    tm = min(256, S)
    tq = min(256, S)
    tm_mlp = min(1024, S)
    tm_pool = min(1024, S)
    tm_ln = min(512, S)
    xs = x[0]
    ps = p[0, 0]
    tgt = loss_target[0]
    row = lambda a: a.reshape(1, -1)
    g0, b0 = row(emb_ln_g), row(emb_ln_b)
    g1, b1, g2, b2, g3, b3 = ln1_g, ln1_b, ln2_g, ln2_b, ln3_g, ln3_b
    wp = w_pool[0]

    xi, yi, ci = _place()
    place = jnp.stack([ci, 2 * xi + yi]).astype(jnp.int32)
    names = ["w_in", "w_out", "w_up", "w_down", "w_ple", "w_ple_gate"]

    big = [w_in[0], w_out[0], w_up[0], w_down[0], w_ple[0], w_ple_gate[0]]
    s_in = _cast_into_slot(big[0], place, "cast_w_in")
    s_out, s_up, s_down, s_ple, s_gate, w_in_s = _cast_rest(big[1:], place, _gather_rider([s_in]))

    xh0, rstd0, q, k, v, u, s_out, s_ple, s_gate = _embln_inproj(
        xs, g0, b0, w_in_s, tm_ln, _gather_rider([s_out, s_ple, s_gate], "chips"))
    o_raw, on, s_up, s_down, w_out_s, w_ple_s, w_gate_s = _attn_fwd(
        q, k, v, attn_out_g, tq, _gather_rider([s_up, s_down], "chips") + _gather_rider([s_out, s_ple, s_gate], "pair"))
    w_out_f = w_out_s.reshape(D_MODEL, D_MODEL)
    w_gate_f = w_gate_s.reshape(D_MODEL, D_MODEL)
    d_b, pooled = _pool_fwd(u, wp, pool_scale, tm_pool)
    xh1, rstd1, x1b, w_up_s, w_down_s = _mix_ln1(on, pooled, xh0, g0, b0, w_out_f, g1, b1, tm_ln,
                                                 _gather_rider([s_up, s_down], "pair"))
    w_down_f = w_down_s.reshape(D_FF, D_MODEL)
    xh2, rstd2, rb = _mlp_ln2(xh1, x1b, g1, b1, w_up_s, w_down_f, tm_mlp)

    (dpre2, dhb, dw_ple, dw_gate, dg3, db3, dg2, db2, loss_row) = _ple_ln3_loss(
        xh2, rstd2, g2, b2, ps, w_ple_s, w_gate_f, g3, b3, tgt, tm)
    def halves_of(g):
        return g.reshape(N_CHIPS, 2, g.shape[1] // 2, g.shape[2])

    ple_halves = [halves_of(dw_ple), halves_of(dw_gate.reshape(N_CHIPS, D_MODEL // N_CHIPS, D_MODEL))]
    dx1m, da, *ple_pair = _mlp_bwd(rb, dhb, w_up_s, w_down_f, tm_mlp, _pair_swap_rider(ple_halves))
    (dw_up,) = _tn_matmul(x1b, da, "grad_w_up", 1024, min(1024, S), True, _no_rider())
    up_halves = halves_of(dw_up)
    dw_down, up_pair = _tn_matmul(rb, dhb, "grad_w_down", 1024, min(1024, S), False,
                                  _pair_swap_rider([up_halves]), square_a=True)
    down_halves = halves_of(dw_down.reshape(N_CHIPS, D_FF // N_CHIPS, D_MODEL))
    dpre1, don, dpooled, dw_out, dg1, db1, down_pair = _mix_bwd(
        dpre2, dx1m, xh1, rstd1, g1, w_out_f, on, pooled, tm, _pair_swap_rider([down_halves]))
    early_sum, _ = _add_pair([up_halves, down_halves] + ple_halves, [up_pair, down_pair] + ple_pair, place,
                             "pair_sum_mlp_ple", _no_rider())
    out_halves = halves_of(dw_out.reshape(N_CHIPS, D_MODEL // N_CHIPS, D_MODEL))
    du, dwp, dsc, out_pair = _pool_bwd(dpooled, d_b, wp, pool_scale, tm_pool, _pair_swap_rider([out_halves]))
    (out_sum,), _ = _add_pair([out_halves], [out_pair], place, "pair_sum_w_out", _no_rider())
    pack_a = _pack([jnp.broadcast_to(loss_row, (8, LANES)), dwp, dsc, dg1, db1, dg2, db2, dg3, db3])
    early_sum = [out_sum] + early_sum
    riding = _chip_scatter_rider([b for _, b in early_sum]) + _pack_exchange_rider(pack_a)
    dq, dk, dv, dga, *arrived = _attn_bwd(q, k, v, don, o_raw, attn_out_g, tq, riding)
    early_chips, recv_a = arrived[:-1], arrived[-1]
    grad_x, dw_in, dg0, db0 = _inproj_bwd(dq, dk, dv, du, dpre1, xh0, rstd0, g0, b0, w_in_s, tm)

    in_halves = halves_of(dw_in)
    pack_b = _pack([dg0, db0, dga])
    early_mine, (in_pair, recv_b) = _add_chips(
        [s for s, _ in early_sum], early_chips, place, "chip_sum_early",
        _pair_swap_rider([in_halves]) + _pack_exchange_rider(pack_b))
    (in_sum,), early_theirs = _add_pair([in_halves], [in_pair], place, "pair_sum_w_in", _pair_send_rider(early_mine))
    ms = [m_w_in, m_w_out, m_w_up, m_w_down, m_w_ple, m_w_ple_gate]
    vs = [v_w_in, v_w_out, v_w_up, v_w_down, v_w_ple, v_w_ple_gate]
    early_res, _ = _adamw(big[1:], early_mine, early_theirs, [m[0] for m in ms[1:]], [v[0] for v in vs[1:]],
                          place, "adamw_early", _no_rider())
    (in_chips,) = _chip_scatter_rider([in_sum[1]]).run("reduce_chips_late")
    (in_mine,), _ = _add_chips([in_sum[0]], [in_chips], place, "chip_sum_w_in", _no_rider())
    (in_theirs,) = _pair_send_rider([in_mine]).run("gather_pair_w_in")
    in_res, _ = _adamw(big[:1], [in_mine], [in_theirs], [ms[0][0]], [vs[0][0]], place, "adamw_w_in", _no_rider())
    big_out = {n: tuple(r.reshape(m.shape) for r in res4) for n, res4, m in zip(names, in_res + early_res, ms)}

    small_names = ["w_pool", "pool_scale", "ln1_g", "ln1_b", "ln2_g", "ln2_b", "ln3_g", "ln3_b",
                   "emb_ln_g", "emb_ln_b", "attn_out_g"]
    small_w = [w_pool, pool_scale, ln1_g, ln1_b, ln2_g, ln2_b, ln3_g, ln3_b, emb_ln_g, emb_ln_b, attn_out_g]
    small_m = [m_w_pool, m_pool_scale, m_ln1_g, m_ln1_b, m_ln2_g, m_ln2_b, m_ln3_g, m_ln3_b,
               m_emb_ln_g, m_emb_ln_b, m_attn_out_g]
    small_v = [v_w_pool, v_pool_scale, v_ln1_g, v_ln1_b, v_ln2_g, v_ln2_b, v_ln3_g, v_ln3_b,
               v_emb_ln_g, v_emb_ln_b, v_attn_out_g]
    loss_like = jnp.zeros((8, LANES), F32)
    gs, ds, mos, vos = _small_sum_adamw(recv_a, recv_b, _pack([loss_like] + small_w), _pack([loss_like] + small_m),
                                        _pack([jnp.ones((8, LANES), F32)] + small_v))
    like = [loss_like] + small_w
    gs_u, ds_u, mos_u, vos_u = (_unpack(a, like) for a in (gs, ds, mos, vos))
    loss = gs_u[0][0, 0]
    small_out = {n: (gs_u[i + 1], ds_u[i + 1], mos_u[i + 1], vos_u[i + 1]) for i, n in enumerate(small_names)}

    order = ["emb_ln_g", "emb_ln_b", "w_in", "attn_out_g", "w_pool", "pool_scale", "w_out", "ln1_g", "ln1_b",
             "w_up", "w_down", "ln2_g", "ln2_b", "w_ple", "w_ple_gate", "ln3_g", "ln3_b"]
    res = {**big_out, **small_out}
    outs = [loss, grad_x.reshape(x.shape)]
    for kind in range(4):
        outs += [res[n][kind] for n in order]
    return tuple(outs)
```

```python
import functools

import jax
import jax.numpy as jnp
from jax import lax
from jax.experimental import pallas as pl
from jax.experimental.pallas import tpu as pltpu

F32 = jnp.float32
BF16 = jnp.bfloat16

D_MODEL = 1024
ATTN_WIDTH = 512
POOL_WIDTH = 512
HEAD_DIM = 64
PAIR = 2 * HEAD_DIM
N_PAIRS = ATTN_WIDTH // PAIR
N_POOL_GROUPS = 4
POOL_GROUP = 128
POOL_HALO = 16
D_FF = 4096
PLE_DIM = 256
N_CHIPS = 4
N_DEV = 8
LN_EPS = 1e-5
RMS_EPS = 1e-6
ALPHA = float(2.0 ** 0.25)
Q_SCALE = 0.125
ADAM_LR = 0.001
ADAM_B1 = 0.9
ADAM_B2 = 0.999
ADAM_EPS = 1e-08
ADAM_WD = 0.01
ADAM_STEP = 10
LANES = 128
MIB = 1024 * 1024

MESH = pl.DeviceIdType.MESH
HBM_SPEC = pl.BlockSpec(memory_space=pltpu.HBM)
VMEM_SPEC = pl.BlockSpec(memory_space=pltpu.VMEM)


def _cp(vmem_mib):
    return pltpu.CompilerParams(vmem_limit_bytes=vmem_mib * MIB)


def _dot(a, b):
    return jnp.dot(a, b, preferred_element_type=F32)


def _dot_nt(a, b):
    return lax.dot_general(a, b, (((1,), (1,)), ((), ())), preferred_element_type=F32)


def _dot_tn(a, b):
    return lax.dot_general(a, b, (((0,), (0,)), ((), ())), preferred_element_type=F32)


def _ln_fwd(pre):
    mu = jnp.mean(pre, axis=-1, keepdims=True)
    xc = pre - mu
    var = jnp.mean(xc * xc, axis=-1, keepdims=True)
    rstd = lax.rsqrt(var + LN_EPS)
    return xc * rstd, rstd


def _ln_bwd(dy, xh, rstd, g):
    dxh = dy * g
    m1 = jnp.mean(dxh, axis=-1, keepdims=True)
    m2 = jnp.mean(dxh * xh, axis=-1, keepdims=True)
    return rstd * (dxh - m1 - xh * m2)


def _colsum(a):
    return jnp.sum(a, axis=0, keepdims=True)


def _neg_softplus(z):
    return -(jnp.maximum(z, 0.0) + jnp.log(1.0 + jnp.exp(-jnp.abs(z))))


def _row_spec(tm, n):
    return pl.BlockSpec((tm, n), lambda i: (i, 0))


def _const_spec(shape):
    nd = len(shape)
    return pl.BlockSpec(shape, lambda *_: (0,) * nd)


def _hbm(*arrays):
    return [pltpu.with_memory_space_constraint(a, pltpu.HBM) for a in arrays]


def _sds(shape, dtype):
    return pltpu.HBM(shape, dtype)


def _embln_inproj(x, g0, b0, w_in_s, tm, rider):
    S, D = x.shape
    n_t = S // tm

    def body(*refs):
        ((x_ref, g_ref, b_ref, w_ref), (xh_ref, rstd_ref, q_ref, k_ref, v_ref, u_ref), _,
         ride) = rider.split(refs, 4, 6, 0)
        i = pl.program_id(0)

        @pl.when(i == 0)
        def _():
            rider.first(ride)

        @pl.when(i == (3 * n_t) // 4)
        def _():
            rider.mid(ride)

        xh, rstd = _ln_fwd(x_ref[...])
        xh_ref[...] = xh
        rstd_ref[...] = rstd
        xb = (xh * g_ref[...] + b_ref[...]).astype(BF16)
        q_ref[...] = (_dot(xb, w_ref[0]) * Q_SCALE).astype(BF16)
        k_ref[...] = _dot(xb, w_ref[1]).astype(BF16)
        v_ref[...] = _dot(xb, w_ref[2]).astype(BF16)
        u_ref[...] = _dot(xb, w_ref[3])

        @pl.when(i == n_t - 1)
        def _():
            rider.last(ride)

    return rider.call(
        body, [x, g0, b0, w_in_s], grid=(n_t,), name="embln_inproj",
        in_specs=[_row_spec(tm, D), _const_spec((1, D)), _const_spec((1, D)),
                  _const_spec((N_CHIPS, D, 512))],
        out_specs=[_row_spec(tm, D), _row_spec(tm, 1), _row_spec(tm, 512), _row_spec(tm, 512),
                   _row_spec(tm, 512), _row_spec(tm, 512)],
        out_shape=[_sds((S, D), F32), _sds((S, 1), F32), _sds((S, 512), BF16), _sds((S, 512), BF16),
                   _sds((S, 512), BF16), _sds((S, 512), F32)],
        scratch_shapes=[], vmem_mib=40)


def _tri(n, upper):
    r = lax.broadcasted_iota(jnp.int32, (n, n), 0)
    c = lax.broadcasted_iota(jnp.int32, (n, n), 1)
    keep = (r < c) if upper else (r > c)
    return jnp.where(keep, 1.0, 0.0).astype(BF16)


def _strictly_causal(n):
    return lax.broadcasted_iota(jnp.int32, (n, n), 1) < lax.broadcasted_iota(jnp.int32, (n, n), 0)


LOG_WEIGHT_FLOOR = -110.0


def _weights_left(c_ls):
    return (jnp.max(jnp.maximum(c_ls[0], c_ls[1])) > LOG_WEIGHT_FLOOR).astype(jnp.int32)


def _sb_tile(qhs, kt, low, c_ls, valid):
    valids = valid if isinstance(valid, (list, tuple)) else [valid] * len(qhs)
    zs = [_dot_nt(qh, kt) for qh in qhs]
    lrs = [_neg_softplus(z) for z in zs]
    ls_ = [lr if m is None else jnp.where(m, lr, 0.0) for lr, m in zip(lrs, valids)]
    sfx = [_dot(l.astype(BF16), low) + c_l for l, c_l in zip(ls_, c_ls)]
    lss = [z + lr for z, lr in zip(zs, lrs)]
    ws = [jnp.exp(ls + s) for ls, s in zip(lss, sfx)]
    ws = [w if m is None else jnp.where(m, w, 0.0) for w, m in zip(ws, valids)]
    return lss, ls_, ws


def _attn_fwd(q, k, v, ga, tq, rider):
    S = q.shape[0]
    nq = S // tq

    def body(*refs):
        (q_ref, k_ref, v_ref, ga_ref), (o_ref, on_ref), (low_s,), ride = rider.split(refs, 4, 2, 1)
        p, i = pl.program_id(0), pl.program_id(1)

        @pl.when(jnp.logical_and(p == 0, i == 0))
        def _():
            rider.first(ride)
            low_s[...] = _tri(tq, upper=False)

        @pl.when(jnp.logical_and(p == N_PAIRS - 1, i == 0))
        def _():
            rider.mid(ride)

        lane = lax.broadcasted_iota(jnp.int32, (1, PAIR), 1)
        m0 = lane < HEAD_DIM
        low = low_s[...]
        q2 = q_ref[...]
        qhs = [jnp.where(m0, q2, jnp.zeros_like(q2)), jnp.where(m0, jnp.zeros_like(q2), q2)]

        def tile(kb, c_ls, accs, valid):
            ks = pl.multiple_of(kb * tq, tq)
            kt = k_ref[pl.ds(ks, tq), :]
            vt = v_ref[pl.ds(ks, tq), :]
            _, ls_, ws = _sb_tile(qhs, kt, low, c_ls, valid)
            new_a = [acc + _dot(w.astype(BF16), vt) for acc, w in zip(accs, ws)]
            new_c = [c_l + jnp.sum(l, axis=1, keepdims=True) for c_l, l in zip(c_ls, ls_)]
            return new_c, new_a

        zc, za = jnp.zeros((tq, 1), F32), jnp.zeros((tq, PAIR), F32)

        def first_two():
            c_ls, accs = tile(i, [zc, zc], [za, za], _strictly_causal(tq))
            c_ls, accs = tile(i - 1, c_ls, accs, None)
            return (_weights_left(c_ls), *c_ls, *accs)

        def first_one():
            c_ls, accs = tile(i, [zc, zc], [za, za], _strictly_causal(tq))
            return (jnp.int32(0), *c_ls, *accs)

        st0 = lax.cond(i >= 1, first_two, first_one)

        def more(st):
            return jnp.logical_and(st[0] <= i, st[1] > 0)

        def step(st):
            n, _, c0, c1, a0, a1 = st
            c_ls, accs = tile(i - n, [c0, c1], [a0, a1], None)
            return (n + 1, _weights_left(c_ls), c_ls[0], c_ls[1], accs[0], accs[1])

        st = lax.while_loop(more, step, (jnp.int32(2), *st0))
        o = jnp.where(m0, st[4], st[5])
        o_ref[...] = o
        sq = o * o
        ms0 = jnp.sum(jnp.where(m0, sq, 0.0), axis=-1, keepdims=True) * (1.0 / HEAD_DIM)
        ms1 = jnp.sum(jnp.where(m0, 0.0, sq), axis=-1, keepdims=True) * (1.0 / HEAD_DIM)
        rs = jnp.where(m0, lax.rsqrt(ms0 + RMS_EPS), lax.rsqrt(ms1 + RMS_EPS))
        on_ref[...] = (o * rs * ga_ref[...]).astype(BF16)

        @pl.when(jnp.logical_and(p == N_PAIRS - 1, i == nq - 1))
        def _():
            rider.last(ride)

    return rider.call(
        body, [q, k, v, ga], grid=(N_PAIRS, nq), name="attn_fwd",
        in_specs=[pl.BlockSpec((tq, PAIR), lambda p, i: (i, p)),
                  pl.BlockSpec((S, PAIR), lambda p, i: (0, p)),
                  pl.BlockSpec((S, PAIR), lambda p, i: (0, p)),
                  pl.BlockSpec((1, PAIR), lambda p, i: (0, p))],
        out_specs=[pl.BlockSpec((tq, PAIR), lambda p, i: (i, p)),
                   pl.BlockSpec((tq, PAIR), lambda p, i: (i, p))],
        out_shape=[_sds((S, ATTN_WIDTH), F32), _sds((S, ATTN_WIDTH), BF16)],
        scratch_shapes=[pltpu.VMEM((tq, tq), BF16)], vmem_mib=40)


def _pool_fwd(u, w_pool, pscale, tm):
    S = u.shape[0]
    hb = tm // POOL_HALO

    def body(u_ref, uh_ref, wp_ref, sc_ref, d_ref, pooled_ref):
        i = pl.program_id(0)
        halo = jnp.where(i > 0, uh_ref[...], 0.0)
        pos = i * tm + lax.broadcasted_iota(jnp.int32, (tm, 1), 0)
        for g in range(N_POOL_GROUPS):
            win = 2 ** (g + 1)
            cols = slice(g * POOL_GROUP, (g + 1) * POOL_GROUP)
            ut = u_ref[:, cols]
            s = jnp.concatenate([halo[:, cols], ut], axis=0)
            for sh in (1, 2, 4, 8)[:g + 1]:
                s = s + pltpu.roll(s, sh, 0)
            cnt = jnp.minimum(pos + 1, win).astype(F32)
            db = (s[POOL_HALO:, :] / cnt - ut).astype(BF16)
            y = _dot(db, wp_ref[g].astype(BF16))
            d_ref[:, cols] = db
            pooled_ref[:, cols] = (y * sc_ref[:, cols]).astype(BF16)

    return pl.pallas_call(
        body, grid=(S // tm,), name="pool_fwd",
        in_specs=[_row_spec(tm, POOL_WIDTH),
                  pl.BlockSpec((POOL_HALO, POOL_WIDTH), lambda i: (jnp.maximum(i * hb - 1, 0), 0)),
                  _const_spec((N_POOL_GROUPS, POOL_GROUP, POOL_GROUP)), _const_spec((1, POOL_WIDTH))],
        out_specs=[_row_spec(tm, POOL_WIDTH), _row_spec(tm, POOL_WIDTH)],
        out_shape=[_sds((S, POOL_WIDTH), BF16), _sds((S, POOL_WIDTH), BF16)],
        compiler_params=_cp(32),
    )(*_hbm(u, u, w_pool, pscale))


def _mix_ln1(on, pooled, xh0, g0, b0, w_out, g1, b1, tm, rider):
    S, D = xh0.shape
    n_t = S // tm

    def body(*refs):
        ((on_ref, po_ref, xh0_ref, g0_ref, b0_ref, w_ref, g1_ref, b1_ref), (xh_ref, rstd_ref, xb_ref), _,
         ride) = rider.split(refs, 8, 3, 0)

        @pl.when(pl.program_id(0) == 0)
        def _():
            rider.first(ride)

        mixed = _dot(on_ref[...], w_ref[:ATTN_WIDTH, :]) + _dot(po_ref[...], w_ref[ATTN_WIDTH:, :])
        x0 = xh0_ref[...] * g0_ref[...] + b0_ref[...]
        xh, rstd = _ln_fwd(ALPHA * x0 + mixed)
        xh_ref[...] = xh
        rstd_ref[...] = rstd
        xb_ref[...] = (xh * g1_ref[...] + b1_ref[...]).astype(BF16)

        @pl.when(pl.program_id(0) == n_t - 1)
        def _():
            rider.last(ride)

    return rider.call(
        body, [on, pooled, xh0, g0, b0, w_out, g1, b1], grid=(n_t,), name="mix_ln1",
        in_specs=[_row_spec(tm, ATTN_WIDTH), _row_spec(tm, POOL_WIDTH), _row_spec(tm, D),
                  _const_spec((1, D)), _const_spec((1, D)), _const_spec((D, D)),
                  _const_spec((1, D)), _const_spec((1, D))],
        out_specs=[_row_spec(tm, D), _row_spec(tm, 1), _row_spec(tm, D)],
        out_shape=[_sds((S, D), F32), _sds((S, 1), F32), _sds((S, D), BF16)],
        scratch_shapes=[], vmem_mib=40)


def _mlp_ln2(xh1, x1b, g1, b1, w_up_s, w_down, tm):
    S, D = xh1.shape
    fc = D_FF // N_CHIPS

    def body(xh_ref, xb_ref, g_ref, b_ref, wu_ref, wd_ref, xh2_ref, rstd_ref, r_ref, acc_ref):
        j = pl.program_id(1)

        @pl.when(j == 0)
        def _():
            acc_ref[...] = jnp.zeros_like(acc_ref)

        r = jnp.maximum(_dot(xb_ref[...], wu_ref[0]), 0.0)
        r_ref[...] = r.astype(BF16)
        acc_ref[...] += _dot((r * r).astype(BF16), wd_ref[...])

        @pl.when(j == N_CHIPS - 1)
        def _():
            x1 = xh_ref[...] * g_ref[...] + b_ref[...]
            xh, rstd = _ln_fwd(ALPHA * x1 + acc_ref[...])
            xh2_ref[...] = xh
            rstd_ref[...] = rstd

    return pl.pallas_call(
        body, grid=(S // tm, N_CHIPS), name="mlp_ln2",
        in_specs=[pl.BlockSpec((tm, D), lambda i, j: (i, 0)), pl.BlockSpec((tm, D), lambda i, j: (i, 0)),
                  pl.BlockSpec((1, D), lambda i, j: (0, 0)), pl.BlockSpec((1, D), lambda i, j: (0, 0)),
                  pl.BlockSpec((1, D, fc), lambda i, j: (j, 0, 0)),
                  pl.BlockSpec((fc, D), lambda i, j: (j, 0))],
        out_specs=[pl.BlockSpec((tm, D), lambda i, j: (i, 0)), pl.BlockSpec((tm, 1), lambda i, j: (i, 0)),
                   pl.BlockSpec((tm, fc), lambda i, j: (i, j))],
        out_shape=[_sds((S, D), F32), _sds((S, 1), F32), _sds((S, D_FF), BF16)],
        scratch_shapes=[pltpu.VMEM((tm, D), F32)],
        compiler_params=_cp(56),
    )(*_hbm(xh1, x1b, g1, b1, w_up_s, w_down))


def _ple_ln3_loss(xh2, rstd2, g2, b2, p, w_ple_s, w_gate, g3, b3, target, tm):
    S, D = xh2.shape
    pc = D // N_CHIPS

    def body(xh2_ref, rstd2_ref, g2_ref, b2_ref, p_ref, wp_ref, wg_ref, g3_ref, b3_ref, t_ref,
             dpre2_ref, dhb_ref, dwp_ref, dwg_ref, dg3_ref, db3_ref, dg2_ref, db2_ref, loss_ref):
        i = pl.program_id(0)

        @pl.when(i == 0)
        def _():
            for r in (dwp_ref, dwg_ref, dg3_ref, db3_ref, dg2_ref, db2_ref, loss_ref):
                r[...] = jnp.zeros_like(r)

        xh2 = xh2_ref[...]
        x2 = xh2 * g2_ref[...] + b2_ref[...]
        x2b = x2.astype(BF16)
        gate = 1.0 / (1.0 + jnp.exp(-_dot(x2b, wg_ref[...])))
        pb = p_ref[...].astype(BF16)
        pe = jnp.concatenate([_dot(pb, wp_ref[c]) for c in range(N_CHIPS)], axis=1)
        xh3, rstd3 = _ln_fwd(ALPHA * x2 + pe * gate)
        diff = xh3 * g3_ref[...] + b3_ref[...] - t_ref[...]
        loss_ref[...] += (0.5 / D) * jnp.sum(diff * diff)
        dy = diff * (1.0 / D)
        dg3_ref[...] += _colsum(dy * xh3)
        db3_ref[...] += _colsum(dy)
        dpre3 = _ln_bwd(dy, xh3, rstd3, g3_ref[...])
        dpe_b = (dpre3 * gate).astype(BF16)
        dgp_b = (dpre3 * pe * gate * (1.0 - gate)).astype(BF16)
        dx2 = ALPHA * dpre3 + _dot_nt(dgp_b, wg_ref[...])
        dwg_ref[...] += _dot_tn(x2b, dgp_b)
        for c in range(N_CHIPS):
            dwp_ref[c] += _dot_tn(pb, dpe_b[:, c * pc:(c + 1) * pc])
        dg2_ref[...] += _colsum(dx2 * xh2)
        db2_ref[...] += _colsum(dx2)
        dpre2 = _ln_bwd(dx2, xh2, rstd2_ref[...], g2_ref[...])
        dpre2_ref[...] = dpre2
        dhb_ref[...] = dpre2.astype(BF16)

    vec = _const_spec((1, D))
    return pl.pallas_call(
        body, grid=(S // tm,), name="ple_ln3_loss",
        in_specs=[_row_spec(tm, D), _row_spec(tm, 1), vec, vec, _row_spec(tm, PLE_DIM),
                  _const_spec((N_CHIPS, PLE_DIM, pc)), _const_spec((D, D)), vec, vec, _row_spec(tm, D)],
        out_specs=[_row_spec(tm, D), _row_spec(tm, D), _const_spec((N_CHIPS, PLE_DIM, pc)),
                   _const_spec((D, D)), vec, vec, vec, vec, _const_spec((1, LANES))],
        out_shape=[_sds((S, D), F32), _sds((S, D), BF16), _sds((N_CHIPS, PLE_DIM, pc), F32),
                   _sds((D, D), F32), _sds((1, D), F32), _sds((1, D), F32), _sds((1, D), F32),
                   _sds((1, D), F32), _sds((1, LANES), F32)],
        compiler_params=_cp(58),
    )(*_hbm(xh2, rstd2, g2, b2, p, w_ple_s, w_gate, g3, b3, target))


def _mlp_bwd(rb, dhb, w_up_s, w_down, tm, rider):
    S, D = dhb.shape
    fc = D_FF // N_CHIPS
    n_t = S // tm

    def body(*refs):
        (r_ref, dh_ref, wu_ref, wd_ref), (dx_ref, da_ref), _, ride = rider.split(refs, 4, 2, 0)
        i, j = pl.program_id(0), pl.program_id(1)

        @pl.when(jnp.logical_and(i == 0, j == 0))
        def _():
            rider.first(ride)

        @pl.when(j == 0)
        def _():
            dx_ref[...] = jnp.zeros_like(dx_ref)

        da = (_dot_nt(dh_ref[...], wd_ref[...]) * (2.0 * r_ref[...].astype(F32))).astype(BF16)
        da_ref[...] = da
        dx_ref[...] += _dot_nt(da, wu_ref[0])

        @pl.when(jnp.logical_and(i == n_t - 1, j == N_CHIPS - 1))
        def _():
            rider.last(ride)

    return rider.call(
        body, [rb, dhb, w_up_s, w_down], grid=(n_t, N_CHIPS), name="mlp_bwd",
        in_specs=[pl.BlockSpec((tm, fc), lambda i, j: (i, j)), pl.BlockSpec((tm, D), lambda i, j: (i, 0)),
                  pl.BlockSpec((1, D, fc), lambda i, j: (j, 0, 0)),
                  pl.BlockSpec((fc, D), lambda i, j: (j, 0))],
        out_specs=[pl.BlockSpec((tm, D), lambda i, j: (i, 0)), pl.BlockSpec((tm, fc), lambda i, j: (i, j))],
        out_shape=[_sds((S, D), F32), _sds((S, D_FF), BF16)],
        scratch_shapes=[], vmem_mib=56)


def _tn_matmul(a, b, name, tk, tt, stacked, rider, square_a=False):
    T, K = a.shape
    N = b.shape[1]
    tn = 1024
    grid = (K // tk, N // tn, T // tt)

    def body(*refs):
        (a_ref, b_ref), (o_ref,), _, ride = rider.split(refs, 2, 1, 0)
        at = [pl.program_id(d) for d in range(3)]

        @pl.when(jnp.logical_and(jnp.logical_and(at[0] == 0, at[1] == 0), at[2] == 0))
        def _():
            rider.first(ride)

        @pl.when(at[2] == 0)
        def _():
            o_ref[...] = jnp.zeros_like(o_ref)

        a_t = a_ref[...]
        if square_a:
            a_t = a_t * a_t
        prod = _dot_tn(a_t, b_ref[...])
        if stacked:
            o_ref[0] += prod
        else:
            o_ref[...] += prod

        @pl.when(jnp.logical_and(jnp.logical_and(at[0] == grid[0] - 1, at[1] == grid[1] - 1),
                                 at[2] == grid[2] - 1))
        def _():
            rider.last(ride)

    if stacked:
        out_spec = pl.BlockSpec((1, tk, tn), lambda k, n, t: (n, k, 0))
        out_shape = _sds((N // tn, K, tn), F32)
    else:
        out_spec = pl.BlockSpec((tk, tn), lambda k, n, t: (k, n))
        out_shape = _sds((K, N), F32)
    return rider.call(
        body, [a, b], grid=grid, name=name,
        in_specs=[pl.BlockSpec((tt, tk), lambda k, n, t: (t, k)),
                  pl.BlockSpec((tt, tn), lambda k, n, t: (t, n))],
        out_specs=[out_spec], out_shape=[out_shape], scratch_shapes=[], vmem_mib=48)


def _mix_bwd(dpre2, dx1m, xh1, rstd1, g1, w_out, on, pooled, tm, rider):
    S, D = xh1.shape
    n_t = S // tm

    def body(*refs):
        ((dp2_ref, dxm_ref, xh_ref, rstd_ref, g_ref, w_ref, on_ref, po_ref),
         (dpre1_ref, don_ref, dpo_ref, dw_ref, dg_ref, db_ref), _, ride) = rider.split(refs, 8, 6, 0)

        @pl.when(pl.program_id(0) == 0)
        def _():
            rider.first(ride)
            for r in (dw_ref, dg_ref, db_ref):
                r[...] = jnp.zeros_like(r)

        xh = xh_ref[...]
        dx1 = ALPHA * dp2_ref[...] + dxm_ref[...]
        dg_ref[...] += _colsum(dx1 * xh)
        db_ref[...] += _colsum(dx1)
        dpre1 = _ln_bwd(dx1, xh, rstd_ref[...], g_ref[...])
        dpre1_ref[...] = dpre1
        dmb = dpre1.astype(BF16)
        dcat = _dot_nt(dmb, w_ref[...])
        don_ref[...] = dcat[:, :ATTN_WIDTH]
        dpo_ref[...] = dcat[:, ATTN_WIDTH:]
        dw_ref[:ATTN_WIDTH, :] += _dot_tn(on_ref[...], dmb)
        dw_ref[ATTN_WIDTH:, :] += _dot_tn(po_ref[...], dmb)

        @pl.when(pl.program_id(0) == n_t - 1)
        def _():
            rider.last(ride)

    vec = _const_spec((1, D))
    return rider.call(
        body, [dpre2, dx1m, xh1, rstd1, g1, w_out, on, pooled], grid=(n_t,), name="mix_bwd",
        in_specs=[_row_spec(tm, D), _row_spec(tm, D), _row_spec(tm, D), _row_spec(tm, 1), vec,
                  _const_spec((D, D)), _row_spec(tm, ATTN_WIDTH), _row_spec(tm, POOL_WIDTH)],
        out_specs=[_row_spec(tm, D), _row_spec(tm, ATTN_WIDTH), _row_spec(tm, POOL_WIDTH),
                   _const_spec((D, D)), vec, vec],
        out_shape=[_sds((S, D), F32), _sds((S, ATTN_WIDTH), F32), _sds((S, POOL_WIDTH), F32),
                   _sds((D, D), F32), _sds((1, D), F32), _sds((1, D), F32)],
        scratch_shapes=[], vmem_mib=56)


def _pool_bwd(dpooled, d_b, w_pool, pscale, tm, rider):
    S = dpooled.shape[0]
    hb = tm // POOL_HALO
    n_t = S // tm
    te = tm + POOL_HALO

    def body(*refs):
        ((dp_ref, dph_ref, d_ref, wp_ref, sc_ref), (du_ref, dwp_ref, dsc_ref), _,
         ride) = rider.split(refs, 5, 3, 0)
        i = pl.program_id(0)

        @pl.when(i == 0)
        def _():
            rider.first(ride)
            dwp_ref[...] = jnp.zeros_like(dwp_ref)
            dsc_ref[...] = jnp.zeros_like(dsc_ref)

        halo = jnp.where(i < n_t - 1, dph_ref[...], 0.0)
        pos = i * tm + lax.broadcasted_iota(jnp.int32, (te, 1), 0)
        for g in range(N_POOL_GROUPS):
            win = 2 ** (g + 1)
            cols = slice(g * POOL_GROUP, (g + 1) * POOL_GROUP)
            wpb = wp_ref[g].astype(BF16)
            dpt = dp_ref[:, cols]
            dpe = jnp.concatenate([dpt, halo[:, cols]], axis=0)
            dyb = (dpe * sc_ref[:, cols]).astype(BF16)
            dd = _dot_nt(dyb, wpb)
            s = dd / jnp.minimum(pos + 1, win).astype(F32)
            for sh in (1, 2, 4, 8)[:g + 1]:
                s = s + pltpu.roll(s, te - sh, 0)
            du_ref[:, cols] = s[:tm, :] - dd[:tm, :]
            db = d_ref[:, cols]
            dwp_ref[g] += _dot_tn(db, dyb[:tm, :])
            dsc_ref[:, cols] += _colsum(dpt * _dot(db, wpb))

        @pl.when(i == n_t - 1)
        def _():
            rider.last(ride)

    return rider.call(
        body, [dpooled, dpooled, d_b, w_pool, pscale], grid=(n_t,), name="pool_bwd",
        in_specs=[_row_spec(tm, POOL_WIDTH),
                  pl.BlockSpec((POOL_HALO, POOL_WIDTH),
                               lambda i: (jnp.minimum((i + 1) * hb, S // POOL_HALO - 1), 0)),
                  _row_spec(tm, POOL_WIDTH),
                  _const_spec((N_POOL_GROUPS, POOL_GROUP, POOL_GROUP)), _const_spec((1, POOL_WIDTH))],
        out_specs=[_row_spec(tm, POOL_WIDTH), _const_spec((N_POOL_GROUPS, POOL_GROUP, POOL_GROUP)),
                   _const_spec((1, POOL_WIDTH))],
        out_shape=[_sds((S, POOL_WIDTH), F32), _sds((N_POOL_GROUPS, POOL_GROUP, POOL_GROUP), F32),
                   _sds((1, POOL_WIDTH), F32)],
        scratch_shapes=[], vmem_mib=32)


def _attn_bwd(q, k, v, don, o_raw, ga, tq, rider):
    S = q.shape[0]
    nq = S // tq

    def body(*refs):
        ((q_ref, k_ref, v_ref, don_ref, o_ref, ga_ref), (dq_ref, dk_ref, dv_ref, dga_ref),
         (g_s, b_s, low_s, upp_s), ride) = rider.split(refs, 6, 4, 4)
        p, i = pl.program_id(0), pl.program_id(1)

        @pl.when(jnp.logical_and(p == 0, i == 0))
        def _():
            rider.first(ride)
            low_s[...] = _tri(tq, upper=False)
            upp_s[...] = _tri(tq, upper=True)

        @pl.when(i == 0)
        def _():
            for r in (dk_ref, dv_ref, dga_ref):
                r[...] = jnp.zeros_like(r)

        lane = lax.broadcasted_iota(jnp.int32, (1, PAIR), 1)
        m0 = lane < HEAD_DIM
        low = low_s[...]
        upp = upp_s[...]

        def seg_mean(a):
            s0 = jnp.sum(jnp.where(m0, a, 0.0), axis=-1, keepdims=True)
            s1 = jnp.sum(jnp.where(m0, 0.0, a), axis=-1, keepdims=True)
            return jnp.where(m0, s0, s1) * (1.0 / HEAD_DIM)

        o = o_ref[...]
        rs = lax.rsqrt(seg_mean(o * o) + RMS_EPS)
        oh = o * rs
        don = don_ref[...]
        dga_ref[...] += _colsum(don * oh)
        doh = don * ga_ref[...]
        do = rs * (doh - oh * seg_mean(doh * oh))
        dob = do.astype(BF16)
        q2 = q_ref[...]
        qhs = [jnp.where(m0, q2, jnp.zeros_like(q2)), jnp.where(m0, jnp.zeros_like(q2), q2)]
        dhs = [jnp.where(m0, dob, jnp.zeros_like(dob)), jnp.where(m0, jnp.zeros_like(dob), dob)]
        causal = _strictly_causal(tq)

        def down(kb, c_ls, valid):
            ks = pl.multiple_of(kb * tq, tq)
            kt = k_ref[pl.ds(ks, tq), :]
            vt = v_ref[pl.ds(ks, tq), :]
            lss, ls_, ws = _sb_tile(qhs, kt, low, c_ls, valid)
            dws = [_dot_nt(dh, vt) for dh in dhs]
            for hh in range(2):
                g_s[hh, kb] = dws[hh] * ws[hh]
                b_s[hh, kb] = jnp.exp(lss[hh])
            dv_ref[pl.ds(ks, tq), :] += (_dot_tn(ws[0].astype(BF16), dhs[0])
                                         + _dot_tn(ws[1].astype(BF16), dhs[1]))
            return [c_l + jnp.sum(l, axis=1, keepdims=True) for c_l, l in zip(c_ls, ls_)]

        zc, za = jnp.zeros((tq, 1), F32), jnp.zeros((tq, PAIR), F32)
        def first_two():
            c_ls = down(i - 1, down(i, [zc, zc], causal), None)
            return (_weights_left(c_ls), *c_ls)

        st0 = lax.cond(i >= 1, first_two, lambda: (jnp.int32(0), *down(i, [zc, zc], causal)))

        def more(st):
            return jnp.logical_and(st[0] <= i, st[1] > 0)

        def down_step(st):
            c_ls = down(i - st[0], [st[2], st[3]], None)
            return (st[0] + 1, _weights_left(c_ls), c_ls[0], c_ls[1])

        n_tiles = lax.while_loop(more, down_step, (jnp.int32(2), *st0))[0]

        def up(kb, c_gs, accs, valid):
            ks = pl.multiple_of(kb * tq, tq)
            kt = k_ref[pl.ds(ks, tq), :]
            gs = [g_s[hh, kb] for hh in range(2)]
            pres = [_dot(g.astype(BF16), upp) + c_g for g, c_g in zip(gs, c_gs)]
            dzs = []
            for hh in range(2):
                beta = b_s[hh, kb]
                dz = gs[hh] - beta * (gs[hh] + pres[hh])
                if valid is not None:
                    dz = jnp.where(valid, dz, 0.0)
                dzs.append(dz.astype(BF16))
            new_a = [acc + _dot(dzb, kt) for acc, dzb in zip(accs, dzs)]
            dk_ref[pl.ds(ks, tq), :] += _dot_tn(dzs[0], qhs[0]) + _dot_tn(dzs[1], qhs[1])
            new_c = [c_g + jnp.sum(g, axis=1, keepdims=True) for c_g, g in zip(c_gs, gs)]
            return new_c, new_a

        def up_step(kb, st):
            c_gs, accs = up(kb, [st[0], st[1]], [st[2], st[3]], None)
            return (c_gs[0], c_gs[1], accs[0], accs[1])

        st = lax.fori_loop(i - n_tiles + 1, i - 1, up_step, (zc, zc, za, za))

        def last_two():
            c_gs, accs = up(i - 1, [st[0], st[1]], [st[2], st[3]], None)
            return tuple(up(i, c_gs, accs, causal)[1])

        accs = lax.cond(i >= 1, last_two, lambda: tuple(up(i, [zc, zc], [za, za], causal)[1]))
        dq_ref[...] = jnp.where(m0, accs[0], accs[1]) * Q_SCALE

        @pl.when(jnp.logical_and(p == N_PAIRS - 1, i == nq - 1))
        def _():
            rider.last(ride)

    return rider.call(
        body, [q, k, v, don, o_raw, ga], grid=(N_PAIRS, nq), name="attn_bwd",
        in_specs=[pl.BlockSpec((tq, PAIR), lambda p, i: (i, p)),
                  pl.BlockSpec((S, PAIR), lambda p, i: (0, p)),
                  pl.BlockSpec((S, PAIR), lambda p, i: (0, p)),
                  pl.BlockSpec((tq, PAIR), lambda p, i: (i, p)),
                  pl.BlockSpec((tq, PAIR), lambda p, i: (i, p)),
                  pl.BlockSpec((1, PAIR), lambda p, i: (0, p))],
        out_specs=[pl.BlockSpec((tq, PAIR), lambda p, i: (i, p)),
                   pl.BlockSpec((S, PAIR), lambda p, i: (0, p)),
                   pl.BlockSpec((S, PAIR), lambda p, i: (0, p)),
                   pl.BlockSpec((1, PAIR), lambda p, i: (0, p))],
        out_shape=[_sds((S, ATTN_WIDTH), F32), _sds((S, ATTN_WIDTH), F32), _sds((S, ATTN_WIDTH), F32),
                   _sds((1, ATTN_WIDTH), F32)],
        scratch_shapes=[pltpu.VMEM((2, nq, tq, tq), F32), pltpu.VMEM((2, nq, tq, tq), F32),
                        pltpu.VMEM((tq, tq), BF16), pltpu.VMEM((tq, tq), BF16)],
        vmem_mib=56)


def _inproj_bwd(dq, dk, dv, du, dpre1, xh0, rstd0, g0, b0, w_in_s, tm):
    S, D = xh0.shape

    def body(dq_ref, dk_ref, dv_ref, du_ref, dp1_ref, xh_ref, rstd_ref, g_ref, b_ref, w_ref,
             gx_ref, dw_ref, dg_ref, db_ref):
        @pl.when(pl.program_id(0) == 0)
        def _():
            for r in (dw_ref, dg_ref, db_ref):
                r[...] = jnp.zeros_like(r)

        xh = xh_ref[...]
        xb = (xh * g_ref[...] + b_ref[...]).astype(BF16)
        dx0 = ALPHA * dp1_ref[...]
        for c, r in enumerate((dq_ref, dk_ref, dv_ref, du_ref)):
            dpb = r[...].astype(BF16)
            dx0 = dx0 + _dot_nt(dpb, w_ref[c])
            dw_ref[c] += _dot_tn(xb, dpb)
        dg_ref[...] += _colsum(dx0 * xh)
        db_ref[...] += _colsum(dx0)
        gx_ref[...] = _ln_bwd(dx0, xh, rstd_ref[...], g_ref[...])

    vec = _const_spec((1, D))
    half = _row_spec(tm, 512)
    return pl.pallas_call(
        body, grid=(S // tm,), name="inproj_bwd",
        in_specs=[half, half, half, half, _row_spec(tm, D), _row_spec(tm, D), _row_spec(tm, 1), vec, vec,
                  _const_spec((N_CHIPS, D, 512))],
        out_specs=[_row_spec(tm, D), _const_spec((N_CHIPS, D, 512)), vec, vec],
        out_shape=[_sds((S, D), F32), _sds((N_CHIPS, D, 512), F32), _sds((1, D), F32), _sds((1, D), F32)],
        compiler_params=_cp(56),
    )(*_hbm(dq, dk, dv, du, dpre1, xh0, rstd0, g0, b0, w_in_s))


def _place():
    return lax.axis_index("x"), lax.axis_index("y"), lax.axis_index("c")


CHIP_FLIPS = ((0, 1), (1, 0), (1, 1))


class _Rider:
    def __init__(self, ins, out_shapes, n_sem, phases, aliases=None):
        self.ins, self.out_shapes, self.n_sem, self.phases = list(ins), list(out_shapes), n_sem, phases
        self.aliases = aliases or {}

    def __add__(self, other):
        na, ma = len(self.ins), len(self.out_shapes)

        def phases(ins, outs, ssem, rsem):
            mine = self.phases(ins[:na], outs[:ma], ssem, rsem)
            rest = pl.ds(self.n_sem, other.n_sem)
            theirs = other.phases(ins[na:], outs[ma:], ssem.at[rest], rsem.at[rest])
            assert len(mine) == 1 and len(theirs) == 1
            return [mine[0] + theirs[0]]

        aliases = {**self.aliases, **{na + i: ma + o for i, o in other.aliases.items()}}
        return _Rider(self.ins + other.ins, self.out_shapes + other.out_shapes, self.n_sem + other.n_sem, phases,
                      aliases)

    def split(self, refs, n_in, n_out, n_scratch):
        a = n_in + len(self.ins)
        b = a + n_out
        c = b + len(self.out_shapes)
        own = (refs[:n_in], refs[a:b], refs[c:c + n_scratch])
        return own + ((refs[n_in:a], refs[b:c]) + tuple(refs[c + n_scratch:]),)

    def first(self, ride):
        for make in self.phases(*ride)[0]:
            make().start()

    def mid(self, ride):
        ph = self.phases(*ride)
        if len(ph) == 2:
            for make in ph[0]:
                make().wait_recv()
            for make in ph[1]:
                make().start()

    def last(self, ride):
        ph = self.phases(*ride)
        if len(ph) == 2:
            for make in ph[0]:
                make().wait_send()
        for make in ph[-1]:
            make().wait()

    def call(self, body, args, *, grid, name, in_specs, out_specs, out_shape, scratch_shapes, vmem_mib,
             prefetch=None):
        n_in, n_out = len(in_specs), len(out_specs)
        sems = [pltpu.SemaphoreType.DMA((self.n_sem,)), pltpu.SemaphoreType.DMA((self.n_sem,))]
        n_pre = 0 if prefetch is None else 1
        grid_spec = pltpu.PrefetchScalarGridSpec(
            num_scalar_prefetch=n_pre, grid=grid,
            in_specs=list(in_specs) + [HBM_SPEC] * len(self.ins),
            out_specs=list(out_specs) + [HBM_SPEC] * len(self.out_shapes),
            scratch_shapes=list(scratch_shapes) + sems)
        return pl.pallas_call(
            body, name=name, grid_spec=grid_spec,
            out_shape=list(out_shape) + self.out_shapes,
            input_output_aliases={n_pre + n_in + i: n_out + o for i, o in self.aliases.items()},
            compiler_params=_cp(vmem_mib),
        )(*([] if prefetch is None else [prefetch]), *_hbm(*args), *self.ins)

    def run(self, name):
        def body(*refs):
            ride = self.split(refs, 0, 0, 0)[3]
            self.first(ride)
            self.mid(ride)
            self.last(ride)

        return self.call(body, [], grid=(), name=name, in_specs=[], out_specs=[], out_shape=[],
                         scratch_shapes=[], vmem_mib=16)


def _remote(src, dst, ssem, rsem, n, dev):
    return functools.partial(pltpu.make_async_remote_copy, src_ref=src, dst_ref=dst, send_sem=ssem.at[n],
                             recv_sem=rsem.at[n], device_id=dev, device_id_type=MESH)


def _cast_into_slot(w, place, name):
    R, C = w.shape
    tr = min(R, 512)

    def body(pl_ref, w_ref, o_ref):
        o_ref[0] = w_ref[...].astype(BF16)

    return pl.pallas_call(
        body, name=name,
        grid_spec=pltpu.PrefetchScalarGridSpec(
            num_scalar_prefetch=1, grid=(R // tr,),
            in_specs=[pl.BlockSpec((tr, C), lambda r, pr: (r, 0))],
            out_specs=pl.BlockSpec((1, tr, C), lambda r, pr: (pr[1], r, 0))),
        out_shape=_sds((N_CHIPS, R, C), BF16),
    )(place, w)


CAST_STEPS = 8


def _cast_rest(ws, place, rider):
    n = len(ws)

    def body(pl_ref, *refs):
        w_refs, o_refs, _, ride = rider.split(refs, n, n, 0)
        r = pl.program_id(0)

        @pl.when(r == 0)
        def _():
            rider.first(ride)

        @pl.when(r == CAST_STEPS // 2)
        def _():
            rider.mid(ride)

        for w_ref, o_ref in zip(w_refs, o_refs):
            o_ref[0] = w_ref[...].astype(BF16)

        @pl.when(r == CAST_STEPS - 1)
        def _():
            rider.last(ride)

    def rows(w):
        return w.shape[0] // CAST_STEPS

    return rider.call(
        body, ws, grid=(CAST_STEPS,), name="cast_weights", prefetch=place,
        in_specs=[pl.BlockSpec((rows(w), w.shape[1]), lambda r, pr: (r, 0)) for w in ws],
        out_specs=[pl.BlockSpec((1, rows(w), w.shape[1]), lambda r, pr: (pr[1], r, 0)) for w in ws],
        out_shape=[_sds((N_CHIPS,) + w.shape, BF16) for w in ws], scratch_shapes=[], vmem_mib=32)


def _gather_rider(stacked, part="both"):
    n, nf = len(stacked), len(CHIP_FLIPS)

    def phases(ins, outs, ssem, rsem):
        x, y, c = _place()
        slot = 2 * x + y
        ici, d2d = [], []
        for w, (i_ref, o_ref) in enumerate(zip(ins, outs)):
            hh = o_ref.shape[1] // 2
            rows = pl.ds(c * hh, hh)
            for f, (fx, fy) in enumerate(CHIP_FLIPS):
                k = w * nf + f
                theirs = 2 * (x ^ fx) + (y ^ fy)
                if part != "pair":
                    ici.append(_remote(i_ref.at[slot, rows], o_ref.at[slot, rows], ssem, rsem, k,
                                       (x ^ fx, y ^ fy, c)))
                if part != "chips":
                    d2d.append(_remote(o_ref.at[theirs, rows], o_ref.at[theirs, rows], ssem, rsem,
                                       (n * nf if part == "both" else 0) + k, (x, y, 1 - c)))
        return [ph for ph in (ici, d2d) if ph]

    return _Rider(stacked, [_sds(s.shape, s.dtype) for s in stacked], (2 if part == "both" else 1) * n * nf,
                  phases, aliases={i: i for i in range(n)})


def _pair_swap_rider(grads):
    def phases(ins, outs, ssem, rsem):
        x, y, c = _place()
        return [[_remote(g.at[:, 1 - c], o, ssem, rsem, k, (x, y, 1 - c))
                 for k, (g, o) in enumerate(zip(ins, outs))]]

    return _Rider(grads, [_sds((N_CHIPS,) + g.shape[2:], g.dtype) for g in grads], len(grads), phases)


def _chip_scatter_rider(parts):
    nf = len(CHIP_FLIPS)

    def phases(ins, outs, ssem, rsem):
        x, y, c = _place()
        return [[_remote(r.at[2 * (x ^ fx) + (y ^ fy)], o.at[f], ssem, rsem, w * nf + f, (x ^ fx, y ^ fy, c))
                 for w, (r, o) in enumerate(zip(ins, outs)) for f, (fx, fy) in enumerate(CHIP_FLIPS)]]

    return _Rider(parts, [_sds((nf,) + r.shape[1:], r.dtype) for r in parts], len(parts) * nf, phases)


def _pair_send_rider(halves):
    def phases(ins, outs, ssem, rsem):
        x, y, c = _place()
        return [[_remote(h, o, ssem, rsem, k, (x, y, 1 - c)) for k, (h, o) in enumerate(zip(ins, outs))]]

    return _Rider(halves, [_sds(h.shape, h.dtype) for h in halves], len(halves), phases)


PAIR_SUM_STEPS = 2
CHIP_SUM_STEPS = 4
ADAMW_STEPS = 4


def _no_rider():
    return _Rider([], [], 1, lambda ins, outs, ssem, rsem: [[]])


def _add_pair(grads, recvs, place, name, rider):
    n = len(grads)

    def body(pl_ref, *refs):
        ins, outs, _, ride = rider.split(refs, 2 * n, 2 * n, 0)
        j, h = pl.program_id(0), pl.program_id(1)

        @pl.when(jnp.logical_and(j == 0, h == 0))
        def _():
            rider.first(ride)

        for w in range(n):
            s = ins[2 * w][:, 0] + ins[2 * w + 1][...]
            outs[2 * w][...] = s
            outs[2 * w + 1][...] = s.astype(BF16)

        @pl.when(jnp.logical_and(j == N_CHIPS - 1, h == PAIR_SUM_STEPS - 1))
        def _():
            rider.last(ride)

    in_specs, out_specs, out_shape, args = [], [], [], []
    for g, r in zip(grads, recvs):
        _, _, H, C = g.shape
        th = H // PAIR_SUM_STEPS
        spec = pl.BlockSpec((1, th, C), lambda j, h, pr: (j, h, 0))
        in_specs += [pl.BlockSpec((1, 1, th, C), lambda j, h, pr: (j, pr[0], h, 0)), spec]
        out_specs += [spec, spec]
        out_shape += [_sds((N_CHIPS, H, C), F32), _sds((N_CHIPS, H, C), BF16)]
        args += [g, r]
    res = rider.call(body, args, grid=(N_CHIPS, PAIR_SUM_STEPS), name=name, prefetch=place, in_specs=in_specs,
                     out_specs=out_specs, out_shape=out_shape, scratch_shapes=[], vmem_mib=32)
    return [(res[2 * w], res[2 * w + 1]) for w in range(n)], res[2 * n:]


def _add_chips(parts, recvs, place, name, rider):
    n = len(parts)

    def body(pl_ref, *refs):
        ins, outs, _, ride = rider.split(refs, 2 * n, n, 0)
        h = pl.program_id(0)

        @pl.when(h == 0)
        def _():
            rider.first(ride)

        for w in range(n):
            p_ref, r_ref = ins[2 * w], ins[2 * w + 1]
            outs[w][...] = p_ref[0] + r_ref[0].astype(F32) + r_ref[1].astype(F32) + r_ref[2].astype(F32)

        @pl.when(h == CHIP_SUM_STEPS - 1)
        def _():
            rider.last(ride)

    in_specs, out_specs, out_shape, args = [], [], [], []
    for p, r in zip(parts, recvs):
        _, H, C = p.shape
        th = H // CHIP_SUM_STEPS
        in_specs += [pl.BlockSpec((1, th, C), lambda h, pr: (pr[1], h, 0)),
                     pl.BlockSpec((len(CHIP_FLIPS), th, C), lambda h, pr: (0, h, 0))]
        out_specs.append(pl.BlockSpec((th, C), lambda h, pr: (h, 0)))
        out_shape.append(_sds((H, C), F32))
        args += [p, r]
    res = rider.call(body, args, grid=(CHIP_SUM_STEPS,), name=name, prefetch=place, in_specs=in_specs,
                     out_specs=out_specs, out_shape=out_shape, scratch_shapes=[], vmem_mib=32)
    return res[:n], res[n:]


def _adamw_math(w, g, m, v):
    m = ADAM_B1 * m + (1.0 - ADAM_B1) * g
    v = ADAM_B2 * v + (1.0 - ADAM_B2) * (g * g)
    m_hat = m / (1.0 - ADAM_B1 ** ADAM_STEP)
    v_hat = v / (1.0 - ADAM_B2 ** ADAM_STEP)
    delta = -ADAM_LR * (m_hat / (jnp.sqrt(v_hat) + ADAM_EPS) + ADAM_WD * w)
    return delta, m, v


def _adamw(ws, mines, theirs, ms, vs, place, name, rider):
    n = len(ws)

    def body(pl_ref, *refs):
        ins, outs, _, ride = rider.split(refs, 5 * n, 4 * n, 0)
        h, r = pl.program_id(0), pl.program_id(1)

        @pl.when(jnp.logical_and(h == 0, r == 0))
        def _():
            rider.first(ride)

        for k in range(n):
            w_ref, a_ref, b_ref, m_ref, v_ref = ins[5 * k:5 * k + 5]
            g = jnp.where(h == pl_ref[0], a_ref[...], b_ref[...])
            d, mo, vo = _adamw_math(w_ref[...], g, m_ref[...], v_ref[...])
            for o_ref, val in zip(outs[4 * k:4 * k + 4], (g, d, mo, vo)):
                o_ref[...] = val

        @pl.when(jnp.logical_and(h == 1, r == ADAMW_STEPS - 1))
        def _():
            rider.last(ride)

    in_specs, out_specs, out_shape, args = [], [], [], []
    for w, a, b, m, v in zip(ws, mines, theirs, ms, vs):
        R, C = w.shape
        th = (R // 2) // ADAMW_STEPS
        whole = pl.BlockSpec((th, C), lambda h, r, pr: (h * ADAMW_STEPS + r, 0))
        mine_spec = pl.BlockSpec((th, C), lambda h, r, pr: (jnp.where(h == pr[0], r, 0), 0))
        theirs_spec = pl.BlockSpec((th, C), lambda h, r, pr: (jnp.where(h == pr[0], 0, r), 0))
        in_specs += [whole, mine_spec, theirs_spec, whole, whole]
        out_specs += [whole] * 4
        out_shape += [_sds((R, C), F32)] * 4
        args += [w, a, b, m, v]
    res = rider.call(body, args, grid=(2, ADAMW_STEPS), name=name, prefetch=place, in_specs=in_specs,
                     out_specs=out_specs, out_shape=out_shape, scratch_shapes=[], vmem_mib=40)
    return [tuple(res[4 * k:4 * k + 4]) for k in range(n)], res[4 * n:]


DEVICE_FLIPS = tuple((fx, fy, fc) for fx in (0, 1) for fy in (0, 1) for fc in (0, 1))[1:]


def _pack_exchange_rider(pack):
    def phases(ins, outs, ssem, rsem):
        x, y, c = _place()
        mine = outs[0].at[4 * x + 2 * y + c]
        copies = [_remote(ins[0], mine, ssem, rsem, k, (x ^ fx, y ^ fy, c ^ fc))
                  for k, (fx, fy, fc) in enumerate(DEVICE_FLIPS)]
        copies.append(functools.partial(pltpu.make_async_copy, ins[0], mine, ssem.at[len(DEVICE_FLIPS)]))
        return [copies]

    return _Rider([pack], [_sds((N_DEV,) + pack.shape, pack.dtype)], len(DEVICE_FLIPS) + 1, phases)


def _small_sum_adamw(recv_a, recv_b, wpack, mpack, vpack):
    R = wpack.shape[0]

    def body(a_ref, b_ref, w_ref, m_ref, v_ref, gs_ref, d_ref, mo_ref, vo_ref):
        ta, tb = a_ref[0], b_ref[0]
        for dev in range(1, N_DEV):
            ta = ta + a_ref[dev]
            tb = tb + b_ref[dev]
        total = jnp.concatenate([ta, tb], axis=0)
        gs_ref[...] = total
        d, mo, vo = _adamw_math(w_ref[...], total, m_ref[...], v_ref[...])
        d_ref[...] = d
        mo_ref[...] = mo
        vo_ref[...] = vo

    return pl.pallas_call(
        body, name="small_sum_adamw", in_specs=[VMEM_SPEC] * 5, out_specs=[VMEM_SPEC] * 4,
        out_shape=[_sds((R, LANES), F32)] * 4,
    )(recv_a, recv_b, wpack, mpack, vpack)


def _rows8(a):
    a = a.reshape(-1, LANES)
    pad = (-a.shape[0]) % 8
    return jnp.pad(a, ((0, pad), (0, 0))) if pad else a


def _pack(parts):
    return jnp.concatenate([_rows8(a) for a in parts], axis=0)


def _unpack(pack, like):
    out, row = [], 0
    for a in like:
        n = a.size // LANES
        out.append(pack[row:row + n].reshape(a.shape))
        row += n + (-n) % 8
    return out


def kernel(x, p, emb_ln_g, emb_ln_b, w_in, attn_out_g, w_pool, pool_scale, w_out, ln1_g, ln1_b, w_up, w_down, ln2_g, ln2_b, w_ple, w_ple_gate, ln3_g, ln3_b, loss_target, m_emb_ln_g, m_emb_ln_b, m_w_in, m_attn_out_g, m_w_pool, m_pool_scale, m_w_out, m_ln1_g, m_ln1_b, m_w_up, m_w_down, m_ln2_g, m_ln2_b, m_w_ple, m_w_ple_gate, m_ln3_g, m_ln3_b, v_emb_ln_g, v_emb_ln_b, v_w_in, v_attn_out_g, v_w_pool, v_pool_scale, v_w_out, v_ln1_g, v_ln1_b, v_w_up, v_w_down, v_ln2_g, v_ln2_b, v_w_ple, v_w_ple_gate, v_ln3_g, v_ln3_b):
    S = x.shape[1]
    tm = min(256, S)
    tq = min(256, S)
    tm_mlp = min(1024, S)
    tm_pool = min(1024, S)
    tm_ln = min(512, S)
    xs = x[0]
    ps = p[0, 0]
    tgt = loss_target[0]
    row = lambda a: a.reshape(1, -1)
    g0, b0 = row(emb_ln_g), row(emb_ln_b)
    g1, b1, g2, b2, g3, b3 = ln1_g, ln1_b, ln2_g, ln2_b, ln3_g, ln3_b
    wp = w_pool[0]

    xi, yi, ci = _place()
    place = jnp.stack([ci, 2 * xi + yi]).astype(jnp.int32)
    names = ["w_in", "w_out", "w_up", "w_down", "w_ple", "w_ple_gate"]

    big = [w_in[0], w_out[0], w_up[0], w_down[0], w_ple[0], w_ple_gate[0]]
    s_in = _cast_into_slot(big[0], place, "cast_w_in")
    s_out, s_up, s_down, s_ple, s_gate, w_in_s = _cast_rest(big[1:], place, _gather_rider([s_in]))

    xh0, rstd0, q, k, v, u, s_out, s_ple, s_gate = _embln_inproj(
        xs, g0, b0, w_in_s, tm_ln, _gather_rider([s_out, s_ple, s_gate], "chips"))
    o_raw, on, s_up, s_down, w_out_s, w_ple_s, w_gate_s = _attn_fwd(
        q, k, v, attn_out_g, tq, _gather_rider([s_up, s_down], "chips") + _gather_rider([s_out, s_ple, s_gate], "pair"))
    w_out_f = w_out_s.reshape(D_MODEL, D_MODEL)
    w_gate_f = w_gate_s.reshape(D_MODEL, D_MODEL)
    d_b, pooled = _pool_fwd(u, wp, pool_scale, tm_pool)
    xh1, rstd1, x1b, w_up_s, w_down_s = _mix_ln1(on, pooled, xh0, g0, b0, w_out_f, g1, b1, tm_ln,
                                                 _gather_rider([s_up, s_down], "pair"))
    w_down_f = w_down_s.reshape(D_FF, D_MODEL)
    xh2, rstd2, rb = _mlp_ln2(xh1, x1b, g1, b1, w_up_s, w_down_f, tm_mlp)

    (dpre2, dhb, dw_ple, dw_gate, dg3, db3, dg2, db2, loss_row) = _ple_ln3_loss(
        xh2, rstd2, g2, b2, ps, w_ple_s, w_gate_f, g3, b3, tgt, tm_ln)
    def halves_of(g):
        return g.reshape(N_CHIPS, 2, g.shape[1] // 2, g.shape[2])

    ple_halves = [halves_of(dw_ple), halves_of(dw_gate.reshape(N_CHIPS, D_MODEL // N_CHIPS, D_MODEL))]
    dx1m, da, *ple_pair = _mlp_bwd(rb, dhb, w_up_s, w_down_f, tm_mlp, _pair_swap_rider(ple_halves))
    (dw_up,) = _tn_matmul(x1b, da, "grad_w_up", 1024, min(2048, S), True, _no_rider())
    up_halves = halves_of(dw_up)
    dw_down, up_pair = _tn_matmul(rb, dhb, "grad_w_down", 1024, min(2048, S), False,
                                  _pair_swap_rider([up_halves]), square_a=True)
    down_halves = halves_of(dw_down.reshape(N_CHIPS, D_FF // N_CHIPS, D_MODEL))
    dpre1, don, dpooled, dw_out, dg1, db1, down_pair = _mix_bwd(
        dpre2, dx1m, xh1, rstd1, g1, w_out_f, on, pooled, tm_ln, _pair_swap_rider([down_halves]))
    early_sum, _ = _add_pair([up_halves, down_halves] + ple_halves, [up_pair, down_pair] + ple_pair, place,
                             "pair_sum_mlp_ple", _no_rider())
    out_halves = halves_of(dw_out.reshape(N_CHIPS, D_MODEL // N_CHIPS, D_MODEL))
    du, dwp, dsc, out_pair = _pool_bwd(dpooled, d_b, wp, pool_scale, tm_pool, _pair_swap_rider([out_halves]))
    (out_sum,), _ = _add_pair([out_halves], [out_pair], place, "pair_sum_w_out", _no_rider())
    pack_a = _pack([jnp.broadcast_to(loss_row, (8, LANES)), dwp, dsc, dg1, db1, dg2, db2, dg3, db3])
    early_sum = [out_sum] + early_sum
    riding = _chip_scatter_rider([b for _, b in early_sum]) + _pack_exchange_rider(pack_a)
    dq, dk, dv, dga, *arrived = _attn_bwd(q, k, v, don, o_raw, attn_out_g, tq, riding)
    early_chips, recv_a = arrived[:-1], arrived[-1]
    grad_x, dw_in, dg0, db0 = _inproj_bwd(dq, dk, dv, du, dpre1, xh0, rstd0, g0, b0, w_in_s, tm)

    in_halves = halves_of(dw_in)
    pack_b = _pack([dg0, db0, dga])
    early_mine, (in_pair, recv_b) = _add_chips(
        [s for s, _ in early_sum], early_chips, place, "chip_sum_early",
        _pair_swap_rider([in_halves]) + _pack_exchange_rider(pack_b))
    (in_sum,), early_theirs = _add_pair([in_halves], [in_pair], place, "pair_sum_w_in", _pair_send_rider(early_mine))
    ms = [m_w_in, m_w_out, m_w_up, m_w_down, m_w_ple, m_w_ple_gate]
    vs = [v_w_in, v_w_out, v_w_up, v_w_down, v_w_ple, v_w_ple_gate]
    early_res, _ = _adamw(big[1:], early_mine, early_theirs, [m[0] for m in ms[1:]], [v[0] for v in vs[1:]],
                          place, "adamw_early", _no_rider())
    (in_chips,) = _chip_scatter_rider([in_sum[1]]).run("reduce_chips_late")
    (in_mine,), _ = _add_chips([in_sum[0]], [in_chips], place, "chip_sum_w_in", _no_rider())
    (in_theirs,) = _pair_send_rider([in_mine]).run("gather_pair_w_in")
    in_res, _ = _adamw(big[:1], [in_mine], [in_theirs], [ms[0][0]], [vs[0][0]], place, "adamw_w_in", _no_rider())
    big_out = {n: tuple(r.reshape(m.shape) for r in res4) for n, res4, m in zip(names, in_res + early_res, ms)}

    small_names = ["w_pool", "pool_scale", "ln1_g", "ln1_b", "ln2_g", "ln2_b", "ln3_g", "ln3_b",
                   "emb_ln_g", "emb_ln_b", "attn_out_g"]
    small_w = [w_pool, pool_scale, ln1_g, ln1_b, ln2_g, ln2_b, ln3_g, ln3_b, emb_ln_g, emb_ln_b, attn_out_g]
    small_m = [m_w_pool, m_pool_scale, m_ln1_g, m_ln1_b, m_ln2_g, m_ln2_b, m_ln3_g, m_ln3_b,
               m_emb_ln_g, m_emb_ln_b, m_attn_out_g]
    small_v = [v_w_pool, v_pool_scale, v_ln1_g, v_ln1_b, v_ln2_g, v_ln2_b, v_ln3_g, v_ln3_b,
               v_emb_ln_g, v_emb_ln_b, v_attn_out_g]
    loss_like = jnp.zeros((8, LANES), F32)
    gs, ds, mos, vos = _small_sum_adamw(recv_a, recv_b, _pack([loss_like] + small_w), _pack([loss_like] + small_m),
                                        _pack([jnp.ones((8, LANES), F32)] + small_v))
    like = [loss_like] + small_w
    gs_u, ds_u, mos_u, vos_u = (_unpack(a, like) for a in (gs, ds, mos, vos))
    loss = gs_u[0][0, 0]
    small_out = {n: (gs_u[i + 1], ds_u[i + 1], mos_u[i + 1], vos_u[i + 1]) for i, n in enumerate(small_names)}

    order = ["emb_ln_g", "emb_ln_b", "w_in", "attn_out_g", "w_pool", "pool_scale", "w_out", "ln1_g", "ln1_b",
             "w_up", "w_down", "ln2_g", "ln2_b", "w_ple", "w_ple_gate", "ln3_g", "ln3_b"]
    res = {**big_out, **small_out}
    outs = [loss, grad_x.reshape(x.shape)]
    for kind in range(4):
        outs += [res[n][kind] for n in order]
    return tuple(outs)
```

```python
import functools

import jax
import jax.numpy as jnp
from jax import lax
from jax.experimental import pallas as pl
from jax.experimental.pallas import tpu as pltpu

F32 = jnp.float32
BF16 = jnp.bfloat16

D_MODEL = 1024
ATTN_WIDTH = 512
POOL_WIDTH = 512
HEAD_DIM = 64
PAIR = 2 * HEAD_DIM
N_PAIRS = ATTN_WIDTH // PAIR
N_POOL_GROUPS = 4
POOL_GROUP = 128
POOL_HALO = 16
D_FF = 4096
PLE_DIM = 256
N_CHIPS = 4
N_DEV = 8
LN_EPS = 1e-5
RMS_EPS = 1e-6
ALPHA = float(2.0 ** 0.25)
Q_SCALE = 0.125
ADAM_LR = 0.001
ADAM_B1 = 0.9
ADAM_B2 = 0.999
ADAM_EPS = 1e-08
ADAM_WD = 0.01
ADAM_STEP = 10
LANES = 128
MIB = 1024 * 1024

MESH = pl.DeviceIdType.MESH
HBM_SPEC = pl.BlockSpec(memory_space=pltpu.HBM)
VMEM_SPEC = pl.BlockSpec(memory_space=pltpu.VMEM)


def _cp(vmem_mib):
    return pltpu.CompilerParams(vmem_limit_bytes=vmem_mib * MIB)


def _dot(a, b):
    return jnp.dot(a, b, preferred_element_type=F32)


def _dot_nt(a, b):
    return lax.dot_general(a, b, (((1,), (1,)), ((), ())), preferred_element_type=F32)


def _dot_tn(a, b):
    return lax.dot_general(a, b, (((0,), (0,)), ((), ())), preferred_element_type=F32)


def _ln_fwd(pre):
    mu = jnp.mean(pre, axis=-1, keepdims=True)
    xc = pre - mu
    var = jnp.mean(xc * xc, axis=-1, keepdims=True)
    rstd = lax.rsqrt(var + LN_EPS)
    return xc * rstd, rstd


def _ln_bwd(dy, xh, rstd, g):
    dxh = dy * g
    m1 = jnp.mean(dxh, axis=-1, keepdims=True)
    m2 = jnp.mean(dxh * xh, axis=-1, keepdims=True)
    return rstd * (dxh - m1 - xh * m2)


def _colsum(a):
    return jnp.sum(a, axis=0, keepdims=True)


def _neg_softplus(z):
    return -(jnp.maximum(z, 0.0) + jnp.log(1.0 + jnp.exp(-jnp.abs(z))))


def _row_spec(tm, n):
    return pl.BlockSpec((tm, n), lambda i: (i, 0))


def _const_spec(shape):
    nd = len(shape)
    return pl.BlockSpec(shape, lambda *_: (0,) * nd)


def _hbm(*arrays):
    return [pltpu.with_memory_space_constraint(a, pltpu.HBM) for a in arrays]


def _sds(shape, dtype):
    return pltpu.HBM(shape, dtype)


def _embln_inproj(x, g0, b0, w_in_s, tm, rider):
    S, D = x.shape
    n_t = S // tm

    def body(*refs):
        ((x_ref, g_ref, b_ref, w_ref), (xh_ref, rstd_ref, q_ref, k_ref, v_ref, u_ref), _,
         ride) = rider.split(refs, 4, 6, 0)
        i = pl.program_id(0)

        @pl.when(i == 0)
        def _():
            rider.first(ride)

        @pl.when(i == (3 * n_t) // 4)
        def _():
            rider.mid(ride)

        xh, rstd = _ln_fwd(x_ref[...])
        xh_ref[...] = xh
        rstd_ref[...] = rstd
        xb = (xh * g_ref[...] + b_ref[...]).astype(BF16)
        q_ref[...] = (_dot(xb, w_ref[0]) * Q_SCALE).astype(BF16)
        k_ref[...] = _dot(xb, w_ref[1]).astype(BF16)
        v_ref[...] = _dot(xb, w_ref[2]).astype(BF16)
        u_ref[...] = _dot(xb, w_ref[3])

        @pl.when(i == n_t - 1)
        def _():
            rider.last(ride)

    return rider.call(
        body, [x, g0, b0, w_in_s], grid=(n_t,), name="embln_inproj",
        in_specs=[_row_spec(tm, D), _const_spec((1, D)), _const_spec((1, D)),
                  _const_spec((N_CHIPS, D, 512))],
        out_specs=[_row_spec(tm, D), _row_spec(tm, 1), _row_spec(tm, 512), _row_spec(tm, 512),
                   _row_spec(tm, 512), _row_spec(tm, 512)],
        out_shape=[_sds((S, D), F32), _sds((S, 1), F32), _sds((S, 512), BF16), _sds((S, 512), BF16),
                   _sds((S, 512), BF16), _sds((S, 512), F32)],
        scratch_shapes=[], vmem_mib=56)


def _tri(n, upper):
    r = lax.broadcasted_iota(jnp.int32, (n, n), 0)
    c = lax.broadcasted_iota(jnp.int32, (n, n), 1)
    keep = (r < c) if upper else (r > c)
    return jnp.where(keep, 1.0, 0.0).astype(BF16)


def _strictly_causal(n):
    return lax.broadcasted_iota(jnp.int32, (n, n), 1) < lax.broadcasted_iota(jnp.int32, (n, n), 0)


LOG_WEIGHT_FLOOR = -110.0


def _weights_left(c_ls):
    return (jnp.max(jnp.maximum(c_ls[0], c_ls[1])) > LOG_WEIGHT_FLOOR).astype(jnp.int32)


def _sb_tile(qhs, kt, low, c_ls, valid):
    valids = valid if isinstance(valid, (list, tuple)) else [valid] * len(qhs)
    zs = [_dot_nt(qh, kt) for qh in qhs]
    lrs = [_neg_softplus(z) for z in zs]
    ls_ = [lr if m is None else jnp.where(m, lr, 0.0) for lr, m in zip(lrs, valids)]
    sfx = [_dot(l.astype(BF16), low) + c_l for l, c_l in zip(ls_, c_ls)]
    lss = [z + lr for z, lr in zip(zs, lrs)]
    ws = [jnp.exp(ls + s) for ls, s in zip(lss, sfx)]
    ws = [w if m is None else jnp.where(m, w, 0.0) for w, m in zip(ws, valids)]
    return lss, ls_, ws


def _attn_fwd(q, k, v, ga, tq, rider):
    S = q.shape[0]
    nq = S // tq

    def body(*refs):
        (q_ref, k_ref, v_ref, ga_ref), (o_ref, on_ref), (low_s,), ride = rider.split(refs, 4, 2, 1)
        p, i = pl.program_id(0), pl.program_id(1)

        @pl.when(jnp.logical_and(p == 0, i == 0))
        def _():
            rider.first(ride)
            low_s[...] = _tri(tq, upper=False)

        @pl.when(jnp.logical_and(p == N_PAIRS - 1, i == 0))
        def _():
            rider.mid(ride)

        lane = lax.broadcasted_iota(jnp.int32, (1, PAIR), 1)
        m0 = lane < HEAD_DIM
        low = low_s[...]
        q2 = q_ref[...]
        qhs = [jnp.where(m0, q2, jnp.zeros_like(q2)), jnp.where(m0, jnp.zeros_like(q2), q2)]

        def tile(kb, c_ls, accs, valid):
            ks = pl.multiple_of(kb * tq, tq)
            kt = k_ref[pl.ds(ks, tq), :]
            vt = v_ref[pl.ds(ks, tq), :]
            _, ls_, ws = _sb_tile(qhs, kt, low, c_ls, valid)
            new_a = [acc + _dot(w.astype(BF16), vt) for acc, w in zip(accs, ws)]
            new_c = [c_l + jnp.sum(l, axis=1, keepdims=True) for c_l, l in zip(c_ls, ls_)]
            return new_c, new_a

        zc, za = jnp.zeros((tq, 1), F32), jnp.zeros((tq, PAIR), F32)

        def first_two():
            c_ls, accs = tile(i, [zc, zc], [za, za], _strictly_causal(tq))
            c_ls, accs = tile(i - 1, c_ls, accs, None)
            return (_weights_left(c_ls), *c_ls, *accs)

        def first_one():
            c_ls, accs = tile(i, [zc, zc], [za, za], _strictly_causal(tq))
            return (jnp.int32(0), *c_ls, *accs)

        st0 = lax.cond(i >= 1, first_two, first_one)

        def more(st):
            return jnp.logical_and(st[0] <= i, st[1] > 0)

        def step(st):
            n, _, c0, c1, a0, a1 = st
            c_ls, accs = tile(i - n, [c0, c1], [a0, a1], None)
            return (n + 1, _weights_left(c_ls), c_ls[0], c_ls[1], accs[0], accs[1])

        st = lax.while_loop(more, step, (jnp.int32(2), *st0))
        o = jnp.where(m0, st[4], st[5])
        o_ref[...] = o
        sq = o * o
        ms0 = jnp.sum(jnp.where(m0, sq, 0.0), axis=-1, keepdims=True) * (1.0 / HEAD_DIM)
        ms1 = jnp.sum(jnp.where(m0, 0.0, sq), axis=-1, keepdims=True) * (1.0 / HEAD_DIM)
        rs = jnp.where(m0, lax.rsqrt(ms0 + RMS_EPS), lax.rsqrt(ms1 + RMS_EPS))
        on_ref[...] = (o * rs * ga_ref[...]).astype(BF16)

        @pl.when(jnp.logical_and(p == N_PAIRS - 1, i == nq - 1))
        def _():
            rider.last(ride)

    return rider.call(
        body, [q, k, v, ga], grid=(N_PAIRS, nq), name="attn_fwd",
        in_specs=[pl.BlockSpec((tq, PAIR), lambda p, i: (i, p)),
                  pl.BlockSpec((S, PAIR), lambda p, i: (0, p)),
                  pl.BlockSpec((S, PAIR), lambda p, i: (0, p)),
                  pl.BlockSpec((1, PAIR), lambda p, i: (0, p))],
        out_specs=[pl.BlockSpec((tq, PAIR), lambda p, i: (i, p)),
                   pl.BlockSpec((tq, PAIR), lambda p, i: (i, p))],
        out_shape=[_sds((S, ATTN_WIDTH), F32), _sds((S, ATTN_WIDTH), BF16)],
        scratch_shapes=[pltpu.VMEM((tq, tq), BF16)], vmem_mib=40)


def _pool_fwd(u, w_pool, pscale, tm):
    S = u.shape[0]
    hb = tm // POOL_HALO

    def body(u_ref, uh_ref, wp_ref, sc_ref, d_ref, pooled_ref):
        i = pl.program_id(0)
        halo = jnp.where(i > 0, uh_ref[...], 0.0)
        pos = i * tm + lax.broadcasted_iota(jnp.int32, (tm, 1), 0)
        for g in range(N_POOL_GROUPS):
            win = 2 ** (g + 1)
            cols = slice(g * POOL_GROUP, (g + 1) * POOL_GROUP)
            ut = u_ref[:, cols]
            s = jnp.concatenate([halo[:, cols], ut], axis=0)
            for sh in (1, 2, 4, 8)[:g + 1]:
                s = s + pltpu.roll(s, sh, 0)
            cnt = jnp.minimum(pos + 1, win).astype(F32)
            db = (s[POOL_HALO:, :] / cnt - ut).astype(BF16)
            y = _dot(db, wp_ref[g].astype(BF16))
            d_ref[:, cols] = db
            pooled_ref[:, cols] = (y * sc_ref[:, cols]).astype(BF16)

    return pl.pallas_call(
        body, grid=(S // tm,), name="pool_fwd",
        in_specs=[_row_spec(tm, POOL_WIDTH),
                  pl.BlockSpec((POOL_HALO, POOL_WIDTH), lambda i: (jnp.maximum(i * hb - 1, 0), 0)),
                  _const_spec((N_POOL_GROUPS, POOL_GROUP, POOL_GROUP)), _const_spec((1, POOL_WIDTH))],
        out_specs=[_row_spec(tm, POOL_WIDTH), _row_spec(tm, POOL_WIDTH)],
        out_shape=[_sds((S, POOL_WIDTH), BF16), _sds((S, POOL_WIDTH), BF16)],
        compiler_params=_cp(32),
    )(*_hbm(u, u, w_pool, pscale))


def _mix_ln1(on, pooled, xh0, g0, b0, w_out, g1, b1, tm, rider):
    S, D = xh0.shape
    n_t = S // tm

    def body(*refs):
        ((on_ref, po_ref, xh0_ref, g0_ref, b0_ref, w_ref, g1_ref, b1_ref), (xh_ref, rstd_ref, xb_ref), _,
         ride) = rider.split(refs, 8, 3, 0)

        @pl.when(pl.program_id(0) == 0)
        def _():
            rider.first(ride)

        mixed = _dot(on_ref[...], w_ref[:ATTN_WIDTH, :]) + _dot(po_ref[...], w_ref[ATTN_WIDTH:, :])
        x0 = xh0_ref[...] * g0_ref[...] + b0_ref[...]
        xh, rstd = _ln_fwd(ALPHA * x0 + mixed)
        xh_ref[...] = xh
        rstd_ref[...] = rstd
        xb_ref[...] = (xh * g1_ref[...] + b1_ref[...]).astype(BF16)

        @pl.when(pl.program_id(0) == n_t - 1)
        def _():
            rider.last(ride)

    return rider.call(
        body, [on, pooled, xh0, g0, b0, w_out, g1, b1], grid=(n_t,), name="mix_ln1",
        in_specs=[_row_spec(tm, ATTN_WIDTH), _row_spec(tm, POOL_WIDTH), _row_spec(tm, D),
                  _const_spec((1, D)), _const_spec((1, D)), _const_spec((D, D)),
                  _const_spec((1, D)), _const_spec((1, D))],
        out_specs=[_row_spec(tm, D), _row_spec(tm, 1), _row_spec(tm, D)],
        out_shape=[_sds((S, D), F32), _sds((S, 1), F32), _sds((S, D), BF16)],
        scratch_shapes=[], vmem_mib=56)


def _mlp_ln2(xh1, x1b, g1, b1, w_up_s, w_down, tm):
    S, D = xh1.shape
    fc = D_FF // N_CHIPS

    def body(xh_ref, xb_ref, g_ref, b_ref, wu_ref, wd_ref, xh2_ref, rstd_ref, r_ref, acc_ref):
        j = pl.program_id(1)

        @pl.when(j == 0)
        def _():
            acc_ref[...] = jnp.zeros_like(acc_ref)

        r = jnp.maximum(_dot(xb_ref[...], wu_ref[0]), 0.0)
        r_ref[...] = r.astype(BF16)
        acc_ref[...] += _dot((r * r).astype(BF16), wd_ref[...])

        @pl.when(j == N_CHIPS - 1)
        def _():
            x1 = xh_ref[...] * g_ref[...] + b_ref[...]
            xh, rstd = _ln_fwd(ALPHA * x1 + acc_ref[...])
            xh2_ref[...] = xh
            rstd_ref[...] = rstd

    return pl.pallas_call(
        body, grid=(S // tm, N_CHIPS), name="mlp_ln2",
        in_specs=[pl.BlockSpec((tm, D), lambda i, j: (i, 0)), pl.BlockSpec((tm, D), lambda i, j: (i, 0)),
                  pl.BlockSpec((1, D), lambda i, j: (0, 0)), pl.BlockSpec((1, D), lambda i, j: (0, 0)),
                  pl.BlockSpec((1, D, fc), lambda i, j: (j, 0, 0)),
                  pl.BlockSpec((fc, D), lambda i, j: (j, 0))],
        out_specs=[pl.BlockSpec((tm, D), lambda i, j: (i, 0)), pl.BlockSpec((tm, 1), lambda i, j: (i, 0)),
                   pl.BlockSpec((tm, fc), lambda i, j: (i, j))],
        out_shape=[_sds((S, D), F32), _sds((S, 1), F32), _sds((S, D_FF), BF16)],
        scratch_shapes=[pltpu.VMEM((tm, D), F32)],
        compiler_params=_cp(56),
    )(*_hbm(xh1, x1b, g1, b1, w_up_s, w_down))


def _ple_ln3_loss(xh2, rstd2, g2, b2, p, w_ple_s, w_gate, g3, b3, target, tm):
    S, D = xh2.shape
    pc = D // N_CHIPS

    def body(xh2_ref, rstd2_ref, g2_ref, b2_ref, p_ref, wp_ref, wg_ref, g3_ref, b3_ref, t_ref,
             dpre2_ref, dhb_ref, dwp_ref, dwg_ref, dg3_ref, db3_ref, dg2_ref, db2_ref, loss_ref):
        i = pl.program_id(0)

        @pl.when(i == 0)
        def _():
            for r in (dwp_ref, dwg_ref, dg3_ref, db3_ref, dg2_ref, db2_ref, loss_ref):
                r[...] = jnp.zeros_like(r)

        xh2 = xh2_ref[...]
        x2 = xh2 * g2_ref[...] + b2_ref[...]
        x2b = x2.astype(BF16)
        gate = 1.0 / (1.0 + jnp.exp(-_dot(x2b, wg_ref[...])))
        pb = p_ref[...].astype(BF16)
        pe = jnp.concatenate([_dot(pb, wp_ref[c]) for c in range(N_CHIPS)], axis=1)
        xh3, rstd3 = _ln_fwd(ALPHA * x2 + pe * gate)
        diff = xh3 * g3_ref[...] + b3_ref[...] - t_ref[...]
        loss_ref[...] += (0.5 / D) * jnp.sum(diff * diff)
        dy = diff * (1.0 / D)
        dg3_ref[...] += _colsum(dy * xh3)
        db3_ref[...] += _colsum(dy)
        dpre3 = _ln_bwd(dy, xh3, rstd3, g3_ref[...])
        dpe_b = (dpre3 * gate).astype(BF16)
        dgp_b = (dpre3 * pe * gate * (1.0 - gate)).astype(BF16)
        dx2 = ALPHA * dpre3 + _dot_nt(dgp_b, wg_ref[...])
        dwg_ref[...] += _dot_tn(x2b, dgp_b)
        for c in range(N_CHIPS):
            dwp_ref[c] += _dot_tn(pb, dpe_b[:, c * pc:(c + 1) * pc])
        dg2_ref[...] += _colsum(dx2 * xh2)
        db2_ref[...] += _colsum(dx2)
        dpre2 = _ln_bwd(dx2, xh2, rstd2_ref[...], g2_ref[...])
        dpre2_ref[...] = dpre2
        dhb_ref[...] = dpre2.astype(BF16)

    vec = _const_spec((1, D))
    return pl.pallas_call(
        body, grid=(S // tm,), name="ple_ln3_loss",
        in_specs=[_row_spec(tm, D), _row_spec(tm, 1), vec, vec, _row_spec(tm, PLE_DIM),
                  _const_spec((N_CHIPS, PLE_DIM, pc)), _const_spec((D, D)), vec, vec, _row_spec(tm, D)],
        out_specs=[_row_spec(tm, D), _row_spec(tm, D), _const_spec((N_CHIPS, PLE_DIM, pc)),
                   _const_spec((D, D)), vec, vec, vec, vec, _const_spec((1, LANES))],
        out_shape=[_sds((S, D), F32), _sds((S, D), BF16), _sds((N_CHIPS, PLE_DIM, pc), F32),
                   _sds((D, D), F32), _sds((1, D), F32), _sds((1, D), F32), _sds((1, D), F32),
                   _sds((1, D), F32), _sds((1, LANES), F32)],
        compiler_params=_cp(58),
    )(*_hbm(xh2, rstd2, g2, b2, p, w_ple_s, w_gate, g3, b3, target))


def _mlp_bwd(rb, dhb, w_up_s, w_down, tm, rider):
    S, D = dhb.shape
    fc = D_FF // N_CHIPS
    n_t = S // tm

    def body(*refs):
        (r_ref, dh_ref, wu_ref, wd_ref), (dx_ref, da_ref), _, ride = rider.split(refs, 4, 2, 0)
        i, j = pl.program_id(0), pl.program_id(1)

        @pl.when(jnp.logical_and(i == 0, j == 0))
        def _():
            rider.first(ride)

        @pl.when(j == 0)
        def _():
            dx_ref[...] = jnp.zeros_like(dx_ref)

        da = (_dot_nt(dh_ref[...], wd_ref[...]) * (2.0 * r_ref[...].astype(F32))).astype(BF16)
        da_ref[...] = da
        dx_ref[...] += _dot_nt(da, wu_ref[0])

        @pl.when(jnp.logical_and(i == n_t - 1, j == N_CHIPS - 1))
        def _():
            rider.last(ride)

    return rider.call(
        body, [rb, dhb, w_up_s, w_down], grid=(n_t, N_CHIPS), name="mlp_bwd",
        in_specs=[pl.BlockSpec((tm, fc), lambda i, j: (i, j)), pl.BlockSpec((tm, D), lambda i, j: (i, 0)),
                  pl.BlockSpec((1, D, fc), lambda i, j: (j, 0, 0)),
                  pl.BlockSpec((fc, D), lambda i, j: (j, 0))],
        out_specs=[pl.BlockSpec((tm, D), lambda i, j: (i, 0)), pl.BlockSpec((tm, fc), lambda i, j: (i, j))],
        out_shape=[_sds((S, D), F32), _sds((S, D_FF), BF16)],
        scratch_shapes=[], vmem_mib=56)


def _tn_matmul(a, b, name, tk, tt, stacked, rider, square_a=False):
    T, K = a.shape
    N = b.shape[1]
    tn = 1024
    grid = (K // tk, N // tn, T // tt)

    def body(*refs):
        (a_ref, b_ref), (o_ref,), _, ride = rider.split(refs, 2, 1, 0)
        at = [pl.program_id(d) for d in range(3)]

        @pl.when(jnp.logical_and(jnp.logical_and(at[0] == 0, at[1] == 0), at[2] == 0))
        def _():
            rider.first(ride)

        @pl.when(at[2] == 0)
        def _():
            o_ref[...] = jnp.zeros_like(o_ref)

        a_t = a_ref[...]
        if square_a:
            a_t = a_t * a_t
        prod = _dot_tn(a_t, b_ref[...])
        if stacked:
            o_ref[0] += prod
        else:
            o_ref[...] += prod

        @pl.when(jnp.logical_and(jnp.logical_and(at[0] == grid[0] - 1, at[1] == grid[1] - 1),
                                 at[2] == grid[2] - 1))
        def _():
            rider.last(ride)

    if stacked:
        out_spec = pl.BlockSpec((1, tk, tn), lambda k, n, t: (n, k, 0))
        out_shape = _sds((N // tn, K, tn), F32)
    else:
        out_spec = pl.BlockSpec((tk, tn), lambda k, n, t: (k, n))
        out_shape = _sds((K, N), F32)
    return rider.call(
        body, [a, b], grid=grid, name=name,
        in_specs=[pl.BlockSpec((tt, tk), lambda k, n, t: (t, k)),
                  pl.BlockSpec((tt, tn), lambda k, n, t: (t, n))],
        out_specs=[out_spec], out_shape=[out_shape], scratch_shapes=[], vmem_mib=58)


def _mix_bwd(dpre2, dx1m, xh1, rstd1, g1, w_out, on, pooled, tm, rider):
    S, D = xh1.shape
    n_t = S // tm

    def body(*refs):
        ((dp2_ref, dxm_ref, xh_ref, rstd_ref, g_ref, w_ref, on_ref, po_ref),
         (dpre1_ref, don_ref, dpo_ref, dw_ref, dg_ref, db_ref), _, ride) = rider.split(refs, 8, 6, 0)

        @pl.when(pl.program_id(0) == 0)
        def _():
            rider.first(ride)
            for r in (dw_ref, dg_ref, db_ref):
                r[...] = jnp.zeros_like(r)

        xh = xh_ref[...]
        dx1 = ALPHA * dp2_ref[...] + dxm_ref[...]
        dg_ref[...] += _colsum(dx1 * xh)
        db_ref[...] += _colsum(dx1)
        dpre1 = _ln_bwd(dx1, xh, rstd_ref[...], g_ref[...])
        dpre1_ref[...] = dpre1
        dmb = dpre1.astype(BF16)
        dcat = _dot_nt(dmb, w_ref[...])
        don_ref[...] = dcat[:, :ATTN_WIDTH]
        dpo_ref[...] = dcat[:, ATTN_WIDTH:]
        dw_ref[:ATTN_WIDTH, :] += _dot_tn(on_ref[...], dmb)
        dw_ref[ATTN_WIDTH:, :] += _dot_tn(po_ref[...], dmb)

        @pl.when(pl.program_id(0) == n_t - 1)
        def _():
            rider.last(ride)

    vec = _const_spec((1, D))
    return rider.call(
        body, [dpre2, dx1m, xh1, rstd1, g1, w_out, on, pooled], grid=(n_t,), name="mix_bwd",
        in_specs=[_row_spec(tm, D), _row_spec(tm, D), _row_spec(tm, D), _row_spec(tm, 1), vec,
                  _const_spec((D, D)), _row_spec(tm, ATTN_WIDTH), _row_spec(tm, POOL_WIDTH)],
        out_specs=[_row_spec(tm, D), _row_spec(tm, ATTN_WIDTH), _row_spec(tm, POOL_WIDTH),
                   _const_spec((D, D)), vec, vec],
        out_shape=[_sds((S, D), F32), _sds((S, ATTN_WIDTH), F32), _sds((S, POOL_WIDTH), F32),
                   _sds((D, D), F32), _sds((1, D), F32), _sds((1, D), F32)],
        scratch_shapes=[], vmem_mib=56)


def _pool_bwd(dpooled, d_b, w_pool, pscale, tm, rider):
    S = dpooled.shape[0]
    hb = tm // POOL_HALO
    n_t = S // tm
    te = tm + POOL_HALO

    def body(*refs):
        ((dp_ref, dph_ref, d_ref, wp_ref, sc_ref), (du_ref, dwp_ref, dsc_ref), _,
         ride) = rider.split(refs, 5, 3, 0)
        i = pl.program_id(0)

        @pl.when(i == 0)
        def _():
            rider.first(ride)
            dwp_ref[...] = jnp.zeros_like(dwp_ref)
            dsc_ref[...] = jnp.zeros_like(dsc_ref)

        halo = jnp.where(i < n_t - 1, dph_ref[...], 0.0)
        pos = i * tm + lax.broadcasted_iota(jnp.int32, (te, 1), 0)
        for g in range(N_POOL_GROUPS):
            win = 2 ** (g + 1)
            cols = slice(g * POOL_GROUP, (g + 1) * POOL_GROUP)
            wpb = wp_ref[g].astype(BF16)
            dpt = dp_ref[:, cols]
            dpe = jnp.concatenate([dpt, halo[:, cols]], axis=0)
            dyb = (dpe * sc_ref[:, cols]).astype(BF16)
            dd = _dot_nt(dyb, wpb)
            s = dd / jnp.minimum(pos + 1, win).astype(F32)
            for sh in (1, 2, 4, 8)[:g + 1]:
                s = s + pltpu.roll(s, te - sh, 0)
            du_ref[:, cols] = s[:tm, :] - dd[:tm, :]
            db = d_ref[:, cols]
            dwp_ref[g] += _dot_tn(db, dyb[:tm, :])
            dsc_ref[:, cols] += _colsum(dpt * _dot(db, wpb))

        @pl.when(i == n_t - 1)
        def _():
            rider.last(ride)

    return rider.call(
        body, [dpooled, dpooled, d_b, w_pool, pscale], grid=(n_t,), name="pool_bwd",
        in_specs=[_row_spec(tm, POOL_WIDTH),
                  pl.BlockSpec((POOL_HALO, POOL_WIDTH),
                               lambda i: (jnp.minimum((i + 1) * hb, S // POOL_HALO - 1), 0)),
                  _row_spec(tm, POOL_WIDTH),
                  _const_spec((N_POOL_GROUPS, POOL_GROUP, POOL_GROUP)), _const_spec((1, POOL_WIDTH))],
        out_specs=[_row_spec(tm, POOL_WIDTH), _const_spec((N_POOL_GROUPS, POOL_GROUP, POOL_GROUP)),
                   _const_spec((1, POOL_WIDTH))],
        out_shape=[_sds((S, POOL_WIDTH), F32), _sds((N_POOL_GROUPS, POOL_GROUP, POOL_GROUP), F32),
                   _sds((1, POOL_WIDTH), F32)],
        scratch_shapes=[], vmem_mib=32)


def _attn_bwd(q, k, v, don, o_raw, ga, tq, rider):
    S = q.shape[0]
    nq = S // tq

    def body(*refs):
        ((q_ref, k_ref, v_ref, don_ref, o_ref, ga_ref), (dq_ref, dk_ref, dv_ref, dga_ref),
         (g_s, b_s, low_s, upp_s), ride) = rider.split(refs, 6, 4, 4)
        p, i = pl.program_id(0), pl.program_id(1)

        @pl.when(jnp.logical_and(p == 0, i == 0))
        def _():
            rider.first(ride)
            low_s[...] = _tri(tq, upper=False)
            upp_s[...] = _tri(tq, upper=True)

        @pl.when(i == 0)
        def _():
            for r in (dk_ref, dv_ref, dga_ref):
                r[...] = jnp.zeros_like(r)

        lane = lax.broadcasted_iota(jnp.int32, (1, PAIR), 1)
        m0 = lane < HEAD_DIM
        low = low_s[...]
        upp = upp_s[...]

        def seg_mean(a):
            s0 = jnp.sum(jnp.where(m0, a, 0.0), axis=-1, keepdims=True)
            s1 = jnp.sum(jnp.where(m0, 0.0, a), axis=-1, keepdims=True)
            return jnp.where(m0, s0, s1) * (1.0 / HEAD_DIM)

        o = o_ref[...]
        rs = lax.rsqrt(seg_mean(o * o) + RMS_EPS)
        oh = o * rs
        don = don_ref[...]
        dga_ref[...] += _colsum(don * oh)
        doh = don * ga_ref[...]
        do = rs * (doh - oh * seg_mean(doh * oh))
        dob = do.astype(BF16)
        q2 = q_ref[...]
        qhs = [jnp.where(m0, q2, jnp.zeros_like(q2)), jnp.where(m0, jnp.zeros_like(q2), q2)]
        dhs = [jnp.where(m0, dob, jnp.zeros_like(dob)), jnp.where(m0, jnp.zeros_like(dob), dob)]
        causal = _strictly_causal(tq)

        def down(kb, c_ls, valid):
            ks = pl.multiple_of(kb * tq, tq)
            kt = k_ref[pl.ds(ks, tq), :]
            vt = v_ref[pl.ds(ks, tq), :]
            lss, ls_, ws = _sb_tile(qhs, kt, low, c_ls, valid)
            dws = [_dot_nt(dh, vt) for dh in dhs]
            for hh in range(2):
                g_s[hh, kb] = dws[hh] * ws[hh]
                b_s[hh, kb] = jnp.exp(lss[hh])
            dv_ref[pl.ds(ks, tq), :] += (_dot_tn(ws[0].astype(BF16), dhs[0])
                                         + _dot_tn(ws[1].astype(BF16), dhs[1]))
            return [c_l + jnp.sum(l, axis=1, keepdims=True) for c_l, l in zip(c_ls, ls_)]

        zc, za = jnp.zeros((tq, 1), F32), jnp.zeros((tq, PAIR), F32)
        def first_two():
            c_ls = down(i - 1, down(i, [zc, zc], causal), None)
            return (_weights_left(c_ls), *c_ls)

        st0 = lax.cond(i >= 1, first_two, lambda: (jnp.int32(0), *down(i, [zc, zc], causal)))

        def more(st):
            return jnp.logical_and(st[0] <= i, st[1] > 0)

        def down_step(st):
            c_ls = down(i - st[0], [st[2], st[3]], None)
            return (st[0] + 1, _weights_left(c_ls), c_ls[0], c_ls[1])

        n_tiles = lax.while_loop(more, down_step, (jnp.int32(2), *st0))[0]

        def up(kb, c_gs, accs, valid):
            ks = pl.multiple_of(kb * tq, tq)
            kt = k_ref[pl.ds(ks, tq), :]
            gs = [g_s[hh, kb] for hh in range(2)]
            pres = [_dot(g.astype(BF16), upp) + c_g for g, c_g in zip(gs, c_gs)]
            dzs = []
            for hh in range(2):
                beta = b_s[hh, kb]
                dz = gs[hh] - beta * (gs[hh] + pres[hh])
                if valid is not None:
                    dz = jnp.where(valid, dz, 0.0)
                dzs.append(dz.astype(BF16))
            new_a = [acc + _dot(dzb, kt) for acc, dzb in zip(accs, dzs)]
            dk_ref[pl.ds(ks, tq), :] += _dot_tn(dzs[0], qhs[0]) + _dot_tn(dzs[1], qhs[1])
            new_c = [c_g + jnp.sum(g, axis=1, keepdims=True) for c_g, g in zip(c_gs, gs)]
            return new_c, new_a

        def up_step(kb, st):
            c_gs, accs = up(kb, [st[0], st[1]], [st[2], st[3]], None)
            return (c_gs[0], c_gs[1], accs[0], accs[1])

        st = lax.fori_loop(i - n_tiles + 1, i - 1, up_step, (zc, zc, za, za))

        def last_two():
            c_gs, accs = up(i - 1, [st[0], st[1]], [st[2], st[3]], None)
            return tuple(up(i, c_gs, accs, causal)[1])

        accs = lax.cond(i >= 1, last_two, lambda: tuple(up(i, [zc, zc], [za, za], causal)[1]))
        dq_ref[...] = jnp.where(m0, accs[0], accs[1]) * Q_SCALE

        @pl.when(jnp.logical_and(p == N_PAIRS - 1, i == nq - 1))
        def _():
            rider.last(ride)

    return rider.call(
        body, [q, k, v, don, o_raw, ga], grid=(N_PAIRS, nq), name="attn_bwd",
        in_specs=[pl.BlockSpec((tq, PAIR), lambda p, i: (i, p)),
                  pl.BlockSpec((S, PAIR), lambda p, i: (0, p)),
                  pl.BlockSpec((S, PAIR), lambda p, i: (0, p)),
                  pl.BlockSpec((tq, PAIR), lambda p, i: (i, p)),
                  pl.BlockSpec((tq, PAIR), lambda p, i: (i, p)),
                  pl.BlockSpec((1, PAIR), lambda p, i: (0, p))],
        out_specs=[pl.BlockSpec((tq, PAIR), lambda p, i: (i, p)),
                   pl.BlockSpec((S, PAIR), lambda p, i: (0, p)),
                   pl.BlockSpec((S, PAIR), lambda p, i: (0, p)),
                   pl.BlockSpec((1, PAIR), lambda p, i: (0, p))],
        out_shape=[_sds((S, ATTN_WIDTH), F32), _sds((S, ATTN_WIDTH), F32), _sds((S, ATTN_WIDTH), F32),
                   _sds((1, ATTN_WIDTH), F32)],
        scratch_shapes=[pltpu.VMEM((2, nq, tq, tq), F32), pltpu.VMEM((2, nq, tq, tq), F32),
                        pltpu.VMEM((tq, tq), BF16), pltpu.VMEM((tq, tq), BF16)],
        vmem_mib=56)


def _inproj_bwd(dq, dk, dv, du, dpre1, xh0, rstd0, g0, b0, w_in_s, tm):
    S, D = xh0.shape

    def body(dq_ref, dk_ref, dv_ref, du_ref, dp1_ref, xh_ref, rstd_ref, g_ref, b_ref, w_ref,
             gx_ref, dw_ref, dg_ref, db_ref):
        @pl.when(pl.program_id(0) == 0)
        def _():
            for r in (dw_ref, dg_ref, db_ref):
                r[...] = jnp.zeros_like(r)

        xh = xh_ref[...]
        xb = (xh * g_ref[...] + b_ref[...]).astype(BF16)
        dx0 = ALPHA * dp1_ref[...]
        for c, r in enumerate((dq_ref, dk_ref, dv_ref, du_ref)):
            dpb = r[...].astype(BF16)
            dx0 = dx0 + _dot_nt(dpb, w_ref[c])
            dw_ref[c] += _dot_tn(xb, dpb)
        dg_ref[...] += _colsum(dx0 * xh)
        db_ref[...] += _colsum(dx0)
        gx_ref[...] = _ln_bwd(dx0, xh, rstd_ref[...], g_ref[...])

    vec = _const_spec((1, D))
    half = _row_spec(tm, 512)
    return pl.pallas_call(
        body, grid=(S // tm,), name="inproj_bwd",
        in_specs=[half, half, half, half, _row_spec(tm, D), _row_spec(tm, D), _row_spec(tm, 1), vec, vec,
                  _const_spec((N_CHIPS, D, 512))],
        out_specs=[_row_spec(tm, D), _const_spec((N_CHIPS, D, 512)), vec, vec],
        out_shape=[_sds((S, D), F32), _sds((N_CHIPS, D, 512), F32), _sds((1, D), F32), _sds((1, D), F32)],
        compiler_params=_cp(58),
    )(*_hbm(dq, dk, dv, du, dpre1, xh0, rstd0, g0, b0, w_in_s))


def _place():
    return lax.axis_index("x"), lax.axis_index("y"), lax.axis_index("c")


CHIP_FLIPS = ((0, 1), (1, 0), (1, 1))


class _Rider:
    def __init__(self, ins, out_shapes, n_sem, phases, aliases=None):
        self.ins, self.out_shapes, self.n_sem, self.phases = list(ins), list(out_shapes), n_sem, phases
        self.aliases = aliases or {}

    def __add__(self, other):
        na, ma = len(self.ins), len(self.out_shapes)

        def phases(ins, outs, ssem, rsem):
            mine = self.phases(ins[:na], outs[:ma], ssem, rsem)
            rest = pl.ds(self.n_sem, other.n_sem)
            theirs = other.phases(ins[na:], outs[ma:], ssem.at[rest], rsem.at[rest])
            assert len(mine) == 1 and len(theirs) == 1
            return [mine[0] + theirs[0]]

        aliases = {**self.aliases, **{na + i: ma + o for i, o in other.aliases.items()}}
        return _Rider(self.ins + other.ins, self.out_shapes + other.out_shapes, self.n_sem + other.n_sem, phases,
                      aliases)

    def split(self, refs, n_in, n_out, n_scratch):
        a = n_in + len(self.ins)
        b = a + n_out
        c = b + len(self.out_shapes)
        own = (refs[:n_in], refs[a:b], refs[c:c + n_scratch])
        return own + ((refs[n_in:a], refs[b:c]) + tuple(refs[c + n_scratch:]),)

    def first(self, ride):
        for make in self.phases(*ride)[0]:
            make().start()

    def mid(self, ride):
        ph = self.phases(*ride)
        if len(ph) == 2:
            for make in ph[0]:
                make().wait_recv()
            for make in ph[1]:
                make().start()

    def last(self, ride):
        ph = self.phases(*ride)
        if len(ph) == 2:
            for make in ph[0]:
                make().wait_send()
        for make in ph[-1]:
            make().wait()

    def call(self, body, args, *, grid, name, in_specs, out_specs, out_shape, scratch_shapes, vmem_mib,
             prefetch=None):
        n_in, n_out = len(in_specs), len(out_specs)
        sems = [pltpu.SemaphoreType.DMA((self.n_sem,)), pltpu.SemaphoreType.DMA((self.n_sem,))]
        n_pre = 0 if prefetch is None else 1
        grid_spec = pltpu.PrefetchScalarGridSpec(
            num_scalar_prefetch=n_pre, grid=grid,
            in_specs=list(in_specs) + [HBM_SPEC] * len(self.ins),
            out_specs=list(out_specs) + [HBM_SPEC] * len(self.out_shapes),
            scratch_shapes=list(scratch_shapes) + sems)
        return pl.pallas_call(
            body, name=name, grid_spec=grid_spec,
            out_shape=list(out_shape) + self.out_shapes,
            input_output_aliases={n_pre + n_in + i: n_out + o for i, o in self.aliases.items()},
            compiler_params=_cp(vmem_mib),
        )(*([] if prefetch is None else [prefetch]), *_hbm(*args), *self.ins)

    def run(self, name):
        def body(*refs):
            ride = self.split(refs, 0, 0, 0)[3]
            self.first(ride)
            self.mid(ride)
            self.last(ride)

        return self.call(body, [], grid=(), name=name, in_specs=[], out_specs=[], out_shape=[],
                         scratch_shapes=[], vmem_mib=16)


def _remote(src, dst, ssem, rsem, n, dev):
    return functools.partial(pltpu.make_async_remote_copy, src_ref=src, dst_ref=dst, send_sem=ssem.at[n],
                             recv_sem=rsem.at[n], device_id=dev, device_id_type=MESH)


def _cast_into_slot(w, place, name):
    R, C = w.shape
    tr = min(R, 512)

    def body(pl_ref, w_ref, o_ref):
        o_ref[0] = w_ref[...].astype(BF16)

    return pl.pallas_call(
        body, name=name,
        grid_spec=pltpu.PrefetchScalarGridSpec(
            num_scalar_prefetch=1, grid=(R // tr,),
            in_specs=[pl.BlockSpec((tr, C), lambda r, pr: (r, 0))],
            out_specs=pl.BlockSpec((1, tr, C), lambda r, pr: (pr[1], r, 0))),
        out_shape=_sds((N_CHIPS, R, C), BF16),
    )(place, w)


CAST_STEPS = 8


def _cast_rest(ws, place, rider):
    n = len(ws)

    def body(pl_ref, *refs):
        w_refs, o_refs, _, ride = rider.split(refs, n, n, 0)
        r = pl.program_id(0)

        @pl.when(r == 0)
        def _():
            rider.first(ride)

        @pl.when(r == CAST_STEPS // 2)
        def _():
            rider.mid(ride)

        for w_ref, o_ref in zip(w_refs, o_refs):
            o_ref[0] = w_ref[...].astype(BF16)

        @pl.when(r == CAST_STEPS - 1)
        def _():
            rider.last(ride)

    def rows(w):
        return w.shape[0] // CAST_STEPS

    return rider.call(
        body, ws, grid=(CAST_STEPS,), name="cast_weights", prefetch=place,
        in_specs=[pl.BlockSpec((rows(w), w.shape[1]), lambda r, pr: (r, 0)) for w in ws],
        out_specs=[pl.BlockSpec((1, rows(w), w.shape[1]), lambda r, pr: (pr[1], r, 0)) for w in ws],
        out_shape=[_sds((N_CHIPS,) + w.shape, BF16) for w in ws], scratch_shapes=[], vmem_mib=32)


def _gather_rider(stacked, part="both"):
    n, nf = len(stacked), len(CHIP_FLIPS)

    def phases(ins, outs, ssem, rsem):
        x, y, c = _place()
        slot = 2 * x + y
        ici, d2d = [], []
        for w, (i_ref, o_ref) in enumerate(zip(ins, outs)):
            hh = o_ref.shape[1] // 2
            rows = pl.ds(c * hh, hh)
            for f, (fx, fy) in enumerate(CHIP_FLIPS):
                k = w * nf + f
                theirs = 2 * (x ^ fx) + (y ^ fy)
                if part != "pair":
                    ici.append(_remote(i_ref.at[slot, rows], o_ref.at[slot, rows], ssem, rsem, k,
                                       (x ^ fx, y ^ fy, c)))
                if part != "chips":
                    d2d.append(_remote(o_ref.at[theirs, rows], o_ref.at[theirs, rows], ssem, rsem,
                                       (n * nf if part == "both" else 0) + k, (x, y, 1 - c)))
        return [ph for ph in (ici, d2d) if ph]

    return _Rider(stacked, [_sds(s.shape, s.dtype) for s in stacked], (2 if part == "both" else 1) * n * nf,
                  phases, aliases={i: i for i in range(n)})


def _pair_swap_rider(grads):
    def phases(ins, outs, ssem, rsem):
        x, y, c = _place()
        return [[_remote(g.at[:, 1 - c], o, ssem, rsem, k, (x, y, 1 - c))
                 for k, (g, o) in enumerate(zip(ins, outs))]]

    return _Rider(grads, [_sds((N_CHIPS,) + g.shape[2:], g.dtype) for g in grads], len(grads), phases)


def _chip_scatter_rider(parts):
    nf = len(CHIP_FLIPS)

    def phases(ins, outs, ssem, rsem):
        x, y, c = _place()
        return [[_remote(r.at[2 * (x ^ fx) + (y ^ fy)], o.at[f], ssem, rsem, w * nf + f, (x ^ fx, y ^ fy, c))
                 for w, (r, o) in enumerate(zip(ins, outs)) for f, (fx, fy) in enumerate(CHIP_FLIPS)]]

    return _Rider(parts, [_sds((nf,) + r.shape[1:], r.dtype) for r in parts], len(parts) * nf, phases)


def _pair_send_rider(halves):
    def phases(ins, outs, ssem, rsem):
        x, y, c = _place()
        return [[_remote(h, o, ssem, rsem, k, (x, y, 1 - c)) for k, (h, o) in enumerate(zip(ins, outs))]]

    return _Rider(halves, [_sds(h.shape, h.dtype) for h in halves], len(halves), phases)


PAIR_SUM_STEPS = 2
CHIP_SUM_STEPS = 4
ADAMW_STEPS = 4


def _no_rider():
    return _Rider([], [], 1, lambda ins, outs, ssem, rsem: [[]])


def _add_pair(grads, recvs, place, name, rider):
    n = len(grads)

    def body(pl_ref, *refs):
        ins, outs, _, ride = rider.split(refs, 2 * n, 2 * n, 0)
        j, h = pl.program_id(0), pl.program_id(1)

        @pl.when(jnp.logical_and(j == 0, h == 0))
        def _():
            rider.first(ride)

        for w in range(n):
            s = ins[2 * w][:, 0] + ins[2 * w + 1][...]
            outs[2 * w][...] = s
            outs[2 * w + 1][...] = s.astype(BF16)

        @pl.when(jnp.logical_and(j == N_CHIPS - 1, h == PAIR_SUM_STEPS - 1))
        def _():
            rider.last(ride)

    in_specs, out_specs, out_shape, args = [], [], [], []
    for g, r in zip(grads, recvs):
        _, _, H, C = g.shape
        th = H // PAIR_SUM_STEPS
        spec = pl.BlockSpec((1, th, C), lambda j, h, pr: (j, h, 0))
        in_specs += [pl.BlockSpec((1, 1, th, C), lambda j, h, pr: (j, pr[0], h, 0)), spec]
        out_specs += [spec, spec]
        out_shape += [_sds((N_CHIPS, H, C), F32), _sds((N_CHIPS, H, C), BF16)]
        args += [g, r]
    res = rider.call(body, args, grid=(N_CHIPS, PAIR_SUM_STEPS), name=name, prefetch=place, in_specs=in_specs,
                     out_specs=out_specs, out_shape=out_shape, scratch_shapes=[], vmem_mib=32)
    return [(res[2 * w], res[2 * w + 1]) for w in range(n)], res[2 * n:]


def _add_chips(parts, recvs, place, name, rider):
    n = len(parts)

    def body(pl_ref, *refs):
        ins, outs, _, ride = rider.split(refs, 2 * n, n, 0)
        h = pl.program_id(0)

        @pl.when(h == 0)
        def _():
            rider.first(ride)

        for w in range(n):
            p_ref, r_ref = ins[2 * w], ins[2 * w + 1]
            outs[w][...] = p_ref[0] + r_ref[0].astype(F32) + r_ref[1].astype(F32) + r_ref[2].astype(F32)

        @pl.when(h == CHIP_SUM_STEPS - 1)
        def _():
            rider.last(ride)

    in_specs, out_specs, out_shape, args = [], [], [], []
    for p, r in zip(parts, recvs):
        _, H, C = p.shape
        th = H // CHIP_SUM_STEPS
        in_specs += [pl.BlockSpec((1, th, C), lambda h, pr: (pr[1], h, 0)),
                     pl.BlockSpec((len(CHIP_FLIPS), th, C), lambda h, pr: (0, h, 0))]
        out_specs.append(pl.BlockSpec((th, C), lambda h, pr: (h, 0)))
        out_shape.append(_sds((H, C), F32))
        args += [p, r]
    res = rider.call(body, args, grid=(CHIP_SUM_STEPS,), name=name, prefetch=place, in_specs=in_specs,
                     out_specs=out_specs, out_shape=out_shape, scratch_shapes=[], vmem_mib=32)
    return res[:n], res[n:]


def _adamw_math(w, g, m, v):
    m = ADAM_B1 * m + (1.0 - ADAM_B1) * g
    v = ADAM_B2 * v + (1.0 - ADAM_B2) * (g * g)
    m_hat = m / (1.0 - ADAM_B1 ** ADAM_STEP)
    v_hat = v / (1.0 - ADAM_B2 ** ADAM_STEP)
    delta = -ADAM_LR * (m_hat / (jnp.sqrt(v_hat) + ADAM_EPS) + ADAM_WD * w)
    return delta, m, v


def _adamw(ws, mines, theirs, ms, vs, place, name, rider):
    n = len(ws)

    def body(pl_ref, *refs):
        ins, outs, _, ride = rider.split(refs, 5 * n, 4 * n, 0)
        h, r = pl.program_id(0), pl.program_id(1)

        @pl.when(jnp.logical_and(h == 0, r == 0))
        def _():
            rider.first(ride)

        for k in range(n):
            w_ref, a_ref, b_ref, m_ref, v_ref = ins[5 * k:5 * k + 5]
            g = jnp.where(h == pl_ref[0], a_ref[...], b_ref[...])
            d, mo, vo = _adamw_math(w_ref[...], g, m_ref[...], v_ref[...])
            for o_ref, val in zip(outs[4 * k:4 * k + 4], (g, d, mo, vo)):
                o_ref[...] = val

        @pl.when(jnp.logical_and(h == 1, r == ADAMW_STEPS - 1))
        def _():
            rider.last(ride)

    in_specs, out_specs, out_shape, args = [], [], [], []
    for w, a, b, m, v in zip(ws, mines, theirs, ms, vs):
        R, C = w.shape
        th = (R // 2) // ADAMW_STEPS
        whole = pl.BlockSpec((th, C), lambda h, r, pr: (h * ADAMW_STEPS + r, 0))
        mine_spec = pl.BlockSpec((th, C), lambda h, r, pr: (jnp.where(h == pr[0], r, 0), 0))
        theirs_spec = pl.BlockSpec((th, C), lambda h, r, pr: (jnp.where(h == pr[0], 0, r), 0))
        in_specs += [whole, mine_spec, theirs_spec, whole, whole]
        out_specs += [whole] * 4
        out_shape += [_sds((R, C), F32)] * 4
        args += [w, a, b, m, v]
    res = rider.call(body, args, grid=(2, ADAMW_STEPS), name=name, prefetch=place, in_specs=in_specs,
                     out_specs=out_specs, out_shape=out_shape, scratch_shapes=[], vmem_mib=40)
    return [tuple(res[4 * k:4 * k + 4]) for k in range(n)], res[4 * n:]


DEVICE_FLIPS = tuple((fx, fy, fc) for fx in (0, 1) for fy in (0, 1) for fc in (0, 1))[1:]


def _pack_exchange_rider(pack):
    def phases(ins, outs, ssem, rsem):
        x, y, c = _place()
        mine = outs[0].at[4 * x + 2 * y + c]
        copies = [_remote(ins[0], mine, ssem, rsem, k, (x ^ fx, y ^ fy, c ^ fc))
                  for k, (fx, fy, fc) in enumerate(DEVICE_FLIPS)]
        copies.append(functools.partial(pltpu.make_async_copy, ins[0], mine, ssem.at[len(DEVICE_FLIPS)]))
        return [copies]

    return _Rider([pack], [_sds((N_DEV,) + pack.shape, pack.dtype)], len(DEVICE_FLIPS) + 1, phases)


def _small_sum_adamw(recv_a, recv_b, wpack, mpack, vpack):
    R = wpack.shape[0]

    def body(a_ref, b_ref, w_ref, m_ref, v_ref, gs_ref, d_ref, mo_ref, vo_ref):
        ta, tb = a_ref[0], b_ref[0]
        for dev in range(1, N_DEV):
            ta = ta + a_ref[dev]
            tb = tb + b_ref[dev]
        total = jnp.concatenate([ta, tb], axis=0)
        gs_ref[...] = total
        d, mo, vo = _adamw_math(w_ref[...], total, m_ref[...], v_ref[...])
        d_ref[...] = d
        mo_ref[...] = mo
        vo_ref[...] = vo

    return pl.pallas_call(
        body, name="small_sum_adamw", in_specs=[VMEM_SPEC] * 5, out_specs=[VMEM_SPEC] * 4,
        out_shape=[_sds((R, LANES), F32)] * 4,
    )(recv_a, recv_b, wpack, mpack, vpack)


def _rows8(a):
    a = a.reshape(-1, LANES)
    pad = (-a.shape[0]) % 8
    return jnp.pad(a, ((0, pad), (0, 0))) if pad else a


def _pack(parts):
    return jnp.concatenate([_rows8(a) for a in parts], axis=0)


def _unpack(pack, like):
    out, row = [], 0
    for a in like:
        n = a.size // LANES
        out.append(pack[row:row + n].reshape(a.shape))
        row += n + (-n) % 8
    return out


def kernel(x, p, emb_ln_g, emb_ln_b, w_in, attn_out_g, w_pool, pool_scale, w_out, ln1_g, ln1_b, w_up, w_down, ln2_g, ln2_b, w_ple, w_ple_gate, ln3_g, ln3_b, loss_target, m_emb_ln_g, m_emb_ln_b, m_w_in, m_attn_out_g, m_w_pool, m_pool_scale, m_w_out, m_ln1_g, m_ln1_b, m_w_up, m_w_down, m_ln2_g, m_ln2_b, m_w_ple, m_w_ple_gate, m_ln3_g, m_ln3_b, v_emb_ln_g, v_emb_ln_b, v_w_in, v_attn_out_g, v_w_pool, v_pool_scale, v_w_out, v_ln1_g, v_ln1_b, v_w_up, v_w_down, v_ln2_g, v_ln2_b, v_w_ple, v_w_ple_gate, v_ln3_g, v_ln3_b):
    S = x.shape[1]
    tq = min(256, S)
    tm_mlp = min(1024, S)
    tm_pool = min(1024, S)
    tm_ln = min(512, S)
    tm_fwd = min(1024, S)
    xs = x[0]
    ps = p[0, 0]
    tgt = loss_target[0]
    row = lambda a: a.reshape(1, -1)
    g0, b0 = row(emb_ln_g), row(emb_ln_b)
    g1, b1, g2, b2, g3, b3 = ln1_g, ln1_b, ln2_g, ln2_b, ln3_g, ln3_b
    wp = w_pool[0]

    xi, yi, ci = _place()
    place = jnp.stack([ci, 2 * xi + yi]).astype(jnp.int32)
    names = ["w_in", "w_out", "w_up", "w_down", "w_ple", "w_ple_gate"]

    big = [w_in[0], w_out[0], w_up[0], w_down[0], w_ple[0], w_ple_gate[0]]
    s_in = _cast_into_slot(big[0], place, "cast_w_in")
    s_out, s_up, s_down, s_ple, s_gate, w_in_s = _cast_rest(big[1:], place, _gather_rider([s_in]))

    xh0, rstd0, q, k, v, u, s_out, s_ple, s_gate = _embln_inproj(
        xs, g0, b0, w_in_s, tm_fwd, _gather_rider([s_out, s_ple, s_gate], "chips"))
    o_raw, on, s_up, s_down, w_out_s, w_ple_s, w_gate_s = _attn_fwd(
        q, k, v, attn_out_g, tq, _gather_rider([s_up, s_down], "chips") + _gather_rider([s_out, s_ple, s_gate], "pair"))
    w_out_f = w_out_s.reshape(D_MODEL, D_MODEL)
    w_gate_f = w_gate_s.reshape(D_MODEL, D_MODEL)
    d_b, pooled = _pool_fwd(u, wp, pool_scale, tm_pool)
    xh1, rstd1, x1b, w_up_s, w_down_s = _mix_ln1(on, pooled, xh0, g0, b0, w_out_f, g1, b1, tm_fwd,
                                                 _gather_rider([s_up, s_down], "pair"))
    w_down_f = w_down_s.reshape(D_FF, D_MODEL)
    xh2, rstd2, rb = _mlp_ln2(xh1, x1b, g1, b1, w_up_s, w_down_f, tm_mlp)

    (dpre2, dhb, dw_ple, dw_gate, dg3, db3, dg2, db2, loss_row) = _ple_ln3_loss(
        xh2, rstd2, g2, b2, ps, w_ple_s, w_gate_f, g3, b3, tgt, tm_ln)
    def halves_of(g):
        return g.reshape(N_CHIPS, 2, g.shape[1] // 2, g.shape[2])

    ple_halves = [halves_of(dw_ple), halves_of(dw_gate.reshape(N_CHIPS, D_MODEL // N_CHIPS, D_MODEL))]
    dx1m, da, *ple_pair = _mlp_bwd(rb, dhb, w_up_s, w_down_f, tm_mlp, _pair_swap_rider(ple_halves))
    (dw_up,) = _tn_matmul(x1b, da, "grad_w_up", 1024, S, True, _no_rider())
    up_halves = halves_of(dw_up)
    dw_down, up_pair = _tn_matmul(rb, dhb, "grad_w_down", 1024, S, False,
                                  _pair_swap_rider([up_halves]), square_a=True)
    down_halves = halves_of(dw_down.reshape(N_CHIPS, D_FF // N_CHIPS, D_MODEL))
    dpre1, don, dpooled, dw_out, dg1, db1, down_pair = _mix_bwd(
        dpre2, dx1m, xh1, rstd1, g1, w_out_f, on, pooled, tm_ln, _pair_swap_rider([down_halves]))
    early_sum, _ = _add_pair([up_halves, down_halves] + ple_halves, [up_pair, down_pair] + ple_pair, place,
                             "pair_sum_mlp_ple", _no_rider())
    out_halves = halves_of(dw_out.reshape(N_CHIPS, D_MODEL // N_CHIPS, D_MODEL))
    du, dwp, dsc, out_pair = _pool_bwd(dpooled, d_b, wp, pool_scale, tm_pool, _pair_swap_rider([out_halves]))
    (out_sum,), _ = _add_pair([out_halves], [out_pair], place, "pair_sum_w_out", _no_rider())
    pack_a = _pack([jnp.broadcast_to(loss_row, (8, LANES)), dwp, dsc, dg1, db1, dg2, db2, dg3, db3])
    early_sum = [out_sum] + early_sum
    riding = _chip_scatter_rider([b for _, b in early_sum]) + _pack_exchange_rider(pack_a)
    dq, dk, dv, dga, *arrived = _attn_bwd(q, k, v, don, o_raw, attn_out_g, tq, riding)
    early_chips, recv_a = arrived[:-1], arrived[-1]
    grad_x, dw_in, dg0, db0 = _inproj_bwd(dq, dk, dv, du, dpre1, xh0, rstd0, g0, b0, w_in_s, tm_ln)

    in_halves = halves_of(dw_in)
    pack_b = _pack([dg0, db0, dga])
    early_mine, (in_pair, recv_b) = _add_chips(
        [s for s, _ in early_sum], early_chips, place, "chip_sum_early",
        _pair_swap_rider([in_halves]) + _pack_exchange_rider(pack_b))
    (in_sum,), early_theirs = _add_pair([in_halves], [in_pair], place, "pair_sum_w_in", _pair_send_rider(early_mine))
    ms = [m_w_in, m_w_out, m_w_up, m_w_down, m_w_ple, m_w_ple_gate]
    vs = [v_w_in, v_w_out, v_w_up, v_w_down, v_w_ple, v_w_ple_gate]
    early_res, _ = _adamw(big[1:], early_mine, early_theirs, [m[0] for m in ms[1:]], [v[0] for v in vs[1:]],
                          place, "adamw_early", _no_rider())
    (in_chips,) = _chip_scatter_rider([in_sum[1]]).run("reduce_chips_late")
    (in_mine,), _ = _add_chips([in_sum[0]], [in_chips], place, "chip_sum_w_in", _no_rider())
    (in_theirs,) = _pair_send_rider([in_mine]).run("gather_pair_w_in")
    in_res, _ = _adamw(big[:1], [in_mine], [in_theirs], [ms[0][0]], [vs[0][0]], place, "adamw_w_in", _no_rider())
    big_out = {n: tuple(r.reshape(m.shape) for r in res4) for n, res4, m in zip(names, in_res + early_res, ms)}

    small_names = ["w_pool", "pool_scale", "ln1_g", "ln1_b", "ln2_g", "ln2_b", "ln3_g", "ln3_b",
                   "emb_ln_g", "emb_ln_b", "attn_out_g"]
    small_w = [w_pool, pool_scale, ln1_g, ln1_b, ln2_g, ln2_b, ln3_g, ln3_b, emb_ln_g, emb_ln_b, attn_out_g]
    small_m = [m_w_pool, m_pool_scale, m_ln1_g, m_ln1_b, m_ln2_g, m_ln2_b, m_ln3_g, m_ln3_b,
               m_emb_ln_g, m_emb_ln_b, m_attn_out_g]
    small_v = [v_w_pool, v_pool_scale, v_ln1_g, v_ln1_b, v_ln2_g, v_ln2_b, v_ln3_g, v_ln3_b,
               v_emb_ln_g, v_emb_ln_b, v_attn_out_g]
    loss_like = jnp.zeros((8, LANES), F32)
    gs, ds, mos, vos = _small_sum_adamw(recv_a, recv_b, _pack([loss_like] + small_w), _pack([loss_like] + small_m),
                                        _pack([jnp.ones((8, LANES), F32)] + small_v))
    like = [loss_like] + small_w
    gs_u, ds_u, mos_u, vos_u = (_unpack(a, like) for a in (gs, ds, mos, vos))
    loss = gs_u[0][0, 0]
    small_out = {n: (gs_u[i + 1], ds_u[i + 1], mos_u[i + 1], vos_u[i + 1]) for i, n in enumerate(small_names)}

    order = ["emb_ln_g", "emb_ln_b", "w_in", "attn_out_g", "w_pool", "pool_scale", "w_out", "ln1_g", "ln1_b",
             "w_up", "w_down", "ln2_g", "ln2_b", "w_ple", "w_ple_gate", "ln3_g", "ln3_b"]
    res = {**big_out, **small_out}
    outs = [loss, grad_x.reshape(x.shape)]
    for kind in range(4):
        outs += [res[n][kind] for n in order]
    return tuple(outs)
```

```python
import functools

import jax
import jax.numpy as jnp
from jax import lax
from jax.experimental import pallas as pl
from jax.experimental.pallas import tpu as pltpu

F32 = jnp.float32
BF16 = jnp.bfloat16

D_MODEL = 1024
ATTN_WIDTH = 512
POOL_WIDTH = 512
HEAD_DIM = 64
PAIR = 2 * HEAD_DIM
N_PAIRS = ATTN_WIDTH // PAIR
N_POOL_GROUPS = 4
POOL_GROUP = 128
POOL_HALO = 16
D_FF = 4096
PLE_DIM = 256
N_CHIPS = 4
N_DEV = 8
LN_EPS = 1e-5
RMS_EPS = 1e-6
ALPHA = float(2.0 ** 0.25)
Q_SCALE = 0.125
ADAM_LR = 0.001
ADAM_B1 = 0.9
ADAM_B2 = 0.999
ADAM_EPS = 1e-08
ADAM_WD = 0.01
ADAM_STEP = 10
LANES = 128
MIB = 1024 * 1024

MESH = pl.DeviceIdType.MESH
HBM_SPEC = pl.BlockSpec(memory_space=pltpu.HBM)
VMEM_SPEC = pl.BlockSpec(memory_space=pltpu.VMEM)


def _cp(vmem_mib):
    return pltpu.CompilerParams(vmem_limit_bytes=vmem_mib * MIB)


def _dot(a, b):
    return jnp.dot(a, b, preferred_element_type=F32)


def _dot_nt(a, b):
    return lax.dot_general(a, b, (((1,), (1,)), ((), ())), preferred_element_type=F32)


def _dot_tn(a, b):
    return lax.dot_general(a, b, (((0,), (0,)), ((), ())), preferred_element_type=F32)


def _ln_fwd(pre):
    mu = jnp.mean(pre, axis=-1, keepdims=True)
    xc = pre - mu
    var = jnp.mean(xc * xc, axis=-1, keepdims=True)
    rstd = lax.rsqrt(var + LN_EPS)
    return xc * rstd, rstd


def _ln_bwd(dy, xh, rstd, g):
    dxh = dy * g
    m1 = jnp.mean(dxh, axis=-1, keepdims=True)
    m2 = jnp.mean(dxh * xh, axis=-1, keepdims=True)
    return rstd * (dxh - m1 - xh * m2)


def _colsum(a):
    return jnp.sum(a, axis=0, keepdims=True)


def _neg_softplus(z):
    return -(jnp.maximum(z, 0.0) + jnp.log(1.0 + jnp.exp(-jnp.abs(z))))


def _row_spec(tm, n):
    return pl.BlockSpec((tm, n), lambda i: (i, 0))


def _const_spec(shape):
    nd = len(shape)
    return pl.BlockSpec(shape, lambda *_: (0,) * nd)


def _hbm(*arrays):
    return [pltpu.with_memory_space_constraint(a, pltpu.HBM) for a in arrays]


def _sds(shape, dtype):
    return pltpu.HBM(shape, dtype)


def _embln_inproj(x, g0, b0, w_in_s, tm, rider):
    S, D = x.shape
    n_t = S // tm

    def body(*refs):
        ((x_ref, g_ref, b_ref, w_ref), (xh_ref, rstd_ref, q_ref, k_ref, v_ref, u_ref), _,
         ride) = rider.split(refs, 4, 6, 0)
        i = pl.program_id(0)

        @pl.when(i == 0)
        def _():
            rider.first(ride)

        @pl.when(i == (3 * n_t) // 4)
        def _():
            rider.mid(ride)

        xh, rstd = _ln_fwd(x_ref[...])
        xh_ref[...] = xh
        rstd_ref[...] = rstd
        xb = (xh * g_ref[...] + b_ref[...]).astype(BF16)
        q_ref[...] = (_dot(xb, w_ref[0]) * Q_SCALE).astype(BF16)
        k_ref[...] = _dot(xb, w_ref[1]).astype(BF16)
        v_ref[...] = _dot(xb, w_ref[2]).astype(BF16)
        u_ref[...] = _dot(xb, w_ref[3])

        @pl.when(i == n_t - 1)
        def _():
            rider.last(ride)

    return rider.call(
        body, [x, g0, b0, w_in_s], grid=(n_t,), name="embln_inproj",
        in_specs=[_row_spec(tm, D), _const_spec((1, D)), _const_spec((1, D)),
                  _const_spec((N_CHIPS, D, 512))],
        out_specs=[_row_spec(tm, D), _row_spec(tm, 1), _row_spec(tm, 512), _row_spec(tm, 512),
                   _row_spec(tm, 512), _row_spec(tm, 512)],
        out_shape=[_sds((S, D), F32), _sds((S, 1), F32), _sds((S, 512), BF16), _sds((S, 512), BF16),
                   _sds((S, 512), BF16), _sds((S, 512), F32)],
        scratch_shapes=[], vmem_mib=56)


def _tri(n, upper):
    r = lax.broadcasted_iota(jnp.int32, (n, n), 0)
    c = lax.broadcasted_iota(jnp.int32, (n, n), 1)
    keep = (r < c) if upper else (r > c)
    return jnp.where(keep, 1.0, 0.0).astype(BF16)


def _strictly_causal(n):
    return lax.broadcasted_iota(jnp.int32, (n, n), 1) < lax.broadcasted_iota(jnp.int32, (n, n), 0)


LOG_WEIGHT_FLOOR = -110.0


def _weights_left(c_ls):
    return (jnp.max(jnp.maximum(c_ls[0], c_ls[1])) > LOG_WEIGHT_FLOOR).astype(jnp.int32)


def _sb_tile(qhs, kt, low, c_ls, valid):
    valids = valid if isinstance(valid, (list, tuple)) else [valid] * len(qhs)
    zs = [_dot_nt(qh, kt) for qh in qhs]
    lrs = [_neg_softplus(z) for z in zs]
    ls_ = [lr if m is None else jnp.where(m, lr, 0.0) for lr, m in zip(lrs, valids)]
    sfx = [_dot(l.astype(BF16), low) + c_l for l, c_l in zip(ls_, c_ls)]
    lss = [z + lr for z, lr in zip(zs, lrs)]
    ws = [jnp.exp(ls + s) for ls, s in zip(lss, sfx)]
    ws = [w if m is None else jnp.where(m, w, 0.0) for w, m in zip(ws, valids)]
    return lss, ls_, ws


def _attn_fwd(q, k, v, ga, tq, rider):
    S = q.shape[0]
    nq = S // tq

    def body(*refs):
        (q_ref, k_ref, v_ref, ga_ref), (o_ref, on_ref), (low_s,), ride = rider.split(refs, 4, 2, 1)
        p, i = pl.program_id(0), pl.program_id(1)

        @pl.when(jnp.logical_and(p == 0, i == 0))
        def _():
            rider.first(ride)
            low_s[...] = _tri(tq, upper=False)

        @pl.when(jnp.logical_and(p == N_PAIRS - 1, i == 0))
        def _():
            rider.mid(ride)

        lane = lax.broadcasted_iota(jnp.int32, (1, PAIR), 1)
        m0 = lane < HEAD_DIM
        low = low_s[...]
        q2 = q_ref[...]
        qhs = [jnp.where(m0, q2, jnp.zeros_like(q2)), jnp.where(m0, jnp.zeros_like(q2), q2)]

        def tile(kb, c_ls, accs, valid):
            ks = pl.multiple_of(kb * tq, tq)
            kt = k_ref[pl.ds(ks, tq), :]
            vt = v_ref[pl.ds(ks, tq), :]
            _, ls_, ws = _sb_tile(qhs, kt, low, c_ls, valid)
            new_a = [acc + _dot(w.astype(BF16), vt) for acc, w in zip(accs, ws)]
            new_c = [c_l + jnp.sum(l, axis=1, keepdims=True) for c_l, l in zip(c_ls, ls_)]
            return new_c, new_a

        zc, za = jnp.zeros((tq, 1), F32), jnp.zeros((tq, PAIR), F32)

        def first_two():
            c_ls, accs = tile(i, [zc, zc], [za, za], _strictly_causal(tq))
            c_ls, accs = tile(i - 1, c_ls, accs, None)
            return (_weights_left(c_ls), *c_ls, *accs)

        def first_one():
            c_ls, accs = tile(i, [zc, zc], [za, za], _strictly_causal(tq))
            return (jnp.int32(0), *c_ls, *accs)

        st0 = lax.cond(i >= 1, first_two, first_one)

        def more(st):
            return jnp.logical_and(st[0] <= i, st[1] > 0)

        def step(st):
            n, _, c0, c1, a0, a1 = st
            c_ls, accs = tile(i - n, [c0, c1], [a0, a1], None)
            return (n + 1, _weights_left(c_ls), c_ls[0], c_ls[1], accs[0], accs[1])

        st = lax.while_loop(more, step, (jnp.int32(2), *st0))
        o = jnp.where(m0, st[4], st[5])
        o_ref[...] = o
        sq = o * o
        ms0 = jnp.sum(jnp.where(m0, sq, 0.0), axis=-1, keepdims=True) * (1.0 / HEAD_DIM)
        ms1 = jnp.sum(jnp.where(m0, 0.0, sq), axis=-1, keepdims=True) * (1.0 / HEAD_DIM)
        rs = jnp.where(m0, lax.rsqrt(ms0 + RMS_EPS), lax.rsqrt(ms1 + RMS_EPS))
        on_ref[...] = (o * rs * ga_ref[...]).astype(BF16)

        @pl.when(jnp.logical_and(p == N_PAIRS - 1, i == nq - 1))
        def _():
            rider.last(ride)

    return rider.call(
        body, [q, k, v, ga], grid=(N_PAIRS, nq), name="attn_fwd",
        in_specs=[pl.BlockSpec((tq, PAIR), lambda p, i: (i, p)),
                  pl.BlockSpec((S, PAIR), lambda p, i: (0, p)),
                  pl.BlockSpec((S, PAIR), lambda p, i: (0, p)),
                  pl.BlockSpec((1, PAIR), lambda p, i: (0, p))],
        out_specs=[pl.BlockSpec((tq, PAIR), lambda p, i: (i, p)),
                   pl.BlockSpec((tq, PAIR), lambda p, i: (i, p))],
        out_shape=[_sds((S, ATTN_WIDTH), F32), _sds((S, ATTN_WIDTH), BF16)],
        scratch_shapes=[pltpu.VMEM((tq, tq), BF16)], vmem_mib=40)


def _pool_fwd(u, w_pool, pscale, tm):
    S = u.shape[0]
    hb = tm // POOL_HALO

    def body(u_ref, uh_ref, wp_ref, sc_ref, d_ref, pooled_ref):
        i = pl.program_id(0)
        halo = jnp.where(i > 0, uh_ref[...], 0.0)
        pos = i * tm + lax.broadcasted_iota(jnp.int32, (tm, 1), 0)
        for g in range(N_POOL_GROUPS):
            win = 2 ** (g + 1)
            cols = slice(g * POOL_GROUP, (g + 1) * POOL_GROUP)
            ut = u_ref[:, cols]
            s = jnp.concatenate([halo[:, cols], ut], axis=0)
            for sh in (1, 2, 4, 8)[:g + 1]:
                s = s + pltpu.roll(s, sh, 0)
            cnt = jnp.minimum(pos + 1, win).astype(F32)
            db = (s[POOL_HALO:, :] / cnt - ut).astype(BF16)
            y = _dot(db, wp_ref[g].astype(BF16))
            d_ref[:, cols] = db
            pooled_ref[:, cols] = (y * sc_ref[:, cols]).astype(BF16)

    return pl.pallas_call(
        body, grid=(S // tm,), name="pool_fwd",
        in_specs=[_row_spec(tm, POOL_WIDTH),
                  pl.BlockSpec((POOL_HALO, POOL_WIDTH), lambda i: (jnp.maximum(i * hb - 1, 0), 0)),
                  _const_spec((N_POOL_GROUPS, POOL_GROUP, POOL_GROUP)), _const_spec((1, POOL_WIDTH))],
        out_specs=[_row_spec(tm, POOL_WIDTH), _row_spec(tm, POOL_WIDTH)],
        out_shape=[_sds((S, POOL_WIDTH), BF16), _sds((S, POOL_WIDTH), BF16)],
        compiler_params=_cp(32),
    )(*_hbm(u, u, w_pool, pscale))


def _mix_ln1(on, pooled, xh0, g0, b0, w_out, g1, b1, tm, rider):
    S, D = xh0.shape
    n_t = S // tm

    def body(*refs):
        ((on_ref, po_ref, xh0_ref, g0_ref, b0_ref, w_ref, g1_ref, b1_ref), (xh_ref, rstd_ref, xb_ref), _,
         ride) = rider.split(refs, 8, 3, 0)

        @pl.when(pl.program_id(0) == 0)
        def _():
            rider.first(ride)

        mixed = _dot(on_ref[...], w_ref[:ATTN_WIDTH, :]) + _dot(po_ref[...], w_ref[ATTN_WIDTH:, :])
        x0 = xh0_ref[...] * g0_ref[...] + b0_ref[...]
        xh, rstd = _ln_fwd(ALPHA * x0 + mixed)
        xh_ref[...] = xh
        rstd_ref[...] = rstd
        xb_ref[...] = (xh * g1_ref[...] + b1_ref[...]).astype(BF16)

        @pl.when(pl.program_id(0) == n_t - 1)
        def _():
            rider.last(ride)

    return rider.call(
        body, [on, pooled, xh0, g0, b0, w_out, g1, b1], grid=(n_t,), name="mix_ln1",
        in_specs=[_row_spec(tm, ATTN_WIDTH), _row_spec(tm, POOL_WIDTH), _row_spec(tm, D),
                  _const_spec((1, D)), _const_spec((1, D)), _const_spec((D, D)),
                  _const_spec((1, D)), _const_spec((1, D))],
        out_specs=[_row_spec(tm, D), _row_spec(tm, 1), _row_spec(tm, D)],
        out_shape=[_sds((S, D), F32), _sds((S, 1), F32), _sds((S, D), BF16)],
        scratch_shapes=[], vmem_mib=56)


def _mlp_ln2(xh1, x1b, g1, b1, w_up_s, w_down, tm, rider):
    S, D = xh1.shape
    fc = D_FF // N_CHIPS
    n_t = S // tm

    def body(*refs):
        ((xh_ref, xb_ref, g_ref, b_ref, wu_ref, wd_ref), (xh2_ref, rstd_ref, r_ref), (acc_ref,),
         ride) = rider.split(refs, 6, 3, 1)
        i, j = pl.program_id(0), pl.program_id(1)

        @pl.when(jnp.logical_and(i == 0, j == 0))
        def _():
            rider.first(ride)

        @pl.when(j == 0)
        def _():
            acc_ref[...] = jnp.zeros_like(acc_ref)

        r = jnp.maximum(_dot(xb_ref[...], wu_ref[0]), 0.0)
        r_ref[...] = r.astype(BF16)
        acc_ref[...] += _dot((r * r).astype(BF16), wd_ref[...])

        @pl.when(j == N_CHIPS - 1)
        def _():
            x1 = xh_ref[...] * g_ref[...] + b_ref[...]
            xh, rstd = _ln_fwd(ALPHA * x1 + acc_ref[...])
            xh2_ref[...] = xh
            rstd_ref[...] = rstd

        @pl.when(jnp.logical_and(i == n_t - 1, j == N_CHIPS - 1))
        def _():
            rider.last(ride)

    return rider.call(
        body, [xh1, x1b, g1, b1, w_up_s, w_down], grid=(n_t, N_CHIPS), name="mlp_ln2",
        in_specs=[pl.BlockSpec((tm, D), lambda i, j: (i, 0)), pl.BlockSpec((tm, D), lambda i, j: (i, 0)),
                  pl.BlockSpec((1, D), lambda i, j: (0, 0)), pl.BlockSpec((1, D), lambda i, j: (0, 0)),
                  pl.BlockSpec((1, D, fc), lambda i, j: (j, 0, 0)),
                  pl.BlockSpec((fc, D), lambda i, j: (j, 0))],
        out_specs=[pl.BlockSpec((tm, D), lambda i, j: (i, 0)), pl.BlockSpec((tm, 1), lambda i, j: (i, 0)),
                   pl.BlockSpec((tm, fc), lambda i, j: (i, j))],
        out_shape=[_sds((S, D), F32), _sds((S, 1), F32), _sds((S, D_FF), BF16)],
        scratch_shapes=[pltpu.VMEM((tm, D), F32)], vmem_mib=56)


def _ple_ln3_loss(xh2, rstd2, g2, b2, p, w_ple_s, w_gate, g3, b3, target, tm):
    S, D = xh2.shape
    pc = D // N_CHIPS

    def body(xh2_ref, rstd2_ref, g2_ref, b2_ref, p_ref, wp_ref, wg_ref, g3_ref, b3_ref, t_ref,
             dpre2_ref, dhb_ref, dwp_ref, dwg_ref, dg3_ref, db3_ref, dg2_ref, db2_ref, loss_ref):
        i = pl.program_id(0)

        @pl.when(i == 0)
        def _():
            for r in (dwp_ref, dwg_ref, dg3_ref, db3_ref, dg2_ref, db2_ref, loss_ref):
                r[...] = jnp.zeros_like(r)

        xh2 = xh2_ref[...]
        x2 = xh2 * g2_ref[...] + b2_ref[...]
        x2b = x2.astype(BF16)
        gate = 1.0 / (1.0 + jnp.exp(-_dot(x2b, wg_ref[...])))
        pb = p_ref[...].astype(BF16)
        pe = jnp.concatenate([_dot(pb, wp_ref[c]) for c in range(N_CHIPS)], axis=1)
        xh3, rstd3 = _ln_fwd(ALPHA * x2 + pe * gate)
        diff = xh3 * g3_ref[...] + b3_ref[...] - t_ref[...]
        loss_ref[...] += (0.5 / D) * jnp.sum(diff * diff)
        dy = diff * (1.0 / D)
        dg3_ref[...] += _colsum(dy * xh3)
        db3_ref[...] += _colsum(dy)
        dpre3 = _ln_bwd(dy, xh3, rstd3, g3_ref[...])
        dpe_b = (dpre3 * gate).astype(BF16)
        dgp_b = (dpre3 * pe * gate * (1.0 - gate)).astype(BF16)
        dx2 = ALPHA * dpre3 + _dot_nt(dgp_b, wg_ref[...])
        dwg_ref[...] += _dot_tn(x2b, dgp_b)
        for c in range(N_CHIPS):
            dwp_ref[c] += _dot_tn(pb, dpe_b[:, c * pc:(c + 1) * pc])
        dg2_ref[...] += _colsum(dx2 * xh2)
        db2_ref[...] += _colsum(dx2)
        dpre2 = _ln_bwd(dx2, xh2, rstd2_ref[...], g2_ref[...])
        dpre2_ref[...] = dpre2
        dhb_ref[...] = dpre2.astype(BF16)

    vec = _const_spec((1, D))
    return pl.pallas_call(
        body, grid=(S // tm,), name="ple_ln3_loss",
        in_specs=[_row_spec(tm, D), _row_spec(tm, 1), vec, vec, _row_spec(tm, PLE_DIM),
                  _const_spec((N_CHIPS, PLE_DIM, pc)), _const_spec((D, D)), vec, vec, _row_spec(tm, D)],
        out_specs=[_row_spec(tm, D), _row_spec(tm, D), _const_spec((N_CHIPS, PLE_DIM, pc)),
                   _const_spec((D, D)), vec, vec, vec, vec, _const_spec((1, LANES))],
        out_shape=[_sds((S, D), F32), _sds((S, D), BF16), _sds((N_CHIPS, PLE_DIM, pc), F32),
                   _sds((D, D), F32), _sds((1, D), F32), _sds((1, D), F32), _sds((1, D), F32),
                   _sds((1, D), F32), _sds((1, LANES), F32)],
        compiler_params=_cp(58),
    )(*_hbm(xh2, rstd2, g2, b2, p, w_ple_s, w_gate, g3, b3, target))


def _mlp_bwd(rb, dhb, w_up_s, w_down, tm, rider):
    S, D = dhb.shape
    fc = D_FF // N_CHIPS
    n_t = S // tm

    def body(*refs):
        (r_ref, dh_ref, wu_ref, wd_ref), (dx_ref, da_ref), _, ride = rider.split(refs, 4, 2, 0)
        i, j = pl.program_id(0), pl.program_id(1)

        @pl.when(jnp.logical_and(i == 0, j == 0))
        def _():
            rider.first(ride)

        @pl.when(j == 0)
        def _():
            dx_ref[...] = jnp.zeros_like(dx_ref)

        da = (_dot_nt(dh_ref[...], wd_ref[...]) * (2.0 * r_ref[...].astype(F32))).astype(BF16)
        da_ref[...] = da
        dx_ref[...] += _dot_nt(da, wu_ref[0])

        @pl.when(jnp.logical_and(i == n_t - 1, j == N_CHIPS - 1))
        def _():
            rider.last(ride)

    return rider.call(
        body, [rb, dhb, w_up_s, w_down], grid=(n_t, N_CHIPS), name="mlp_bwd",
        in_specs=[pl.BlockSpec((tm, fc), lambda i, j: (i, j)), pl.BlockSpec((tm, D), lambda i, j: (i, 0)),
                  pl.BlockSpec((1, D, fc), lambda i, j: (j, 0, 0)),
                  pl.BlockSpec((fc, D), lambda i, j: (j, 0))],
        out_specs=[pl.BlockSpec((tm, D), lambda i, j: (i, 0)), pl.BlockSpec((tm, fc), lambda i, j: (i, j))],
        out_shape=[_sds((S, D), F32), _sds((S, D_FF), BF16)],
        scratch_shapes=[], vmem_mib=56)


def _tn_matmul(a, b, name, tk, tt, stacked, rider, square_a=False):
    T, K = a.shape
    N = b.shape[1]
    tn = 1024
    grid = (K // tk, N // tn, T // tt)

    def body(*refs):
        (a_ref, b_ref), (o_ref,), _, ride = rider.split(refs, 2, 1, 0)
        at = [pl.program_id(d) for d in range(3)]

        @pl.when(jnp.logical_and(jnp.logical_and(at[0] == 0, at[1] == 0), at[2] == 0))
        def _():
            rider.first(ride)

        @pl.when(at[2] == 0)
        def _():
            o_ref[...] = jnp.zeros_like(o_ref)

        a_t = a_ref[...]
        if square_a:
            a_t = a_t * a_t
        prod = _dot_tn(a_t, b_ref[...])
        if stacked:
            o_ref[0] += prod
        else:
            o_ref[...] += prod

        @pl.when(jnp.logical_and(jnp.logical_and(at[0] == grid[0] - 1, at[1] == grid[1] - 1),
                                 at[2] == grid[2] - 1))
        def _():
            rider.last(ride)

    if stacked:
        out_spec = pl.BlockSpec((1, tk, tn), lambda k, n, t: (n, k, 0))
        out_shape = _sds((N // tn, K, tn), F32)
    else:
        out_spec = pl.BlockSpec((tk, tn), lambda k, n, t: (k, n))
        out_shape = _sds((K, N), F32)
    return rider.call(
        body, [a, b], grid=grid, name=name,
        in_specs=[pl.BlockSpec((tt, tk), lambda k, n, t: (t, k)),
                  pl.BlockSpec((tt, tn), lambda k, n, t: (t, n))],
        out_specs=[out_spec], out_shape=[out_shape], scratch_shapes=[], vmem_mib=58)


def _mix_bwd(dpre2, dx1m, xh1, rstd1, g1, w_out, on, pooled, tm, rider):
    S, D = xh1.shape
    n_t = S // tm

    def body(*refs):
        ((dp2_ref, dxm_ref, xh_ref, rstd_ref, g_ref, w_ref, on_ref, po_ref),
         (dpre1_ref, don_ref, dpo_ref, dw_ref, dg_ref, db_ref), _, ride) = rider.split(refs, 8, 6, 0)

        @pl.when(pl.program_id(0) == 0)
        def _():
            rider.first(ride)
            for r in (dw_ref, dg_ref, db_ref):
                r[...] = jnp.zeros_like(r)

        xh = xh_ref[...]
        dx1 = ALPHA * dp2_ref[...] + dxm_ref[...]
        dg_ref[...] += _colsum(dx1 * xh)
        db_ref[...] += _colsum(dx1)
        dpre1 = _ln_bwd(dx1, xh, rstd_ref[...], g_ref[...])
        dpre1_ref[...] = dpre1
        dmb = dpre1.astype(BF16)
        dcat = _dot_nt(dmb, w_ref[...])
        don_ref[...] = dcat[:, :ATTN_WIDTH]
        dpo_ref[...] = dcat[:, ATTN_WIDTH:]
        dw_ref[:ATTN_WIDTH, :] += _dot_tn(on_ref[...], dmb)
        dw_ref[ATTN_WIDTH:, :] += _dot_tn(po_ref[...], dmb)

        @pl.when(pl.program_id(0) == n_t - 1)
        def _():
            rider.last(ride)

    vec = _const_spec((1, D))
    return rider.call(
        body, [dpre2, dx1m, xh1, rstd1, g1, w_out, on, pooled], grid=(n_t,), name="mix_bwd",
        in_specs=[_row_spec(tm, D), _row_spec(tm, D), _row_spec(tm, D), _row_spec(tm, 1), vec,
                  _const_spec((D, D)), _row_spec(tm, ATTN_WIDTH), _row_spec(tm, POOL_WIDTH)],
        out_specs=[_row_spec(tm, D), _row_spec(tm, ATTN_WIDTH), _row_spec(tm, POOL_WIDTH),
                   _const_spec((D, D)), vec, vec],
        out_shape=[_sds((S, D), F32), _sds((S, ATTN_WIDTH), F32), _sds((S, POOL_WIDTH), F32),
                   _sds((D, D), F32), _sds((1, D), F32), _sds((1, D), F32)],
        scratch_shapes=[], vmem_mib=56)


def _pool_bwd(dpooled, d_b, w_pool, pscale, tm, rider):
    S = dpooled.shape[0]
    hb = tm // POOL_HALO
    n_t = S // tm
    te = tm + POOL_HALO

    def body(*refs):
        ((dp_ref, dph_ref, d_ref, wp_ref, sc_ref), (du_ref, dwp_ref, dsc_ref), _,
         ride) = rider.split(refs, 5, 3, 0)
        i = pl.program_id(0)

        @pl.when(i == 0)
        def _():
            rider.first(ride)
            dwp_ref[...] = jnp.zeros_like(dwp_ref)
            dsc_ref[...] = jnp.zeros_like(dsc_ref)

        halo = jnp.where(i < n_t - 1, dph_ref[...], 0.0)
        pos = i * tm + lax.broadcasted_iota(jnp.int32, (te, 1), 0)
        for g in range(N_POOL_GROUPS):
            win = 2 ** (g + 1)
            cols = slice(g * POOL_GROUP, (g + 1) * POOL_GROUP)
            wpb = wp_ref[g].astype(BF16)
            dpt = dp_ref[:, cols]
            dpe = jnp.concatenate([dpt, halo[:, cols]], axis=0)
            dyb = (dpe * sc_ref[:, cols]).astype(BF16)
            dd = _dot_nt(dyb, wpb)
            s = dd / jnp.minimum(pos + 1, win).astype(F32)
            for sh in (1, 2, 4, 8)[:g + 1]:
                s = s + pltpu.roll(s, te - sh, 0)
            du_ref[:, cols] = s[:tm, :] - dd[:tm, :]
            db = d_ref[:, cols]
            dwp_ref[g] += _dot_tn(db, dyb[:tm, :])
            dsc_ref[:, cols] += _colsum(dpt * _dot(db, wpb))

        @pl.when(i == n_t - 1)
        def _():
            rider.last(ride)

    return rider.call(
        body, [dpooled, dpooled, d_b, w_pool, pscale], grid=(n_t,), name="pool_bwd",
        in_specs=[_row_spec(tm, POOL_WIDTH),
                  pl.BlockSpec((POOL_HALO, POOL_WIDTH),
                               lambda i: (jnp.minimum((i + 1) * hb, S // POOL_HALO - 1), 0)),
                  _row_spec(tm, POOL_WIDTH),
                  _const_spec((N_POOL_GROUPS, POOL_GROUP, POOL_GROUP)), _const_spec((1, POOL_WIDTH))],
        out_specs=[_row_spec(tm, POOL_WIDTH), _const_spec((N_POOL_GROUPS, POOL_GROUP, POOL_GROUP)),
                   _const_spec((1, POOL_WIDTH))],
        out_shape=[_sds((S, POOL_WIDTH), F32), _sds((N_POOL_GROUPS, POOL_GROUP, POOL_GROUP), F32),
                   _sds((1, POOL_WIDTH), F32)],
        scratch_shapes=[], vmem_mib=32)


def _attn_bwd(q, k, v, don, o_raw, ga, tq, rider):
    S = q.shape[0]
    nq = S // tq

    def body(*refs):
        ((q_ref, k_ref, v_ref, don_ref, o_ref, ga_ref), (dq_ref, dk_ref, dv_ref, dga_ref),
         (g_s, b_s, low_s, upp_s), ride) = rider.split(refs, 6, 4, 4)
        p, i = pl.program_id(0), pl.program_id(1)

        @pl.when(jnp.logical_and(p == 0, i == 0))
        def _():
            rider.first(ride)
            low_s[...] = _tri(tq, upper=False)
            upp_s[...] = _tri(tq, upper=True)

        @pl.when(i == 0)
        def _():
            for r in (dk_ref, dv_ref, dga_ref):
                r[...] = jnp.zeros_like(r)

        lane = lax.broadcasted_iota(jnp.int32, (1, PAIR), 1)
        m0 = lane < HEAD_DIM
        low = low_s[...]
        upp = upp_s[...]

        def seg_mean(a):
            s0 = jnp.sum(jnp.where(m0, a, 0.0), axis=-1, keepdims=True)
            s1 = jnp.sum(jnp.where(m0, 0.0, a), axis=-1, keepdims=True)
            return jnp.where(m0, s0, s1) * (1.0 / HEAD_DIM)

        o = o_ref[...]
        rs = lax.rsqrt(seg_mean(o * o) + RMS_EPS)
        oh = o * rs
        don = don_ref[...]
        dga_ref[...] += _colsum(don * oh)
        doh = don * ga_ref[...]
        do = rs * (doh - oh * seg_mean(doh * oh))
        dob = do.astype(BF16)
        q2 = q_ref[...]
        qhs = [jnp.where(m0, q2, jnp.zeros_like(q2)), jnp.where(m0, jnp.zeros_like(q2), q2)]
        dhs = [jnp.where(m0, dob, jnp.zeros_like(dob)), jnp.where(m0, jnp.zeros_like(dob), dob)]
        causal = _strictly_causal(tq)

        def down(kb, c_ls, valid):
            ks = pl.multiple_of(kb * tq, tq)
            kt = k_ref[pl.ds(ks, tq), :]
            vt = v_ref[pl.ds(ks, tq), :]
            lss, ls_, ws = _sb_tile(qhs, kt, low, c_ls, valid)
            dws = [_dot_nt(dh, vt) for dh in dhs]
            for hh in range(2):
                g_s[hh, kb] = dws[hh] * ws[hh]
                b_s[hh, kb] = jnp.exp(lss[hh])
            dv_ref[pl.ds(ks, tq), :] += (_dot_tn(ws[0].astype(BF16), dhs[0])
                                         + _dot_tn(ws[1].astype(BF16), dhs[1]))
            return [c_l + jnp.sum(l, axis=1, keepdims=True) for c_l, l in zip(c_ls, ls_)]

        zc, za = jnp.zeros((tq, 1), F32), jnp.zeros((tq, PAIR), F32)
        def first_two():
            c_ls = down(i - 1, down(i, [zc, zc], causal), None)
            return (_weights_left(c_ls), *c_ls)

        st0 = lax.cond(i >= 1, first_two, lambda: (jnp.int32(0), *down(i, [zc, zc], causal)))

        def more(st):
            return jnp.logical_and(st[0] <= i, st[1] > 0)

        def down_step(st):
            c_ls = down(i - st[0], [st[2], st[3]], None)
            return (st[0] + 1, _weights_left(c_ls), c_ls[0], c_ls[1])

        n_tiles = lax.while_loop(more, down_step, (jnp.int32(2), *st0))[0]

        def up(kb, c_gs, accs, valid):
            ks = pl.multiple_of(kb * tq, tq)
            kt = k_ref[pl.ds(ks, tq), :]
            gs = [g_s[hh, kb] for hh in range(2)]
            pres = [_dot(g.astype(BF16), upp) + c_g for g, c_g in zip(gs, c_gs)]
            dzs = []
            for hh in range(2):
                beta = b_s[hh, kb]
                dz = gs[hh] - beta * (gs[hh] + pres[hh])
                if valid is not None:
                    dz = jnp.where(valid, dz, 0.0)
                dzs.append(dz.astype(BF16))
            new_a = [acc + _dot(dzb, kt) for acc, dzb in zip(accs, dzs)]
            dk_ref[pl.ds(ks, tq), :] += _dot_tn(dzs[0], qhs[0]) + _dot_tn(dzs[1], qhs[1])
            new_c = [c_g + jnp.sum(g, axis=1, keepdims=True) for c_g, g in zip(c_gs, gs)]
            return new_c, new_a

        def up_step(kb, st):
            c_gs, accs = up(kb, [st[0], st[1]], [st[2], st[3]], None)
            return (c_gs[0], c_gs[1], accs[0], accs[1])

        st = lax.fori_loop(i - n_tiles + 1, i - 1, up_step, (zc, zc, za, za))

        def last_two():
            c_gs, accs = up(i - 1, [st[0], st[1]], [st[2], st[3]], None)
            return tuple(up(i, c_gs, accs, causal)[1])

        accs = lax.cond(i >= 1, last_two, lambda: tuple(up(i, [zc, zc], [za, za], causal)[1]))
        dq_ref[...] = jnp.where(m0, accs[0], accs[1]) * Q_SCALE

        @pl.when(jnp.logical_and(p == N_PAIRS - 1, i == nq - 1))
        def _():
            rider.last(ride)

    return rider.call(
        body, [q, k, v, don, o_raw, ga], grid=(N_PAIRS, nq), name="attn_bwd",
        in_specs=[pl.BlockSpec((tq, PAIR), lambda p, i: (i, p)),
                  pl.BlockSpec((S, PAIR), lambda p, i: (0, p)),
                  pl.BlockSpec((S, PAIR), lambda p, i: (0, p)),
                  pl.BlockSpec((tq, PAIR), lambda p, i: (i, p)),
                  pl.BlockSpec((tq, PAIR), lambda p, i: (i, p)),
                  pl.BlockSpec((1, PAIR), lambda p, i: (0, p))],
        out_specs=[pl.BlockSpec((tq, PAIR), lambda p, i: (i, p)),
                   pl.BlockSpec((S, PAIR), lambda p, i: (0, p)),
                   pl.BlockSpec((S, PAIR), lambda p, i: (0, p)),
                   pl.BlockSpec((1, PAIR), lambda p, i: (0, p))],
        out_shape=[_sds((S, ATTN_WIDTH), F32), _sds((S, ATTN_WIDTH), F32), _sds((S, ATTN_WIDTH), F32),
                   _sds((1, ATTN_WIDTH), F32)],
        scratch_shapes=[pltpu.VMEM((2, nq, tq, tq), F32), pltpu.VMEM((2, nq, tq, tq), F32),
                        pltpu.VMEM((tq, tq), BF16), pltpu.VMEM((tq, tq), BF16)],
        vmem_mib=56)


def _inproj_bwd(dq, dk, dv, du, dpre1, xh0, rstd0, g0, b0, w_in_s, tm):
    S, D = xh0.shape

    def body(dq_ref, dk_ref, dv_ref, du_ref, dp1_ref, xh_ref, rstd_ref, g_ref, b_ref, w_ref,
             gx_ref, dw_ref, dg_ref, db_ref):
        @pl.when(pl.program_id(0) == 0)
        def _():
            for r in (dw_ref, dg_ref, db_ref):
                r[...] = jnp.zeros_like(r)

        xh = xh_ref[...]
        xb = (xh * g_ref[...] + b_ref[...]).astype(BF16)
        dx0 = ALPHA * dp1_ref[...]
        for c, r in enumerate((dq_ref, dk_ref, dv_ref, du_ref)):
            dpb = r[...].astype(BF16)
            dx0 = dx0 + _dot_nt(dpb, w_ref[c])
            dw_ref[c] += _dot_tn(xb, dpb)
        dg_ref[...] += _colsum(dx0 * xh)
        db_ref[...] += _colsum(dx0)
        gx_ref[...] = _ln_bwd(dx0, xh, rstd_ref[...], g_ref[...])

    vec = _const_spec((1, D))
    half = _row_spec(tm, 512)
    return pl.pallas_call(
        body, grid=(S // tm,), name="inproj_bwd",
        in_specs=[half, half, half, half, _row_spec(tm, D), _row_spec(tm, D), _row_spec(tm, 1), vec, vec,
                  _const_spec((N_CHIPS, D, 512))],
        out_specs=[_row_spec(tm, D), _const_spec((N_CHIPS, D, 512)), vec, vec],
        out_shape=[_sds((S, D), F32), _sds((N_CHIPS, D, 512), F32), _sds((1, D), F32), _sds((1, D), F32)],
        compiler_params=_cp(58),
    )(*_hbm(dq, dk, dv, du, dpre1, xh0, rstd0, g0, b0, w_in_s))


def _place():
    return lax.axis_index("x"), lax.axis_index("y"), lax.axis_index("c")


CHIP_FLIPS = ((0, 1), (1, 0), (1, 1))


class _Rider:
    def __init__(self, ins, out_shapes, n_sem, phases, aliases=None):
        self.ins, self.out_shapes, self.n_sem, self.phases = list(ins), list(out_shapes), n_sem, phases
        self.aliases = aliases or {}

    def __add__(self, other):
        na, ma = len(self.ins), len(self.out_shapes)

        def phases(ins, outs, ssem, rsem):
            mine = self.phases(ins[:na], outs[:ma], ssem, rsem)
            rest = pl.ds(self.n_sem, other.n_sem)
            theirs = other.phases(ins[na:], outs[ma:], ssem.at[rest], rsem.at[rest])
            assert len(mine) == 1 and len(theirs) == 1
            return [mine[0] + theirs[0]]

        aliases = {**self.aliases, **{na + i: ma + o for i, o in other.aliases.items()}}
        return _Rider(self.ins + other.ins, self.out_shapes + other.out_shapes, self.n_sem + other.n_sem, phases,
                      aliases)

    def split(self, refs, n_in, n_out, n_scratch):
        a = n_in + len(self.ins)
        b = a + n_out
        c = b + len(self.out_shapes)
        own = (refs[:n_in], refs[a:b], refs[c:c + n_scratch])
        return own + ((refs[n_in:a], refs[b:c]) + tuple(refs[c + n_scratch:]),)

    def first(self, ride):
        for make in self.phases(*ride)[0]:
            make().start()

    def mid(self, ride):
        ph = self.phases(*ride)
        if len(ph) == 2:
            for make in ph[0]:
                make().wait_recv()
            for make in ph[1]:
                make().start()

    def last(self, ride):
        ph = self.phases(*ride)
        if len(ph) == 2:
            for make in ph[0]:
                make().wait_send()
        for make in ph[-1]:
            make().wait()

    def call(self, body, args, *, grid, name, in_specs, out_specs, out_shape, scratch_shapes, vmem_mib,
             prefetch=None):
        n_in, n_out = len(in_specs), len(out_specs)
        sems = [pltpu.SemaphoreType.DMA((self.n_sem,)), pltpu.SemaphoreType.DMA((self.n_sem,))]
        n_pre = 0 if prefetch is None else 1
        grid_spec = pltpu.PrefetchScalarGridSpec(
            num_scalar_prefetch=n_pre, grid=grid,
            in_specs=list(in_specs) + [HBM_SPEC] * len(self.ins),
            out_specs=list(out_specs) + [HBM_SPEC] * len(self.out_shapes),
            scratch_shapes=list(scratch_shapes) + sems)
        return pl.pallas_call(
            body, name=name, grid_spec=grid_spec,
            out_shape=list(out_shape) + self.out_shapes,
            input_output_aliases={n_pre + n_in + i: n_out + o for i, o in self.aliases.items()},
            compiler_params=_cp(vmem_mib),
        )(*([] if prefetch is None else [prefetch]), *_hbm(*args), *self.ins)

    def run(self, name):
        def body(*refs):
            ride = self.split(refs, 0, 0, 0)[3]
            self.first(ride)
            self.mid(ride)
            self.last(ride)

        return self.call(body, [], grid=(), name=name, in_specs=[], out_specs=[], out_shape=[],
                         scratch_shapes=[], vmem_mib=16)


def _remote(src, dst, ssem, rsem, n, dev):
    return functools.partial(pltpu.make_async_remote_copy, src_ref=src, dst_ref=dst, send_sem=ssem.at[n],
                             recv_sem=rsem.at[n], device_id=dev, device_id_type=MESH)


def _cast_into_slot(w, place, name):
    R, C = w.shape
    tr = min(R, 512)

    def body(pl_ref, w_ref, o_ref):
        o_ref[0] = w_ref[...].astype(BF16)

    return pl.pallas_call(
        body, name=name,
        grid_spec=pltpu.PrefetchScalarGridSpec(
            num_scalar_prefetch=1, grid=(R // tr,),
            in_specs=[pl.BlockSpec((tr, C), lambda r, pr: (r, 0))],
            out_specs=pl.BlockSpec((1, tr, C), lambda r, pr: (pr[1], r, 0))),
        out_shape=_sds((N_CHIPS, R, C), BF16),
    )(place, w)


CAST_STEPS = 8


def _cast_rest(ws, place, rider):
    n = len(ws)

    def body(pl_ref, *refs):
        w_refs, o_refs, _, ride = rider.split(refs, n, n, 0)
        r = pl.program_id(0)

        @pl.when(r == 0)
        def _():
            rider.first(ride)

        @pl.when(r == CAST_STEPS // 2)
        def _():
            rider.mid(ride)

        for w_ref, o_ref in zip(w_refs, o_refs):
            o_ref[0] = w_ref[...].astype(BF16)

        @pl.when(r == CAST_STEPS - 1)
        def _():
            rider.last(ride)

    def rows(w):
        return w.shape[0] // CAST_STEPS

    return rider.call(
        body, ws, grid=(CAST_STEPS,), name="cast_weights", prefetch=place,
        in_specs=[pl.BlockSpec((rows(w), w.shape[1]), lambda r, pr: (r, 0)) for w in ws],
        out_specs=[pl.BlockSpec((1, rows(w), w.shape[1]), lambda r, pr: (pr[1], r, 0)) for w in ws],
        out_shape=[_sds((N_CHIPS,) + w.shape, BF16) for w in ws], scratch_shapes=[], vmem_mib=32)


def _gather_rider(stacked, part="both"):
    n, nf = len(stacked), len(CHIP_FLIPS)

    def phases(ins, outs, ssem, rsem):
        x, y, c = _place()
        slot = 2 * x + y
        ici, d2d = [], []
        for w, (i_ref, o_ref) in enumerate(zip(ins, outs)):
            hh = o_ref.shape[1] // 2
            rows = pl.ds(c * hh, hh)
            for f, (fx, fy) in enumerate(CHIP_FLIPS):
                k = w * nf + f
                theirs = 2 * (x ^ fx) + (y ^ fy)
                if part != "pair":
                    ici.append(_remote(i_ref.at[slot, rows], o_ref.at[slot, rows], ssem, rsem, k,
                                       (x ^ fx, y ^ fy, c)))
                if part != "chips":
                    d2d.append(_remote(o_ref.at[theirs, rows], o_ref.at[theirs, rows], ssem, rsem,
                                       (n * nf if part == "both" else 0) + k, (x, y, 1 - c)))
        return [ph for ph in (ici, d2d) if ph]

    return _Rider(stacked, [_sds(s.shape, s.dtype) for s in stacked], (2 if part == "both" else 1) * n * nf,
                  phases, aliases={i: i for i in range(n)})


def _pair_swap_rider(grads):
    def phases(ins, outs, ssem, rsem):
        x, y, c = _place()
        return [[_remote(g.at[:, 1 - c], o, ssem, rsem, k, (x, y, 1 - c))
                 for k, (g, o) in enumerate(zip(ins, outs))]]

    return _Rider(grads, [_sds((N_CHIPS,) + g.shape[2:], g.dtype) for g in grads], len(grads), phases)


def _chip_scatter_rider(parts):
    nf = len(CHIP_FLIPS)

    def phases(ins, outs, ssem, rsem):
        x, y, c = _place()
        return [[_remote(r.at[2 * (x ^ fx) + (y ^ fy)], o.at[f], ssem, rsem, w * nf + f, (x ^ fx, y ^ fy, c))
                 for w, (r, o) in enumerate(zip(ins, outs)) for f, (fx, fy) in enumerate(CHIP_FLIPS)]]

    return _Rider(parts, [_sds((nf,) + r.shape[1:], r.dtype) for r in parts], len(parts) * nf, phases)


def _pair_send_rider(halves):
    def phases(ins, outs, ssem, rsem):
        x, y, c = _place()
        return [[_remote(h, o, ssem, rsem, k, (x, y, 1 - c)) for k, (h, o) in enumerate(zip(ins, outs))]]

    return _Rider(halves, [_sds(h.shape, h.dtype) for h in halves], len(halves), phases)


PAIR_SUM_STEPS = 2
CHIP_SUM_STEPS = 4
ADAMW_STEPS = 4


def _no_rider():
    return _Rider([], [], 1, lambda ins, outs, ssem, rsem: [[]])


def _add_pair(grads, recvs, place, name, rider):
    n = len(grads)

    def body(pl_ref, *refs):
        ins, outs, _, ride = rider.split(refs, 2 * n, 2 * n, 0)
        j, h = pl.program_id(0), pl.program_id(1)

        @pl.when(jnp.logical_and(j == 0, h == 0))
        def _():
            rider.first(ride)

        for w in range(n):
            s = ins[2 * w][:, 0] + ins[2 * w + 1][...]
            outs[2 * w][...] = s
            outs[2 * w + 1][...] = s.astype(BF16)

        @pl.when(jnp.logical_and(j == N_CHIPS - 1, h == PAIR_SUM_STEPS - 1))
        def _():
            rider.last(ride)

    in_specs, out_specs, out_shape, args = [], [], [], []
    for g, r in zip(grads, recvs):
        _, _, H, C = g.shape
        th = H // PAIR_SUM_STEPS
        spec = pl.BlockSpec((1, th, C), lambda j, h, pr: (j, h, 0))
        in_specs += [pl.BlockSpec((1, 1, th, C), lambda j, h, pr: (j, pr[0], h, 0)), spec]
        out_specs += [spec, spec]
        out_shape += [_sds((N_CHIPS, H, C), F32), _sds((N_CHIPS, H, C), BF16)]
        args += [g, r]
    res = rider.call(body, args, grid=(N_CHIPS, PAIR_SUM_STEPS), name=name, prefetch=place, in_specs=in_specs,
                     out_specs=out_specs, out_shape=out_shape, scratch_shapes=[], vmem_mib=32)
    return [(res[2 * w], res[2 * w + 1]) for w in range(n)], res[2 * n:]


def _add_chips(parts, recvs, place, name, rider):
    n = len(parts)

    def body(pl_ref, *refs):
        ins, outs, _, ride = rider.split(refs, 2 * n, n, 0)
        h = pl.program_id(0)

        @pl.when(h == 0)
        def _():
            rider.first(ride)

        for w in range(n):
            p_ref, r_ref = ins[2 * w], ins[2 * w + 1]
            outs[w][...] = p_ref[0] + r_ref[0].astype(F32) + r_ref[1].astype(F32) + r_ref[2].astype(F32)

        @pl.when(h == CHIP_SUM_STEPS - 1)
        def _():
            rider.last(ride)

    in_specs, out_specs, out_shape, args = [], [], [], []
    for p, r in zip(parts, recvs):
        _, H, C = p.shape
        th = H // CHIP_SUM_STEPS
        in_specs += [pl.BlockSpec((1, th, C), lambda h, pr: (pr[1], h, 0)),
                     pl.BlockSpec((len(CHIP_FLIPS), th, C), lambda h, pr: (0, h, 0))]
        out_specs.append(pl.BlockSpec((th, C), lambda h, pr: (h, 0)))
        out_shape.append(_sds((H, C), F32))
        args += [p, r]
    res = rider.call(body, args, grid=(CHIP_SUM_STEPS,), name=name, prefetch=place, in_specs=in_specs,
                     out_specs=out_specs, out_shape=out_shape, scratch_shapes=[], vmem_mib=32)
    return res[:n], res[n:]


def _adamw_math(w, g, m, v):
    m = ADAM_B1 * m + (1.0 - ADAM_B1) * g
    v = ADAM_B2 * v + (1.0 - ADAM_B2) * (g * g)
    m_hat = m / (1.0 - ADAM_B1 ** ADAM_STEP)
    v_hat = v / (1.0 - ADAM_B2 ** ADAM_STEP)
    delta = -ADAM_LR * (m_hat / (jnp.sqrt(v_hat) + ADAM_EPS) + ADAM_WD * w)
    return delta, m, v


def _adamw(ws, mines, theirs, ms, vs, place, name, rider):
    n = len(ws)

    def body(pl_ref, *refs):
        ins, outs, _, ride = rider.split(refs, 5 * n, 4 * n, 0)
        h, r = pl.program_id(0), pl.program_id(1)

        @pl.when(jnp.logical_and(h == 0, r == 0))
        def _():
            rider.first(ride)

        for k in range(n):
            w_ref, a_ref, b_ref, m_ref, v_ref = ins[5 * k:5 * k + 5]
            g = jnp.where(h == pl_ref[0], a_ref[...], b_ref[...])
            d, mo, vo = _adamw_math(w_ref[...], g, m_ref[...], v_ref[...])
            for o_ref, val in zip(outs[4 * k:4 * k + 4], (g, d, mo, vo)):
                o_ref[...] = val

        @pl.when(jnp.logical_and(h == 1, r == ADAMW_STEPS - 1))
        def _():
            rider.last(ride)

    in_specs, out_specs, out_shape, args = [], [], [], []
    for w, a, b, m, v in zip(ws, mines, theirs, ms, vs):
        R, C = w.shape
        th = (R // 2) // ADAMW_STEPS
        whole = pl.BlockSpec((th, C), lambda h, r, pr: (h * ADAMW_STEPS + r, 0))
        mine_spec = pl.BlockSpec((th, C), lambda h, r, pr: (jnp.where(h == pr[0], r, 0), 0))
        theirs_spec = pl.BlockSpec((th, C), lambda h, r, pr: (jnp.where(h == pr[0], 0, r), 0))
        in_specs += [whole, mine_spec, theirs_spec, whole, whole]
        out_specs += [whole] * 4
        out_shape += [_sds((R, C), F32)] * 4
        args += [w, a, b, m, v]
    res = rider.call(body, args, grid=(2, ADAMW_STEPS), name=name, prefetch=place, in_specs=in_specs,
                     out_specs=out_specs, out_shape=out_shape, scratch_shapes=[], vmem_mib=40)
    return [tuple(res[4 * k:4 * k + 4]) for k in range(n)], res[4 * n:]


DEVICE_FLIPS = tuple((fx, fy, fc) for fx in (0, 1) for fy in (0, 1) for fc in (0, 1))[1:]


def _pack_exchange_rider(pack):
    def phases(ins, outs, ssem, rsem):
        x, y, c = _place()
        mine = outs[0].at[4 * x + 2 * y + c]
        copies = [_remote(ins[0], mine, ssem, rsem, k, (x ^ fx, y ^ fy, c ^ fc))
                  for k, (fx, fy, fc) in enumerate(DEVICE_FLIPS)]
        copies.append(functools.partial(pltpu.make_async_copy, ins[0], mine, ssem.at[len(DEVICE_FLIPS)]))
        return [copies]

    return _Rider([pack], [_sds((N_DEV,) + pack.shape, pack.dtype)], len(DEVICE_FLIPS) + 1, phases)


def _small_sum_adamw(recv_a, recv_b, wpack, mpack, vpack):
    R = wpack.shape[0]

    def body(a_ref, b_ref, w_ref, m_ref, v_ref, gs_ref, d_ref, mo_ref, vo_ref):
        ta, tb = a_ref[0], b_ref[0]
        for dev in range(1, N_DEV):
            ta = ta + a_ref[dev]
            tb = tb + b_ref[dev]
        total = jnp.concatenate([ta, tb], axis=0)
        gs_ref[...] = total
        d, mo, vo = _adamw_math(w_ref[...], total, m_ref[...], v_ref[...])
        d_ref[...] = d
        mo_ref[...] = mo
        vo_ref[...] = vo

    return pl.pallas_call(
        body, name="small_sum_adamw", in_specs=[VMEM_SPEC] * 5, out_specs=[VMEM_SPEC] * 4,
        out_shape=[_sds((R, LANES), F32)] * 4,
    )(recv_a, recv_b, wpack, mpack, vpack)


def _rows8(a):
    a = a.reshape(-1, LANES)
    pad = (-a.shape[0]) % 8
    return jnp.pad(a, ((0, pad), (0, 0))) if pad else a


def _pack(parts):
    return jnp.concatenate([_rows8(a) for a in parts], axis=0)


def _unpack(pack, like):
    out, row = [], 0
    for a in like:
        n = a.size // LANES
        out.append(pack[row:row + n].reshape(a.shape))
        row += n + (-n) % 8
    return out


def kernel(x, p, emb_ln_g, emb_ln_b, w_in, attn_out_g, w_pool, pool_scale, w_out, ln1_g, ln1_b, w_up, w_down, ln2_g, ln2_b, w_ple, w_ple_gate, ln3_g, ln3_b, loss_target, m_emb_ln_g, m_emb_ln_b, m_w_in, m_attn_out_g, m_w_pool, m_pool_scale, m_w_out, m_ln1_g, m_ln1_b, m_w_up, m_w_down, m_ln2_g, m_ln2_b, m_w_ple, m_w_ple_gate, m_ln3_g, m_ln3_b, v_emb_ln_g, v_emb_ln_b, v_w_in, v_attn_out_g, v_w_pool, v_pool_scale, v_w_out, v_ln1_g, v_ln1_b, v_w_up, v_w_down, v_ln2_g, v_ln2_b, v_w_ple, v_w_ple_gate, v_ln3_g, v_ln3_b):
    S = x.shape[1]
    tq = min(256, S)
    tm_mlp = min(1024, S)
    tm_pool = min(1024, S)
    tm_ln = min(512, S)
    tm_fwd = min(1024, S)
    xs = x[0]
    ps = p[0, 0]
    tgt = loss_target[0]
    row = lambda a: a.reshape(1, -1)
    g0, b0 = row(emb_ln_g), row(emb_ln_b)
    g1, b1, g2, b2, g3, b3 = ln1_g, ln1_b, ln2_g, ln2_b, ln3_g, ln3_b
    wp = w_pool[0]

    xi, yi, ci = _place()
    place = jnp.stack([ci, 2 * xi + yi]).astype(jnp.int32)
    names = ["w_in", "w_out", "w_up", "w_down", "w_ple", "w_ple_gate"]

    big = [w_in[0], w_out[0], w_up[0], w_down[0], w_ple[0], w_ple_gate[0]]
    s_in = _cast_into_slot(big[0], place, "cast_w_in")
    s_out, s_up, s_down, s_ple, s_gate, w_in_s = _cast_rest(big[1:], place, _gather_rider([s_in]))

    xh0, rstd0, q, k, v, u, s_out = _embln_inproj(
        xs, g0, b0, w_in_s, tm_fwd, _gather_rider([s_out], "chips"))
    o_raw, on, s_up, s_down, w_out_s = _attn_fwd(
        q, k, v, attn_out_g, tq, _gather_rider([s_up, s_down], "chips") + _gather_rider([s_out], "pair"))
    w_out_f = w_out_s.reshape(D_MODEL, D_MODEL)
    d_b, pooled = _pool_fwd(u, wp, pool_scale, tm_pool)
    xh1, rstd1, x1b, w_up_s, w_down_s, s_ple, s_gate = _mix_ln1(
        on, pooled, xh0, g0, b0, w_out_f, g1, b1, tm_fwd,
        _gather_rider([s_up, s_down], "pair") + _gather_rider([s_ple, s_gate], "chips"))
    w_down_f = w_down_s.reshape(D_FF, D_MODEL)
    xh2, rstd2, rb, w_ple_s, w_gate_s = _mlp_ln2(xh1, x1b, g1, b1, w_up_s, w_down_f, tm_mlp,
                                                 _gather_rider([s_ple, s_gate], "pair"))
    w_gate_f = w_gate_s.reshape(D_MODEL, D_MODEL)

    (dpre2, dhb, dw_ple, dw_gate, dg3, db3, dg2, db2, loss_row) = _ple_ln3_loss(
        xh2, rstd2, g2, b2, ps, w_ple_s, w_gate_f, g3, b3, tgt, tm_ln)
    def halves_of(g):
        return g.reshape(N_CHIPS, 2, g.shape[1] // 2, g.shape[2])

    ple_halves = [halves_of(dw_ple), halves_of(dw_gate.reshape(N_CHIPS, D_MODEL // N_CHIPS, D_MODEL))]
    dx1m, da, *ple_pair = _mlp_bwd(rb, dhb, w_up_s, w_down_f, tm_mlp, _pair_swap_rider(ple_halves))
    (dw_up,) = _tn_matmul(x1b, da, "grad_w_up", 1024, S, True, _no_rider())
    up_halves = halves_of(dw_up)
    dw_down, up_pair = _tn_matmul(rb, dhb, "grad_w_down", 1024, S, False,
                                  _pair_swap_rider([up_halves]), square_a=True)
    down_halves = halves_of(dw_down.reshape(N_CHIPS, D_FF // N_CHIPS, D_MODEL))
    dpre1, don, dpooled, dw_out, dg1, db1, down_pair = _mix_bwd(
        dpre2, dx1m, xh1, rstd1, g1, w_out_f, on, pooled, tm_ln, _pair_swap_rider([down_halves]))
    out_halves = halves_of(dw_out.reshape(N_CHIPS, D_MODEL // N_CHIPS, D_MODEL))
    du, dwp, dsc, out_pair = _pool_bwd(dpooled, d_b, wp, pool_scale, tm_pool, _pair_swap_rider([out_halves]))
    early_sum, _ = _add_pair(
        [out_halves, up_halves, down_halves] + ple_halves, [out_pair, up_pair, down_pair] + ple_pair, place,
        "pair_sum_early", _no_rider())
    pack_a = _pack([jnp.broadcast_to(loss_row, (8, LANES)), dwp, dsc, dg1, db1, dg2, db2, dg3, db3])
    riding = _chip_scatter_rider([b for _, b in early_sum]) + _pack_exchange_rider(pack_a)
    dq, dk, dv, dga, *arrived = _attn_bwd(q, k, v, don, o_raw, attn_out_g, tq, riding)
    early_chips, recv_a = arrived[:-1], arrived[-1]
    grad_x, dw_in, dg0, db0 = _inproj_bwd(dq, dk, dv, du, dpre1, xh0, rstd0, g0, b0, w_in_s, tm_ln)

    in_halves = halves_of(dw_in)
    pack_b = _pack([dg0, db0, dga])
    early_mine, (in_pair, recv_b) = _add_chips(
        [s for s, _ in early_sum], early_chips, place, "chip_sum_early",
        _pair_swap_rider([in_halves]) + _pack_exchange_rider(pack_b))
    (in_sum,), early_theirs = _add_pair([in_halves], [in_pair], place, "pair_sum_w_in", _pair_send_rider(early_mine))
    ms = [m_w_in, m_w_out, m_w_up, m_w_down, m_w_ple, m_w_ple_gate]
    vs = [v_w_in, v_w_out, v_w_up, v_w_down, v_w_ple, v_w_ple_gate]
    early_res, _ = _adamw(big[1:], early_mine, early_theirs, [m[0] for m in ms[1:]], [v[0] for v in vs[1:]],
                          place, "adamw_early", _no_rider())
    (in_chips,) = _chip_scatter_rider([in_sum[1]]).run("reduce_chips_late")
    (in_mine,), _ = _add_chips([in_sum[0]], [in_chips], place, "chip_sum_w_in", _no_rider())
    (in_theirs,) = _pair_send_rider([in_mine]).run("gather_pair_w_in")
    in_res, _ = _adamw(big[:1], [in_mine], [in_theirs], [ms[0][0]], [vs[0][0]], place, "adamw_w_in", _no_rider())
    big_out = {n: tuple(r.reshape(m.shape) for r in res4) for n, res4, m in zip(names, in_res + early_res, ms)}

    small_names = ["w_pool", "pool_scale", "ln1_g", "ln1_b", "ln2_g", "ln2_b", "ln3_g", "ln3_b",
                   "emb_ln_g", "emb_ln_b", "attn_out_g"]
    small_w = [w_pool, pool_scale, ln1_g, ln1_b, ln2_g, ln2_b, ln3_g, ln3_b, emb_ln_g, emb_ln_b, attn_out_g]
    small_m = [m_w_pool, m_pool_scale, m_ln1_g, m_ln1_b, m_ln2_g, m_ln2_b, m_ln3_g, m_ln3_b,
               m_emb_ln_g, m_emb_ln_b, m_attn_out_g]
    small_v = [v_w_pool, v_pool_scale, v_ln1_g, v_ln1_b, v_ln2_g, v_ln2_b, v_ln3_g, v_ln3_b,
               v_emb_ln_g, v_emb_ln_b, v_attn_out_g]
    loss_like = jnp.zeros((8, LANES), F32)
    gs, ds, mos, vos = _small_sum_adamw(recv_a, recv_b, _pack([loss_like] + small_w), _pack([loss_like] + small_m),
                                        _pack([jnp.ones((8, LANES), F32)] + small_v))
    like = [loss_like] + small_w
    gs_u, ds_u, mos_u, vos_u = (_unpack(a, like) for a in (gs, ds, mos, vos))
    loss = gs_u[0][0, 0]
    small_out = {n: (gs_u[i + 1], ds_u[i + 1], mos_u[i + 1], vos_u[i + 1]) for i, n in enumerate(small_names)}

    order = ["emb_ln_g", "emb_ln_b", "w_in", "attn_out_g", "w_pool", "pool_scale", "w_out", "ln1_g", "ln1_b",
             "w_up", "w_down", "ln2_g", "ln2_b", "w_ple", "w_ple_gate", "ln3_g", "ln3_b"]
    res = {**big_out, **small_out}
    outs = [loss, grad_x.reshape(x.shape)]
    for kind in range(4):
        outs += [res[n][kind] for n in order]
    return tuple(outs)
```

```python
import functools

import jax
import jax.numpy as jnp
from jax import lax
from jax.experimental import pallas as pl
from jax.experimental.pallas import tpu as pltpu

F32 = jnp.float32
BF16 = jnp.bfloat16

D_MODEL = 1024
ATTN_WIDTH = 512
POOL_WIDTH = 512
HEAD_DIM = 64
PAIR = 2 * HEAD_DIM
N_PAIRS = ATTN_WIDTH // PAIR
N_POOL_GROUPS = 4
POOL_GROUP = 128
POOL_HALO = 16
D_FF = 4096
PLE_DIM = 256
N_CHIPS = 4
N_DEV = 8
LN_EPS = 1e-5
RMS_EPS = 1e-6
ALPHA = float(2.0 ** 0.25)
Q_SCALE = 0.125
ADAM_LR = 0.001
ADAM_B1 = 0.9
ADAM_B2 = 0.999
ADAM_EPS = 1e-08
ADAM_WD = 0.01
ADAM_STEP = 10
LANES = 128
MIB = 1024 * 1024

MESH = pl.DeviceIdType.MESH
HBM_SPEC = pl.BlockSpec(memory_space=pltpu.HBM)
VMEM_SPEC = pl.BlockSpec(memory_space=pltpu.VMEM)


def _cp(vmem_mib):
    return pltpu.CompilerParams(vmem_limit_bytes=vmem_mib * MIB)


def _dot(a, b):
    return jnp.dot(a, b, preferred_element_type=F32)


def _dot_nt(a, b):
    return lax.dot_general(a, b, (((1,), (1,)), ((), ())), preferred_element_type=F32)


def _dot_tn(a, b):
    return lax.dot_general(a, b, (((0,), (0,)), ((), ())), preferred_element_type=F32)


def _ln_fwd(pre):
    mu = jnp.mean(pre, axis=-1, keepdims=True)
    xc = pre - mu
    var = jnp.mean(xc * xc, axis=-1, keepdims=True)
    rstd = lax.rsqrt(var + LN_EPS)
    return xc * rstd, rstd


def _ln_bwd(dy, xh, rstd, g):
    dxh = dy * g
    m1 = jnp.mean(dxh, axis=-1, keepdims=True)
    m2 = jnp.mean(dxh * xh, axis=-1, keepdims=True)
    return rstd * (dxh - m1 - xh * m2)


def _colsum(a):
    return jnp.sum(a, axis=0, keepdims=True)


def _neg_softplus(z):
    return -(jnp.maximum(z, 0.0) + jnp.log(1.0 + jnp.exp(-jnp.abs(z))))


def _row_spec(tm, n):
    return pl.BlockSpec((tm, n), lambda i: (i, 0))


def _const_spec(shape):
    nd = len(shape)
    return pl.BlockSpec(shape, lambda *_: (0,) * nd)


def _hbm(*arrays):
    return [pltpu.with_memory_space_constraint(a, pltpu.HBM) for a in arrays]


def _sds(shape, dtype):
    return pltpu.HBM(shape, dtype)


def _embln_inproj(x, g0, b0, w_in_s, tm, rider):
    S, D = x.shape
    n_t = S // tm

    def body(*refs):
        ((x_ref, g_ref, b_ref, w_ref), (xh_ref, rstd_ref, q_ref, k_ref, v_ref, u_ref), _,
         ride) = rider.split(refs, 4, 6, 0)
        i = pl.program_id(0)

        @pl.when(i == 0)
        def _():
            rider.first(ride)

        @pl.when(i == (3 * n_t) // 4)
        def _():
            rider.mid(ride)

        xh, rstd = _ln_fwd(x_ref[...])
        xh_ref[...] = xh
        rstd_ref[...] = rstd
        xb = (xh * g_ref[...] + b_ref[...]).astype(BF16)
        q_ref[...] = (_dot(xb, w_ref[0]) * Q_SCALE).astype(BF16)
        k_ref[...] = _dot(xb, w_ref[1]).astype(BF16)
        v_ref[...] = _dot(xb, w_ref[2]).astype(BF16)
        u_ref[...] = _dot(xb, w_ref[3])

        @pl.when(i == n_t - 1)
        def _():
            rider.last(ride)

    return rider.call(
        body, [x, g0, b0, w_in_s], grid=(n_t,), name="embln_inproj",
        in_specs=[_row_spec(tm, D), _const_spec((1, D)), _const_spec((1, D)),
                  _const_spec((N_CHIPS, D, 512))],
        out_specs=[_row_spec(tm, D), _row_spec(tm, 1), _row_spec(tm, 512), _row_spec(tm, 512),
                   _row_spec(tm, 512), _row_spec(tm, 512)],
        out_shape=[_sds((S, D), F32), _sds((S, 1), F32), _sds((S, 512), BF16), _sds((S, 512), BF16),
                   _sds((S, 512), BF16), _sds((S, 512), F32)],
        scratch_shapes=[], vmem_mib=56)


def _tri(n, upper):
    r = lax.broadcasted_iota(jnp.int32, (n, n), 0)
    c = lax.broadcasted_iota(jnp.int32, (n, n), 1)
    keep = (r < c) if upper else (r > c)
    return jnp.where(keep, 1.0, 0.0).astype(BF16)


def _strictly_causal(n):
    return lax.broadcasted_iota(jnp.int32, (n, n), 1) < lax.broadcasted_iota(jnp.int32, (n, n), 0)


LOG_WEIGHT_FLOOR = -110.0


def _weights_left(c_ls):
    return (jnp.max(jnp.maximum(c_ls[0], c_ls[1])) > LOG_WEIGHT_FLOOR).astype(jnp.int32)


def _sb_tile(qhs, kt, low, c_ls, valid):
    valids = valid if isinstance(valid, (list, tuple)) else [valid] * len(qhs)
    zs = [_dot_nt(qh, kt) for qh in qhs]
    lrs = [_neg_softplus(z) for z in zs]
    ls_ = [lr if m is None else jnp.where(m, lr, 0.0) for lr, m in zip(lrs, valids)]
    sfx = [_dot(l.astype(BF16), low) + c_l for l, c_l in zip(ls_, c_ls)]
    lss = [z + lr for z, lr in zip(zs, lrs)]
    ws = [jnp.exp(ls + s) for ls, s in zip(lss, sfx)]
    ws = [w if m is None else jnp.where(m, w, 0.0) for w, m in zip(ws, valids)]
    return lss, ls_, ws


def _attn_fwd(q, k, v, ga, tq, rider):
    S = q.shape[0]
    nq = S // tq

    def body(*refs):
        (q_ref, k_ref, v_ref, ga_ref), (o_ref, on_ref), (low_s,), ride = rider.split(refs, 4, 2, 1)
        p, i = pl.program_id(0), pl.program_id(1)

        @pl.when(jnp.logical_and(p == 0, i == 0))
        def _():
            rider.first(ride)
            low_s[...] = _tri(tq, upper=False)

        @pl.when(jnp.logical_and(p == N_PAIRS - 1, i == 0))
        def _():
            rider.mid(ride)

        lane = lax.broadcasted_iota(jnp.int32, (1, PAIR), 1)
        m0 = lane < HEAD_DIM
        low = low_s[...]
        q2 = q_ref[...]
        qhs = [jnp.where(m0, q2, jnp.zeros_like(q2)), jnp.where(m0, jnp.zeros_like(q2), q2)]

        def tile(kb, c_ls, accs, valid):
            ks = pl.multiple_of(kb * tq, tq)
            kt = k_ref[pl.ds(ks, tq), :]
            vt = v_ref[pl.ds(ks, tq), :]
            _, ls_, ws = _sb_tile(qhs, kt, low, c_ls, valid)
            new_a = [acc + _dot(w.astype(BF16), vt) for acc, w in zip(accs, ws)]
            new_c = [c_l + jnp.sum(l, axis=1, keepdims=True) for c_l, l in zip(c_ls, ls_)]
            return new_c, new_a

        zc, za = jnp.zeros((tq, 1), F32), jnp.zeros((tq, PAIR), F32)

        def first_two():
            c_ls, accs = tile(i, [zc, zc], [za, za], _strictly_causal(tq))
            c_ls, accs = tile(i - 1, c_ls, accs, None)
            return (_weights_left(c_ls), *c_ls, *accs)

        def first_one():
            c_ls, accs = tile(i, [zc, zc], [za, za], _strictly_causal(tq))
            return (jnp.int32(0), *c_ls, *accs)

        st0 = lax.cond(i >= 1, first_two, first_one)

        def more(st):
            return jnp.logical_and(st[0] <= i, st[1] > 0)

        def step(st):
            n, _, c0, c1, a0, a1 = st
            c_ls, accs = tile(i - n, [c0, c1], [a0, a1], None)
            return (n + 1, _weights_left(c_ls), c_ls[0], c_ls[1], accs[0], accs[1])

        st = lax.while_loop(more, step, (jnp.int32(2), *st0))
        o = jnp.where(m0, st[4], st[5])
        o_ref[...] = o
        sq = o * o
        ms0 = jnp.sum(jnp.where(m0, sq, 0.0), axis=-1, keepdims=True) * (1.0 / HEAD_DIM)
        ms1 = jnp.sum(jnp.where(m0, 0.0, sq), axis=-1, keepdims=True) * (1.0 / HEAD_DIM)
        rs = jnp.where(m0, lax.rsqrt(ms0 + RMS_EPS), lax.rsqrt(ms1 + RMS_EPS))
        on_ref[...] = (o * rs * ga_ref[...]).astype(BF16)

        @pl.when(jnp.logical_and(p == N_PAIRS - 1, i == nq - 1))
        def _():
            rider.last(ride)

    return rider.call(
        body, [q, k, v, ga], grid=(N_PAIRS, nq), name="attn_fwd",
        in_specs=[pl.BlockSpec((tq, PAIR), lambda p, i: (i, p)),
                  pl.BlockSpec((S, PAIR), lambda p, i: (0, p)),
                  pl.BlockSpec((S, PAIR), lambda p, i: (0, p)),
                  pl.BlockSpec((1, PAIR), lambda p, i: (0, p))],
        out_specs=[pl.BlockSpec((tq, PAIR), lambda p, i: (i, p)),
                   pl.BlockSpec((tq, PAIR), lambda p, i: (i, p))],
        out_shape=[_sds((S, ATTN_WIDTH), F32), _sds((S, ATTN_WIDTH), BF16)],
        scratch_shapes=[pltpu.VMEM((tq, tq), BF16)], vmem_mib=40)


def _pool_fwd(u, w_pool, pscale, tm):
    S = u.shape[0]
    hb = tm // POOL_HALO

    def body(u_ref, uh_ref, wp_ref, sc_ref, d_ref, pooled_ref):
        i = pl.program_id(0)
        halo = jnp.where(i > 0, uh_ref[...], 0.0)
        pos = i * tm + lax.broadcasted_iota(jnp.int32, (tm, 1), 0)
        for g in range(N_POOL_GROUPS):
            win = 2 ** (g + 1)
            cols = slice(g * POOL_GROUP, (g + 1) * POOL_GROUP)
            ut = u_ref[:, cols]
            s = jnp.concatenate([halo[:, cols], ut], axis=0)
            for sh in (1, 2, 4, 8)[:g + 1]:
                s = s + pltpu.roll(s, sh, 0)
            cnt = jnp.minimum(pos + 1, win).astype(F32)
            db = (s[POOL_HALO:, :] / cnt - ut).astype(BF16)
            y = _dot(db, wp_ref[g].astype(BF16))
            d_ref[:, cols] = db
            pooled_ref[:, cols] = (y * sc_ref[:, cols]).astype(BF16)

    return pl.pallas_call(
        body, grid=(S // tm,), name="pool_fwd",
        in_specs=[_row_spec(tm, POOL_WIDTH),
                  pl.BlockSpec((POOL_HALO, POOL_WIDTH), lambda i: (jnp.maximum(i * hb - 1, 0), 0)),
                  _const_spec((N_POOL_GROUPS, POOL_GROUP, POOL_GROUP)), _const_spec((1, POOL_WIDTH))],
        out_specs=[_row_spec(tm, POOL_WIDTH), _row_spec(tm, POOL_WIDTH)],
        out_shape=[_sds((S, POOL_WIDTH), BF16), _sds((S, POOL_WIDTH), BF16)],
        compiler_params=_cp(32),
    )(*_hbm(u, u, w_pool, pscale))


def _mix_ln1(on, pooled, xh0, g0, b0, w_out, g1, b1, tm, rider):
    S, D = xh0.shape
    n_t = S // tm

    def body(*refs):
        ((on_ref, po_ref, xh0_ref, g0_ref, b0_ref, w_ref, g1_ref, b1_ref), (xh_ref, rstd_ref, xb_ref), _,
         ride) = rider.split(refs, 8, 3, 0)

        @pl.when(pl.program_id(0) == 0)
        def _():
            rider.first(ride)

        mixed = _dot(on_ref[...], w_ref[:ATTN_WIDTH, :]) + _dot(po_ref[...], w_ref[ATTN_WIDTH:, :])
        x0 = xh0_ref[...] * g0_ref[...] + b0_ref[...]
        xh, rstd = _ln_fwd(ALPHA * x0 + mixed)
        xh_ref[...] = xh
        rstd_ref[...] = rstd
        xb_ref[...] = (xh * g1_ref[...] + b1_ref[...]).astype(BF16)

        @pl.when(pl.program_id(0) == n_t - 1)
        def _():
            rider.last(ride)

    return rider.call(
        body, [on, pooled, xh0, g0, b0, w_out, g1, b1], grid=(n_t,), name="mix_ln1",
        in_specs=[_row_spec(tm, ATTN_WIDTH), _row_spec(tm, POOL_WIDTH), _row_spec(tm, D),
                  _const_spec((1, D)), _const_spec((1, D)), _const_spec((D, D)),
                  _const_spec((1, D)), _const_spec((1, D))],
        out_specs=[_row_spec(tm, D), _row_spec(tm, 1), _row_spec(tm, D)],
        out_shape=[_sds((S, D), F32), _sds((S, 1), F32), _sds((S, D), BF16)],
        scratch_shapes=[], vmem_mib=56)


def _mlp_ln2(xh1, x1b, g1, b1, w_up_s, w_down, tm, rider):
    S, D = xh1.shape
    fc = D_FF // N_CHIPS
    n_t = S // tm

    def body(*refs):
        ((xh_ref, xb_ref, g_ref, b_ref, wu_ref, wd_ref), (xh2_ref, rstd_ref, r_ref), (acc_ref,),
         ride) = rider.split(refs, 6, 3, 1)
        i, j = pl.program_id(0), pl.program_id(1)

        @pl.when(jnp.logical_and(i == 0, j == 0))
        def _():
            rider.first(ride)

        @pl.when(j == 0)
        def _():
            acc_ref[...] = jnp.zeros_like(acc_ref)

        r = jnp.maximum(_dot(xb_ref[...], wu_ref[0]), 0.0)
        r_ref[...] = r.astype(BF16)
        acc_ref[...] += _dot((r * r).astype(BF16), wd_ref[...])

        @pl.when(j == N_CHIPS - 1)
        def _():
            x1 = xh_ref[...] * g_ref[...] + b_ref[...]
            xh, rstd = _ln_fwd(ALPHA * x1 + acc_ref[...])
            xh2_ref[...] = xh
            rstd_ref[...] = rstd

        @pl.when(jnp.logical_and(i == n_t - 1, j == N_CHIPS - 1))
        def _():
            rider.last(ride)

    return rider.call(
        body, [xh1, x1b, g1, b1, w_up_s, w_down], grid=(n_t, N_CHIPS), name="mlp_ln2",
        in_specs=[pl.BlockSpec((tm, D), lambda i, j: (i, 0)), pl.BlockSpec((tm, D), lambda i, j: (i, 0)),
                  pl.BlockSpec((1, D), lambda i, j: (0, 0)), pl.BlockSpec((1, D), lambda i, j: (0, 0)),
                  pl.BlockSpec((1, D, fc), lambda i, j: (j, 0, 0)),
                  pl.BlockSpec((fc, D), lambda i, j: (j, 0))],
        out_specs=[pl.BlockSpec((tm, D), lambda i, j: (i, 0)), pl.BlockSpec((tm, 1), lambda i, j: (i, 0)),
                   pl.BlockSpec((tm, fc), lambda i, j: (i, j))],
        out_shape=[_sds((S, D), F32), _sds((S, 1), F32), _sds((S, D_FF), BF16)],
        scratch_shapes=[pltpu.VMEM((tm, D), F32)], vmem_mib=56)


def _ple_ln3_loss(xh2, rstd2, g2, b2, p, w_ple_s, w_gate, g3, b3, target, tm):
    S, D = xh2.shape
    pc = D // N_CHIPS

    def body(xh2_ref, rstd2_ref, g2_ref, b2_ref, p_ref, wp_ref, wg_ref, g3_ref, b3_ref, t_ref,
             dpre2_ref, dhb_ref, dwp_ref, dwg_ref, dg3_ref, db3_ref, dg2_ref, db2_ref, loss_ref):
        i = pl.program_id(0)

        @pl.when(i == 0)
        def _():
            for r in (dwp_ref, dwg_ref, dg3_ref, db3_ref, dg2_ref, db2_ref, loss_ref):
                r[...] = jnp.zeros_like(r)

        xh2 = xh2_ref[...]
        x2 = xh2 * g2_ref[...] + b2_ref[...]
        x2b = x2.astype(BF16)
        gate = 1.0 / (1.0 + jnp.exp(-_dot(x2b, wg_ref[...])))
        pb = p_ref[...].astype(BF16)
        pe = jnp.concatenate([_dot(pb, wp_ref[c]) for c in range(N_CHIPS)], axis=1)
        xh3, rstd3 = _ln_fwd(ALPHA * x2 + pe * gate)
        diff = xh3 * g3_ref[...] + b3_ref[...] - t_ref[...]
        loss_ref[...] += (0.5 / D) * jnp.sum(diff * diff)
        dy = diff * (1.0 / D)
        dg3_ref[...] += _colsum(dy * xh3)
        db3_ref[...] += _colsum(dy)
        dpre3 = _ln_bwd(dy, xh3, rstd3, g3_ref[...])
        dpe_b = (dpre3 * gate).astype(BF16)
        dgp_b = (dpre3 * pe * gate * (1.0 - gate)).astype(BF16)
        dx2 = ALPHA * dpre3 + _dot_nt(dgp_b, wg_ref[...])
        dwg_ref[...] += _dot_tn(x2b, dgp_b)
        for c in range(N_CHIPS):
            dwp_ref[c] += _dot_tn(pb, dpe_b[:, c * pc:(c + 1) * pc])
        dg2_ref[...] += _colsum(dx2 * xh2)
        db2_ref[...] += _colsum(dx2)
        dpre2 = _ln_bwd(dx2, xh2, rstd2_ref[...], g2_ref[...])
        dpre2_ref[...] = dpre2
        dhb_ref[...] = dpre2.astype(BF16)

    vec = _const_spec((1, D))
    return pl.pallas_call(
        body, grid=(S // tm,), name="ple_ln3_loss",
        in_specs=[_row_spec(tm, D), _row_spec(tm, 1), vec, vec, _row_spec(tm, PLE_DIM),
                  _const_spec((N_CHIPS, PLE_DIM, pc)), _const_spec((D, D)), vec, vec, _row_spec(tm, D)],
        out_specs=[_row_spec(tm, D), _row_spec(tm, D), _const_spec((N_CHIPS, PLE_DIM, pc)),
                   _const_spec((D, D)), vec, vec, vec, vec, _const_spec((1, LANES))],
        out_shape=[_sds((S, D), F32), _sds((S, D), BF16), _sds((N_CHIPS, PLE_DIM, pc), F32),
                   _sds((D, D), F32), _sds((1, D), F32), _sds((1, D), F32), _sds((1, D), F32),
                   _sds((1, D), F32), _sds((1, LANES), F32)],
        compiler_params=_cp(58),
    )(*_hbm(xh2, rstd2, g2, b2, p, w_ple_s, w_gate, g3, b3, target))


def _mlp_bwd(rb, dhb, w_up_s, w_down, tm, rider):
    S, D = dhb.shape
    fc = D_FF // N_CHIPS
    n_t = S // tm

    def body(*refs):
        (r_ref, dh_ref, wu_ref, wd_ref), (dx_ref, da_ref), _, ride = rider.split(refs, 4, 2, 0)
        i, j = pl.program_id(0), pl.program_id(1)

        @pl.when(jnp.logical_and(i == 0, j == 0))
        def _():
            rider.first(ride)

        @pl.when(j == 0)
        def _():
            dx_ref[...] = jnp.zeros_like(dx_ref)

        da = (_dot_nt(dh_ref[...], wd_ref[...]) * (2.0 * r_ref[...].astype(F32))).astype(BF16)
        da_ref[...] = da
        dx_ref[...] += _dot_nt(da, wu_ref[0])

        @pl.when(jnp.logical_and(i == n_t - 1, j == N_CHIPS - 1))
        def _():
            rider.last(ride)

    return rider.call(
        body, [rb, dhb, w_up_s, w_down], grid=(n_t, N_CHIPS), name="mlp_bwd",
        in_specs=[pl.BlockSpec((tm, fc), lambda i, j: (i, j)), pl.BlockSpec((tm, D), lambda i, j: (i, 0)),
                  pl.BlockSpec((1, D, fc), lambda i, j: (j, 0, 0)),
                  pl.BlockSpec((fc, D), lambda i, j: (j, 0))],
        out_specs=[pl.BlockSpec((tm, D), lambda i, j: (i, 0)), pl.BlockSpec((tm, fc), lambda i, j: (i, j))],
        out_shape=[_sds((S, D), F32), _sds((S, D_FF), BF16)],
        scratch_shapes=[], vmem_mib=56)


def _tn_matmul(a, b, name, tk, tt, stacked, rider, square_a=False):
    T, K = a.shape
    N = b.shape[1]
    tn = 1024
    grid = (K // tk, N // tn, T // tt)

    def body(*refs):
        (a_ref, b_ref), (o_ref,), _, ride = rider.split(refs, 2, 1, 0)
        at = [pl.program_id(d) for d in range(3)]

        @pl.when(jnp.logical_and(jnp.logical_and(at[0] == 0, at[1] == 0), at[2] == 0))
        def _():
            rider.first(ride)

        @pl.when(at[2] == 0)
        def _():
            o_ref[...] = jnp.zeros_like(o_ref)

        a_t = a_ref[...]
        if square_a:
            a_t = a_t * a_t
        prod = _dot_tn(a_t, b_ref[...])
        if stacked:
            o_ref[0] += prod
        else:
            o_ref[...] += prod

        @pl.when(jnp.logical_and(jnp.logical_and(at[0] == grid[0] - 1, at[1] == grid[1] - 1),
                                 at[2] == grid[2] - 1))
        def _():
            rider.last(ride)

    if stacked:
        out_spec = pl.BlockSpec((1, tk, tn), lambda k, n, t: (n, k, 0))
        out_shape = _sds((N // tn, K, tn), F32)
    else:
        out_spec = pl.BlockSpec((tk, tn), lambda k, n, t: (k, n))
        out_shape = _sds((K, N), F32)
    return rider.call(
        body, [a, b], grid=grid, name=name,
        in_specs=[pl.BlockSpec((tt, tk), lambda k, n, t: (t, k)),
                  pl.BlockSpec((tt, tn), lambda k, n, t: (t, n))],
        out_specs=[out_spec], out_shape=[out_shape], scratch_shapes=[], vmem_mib=58)


def _mix_bwd(dpre2, dx1m, xh1, rstd1, g1, w_out, on, pooled, tm, rider):
    S, D = xh1.shape
    n_t = S // tm

    def body(*refs):
        ((dp2_ref, dxm_ref, xh_ref, rstd_ref, g_ref, w_ref, on_ref, po_ref),
         (dpre1_ref, don_ref, dpo_ref, dw_ref, dg_ref, db_ref), _, ride) = rider.split(refs, 8, 6, 0)

        @pl.when(pl.program_id(0) == 0)
        def _():
            rider.first(ride)
            for r in (dw_ref, dg_ref, db_ref):
                r[...] = jnp.zeros_like(r)

        xh = xh_ref[...]
        dx1 = ALPHA * dp2_ref[...] + dxm_ref[...]
        dg_ref[...] += _colsum(dx1 * xh)
        db_ref[...] += _colsum(dx1)
        dpre1 = _ln_bwd(dx1, xh, rstd_ref[...], g_ref[...])
        dpre1_ref[...] = dpre1
        dmb = dpre1.astype(BF16)
        dcat = _dot_nt(dmb, w_ref[...])
        don_ref[...] = dcat[:, :ATTN_WIDTH]
        dpo_ref[...] = dcat[:, ATTN_WIDTH:]
        dw_ref[:ATTN_WIDTH, :] += _dot_tn(on_ref[...], dmb)
        dw_ref[ATTN_WIDTH:, :] += _dot_tn(po_ref[...], dmb)

        @pl.when(pl.program_id(0) == n_t - 1)
        def _():
            rider.last(ride)

    vec = _const_spec((1, D))
    return rider.call(
        body, [dpre2, dx1m, xh1, rstd1, g1, w_out, on, pooled], grid=(n_t,), name="mix_bwd",
        in_specs=[_row_spec(tm, D), _row_spec(tm, D), _row_spec(tm, D), _row_spec(tm, 1), vec,
                  _const_spec((D, D)), _row_spec(tm, ATTN_WIDTH), _row_spec(tm, POOL_WIDTH)],
        out_specs=[_row_spec(tm, D), _row_spec(tm, ATTN_WIDTH), _row_spec(tm, POOL_WIDTH),
                   _const_spec((D, D)), vec, vec],
        out_shape=[_sds((S, D), F32), _sds((S, ATTN_WIDTH), F32), _sds((S, POOL_WIDTH), F32),
                   _sds((D, D), F32), _sds((1, D), F32), _sds((1, D), F32)],
        scratch_shapes=[], vmem_mib=56)


def _pool_bwd(dpooled, d_b, w_pool, pscale, tm, rider):
    S = dpooled.shape[0]
    hb = tm // POOL_HALO
    n_t = S // tm
    te = tm + POOL_HALO

    def body(*refs):
        ((dp_ref, dph_ref, d_ref, wp_ref, sc_ref), (du_ref, dwp_ref, dsc_ref), _,
         ride) = rider.split(refs, 5, 3, 0)
        i = pl.program_id(0)

        @pl.when(i == 0)
        def _():
            rider.first(ride)
            dwp_ref[...] = jnp.zeros_like(dwp_ref)
            dsc_ref[...] = jnp.zeros_like(dsc_ref)

        halo = jnp.where(i < n_t - 1, dph_ref[...], 0.0)
        pos = i * tm + lax.broadcasted_iota(jnp.int32, (te, 1), 0)
        for g in range(N_POOL_GROUPS):
            win = 2 ** (g + 1)
            cols = slice(g * POOL_GROUP, (g + 1) * POOL_GROUP)
            wpb = wp_ref[g].astype(BF16)
            dpt = dp_ref[:, cols]
            dpe = jnp.concatenate([dpt, halo[:, cols]], axis=0)
            dyb = (dpe * sc_ref[:, cols]).astype(BF16)
            dd = _dot_nt(dyb, wpb)
            s = dd / jnp.minimum(pos + 1, win).astype(F32)
            for sh in (1, 2, 4, 8)[:g + 1]:
                s = s + pltpu.roll(s, te - sh, 0)
            du_ref[:, cols] = s[:tm, :] - dd[:tm, :]
            db = d_ref[:, cols]
            dwp_ref[g] += _dot_tn(db, dyb[:tm, :])
            dsc_ref[:, cols] += _colsum(dpt * _dot(db, wpb))

        @pl.when(i == n_t - 1)
        def _():
            rider.last(ride)

    return rider.call(
        body, [dpooled, dpooled, d_b, w_pool, pscale], grid=(n_t,), name="pool_bwd",
        in_specs=[_row_spec(tm, POOL_WIDTH),
                  pl.BlockSpec((POOL_HALO, POOL_WIDTH),
                               lambda i: (jnp.minimum((i + 1) * hb, S // POOL_HALO - 1), 0)),
                  _row_spec(tm, POOL_WIDTH),
                  _const_spec((N_POOL_GROUPS, POOL_GROUP, POOL_GROUP)), _const_spec((1, POOL_WIDTH))],
        out_specs=[_row_spec(tm, POOL_WIDTH), _const_spec((N_POOL_GROUPS, POOL_GROUP, POOL_GROUP)),
                   _const_spec((1, POOL_WIDTH))],
        out_shape=[_sds((S, POOL_WIDTH), F32), _sds((N_POOL_GROUPS, POOL_GROUP, POOL_GROUP), F32),
                   _sds((1, POOL_WIDTH), F32)],
        scratch_shapes=[], vmem_mib=32)


def _attn_bwd(q, k, v, don, o_raw, ga, tq, rider):
    S = q.shape[0]
    nq = S // tq

    def body(*refs):
        ((q_ref, k_ref, v_ref, don_ref, o_ref, ga_ref), (dq_ref, dk_ref, dv_ref, dga_ref),
         (g_s, b_s, low_s, upp_s), ride) = rider.split(refs, 6, 4, 4)
        p, i = pl.program_id(0), pl.program_id(1)

        @pl.when(jnp.logical_and(p == 0, i == 0))
        def _():
            rider.first(ride)
            low_s[...] = _tri(tq, upper=False)
            upp_s[...] = _tri(tq, upper=True)

        @pl.when(i == 0)
        def _():
            for r in (dk_ref, dv_ref, dga_ref):
                r[...] = jnp.zeros_like(r)

        lane = lax.broadcasted_iota(jnp.int32, (1, PAIR), 1)
        m0 = lane < HEAD_DIM
        low = low_s[...]
        upp = upp_s[...]

        def seg_mean(a):
            s0 = jnp.sum(jnp.where(m0, a, 0.0), axis=-1, keepdims=True)
            s1 = jnp.sum(jnp.where(m0, 0.0, a), axis=-1, keepdims=True)
            return jnp.where(m0, s0, s1) * (1.0 / HEAD_DIM)

        o = o_ref[...]
        rs = lax.rsqrt(seg_mean(o * o) + RMS_EPS)
        oh = o * rs
        don = don_ref[...]
        dga_ref[...] += _colsum(don * oh)
        doh = don * ga_ref[...]
        do = rs * (doh - oh * seg_mean(doh * oh))
        dob = do.astype(BF16)
        q2 = q_ref[...]
        qhs = [jnp.where(m0, q2, jnp.zeros_like(q2)), jnp.where(m0, jnp.zeros_like(q2), q2)]
        dhs = [jnp.where(m0, dob, jnp.zeros_like(dob)), jnp.where(m0, jnp.zeros_like(dob), dob)]
        causal = _strictly_causal(tq)

        def down(kb, c_ls, valid):
            ks = pl.multiple_of(kb * tq, tq)
            kt = k_ref[pl.ds(ks, tq), :]
            vt = v_ref[pl.ds(ks, tq), :]
            lss, ls_, ws = _sb_tile(qhs, kt, low, c_ls, valid)
            dws = [_dot_nt(dh, vt) for dh in dhs]
            for hh in range(2):
                g_s[hh, kb] = dws[hh] * ws[hh]
                b_s[hh, kb] = jnp.exp(lss[hh])
            dv_ref[pl.ds(ks, tq), :] += (_dot_tn(ws[0].astype(BF16), dhs[0])
                                         + _dot_tn(ws[1].astype(BF16), dhs[1]))
            return [c_l + jnp.sum(l, axis=1, keepdims=True) for c_l, l in zip(c_ls, ls_)]

        zc, za = jnp.zeros((tq, 1), F32), jnp.zeros((tq, PAIR), F32)
        def first_two():
            c_ls = down(i - 1, down(i, [zc, zc], causal), None)
            return (_weights_left(c_ls), *c_ls)

        st0 = lax.cond(i >= 1, first_two, lambda: (jnp.int32(0), *down(i, [zc, zc], causal)))

        def more(st):
            return jnp.logical_and(st[0] <= i, st[1] > 0)

        def down_step(st):
            c_ls = down(i - st[0], [st[2], st[3]], None)
            return (st[0] + 1, _weights_left(c_ls), c_ls[0], c_ls[1])

        n_tiles = lax.while_loop(more, down_step, (jnp.int32(2), *st0))[0]

        def up(kb, c_gs, accs, valid):
            ks = pl.multiple_of(kb * tq, tq)
            kt = k_ref[pl.ds(ks, tq), :]
            gs = [g_s[hh, kb] for hh in range(2)]
            pres = [_dot(g.astype(BF16), upp) + c_g for g, c_g in zip(gs, c_gs)]
            dzs = []
            for hh in range(2):
                beta = b_s[hh, kb]
                dz = gs[hh] - beta * (gs[hh] + pres[hh])
                if valid is not None:
                    dz = jnp.where(valid, dz, 0.0)
                dzs.append(dz.astype(BF16))
            new_a = [acc + _dot(dzb, kt) for acc, dzb in zip(accs, dzs)]
            dk_ref[pl.ds(ks, tq), :] += _dot_tn(dzs[0], qhs[0]) + _dot_tn(dzs[1], qhs[1])
            new_c = [c_g + jnp.sum(g, axis=1, keepdims=True) for c_g, g in zip(c_gs, gs)]
            return new_c, new_a

        def up_step(kb, st):
            c_gs, accs = up(kb, [st[0], st[1]], [st[2], st[3]], None)
            return (c_gs[0], c_gs[1], accs[0], accs[1])

        st = lax.fori_loop(i - n_tiles + 1, i - 1, up_step, (zc, zc, za, za))

        def last_two():
            c_gs, accs = up(i - 1, [st[0], st[1]], [st[2], st[3]], None)
            return tuple(up(i, c_gs, accs, causal)[1])

        accs = lax.cond(i >= 1, last_two, lambda: tuple(up(i, [zc, zc], [za, za], causal)[1]))
        dq_ref[...] = jnp.where(m0, accs[0], accs[1]) * Q_SCALE

        @pl.when(jnp.logical_and(p == N_PAIRS - 1, i == nq - 1))
        def _():
            rider.last(ride)

    return rider.call(
        body, [q, k, v, don, o_raw, ga], grid=(N_PAIRS, nq), name="attn_bwd",
        in_specs=[pl.BlockSpec((tq, PAIR), lambda p, i: (i, p)),
                  pl.BlockSpec((S, PAIR), lambda p, i: (0, p)),
                  pl.BlockSpec((S, PAIR), lambda p, i: (0, p)),
                  pl.BlockSpec((tq, PAIR), lambda p, i: (i, p)),
                  pl.BlockSpec((tq, PAIR), lambda p, i: (i, p)),
                  pl.BlockSpec((1, PAIR), lambda p, i: (0, p))],
        out_specs=[pl.BlockSpec((tq, PAIR), lambda p, i: (i, p)),
                   pl.BlockSpec((S, PAIR), lambda p, i: (0, p)),
                   pl.BlockSpec((S, PAIR), lambda p, i: (0, p)),
                   pl.BlockSpec((1, PAIR), lambda p, i: (0, p))],
        out_shape=[_sds((S, ATTN_WIDTH), F32), _sds((S, ATTN_WIDTH), F32), _sds((S, ATTN_WIDTH), F32),
                   _sds((1, ATTN_WIDTH), F32)],
        scratch_shapes=[pltpu.VMEM((2, nq, tq, tq), F32), pltpu.VMEM((2, nq, tq, tq), F32),
                        pltpu.VMEM((tq, tq), BF16), pltpu.VMEM((tq, tq), BF16)],
        vmem_mib=56)


def _inproj_bwd(dq, dk, dv, du, dpre1, xh0, rstd0, g0, b0, w_in_s, tm):
    S, D = xh0.shape

    def body(dq_ref, dk_ref, dv_ref, du_ref, dp1_ref, xh_ref, rstd_ref, g_ref, b_ref, w_ref,
             gx_ref, dw_ref, dg_ref, db_ref):
        @pl.when(pl.program_id(0) == 0)
        def _():
            for r in (dw_ref, dg_ref, db_ref):
                r[...] = jnp.zeros_like(r)

        xh = xh_ref[...]
        xb = (xh * g_ref[...] + b_ref[...]).astype(BF16)
        dx0 = ALPHA * dp1_ref[...]
        for c, r in enumerate((dq_ref, dk_ref, dv_ref, du_ref)):
            dpb = r[...].astype(BF16)
            dx0 = dx0 + _dot_nt(dpb, w_ref[c])
            dw_ref[c] += _dot_tn(xb, dpb)
        dg_ref[...] += _colsum(dx0 * xh)
        db_ref[...] += _colsum(dx0)
        gx_ref[...] = _ln_bwd(dx0, xh, rstd_ref[...], g_ref[...])

    vec = _const_spec((1, D))
    half = _row_spec(tm, 512)
    return pl.pallas_call(
        body, grid=(S // tm,), name="inproj_bwd",
        in_specs=[half, half, half, half, _row_spec(tm, D), _row_spec(tm, D), _row_spec(tm, 1), vec, vec,
                  _const_spec((N_CHIPS, D, 512))],
        out_specs=[_row_spec(tm, D), _const_spec((N_CHIPS, D, 512)), vec, vec],
        out_shape=[_sds((S, D), F32), _sds((N_CHIPS, D, 512), F32), _sds((1, D), F32), _sds((1, D), F32)],
        compiler_params=_cp(58),
    )(*_hbm(dq, dk, dv, du, dpre1, xh0, rstd0, g0, b0, w_in_s))


def _place():
    return lax.axis_index("x"), lax.axis_index("y"), lax.axis_index("c")


CHIP_FLIPS = ((0, 1), (1, 0), (1, 1))


class _Rider:
    def __init__(self, ins, out_shapes, n_sem, phases, aliases=None):
        self.ins, self.out_shapes, self.n_sem, self.phases = list(ins), list(out_shapes), n_sem, phases
        self.aliases = aliases or {}

    def __add__(self, other):
        na, ma = len(self.ins), len(self.out_shapes)

        def phases(ins, outs, ssem, rsem):
            mine = self.phases(ins[:na], outs[:ma], ssem, rsem)
            rest = pl.ds(self.n_sem, other.n_sem)
            theirs = other.phases(ins[na:], outs[ma:], ssem.at[rest], rsem.at[rest])
            assert len(mine) == 1 and len(theirs) == 1
            return [mine[0] + theirs[0]]

        aliases = {**self.aliases, **{na + i: ma + o for i, o in other.aliases.items()}}
        return _Rider(self.ins + other.ins, self.out_shapes + other.out_shapes, self.n_sem + other.n_sem, phases,
                      aliases)

    def split(self, refs, n_in, n_out, n_scratch):
        a = n_in + len(self.ins)
        b = a + n_out
        c = b + len(self.out_shapes)
        own = (refs[:n_in], refs[a:b], refs[c:c + n_scratch])
        return own + ((refs[n_in:a], refs[b:c]) + tuple(refs[c + n_scratch:]),)

    def first(self, ride):
        for make in self.phases(*ride)[0]:
            make().start()

    def mid(self, ride):
        ph = self.phases(*ride)
        if len(ph) == 2:
            for make in ph[0]:
                make().wait_recv()
            for make in ph[1]:
                make().start()

    def last(self, ride):
        ph = self.phases(*ride)
        if len(ph) == 2:
            for make in ph[0]:
                make().wait_send()
        for make in ph[-1]:
            make().wait()

    def call(self, body, args, *, grid, name, in_specs, out_specs, out_shape, scratch_shapes, vmem_mib,
             prefetch=None):
        n_in, n_out = len(in_specs), len(out_specs)
        sems = [pltpu.SemaphoreType.DMA((self.n_sem,)), pltpu.SemaphoreType.DMA((self.n_sem,))]
        n_pre = 0 if prefetch is None else 1
        grid_spec = pltpu.PrefetchScalarGridSpec(
            num_scalar_prefetch=n_pre, grid=grid,
            in_specs=list(in_specs) + [HBM_SPEC] * len(self.ins),
            out_specs=list(out_specs) + [HBM_SPEC] * len(self.out_shapes),
            scratch_shapes=list(scratch_shapes) + sems)
        return pl.pallas_call(
            body, name=name, grid_spec=grid_spec,
            out_shape=list(out_shape) + self.out_shapes,
            input_output_aliases={n_pre + n_in + i: n_out + o for i, o in self.aliases.items()},
            compiler_params=_cp(vmem_mib),
        )(*([] if prefetch is None else [prefetch]), *_hbm(*args), *self.ins)

    def run(self, name):
        def body(*refs):
            ride = self.split(refs, 0, 0, 0)[3]
            self.first(ride)
            self.mid(ride)
            self.last(ride)

        return self.call(body, [], grid=(), name=name, in_specs=[], out_specs=[], out_shape=[],
                         scratch_shapes=[], vmem_mib=16)


def _remote(src, dst, ssem, rsem, n, dev):
    return functools.partial(pltpu.make_async_remote_copy, src_ref=src, dst_ref=dst, send_sem=ssem.at[n],
                             recv_sem=rsem.at[n], device_id=dev, device_id_type=MESH)


def _cast_into_slot(w, place, name):
    R, C = w.shape
    tr = min(R, 512)

    def body(pl_ref, w_ref, o_ref):
        o_ref[0] = w_ref[...].astype(BF16)

    return pl.pallas_call(
        body, name=name,
        grid_spec=pltpu.PrefetchScalarGridSpec(
            num_scalar_prefetch=1, grid=(R // tr,),
            in_specs=[pl.BlockSpec((tr, C), lambda r, pr: (r, 0))],
            out_specs=pl.BlockSpec((1, tr, C), lambda r, pr: (pr[1], r, 0))),
        out_shape=_sds((N_CHIPS, R, C), BF16),
    )(place, w)


CAST_STEPS = 8


def _cast_rest(ws, place, rider):
    n = len(ws)

    def body(pl_ref, *refs):
        w_refs, o_refs, _, ride = rider.split(refs, n, n, 0)
        r = pl.program_id(0)

        @pl.when(r == 0)
        def _():
            rider.first(ride)

        @pl.when(r == CAST_STEPS // 2)
        def _():
            rider.mid(ride)

        for w_ref, o_ref in zip(w_refs, o_refs):
            o_ref[0] = w_ref[...].astype(BF16)

        @pl.when(r == CAST_STEPS - 1)
        def _():
            rider.last(ride)

    def rows(w):
        return w.shape[0] // CAST_STEPS

    return rider.call(
        body, ws, grid=(CAST_STEPS,), name="cast_weights", prefetch=place,
        in_specs=[pl.BlockSpec((rows(w), w.shape[1]), lambda r, pr: (r, 0)) for w in ws],
        out_specs=[pl.BlockSpec((1, rows(w), w.shape[1]), lambda r, pr: (pr[1], r, 0)) for w in ws],
        out_shape=[_sds((N_CHIPS,) + w.shape, BF16) for w in ws], scratch_shapes=[], vmem_mib=32)


def _gather_rider(stacked, part="both"):
    n, nf = len(stacked), len(CHIP_FLIPS)

    def phases(ins, outs, ssem, rsem):
        x, y, c = _place()
        slot = 2 * x + y
        ici, d2d = [], []
        for w, (i_ref, o_ref) in enumerate(zip(ins, outs)):
            hh = o_ref.shape[1] // 2
            rows = pl.ds(c * hh, hh)
            for f, (fx, fy) in enumerate(CHIP_FLIPS):
                k = w * nf + f
                theirs = 2 * (x ^ fx) + (y ^ fy)
                if part != "pair":
                    ici.append(_remote(i_ref.at[slot, rows], o_ref.at[slot, rows], ssem, rsem, k,
                                       (x ^ fx, y ^ fy, c)))
                if part != "chips":
                    d2d.append(_remote(o_ref.at[theirs, rows], o_ref.at[theirs, rows], ssem, rsem,
                                       (n * nf if part == "both" else 0) + k, (x, y, 1 - c)))
        return [ph for ph in (ici, d2d) if ph]

    return _Rider(stacked, [_sds(s.shape, s.dtype) for s in stacked], (2 if part == "both" else 1) * n * nf,
                  phases, aliases={i: i for i in range(n)})


def _pair_swap_rider(grads):
    def phases(ins, outs, ssem, rsem):
        x, y, c = _place()
        return [[_remote(g.at[:, 1 - c], o, ssem, rsem, k, (x, y, 1 - c))
                 for k, (g, o) in enumerate(zip(ins, outs))]]

    return _Rider(grads, [_sds((N_CHIPS,) + g.shape[2:], g.dtype) for g in grads], len(grads), phases)


def _chip_scatter_rider(parts):
    nf = len(CHIP_FLIPS)

    def phases(ins, outs, ssem, rsem):
        x, y, c = _place()
        return [[_remote(r.at[2 * (x ^ fx) + (y ^ fy)], o.at[f], ssem, rsem, w * nf + f, (x ^ fx, y ^ fy, c))
                 for w, (r, o) in enumerate(zip(ins, outs)) for f, (fx, fy) in enumerate(CHIP_FLIPS)]]

    return _Rider(parts, [_sds((nf,) + r.shape[1:], r.dtype) for r in parts], len(parts) * nf, phases)


def _pair_send_rider(halves):
    def phases(ins, outs, ssem, rsem):
        x, y, c = _place()
        return [[_remote(h, o, ssem, rsem, k, (x, y, 1 - c)) for k, (h, o) in enumerate(zip(ins, outs))]]

    return _Rider(halves, [_sds(h.shape, h.dtype) for h in halves], len(halves), phases)


PAIR_SUM_STEPS = 2
CHIP_SUM_STEPS = 4
ADAMW_STEPS = 4


def _no_rider():
    return _Rider([], [], 1, lambda ins, outs, ssem, rsem: [[]])


def _add_pair(grads, recvs, place, name, rider):
    n = len(grads)

    def body(pl_ref, *refs):
        ins, outs, _, ride = rider.split(refs, 2 * n, 2 * n, 0)
        j, h = pl.program_id(0), pl.program_id(1)

        @pl.when(jnp.logical_and(j == 0, h == 0))
        def _():
            rider.first(ride)

        for w in range(n):
            s = ins[2 * w][:, 0] + ins[2 * w + 1][...]
            outs[2 * w][...] = s
            outs[2 * w + 1][...] = s.astype(BF16)

        @pl.when(jnp.logical_and(j == N_CHIPS - 1, h == PAIR_SUM_STEPS - 1))
        def _():
            rider.last(ride)

    in_specs, out_specs, out_shape, args = [], [], [], []
    for g, r in zip(grads, recvs):
        _, _, H, C = g.shape
        th = H // PAIR_SUM_STEPS
        spec = pl.BlockSpec((1, th, C), lambda j, h, pr: (j, h, 0))
        in_specs += [pl.BlockSpec((1, 1, th, C), lambda j, h, pr: (j, pr[0], h, 0)), spec]
        out_specs += [spec, spec]
        out_shape += [_sds((N_CHIPS, H, C), F32), _sds((N_CHIPS, H, C), BF16)]
        args += [g, r]
    res = rider.call(body, args, grid=(N_CHIPS, PAIR_SUM_STEPS), name=name, prefetch=place, in_specs=in_specs,
                     out_specs=out_specs, out_shape=out_shape, scratch_shapes=[], vmem_mib=32)
    return [(res[2 * w], res[2 * w + 1]) for w in range(n)], res[2 * n:]


def _add_chips(parts, recvs, place, name, rider):
    n = len(parts)

    def body(pl_ref, *refs):
        ins, outs, _, ride = rider.split(refs, 2 * n, n, 0)
        h = pl.program_id(0)

        @pl.when(h == 0)
        def _():
            rider.first(ride)

        for w in range(n):
            p_ref, r_ref = ins[2 * w], ins[2 * w + 1]
            outs[w][...] = p_ref[0] + r_ref[0].astype(F32) + r_ref[1].astype(F32) + r_ref[2].astype(F32)

        @pl.when(h == CHIP_SUM_STEPS - 1)
        def _():
            rider.last(ride)

    in_specs, out_specs, out_shape, args = [], [], [], []
    for p, r in zip(parts, recvs):
        _, H, C = p.shape
        th = H // CHIP_SUM_STEPS
        in_specs += [pl.BlockSpec((1, th, C), lambda h, pr: (pr[1], h, 0)),
                     pl.BlockSpec((len(CHIP_FLIPS), th, C), lambda h, pr: (0, h, 0))]
        out_specs.append(pl.BlockSpec((th, C), lambda h, pr: (h, 0)))
        out_shape.append(_sds((H, C), F32))
        args += [p, r]
    res = rider.call(body, args, grid=(CHIP_SUM_STEPS,), name=name, prefetch=place, in_specs=in_specs,
                     out_specs=out_specs, out_shape=out_shape, scratch_shapes=[], vmem_mib=32)
    return res[:n], res[n:]


def _adamw_math(w, g, m, v):
    m = ADAM_B1 * m + (1.0 - ADAM_B1) * g
    v = ADAM_B2 * v + (1.0 - ADAM_B2) * (g * g)
    m_hat = m / (1.0 - ADAM_B1 ** ADAM_STEP)
    v_hat = v / (1.0 - ADAM_B2 ** ADAM_STEP)
    delta = -ADAM_LR * (m_hat / (jnp.sqrt(v_hat) + ADAM_EPS) + ADAM_WD * w)
    return delta, m, v


def _adamw(ws, mines, theirs, ms, vs, place, name, rider):
    n = len(ws)

    def body(pl_ref, *refs):
        ins, outs, _, ride = rider.split(refs, 5 * n, 4 * n, 0)
        h, r = pl.program_id(0), pl.program_id(1)

        @pl.when(jnp.logical_and(h == 0, r == 0))
        def _():
            rider.first(ride)

        for k in range(n):
            w_ref, a_ref, b_ref, m_ref, v_ref = ins[5 * k:5 * k + 5]
            g = jnp.where(h == pl_ref[0], a_ref[...], b_ref[...])
            d, mo, vo = _adamw_math(w_ref[...], g, m_ref[...], v_ref[...])
            for o_ref, val in zip(outs[4 * k:4 * k + 4], (g, d, mo, vo)):
                o_ref[...] = val

        @pl.when(jnp.logical_and(h == 1, r == ADAMW_STEPS - 1))
        def _():
            rider.last(ride)

    in_specs, out_specs, out_shape, args = [], [], [], []
    for w, a, b, m, v in zip(ws, mines, theirs, ms, vs):
        R, C = w.shape
        th = (R // 2) // ADAMW_STEPS
        whole = pl.BlockSpec((th, C), lambda h, r, pr: (h * ADAMW_STEPS + r, 0))
        mine_spec = pl.BlockSpec((th, C), lambda h, r, pr: (jnp.where(h == pr[0], r, 0), 0))
        theirs_spec = pl.BlockSpec((th, C), lambda h, r, pr: (jnp.where(h == pr[0], 0, r), 0))
        in_specs += [whole, mine_spec, theirs_spec, whole, whole]
        out_specs += [whole] * 4
        out_shape += [_sds((R, C), F32)] * 4
        args += [w, a, b, m, v]
    res = rider.call(body, args, grid=(2, ADAMW_STEPS), name=name, prefetch=place, in_specs=in_specs,
                     out_specs=out_specs, out_shape=out_shape, scratch_shapes=[], vmem_mib=40)
    return [tuple(res[4 * k:4 * k + 4]) for k in range(n)], res[4 * n:]


DEVICE_FLIPS = tuple((fx, fy, fc) for fx in (0, 1) for fy in (0, 1) for fc in (0, 1))[1:]


def _pack_exchange_rider(pack):
    def phases(ins, outs, ssem, rsem):
        x, y, c = _place()
        mine = outs[0].at[4 * x + 2 * y + c]
        copies = [_remote(ins[0], mine, ssem, rsem, k, (x ^ fx, y ^ fy, c ^ fc))
                  for k, (fx, fy, fc) in enumerate(DEVICE_FLIPS)]
        copies.append(functools.partial(pltpu.make_async_copy, ins[0], mine, ssem.at[len(DEVICE_FLIPS)]))
        return [copies]

    return _Rider([pack], [_sds((N_DEV,) + pack.shape, pack.dtype)], len(DEVICE_FLIPS) + 1, phases)


def _small_sum_adamw(recv_a, recv_b, wpack, mpack, vpack):
    R = wpack.shape[0]

    def body(a_ref, b_ref, w_ref, m_ref, v_ref, gs_ref, d_ref, mo_ref, vo_ref):
        ta, tb = a_ref[0], b_ref[0]
        for dev in range(1, N_DEV):
            ta = ta + a_ref[dev]
            tb = tb + b_ref[dev]
        total = jnp.concatenate([ta, tb], axis=0)
        gs_ref[...] = total
        d, mo, vo = _adamw_math(w_ref[...], total, m_ref[...], v_ref[...])
        d_ref[...] = d
        mo_ref[...] = mo
        vo_ref[...] = vo

    return pl.pallas_call(
        body, name="small_sum_adamw", in_specs=[VMEM_SPEC] * 5, out_specs=[VMEM_SPEC] * 4,
        out_shape=[_sds((R, LANES), F32)] * 4,
    )(recv_a, recv_b, wpack, mpack, vpack)


def _rows8(a):
    a = a.reshape(-1, LANES)
    pad = (-a.shape[0]) % 8
    return jnp.pad(a, ((0, pad), (0, 0))) if pad else a


def _pack(parts):
    return jnp.concatenate([_rows8(a) for a in parts], axis=0)


def _unpack(pack, like):
    out, row = [], 0
    for a in like:
        n = a.size // LANES
        out.append(pack[row:row + n].reshape(a.shape))
        row += n + (-n) % 8
    return out


def kernel(x, p, emb_ln_g, emb_ln_b, w_in, attn_out_g, w_pool, pool_scale, w_out, ln1_g, ln1_b, w_up, w_down, ln2_g, ln2_b, w_ple, w_ple_gate, ln3_g, ln3_b, loss_target, m_emb_ln_g, m_emb_ln_b, m_w_in, m_attn_out_g, m_w_pool, m_pool_scale, m_w_out, m_ln1_g, m_ln1_b, m_w_up, m_w_down, m_ln2_g, m_ln2_b, m_w_ple, m_w_ple_gate, m_ln3_g, m_ln3_b, v_emb_ln_g, v_emb_ln_b, v_w_in, v_attn_out_g, v_w_pool, v_pool_scale, v_w_out, v_ln1_g, v_ln1_b, v_w_up, v_w_down, v_ln2_g, v_ln2_b, v_w_ple, v_w_ple_gate, v_ln3_g, v_ln3_b):
    S = x.shape[1]
    tq = min(256, S)
    tm_mlp = min(1024, S)
    tm_pool = min(1024, S)
    tm_ln = min(512, S)
    tm_fwd = min(1024, S)
    xs = x[0]
    ps = p[0, 0]
    tgt = loss_target[0]
    row = lambda a: a.reshape(1, -1)
    g0, b0 = row(emb_ln_g), row(emb_ln_b)
    g1, b1, g2, b2, g3, b3 = ln1_g, ln1_b, ln2_g, ln2_b, ln3_g, ln3_b
    wp = w_pool[0]

    xi, yi, ci = _place()
    place = jnp.stack([ci, 2 * xi + yi]).astype(jnp.int32)
    names = ["w_in", "w_out", "w_up", "w_down", "w_ple", "w_ple_gate"]

    big = [w_in[0], w_out[0], w_up[0], w_down[0], w_ple[0], w_ple_gate[0]]
    s_in = _cast_into_slot(big[0], place, "cast_w_in")
    s_out, s_up, s_down, s_ple, s_gate, w_in_s = _cast_rest(big[1:], place, _gather_rider([s_in]))

    xh0, rstd0, q, k, v, u, s_out, s_ple, s_gate = _embln_inproj(
        xs, g0, b0, w_in_s, tm_fwd, _gather_rider([s_out, s_ple, s_gate], "chips"))
    o_raw, on, s_up, s_down, w_out_s, w_ple_s, w_gate_s = _attn_fwd(
        q, k, v, attn_out_g, tq, _gather_rider([s_up, s_down], "chips") + _gather_rider([s_out, s_ple, s_gate], "pair"))
    w_out_f = w_out_s.reshape(D_MODEL, D_MODEL)
    w_gate_f = w_gate_s.reshape(D_MODEL, D_MODEL)
    d_b, pooled = _pool_fwd(u, wp, pool_scale, tm_pool)
    xh1, rstd1, x1b, w_up_s, w_down_s = _mix_ln1(on, pooled, xh0, g0, b0, w_out_f, g1, b1, tm_fwd,
                                                 _gather_rider([s_up, s_down], "pair"))
    w_down_f = w_down_s.reshape(D_FF, D_MODEL)
    xh2, rstd2, rb = _mlp_ln2(xh1, x1b, g1, b1, w_up_s, w_down_f, tm_mlp, _no_rider())

    (dpre2, dhb, dw_ple, dw_gate, dg3, db3, dg2, db2, loss_row) = _ple_ln3_loss(
        xh2, rstd2, g2, b2, ps, w_ple_s, w_gate_f, g3, b3, tgt, tm_ln)
    def halves_of(g):
        return g.reshape(N_CHIPS, 2, g.shape[1] // 2, g.shape[2])

    ple_halves = [halves_of(dw_ple), halves_of(dw_gate.reshape(N_CHIPS, D_MODEL // N_CHIPS, D_MODEL))]
    dx1m, da, *ple_pair = _mlp_bwd(rb, dhb, w_up_s, w_down_f, tm_mlp, _pair_swap_rider(ple_halves))
    (dw_up,) = _tn_matmul(x1b, da, "grad_w_up", 1024, S, True, _no_rider())
    up_halves = halves_of(dw_up)
    dw_down, up_pair = _tn_matmul(rb, dhb, "grad_w_down", 1024, S, False,
                                  _pair_swap_rider([up_halves]), square_a=True)
    down_halves = halves_of(dw_down.reshape(N_CHIPS, D_FF // N_CHIPS, D_MODEL))
    dpre1, don, dpooled, dw_out, dg1, db1, down_pair = _mix_bwd(
        dpre2, dx1m, xh1, rstd1, g1, w_out_f, on, pooled, tm_ln, _pair_swap_rider([down_halves]))
    out_halves = halves_of(dw_out.reshape(N_CHIPS, D_MODEL // N_CHIPS, D_MODEL))
    du, dwp, dsc, out_pair = _pool_bwd(dpooled, d_b, wp, pool_scale, tm_pool, _pair_swap_rider([out_halves]))
    early_sum, _ = _add_pair(
        [out_halves, up_halves, down_halves] + ple_halves, [out_pair, up_pair, down_pair] + ple_pair, place,
        "pair_sum_early", _no_rider())
    pack_a = _pack([jnp.broadcast_to(loss_row, (8, LANES)), dwp, dsc, dg1, db1, dg2, db2, dg3, db3])
    riding = _chip_scatter_rider([b for _, b in early_sum]) + _pack_exchange_rider(pack_a)
    dq, dk, dv, dga, *arrived = _attn_bwd(q, k, v, don, o_raw, attn_out_g, tq, riding)
    early_chips, recv_a = arrived[:-1], arrived[-1]
    grad_x, dw_in, dg0, db0 = _inproj_bwd(dq, dk, dv, du, dpre1, xh0, rstd0, g0, b0, w_in_s, tm_ln)

    in_halves = halves_of(dw_in)
    pack_b = _pack([dg0, db0, dga])
    early_mine, (in_pair, recv_b) = _add_chips(
        [s for s, _ in early_sum], early_chips, place, "chip_sum_early",
        _pair_swap_rider([in_halves]) + _pack_exchange_rider(pack_b))
    (in_sum,), early_theirs = _add_pair([in_halves], [in_pair], place, "pair_sum_w_in", _pair_send_rider(early_mine))
    ms = [m_w_in, m_w_out, m_w_up, m_w_down, m_w_ple, m_w_ple_gate]
    vs = [v_w_in, v_w_out, v_w_up, v_w_down, v_w_ple, v_w_ple_gate]
    early_res, _ = _adamw(big[1:], early_mine, early_theirs, [m[0] for m in ms[1:]], [v[0] for v in vs[1:]],
                          place, "adamw_early", _no_rider())
    (in_chips,) = _chip_scatter_rider([in_sum[1]]).run("reduce_chips_late")
    (in_mine,), _ = _add_chips([in_sum[0]], [in_chips], place, "chip_sum_w_in", _no_rider())
    (in_theirs,) = _pair_send_rider([in_mine]).run("gather_pair_w_in")
    in_res, _ = _adamw(big[:1], [in_mine], [in_theirs], [ms[0][0]], [vs[0][0]], place, "adamw_w_in", _no_rider())
    big_out = {n: tuple(r.reshape(m.shape) for r in res4) for n, res4, m in zip(names, in_res + early_res, ms)}

    small_names = ["w_pool", "pool_scale", "ln1_g", "ln1_b", "ln2_g", "ln2_b", "ln3_g", "ln3_b",
                   "emb_ln_g", "emb_ln_b", "attn_out_g"]
    small_w = [w_pool, pool_scale, ln1_g, ln1_b, ln2_g, ln2_b, ln3_g, ln3_b, emb_ln_g, emb_ln_b, attn_out_g]
    small_m = [m_w_pool, m_pool_scale, m_ln1_g, m_ln1_b, m_ln2_g, m_ln2_b, m_ln3_g, m_ln3_b,
               m_emb_ln_g, m_emb_ln_b, m_attn_out_g]
    small_v = [v_w_pool, v_pool_scale, v_ln1_g, v_ln1_b, v_ln2_g, v_ln2_b, v_ln3_g, v_ln3_b,
               v_emb_ln_g, v_emb_ln_b, v_attn_out_g]
    loss_like = jnp.zeros((8, LANES), F32)
    gs, ds, mos, vos = _small_sum_adamw(recv_a, recv_b, _pack([loss_like] + small_w), _pack([loss_like] + small_m),
                                        _pack([jnp.ones((8, LANES), F32)] + small_v))
    like = [loss_like] + small_w
    gs_u, ds_u, mos_u, vos_u = (_unpack(a, like) for a in (gs, ds, mos, vos))
    loss = gs_u[0][0, 0]
    small_out = {n: (gs_u[i + 1], ds_u[i + 1], mos_u[i + 1], vos_u[i + 1]) for i, n in enumerate(small_names)}

    order = ["emb_ln_g", "emb_ln_b", "w_in", "attn_out_g", "w_pool", "pool_scale", "w_out", "ln1_g", "ln1_b",
             "w_up", "w_down", "ln2_g", "ln2_b", "w_ple", "w_ple_gate", "ln3_g", "ln3_b"]
    res = {**big_out, **small_out}
    outs = [loss, grad_x.reshape(x.shape)]
    for kind in range(4):
        outs += [res[n][kind] for n in order]
    return tuple(outs)
```

```python
import functools

import jax
import jax.numpy as jnp
from jax import lax
from jax.experimental import pallas as pl
from jax.experimental.pallas import tpu as pltpu

F32 = jnp.float32
BF16 = jnp.bfloat16

D_MODEL = 1024
ATTN_WIDTH = 512
POOL_WIDTH = 512
HEAD_DIM = 64
PAIR = 2 * HEAD_DIM
N_PAIRS = ATTN_WIDTH // PAIR
N_POOL_GROUPS = 4
POOL_GROUP = 128
POOL_HALO = 16
D_FF = 4096
PLE_DIM = 256
N_CHIPS = 4
N_DEV = 8
LN_EPS = 1e-5
RMS_EPS = 1e-6
ALPHA = float(2.0 ** 0.25)
Q_SCALE = 0.125
ADAM_LR = 0.001
ADAM_B1 = 0.9
ADAM_B2 = 0.999
ADAM_EPS = 1e-08
ADAM_WD = 0.01
ADAM_STEP = 10
LANES = 128
MIB = 1024 * 1024

MESH = pl.DeviceIdType.MESH
HBM_SPEC = pl.BlockSpec(memory_space=pltpu.HBM)
VMEM_SPEC = pl.BlockSpec(memory_space=pltpu.VMEM)


def _cp(vmem_mib):
    return pltpu.CompilerParams(vmem_limit_bytes=vmem_mib * MIB)


def _dot(a, b):
    return jnp.dot(a, b, preferred_element_type=F32)


def _dot_nt(a, b):
    return lax.dot_general(a, b, (((1,), (1,)), ((), ())), preferred_element_type=F32)


def _dot_tn(a, b):
    return lax.dot_general(a, b, (((0,), (0,)), ((), ())), preferred_element_type=F32)


def _ln_fwd(pre):
    mu = jnp.mean(pre, axis=-1, keepdims=True)
    xc = pre - mu
    var = jnp.mean(xc * xc, axis=-1, keepdims=True)
    rstd = lax.rsqrt(var + LN_EPS)
    return xc * rstd, rstd


def _ln_bwd(dy, xh, rstd, g):
    dxh = dy * g
    m1 = jnp.mean(dxh, axis=-1, keepdims=True)
    m2 = jnp.mean(dxh * xh, axis=-1, keepdims=True)
    return rstd * (dxh - m1 - xh * m2)


def _colsum(a):
    return jnp.sum(a, axis=0, keepdims=True)


def _neg_softplus(z):
    return -(jnp.maximum(z, 0.0) + jnp.log(1.0 + jnp.exp(-jnp.abs(z))))


def _row_spec(tm, n):
    return pl.BlockSpec((tm, n), lambda i: (i, 0))


def _const_spec(shape):
    nd = len(shape)
    return pl.BlockSpec(shape, lambda *_: (0,) * nd)


def _hbm(*arrays):
    return [pltpu.with_memory_space_constraint(a, pltpu.HBM) for a in arrays]


def _sds(shape, dtype):
    return pltpu.HBM(shape, dtype)


def _embln_inproj(x, g0, b0, w_in_s, tm, rider):
    S, D = x.shape
    n_t = S // tm

    def body(*refs):
        ((x_ref, g_ref, b_ref, w_ref), (xh_ref, rstd_ref, q_ref, k_ref, v_ref, u_ref), _,
         ride) = rider.split(refs, 4, 6, 0)
        i = pl.program_id(0)

        @pl.when(i == 0)
        def _():
            rider.first(ride)

        @pl.when(i == (3 * n_t) // 4)
        def _():
            rider.mid(ride)

        xh, rstd = _ln_fwd(x_ref[...])
        xh_ref[...] = xh
        rstd_ref[...] = rstd
        xb = (xh * g_ref[...] + b_ref[...]).astype(BF16)
        q_ref[...] = (_dot(xb, w_ref[0]) * Q_SCALE).astype(BF16)
        k_ref[...] = _dot(xb, w_ref[1]).astype(BF16)
        v_ref[...] = _dot(xb, w_ref[2]).astype(BF16)
        u_ref[...] = _dot(xb, w_ref[3])

        @pl.when(i == n_t - 1)
        def _():
            rider.last(ride)

    return rider.call(
        body, [x, g0, b0, w_in_s], grid=(n_t,), name="embln_inproj",
        in_specs=[_row_spec(tm, D), _const_spec((1, D)), _const_spec((1, D)),
                  _const_spec((N_CHIPS, D, 512))],
        out_specs=[_row_spec(tm, D), _row_spec(tm, 1), _row_spec(tm, 512), _row_spec(tm, 512),
                   _row_spec(tm, 512), _row_spec(tm, 512)],
        out_shape=[_sds((S, D), F32), _sds((S, 1), F32), _sds((S, 512), BF16), _sds((S, 512), BF16),
                   _sds((S, 512), BF16), _sds((S, 512), F32)],
        scratch_shapes=[], vmem_mib=56)


def _tri(n, upper):
    r = lax.broadcasted_iota(jnp.int32, (n, n), 0)
    c = lax.broadcasted_iota(jnp.int32, (n, n), 1)
    keep = (r < c) if upper else (r > c)
    return jnp.where(keep, 1.0, 0.0).astype(BF16)


def _strictly_causal(n):
    return lax.broadcasted_iota(jnp.int32, (n, n), 1) < lax.broadcasted_iota(jnp.int32, (n, n), 0)


LOG_WEIGHT_FLOOR = -110.0


def _weights_left(c_ls):
    return (jnp.max(jnp.maximum(c_ls[0], c_ls[1])) > LOG_WEIGHT_FLOOR).astype(jnp.int32)


def _sb_tile(qhs, kt, low, c_ls, valid):
    valids = valid if isinstance(valid, (list, tuple)) else [valid] * len(qhs)
    zs = [_dot_nt(qh, kt) for qh in qhs]
    lrs = [_neg_softplus(z) for z in zs]
    ls_ = [lr if m is None else jnp.where(m, lr, 0.0) for lr, m in zip(lrs, valids)]
    sfx = [_dot(l.astype(BF16), low) + c_l for l, c_l in zip(ls_, c_ls)]
    lss = [z + lr for z, lr in zip(zs, lrs)]
    ws = [jnp.exp(ls + s) for ls, s in zip(lss, sfx)]
    ws = [w if m is None else jnp.where(m, w, 0.0) for w, m in zip(ws, valids)]
    return lss, ls_, ws


def _attn_fwd(q, k, v, ga, tq, rider):
    S = q.shape[0]
    nq = S // tq

    def body(*refs):
        (q_ref, k_ref, v_ref, ga_ref), (o_ref, on_ref), (low_s,), ride = rider.split(refs, 4, 2, 1)
        p, i = pl.program_id(0), pl.program_id(1)

        @pl.when(jnp.logical_and(p == 0, i == 0))
        def _():
            rider.first(ride)
            low_s[...] = _tri(tq, upper=False)

        @pl.when(jnp.logical_and(p == N_PAIRS - 1, i == 0))
        def _():
            rider.mid(ride)

        lane = lax.broadcasted_iota(jnp.int32, (1, PAIR), 1)
        m0 = lane < HEAD_DIM
        low = low_s[...]
        q2 = q_ref[...]
        qhs = [jnp.where(m0, q2, jnp.zeros_like(q2)), jnp.where(m0, jnp.zeros_like(q2), q2)]

        def tile(kb, c_ls, accs, valid):
            ks = pl.multiple_of(kb * tq, tq)
            kt = k_ref[pl.ds(ks, tq), :]
            vt = v_ref[pl.ds(ks, tq), :]
            _, ls_, ws = _sb_tile(qhs, kt, low, c_ls, valid)
            new_a = [acc + _dot(w.astype(BF16), vt) for acc, w in zip(accs, ws)]
            new_c = [c_l + jnp.sum(l, axis=1, keepdims=True) for c_l, l in zip(c_ls, ls_)]
            return new_c, new_a

        zc, za = jnp.zeros((tq, 1), F32), jnp.zeros((tq, PAIR), F32)

        def first_two():
            c_ls, accs = tile(i, [zc, zc], [za, za], _strictly_causal(tq))
            c_ls, accs = tile(i - 1, c_ls, accs, None)
            return (_weights_left(c_ls), *c_ls, *accs)

        def first_one():
            c_ls, accs = tile(i, [zc, zc], [za, za], _strictly_causal(tq))
            return (jnp.int32(0), *c_ls, *accs)

        st0 = lax.cond(i >= 1, first_two, first_one)

        def more(st):
            return jnp.logical_and(st[0] <= i, st[1] > 0)

        def step(st):
            n, _, c0, c1, a0, a1 = st
            c_ls, accs = tile(i - n, [c0, c1], [a0, a1], None)
            return (n + 1, _weights_left(c_ls), c_ls[0], c_ls[1], accs[0], accs[1])

        st = lax.while_loop(more, step, (jnp.int32(2), *st0))
        o = jnp.where(m0, st[4], st[5])
        o_ref[...] = o
        sq = o * o
        ms0 = jnp.sum(jnp.where(m0, sq, 0.0), axis=-1, keepdims=True) * (1.0 / HEAD_DIM)
        ms1 = jnp.sum(jnp.where(m0, 0.0, sq), axis=-1, keepdims=True) * (1.0 / HEAD_DIM)
        rs = jnp.where(m0, lax.rsqrt(ms0 + RMS_EPS), lax.rsqrt(ms1 + RMS_EPS))
        on_ref[...] = (o * rs * ga_ref[...]).astype(BF16)

        @pl.when(jnp.logical_and(p == N_PAIRS - 1, i == nq - 1))
        def _():
            rider.last(ride)

    return rider.call(
        body, [q, k, v, ga], grid=(N_PAIRS, nq), name="attn_fwd",
        in_specs=[pl.BlockSpec((tq, PAIR), lambda p, i: (i, p)),
                  pl.BlockSpec((S, PAIR), lambda p, i: (0, p)),
                  pl.BlockSpec((S, PAIR), lambda p, i: (0, p)),
                  pl.BlockSpec((1, PAIR), lambda p, i: (0, p))],
        out_specs=[pl.BlockSpec((tq, PAIR), lambda p, i: (i, p)),
                   pl.BlockSpec((tq, PAIR), lambda p, i: (i, p))],
        out_shape=[_sds((S, ATTN_WIDTH), F32), _sds((S, ATTN_WIDTH), BF16)],
        scratch_shapes=[pltpu.VMEM((tq, tq), BF16)], vmem_mib=40)


def _pool_fwd(u, w_pool, pscale, tm):
    S = u.shape[0]
    hb = tm // POOL_HALO

    def body(u_ref, uh_ref, wp_ref, sc_ref, d_ref, pooled_ref):
        i = pl.program_id(0)
        halo = jnp.where(i > 0, uh_ref[...], 0.0)
        pos = i * tm + lax.broadcasted_iota(jnp.int32, (tm, 1), 0)
        for g in range(N_POOL_GROUPS):
            win = 2 ** (g + 1)
            cols = slice(g * POOL_GROUP, (g + 1) * POOL_GROUP)
            ut = u_ref[:, cols]
            s = jnp.concatenate([halo[:, cols], ut], axis=0)
            for sh in (1, 2, 4, 8)[:g + 1]:
                s = s + pltpu.roll(s, sh, 0)
            cnt = jnp.minimum(pos + 1, win).astype(F32)
            db = (s[POOL_HALO:, :] / cnt - ut).astype(BF16)
            y = _dot(db, wp_ref[g].astype(BF16))
            d_ref[:, cols] = db
            pooled_ref[:, cols] = (y * sc_ref[:, cols]).astype(BF16)

    return pl.pallas_call(
        body, grid=(S // tm,), name="pool_fwd",
        in_specs=[_row_spec(tm, POOL_WIDTH),
                  pl.BlockSpec((POOL_HALO, POOL_WIDTH), lambda i: (jnp.maximum(i * hb - 1, 0), 0)),
                  _const_spec((N_POOL_GROUPS, POOL_GROUP, POOL_GROUP)), _const_spec((1, POOL_WIDTH))],
        out_specs=[_row_spec(tm, POOL_WIDTH), _row_spec(tm, POOL_WIDTH)],
        out_shape=[_sds((S, POOL_WIDTH), BF16), _sds((S, POOL_WIDTH), BF16)],
        compiler_params=_cp(32),
    )(*_hbm(u, u, w_pool, pscale))


def _mix_ln1(on, pooled, xh0, g0, b0, w_out, g1, b1, tm, rider):
    S, D = xh0.shape
    n_t = S // tm

    def body(*refs):
        ((on_ref, po_ref, xh0_ref, g0_ref, b0_ref, w_ref, g1_ref, b1_ref), (xh_ref, rstd_ref, xb_ref), _,
         ride) = rider.split(refs, 8, 3, 0)

        @pl.when(pl.program_id(0) == 0)
        def _():
            rider.first(ride)

        mixed = _dot(on_ref[...], w_ref[:ATTN_WIDTH, :]) + _dot(po_ref[...], w_ref[ATTN_WIDTH:, :])
        x0 = xh0_ref[...] * g0_ref[...] + b0_ref[...]
        xh, rstd = _ln_fwd(ALPHA * x0 + mixed)
        xh_ref[...] = xh
        rstd_ref[...] = rstd
        xb_ref[...] = (xh * g1_ref[...] + b1_ref[...]).astype(BF16)

        @pl.when(pl.program_id(0) == n_t - 1)
        def _():
            rider.last(ride)

    return rider.call(
        body, [on, pooled, xh0, g0, b0, w_out, g1, b1], grid=(n_t,), name="mix_ln1",
        in_specs=[_row_spec(tm, ATTN_WIDTH), _row_spec(tm, POOL_WIDTH), _row_spec(tm, D),
                  _const_spec((1, D)), _const_spec((1, D)), _const_spec((D, D)),
                  _const_spec((1, D)), _const_spec((1, D))],
        out_specs=[_row_spec(tm, D), _row_spec(tm, 1), _row_spec(tm, D)],
        out_shape=[_sds((S, D), F32), _sds((S, 1), F32), _sds((S, D), BF16)],
        scratch_shapes=[], vmem_mib=56)


def _mlp_ln2(xh1, x1b, g1, b1, w_up_s, w_down, tm, rider):
    S, D = xh1.shape
    fc = D_FF // N_CHIPS
    n_t = S // tm

    def body(*refs):
        ((xh_ref, xb_ref, g_ref, b_ref, wu_ref, wd_ref), (xh2_ref, rstd_ref, r_ref), (acc_ref,),
         ride) = rider.split(refs, 6, 3, 1)
        i, j = pl.program_id(0), pl.program_id(1)

        @pl.when(jnp.logical_and(i == 0, j == 0))
        def _():
            rider.first(ride)

        @pl.when(j == 0)
        def _():
            acc_ref[...] = jnp.zeros_like(acc_ref)

        r = jnp.maximum(_dot(xb_ref[...], wu_ref[0]), 0.0)
        r_ref[...] = r.astype(BF16)
        acc_ref[...] += _dot((r * r).astype(BF16), wd_ref[...])

        @pl.when(j == N_CHIPS - 1)
        def _():
            x1 = xh_ref[...] * g_ref[...] + b_ref[...]
            xh, rstd = _ln_fwd(ALPHA * x1 + acc_ref[...])
            xh2_ref[...] = xh
            rstd_ref[...] = rstd

        @pl.when(jnp.logical_and(i == n_t - 1, j == N_CHIPS - 1))
        def _():
            rider.last(ride)

    return rider.call(
        body, [xh1, x1b, g1, b1, w_up_s, w_down], grid=(n_t, N_CHIPS), name="mlp_ln2",
        in_specs=[pl.BlockSpec((tm, D), lambda i, j: (i, 0)), pl.BlockSpec((tm, D), lambda i, j: (i, 0)),
                  pl.BlockSpec((1, D), lambda i, j: (0, 0)), pl.BlockSpec((1, D), lambda i, j: (0, 0)),
                  pl.BlockSpec((1, D, fc), lambda i, j: (j, 0, 0)),
                  pl.BlockSpec((fc, D), lambda i, j: (j, 0))],
        out_specs=[pl.BlockSpec((tm, D), lambda i, j: (i, 0)), pl.BlockSpec((tm, 1), lambda i, j: (i, 0)),
                   pl.BlockSpec((tm, fc), lambda i, j: (i, j))],
        out_shape=[_sds((S, D), F32), _sds((S, 1), F32), _sds((S, D_FF), BF16)],
        scratch_shapes=[pltpu.VMEM((tm, D), F32)], vmem_mib=56)


def _ple_ln3_loss(xh2, rstd2, g2, b2, p, w_ple_s, w_gate, g3, b3, target, tm):
    S, D = xh2.shape
    pc = D // N_CHIPS

    def body(xh2_ref, rstd2_ref, g2_ref, b2_ref, p_ref, wp_ref, wg_ref, g3_ref, b3_ref, t_ref,
             dpre2_ref, dhb_ref, dwp_ref, dwg_ref, dg3_ref, db3_ref, dg2_ref, db2_ref, loss_ref):
        i = pl.program_id(0)

        @pl.when(i == 0)
        def _():
            for r in (dwp_ref, dwg_ref, dg3_ref, db3_ref, dg2_ref, db2_ref, loss_ref):
                r[...] = jnp.zeros_like(r)

        xh2 = xh2_ref[...]
        x2 = xh2 * g2_ref[...] + b2_ref[...]
        x2b = x2.astype(BF16)
        gate = 1.0 / (1.0 + jnp.exp(-_dot(x2b, wg_ref[...])))
        pb = p_ref[...].astype(BF16)
        pe = jnp.concatenate([_dot(pb, wp_ref[c]) for c in range(N_CHIPS)], axis=1)
        xh3, rstd3 = _ln_fwd(ALPHA * x2 + pe * gate)
        diff = xh3 * g3_ref[...] + b3_ref[...] - t_ref[...]
        loss_ref[...] += (0.5 / D) * jnp.sum(diff * diff)
        dy = diff * (1.0 / D)
        dg3_ref[...] += _colsum(dy * xh3)
        db3_ref[...] += _colsum(dy)
        dpre3 = _ln_bwd(dy, xh3, rstd3, g3_ref[...])
        dpe_b = (dpre3 * gate).astype(BF16)
        dgp_b = (dpre3 * pe * gate * (1.0 - gate)).astype(BF16)
        dx2 = ALPHA * dpre3 + _dot_nt(dgp_b, wg_ref[...])
        dwg_ref[...] += _dot_tn(x2b, dgp_b)
        for c in range(N_CHIPS):
            dwp_ref[c] += _dot_tn(pb, dpe_b[:, c * pc:(c + 1) * pc])
        dg2_ref[...] += _colsum(dx2 * xh2)
        db2_ref[...] += _colsum(dx2)
        dpre2 = _ln_bwd(dx2, xh2, rstd2_ref[...], g2_ref[...])
        dpre2_ref[...] = dpre2
        dhb_ref[...] = dpre2.astype(BF16)

    vec = _const_spec((1, D))
    return pl.pallas_call(
        body, grid=(S // tm,), name="ple_ln3_loss",
        in_specs=[_row_spec(tm, D), _row_spec(tm, 1), vec, vec, _row_spec(tm, PLE_DIM),
                  _const_spec((N_CHIPS, PLE_DIM, pc)), _const_spec((D, D)), vec, vec, _row_spec(tm, D)],
        out_specs=[_row_spec(tm, D), _row_spec(tm, D), _const_spec((N_CHIPS, PLE_DIM, pc)),
                   _const_spec((D, D)), vec, vec, vec, vec, _const_spec((1, LANES))],
        out_shape=[_sds((S, D), F32), _sds((S, D), BF16), _sds((N_CHIPS, PLE_DIM, pc), F32),
                   _sds((D, D), F32), _sds((1, D), F32), _sds((1, D), F32), _sds((1, D), F32),
                   _sds((1, D), F32), _sds((1, LANES), F32)],
        compiler_params=_cp(58),
    )(*_hbm(xh2, rstd2, g2, b2, p, w_ple_s, w_gate, g3, b3, target))


def _mlp_bwd(rb, dhb, w_up_s, w_down, tm, rider):
    S, D = dhb.shape
    fc = D_FF // N_CHIPS
    n_t = S // tm

    def body(*refs):
        (r_ref, dh_ref, wu_ref, wd_ref), (dx_ref, da_ref), _, ride = rider.split(refs, 4, 2, 0)
        i, j = pl.program_id(0), pl.program_id(1)

        @pl.when(jnp.logical_and(i == 0, j == 0))
        def _():
            rider.first(ride)

        @pl.when(j == 0)
        def _():
            dx_ref[...] = jnp.zeros_like(dx_ref)

        da = (_dot_nt(dh_ref[...], wd_ref[...]) * (2.0 * r_ref[...].astype(F32))).astype(BF16)
        da_ref[...] = da
        dx_ref[...] += _dot_nt(da, wu_ref[0])

        @pl.when(jnp.logical_and(i == n_t - 1, j == N_CHIPS - 1))
        def _():
            rider.last(ride)

    return rider.call(
        body, [rb, dhb, w_up_s, w_down], grid=(n_t, N_CHIPS), name="mlp_bwd",
        in_specs=[pl.BlockSpec((tm, fc), lambda i, j: (i, j)), pl.BlockSpec((tm, D), lambda i, j: (i, 0)),
                  pl.BlockSpec((1, D, fc), lambda i, j: (j, 0, 0)),
                  pl.BlockSpec((fc, D), lambda i, j: (j, 0))],
        out_specs=[pl.BlockSpec((tm, D), lambda i, j: (i, 0)), pl.BlockSpec((tm, fc), lambda i, j: (i, j))],
        out_shape=[_sds((S, D), F32), _sds((S, D_FF), BF16)],
        scratch_shapes=[], vmem_mib=56)


def _tn_matmul(a, b, name, tk, tt, stacked, rider, square_a=False):
    T, K = a.shape
    N = b.shape[1]
    tn = 1024
    grid = (K // tk, N // tn, T // tt)

    def body(*refs):
        (a_ref, b_ref), (o_ref,), _, ride = rider.split(refs, 2, 1, 0)
        at = [pl.program_id(d) for d in range(3)]

        @pl.when(jnp.logical_and(jnp.logical_and(at[0] == 0, at[1] == 0), at[2] == 0))
        def _():
            rider.first(ride)

        @pl.when(at[2] == 0)
        def _():
            o_ref[...] = jnp.zeros_like(o_ref)

        a_t = a_ref[...]
        if square_a:
            a_t = a_t * a_t
        prod = _dot_tn(a_t, b_ref[...])
        if stacked:
            o_ref[0] += prod
        else:
            o_ref[...] += prod

        @pl.when(jnp.logical_and(jnp.logical_and(at[0] == grid[0] - 1, at[1] == grid[1] - 1),
                                 at[2] == grid[2] - 1))
        def _():
            rider.last(ride)

    if stacked:
        out_spec = pl.BlockSpec((1, tk, tn), lambda k, n, t: (n, k, 0))
        out_shape = _sds((N // tn, K, tn), F32)
    else:
        out_spec = pl.BlockSpec((tk, tn), lambda k, n, t: (k, n))
        out_shape = _sds((K, N), F32)
    return rider.call(
        body, [a, b], grid=grid, name=name,
        in_specs=[pl.BlockSpec((tt, tk), lambda k, n, t: (t, k)),
                  pl.BlockSpec((tt, tn), lambda k, n, t: (t, n))],
        out_specs=[out_spec], out_shape=[out_shape], scratch_shapes=[], vmem_mib=58)


def _mix_bwd(dpre2, dx1m, xh1, rstd1, g1, w_out, on, pooled, o_raw, ga, tm, rider):
    S, D = xh1.shape
    n_t = S // tm

    def body(*refs):
        ((dp2_ref, dxm_ref, xh_ref, rstd_ref, g_ref, w_ref, on_ref, po_ref, o_ref, ga_ref),
         (dpre1_ref, do_ref, dpo_ref, dw_ref, dg_ref, db_ref, dga_ref), _, ride) = rider.split(refs, 10, 7, 0)

        @pl.when(pl.program_id(0) == 0)
        def _():
            rider.first(ride)
            for r in (dw_ref, dg_ref, db_ref, dga_ref):
                r[...] = jnp.zeros_like(r)

        xh = xh_ref[...]
        dx1 = ALPHA * dp2_ref[...] + dxm_ref[...]
        dg_ref[...] += _colsum(dx1 * xh)
        db_ref[...] += _colsum(dx1)
        dpre1 = _ln_bwd(dx1, xh, rstd_ref[...], g_ref[...])
        dpre1_ref[...] = dpre1
        dmb = dpre1.astype(BF16)
        dcat = _dot_nt(dmb, w_ref[...])
        dpo_ref[...] = dcat[:, ATTN_WIDTH:]
        dw_ref[:ATTN_WIDTH, :] += _dot_tn(on_ref[...], dmb)
        dw_ref[ATTN_WIDTH:, :] += _dot_tn(po_ref[...], dmb)

        m0 = lax.broadcasted_iota(jnp.int32, (1, PAIR), 1) < HEAD_DIM

        def seg_mean(a):
            s0 = jnp.sum(jnp.where(m0, a, 0.0), axis=-1, keepdims=True)
            s1 = jnp.sum(jnp.where(m0, 0.0, a), axis=-1, keepdims=True)
            return jnp.where(m0, s0, s1) * (1.0 / HEAD_DIM)

        for p in range(N_PAIRS):
            cols = slice(p * PAIR, (p + 1) * PAIR)
            o = o_ref[:, cols]
            rs = lax.rsqrt(seg_mean(o * o) + RMS_EPS)
            oh = o * rs
            don = dcat[:, cols]
            dga_ref[:, cols] += _colsum(don * oh)
            doh = don * ga_ref[:, cols]
            do_ref[:, cols] = rs * (doh - oh * seg_mean(doh * oh))

        @pl.when(pl.program_id(0) == n_t - 1)
        def _():
            rider.last(ride)

    vec = _const_spec((1, D))
    return rider.call(
        body, [dpre2, dx1m, xh1, rstd1, g1, w_out, on, pooled, o_raw, ga], grid=(n_t,), name="mix_bwd",
        in_specs=[_row_spec(tm, D), _row_spec(tm, D), _row_spec(tm, D), _row_spec(tm, 1), vec,
                  _const_spec((D, D)), _row_spec(tm, ATTN_WIDTH), _row_spec(tm, POOL_WIDTH),
                  _row_spec(tm, ATTN_WIDTH), _const_spec((1, ATTN_WIDTH))],
        out_specs=[_row_spec(tm, D), _row_spec(tm, ATTN_WIDTH), _row_spec(tm, POOL_WIDTH),
                   _const_spec((D, D)), vec, vec, _const_spec((1, ATTN_WIDTH))],
        out_shape=[_sds((S, D), F32), _sds((S, ATTN_WIDTH), F32), _sds((S, POOL_WIDTH), F32),
                   _sds((D, D), F32), _sds((1, D), F32), _sds((1, D), F32), _sds((1, ATTN_WIDTH), F32)],
        scratch_shapes=[], vmem_mib=56)


def _pool_bwd(dpooled, d_b, w_pool, pscale, tm, rider):
    S = dpooled.shape[0]
    hb = tm // POOL_HALO
    n_t = S // tm
    te = tm + POOL_HALO

    def body(*refs):
        ((dp_ref, dph_ref, d_ref, wp_ref, sc_ref), (du_ref, dwp_ref, dsc_ref), _,
         ride) = rider.split(refs, 5, 3, 0)
        i = pl.program_id(0)

        @pl.when(i == 0)
        def _():
            rider.first(ride)
            dwp_ref[...] = jnp.zeros_like(dwp_ref)
            dsc_ref[...] = jnp.zeros_like(dsc_ref)

        halo = jnp.where(i < n_t - 1, dph_ref[...], 0.0)
        pos = i * tm + lax.broadcasted_iota(jnp.int32, (te, 1), 0)
        for g in range(N_POOL_GROUPS):
            win = 2 ** (g + 1)
            cols = slice(g * POOL_GROUP, (g + 1) * POOL_GROUP)
            wpb = wp_ref[g].astype(BF16)
            dpt = dp_ref[:, cols]
            dpe = jnp.concatenate([dpt, halo[:, cols]], axis=0)
            dyb = (dpe * sc_ref[:, cols]).astype(BF16)
            dd = _dot_nt(dyb, wpb)
            s = dd / jnp.minimum(pos + 1, win).astype(F32)
            for sh in (1, 2, 4, 8)[:g + 1]:
                s = s + pltpu.roll(s, te - sh, 0)
            du_ref[:, cols] = s[:tm, :] - dd[:tm, :]
            db = d_ref[:, cols]
            dwp_ref[g] += _dot_tn(db, dyb[:tm, :])
            dsc_ref[:, cols] += _colsum(dpt * _dot(db, wpb))

        @pl.when(i == n_t - 1)
        def _():
            rider.last(ride)

    return rider.call(
        body, [dpooled, dpooled, d_b, w_pool, pscale], grid=(n_t,), name="pool_bwd",
        in_specs=[_row_spec(tm, POOL_WIDTH),
                  pl.BlockSpec((POOL_HALO, POOL_WIDTH),
                               lambda i: (jnp.minimum((i + 1) * hb, S // POOL_HALO - 1), 0)),
                  _row_spec(tm, POOL_WIDTH),
                  _const_spec((N_POOL_GROUPS, POOL_GROUP, POOL_GROUP)), _const_spec((1, POOL_WIDTH))],
        out_specs=[_row_spec(tm, POOL_WIDTH), _const_spec((N_POOL_GROUPS, POOL_GROUP, POOL_GROUP)),
                   _const_spec((1, POOL_WIDTH))],
        out_shape=[_sds((S, POOL_WIDTH), F32), _sds((N_POOL_GROUPS, POOL_GROUP, POOL_GROUP), F32),
                   _sds((1, POOL_WIDTH), F32)],
        scratch_shapes=[], vmem_mib=32)


def _attn_bwd(q, k, v, do, tq, rider):
    S = q.shape[0]
    nq = S // tq

    def body(*refs):
        ((q_ref, k_ref, v_ref, do_ref), (dq_ref, dk_ref, dv_ref),
         (g_s, b_s, low_s, upp_s), ride) = rider.split(refs, 4, 3, 4)
        p, i = pl.program_id(0), pl.program_id(1)

        @pl.when(jnp.logical_and(p == 0, i == 0))
        def _():
            rider.first(ride)
            low_s[...] = _tri(tq, upper=False)
            upp_s[...] = _tri(tq, upper=True)

        @pl.when(i == 0)
        def _():
            for r in (dk_ref, dv_ref):
                r[...] = jnp.zeros_like(r)

        lane = lax.broadcasted_iota(jnp.int32, (1, PAIR), 1)
        m0 = lane < HEAD_DIM
        low = low_s[...]
        upp = upp_s[...]
        dob = do_ref[...].astype(BF16)
        q2 = q_ref[...]
        qhs = [jnp.where(m0, q2, jnp.zeros_like(q2)), jnp.where(m0, jnp.zeros_like(q2), q2)]
        dhs = [jnp.where(m0, dob, jnp.zeros_like(dob)), jnp.where(m0, jnp.zeros_like(dob), dob)]
        causal = _strictly_causal(tq)

        def down(kb, c_ls, valid):
            ks = pl.multiple_of(kb * tq, tq)
            kt = k_ref[pl.ds(ks, tq), :]
            vt = v_ref[pl.ds(ks, tq), :]
            lss, ls_, ws = _sb_tile(qhs, kt, low, c_ls, valid)
            dws = [_dot_nt(dh, vt) for dh in dhs]
            for hh in range(2):
                g_s[hh, kb] = dws[hh] * ws[hh]
                b_s[hh, kb] = jnp.exp(lss[hh])
            dv_ref[pl.ds(ks, tq), :] += (_dot_tn(ws[0].astype(BF16), dhs[0])
                                         + _dot_tn(ws[1].astype(BF16), dhs[1]))
            return [c_l + jnp.sum(l, axis=1, keepdims=True) for c_l, l in zip(c_ls, ls_)]

        zc, za = jnp.zeros((tq, 1), F32), jnp.zeros((tq, PAIR), F32)
        def first_two():
            c_ls = down(i - 1, down(i, [zc, zc], causal), None)
            return (_weights_left(c_ls), *c_ls)

        st0 = lax.cond(i >= 1, first_two, lambda: (jnp.int32(0), *down(i, [zc, zc], causal)))

        def more(st):
            return jnp.logical_and(st[0] <= i, st[1] > 0)

        def down_step(st):
            c_ls = down(i - st[0], [st[2], st[3]], None)
            return (st[0] + 1, _weights_left(c_ls), c_ls[0], c_ls[1])

        n_tiles = lax.while_loop(more, down_step, (jnp.int32(2), *st0))[0]

        def up(kb, c_gs, accs, valid):
            ks = pl.multiple_of(kb * tq, tq)
            kt = k_ref[pl.ds(ks, tq), :]
            gs = [g_s[hh, kb] for hh in range(2)]
            pres = [_dot(g.astype(BF16), upp) + c_g for g, c_g in zip(gs, c_gs)]
            dzs = []
            for hh in range(2):
                beta = b_s[hh, kb]
                dz = gs[hh] - beta * (gs[hh] + pres[hh])
                if valid is not None:
                    dz = jnp.where(valid, dz, 0.0)
                dzs.append(dz.astype(BF16))
            new_a = [acc + _dot(dzb, kt) for acc, dzb in zip(accs, dzs)]
            dk_ref[pl.ds(ks, tq), :] += _dot_tn(dzs[0], qhs[0]) + _dot_tn(dzs[1], qhs[1])
            new_c = [c_g + jnp.sum(g, axis=1, keepdims=True) for c_g, g in zip(c_gs, gs)]
            return new_c, new_a

        def up_step(kb, st):
            c_gs, accs = up(kb, [st[0], st[1]], [st[2], st[3]], None)
            return (c_gs[0], c_gs[1], accs[0], accs[1])

        st = lax.fori_loop(i - n_tiles + 1, i - 1, up_step, (zc, zc, za, za))

        def last_two():
            c_gs, accs = up(i - 1, [st[0], st[1]], [st[2], st[3]], None)
            return tuple(up(i, c_gs, accs, causal)[1])

        accs = lax.cond(i >= 1, last_two, lambda: tuple(up(i, [zc, zc], [za, za], causal)[1]))
        dq_ref[...] = jnp.where(m0, accs[0], accs[1]) * Q_SCALE

        @pl.when(jnp.logical_and(p == N_PAIRS - 1, i == nq - 1))
        def _():
            rider.last(ride)

    return rider.call(
        body, [q, k, v, do], grid=(N_PAIRS, nq), name="attn_bwd",
        in_specs=[pl.BlockSpec((tq, PAIR), lambda p, i: (i, p)),
                  pl.BlockSpec((S, PAIR), lambda p, i: (0, p)),
                  pl.BlockSpec((S, PAIR), lambda p, i: (0, p)),
                  pl.BlockSpec((tq, PAIR), lambda p, i: (i, p))],
        out_specs=[pl.BlockSpec((tq, PAIR), lambda p, i: (i, p)),
                   pl.BlockSpec((S, PAIR), lambda p, i: (0, p)),
                   pl.BlockSpec((S, PAIR), lambda p, i: (0, p))],
        out_shape=[_sds((S, ATTN_WIDTH), F32), _sds((S, ATTN_WIDTH), F32), _sds((S, ATTN_WIDTH), F32)],
        scratch_shapes=[pltpu.VMEM((2, nq, tq, tq), F32), pltpu.VMEM((2, nq, tq, tq), F32),
                        pltpu.VMEM((tq, tq), BF16), pltpu.VMEM((tq, tq), BF16)],
        vmem_mib=56)


def _inproj_bwd(dq, dk, dv, du, dpre1, xh0, rstd0, g0, b0, w_in_s, tm):
    S, D = xh0.shape

    def body(dq_ref, dk_ref, dv_ref, du_ref, dp1_ref, xh_ref, rstd_ref, g_ref, b_ref, w_ref,
             gx_ref, dw_ref, dg_ref, db_ref):
        @pl.when(pl.program_id(0) == 0)
        def _():
            for r in (dw_ref, dg_ref, db_ref):
                r[...] = jnp.zeros_like(r)

        xh = xh_ref[...]
        xb = (xh * g_ref[...] + b_ref[...]).astype(BF16)
        dx0 = ALPHA * dp1_ref[...]
        for c, r in enumerate((dq_ref, dk_ref, dv_ref, du_ref)):
            dpb = r[...].astype(BF16)
            dx0 = dx0 + _dot_nt(dpb, w_ref[c])
            dw_ref[c] += _dot_tn(xb, dpb)
        dg_ref[...] += _colsum(dx0 * xh)
        db_ref[...] += _colsum(dx0)
        gx_ref[...] = _ln_bwd(dx0, xh, rstd_ref[...], g_ref[...])

    vec = _const_spec((1, D))
    half = _row_spec(tm, 512)
    return pl.pallas_call(
        body, grid=(S // tm,), name="inproj_bwd",
        in_specs=[half, half, half, half, _row_spec(tm, D), _row_spec(tm, D), _row_spec(tm, 1), vec, vec,
                  _const_spec((N_CHIPS, D, 512))],
        out_specs=[_row_spec(tm, D), _const_spec((N_CHIPS, D, 512)), vec, vec],
        out_shape=[_sds((S, D), F32), _sds((N_CHIPS, D, 512), F32), _sds((1, D), F32), _sds((1, D), F32)],
        compiler_params=_cp(58),
    )(*_hbm(dq, dk, dv, du, dpre1, xh0, rstd0, g0, b0, w_in_s))


def _place():
    return lax.axis_index("x"), lax.axis_index("y"), lax.axis_index("c")


CHIP_FLIPS = ((0, 1), (1, 0), (1, 1))


class _Rider:
    def __init__(self, ins, out_shapes, n_sem, phases, aliases=None):
        self.ins, self.out_shapes, self.n_sem, self.phases = list(ins), list(out_shapes), n_sem, phases
        self.aliases = aliases or {}

    def __add__(self, other):
        na, ma = len(self.ins), len(self.out_shapes)

        def phases(ins, outs, ssem, rsem):
            mine = self.phases(ins[:na], outs[:ma], ssem, rsem)
            rest = pl.ds(self.n_sem, other.n_sem)
            theirs = other.phases(ins[na:], outs[ma:], ssem.at[rest], rsem.at[rest])
            assert len(mine) == 1 and len(theirs) == 1
            return [mine[0] + theirs[0]]

        aliases = {**self.aliases, **{na + i: ma + o for i, o in other.aliases.items()}}
        return _Rider(self.ins + other.ins, self.out_shapes + other.out_shapes, self.n_sem + other.n_sem, phases,
                      aliases)

    def split(self, refs, n_in, n_out, n_scratch):
        a = n_in + len(self.ins)
        b = a + n_out
        c = b + len(self.out_shapes)
        own = (refs[:n_in], refs[a:b], refs[c:c + n_scratch])
        return own + ((refs[n_in:a], refs[b:c]) + tuple(refs[c + n_scratch:]),)

    def first(self, ride):
        for make in self.phases(*ride)[0]:
            make().start()

    def mid(self, ride):
        ph = self.phases(*ride)
        if len(ph) == 2:
            for make in ph[0]:
                make().wait_recv()
            for make in ph[1]:
                make().start()

    def last(self, ride):
        ph = self.phases(*ride)
        if len(ph) == 2:
            for make in ph[0]:
                make().wait_send()
        for make in ph[-1]:
            make().wait()

    def call(self, body, args, *, grid, name, in_specs, out_specs, out_shape, scratch_shapes, vmem_mib,
             prefetch=None):
        n_in, n_out = len(in_specs), len(out_specs)
        sems = [pltpu.SemaphoreType.DMA((self.n_sem,)), pltpu.SemaphoreType.DMA((self.n_sem,))]
        n_pre = 0 if prefetch is None else 1
        grid_spec = pltpu.PrefetchScalarGridSpec(
            num_scalar_prefetch=n_pre, grid=grid,
            in_specs=list(in_specs) + [HBM_SPEC] * len(self.ins),
            out_specs=list(out_specs) + [HBM_SPEC] * len(self.out_shapes),
            scratch_shapes=list(scratch_shapes) + sems)
        return pl.pallas_call(
            body, name=name, grid_spec=grid_spec,
            out_shape=list(out_shape) + self.out_shapes,
            input_output_aliases={n_pre + n_in + i: n_out + o for i, o in self.aliases.items()},
            compiler_params=_cp(vmem_mib),
        )(*([] if prefetch is None else [prefetch]), *_hbm(*args), *self.ins)

    def run(self, name):
        def body(*refs):
            ride = self.split(refs, 0, 0, 0)[3]
            self.first(ride)
            self.mid(ride)
            self.last(ride)

        return self.call(body, [], grid=(), name=name, in_specs=[], out_specs=[], out_shape=[],
                         scratch_shapes=[], vmem_mib=16)


def _remote(src, dst, ssem, rsem, n, dev):
    return functools.partial(pltpu.make_async_remote_copy, src_ref=src, dst_ref=dst, send_sem=ssem.at[n],
                             recv_sem=rsem.at[n], device_id=dev, device_id_type=MESH)


def _cast_into_slot(w, place, name):
    R, C = w.shape
    tr = min(R, 512)

    def body(pl_ref, w_ref, o_ref):
        o_ref[0] = w_ref[...].astype(BF16)

    return pl.pallas_call(
        body, name=name,
        grid_spec=pltpu.PrefetchScalarGridSpec(
            num_scalar_prefetch=1, grid=(R // tr,),
            in_specs=[pl.BlockSpec((tr, C), lambda r, pr: (r, 0))],
            out_specs=pl.BlockSpec((1, tr, C), lambda r, pr: (pr[1], r, 0))),
        out_shape=_sds((N_CHIPS, R, C), BF16),
    )(place, w)


CAST_STEPS = 8


def _cast_rest(ws, place, rider):
    n = len(ws)

    def body(pl_ref, *refs):
        w_refs, o_refs, _, ride = rider.split(refs, n, n, 0)
        r = pl.program_id(0)

        @pl.when(r == 0)
        def _():
            rider.first(ride)

        @pl.when(r == CAST_STEPS // 2)
        def _():
            rider.mid(ride)

        for w_ref, o_ref in zip(w_refs, o_refs):
            o_ref[0] = w_ref[...].astype(BF16)

        @pl.when(r == CAST_STEPS - 1)
        def _():
            rider.last(ride)

    def rows(w):
        return w.shape[0] // CAST_STEPS

    return rider.call(
        body, ws, grid=(CAST_STEPS,), name="cast_weights", prefetch=place,
        in_specs=[pl.BlockSpec((rows(w), w.shape[1]), lambda r, pr: (r, 0)) for w in ws],
        out_specs=[pl.BlockSpec((1, rows(w), w.shape[1]), lambda r, pr: (pr[1], r, 0)) for w in ws],
        out_shape=[_sds((N_CHIPS,) + w.shape, BF16) for w in ws], scratch_shapes=[], vmem_mib=32)


def _gather_rider(stacked, part="both"):
    n, nf = len(stacked), len(CHIP_FLIPS)

    def phases(ins, outs, ssem, rsem):
        x, y, c = _place()
        slot = 2 * x + y
        ici, d2d = [], []
        for w, (i_ref, o_ref) in enumerate(zip(ins, outs)):
            hh = o_ref.shape[1] // 2
            rows = pl.ds(c * hh, hh)
            for f, (fx, fy) in enumerate(CHIP_FLIPS):
                k = w * nf + f
                theirs = 2 * (x ^ fx) + (y ^ fy)
                if part != "pair":
                    ici.append(_remote(i_ref.at[slot, rows], o_ref.at[slot, rows], ssem, rsem, k,
                                       (x ^ fx, y ^ fy, c)))
                if part != "chips":
                    d2d.append(_remote(o_ref.at[theirs, rows], o_ref.at[theirs, rows], ssem, rsem,
                                       (n * nf if part == "both" else 0) + k, (x, y, 1 - c)))
        return [ph for ph in (ici, d2d) if ph]

    return _Rider(stacked, [_sds(s.shape, s.dtype) for s in stacked], (2 if part == "both" else 1) * n * nf,
                  phases, aliases={i: i for i in range(n)})


def _pair_swap_rider(grads):
    def phases(ins, outs, ssem, rsem):
        x, y, c = _place()
        return [[_remote(g.at[:, 1 - c], o, ssem, rsem, k, (x, y, 1 - c))
                 for k, (g, o) in enumerate(zip(ins, outs))]]

    return _Rider(grads, [_sds((N_CHIPS,) + g.shape[2:], g.dtype) for g in grads], len(grads), phases)


def _chip_scatter_rider(parts):
    nf = len(CHIP_FLIPS)

    def phases(ins, outs, ssem, rsem):
        x, y, c = _place()
        return [[_remote(r.at[2 * (x ^ fx) + (y ^ fy)], o.at[f], ssem, rsem, w * nf + f, (x ^ fx, y ^ fy, c))
                 for w, (r, o) in enumerate(zip(ins, outs)) for f, (fx, fy) in enumerate(CHIP_FLIPS)]]

    return _Rider(parts, [_sds((nf,) + r.shape[1:], r.dtype) for r in parts], len(parts) * nf, phases)


def _pair_send_rider(halves):
    def phases(ins, outs, ssem, rsem):
        x, y, c = _place()
        return [[_remote(h, o, ssem, rsem, k, (x, y, 1 - c)) for k, (h, o) in enumerate(zip(ins, outs))]]

    return _Rider(halves, [_sds(h.shape, h.dtype) for h in halves], len(halves), phases)


PAIR_SUM_STEPS = 2
CHIP_SUM_STEPS = 4
ADAMW_STEPS = 4


def _no_rider():
    return _Rider([], [], 1, lambda ins, outs, ssem, rsem: [[]])


def _add_pair(grads, recvs, place, name, rider):
    n = len(grads)

    def body(pl_ref, *refs):
        ins, outs, _, ride = rider.split(refs, 2 * n, 2 * n, 0)
        j, h = pl.program_id(0), pl.program_id(1)

        @pl.when(jnp.logical_and(j == 0, h == 0))
        def _():
            rider.first(ride)

        for w in range(n):
            s = ins[2 * w][:, 0] + ins[2 * w + 1][...]
            outs[2 * w][...] = s
            outs[2 * w + 1][...] = s.astype(BF16)

        @pl.when(jnp.logical_and(j == N_CHIPS - 1, h == PAIR_SUM_STEPS - 1))
        def _():
            rider.last(ride)

    in_specs, out_specs, out_shape, args = [], [], [], []
    for g, r in zip(grads, recvs):
        _, _, H, C = g.shape
        th = H // PAIR_SUM_STEPS
        spec = pl.BlockSpec((1, th, C), lambda j, h, pr: (j, h, 0))
        in_specs += [pl.BlockSpec((1, 1, th, C), lambda j, h, pr: (j, pr[0], h, 0)), spec]
        out_specs += [spec, spec]
        out_shape += [_sds((N_CHIPS, H, C), F32), _sds((N_CHIPS, H, C), BF16)]
        args += [g, r]
    res = rider.call(body, args, grid=(N_CHIPS, PAIR_SUM_STEPS), name=name, prefetch=place, in_specs=in_specs,
                     out_specs=out_specs, out_shape=out_shape, scratch_shapes=[], vmem_mib=32)
    return [(res[2 * w], res[2 * w + 1]) for w in range(n)], res[2 * n:]


def _add_chips(parts, recvs, place, name, rider):
    n = len(parts)

    def body(pl_ref, *refs):
        ins, outs, _, ride = rider.split(refs, 2 * n, n, 0)
        h = pl.program_id(0)

        @pl.when(h == 0)
        def _():
            rider.first(ride)

        for w in range(n):
            p_ref, r_ref = ins[2 * w], ins[2 * w + 1]
            outs[w][...] = p_ref[0] + r_ref[0].astype(F32) + r_ref[1].astype(F32) + r_ref[2].astype(F32)

        @pl.when(h == CHIP_SUM_STEPS - 1)
        def _():
            rider.last(ride)

    in_specs, out_specs, out_shape, args = [], [], [], []
    for p, r in zip(parts, recvs):
        _, H, C = p.shape
        th = H // CHIP_SUM_STEPS
        in_specs += [pl.BlockSpec((1, th, C), lambda h, pr: (pr[1], h, 0)),
                     pl.BlockSpec((len(CHIP_FLIPS), th, C), lambda h, pr: (0, h, 0))]
        out_specs.append(pl.BlockSpec((th, C), lambda h, pr: (h, 0)))
        out_shape.append(_sds((H, C), F32))
        args += [p, r]
    res = rider.call(body, args, grid=(CHIP_SUM_STEPS,), name=name, prefetch=place, in_specs=in_specs,
                     out_specs=out_specs, out_shape=out_shape, scratch_shapes=[], vmem_mib=32)
    return res[:n], res[n:]


def _adamw_math(w, g, m, v):
    m = ADAM_B1 * m + (1.0 - ADAM_B1) * g
    v = ADAM_B2 * v + (1.0 - ADAM_B2) * (g * g)
    m_hat = m / (1.0 - ADAM_B1 ** ADAM_STEP)
    v_hat = v / (1.0 - ADAM_B2 ** ADAM_STEP)
    delta = -ADAM_LR * (m_hat / (jnp.sqrt(v_hat) + ADAM_EPS) + ADAM_WD * w)
    return delta, m, v


def _adamw(ws, mines, theirs, ms, vs, place, name, rider):
    n = len(ws)

    def body(pl_ref, *refs):
        ins, outs, _, ride = rider.split(refs, 5 * n, 4 * n, 0)
        h, r = pl.program_id(0), pl.program_id(1)

        @pl.when(jnp.logical_and(h == 0, r == 0))
        def _():
            rider.first(ride)

        for k in range(n):
            w_ref, a_ref, b_ref, m_ref, v_ref = ins[5 * k:5 * k + 5]
            g = jnp.where(h == pl_ref[0], a_ref[...], b_ref[...])
            d, mo, vo = _adamw_math(w_ref[...], g, m_ref[...], v_ref[...])
            for o_ref, val in zip(outs[4 * k:4 * k + 4], (g, d, mo, vo)):
                o_ref[...] = val

        @pl.when(jnp.logical_and(h == 1, r == ADAMW_STEPS - 1))
        def _():
            rider.last(ride)

    in_specs, out_specs, out_shape, args = [], [], [], []
    for w, a, b, m, v in zip(ws, mines, theirs, ms, vs):
        R, C = w.shape
        th = (R // 2) // ADAMW_STEPS
        whole = pl.BlockSpec((th, C), lambda h, r, pr: (h * ADAMW_STEPS + r, 0))
        mine_spec = pl.BlockSpec((th, C), lambda h, r, pr: (jnp.where(h == pr[0], r, 0), 0))
        theirs_spec = pl.BlockSpec((th, C), lambda h, r, pr: (jnp.where(h == pr[0], 0, r), 0))
        in_specs += [whole, mine_spec, theirs_spec, whole, whole]
        out_specs += [whole] * 4
        out_shape += [_sds((R, C), F32)] * 4
        args += [w, a, b, m, v]
    res = rider.call(body, args, grid=(2, ADAMW_STEPS), name=name, prefetch=place, in_specs=in_specs,
                     out_specs=out_specs, out_shape=out_shape, scratch_shapes=[], vmem_mib=40)
    return [tuple(res[4 * k:4 * k + 4]) for k in range(n)], res[4 * n:]


DEVICE_FLIPS = tuple((fx, fy, fc) for fx in (0, 1) for fy in (0, 1) for fc in (0, 1))[1:]


def _pack_exchange_rider(pack):
    def phases(ins, outs, ssem, rsem):
        x, y, c = _place()
        mine = outs[0].at[4 * x + 2 * y + c]
        copies = [_remote(ins[0], mine, ssem, rsem, k, (x ^ fx, y ^ fy, c ^ fc))
                  for k, (fx, fy, fc) in enumerate(DEVICE_FLIPS)]
        copies.append(functools.partial(pltpu.make_async_copy, ins[0], mine, ssem.at[len(DEVICE_FLIPS)]))
        return [copies]

    return _Rider([pack], [_sds((N_DEV,) + pack.shape, pack.dtype)], len(DEVICE_FLIPS) + 1, phases)


def _small_sum_adamw(recv_a, recv_b, wpack, mpack, vpack):
    R = wpack.shape[0]

    def body(a_ref, b_ref, w_ref, m_ref, v_ref, gs_ref, d_ref, mo_ref, vo_ref):
        ta, tb = a_ref[0], b_ref[0]
        for dev in range(1, N_DEV):
            ta = ta + a_ref[dev]
            tb = tb + b_ref[dev]
        total = jnp.concatenate([ta, tb], axis=0)
        gs_ref[...] = total
        d, mo, vo = _adamw_math(w_ref[...], total, m_ref[...], v_ref[...])
        d_ref[...] = d
        mo_ref[...] = mo
        vo_ref[...] = vo

    return pl.pallas_call(
        body, name="small_sum_adamw", in_specs=[VMEM_SPEC] * 5, out_specs=[VMEM_SPEC] * 4,
        out_shape=[_sds((R, LANES), F32)] * 4,
    )(recv_a, recv_b, wpack, mpack, vpack)


def _rows8(a):
    a = a.reshape(-1, LANES)
    pad = (-a.shape[0]) % 8
    return jnp.pad(a, ((0, pad), (0, 0))) if pad else a


def _pack(parts):
    return jnp.concatenate([_rows8(a) for a in parts], axis=0)


def _unpack(pack, like):
    out, row = [], 0
    for a in like:
        n = a.size // LANES
        out.append(pack[row:row + n].reshape(a.shape))
        row += n + (-n) % 8
    return out


def kernel(x, p, emb_ln_g, emb_ln_b, w_in, attn_out_g, w_pool, pool_scale, w_out, ln1_g, ln1_b, w_up, w_down, ln2_g, ln2_b, w_ple, w_ple_gate, ln3_g, ln3_b, loss_target, m_emb_ln_g, m_emb_ln_b, m_w_in, m_attn_out_g, m_w_pool, m_pool_scale, m_w_out, m_ln1_g, m_ln1_b, m_w_up, m_w_down, m_ln2_g, m_ln2_b, m_w_ple, m_w_ple_gate, m_ln3_g, m_ln3_b, v_emb_ln_g, v_emb_ln_b, v_w_in, v_attn_out_g, v_w_pool, v_pool_scale, v_w_out, v_ln1_g, v_ln1_b, v_w_up, v_w_down, v_ln2_g, v_ln2_b, v_w_ple, v_w_ple_gate, v_ln3_g, v_ln3_b):
    S = x.shape[1]
    tq = min(256, S)
    tm_mlp = min(1024, S)
    tm_pool = min(1024, S)
    tm_ln = min(512, S)
    tm_fwd = min(1024, S)
    xs = x[0]
    ps = p[0, 0]
    tgt = loss_target[0]
    row = lambda a: a.reshape(1, -1)
    g0, b0 = row(emb_ln_g), row(emb_ln_b)
    g1, b1, g2, b2, g3, b3 = ln1_g, ln1_b, ln2_g, ln2_b, ln3_g, ln3_b
    wp = w_pool[0]

    xi, yi, ci = _place()
    place = jnp.stack([ci, 2 * xi + yi]).astype(jnp.int32)
    names = ["w_in", "w_out", "w_up", "w_down", "w_ple", "w_ple_gate"]

    big = [w_in[0], w_out[0], w_up[0], w_down[0], w_ple[0], w_ple_gate[0]]
    s_in = _cast_into_slot(big[0], place, "cast_w_in")
    s_out, s_up, s_down, s_ple, s_gate, w_in_s = _cast_rest(big[1:], place, _gather_rider([s_in]))

    xh0, rstd0, q, k, v, u, s_out, s_ple, s_gate = _embln_inproj(
        xs, g0, b0, w_in_s, tm_fwd, _gather_rider([s_out, s_ple, s_gate], "chips"))
    o_raw, on, s_up, s_down, w_out_s, w_ple_s, w_gate_s = _attn_fwd(
        q, k, v, attn_out_g, tq, _gather_rider([s_up, s_down], "chips") + _gather_rider([s_out, s_ple, s_gate], "pair"))
    w_out_f = w_out_s.reshape(D_MODEL, D_MODEL)
    w_gate_f = w_gate_s.reshape(D_MODEL, D_MODEL)
    d_b, pooled = _pool_fwd(u, wp, pool_scale, tm_pool)
    xh1, rstd1, x1b, w_up_s, w_down_s = _mix_ln1(on, pooled, xh0, g0, b0, w_out_f, g1, b1, tm_fwd,
                                                 _gather_rider([s_up, s_down], "pair"))
    w_down_f = w_down_s.reshape(D_FF, D_MODEL)
    xh2, rstd2, rb = _mlp_ln2(xh1, x1b, g1, b1, w_up_s, w_down_f, tm_mlp, _no_rider())

    (dpre2, dhb, dw_ple, dw_gate, dg3, db3, dg2, db2, loss_row) = _ple_ln3_loss(
        xh2, rstd2, g2, b2, ps, w_ple_s, w_gate_f, g3, b3, tgt, tm_ln)
    def halves_of(g):
        return g.reshape(N_CHIPS, 2, g.shape[1] // 2, g.shape[2])

    ple_halves = [halves_of(dw_ple), halves_of(dw_gate.reshape(N_CHIPS, D_MODEL // N_CHIPS, D_MODEL))]
    dx1m, da, *ple_pair = _mlp_bwd(rb, dhb, w_up_s, w_down_f, tm_mlp, _pair_swap_rider(ple_halves))
    (dw_up,) = _tn_matmul(x1b, da, "grad_w_up", 1024, S, True, _no_rider())
    up_halves = halves_of(dw_up)
    dw_down, up_pair = _tn_matmul(rb, dhb, "grad_w_down", 1024, S, False,
                                  _pair_swap_rider([up_halves]), square_a=True)
    down_halves = halves_of(dw_down.reshape(N_CHIPS, D_FF // N_CHIPS, D_MODEL))
    dpre1, do, dpooled, dw_out, dg1, db1, dga, down_pair = _mix_bwd(
        dpre2, dx1m, xh1, rstd1, g1, w_out_f, on, pooled, o_raw, attn_out_g, tm_ln,
        _pair_swap_rider([down_halves]))
    out_halves = halves_of(dw_out.reshape(N_CHIPS, D_MODEL // N_CHIPS, D_MODEL))
    du, dwp, dsc, out_pair = _pool_bwd(dpooled, d_b, wp, pool_scale, tm_pool, _pair_swap_rider([out_halves]))
    early_sum, _ = _add_pair(
        [out_halves, up_halves, down_halves] + ple_halves, [out_pair, up_pair, down_pair] + ple_pair, place,
        "pair_sum_early", _no_rider())
    pack_a = _pack([jnp.broadcast_to(loss_row, (8, LANES)), dwp, dsc, dg1, db1, dg2, db2, dg3, db3])
    riding = _chip_scatter_rider([b for _, b in early_sum]) + _pack_exchange_rider(pack_a)
    dq, dk, dv, *arrived = _attn_bwd(q, k, v, do, tq, riding)
    early_chips, recv_a = arrived[:-1], arrived[-1]
    grad_x, dw_in, dg0, db0 = _inproj_bwd(dq, dk, dv, du, dpre1, xh0, rstd0, g0, b0, w_in_s, tm_ln)

    in_halves = halves_of(dw_in)
    pack_b = _pack([dg0, db0, dga])
    early_mine, (in_pair, recv_b) = _add_chips(
        [s for s, _ in early_sum], early_chips, place, "chip_sum_early",
        _pair_swap_rider([in_halves]) + _pack_exchange_rider(pack_b))
    (in_sum,), early_theirs = _add_pair([in_halves], [in_pair], place, "pair_sum_w_in", _pair_send_rider(early_mine))
    ms = [m_w_in, m_w_out, m_w_up, m_w_down, m_w_ple, m_w_ple_gate]
    vs = [v_w_in, v_w_out, v_w_up, v_w_down, v_w_ple, v_w_ple_gate]
    early_res, _ = _adamw(big[1:], early_mine, early_theirs, [m[0] for m in ms[1:]], [v[0] for v in vs[1:]],
                          place, "adamw_early", _no_rider())
    (in_chips,) = _chip_scatter_rider([in_sum[1]]).run("reduce_chips_late")
    (in_mine,), _ = _add_chips([in_sum[0]], [in_chips], place, "chip_sum_w_in", _no_rider())
    (in_theirs,) = _pair_send_rider([in_mine]).run("gather_pair_w_in")
    in_res, _ = _adamw(big[:1], [in_mine], [in_theirs], [ms[0][0]], [vs[0][0]], place, "adamw_w_in", _no_rider())
    big_out = {n: tuple(r.reshape(m.shape) for r in res4) for n, res4, m in zip(names, in_res + early_res, ms)}

    small_names = ["w_pool", "pool_scale", "ln1_g", "ln1_b", "ln2_g", "ln2_b", "ln3_g", "ln3_b",
                   "emb_ln_g", "emb_ln_b", "attn_out_g"]
    small_w = [w_pool, pool_scale, ln1_g, ln1_b, ln2_g, ln2_b, ln3_g, ln3_b, emb_ln_g, emb_ln_b, attn_out_g]
    small_m = [m_w_pool, m_pool_scale, m_ln1_g, m_ln1_b, m_ln2_g, m_ln2_b, m_ln3_g, m_ln3_b,
               m_emb_ln_g, m_emb_ln_b, m_attn_out_g]
    small_v = [v_w_pool, v_pool_scale, v_ln1_g, v_ln1_b, v_ln2_g, v_ln2_b, v_ln3_g, v_ln3_b,
               v_emb_ln_g, v_emb_ln_b, v_attn_out_g]
    loss_like = jnp.zeros((8, LANES), F32)
    gs, ds, mos, vos = _small_sum_adamw(recv_a, recv_b, _pack([loss_like] + small_w), _pack([loss_like] + small_m),
                                        _pack([jnp.ones((8, LANES), F32)] + small_v))
    like = [loss_like] + small_w
    gs_u, ds_u, mos_u, vos_u = (_unpack(a, like) for a in (gs, ds, mos, vos))
    loss = gs_u[0][0, 0]
    small_out = {n: (gs_u[i + 1], ds_u[i + 1], mos_u[i + 1], vos_u[i + 1]) for i, n in enumerate(small_names)}

    order = ["emb_ln_g", "emb_ln_b", "w_in", "attn_out_g", "w_pool", "pool_scale", "w_out", "ln1_g", "ln1_b",
             "w_up", "w_down", "ln2_g", "ln2_b", "w_ple", "w_ple_gate", "ln3_g", "ln3_b"]
    res = {**big_out, **small_out}
    outs = [loss, grad_x.reshape(x.shape)]
    for kind in range(4):
        outs += [res[n][kind] for n in order]
    return tuple(outs)
```

```python
import functools

import jax
import jax.numpy as jnp
from jax import lax
from jax.experimental import pallas as pl
from jax.experimental.pallas import tpu as pltpu

F32 = jnp.float32
BF16 = jnp.bfloat16

D_MODEL = 1024
ATTN_WIDTH = 512
POOL_WIDTH = 512
HEAD_DIM = 64
PAIR = 2 * HEAD_DIM
N_PAIRS = ATTN_WIDTH // PAIR
N_POOL_GROUPS = 4
POOL_GROUP = 128
POOL_HALO = 16
D_FF = 4096
PLE_DIM = 256
N_CHIPS = 4
N_DEV = 8
LN_EPS = 1e-5
RMS_EPS = 1e-6
ALPHA = float(2.0 ** 0.25)
Q_SCALE = 0.125
ADAM_LR = 0.001
ADAM_B1 = 0.9
ADAM_B2 = 0.999
ADAM_EPS = 1e-08
ADAM_WD = 0.01
ADAM_STEP = 10
LANES = 128
MIB = 1024 * 1024

MESH = pl.DeviceIdType.MESH
HBM_SPEC = pl.BlockSpec(memory_space=pltpu.HBM)
VMEM_SPEC = pl.BlockSpec(memory_space=pltpu.VMEM)


def _cp(vmem_mib):
    return pltpu.CompilerParams(vmem_limit_bytes=vmem_mib * MIB)


def _dot(a, b):
    return jnp.dot(a, b, preferred_element_type=F32)


def _dot_nt(a, b):
    return lax.dot_general(a, b, (((1,), (1,)), ((), ())), preferred_element_type=F32)


def _dot_tn(a, b):
    return lax.dot_general(a, b, (((0,), (0,)), ((), ())), preferred_element_type=F32)


def _ln_fwd(pre):
    mu = jnp.mean(pre, axis=-1, keepdims=True)
    xc = pre - mu
    var = jnp.mean(xc * xc, axis=-1, keepdims=True)
    rstd = lax.rsqrt(var + LN_EPS)
    return xc * rstd, rstd


def _ln_bwd(dy, xh, rstd, g):
    dxh = dy * g
    m1 = jnp.mean(dxh, axis=-1, keepdims=True)
    m2 = jnp.mean(dxh * xh, axis=-1, keepdims=True)
    return rstd * (dxh - m1 - xh * m2)


def _colsum(a):
    return jnp.sum(a, axis=0, keepdims=True)


def _neg_softplus(z):
    return -(jnp.maximum(z, 0.0) + jnp.log(1.0 + jnp.exp(-jnp.abs(z))))


def _seg_mean(a, m0):
    s0 = jnp.sum(jnp.where(m0, a, 0.0), axis=-1, keepdims=True)
    s1 = jnp.sum(jnp.where(m0, 0.0, a), axis=-1, keepdims=True)
    return jnp.where(m0, s0, s1) * (1.0 / HEAD_DIM)


def _rms_heads(o_ref, ga_ref):
    m0 = lax.broadcasted_iota(jnp.int32, (1, PAIR), 1) < HEAD_DIM
    out = []
    for p in range(N_PAIRS):
        cols = slice(p * PAIR, (p + 1) * PAIR)
        o = o_ref[:, cols]
        out.append((o, lax.rsqrt(_seg_mean(o * o, m0) + RMS_EPS), ga_ref[:, cols], cols, m0))
    return out


def _row_spec(tm, n):
    return pl.BlockSpec((tm, n), lambda i: (i, 0))


def _const_spec(shape):
    nd = len(shape)
    return pl.BlockSpec(shape, lambda *_: (0,) * nd)


def _hbm(*arrays):
    return [pltpu.with_memory_space_constraint(a, pltpu.HBM) for a in arrays]


def _sds(shape, dtype):
    return pltpu.HBM(shape, dtype)


def _embln_inproj(x, g0, b0, w_in_s, tm, rider):
    S, D = x.shape
    n_t = S // tm

    def body(*refs):
        ((x_ref, g_ref, b_ref, w_ref), (xh_ref, rstd_ref, q_ref, k_ref, v_ref, u_ref), _,
         ride) = rider.split(refs, 4, 6, 0)
        i = pl.program_id(0)

        @pl.when(i == 0)
        def _():
            rider.first(ride)

        @pl.when(i == (3 * n_t) // 4)
        def _():
            rider.mid(ride)

        xh, rstd = _ln_fwd(x_ref[...])
        xh_ref[...] = xh
        rstd_ref[...] = rstd
        xb = (xh * g_ref[...] + b_ref[...]).astype(BF16)
        q_ref[...] = (_dot(xb, w_ref[0]) * Q_SCALE).astype(BF16)
        k_ref[...] = _dot(xb, w_ref[1]).astype(BF16)
        v_ref[...] = _dot(xb, w_ref[2]).astype(BF16)
        u_ref[...] = _dot(xb, w_ref[3])

        @pl.when(i == n_t - 1)
        def _():
            rider.last(ride)

    return rider.call(
        body, [x, g0, b0, w_in_s], grid=(n_t,), name="embln_inproj",
        in_specs=[_row_spec(tm, D), _const_spec((1, D)), _const_spec((1, D)),
                  _const_spec((N_CHIPS, D, 512))],
        out_specs=[_row_spec(tm, D), _row_spec(tm, 1), _row_spec(tm, 512), _row_spec(tm, 512),
                   _row_spec(tm, 512), _row_spec(tm, 512)],
        out_shape=[_sds((S, D), F32), _sds((S, 1), F32), _sds((S, 512), BF16), _sds((S, 512), BF16),
                   _sds((S, 512), BF16), _sds((S, 512), F32)],
        scratch_shapes=[], vmem_mib=56)


def _tri(n, upper):
    r = lax.broadcasted_iota(jnp.int32, (n, n), 0)
    c = lax.broadcasted_iota(jnp.int32, (n, n), 1)
    keep = (r < c) if upper else (r > c)
    return jnp.where(keep, 1.0, 0.0).astype(BF16)


def _strictly_causal(n):
    return lax.broadcasted_iota(jnp.int32, (n, n), 1) < lax.broadcasted_iota(jnp.int32, (n, n), 0)


LOG_WEIGHT_FLOOR = -110.0


def _weights_left(c_ls):
    return (jnp.max(jnp.maximum(c_ls[0], c_ls[1])) > LOG_WEIGHT_FLOOR).astype(jnp.int32)


def _sb_tile(qhs, kt, low, c_ls, valid):
    valids = valid if isinstance(valid, (list, tuple)) else [valid] * len(qhs)
    zs = [_dot_nt(qh, kt) for qh in qhs]
    lrs = [_neg_softplus(z) for z in zs]
    ls_ = [lr if m is None else jnp.where(m, lr, 0.0) for lr, m in zip(lrs, valids)]
    sfx = [_dot(l.astype(BF16), low) + c_l for l, c_l in zip(ls_, c_ls)]
    lss = [z + lr for z, lr in zip(zs, lrs)]
    ws = [jnp.exp(ls + s) for ls, s in zip(lss, sfx)]
    ws = [w if m is None else jnp.where(m, w, 0.0) for w, m in zip(ws, valids)]
    return lss, ls_, ws


def _attn_fwd(q, k, v, tq, rider):
    S = q.shape[0]
    nq = S // tq

    def body(*refs):
        (q_ref, k_ref, v_ref), (o_ref,), (low_s,), ride = rider.split(refs, 3, 1, 1)
        p, i = pl.program_id(0), pl.program_id(1)

        @pl.when(jnp.logical_and(p == 0, i == 0))
        def _():
            rider.first(ride)
            low_s[...] = _tri(tq, upper=False)

        @pl.when(jnp.logical_and(p == N_PAIRS - 1, i == 0))
        def _():
            rider.mid(ride)

        lane = lax.broadcasted_iota(jnp.int32, (1, PAIR), 1)
        m0 = lane < HEAD_DIM
        low = low_s[...]
        q2 = q_ref[...]
        qhs = [jnp.where(m0, q2, jnp.zeros_like(q2)), jnp.where(m0, jnp.zeros_like(q2), q2)]

        def tile(kb, c_ls, accs, valid):
            ks = pl.multiple_of(kb * tq, tq)
            kt = k_ref[pl.ds(ks, tq), :]
            vt = v_ref[pl.ds(ks, tq), :]
            _, ls_, ws = _sb_tile(qhs, kt, low, c_ls, valid)
            new_a = [acc + _dot(w.astype(BF16), vt) for acc, w in zip(accs, ws)]
            new_c = [c_l + jnp.sum(l, axis=1, keepdims=True) for c_l, l in zip(c_ls, ls_)]
            return new_c, new_a

        zc, za = jnp.zeros((tq, 1), F32), jnp.zeros((tq, PAIR), F32)

        def first_two():
            c_ls, accs = tile(i, [zc, zc], [za, za], _strictly_causal(tq))
            c_ls, accs = tile(i - 1, c_ls, accs, None)
            return (_weights_left(c_ls), *c_ls, *accs)

        def first_one():
            c_ls, accs = tile(i, [zc, zc], [za, za], _strictly_causal(tq))
            return (jnp.int32(0), *c_ls, *accs)

        st0 = lax.cond(i >= 1, first_two, first_one)

        def more(st):
            return jnp.logical_and(st[0] <= i, st[1] > 0)

        def step(st):
            n, _, c0, c1, a0, a1 = st
            c_ls, accs = tile(i - n, [c0, c1], [a0, a1], None)
            return (n + 1, _weights_left(c_ls), c_ls[0], c_ls[1], accs[0], accs[1])

        st = lax.while_loop(more, step, (jnp.int32(2), *st0))
        o_ref[...] = jnp.where(m0, st[4], st[5])

        @pl.when(jnp.logical_and(p == N_PAIRS - 1, i == nq - 1))
        def _():
            rider.last(ride)

    return rider.call(
        body, [q, k, v], grid=(N_PAIRS, nq), name="attn_fwd",
        in_specs=[pl.BlockSpec((tq, PAIR), lambda p, i: (i, p)),
                  pl.BlockSpec((S, PAIR), lambda p, i: (0, p)),
                  pl.BlockSpec((S, PAIR), lambda p, i: (0, p))],
        out_specs=[pl.BlockSpec((tq, PAIR), lambda p, i: (i, p))],
        out_shape=[_sds((S, ATTN_WIDTH), F32)],
        scratch_shapes=[pltpu.VMEM((tq, tq), BF16)], vmem_mib=40)


def _pool_fwd(u, w_pool, pscale, tm):
    S = u.shape[0]
    hb = tm // POOL_HALO

    def body(u_ref, uh_ref, wp_ref, sc_ref, d_ref, pooled_ref):
        i = pl.program_id(0)
        halo = jnp.where(i > 0, uh_ref[...], 0.0)
        pos = i * tm + lax.broadcasted_iota(jnp.int32, (tm, 1), 0)
        for g in range(N_POOL_GROUPS):
            win = 2 ** (g + 1)
            cols = slice(g * POOL_GROUP, (g + 1) * POOL_GROUP)
            ut = u_ref[:, cols]
            s = jnp.concatenate([halo[:, cols], ut], axis=0)
            for sh in (1, 2, 4, 8)[:g + 1]:
                s = s + pltpu.roll(s, sh, 0)
            cnt = jnp.minimum(pos + 1, win).astype(F32)
            db = (s[POOL_HALO:, :] / cnt - ut).astype(BF16)
            y = _dot(db, wp_ref[g].astype(BF16))
            d_ref[:, cols] = db
            pooled_ref[:, cols] = (y * sc_ref[:, cols]).astype(BF16)

    return pl.pallas_call(
        body, grid=(S // tm,), name="pool_fwd",
        in_specs=[_row_spec(tm, POOL_WIDTH),
                  pl.BlockSpec((POOL_HALO, POOL_WIDTH), lambda i: (jnp.maximum(i * hb - 1, 0), 0)),
                  _const_spec((N_POOL_GROUPS, POOL_GROUP, POOL_GROUP)), _const_spec((1, POOL_WIDTH))],
        out_specs=[_row_spec(tm, POOL_WIDTH), _row_spec(tm, POOL_WIDTH)],
        out_shape=[_sds((S, POOL_WIDTH), BF16), _sds((S, POOL_WIDTH), BF16)],
        compiler_params=_cp(32),
    )(*_hbm(u, u, w_pool, pscale))


def _mix_ln1(o_raw, ga, pooled, xh0, g0, b0, w_out, g1, b1, tm, rider):
    S, D = xh0.shape
    n_t = S // tm

    def body(*refs):
        ((o_ref, ga_ref, po_ref, xh0_ref, g0_ref, b0_ref, w_ref, g1_ref, b1_ref),
         (xh_ref, rstd_ref, xb_ref), _, ride) = rider.split(refs, 9, 3, 0)

        @pl.when(pl.program_id(0) == 0)
        def _():
            rider.first(ride)

        on = jnp.concatenate([(o * rs * ga).astype(BF16) for o, rs, ga, _, _ in _rms_heads(o_ref, ga_ref)], axis=1)
        mixed = _dot(on, w_ref[:ATTN_WIDTH, :]) + _dot(po_ref[...], w_ref[ATTN_WIDTH:, :])
        x0 = xh0_ref[...] * g0_ref[...] + b0_ref[...]
        xh, rstd = _ln_fwd(ALPHA * x0 + mixed)
        xh_ref[...] = xh
        rstd_ref[...] = rstd
        xb_ref[...] = (xh * g1_ref[...] + b1_ref[...]).astype(BF16)

        @pl.when(pl.program_id(0) == n_t - 1)
        def _():
            rider.last(ride)

    return rider.call(
        body, [o_raw, ga, pooled, xh0, g0, b0, w_out, g1, b1], grid=(n_t,), name="mix_ln1",
        in_specs=[_row_spec(tm, ATTN_WIDTH), _const_spec((1, ATTN_WIDTH)), _row_spec(tm, POOL_WIDTH), _row_spec(tm, D),
                  _const_spec((1, D)), _const_spec((1, D)), _const_spec((D, D)),
                  _const_spec((1, D)), _const_spec((1, D))],
        out_specs=[_row_spec(tm, D), _row_spec(tm, 1), _row_spec(tm, D)],
        out_shape=[_sds((S, D), F32), _sds((S, 1), F32), _sds((S, D), BF16)],
        scratch_shapes=[], vmem_mib=56)


def _mlp_ln2(xh1, x1b, g1, b1, w_up_s, w_down, tm, rider):
    S, D = xh1.shape
    fc = D_FF // N_CHIPS
    n_t = S // tm

    def body(*refs):
        ((xh_ref, xb_ref, g_ref, b_ref, wu_ref, wd_ref), (xh2_ref, rstd_ref, r_ref), (acc_ref,),
         ride) = rider.split(refs, 6, 3, 1)
        i, j = pl.program_id(0), pl.program_id(1)

        @pl.when(jnp.logical_and(i == 0, j == 0))
        def _():
            rider.first(ride)

        @pl.when(j == 0)
        def _():
            acc_ref[...] = jnp.zeros_like(acc_ref)

        r = jnp.maximum(_dot(xb_ref[...], wu_ref[0]), 0.0)
        r_ref[...] = r.astype(BF16)
        acc_ref[...] += _dot((r * r).astype(BF16), wd_ref[...])

        @pl.when(j == N_CHIPS - 1)
        def _():
            x1 = xh_ref[...] * g_ref[...] + b_ref[...]
            xh, rstd = _ln_fwd(ALPHA * x1 + acc_ref[...])
            xh2_ref[...] = xh
            rstd_ref[...] = rstd

        @pl.when(jnp.logical_and(i == n_t - 1, j == N_CHIPS - 1))
        def _():
            rider.last(ride)

    return rider.call(
        body, [xh1, x1b, g1, b1, w_up_s, w_down], grid=(n_t, N_CHIPS), name="mlp_ln2",
        in_specs=[pl.BlockSpec((tm, D), lambda i, j: (i, 0)), pl.BlockSpec((tm, D), lambda i, j: (i, 0)),
                  pl.BlockSpec((1, D), lambda i, j: (0, 0)), pl.BlockSpec((1, D), lambda i, j: (0, 0)),
                  pl.BlockSpec((1, D, fc), lambda i, j: (j, 0, 0)),
                  pl.BlockSpec((fc, D), lambda i, j: (j, 0))],
        out_specs=[pl.BlockSpec((tm, D), lambda i, j: (i, 0)), pl.BlockSpec((tm, 1), lambda i, j: (i, 0)),
                   pl.BlockSpec((tm, fc), lambda i, j: (i, j))],
        out_shape=[_sds((S, D), F32), _sds((S, 1), F32), _sds((S, D_FF), BF16)],
        scratch_shapes=[pltpu.VMEM((tm, D), F32)], vmem_mib=56)


def _ple_ln3_loss(xh2, rstd2, g2, b2, p, w_ple_s, w_gate, g3, b3, target, tm):
    S, D = xh2.shape
    pc = D // N_CHIPS

    def body(xh2_ref, rstd2_ref, g2_ref, b2_ref, p_ref, wp_ref, wg_ref, g3_ref, b3_ref, t_ref,
             dpre2_ref, dhb_ref, dwp_ref, dwg_ref, dg3_ref, db3_ref, dg2_ref, db2_ref, loss_ref):
        i = pl.program_id(0)

        @pl.when(i == 0)
        def _():
            for r in (dwp_ref, dwg_ref, dg3_ref, db3_ref, dg2_ref, db2_ref, loss_ref):
                r[...] = jnp.zeros_like(r)

        xh2 = xh2_ref[...]
        x2 = xh2 * g2_ref[...] + b2_ref[...]
        x2b = x2.astype(BF16)
        gate = 1.0 / (1.0 + jnp.exp(-_dot(x2b, wg_ref[...])))
        pb = p_ref[...].astype(BF16)
        pe = jnp.concatenate([_dot(pb, wp_ref[c]) for c in range(N_CHIPS)], axis=1)
        xh3, rstd3 = _ln_fwd(ALPHA * x2 + pe * gate)
        diff = xh3 * g3_ref[...] + b3_ref[...] - t_ref[...]
        loss_ref[...] += (0.5 / D) * jnp.sum(diff * diff)
        dy = diff * (1.0 / D)
        dg3_ref[...] += _colsum(dy * xh3)
        db3_ref[...] += _colsum(dy)
        dpre3 = _ln_bwd(dy, xh3, rstd3, g3_ref[...])
        dpe_b = (dpre3 * gate).astype(BF16)
        dgp_b = (dpre3 * pe * gate * (1.0 - gate)).astype(BF16)
        dx2 = ALPHA * dpre3 + _dot_nt(dgp_b, wg_ref[...])
        dwg_ref[...] += _dot_tn(x2b, dgp_b)
        for c in range(N_CHIPS):
            dwp_ref[c] += _dot_tn(pb, dpe_b[:, c * pc:(c + 1) * pc])
        dg2_ref[...] += _colsum(dx2 * xh2)
        db2_ref[...] += _colsum(dx2)
        dpre2 = _ln_bwd(dx2, xh2, rstd2_ref[...], g2_ref[...])
        dpre2_ref[...] = dpre2
        dhb_ref[...] = dpre2.astype(BF16)

    vec = _const_spec((1, D))
    return pl.pallas_call(
        body, grid=(S // tm,), name="ple_ln3_loss",
        in_specs=[_row_spec(tm, D), _row_spec(tm, 1), vec, vec, _row_spec(tm, PLE_DIM),
                  _const_spec((N_CHIPS, PLE_DIM, pc)), _const_spec((D, D)), vec, vec, _row_spec(tm, D)],
        out_specs=[_row_spec(tm, D), _row_spec(tm, D), _const_spec((N_CHIPS, PLE_DIM, pc)),
                   _const_spec((D, D)), vec, vec, vec, vec, _const_spec((1, LANES))],
        out_shape=[_sds((S, D), F32), _sds((S, D), BF16), _sds((N_CHIPS, PLE_DIM, pc), F32),
                   _sds((D, D), F32), _sds((1, D), F32), _sds((1, D), F32), _sds((1, D), F32),
                   _sds((1, D), F32), _sds((1, LANES), F32)],
        compiler_params=_cp(58),
    )(*_hbm(xh2, rstd2, g2, b2, p, w_ple_s, w_gate, g3, b3, target))


def _mlp_bwd(rb, dhb, w_up_s, w_down, tm, rider):
    S, D = dhb.shape
    fc = D_FF // N_CHIPS
    n_t = S // tm

    def body(*refs):
        (r_ref, dh_ref, wu_ref, wd_ref), (dx_ref, da_ref), _, ride = rider.split(refs, 4, 2, 0)
        i, j = pl.program_id(0), pl.program_id(1)

        @pl.when(jnp.logical_and(i == 0, j == 0))
        def _():
            rider.first(ride)

        @pl.when(j == 0)
        def _():
            dx_ref[...] = jnp.zeros_like(dx_ref)

        da = (_dot_nt(dh_ref[...], wd_ref[...]) * (2.0 * r_ref[...].astype(F32))).astype(BF16)
        da_ref[...] = da
        dx_ref[...] += _dot_nt(da, wu_ref[0])

        @pl.when(jnp.logical_and(i == n_t - 1, j == N_CHIPS - 1))
        def _():
            rider.last(ride)

    return rider.call(
        body, [rb, dhb, w_up_s, w_down], grid=(n_t, N_CHIPS), name="mlp_bwd",
        in_specs=[pl.BlockSpec((tm, fc), lambda i, j: (i, j)), pl.BlockSpec((tm, D), lambda i, j: (i, 0)),
                  pl.BlockSpec((1, D, fc), lambda i, j: (j, 0, 0)),
                  pl.BlockSpec((fc, D), lambda i, j: (j, 0))],
        out_specs=[pl.BlockSpec((tm, D), lambda i, j: (i, 0)), pl.BlockSpec((tm, fc), lambda i, j: (i, j))],
        out_shape=[_sds((S, D), F32), _sds((S, D_FF), BF16)],
        scratch_shapes=[], vmem_mib=56)


def _tn_matmul(a, b, name, tk, tt, stacked, rider, square_a=False):
    T, K = a.shape
    N = b.shape[1]
    tn = 1024
    grid = (K // tk, N // tn, T // tt)

    def body(*refs):
        (a_ref, b_ref), (o_ref,), _, ride = rider.split(refs, 2, 1, 0)
        at = [pl.program_id(d) for d in range(3)]

        @pl.when(jnp.logical_and(jnp.logical_and(at[0] == 0, at[1] == 0), at[2] == 0))
        def _():
            rider.first(ride)

        @pl.when(at[2] == 0)
        def _():
            o_ref[...] = jnp.zeros_like(o_ref)

        a_t = a_ref[...]
        if square_a:
            a_t = a_t * a_t
        prod = _dot_tn(a_t, b_ref[...])
        if stacked:
            o_ref[0] += prod
        else:
            o_ref[...] += prod

        @pl.when(jnp.logical_and(jnp.logical_and(at[0] == grid[0] - 1, at[1] == grid[1] - 1),
                                 at[2] == grid[2] - 1))
        def _():
            rider.last(ride)

    if stacked:
        out_spec = pl.BlockSpec((1, tk, tn), lambda k, n, t: (n, k, 0))
        out_shape = _sds((N // tn, K, tn), F32)
    else:
        out_spec = pl.BlockSpec((tk, tn), lambda k, n, t: (k, n))
        out_shape = _sds((K, N), F32)
    return rider.call(
        body, [a, b], grid=grid, name=name,
        in_specs=[pl.BlockSpec((tt, tk), lambda k, n, t: (t, k)),
                  pl.BlockSpec((tt, tn), lambda k, n, t: (t, n))],
        out_specs=[out_spec], out_shape=[out_shape], scratch_shapes=[], vmem_mib=58)


def _mix_bwd(dpre2, dx1m, xh1, rstd1, g1, w_out, pooled, o_raw, ga, tm, rider):
    S, D = xh1.shape
    n_t = S // tm

    def body(*refs):
        ((dp2_ref, dxm_ref, xh_ref, rstd_ref, g_ref, w_ref, po_ref, o_ref, ga_ref),
         (dpre1_ref, do_ref, dpo_ref, dw_ref, dg_ref, db_ref, dga_ref), _, ride) = rider.split(refs, 9, 7, 0)

        @pl.when(pl.program_id(0) == 0)
        def _():
            rider.first(ride)
            for r in (dw_ref, dg_ref, db_ref, dga_ref):
                r[...] = jnp.zeros_like(r)

        xh = xh_ref[...]
        dx1 = ALPHA * dp2_ref[...] + dxm_ref[...]
        dg_ref[...] += _colsum(dx1 * xh)
        db_ref[...] += _colsum(dx1)
        dpre1 = _ln_bwd(dx1, xh, rstd_ref[...], g_ref[...])
        dpre1_ref[...] = dpre1
        dmb = dpre1.astype(BF16)
        dcat = _dot_nt(dmb, w_ref[...])
        dpo_ref[...] = dcat[:, ATTN_WIDTH:]
        dw_ref[ATTN_WIDTH:, :] += _dot_tn(po_ref[...], dmb)

        ons = []
        for o, rs, ga, cols, m0 in _rms_heads(o_ref, ga_ref):
            oh = o * rs
            ons.append((oh * ga).astype(BF16))
            don = dcat[:, cols]
            dga_ref[:, cols] += _colsum(don * oh)
            doh = don * ga
            do_ref[:, cols] = rs * (doh - oh * _seg_mean(doh * oh, m0))
        dw_ref[:ATTN_WIDTH, :] += _dot_tn(jnp.concatenate(ons, axis=1), dmb)

        @pl.when(pl.program_id(0) == n_t - 1)
        def _():
            rider.last(ride)

    vec = _const_spec((1, D))
    return rider.call(
        body, [dpre2, dx1m, xh1, rstd1, g1, w_out, pooled, o_raw, ga], grid=(n_t,), name="mix_bwd",
        in_specs=[_row_spec(tm, D), _row_spec(tm, D), _row_spec(tm, D), _row_spec(tm, 1), vec,
                  _const_spec((D, D)), _row_spec(tm, POOL_WIDTH),
                  _row_spec(tm, ATTN_WIDTH), _const_spec((1, ATTN_WIDTH))],
        out_specs=[_row_spec(tm, D), _row_spec(tm, ATTN_WIDTH), _row_spec(tm, POOL_WIDTH),
                   _const_spec((D, D)), vec, vec, _const_spec((1, ATTN_WIDTH))],
        out_shape=[_sds((S, D), F32), _sds((S, ATTN_WIDTH), F32), _sds((S, POOL_WIDTH), F32),
                   _sds((D, D), F32), _sds((1, D), F32), _sds((1, D), F32), _sds((1, ATTN_WIDTH), F32)],
        scratch_shapes=[], vmem_mib=56)


def _pool_bwd(dpooled, d_b, w_pool, pscale, tm, rider):
    S = dpooled.shape[0]
    hb = tm // POOL_HALO
    n_t = S // tm
    te = tm + POOL_HALO

    def body(*refs):
        ((dp_ref, dph_ref, d_ref, wp_ref, sc_ref), (du_ref, dwp_ref, dsc_ref), _,
         ride) = rider.split(refs, 5, 3, 0)
        i = pl.program_id(0)

        @pl.when(i == 0)
        def _():
            rider.first(ride)
            dwp_ref[...] = jnp.zeros_like(dwp_ref)
            dsc_ref[...] = jnp.zeros_like(dsc_ref)

        halo = jnp.where(i < n_t - 1, dph_ref[...], 0.0)
        pos = i * tm + lax.broadcasted_iota(jnp.int32, (te, 1), 0)
        for g in range(N_POOL_GROUPS):
            win = 2 ** (g + 1)
            cols = slice(g * POOL_GROUP, (g + 1) * POOL_GROUP)
            wpb = wp_ref[g].astype(BF16)
            dpt = dp_ref[:, cols]
            dpe = jnp.concatenate([dpt, halo[:, cols]], axis=0)
            dyb = (dpe * sc_ref[:, cols]).astype(BF16)
            dd = _dot_nt(dyb, wpb)
            s = dd / jnp.minimum(pos + 1, win).astype(F32)
            for sh in (1, 2, 4, 8)[:g + 1]:
                s = s + pltpu.roll(s, te - sh, 0)
            du_ref[:, cols] = s[:tm, :] - dd[:tm, :]
            db = d_ref[:, cols]
            dwp_ref[g] += _dot_tn(db, dyb[:tm, :])
            dsc_ref[:, cols] += _colsum(dpt * _dot(db, wpb))

        @pl.when(i == n_t - 1)
        def _():
            rider.last(ride)

    return rider.call(
        body, [dpooled, dpooled, d_b, w_pool, pscale], grid=(n_t,), name="pool_bwd",
        in_specs=[_row_spec(tm, POOL_WIDTH),
                  pl.BlockSpec((POOL_HALO, POOL_WIDTH),
                               lambda i: (jnp.minimum((i + 1) * hb, S // POOL_HALO - 1), 0)),
                  _row_spec(tm, POOL_WIDTH),
                  _const_spec((N_POOL_GROUPS, POOL_GROUP, POOL_GROUP)), _const_spec((1, POOL_WIDTH))],
        out_specs=[_row_spec(tm, POOL_WIDTH), _const_spec((N_POOL_GROUPS, POOL_GROUP, POOL_GROUP)),
                   _const_spec((1, POOL_WIDTH))],
        out_shape=[_sds((S, POOL_WIDTH), F32), _sds((N_POOL_GROUPS, POOL_GROUP, POOL_GROUP), F32),
                   _sds((1, POOL_WIDTH), F32)],
        scratch_shapes=[], vmem_mib=32)


def _attn_bwd(q, k, v, do, tq, rider):
    S = q.shape[0]
    nq = S // tq

    def body(*refs):
        ((q_ref, k_ref, v_ref, do_ref), (dq_ref, dk_ref, dv_ref),
         (g_s, b_s, low_s, upp_s), ride) = rider.split(refs, 4, 3, 4)
        p, i = pl.program_id(0), pl.program_id(1)

        @pl.when(jnp.logical_and(p == 0, i == 0))
        def _():
            rider.first(ride)
            low_s[...] = _tri(tq, upper=False)
            upp_s[...] = _tri(tq, upper=True)

        @pl.when(i == 0)
        def _():
            for r in (dk_ref, dv_ref):
                r[...] = jnp.zeros_like(r)

        lane = lax.broadcasted_iota(jnp.int32, (1, PAIR), 1)
        m0 = lane < HEAD_DIM
        low = low_s[...]
        upp = upp_s[...]
        dob = do_ref[...].astype(BF16)
        q2 = q_ref[...]
        qhs = [jnp.where(m0, q2, jnp.zeros_like(q2)), jnp.where(m0, jnp.zeros_like(q2), q2)]
        dhs = [jnp.where(m0, dob, jnp.zeros_like(dob)), jnp.where(m0, jnp.zeros_like(dob), dob)]
        causal = _strictly_causal(tq)

        def down(kb, c_ls, valid):
            ks = pl.multiple_of(kb * tq, tq)
            kt = k_ref[pl.ds(ks, tq), :]
            vt = v_ref[pl.ds(ks, tq), :]
            lss, ls_, ws = _sb_tile(qhs, kt, low, c_ls, valid)
            dws = [_dot_nt(dh, vt) for dh in dhs]
            for hh in range(2):
                g_s[hh, kb] = dws[hh] * ws[hh]
                b_s[hh, kb] = jnp.exp(lss[hh])
            dv_ref[pl.ds(ks, tq), :] += (_dot_tn(ws[0].astype(BF16), dhs[0])
                                         + _dot_tn(ws[1].astype(BF16), dhs[1]))
            return [c_l + jnp.sum(l, axis=1, keepdims=True) for c_l, l in zip(c_ls, ls_)]

        zc, za = jnp.zeros((tq, 1), F32), jnp.zeros((tq, PAIR), F32)
        def first_two():
            c_ls = down(i - 1, down(i, [zc, zc], causal), None)
            return (_weights_left(c_ls), *c_ls)

        st0 = lax.cond(i >= 1, first_two, lambda: (jnp.int32(0), *down(i, [zc, zc], causal)))

        def more(st):
            return jnp.logical_and(st[0] <= i, st[1] > 0)

        def down_step(st):
            c_ls = down(i - st[0], [st[2], st[3]], None)
            return (st[0] + 1, _weights_left(c_ls), c_ls[0], c_ls[1])

        n_tiles = lax.while_loop(more, down_step, (jnp.int32(2), *st0))[0]

        def up(kb, c_gs, accs, valid):
            ks = pl.multiple_of(kb * tq, tq)
            kt = k_ref[pl.ds(ks, tq), :]
            gs = [g_s[hh, kb] for hh in range(2)]
            pres = [_dot(g.astype(BF16), upp) + c_g for g, c_g in zip(gs, c_gs)]
            dzs = []
            for hh in range(2):
                beta = b_s[hh, kb]
                dz = gs[hh] - beta * (gs[hh] + pres[hh])
                if valid is not None:
                    dz = jnp.where(valid, dz, 0.0)
                dzs.append(dz.astype(BF16))
            new_a = [acc + _dot(dzb, kt) for acc, dzb in zip(accs, dzs)]
            dk_ref[pl.ds(ks, tq), :] += _dot_tn(dzs[0], qhs[0]) + _dot_tn(dzs[1], qhs[1])
            new_c = [c_g + jnp.sum(g, axis=1, keepdims=True) for c_g, g in zip(c_gs, gs)]
            return new_c, new_a

        def up_step(kb, st):
            c_gs, accs = up(kb, [st[0], st[1]], [st[2], st[3]], None)
            return (c_gs[0], c_gs[1], accs[0], accs[1])

        st = lax.fori_loop(i - n_tiles + 1, i - 1, up_step, (zc, zc, za, za))

        def last_two():
            c_gs, accs = up(i - 1, [st[0], st[1]], [st[2], st[3]], None)
            return tuple(up(i, c_gs, accs, causal)[1])

        accs = lax.cond(i >= 1, last_two, lambda: tuple(up(i, [zc, zc], [za, za], causal)[1]))
        dq_ref[...] = jnp.where(m0, accs[0], accs[1]) * Q_SCALE

        @pl.when(jnp.logical_and(p == N_PAIRS - 1, i == nq - 1))
        def _():
            rider.last(ride)

    return rider.call(
        body, [q, k, v, do], grid=(N_PAIRS, nq), name="attn_bwd",
        in_specs=[pl.BlockSpec((tq, PAIR), lambda p, i: (i, p)),
                  pl.BlockSpec((S, PAIR), lambda p, i: (0, p)),
                  pl.BlockSpec((S, PAIR), lambda p, i: (0, p)),
                  pl.BlockSpec((tq, PAIR), lambda p, i: (i, p))],
        out_specs=[pl.BlockSpec((tq, PAIR), lambda p, i: (i, p)),
                   pl.BlockSpec((S, PAIR), lambda p, i: (0, p)),
                   pl.BlockSpec((S, PAIR), lambda p, i: (0, p))],
        out_shape=[_sds((S, ATTN_WIDTH), F32), _sds((S, ATTN_WIDTH), F32), _sds((S, ATTN_WIDTH), F32)],
        scratch_shapes=[pltpu.VMEM((2, nq, tq, tq), F32), pltpu.VMEM((2, nq, tq, tq), F32),
                        pltpu.VMEM((tq, tq), BF16), pltpu.VMEM((tq, tq), BF16)],
        vmem_mib=56)


def _inproj_bwd(dq, dk, dv, du, dpre1, xh0, rstd0, g0, b0, w_in_s, tm):
    S, D = xh0.shape

    def body(dq_ref, dk_ref, dv_ref, du_ref, dp1_ref, xh_ref, rstd_ref, g_ref, b_ref, w_ref,
             gx_ref, dw_ref, dg_ref, db_ref):
        @pl.when(pl.program_id(0) == 0)
        def _():
            for r in (dw_ref, dg_ref, db_ref):
                r[...] = jnp.zeros_like(r)

        xh = xh_ref[...]
        xb = (xh * g_ref[...] + b_ref[...]).astype(BF16)
        dx0 = ALPHA * dp1_ref[...]
        for c, r in enumerate((dq_ref, dk_ref, dv_ref, du_ref)):
            dpb = r[...].astype(BF16)
            dx0 = dx0 + _dot_nt(dpb, w_ref[c])
            dw_ref[c] += _dot_tn(xb, dpb)
        dg_ref[...] += _colsum(dx0 * xh)
        db_ref[...] += _colsum(dx0)
        gx_ref[...] = _ln_bwd(dx0, xh, rstd_ref[...], g_ref[...])

    vec = _const_spec((1, D))
    half = _row_spec(tm, 512)
    return pl.pallas_call(
        body, grid=(S // tm,), name="inproj_bwd",
        in_specs=[half, half, half, half, _row_spec(tm, D), _row_spec(tm, D), _row_spec(tm, 1), vec, vec,
                  _const_spec((N_CHIPS, D, 512))],
        out_specs=[_row_spec(tm, D), _const_spec((N_CHIPS, D, 512)), vec, vec],
        out_shape=[_sds((S, D), F32), _sds((N_CHIPS, D, 512), F32), _sds((1, D), F32), _sds((1, D), F32)],
        compiler_params=_cp(58),
    )(*_hbm(dq, dk, dv, du, dpre1, xh0, rstd0, g0, b0, w_in_s))


def _place():
    return lax.axis_index("x"), lax.axis_index("y"), lax.axis_index("c")


CHIP_FLIPS = ((0, 1), (1, 0), (1, 1))


class _Rider:
    def __init__(self, ins, out_shapes, n_sem, phases, aliases=None):
        self.ins, self.out_shapes, self.n_sem, self.phases = list(ins), list(out_shapes), n_sem, phases
        self.aliases = aliases or {}

    def __add__(self, other):
        na, ma = len(self.ins), len(self.out_shapes)

        def phases(ins, outs, ssem, rsem):
            mine = self.phases(ins[:na], outs[:ma], ssem, rsem)
            rest = pl.ds(self.n_sem, other.n_sem)
            theirs = other.phases(ins[na:], outs[ma:], ssem.at[rest], rsem.at[rest])
            assert len(mine) == 1 and len(theirs) == 1
            return [mine[0] + theirs[0]]

        aliases = {**self.aliases, **{na + i: ma + o for i, o in other.aliases.items()}}
        return _Rider(self.ins + other.ins, self.out_shapes + other.out_shapes, self.n_sem + other.n_sem, phases,
                      aliases)

    def split(self, refs, n_in, n_out, n_scratch):
        a = n_in + len(self.ins)
        b = a + n_out
        c = b + len(self.out_shapes)
        own = (refs[:n_in], refs[a:b], refs[c:c + n_scratch])
        return own + ((refs[n_in:a], refs[b:c]) + tuple(refs[c + n_scratch:]),)

    def first(self, ride):
        for make in self.phases(*ride)[0]:
            make().start()

    def mid(self, ride):
        ph = self.phases(*ride)
        if len(ph) == 2:
            for make in ph[0]:
                make().wait_recv()
            for make in ph[1]:
                make().start()

    def last(self, ride):
        ph = self.phases(*ride)
        if len(ph) == 2:
            for make in ph[0]:
                make().wait_send()
        for make in ph[-1]:
            make().wait()

    def call(self, body, args, *, grid, name, in_specs, out_specs, out_shape, scratch_shapes, vmem_mib,
             prefetch=None):
        n_in, n_out = len(in_specs), len(out_specs)
        sems = [pltpu.SemaphoreType.DMA((self.n_sem,)), pltpu.SemaphoreType.DMA((self.n_sem,))]
        n_pre = 0 if prefetch is None else 1
        grid_spec = pltpu.PrefetchScalarGridSpec(
            num_scalar_prefetch=n_pre, grid=grid,
            in_specs=list(in_specs) + [HBM_SPEC] * len(self.ins),
            out_specs=list(out_specs) + [HBM_SPEC] * len(self.out_shapes),
            scratch_shapes=list(scratch_shapes) + sems)
        return pl.pallas_call(
            body, name=name, grid_spec=grid_spec,
            out_shape=list(out_shape) + self.out_shapes,
            input_output_aliases={n_pre + n_in + i: n_out + o for i, o in self.aliases.items()},
            compiler_params=_cp(vmem_mib),
        )(*([] if prefetch is None else [prefetch]), *_hbm(*args), *self.ins)

    def run(self, name):
        def body(*refs):
            ride = self.split(refs, 0, 0, 0)[3]
            self.first(ride)
            self.mid(ride)
            self.last(ride)

        return self.call(body, [], grid=(), name=name, in_specs=[], out_specs=[], out_shape=[],
                         scratch_shapes=[], vmem_mib=16)


def _remote(src, dst, ssem, rsem, n, dev):
    return functools.partial(pltpu.make_async_remote_copy, src_ref=src, dst_ref=dst, send_sem=ssem.at[n],
                             recv_sem=rsem.at[n], device_id=dev, device_id_type=MESH)


def _cast_into_slot(w, place, name):
    R, C = w.shape
    tr = min(R, 512)

    def body(pl_ref, w_ref, o_ref):
        o_ref[0] = w_ref[...].astype(BF16)

    return pl.pallas_call(
        body, name=name,
        grid_spec=pltpu.PrefetchScalarGridSpec(
            num_scalar_prefetch=1, grid=(R // tr,),
            in_specs=[pl.BlockSpec((tr, C), lambda r, pr: (r, 0))],
            out_specs=pl.BlockSpec((1, tr, C), lambda r, pr: (pr[1], r, 0))),
        out_shape=_sds((N_CHIPS, R, C), BF16),
    )(place, w)


CAST_STEPS = 8


def _cast_rest(ws, place, rider):
    n = len(ws)

    def body(pl_ref, *refs):
        w_refs, o_refs, _, ride = rider.split(refs, n, n, 0)
        r = pl.program_id(0)

        @pl.when(r == 0)
        def _():
            rider.first(ride)

        @pl.when(r == CAST_STEPS // 2)
        def _():
            rider.mid(ride)

        for w_ref, o_ref in zip(w_refs, o_refs):
            o_ref[0] = w_ref[...].astype(BF16)

        @pl.when(r == CAST_STEPS - 1)
        def _():
            rider.last(ride)

    def rows(w):
        return w.shape[0] // CAST_STEPS

    return rider.call(
        body, ws, grid=(CAST_STEPS,), name="cast_weights", prefetch=place,
        in_specs=[pl.BlockSpec((rows(w), w.shape[1]), lambda r, pr: (r, 0)) for w in ws],
        out_specs=[pl.BlockSpec((1, rows(w), w.shape[1]), lambda r, pr: (pr[1], r, 0)) for w in ws],
        out_shape=[_sds((N_CHIPS,) + w.shape, BF16) for w in ws], scratch_shapes=[], vmem_mib=32)


def _gather_rider(stacked, part="both"):
    n, nf = len(stacked), len(CHIP_FLIPS)

    def phases(ins, outs, ssem, rsem):
        x, y, c = _place()
        slot = 2 * x + y
        ici, d2d = [], []
        for w, (i_ref, o_ref) in enumerate(zip(ins, outs)):
            hh = o_ref.shape[1] // 2
            rows = pl.ds(c * hh, hh)
            for f, (fx, fy) in enumerate(CHIP_FLIPS):
                k = w * nf + f
                theirs = 2 * (x ^ fx) + (y ^ fy)
                if part != "pair":
                    ici.append(_remote(i_ref.at[slot, rows], o_ref.at[slot, rows], ssem, rsem, k,
                                       (x ^ fx, y ^ fy, c)))
                if part != "chips":
                    d2d.append(_remote(o_ref.at[theirs, rows], o_ref.at[theirs, rows], ssem, rsem,
                                       (n * nf if part == "both" else 0) + k, (x, y, 1 - c)))
        return [ph for ph in (ici, d2d) if ph]

    return _Rider(stacked, [_sds(s.shape, s.dtype) for s in stacked], (2 if part == "both" else 1) * n * nf,
                  phases, aliases={i: i for i in range(n)})


def _pair_swap_rider(grads):
    def phases(ins, outs, ssem, rsem):
        x, y, c = _place()
        return [[_remote(g.at[:, 1 - c], o, ssem, rsem, k, (x, y, 1 - c))
                 for k, (g, o) in enumerate(zip(ins, outs))]]

    return _Rider(grads, [_sds((N_CHIPS,) + g.shape[2:], g.dtype) for g in grads], len(grads), phases)


def _chip_scatter_rider(parts):
    nf = len(CHIP_FLIPS)

    def phases(ins, outs, ssem, rsem):
        x, y, c = _place()
        return [[_remote(r.at[2 * (x ^ fx) + (y ^ fy)], o.at[f], ssem, rsem, w * nf + f, (x ^ fx, y ^ fy, c))
                 for w, (r, o) in enumerate(zip(ins, outs)) for f, (fx, fy) in enumerate(CHIP_FLIPS)]]

    return _Rider(parts, [_sds((nf,) + r.shape[1:], r.dtype) for r in parts], len(parts) * nf, phases)


def _pair_send_rider(halves):
    def phases(ins, outs, ssem, rsem):
        x, y, c = _place()
        return [[_remote(h, o, ssem, rsem, k, (x, y, 1 - c)) for k, (h, o) in enumerate(zip(ins, outs))]]

    return _Rider(halves, [_sds(h.shape, h.dtype) for h in halves], len(halves), phases)


PAIR_SUM_STEPS = 2
CHIP_SUM_STEPS = 4
ADAMW_STEPS = 4


def _no_rider():
    return _Rider([], [], 1, lambda ins, outs, ssem, rsem: [[]])


def _add_pair(grads, recvs, place, name, rider):
    n = len(grads)

    def body(pl_ref, *refs):
        ins, outs, _, ride = rider.split(refs, 2 * n, 2 * n, 0)
        j, h = pl.program_id(0), pl.program_id(1)

        @pl.when(jnp.logical_and(j == 0, h == 0))
        def _():
            rider.first(ride)

        for w in range(n):
            s = ins[2 * w][:, 0] + ins[2 * w + 1][...]
            outs[2 * w][...] = s
            outs[2 * w + 1][...] = s.astype(BF16)

        @pl.when(jnp.logical_and(j == N_CHIPS - 1, h == PAIR_SUM_STEPS - 1))
        def _():
            rider.last(ride)

    in_specs, out_specs, out_shape, args = [], [], [], []
    for g, r in zip(grads, recvs):
        _, _, H, C = g.shape
        th = H // PAIR_SUM_STEPS
        spec = pl.BlockSpec((1, th, C), lambda j, h, pr: (j, h, 0))
        in_specs += [pl.BlockSpec((1, 1, th, C), lambda j, h, pr: (j, pr[0], h, 0)), spec]
        out_specs += [spec, spec]
        out_shape += [_sds((N_CHIPS, H, C), F32), _sds((N_CHIPS, H, C), BF16)]
        args += [g, r]
    res = rider.call(body, args, grid=(N_CHIPS, PAIR_SUM_STEPS), name=name, prefetch=place, in_specs=in_specs,
                     out_specs=out_specs, out_shape=out_shape, scratch_shapes=[], vmem_mib=32)
    return [(res[2 * w], res[2 * w + 1]) for w in range(n)], res[2 * n:]


def _add_chips(parts, recvs, place, name, rider):
    n = len(parts)

    def body(pl_ref, *refs):
        ins, outs, _, ride = rider.split(refs, 2 * n, n, 0)
        h = pl.program_id(0)

        @pl.when(h == 0)
        def _():
            rider.first(ride)

        for w in range(n):
            p_ref, r_ref = ins[2 * w], ins[2 * w + 1]
            outs[w][...] = p_ref[0] + r_ref[0].astype(F32) + r_ref[1].astype(F32) + r_ref[2].astype(F32)

        @pl.when(h == CHIP_SUM_STEPS - 1)
        def _():
            rider.last(ride)

    in_specs, out_specs, out_shape, args = [], [], [], []
    for p, r in zip(parts, recvs):
        _, H, C = p.shape
        th = H // CHIP_SUM_STEPS
        in_specs += [pl.BlockSpec((1, th, C), lambda h, pr: (pr[1], h, 0)),
                     pl.BlockSpec((len(CHIP_FLIPS), th, C), lambda h, pr: (0, h, 0))]
        out_specs.append(pl.BlockSpec((th, C), lambda h, pr: (h, 0)))
        out_shape.append(_sds((H, C), F32))
        args += [p, r]
    res = rider.call(body, args, grid=(CHIP_SUM_STEPS,), name=name, prefetch=place, in_specs=in_specs,
                     out_specs=out_specs, out_shape=out_shape, scratch_shapes=[], vmem_mib=32)
    return res[:n], res[n:]


def _adamw_math(w, g, m, v):
    m = ADAM_B1 * m + (1.0 - ADAM_B1) * g
    v = ADAM_B2 * v + (1.0 - ADAM_B2) * (g * g)
    m_hat = m / (1.0 - ADAM_B1 ** ADAM_STEP)
    v_hat = v / (1.0 - ADAM_B2 ** ADAM_STEP)
    delta = -ADAM_LR * (m_hat / (jnp.sqrt(v_hat) + ADAM_EPS) + ADAM_WD * w)
    return delta, m, v


def _adamw(ws, mines, theirs, ms, vs, place, name, rider):
    n = len(ws)

    def body(pl_ref, *refs):
        ins, outs, _, ride = rider.split(refs, 5 * n, 4 * n, 0)
        h, r = pl.program_id(0), pl.program_id(1)

        @pl.when(jnp.logical_and(h == 0, r == 0))
        def _():
            rider.first(ride)

        for k in range(n):
            w_ref, a_ref, b_ref, m_ref, v_ref = ins[5 * k:5 * k + 5]
            g = jnp.where(h == pl_ref[0], a_ref[...], b_ref[...])
            d, mo, vo = _adamw_math(w_ref[...], g, m_ref[...], v_ref[...])
            for o_ref, val in zip(outs[4 * k:4 * k + 4], (g, d, mo, vo)):
                o_ref[...] = val

        @pl.when(jnp.logical_and(h == 1, r == ADAMW_STEPS - 1))
        def _():
            rider.last(ride)

    in_specs, out_specs, out_shape, args = [], [], [], []
    for w, a, b, m, v in zip(ws, mines, theirs, ms, vs):
        R, C = w.shape
        th = (R // 2) // ADAMW_STEPS
        whole = pl.BlockSpec((th, C), lambda h, r, pr: (h * ADAMW_STEPS + r, 0))
        mine_spec = pl.BlockSpec((th, C), lambda h, r, pr: (jnp.where(h == pr[0], r, 0), 0))
        theirs_spec = pl.BlockSpec((th, C), lambda h, r, pr: (jnp.where(h == pr[0], 0, r), 0))
        in_specs += [whole, mine_spec, theirs_spec, whole, whole]
        out_specs += [whole] * 4
        out_shape += [_sds((R, C), F32)] * 4
        args += [w, a, b, m, v]
    res = rider.call(body, args, grid=(2, ADAMW_STEPS), name=name, prefetch=place, in_specs=in_specs,
                     out_specs=out_specs, out_shape=out_shape, scratch_shapes=[], vmem_mib=40)
    return [tuple(res[4 * k:4 * k + 4]) for k in range(n)], res[4 * n:]


DEVICE_FLIPS = tuple((fx, fy, fc) for fx in (0, 1) for fy in (0, 1) for fc in (0, 1))[1:]


def _pack_exchange_rider(pack):
    def phases(ins, outs, ssem, rsem):
        x, y, c = _place()
        mine = outs[0].at[4 * x + 2 * y + c]
        copies = [_remote(ins[0], mine, ssem, rsem, k, (x ^ fx, y ^ fy, c ^ fc))
                  for k, (fx, fy, fc) in enumerate(DEVICE_FLIPS)]
        copies.append(functools.partial(pltpu.make_async_copy, ins[0], mine, ssem.at[len(DEVICE_FLIPS)]))
        return [copies]

    return _Rider([pack], [_sds((N_DEV,) + pack.shape, pack.dtype)], len(DEVICE_FLIPS) + 1, phases)


def _small_sum_adamw(recv_a, recv_b, wpack, mpack, vpack):
    R = wpack.shape[0]

    def body(a_ref, b_ref, w_ref, m_ref, v_ref, gs_ref, d_ref, mo_ref, vo_ref):
        ta, tb = a_ref[0], b_ref[0]
        for dev in range(1, N_DEV):
            ta = ta + a_ref[dev]
            tb = tb + b_ref[dev]
        total = jnp.concatenate([ta, tb], axis=0)
        gs_ref[...] = total
        d, mo, vo = _adamw_math(w_ref[...], total, m_ref[...], v_ref[...])
        d_ref[...] = d
        mo_ref[...] = mo
        vo_ref[...] = vo

    return pl.pallas_call(
        body, name="small_sum_adamw", in_specs=[VMEM_SPEC] * 5, out_specs=[VMEM_SPEC] * 4,
        out_shape=[_sds((R, LANES), F32)] * 4,
    )(recv_a, recv_b, wpack, mpack, vpack)


def _rows8(a):
    a = a.reshape(-1, LANES)
    pad = (-a.shape[0]) % 8
    return jnp.pad(a, ((0, pad), (0, 0))) if pad else a


def _pack(parts):
    return jnp.concatenate([_rows8(a) for a in parts], axis=0)


def _unpack(pack, like):
    out, row = [], 0
    for a in like:
        n = a.size // LANES
        out.append(pack[row:row + n].reshape(a.shape))
        row += n + (-n) % 8
    return out


def kernel(x, p, emb_ln_g, emb_ln_b, w_in, attn_out_g, w_pool, pool_scale, w_out, ln1_g, ln1_b, w_up, w_down, ln2_g, ln2_b, w_ple, w_ple_gate, ln3_g, ln3_b, loss_target, m_emb_ln_g, m_emb_ln_b, m_w_in, m_attn_out_g, m_w_pool, m_pool_scale, m_w_out, m_ln1_g, m_ln1_b, m_w_up, m_w_down, m_ln2_g, m_ln2_b, m_w_ple, m_w_ple_gate, m_ln3_g, m_ln3_b, v_emb_ln_g, v_emb_ln_b, v_w_in, v_attn_out_g, v_w_pool, v_pool_scale, v_w_out, v_ln1_g, v_ln1_b, v_w_up, v_w_down, v_ln2_g, v_ln2_b, v_w_ple, v_w_ple_gate, v_ln3_g, v_ln3_b):
    S = x.shape[1]
    tq = min(256, S)
    tm_mlp = min(1024, S)
    tm_pool = min(1024, S)
    tm_ln = min(512, S)
    tm_fwd = min(1024, S)
    xs = x[0]
    ps = p[0, 0]
    tgt = loss_target[0]
    row = lambda a: a.reshape(1, -1)
    g0, b0 = row(emb_ln_g), row(emb_ln_b)
    g1, b1, g2, b2, g3, b3 = ln1_g, ln1_b, ln2_g, ln2_b, ln3_g, ln3_b
    wp = w_pool[0]

    xi, yi, ci = _place()
    place = jnp.stack([ci, 2 * xi + yi]).astype(jnp.int32)
    names = ["w_in", "w_out", "w_up", "w_down", "w_ple", "w_ple_gate"]

    big = [w_in[0], w_out[0], w_up[0], w_down[0], w_ple[0], w_ple_gate[0]]
    s_in = _cast_into_slot(big[0], place, "cast_w_in")
    s_out, s_up, s_down, s_ple, s_gate, w_in_s = _cast_rest(big[1:], place, _gather_rider([s_in]))

    xh0, rstd0, q, k, v, u, s_out, s_ple, s_gate = _embln_inproj(
        xs, g0, b0, w_in_s, tm_fwd, _gather_rider([s_out, s_ple, s_gate], "chips"))
    o_raw, s_up, s_down, w_out_s, w_ple_s, w_gate_s = _attn_fwd(
        q, k, v, tq, _gather_rider([s_up, s_down], "chips") + _gather_rider([s_out, s_ple, s_gate], "pair"))
    w_out_f = w_out_s.reshape(D_MODEL, D_MODEL)
    w_gate_f = w_gate_s.reshape(D_MODEL, D_MODEL)
    d_b, pooled = _pool_fwd(u, wp, pool_scale, tm_pool)
    xh1, rstd1, x1b, w_up_s, w_down_s = _mix_ln1(o_raw, attn_out_g, pooled, xh0, g0, b0, w_out_f, g1, b1, tm_fwd,
                                                 _gather_rider([s_up, s_down], "pair"))
    w_down_f = w_down_s.reshape(D_FF, D_MODEL)
    xh2, rstd2, rb = _mlp_ln2(xh1, x1b, g1, b1, w_up_s, w_down_f, tm_mlp, _no_rider())

    (dpre2, dhb, dw_ple, dw_gate, dg3, db3, dg2, db2, loss_row) = _ple_ln3_loss(
        xh2, rstd2, g2, b2, ps, w_ple_s, w_gate_f, g3, b3, tgt, tm_ln)
    def halves_of(g):
        return g.reshape(N_CHIPS, 2, g.shape[1] // 2, g.shape[2])

    ple_halves = [halves_of(dw_ple), halves_of(dw_gate.reshape(N_CHIPS, D_MODEL // N_CHIPS, D_MODEL))]
    dx1m, da, *ple_pair = _mlp_bwd(rb, dhb, w_up_s, w_down_f, tm_mlp, _pair_swap_rider(ple_halves))
    (dw_up,) = _tn_matmul(x1b, da, "grad_w_up", 1024, S, True, _no_rider())
    up_halves = halves_of(dw_up)
    dw_down, up_pair = _tn_matmul(rb, dhb, "grad_w_down", 1024, S, False,
                                  _pair_swap_rider([up_halves]), square_a=True)
    down_halves = halves_of(dw_down.reshape(N_CHIPS, D_FF // N_CHIPS, D_MODEL))
    dpre1, do, dpooled, dw_out, dg1, db1, dga, down_pair = _mix_bwd(
        dpre2, dx1m, xh1, rstd1, g1, w_out_f, pooled, o_raw, attn_out_g, tm_ln,
        _pair_swap_rider([down_halves]))
    out_halves = halves_of(dw_out.reshape(N_CHIPS, D_MODEL // N_CHIPS, D_MODEL))
    du, dwp, dsc, out_pair = _pool_bwd(dpooled, d_b, wp, pool_scale, tm_pool, _pair_swap_rider([out_halves]))
    early_sum, _ = _add_pair(
        [out_halves, up_halves, down_halves] + ple_halves, [out_pair, up_pair, down_pair] + ple_pair, place,
        "pair_sum_early", _no_rider())
    pack_a = _pack([jnp.broadcast_to(loss_row, (8, LANES)), dwp, dsc, dg1, db1, dg2, db2, dg3, db3])
    riding = _chip_scatter_rider([b for _, b in early_sum]) + _pack_exchange_rider(pack_a)
    dq, dk, dv, *arrived = _attn_bwd(q, k, v, do, tq, riding)
    early_chips, recv_a = arrived[:-1], arrived[-1]
    grad_x, dw_in, dg0, db0 = _inproj_bwd(dq, dk, dv, du, dpre1, xh0, rstd0, g0, b0, w_in_s, tm_ln)

    in_halves = halves_of(dw_in)
    pack_b = _pack([dg0, db0, dga])
    early_mine, (in_pair, recv_b) = _add_chips(
        [s for s, _ in early_sum], early_chips, place, "chip_sum_early",
        _pair_swap_rider([in_halves]) + _pack_exchange_rider(pack_b))
    (in_sum,), early_theirs = _add_pair([in_halves], [in_pair], place, "pair_sum_w_in", _pair_send_rider(early_mine))
    ms = [m_w_in, m_w_out, m_w_up, m_w_down, m_w_ple, m_w_ple_gate]
    vs = [v_w_in, v_w_out, v_w_up, v_w_down, v_w_ple, v_w_ple_gate]
    early_res, _ = _adamw(big[1:], early_mine, early_theirs, [m[0] for m in ms[1:]], [v[0] for v in vs[1:]],
                          place, "adamw_early", _no_rider())
    (in_chips,) = _chip_scatter_rider([in_sum[1]]).run("reduce_chips_late")
    (in_mine,), _ = _add_chips([in_sum[0]], [in_chips], place, "chip_sum_w_in", _no_rider())
    (in_theirs,) = _pair_send_rider([in_mine]).run("gather_pair_w_in")
    in_res, _ = _adamw(big[:1], [in_mine], [in_theirs], [ms[0][0]], [vs[0][0]], place, "adamw_w_in", _no_rider())
    big_out = {n: tuple(r.reshape(m.shape) for r in res4) for n, res4, m in zip(names, in_res + early_res, ms)}

    small_names = ["w_pool", "pool_scale", "ln1_g", "ln1_b", "ln2_g", "ln2_b", "ln3_g", "ln3_b",
                   "emb_ln_g", "emb_ln_b", "attn_out_g"]
    small_w = [w_pool, pool_scale, ln1_g, ln1_b, ln2_g, ln2_b, ln3_g, ln3_b, emb_ln_g, emb_ln_b, attn_out_g]
    small_m = [m_w_pool, m_pool_scale, m_ln1_g, m_ln1_b, m_ln2_g, m_ln2_b, m_ln3_g, m_ln3_b,
               m_emb_ln_g, m_emb_ln_b, m_attn_out_g]
    small_v = [v_w_pool, v_pool_scale, v_ln1_g, v_ln1_b, v_ln2_g, v_ln2_b, v_ln3_g, v_ln3_b,
               v_emb_ln_g, v_emb_ln_b, v_attn_out_g]
    loss_like = jnp.zeros((8, LANES), F32)
    gs, ds, mos, vos = _small_sum_adamw(recv_a, recv_b, _pack([loss_like] + small_w), _pack([loss_like] + small_m),
                                        _pack([jnp.ones((8, LANES), F32)] + small_v))
    like = [loss_like] + small_w
    gs_u, ds_u, mos_u, vos_u = (_unpack(a, like) for a in (gs, ds, mos, vos))
    loss = gs_u[0][0, 0]
    small_out = {n: (gs_u[i + 1], ds_u[i + 1], mos_u[i + 1], vos_u[i + 1]) for i, n in enumerate(small_names)}

    order = ["emb_ln_g", "emb_ln_b", "w_in", "attn_out_g", "w_pool", "pool_scale", "w_out", "ln1_g", "ln1_b",
             "w_up", "w_down", "ln2_g", "ln2_b", "w_ple", "w_ple_gate", "ln3_g", "ln3_b"]
    res = {**big_out, **small_out}
    outs = [loss, grad_x.reshape(x.shape)]
    for kind in range(4):
        outs += [res[n][kind] for n in order]
    return tuple(outs)
```

```python
import functools

import jax
import jax.numpy as jnp
from jax import lax
from jax.experimental import pallas as pl
from jax.experimental.pallas import tpu as pltpu

F32 = jnp.float32
BF16 = jnp.bfloat16

D_MODEL = 1024
ATTN_WIDTH = 512
POOL_WIDTH = 512
HEAD_DIM = 64
PAIR = 2 * HEAD_DIM
N_PAIRS = ATTN_WIDTH // PAIR
N_POOL_GROUPS = 4
POOL_GROUP = 128
POOL_HALO = 16
D_FF = 4096
PLE_DIM = 256
N_CHIPS = 4
N_DEV = 8
LN_EPS = 1e-5
RMS_EPS = 1e-6
ALPHA = float(2.0 ** 0.25)
Q_SCALE = 0.125
ADAM_LR = 0.001
ADAM_B1 = 0.9
ADAM_B2 = 0.999
ADAM_EPS = 1e-08
ADAM_WD = 0.01
ADAM_STEP = 10
LANES = 128
MIB = 1024 * 1024

MESH = pl.DeviceIdType.MESH
HBM_SPEC = pl.BlockSpec(memory_space=pltpu.HBM)
VMEM_SPEC = pl.BlockSpec(memory_space=pltpu.VMEM)


def _cp(vmem_mib):
    return pltpu.CompilerParams(vmem_limit_bytes=vmem_mib * MIB)


def _dot(a, b):
    return jnp.dot(a, b, preferred_element_type=F32)


def _dot_nt(a, b):
    return lax.dot_general(a, b, (((1,), (1,)), ((), ())), preferred_element_type=F32)


def _dot_tn(a, b):
    return lax.dot_general(a, b, (((0,), (0,)), ((), ())), preferred_element_type=F32)


def _ln_fwd(pre):
    mu = jnp.mean(pre, axis=-1, keepdims=True)
    xc = pre - mu
    var = jnp.mean(xc * xc, axis=-1, keepdims=True)
    rstd = lax.rsqrt(var + LN_EPS)
    return xc * rstd, rstd


def _ln_bwd(dy, xh, rstd, g):
    dxh = dy * g
    m1 = jnp.mean(dxh, axis=-1, keepdims=True)
    m2 = jnp.mean(dxh * xh, axis=-1, keepdims=True)
    return rstd * (dxh - m1 - xh * m2)


def _colsum(a):
    return jnp.sum(a, axis=0, keepdims=True)


def _neg_softplus(z):
    return -(jnp.maximum(z, 0.0) + jnp.log(1.0 + jnp.exp(-jnp.abs(z))))


def _row_spec(tm, n):
    return pl.BlockSpec((tm, n), lambda i: (i, 0))


def _const_spec(shape):
    nd = len(shape)
    return pl.BlockSpec(shape, lambda *_: (0,) * nd)


def _hbm(*arrays):
    return [pltpu.with_memory_space_constraint(a, pltpu.HBM) for a in arrays]


def _sds(shape, dtype):
    return pltpu.HBM(shape, dtype)


def _embln_inproj(x, g0, b0, w_in_s, tm, rider):
    S, D = x.shape
    n_t = S // tm

    def body(*refs):
        ((x_ref, g_ref, b_ref, w_ref), (xh_ref, rstd_ref, q_ref, k_ref, v_ref, u_ref), _,
         ride) = rider.split(refs, 4, 6, 0)
        i = pl.program_id(0)

        @pl.when(i == 0)
        def _():
            rider.first(ride)

        @pl.when(i == (3 * n_t) // 4)
        def _():
            rider.mid(ride)

        xh, rstd = _ln_fwd(x_ref[...])
        xh_ref[...] = xh
        rstd_ref[...] = rstd
        xb = (xh * g_ref[...] + b_ref[...]).astype(BF16)
        q_ref[...] = (_dot(xb, w_ref[0]) * Q_SCALE).astype(BF16)
        k_ref[...] = _dot(xb, w_ref[1]).astype(BF16)
        v_ref[...] = _dot(xb, w_ref[2]).astype(BF16)
        u_ref[...] = _dot(xb, w_ref[3])

        @pl.when(i == n_t - 1)
        def _():
            rider.last(ride)

    return rider.call(
        body, [x, g0, b0, w_in_s], grid=(n_t,), name="embln_inproj",
        in_specs=[_row_spec(tm, D), _const_spec((1, D)), _const_spec((1, D)),
                  _const_spec((N_CHIPS, D, 512))],
        out_specs=[_row_spec(tm, D), _row_spec(tm, 1), _row_spec(tm, 512), _row_spec(tm, 512),
                   _row_spec(tm, 512), _row_spec(tm, 512)],
        out_shape=[_sds((S, D), F32), _sds((S, 1), F32), _sds((S, 512), BF16), _sds((S, 512), BF16),
                   _sds((S, 512), BF16), _sds((S, 512), F32)],
        scratch_shapes=[], vmem_mib=56)


def _tri(n, upper):
    r = lax.broadcasted_iota(jnp.int32, (n, n), 0)
    c = lax.broadcasted_iota(jnp.int32, (n, n), 1)
    keep = (r < c) if upper else (r > c)
    return jnp.where(keep, 1.0, 0.0).astype(BF16)


def _strictly_causal(n):
    return lax.broadcasted_iota(jnp.int32, (n, n), 1) < lax.broadcasted_iota(jnp.int32, (n, n), 0)


LOG_WEIGHT_FLOOR = -110.0


def _weights_left(c_ls):
    return (jnp.max(jnp.maximum(c_ls[0], c_ls[1])) > LOG_WEIGHT_FLOOR).astype(jnp.int32)


def _sb_tile(qhs, kt, low, c_ls, valid):
    valids = valid if isinstance(valid, (list, tuple)) else [valid] * len(qhs)
    zs = [_dot_nt(qh, kt) for qh in qhs]
    lrs = [_neg_softplus(z) for z in zs]
    ls_ = [lr if m is None else jnp.where(m, lr, 0.0) for lr, m in zip(lrs, valids)]
    sfx = [_dot(l.astype(BF16), low) + c_l for l, c_l in zip(ls_, c_ls)]
    lss = [z + lr for z, lr in zip(zs, lrs)]
    ws = [jnp.exp(ls + s) for ls, s in zip(lss, sfx)]
    ws = [w if m is None else jnp.where(m, w, 0.0) for w, m in zip(ws, valids)]
    return lss, ls_, ws


def _attn_fwd(q, k, v, ga, tq, rider):
    S = q.shape[0]
    nq = S // tq

    def body(*refs):
        (q_ref, k_ref, v_ref, ga_ref), (o_ref, on_ref), (low_s,), ride = rider.split(refs, 4, 2, 1)
        p, i = pl.program_id(0), pl.program_id(1)

        @pl.when(jnp.logical_and(p == 0, i == 0))
        def _():
            rider.first(ride)
            low_s[...] = _tri(tq, upper=False)

        @pl.when(jnp.logical_and(p == N_PAIRS - 1, i == 0))
        def _():
            rider.mid(ride)

        lane = lax.broadcasted_iota(jnp.int32, (1, PAIR), 1)
        m0 = lane < HEAD_DIM
        low = low_s[...]
        q2 = q_ref[...]
        qhs = [jnp.where(m0, q2, jnp.zeros_like(q2)), jnp.where(m0, jnp.zeros_like(q2), q2)]

        def tile(kb, c_ls, accs, valid):
            ks = pl.multiple_of(kb * tq, tq)
            kt = k_ref[pl.ds(ks, tq), :]
            vt = v_ref[pl.ds(ks, tq), :]
            _, ls_, ws = _sb_tile(qhs, kt, low, c_ls, valid)
            new_a = [acc + _dot(w.astype(BF16), vt) for acc, w in zip(accs, ws)]
            new_c = [c_l + jnp.sum(l, axis=1, keepdims=True) for c_l, l in zip(c_ls, ls_)]
            return new_c, new_a

        zc, za = jnp.zeros((tq, 1), F32), jnp.zeros((tq, PAIR), F32)

        def first_two():
            c_ls, accs = tile(i, [zc, zc], [za, za], _strictly_causal(tq))
            c_ls, accs = tile(i - 1, c_ls, accs, None)
            return (_weights_left(c_ls), *c_ls, *accs)

        def first_one():
            c_ls, accs = tile(i, [zc, zc], [za, za], _strictly_causal(tq))
            return (jnp.int32(0), *c_ls, *accs)

        st0 = lax.cond(i >= 1, first_two, first_one)

        def more(st):
            return jnp.logical_and(st[0] <= i, st[1] > 0)

        def step(st):
            n, _, c0, c1, a0, a1 = st
            c_ls, accs = tile(i - n, [c0, c1], [a0, a1], None)
            return (n + 1, _weights_left(c_ls), c_ls[0], c_ls[1], accs[0], accs[1])

        st = lax.while_loop(more, step, (jnp.int32(2), *st0))
        o = jnp.where(m0, st[4], st[5])
        o_ref[...] = o
        sq = o * o
        ms0 = jnp.sum(jnp.where(m0, sq, 0.0), axis=-1, keepdims=True) * (1.0 / HEAD_DIM)
        ms1 = jnp.sum(jnp.where(m0, 0.0, sq), axis=-1, keepdims=True) * (1.0 / HEAD_DIM)
        rs = jnp.where(m0, lax.rsqrt(ms0 + RMS_EPS), lax.rsqrt(ms1 + RMS_EPS))
        on_ref[...] = (o * rs * ga_ref[...]).astype(BF16)

        @pl.when(jnp.logical_and(p == N_PAIRS - 1, i == nq - 1))
        def _():
            rider.last(ride)

    return rider.call(
        body, [q, k, v, ga], grid=(N_PAIRS, nq), name="attn_fwd",
        in_specs=[pl.BlockSpec((tq, PAIR), lambda p, i: (i, p)),
                  pl.BlockSpec((S, PAIR), lambda p, i: (0, p)),
                  pl.BlockSpec((S, PAIR), lambda p, i: (0, p)),
                  pl.BlockSpec((1, PAIR), lambda p, i: (0, p))],
        out_specs=[pl.BlockSpec((tq, PAIR), lambda p, i: (i, p)),
                   pl.BlockSpec((tq, PAIR), lambda p, i: (i, p))],
        out_shape=[_sds((S, ATTN_WIDTH), F32), _sds((S, ATTN_WIDTH), BF16)],
        scratch_shapes=[pltpu.VMEM((tq, tq), BF16)], vmem_mib=40)


def _pool_fwd(u, w_pool, pscale, tm):
    S = u.shape[0]
    hb = tm // POOL_HALO

    def body(u_ref, uh_ref, wp_ref, sc_ref, d_ref, pooled_ref):
        i = pl.program_id(0)
        halo = jnp.where(i > 0, uh_ref[...], 0.0)
        pos = i * tm + lax.broadcasted_iota(jnp.int32, (tm, 1), 0)
        for g in range(N_POOL_GROUPS):
            win = 2 ** (g + 1)
            cols = slice(g * POOL_GROUP, (g + 1) * POOL_GROUP)
            ut = u_ref[:, cols]
            s = jnp.concatenate([halo[:, cols], ut], axis=0)
            for sh in (1, 2, 4, 8)[:g + 1]:
                s = s + pltpu.roll(s, sh, 0)
            cnt = jnp.minimum(pos + 1, win).astype(F32)
            db = (s[POOL_HALO:, :] / cnt - ut).astype(BF16)
            y = _dot(db, wp_ref[g].astype(BF16))
            d_ref[:, cols] = db
            pooled_ref[:, cols] = (y * sc_ref[:, cols]).astype(BF16)

    return pl.pallas_call(
        body, grid=(S // tm,), name="pool_fwd",
        in_specs=[_row_spec(tm, POOL_WIDTH),
                  pl.BlockSpec((POOL_HALO, POOL_WIDTH), lambda i: (jnp.maximum(i * hb - 1, 0), 0)),
                  _const_spec((N_POOL_GROUPS, POOL_GROUP, POOL_GROUP)), _const_spec((1, POOL_WIDTH))],
        out_specs=[_row_spec(tm, POOL_WIDTH), _row_spec(tm, POOL_WIDTH)],
        out_shape=[_sds((S, POOL_WIDTH), BF16), _sds((S, POOL_WIDTH), BF16)],
        compiler_params=_cp(32),
    )(*_hbm(u, u, w_pool, pscale))


def _mix_ln1(on, pooled, xh0, g0, b0, w_out, g1, b1, tm, rider):
    S, D = xh0.shape
    n_t = S // tm

    def body(*refs):
        ((on_ref, po_ref, xh0_ref, g0_ref, b0_ref, w_ref, g1_ref, b1_ref), (xh_ref, rstd_ref, xb_ref), _,
         ride) = rider.split(refs, 8, 3, 0)

        @pl.when(pl.program_id(0) == 0)
        def _():
            rider.first(ride)

        mixed = _dot(on_ref[...], w_ref[:ATTN_WIDTH, :]) + _dot(po_ref[...], w_ref[ATTN_WIDTH:, :])
        x0 = xh0_ref[...] * g0_ref[...] + b0_ref[...]
        xh, rstd = _ln_fwd(ALPHA * x0 + mixed)
        xh_ref[...] = xh
        rstd_ref[...] = rstd
        xb_ref[...] = (xh * g1_ref[...] + b1_ref[...]).astype(BF16)

        @pl.when(pl.program_id(0) == n_t - 1)
        def _():
            rider.last(ride)

    return rider.call(
        body, [on, pooled, xh0, g0, b0, w_out, g1, b1], grid=(n_t,), name="mix_ln1",
        in_specs=[_row_spec(tm, ATTN_WIDTH), _row_spec(tm, POOL_WIDTH), _row_spec(tm, D),
                  _const_spec((1, D)), _const_spec((1, D)), _const_spec((D, D)),
                  _const_spec((1, D)), _const_spec((1, D))],
        out_specs=[_row_spec(tm, D), _row_spec(tm, 1), _row_spec(tm, D)],
        out_shape=[_sds((S, D), F32), _sds((S, 1), F32), _sds((S, D), BF16)],
        scratch_shapes=[], vmem_mib=56)


def _mlp_ln2(xh1, x1b, g1, b1, w_up_s, w_down, tm, rider):
    S, D = xh1.shape
    fc = D_FF // N_CHIPS
    n_t = S // tm

    def body(*refs):
        ((xh_ref, xb_ref, g_ref, b_ref, wu_ref, wd_ref), (xh2_ref, rstd_ref, r_ref), (acc_ref,),
         ride) = rider.split(refs, 6, 3, 1)
        i, j = pl.program_id(0), pl.program_id(1)

        @pl.when(jnp.logical_and(i == 0, j == 0))
        def _():
            rider.first(ride)

        @pl.when(j == 0)
        def _():
            acc_ref[...] = jnp.zeros_like(acc_ref)

        r = jnp.maximum(_dot(xb_ref[...], wu_ref[0]), 0.0)
        r_ref[...] = r.astype(BF16)
        acc_ref[...] += _dot((r * r).astype(BF16), wd_ref[...])

        @pl.when(j == N_CHIPS - 1)
        def _():
            x1 = xh_ref[...] * g_ref[...] + b_ref[...]
            xh, rstd = _ln_fwd(ALPHA * x1 + acc_ref[...])
            xh2_ref[...] = xh
            rstd_ref[...] = rstd

        @pl.when(jnp.logical_and(i == n_t - 1, j == N_CHIPS - 1))
        def _():
            rider.last(ride)

    return rider.call(
        body, [xh1, x1b, g1, b1, w_up_s, w_down], grid=(n_t, N_CHIPS), name="mlp_ln2",
        in_specs=[pl.BlockSpec((tm, D), lambda i, j: (i, 0)), pl.BlockSpec((tm, D), lambda i, j: (i, 0)),
                  pl.BlockSpec((1, D), lambda i, j: (0, 0)), pl.BlockSpec((1, D), lambda i, j: (0, 0)),
                  pl.BlockSpec((1, D, fc), lambda i, j: (j, 0, 0)),
                  pl.BlockSpec((fc, D), lambda i, j: (j, 0))],
        out_specs=[pl.BlockSpec((tm, D), lambda i, j: (i, 0)), pl.BlockSpec((tm, 1), lambda i, j: (i, 0)),
                   pl.BlockSpec((tm, fc), lambda i, j: (i, j))],
        out_shape=[_sds((S, D), F32), _sds((S, 1), F32), _sds((S, D_FF), BF16)],
        scratch_shapes=[pltpu.VMEM((tm, D), F32)], vmem_mib=56)


def _ple_ln3_loss(xh2, rstd2, g2, b2, p, w_ple_s, w_gate, g3, b3, target, tm):
    S, D = xh2.shape
    pc = D // N_CHIPS

    def body(xh2_ref, rstd2_ref, g2_ref, b2_ref, p_ref, wp_ref, wg_ref, g3_ref, b3_ref, t_ref,
             dpre2_ref, dhb_ref, dwp_ref, dwg_ref, dg3_ref, db3_ref, dg2_ref, db2_ref, loss_ref):
        i = pl.program_id(0)

        @pl.when(i == 0)
        def _():
            for r in (dwp_ref, dwg_ref, dg3_ref, db3_ref, dg2_ref, db2_ref, loss_ref):
                r[...] = jnp.zeros_like(r)

        xh2 = xh2_ref[...]
        x2 = xh2 * g2_ref[...] + b2_ref[...]
        x2b = x2.astype(BF16)
        gate = 1.0 / (1.0 + jnp.exp(-_dot(x2b, wg_ref[...])))
        pb = p_ref[...].astype(BF16)
        pe = jnp.concatenate([_dot(pb, wp_ref[c]) for c in range(N_CHIPS)], axis=1)
        xh3, rstd3 = _ln_fwd(ALPHA * x2 + pe * gate)
        diff = xh3 * g3_ref[...] + b3_ref[...] - t_ref[...]
        loss_ref[...] += (0.5 / D) * jnp.sum(diff * diff)
        dy = diff * (1.0 / D)
        dg3_ref[...] += _colsum(dy * xh3)
        db3_ref[...] += _colsum(dy)
        dpre3 = _ln_bwd(dy, xh3, rstd3, g3_ref[...])
        dpe_b = (dpre3 * gate).astype(BF16)
        dgp_b = (dpre3 * pe * gate * (1.0 - gate)).astype(BF16)
        dx2 = ALPHA * dpre3 + _dot_nt(dgp_b, wg_ref[...])
        dwg_ref[...] += _dot_tn(x2b, dgp_b)
        for c in range(N_CHIPS):
            dwp_ref[c] += _dot_tn(pb, dpe_b[:, c * pc:(c + 1) * pc])
        dg2_ref[...] += _colsum(dx2 * xh2)
        db2_ref[...] += _colsum(dx2)
        dpre2 = _ln_bwd(dx2, xh2, rstd2_ref[...], g2_ref[...])
        dpre2_ref[...] = dpre2
        dhb_ref[...] = dpre2.astype(BF16)

    vec = _const_spec((1, D))
    return pl.pallas_call(
        body, grid=(S // tm,), name="ple_ln3_loss",
        in_specs=[_row_spec(tm, D), _row_spec(tm, 1), vec, vec, _row_spec(tm, PLE_DIM),
                  _const_spec((N_CHIPS, PLE_DIM, pc)), _const_spec((D, D)), vec, vec, _row_spec(tm, D)],
        out_specs=[_row_spec(tm, D), _row_spec(tm, D), _const_spec((N_CHIPS, PLE_DIM, pc)),
                   _const_spec((D, D)), vec, vec, vec, vec, _const_spec((1, LANES))],
        out_shape=[_sds((S, D), F32), _sds((S, D), BF16), _sds((N_CHIPS, PLE_DIM, pc), F32),
                   _sds((D, D), F32), _sds((1, D), F32), _sds((1, D), F32), _sds((1, D), F32),
                   _sds((1, D), F32), _sds((1, LANES), F32)],
        compiler_params=_cp(58),
    )(*_hbm(xh2, rstd2, g2, b2, p, w_ple_s, w_gate, g3, b3, target))


def _mlp_bwd(rb, dhb, w_up_s, w_down, tm, rider):
    S, D = dhb.shape
    fc = D_FF // N_CHIPS
    n_t = S // tm

    def body(*refs):
        (r_ref, dh_ref, wu_ref, wd_ref), (dx_ref, da_ref), _, ride = rider.split(refs, 4, 2, 0)
        i, j = pl.program_id(0), pl.program_id(1)

        @pl.when(jnp.logical_and(i == 0, j == 0))
        def _():
            rider.first(ride)

        @pl.when(j == 0)
        def _():
            dx_ref[...] = jnp.zeros_like(dx_ref)

        da = (_dot_nt(dh_ref[...], wd_ref[...]) * (2.0 * r_ref[...].astype(F32))).astype(BF16)
        da_ref[...] = da
        dx_ref[...] += _dot_nt(da, wu_ref[0])

        @pl.when(jnp.logical_and(i == n_t - 1, j == N_CHIPS - 1))
        def _():
            rider.last(ride)

    return rider.call(
        body, [rb, dhb, w_up_s, w_down], grid=(n_t, N_CHIPS), name="mlp_bwd",
        in_specs=[pl.BlockSpec((tm, fc), lambda i, j: (i, j)), pl.BlockSpec((tm, D), lambda i, j: (i, 0)),
                  pl.BlockSpec((1, D, fc), lambda i, j: (j, 0, 0)),
                  pl.BlockSpec((fc, D), lambda i, j: (j, 0))],
        out_specs=[pl.BlockSpec((tm, D), lambda i, j: (i, 0)), pl.BlockSpec((tm, fc), lambda i, j: (i, j))],
        out_shape=[_sds((S, D), F32), _sds((S, D_FF), BF16)],
        scratch_shapes=[], vmem_mib=56)


def _tn_matmul(a, b, name, tk, tt, stacked, rider, square_a=False):
    T, K = a.shape
    N = b.shape[1]
    tn = 1024
    grid = (K // tk, N // tn, T // tt)

    def body(*refs):
        (a_ref, b_ref), (o_ref,), _, ride = rider.split(refs, 2, 1, 0)
        at = [pl.program_id(d) for d in range(3)]

        @pl.when(jnp.logical_and(jnp.logical_and(at[0] == 0, at[1] == 0), at[2] == 0))
        def _():
            rider.first(ride)

        @pl.when(at[2] == 0)
        def _():
            o_ref[...] = jnp.zeros_like(o_ref)

        a_t = a_ref[...]
        if square_a:
            a_t = a_t * a_t
        prod = _dot_tn(a_t, b_ref[...])
        if stacked:
            o_ref[0] += prod
        else:
            o_ref[...] += prod

        @pl.when(jnp.logical_and(jnp.logical_and(at[0] == grid[0] - 1, at[1] == grid[1] - 1),
                                 at[2] == grid[2] - 1))
        def _():
            rider.last(ride)

    if stacked:
        out_spec = pl.BlockSpec((1, tk, tn), lambda k, n, t: (n, k, 0))
        out_shape = _sds((N // tn, K, tn), F32)
    else:
        out_spec = pl.BlockSpec((tk, tn), lambda k, n, t: (k, n))
        out_shape = _sds((K, N), F32)
    return rider.call(
        body, [a, b], grid=grid, name=name,
        in_specs=[pl.BlockSpec((tt, tk), lambda k, n, t: (t, k)),
                  pl.BlockSpec((tt, tn), lambda k, n, t: (t, n))],
        out_specs=[out_spec], out_shape=[out_shape], scratch_shapes=[], vmem_mib=58)


def _mix_bwd(dpre2, dx1m, xh1, rstd1, g1, w_out, on, pooled, o_raw, ga, tm, rider):
    S, D = xh1.shape
    n_t = S // tm

    def body(*refs):
        ((dp2_ref, dxm_ref, xh_ref, rstd_ref, g_ref, w_ref, on_ref, po_ref, o_ref, ga_ref),
         (dpre1_ref, do_ref, dpo_ref, dw_ref, dg_ref, db_ref, dga_ref), _, ride) = rider.split(refs, 10, 7, 0)

        @pl.when(pl.program_id(0) == 0)
        def _():
            rider.first(ride)
            for r in (dw_ref, dg_ref, db_ref, dga_ref):
                r[...] = jnp.zeros_like(r)

        xh = xh_ref[...]
        dx1 = ALPHA * dp2_ref[...] + dxm_ref[...]
        dg_ref[...] += _colsum(dx1 * xh)
        db_ref[...] += _colsum(dx1)
        dpre1 = _ln_bwd(dx1, xh, rstd_ref[...], g_ref[...])
        dpre1_ref[...] = dpre1
        dmb = dpre1.astype(BF16)
        dcat = _dot_nt(dmb, w_ref[...])
        dpo_ref[...] = dcat[:, ATTN_WIDTH:]
        dw_ref[:ATTN_WIDTH, :] += _dot_tn(on_ref[...], dmb)
        dw_ref[ATTN_WIDTH:, :] += _dot_tn(po_ref[...], dmb)

        m0 = lax.broadcasted_iota(jnp.int32, (1, PAIR), 1) < HEAD_DIM

        def seg_mean(a):
            s0 = jnp.sum(jnp.where(m0, a, 0.0), axis=-1, keepdims=True)
            s1 = jnp.sum(jnp.where(m0, 0.0, a), axis=-1, keepdims=True)
            return jnp.where(m0, s0, s1) * (1.0 / HEAD_DIM)

        for p in range(N_PAIRS):
            cols = slice(p * PAIR, (p + 1) * PAIR)
            o = o_ref[:, cols]
            rs = lax.rsqrt(seg_mean(o * o) + RMS_EPS)
            oh = o * rs
            don = dcat[:, cols]
            dga_ref[:, cols] += _colsum(don * oh)
            doh = don * ga_ref[:, cols]
            do_ref[:, cols] = rs * (doh - oh * seg_mean(doh * oh))

        @pl.when(pl.program_id(0) == n_t - 1)
        def _():
            rider.last(ride)

    vec = _const_spec((1, D))
    return rider.call(
        body, [dpre2, dx1m, xh1, rstd1, g1, w_out, on, pooled, o_raw, ga], grid=(n_t,), name="mix_bwd",
        in_specs=[_row_spec(tm, D), _row_spec(tm, D), _row_spec(tm, D), _row_spec(tm, 1), vec,
                  _const_spec((D, D)), _row_spec(tm, ATTN_WIDTH), _row_spec(tm, POOL_WIDTH),
                  _row_spec(tm, ATTN_WIDTH), _const_spec((1, ATTN_WIDTH))],
        out_specs=[_row_spec(tm, D), _row_spec(tm, ATTN_WIDTH), _row_spec(tm, POOL_WIDTH),
                   _const_spec((D, D)), vec, vec, _const_spec((1, ATTN_WIDTH))],
        out_shape=[_sds((S, D), F32), _sds((S, ATTN_WIDTH), F32), _sds((S, POOL_WIDTH), F32),
                   _sds((D, D), F32), _sds((1, D), F32), _sds((1, D), F32), _sds((1, ATTN_WIDTH), F32)],
        scratch_shapes=[], vmem_mib=56)


def _pool_bwd(dpooled, d_b, w_pool, pscale, tm, rider):
    S = dpooled.shape[0]
    hb = tm // POOL_HALO
    n_t = S // tm
    te = tm + POOL_HALO

    def body(*refs):
        ((dp_ref, dph_ref, d_ref, wp_ref, sc_ref), (du_ref, dwp_ref, dsc_ref), _,
         ride) = rider.split(refs, 5, 3, 0)
        i = pl.program_id(0)

        @pl.when(i == 0)
        def _():
            rider.first(ride)
            dwp_ref[...] = jnp.zeros_like(dwp_ref)
            dsc_ref[...] = jnp.zeros_like(dsc_ref)

        halo = jnp.where(i < n_t - 1, dph_ref[...], 0.0)
        pos = i * tm + lax.broadcasted_iota(jnp.int32, (te, 1), 0)
        for g in range(N_POOL_GROUPS):
            win = 2 ** (g + 1)
            cols = slice(g * POOL_GROUP, (g + 1) * POOL_GROUP)
            wpb = wp_ref[g].astype(BF16)
            dpt = dp_ref[:, cols]
            dpe = jnp.concatenate([dpt, halo[:, cols]], axis=0)
            dyb = (dpe * sc_ref[:, cols]).astype(BF16)
            dd = _dot_nt(dyb, wpb)
            s = dd / jnp.minimum(pos + 1, win).astype(F32)
            for sh in (1, 2, 4, 8)[:g + 1]:
                s = s + pltpu.roll(s, te - sh, 0)
            du_ref[:, cols] = s[:tm, :] - dd[:tm, :]
            db = d_ref[:, cols]
            dwp_ref[g] += _dot_tn(db, dyb[:tm, :])
            dsc_ref[:, cols] += _colsum(dpt * _dot(db, wpb))

        @pl.when(i == n_t - 1)
        def _():
            rider.last(ride)

    return rider.call(
        body, [dpooled, dpooled, d_b, w_pool, pscale], grid=(n_t,), name="pool_bwd",
        in_specs=[_row_spec(tm, POOL_WIDTH),
                  pl.BlockSpec((POOL_HALO, POOL_WIDTH),
                               lambda i: (jnp.minimum((i + 1) * hb, S // POOL_HALO - 1), 0)),
                  _row_spec(tm, POOL_WIDTH),
                  _const_spec((N_POOL_GROUPS, POOL_GROUP, POOL_GROUP)), _const_spec((1, POOL_WIDTH))],
        out_specs=[_row_spec(tm, POOL_WIDTH), _const_spec((N_POOL_GROUPS, POOL_GROUP, POOL_GROUP)),
                   _const_spec((1, POOL_WIDTH))],
        out_shape=[_sds((S, POOL_WIDTH), F32), _sds((N_POOL_GROUPS, POOL_GROUP, POOL_GROUP), F32),
                   _sds((1, POOL_WIDTH), F32)],
        scratch_shapes=[], vmem_mib=32)


def _attn_bwd(q, k, v, do, tq, rider):
    S = q.shape[0]
    nq = S // tq

    def body(*refs):
        ((q_ref, k_ref, v_ref, do_ref), (dq_ref, dk_ref, dv_ref),
         (g_s, b_s, low_s, upp_s), ride) = rider.split(refs, 4, 3, 4)
        p, i = pl.program_id(0), pl.program_id(1)

        @pl.when(jnp.logical_and(p == 0, i == 0))
        def _():
            rider.first(ride)
            low_s[...] = _tri(tq, upper=False)
            upp_s[...] = _tri(tq, upper=True)

        @pl.when(i == 0)
        def _():
            for r in (dk_ref, dv_ref):
                r[...] = jnp.zeros_like(r)

        lane = lax.broadcasted_iota(jnp.int32, (1, PAIR), 1)
        m0 = lane < HEAD_DIM
        low = low_s[...]
        upp = upp_s[...]
        dob = do_ref[...].astype(BF16)
        q2 = q_ref[...]
        qhs = [jnp.where(m0, q2, jnp.zeros_like(q2)), jnp.where(m0, jnp.zeros_like(q2), q2)]
        dhs = [jnp.where(m0, dob, jnp.zeros_like(dob)), jnp.where(m0, jnp.zeros_like(dob), dob)]
        causal = _strictly_causal(tq)

        def down(kb, c_ls, valid):
            ks = pl.multiple_of(kb * tq, tq)
            kt = k_ref[pl.ds(ks, tq), :]
            vt = v_ref[pl.ds(ks, tq), :]
            lss, ls_, ws = _sb_tile(qhs, kt, low, c_ls, valid)
            dws = [_dot_nt(dh, vt) for dh in dhs]
            for hh in range(2):
                g_s[hh, kb] = dws[hh] * ws[hh]
                b_s[hh, kb] = jnp.exp(lss[hh])
            dv_ref[pl.ds(ks, tq), :] += (_dot_tn(ws[0].astype(BF16), dhs[0])
                                         + _dot_tn(ws[1].astype(BF16), dhs[1]))
            return [c_l + jnp.sum(l, axis=1, keepdims=True) for c_l, l in zip(c_ls, ls_)]

        zc, za = jnp.zeros((tq, 1), F32), jnp.zeros((tq, PAIR), F32)
        def first_two():
            c_ls = down(i - 1, down(i, [zc, zc], causal), None)
            return (_weights_left(c_ls), *c_ls)

        st0 = lax.cond(i >= 1, first_two, lambda: (jnp.int32(0), *down(i, [zc, zc], causal)))

        def more(st):
            return jnp.logical_and(st[0] <= i, st[1] > 0)

        def down_step(st):
            c_ls = down(i - st[0], [st[2], st[3]], None)
            return (st[0] + 1, _weights_left(c_ls), c_ls[0], c_ls[1])

        n_tiles = lax.while_loop(more, down_step, (jnp.int32(2), *st0))[0]

        def up(kb, c_gs, accs, valid):
            ks = pl.multiple_of(kb * tq, tq)
            kt = k_ref[pl.ds(ks, tq), :]
            gs = [g_s[hh, kb] for hh in range(2)]
            pres = [_dot(g.astype(BF16), upp) + c_g for g, c_g in zip(gs, c_gs)]
            dzs = []
            for hh in range(2):
                beta = b_s[hh, kb]
                dz = gs[hh] - beta * (gs[hh] + pres[hh])
                if valid is not None:
                    dz = jnp.where(valid, dz, 0.0)
                dzs.append(dz.astype(BF16))
            new_a = [acc + _dot(dzb, kt) for acc, dzb in zip(accs, dzs)]
            dk_ref[pl.ds(ks, tq), :] += _dot_tn(dzs[0], qhs[0]) + _dot_tn(dzs[1], qhs[1])
            new_c = [c_g + jnp.sum(g, axis=1, keepdims=True) for c_g, g in zip(c_gs, gs)]
            return new_c, new_a

        def up_step(kb, st):
            c_gs, accs = up(kb, [st[0], st[1]], [st[2], st[3]], None)
            return (c_gs[0], c_gs[1], accs[0], accs[1])

        st = lax.fori_loop(i - n_tiles + 1, i - 1, up_step, (zc, zc, za, za))

        def last_two():
            c_gs, accs = up(i - 1, [st[0], st[1]], [st[2], st[3]], None)
            return tuple(up(i, c_gs, accs, causal)[1])

        accs = lax.cond(i >= 1, last_two, lambda: tuple(up(i, [zc, zc], [za, za], causal)[1]))
        dq_ref[...] = jnp.where(m0, accs[0], accs[1]) * Q_SCALE

        @pl.when(jnp.logical_and(p == N_PAIRS - 1, i == nq - 1))
        def _():
            rider.last(ride)

    return rider.call(
        body, [q, k, v, do], grid=(N_PAIRS, nq), name="attn_bwd",
        in_specs=[pl.BlockSpec((tq, PAIR), lambda p, i: (i, p)),
                  pl.BlockSpec((S, PAIR), lambda p, i: (0, p)),
                  pl.BlockSpec((S, PAIR), lambda p, i: (0, p)),
                  pl.BlockSpec((tq, PAIR), lambda p, i: (i, p))],
        out_specs=[pl.BlockSpec((tq, PAIR), lambda p, i: (i, p)),
                   pl.BlockSpec((S, PAIR), lambda p, i: (0, p)),
                   pl.BlockSpec((S, PAIR), lambda p, i: (0, p))],
        out_shape=[_sds((S, ATTN_WIDTH), F32), _sds((S, ATTN_WIDTH), F32), _sds((S, ATTN_WIDTH), F32)],
        scratch_shapes=[pltpu.VMEM((2, nq, tq, tq), F32), pltpu.VMEM((2, nq, tq, tq), F32),
                        pltpu.VMEM((tq, tq), BF16), pltpu.VMEM((tq, tq), BF16)],
        vmem_mib=56)


def _inproj_bwd(dq, dk, dv, du, dpre1, xh0, rstd0, g0, b0, w_in_s, tm):
    S, D = xh0.shape

    def body(dq_ref, dk_ref, dv_ref, du_ref, dp1_ref, xh_ref, rstd_ref, g_ref, b_ref, w_ref,
             gx_ref, dw_ref, dg_ref, db_ref):
        @pl.when(pl.program_id(0) == 0)
        def _():
            for r in (dw_ref, dg_ref, db_ref):
                r[...] = jnp.zeros_like(r)

        xh = xh_ref[...]
        xb = (xh * g_ref[...] + b_ref[...]).astype(BF16)
        dx0 = ALPHA * dp1_ref[...]
        for c, r in enumerate((dq_ref, dk_ref, dv_ref, du_ref)):
            dpb = r[...].astype(BF16)
            dx0 = dx0 + _dot_nt(dpb, w_ref[c])
            dw_ref[c] += _dot_tn(xb, dpb)
        dg_ref[...] += _colsum(dx0 * xh)
        db_ref[...] += _colsum(dx0)
        gx_ref[...] = _ln_bwd(dx0, xh, rstd_ref[...], g_ref[...])

    vec = _const_spec((1, D))
    half = _row_spec(tm, 512)
    return pl.pallas_call(
        body, grid=(S // tm,), name="inproj_bwd",
        in_specs=[half, half, half, half, _row_spec(tm, D), _row_spec(tm, D), _row_spec(tm, 1), vec, vec,
                  _const_spec((N_CHIPS, D, 512))],
        out_specs=[_row_spec(tm, D), _const_spec((N_CHIPS, D, 512)), vec, vec],
        out_shape=[_sds((S, D), F32), _sds((N_CHIPS, D, 512), F32), _sds((1, D), F32), _sds((1, D), F32)],
        compiler_params=_cp(58),
    )(*_hbm(dq, dk, dv, du, dpre1, xh0, rstd0, g0, b0, w_in_s))


def _place():
    return lax.axis_index("x"), lax.axis_index("y"), lax.axis_index("c")


CHIP_FLIPS = ((0, 1), (1, 0), (1, 1))


class _Rider:
    def __init__(self, ins, out_shapes, n_sem, phases, aliases=None):
        self.ins, self.out_shapes, self.n_sem, self.phases = list(ins), list(out_shapes), n_sem, phases
        self.aliases = aliases or {}

    def __add__(self, other):
        na, ma = len(self.ins), len(self.out_shapes)

        def phases(ins, outs, ssem, rsem):
            mine = self.phases(ins[:na], outs[:ma], ssem, rsem)
            rest = pl.ds(self.n_sem, other.n_sem)
            theirs = other.phases(ins[na:], outs[ma:], ssem.at[rest], rsem.at[rest])
            assert len(mine) == 1 and len(theirs) == 1
            return [mine[0] + theirs[0]]

        aliases = {**self.aliases, **{na + i: ma + o for i, o in other.aliases.items()}}
        return _Rider(self.ins + other.ins, self.out_shapes + other.out_shapes, self.n_sem + other.n_sem, phases,
                      aliases)

    def split(self, refs, n_in, n_out, n_scratch):
        a = n_in + len(self.ins)
        b = a + n_out
        c = b + len(self.out_shapes)
        own = (refs[:n_in], refs[a:b], refs[c:c + n_scratch])
        return own + ((refs[n_in:a], refs[b:c]) + tuple(refs[c + n_scratch:]),)

    def first(self, ride):
        for make in self.phases(*ride)[0]:
            make().start()

    def mid(self, ride):
        ph = self.phases(*ride)
        if len(ph) == 2:
            for make in ph[0]:
                make().wait_recv()
            for make in ph[1]:
                make().start()

    def last(self, ride):
        ph = self.phases(*ride)
        if len(ph) == 2:
            for make in ph[0]:
                make().wait_send()
        for make in ph[-1]:
            make().wait()

    def call(self, body, args, *, grid, name, in_specs, out_specs, out_shape, scratch_shapes, vmem_mib,
             prefetch=None):
        n_in, n_out = len(in_specs), len(out_specs)
        sems = [pltpu.SemaphoreType.DMA((self.n_sem,)), pltpu.SemaphoreType.DMA((self.n_sem,))]
        n_pre = 0 if prefetch is None else 1
        grid_spec = pltpu.PrefetchScalarGridSpec(
            num_scalar_prefetch=n_pre, grid=grid,
            in_specs=list(in_specs) + [HBM_SPEC] * len(self.ins),
            out_specs=list(out_specs) + [HBM_SPEC] * len(self.out_shapes),
            scratch_shapes=list(scratch_shapes) + sems)
        return pl.pallas_call(
            body, name=name, grid_spec=grid_spec,
            out_shape=list(out_shape) + self.out_shapes,
            input_output_aliases={n_pre + n_in + i: n_out + o for i, o in self.aliases.items()},
            compiler_params=_cp(vmem_mib),
        )(*([] if prefetch is None else [prefetch]), *_hbm(*args), *self.ins)

    def run(self, name):
        def body(*refs):
            ride = self.split(refs, 0, 0, 0)[3]
            self.first(ride)
            self.mid(ride)
            self.last(ride)

        return self.call(body, [], grid=(), name=name, in_specs=[], out_specs=[], out_shape=[],
                         scratch_shapes=[], vmem_mib=16)


def _remote(src, dst, ssem, rsem, n, dev):
    return functools.partial(pltpu.make_async_remote_copy, src_ref=src, dst_ref=dst, send_sem=ssem.at[n],
                             recv_sem=rsem.at[n], device_id=dev, device_id_type=MESH)


def _cast_into_slot(w, place, name):
    R, C = w.shape
    tr = min(R, 512)

    def body(pl_ref, w_ref, o_ref):
        o_ref[0] = w_ref[...].astype(BF16)

    return pl.pallas_call(
        body, name=name,
        grid_spec=pltpu.PrefetchScalarGridSpec(
            num_scalar_prefetch=1, grid=(R // tr,),
            in_specs=[pl.BlockSpec((tr, C), lambda r, pr: (r, 0))],
            out_specs=pl.BlockSpec((1, tr, C), lambda r, pr: (pr[1], r, 0))),
        out_shape=_sds((N_CHIPS, R, C), BF16),
    )(place, w)


CAST_STEPS = 8


def _cast_rest(ws, place, rider):
    n = len(ws)

    def body(pl_ref, *refs):
        w_refs, o_refs, _, ride = rider.split(refs, n, n, 0)
        r = pl.program_id(0)

        @pl.when(r == 0)
        def _():
            rider.first(ride)

        @pl.when(r == CAST_STEPS // 2)
        def _():
            rider.mid(ride)

        for w_ref, o_ref in zip(w_refs, o_refs):
            o_ref[0] = w_ref[...].astype(BF16)

        @pl.when(r == CAST_STEPS - 1)
        def _():
            rider.last(ride)

    def rows(w):
        return w.shape[0] // CAST_STEPS

    return rider.call(
        body, ws, grid=(CAST_STEPS,), name="cast_weights", prefetch=place,
        in_specs=[pl.BlockSpec((rows(w), w.shape[1]), lambda r, pr: (r, 0)) for w in ws],
        out_specs=[pl.BlockSpec((1, rows(w), w.shape[1]), lambda r, pr: (pr[1], r, 0)) for w in ws],
        out_shape=[_sds((N_CHIPS,) + w.shape, BF16) for w in ws], scratch_shapes=[], vmem_mib=32)


def _gather_rider(stacked, part="both"):
    n, nf = len(stacked), len(CHIP_FLIPS)

    def phases(ins, outs, ssem, rsem):
        x, y, c = _place()
        slot = 2 * x + y
        ici, d2d = [], []
        for w, (i_ref, o_ref) in enumerate(zip(ins, outs)):
            hh = o_ref.shape[1] // 2
            rows = pl.ds(c * hh, hh)
            for f, (fx, fy) in enumerate(CHIP_FLIPS):
                k = w * nf + f
                theirs = 2 * (x ^ fx) + (y ^ fy)
                if part != "pair":
                    ici.append(_remote(i_ref.at[slot, rows], o_ref.at[slot, rows], ssem, rsem, k,
                                       (x ^ fx, y ^ fy, c)))
                if part != "chips":
                    d2d.append(_remote(o_ref.at[theirs, rows], o_ref.at[theirs, rows], ssem, rsem,
                                       (n * nf if part == "both" else 0) + k, (x, y, 1 - c)))
        return [ph for ph in (ici, d2d) if ph]

    return _Rider(stacked, [_sds(s.shape, s.dtype) for s in stacked], (2 if part == "both" else 1) * n * nf,
                  phases, aliases={i: i for i in range(n)})


def _pair_swap_rider(grads):
    def phases(ins, outs, ssem, rsem):
        x, y, c = _place()
        return [[_remote(g.at[:, 1 - c], o, ssem, rsem, k, (x, y, 1 - c))
                 for k, (g, o) in enumerate(zip(ins, outs))]]

    return _Rider(grads, [_sds((N_CHIPS,) + g.shape[2:], g.dtype) for g in grads], len(grads), phases)


def _chip_scatter_rider(parts):
    nf = len(CHIP_FLIPS)

    def phases(ins, outs, ssem, rsem):
        x, y, c = _place()
        return [[_remote(r.at[2 * (x ^ fx) + (y ^ fy)], o.at[f], ssem, rsem, w * nf + f, (x ^ fx, y ^ fy, c))
                 for w, (r, o) in enumerate(zip(ins, outs)) for f, (fx, fy) in enumerate(CHIP_FLIPS)]]

    return _Rider(parts, [_sds((nf,) + r.shape[1:], r.dtype) for r in parts], len(parts) * nf, phases)


def _pair_send_rider(halves):
    def phases(ins, outs, ssem, rsem):
        x, y, c = _place()
        return [[_remote(h, o, ssem, rsem, k, (x, y, 1 - c)) for k, (h, o) in enumerate(zip(ins, outs))]]

    return _Rider(halves, [_sds(h.shape, h.dtype) for h in halves], len(halves), phases)


PAIR_SUM_STEPS = 2
CHIP_SUM_STEPS = 4
ADAMW_STEPS = 4


def _no_rider():
    return _Rider([], [], 1, lambda ins, outs, ssem, rsem: [[]])


def _add_pair(grads, recvs, place, name, rider):
    n = len(grads)

    def body(pl_ref, *refs):
        ins, outs, _, ride = rider.split(refs, 2 * n, 2 * n, 0)
        h, j = pl.program_id(0), pl.program_id(1)

        @pl.when(jnp.logical_and(h == 0, j == 0))
        def _():
            rider.first(ride)

        sums = [ins[2 * w][0, 0] + ins[2 * w + 1][0] for w in range(n)]
        for w in range(n):
            outs[2 * w + 1][0] = sums[w].astype(BF16)

        @pl.when(j == pl_ref[1])
        def _():
            for w in range(n):
                outs[2 * w][...] = sums[w]

        @pl.when(jnp.logical_and(h == PAIR_SUM_STEPS - 1, j == N_CHIPS - 1))
        def _():
            rider.last(ride)

    in_specs, out_specs, out_shape, args = [], [], [], []
    for g, r in zip(grads, recvs):
        _, _, H, C = g.shape
        th = H // PAIR_SUM_STEPS
        spec = pl.BlockSpec((1, th, C), lambda h, j, pr: (j, h, 0))
        in_specs += [pl.BlockSpec((1, 1, th, C), lambda h, j, pr: (j, pr[0], h, 0)), spec]
        out_specs += [pl.BlockSpec((th, C), lambda h, j, pr: (h, 0)), spec]
        out_shape += [_sds((H, C), F32), _sds((N_CHIPS, H, C), BF16)]
        args += [g, r]
    res = rider.call(body, args, grid=(PAIR_SUM_STEPS, N_CHIPS), name=name, prefetch=place, in_specs=in_specs,
                     out_specs=out_specs, out_shape=out_shape, scratch_shapes=[], vmem_mib=32)
    return [(res[2 * w], res[2 * w + 1]) for w in range(n)], res[2 * n:]


def _add_chips(parts, recvs, place, name, rider):
    n = len(parts)

    def body(pl_ref, *refs):
        ins, outs, _, ride = rider.split(refs, 2 * n, n, 0)
        h = pl.program_id(0)

        @pl.when(h == 0)
        def _():
            rider.first(ride)

        for w in range(n):
            p_ref, r_ref = ins[2 * w], ins[2 * w + 1]
            outs[w][...] = p_ref[...] + r_ref[0].astype(F32) + r_ref[1].astype(F32) + r_ref[2].astype(F32)

        @pl.when(h == CHIP_SUM_STEPS - 1)
        def _():
            rider.last(ride)

    in_specs, out_specs, out_shape, args = [], [], [], []
    for p, r in zip(parts, recvs):
        H, C = p.shape
        th = H // CHIP_SUM_STEPS
        in_specs += [pl.BlockSpec((th, C), lambda h, pr: (h, 0)),
                     pl.BlockSpec((len(CHIP_FLIPS), th, C), lambda h, pr: (0, h, 0))]
        out_specs.append(pl.BlockSpec((th, C), lambda h, pr: (h, 0)))
        out_shape.append(_sds((H, C), F32))
        args += [p, r]
    res = rider.call(body, args, grid=(CHIP_SUM_STEPS,), name=name, prefetch=place, in_specs=in_specs,
                     out_specs=out_specs, out_shape=out_shape, scratch_shapes=[], vmem_mib=32)
    return res[:n], res[n:]


def _adamw_math(w, g, m, v):
    m = ADAM_B1 * m + (1.0 - ADAM_B1) * g
    v = ADAM_B2 * v + (1.0 - ADAM_B2) * (g * g)
    m_hat = m / (1.0 - ADAM_B1 ** ADAM_STEP)
    v_hat = v / (1.0 - ADAM_B2 ** ADAM_STEP)
    delta = -ADAM_LR * (m_hat / (jnp.sqrt(v_hat) + ADAM_EPS) + ADAM_WD * w)
    return delta, m, v


def _adamw(ws, mines, theirs, ms, vs, place, name, rider):
    n = len(ws)

    def body(pl_ref, *refs):
        ins, outs, _, ride = rider.split(refs, 5 * n, 4 * n, 0)
        h, r = pl.program_id(0), pl.program_id(1)

        @pl.when(jnp.logical_and(h == 0, r == 0))
        def _():
            rider.first(ride)

        for k in range(n):
            w_ref, a_ref, b_ref, m_ref, v_ref = ins[5 * k:5 * k + 5]
            g = jnp.where(h == pl_ref[0], a_ref[...], b_ref[...])
            d, mo, vo = _adamw_math(w_ref[...], g, m_ref[...], v_ref[...])
            for o_ref, val in zip(outs[4 * k:4 * k + 4], (g, d, mo, vo)):
                o_ref[...] = val

        @pl.when(jnp.logical_and(h == 1, r == ADAMW_STEPS - 1))
        def _():
            rider.last(ride)

    in_specs, out_specs, out_shape, args = [], [], [], []
    for w, a, b, m, v in zip(ws, mines, theirs, ms, vs):
        R, C = w.shape
        th = (R // 2) // ADAMW_STEPS
        whole = pl.BlockSpec((th, C), lambda h, r, pr: (h * ADAMW_STEPS + r, 0))
        mine_spec = pl.BlockSpec((th, C), lambda h, r, pr: (jnp.where(h == pr[0], r, 0), 0))
        theirs_spec = pl.BlockSpec((th, C), lambda h, r, pr: (jnp.where(h == pr[0], 0, r), 0))
        in_specs += [whole, mine_spec, theirs_spec, whole, whole]
        out_specs += [whole] * 4
        out_shape += [_sds((R, C), F32)] * 4
        args += [w, a, b, m, v]
    res = rider.call(body, args, grid=(2, ADAMW_STEPS), name=name, prefetch=place, in_specs=in_specs,
                     out_specs=out_specs, out_shape=out_shape, scratch_shapes=[], vmem_mib=40)
    return [tuple(res[4 * k:4 * k + 4]) for k in range(n)], res[4 * n:]


DEVICE_FLIPS = tuple((fx, fy, fc) for fx in (0, 1) for fy in (0, 1) for fc in (0, 1))[1:]


def _pack_exchange_rider(pack):
    def phases(ins, outs, ssem, rsem):
        x, y, c = _place()
        mine = outs[0].at[4 * x + 2 * y + c]
        copies = [_remote(ins[0], mine, ssem, rsem, k, (x ^ fx, y ^ fy, c ^ fc))
                  for k, (fx, fy, fc) in enumerate(DEVICE_FLIPS)]
        copies.append(functools.partial(pltpu.make_async_copy, ins[0], mine, ssem.at[len(DEVICE_FLIPS)]))
        return [copies]

    return _Rider([pack], [_sds((N_DEV,) + pack.shape, pack.dtype)], len(DEVICE_FLIPS) + 1, phases)


def _small_sum_adamw(recv_a, recv_b, wpack, mpack, vpack):
    R = wpack.shape[0]

    def body(a_ref, b_ref, w_ref, m_ref, v_ref, gs_ref, d_ref, mo_ref, vo_ref):
        ta, tb = a_ref[0], b_ref[0]
        for dev in range(1, N_DEV):
            ta = ta + a_ref[dev]
            tb = tb + b_ref[dev]
        total = jnp.concatenate([ta, tb], axis=0)
        gs_ref[...] = total
        d, mo, vo = _adamw_math(w_ref[...], total, m_ref[...], v_ref[...])
        d_ref[...] = d
        mo_ref[...] = mo
        vo_ref[...] = vo

    return pl.pallas_call(
        body, name="small_sum_adamw", in_specs=[VMEM_SPEC] * 5, out_specs=[VMEM_SPEC] * 4,
        out_shape=[_sds((R, LANES), F32)] * 4,
    )(recv_a, recv_b, wpack, mpack, vpack)


def _rows8(a):
    a = a.reshape(-1, LANES)
    pad = (-a.shape[0]) % 8
    return jnp.pad(a, ((0, pad), (0, 0))) if pad else a


def _pack(parts):
    return jnp.concatenate([_rows8(a) for a in parts], axis=0)


def _unpack(pack, like):
    out, row = [], 0
    for a in like:
        n = a.size // LANES
        out.append(pack[row:row + n].reshape(a.shape))
        row += n + (-n) % 8
    return out


def kernel(x, p, emb_ln_g, emb_ln_b, w_in, attn_out_g, w_pool, pool_scale, w_out, ln1_g, ln1_b, w_up, w_down, ln2_g, ln2_b, w_ple, w_ple_gate, ln3_g, ln3_b, loss_target, m_emb_ln_g, m_emb_ln_b, m_w_in, m_attn_out_g, m_w_pool, m_pool_scale, m_w_out, m_ln1_g, m_ln1_b, m_w_up, m_w_down, m_ln2_g, m_ln2_b, m_w_ple, m_w_ple_gate, m_ln3_g, m_ln3_b, v_emb_ln_g, v_emb_ln_b, v_w_in, v_attn_out_g, v_w_pool, v_pool_scale, v_w_out, v_ln1_g, v_ln1_b, v_w_up, v_w_down, v_ln2_g, v_ln2_b, v_w_ple, v_w_ple_gate, v_ln3_g, v_ln3_b):
    S = x.shape[1]
    tq = min(256, S)
    tm_mlp = min(1024, S)
    tm_pool = min(1024, S)
    tm_ln = min(512, S)
    tm_fwd = min(1024, S)
    xs = x[0]
    ps = p[0, 0]
    tgt = loss_target[0]
    row = lambda a: a.reshape(1, -1)
    g0, b0 = row(emb_ln_g), row(emb_ln_b)
    g1, b1, g2, b2, g3, b3 = ln1_g, ln1_b, ln2_g, ln2_b, ln3_g, ln3_b
    wp = w_pool[0]

    xi, yi, ci = _place()
    place = jnp.stack([ci, 2 * xi + yi]).astype(jnp.int32)
    names = ["w_in", "w_out", "w_up", "w_down", "w_ple", "w_ple_gate"]

    big = [w_in[0], w_out[0], w_up[0], w_down[0], w_ple[0], w_ple_gate[0]]
    s_in = _cast_into_slot(big[0], place, "cast_w_in")
    s_out, s_up, s_down, s_ple, s_gate, w_in_s = _cast_rest(big[1:], place, _gather_rider([s_in]))

    xh0, rstd0, q, k, v, u, s_out, s_ple, s_gate = _embln_inproj(
        xs, g0, b0, w_in_s, tm_fwd, _gather_rider([s_out, s_ple, s_gate], "chips"))
    o_raw, on, s_up, s_down, w_out_s, w_ple_s, w_gate_s = _attn_fwd(
        q, k, v, attn_out_g, tq, _gather_rider([s_up, s_down], "chips") + _gather_rider([s_out, s_ple, s_gate], "pair"))
    w_out_f = w_out_s.reshape(D_MODEL, D_MODEL)
    w_gate_f = w_gate_s.reshape(D_MODEL, D_MODEL)
    d_b, pooled = _pool_fwd(u, wp, pool_scale, tm_pool)
    xh1, rstd1, x1b, w_up_s, w_down_s = _mix_ln1(on, pooled, xh0, g0, b0, w_out_f, g1, b1, tm_fwd,
                                                 _gather_rider([s_up, s_down], "pair"))
    w_down_f = w_down_s.reshape(D_FF, D_MODEL)
    xh2, rstd2, rb = _mlp_ln2(xh1, x1b, g1, b1, w_up_s, w_down_f, tm_mlp, _no_rider())

    (dpre2, dhb, dw_ple, dw_gate, dg3, db3, dg2, db2, loss_row) = _ple_ln3_loss(
        xh2, rstd2, g2, b2, ps, w_ple_s, w_gate_f, g3, b3, tgt, tm_ln)
    def halves_of(g):
        return g.reshape(N_CHIPS, 2, g.shape[1] // 2, g.shape[2])

    ple_halves = [halves_of(dw_ple), halves_of(dw_gate.reshape(N_CHIPS, D_MODEL // N_CHIPS, D_MODEL))]
    dx1m, da, *ple_pair = _mlp_bwd(rb, dhb, w_up_s, w_down_f, tm_mlp, _pair_swap_rider(ple_halves))
    (dw_up,) = _tn_matmul(x1b, da, "grad_w_up", 1024, S, True, _no_rider())
    up_halves = halves_of(dw_up)
    dw_down, up_pair = _tn_matmul(rb, dhb, "grad_w_down", 1024, S, False,
                                  _pair_swap_rider([up_halves]), square_a=True)
    down_halves = halves_of(dw_down.reshape(N_CHIPS, D_FF // N_CHIPS, D_MODEL))
    dpre1, do, dpooled, dw_out, dg1, db1, dga, down_pair = _mix_bwd(
        dpre2, dx1m, xh1, rstd1, g1, w_out_f, on, pooled, o_raw, attn_out_g, tm_ln,
        _pair_swap_rider([down_halves]))
    out_halves = halves_of(dw_out.reshape(N_CHIPS, D_MODEL // N_CHIPS, D_MODEL))
    du, dwp, dsc, out_pair = _pool_bwd(dpooled, d_b, wp, pool_scale, tm_pool, _pair_swap_rider([out_halves]))
    early_sum, _ = _add_pair(
        [out_halves, up_halves, down_halves] + ple_halves, [out_pair, up_pair, down_pair] + ple_pair, place,
        "pair_sum_early", _no_rider())
    pack_a = _pack([jnp.broadcast_to(loss_row, (8, LANES)), dwp, dsc, dg1, db1, dg2, db2, dg3, db3])
    riding = _chip_scatter_rider([b for _, b in early_sum]) + _pack_exchange_rider(pack_a)
    dq, dk, dv, *arrived = _attn_bwd(q, k, v, do, tq, riding)
    early_chips, recv_a = arrived[:-1], arrived[-1]
    grad_x, dw_in, dg0, db0 = _inproj_bwd(dq, dk, dv, du, dpre1, xh0, rstd0, g0, b0, w_in_s, tm_ln)

    in_halves = halves_of(dw_in)
    pack_b = _pack([dg0, db0, dga])
    early_mine, (in_pair, recv_b) = _add_chips(
        [s for s, _ in early_sum], early_chips, place, "chip_sum_early",
        _pair_swap_rider([in_halves]) + _pack_exchange_rider(pack_b))
    (in_sum,), early_theirs = _add_pair([in_halves], [in_pair], place, "pair_sum_w_in", _pair_send_rider(early_mine))
    ms = [m_w_in, m_w_out, m_w_up, m_w_down, m_w_ple, m_w_ple_gate]
    vs = [v_w_in, v_w_out, v_w_up, v_w_down, v_w_ple, v_w_ple_gate]
    early_res, _ = _adamw(big[1:], early_mine, early_theirs, [m[0] for m in ms[1:]], [v[0] for v in vs[1:]],
                          place, "adamw_early", _no_rider())
    (in_chips,) = _chip_scatter_rider([in_sum[1]]).run("reduce_chips_late")
    (in_mine,), _ = _add_chips([in_sum[0]], [in_chips], place, "chip_sum_w_in", _no_rider())
    (in_theirs,) = _pair_send_rider([in_mine]).run("gather_pair_w_in")
    in_res, _ = _adamw(big[:1], [in_mine], [in_theirs], [ms[0][0]], [vs[0][0]], place, "adamw_w_in", _no_rider())
    big_out = {n: tuple(r.reshape(m.shape) for r in res4) for n, res4, m in zip(names, in_res + early_res, ms)}

    small_names = ["w_pool", "pool_scale", "ln1_g", "ln1_b", "ln2_g", "ln2_b", "ln3_g", "ln3_b",
                   "emb_ln_g", "emb_ln_b", "attn_out_g"]
    small_w = [w_pool, pool_scale, ln1_g, ln1_b, ln2_g, ln2_b, ln3_g, ln3_b, emb_ln_g, emb_ln_b, attn_out_g]
    small_m = [m_w_pool, m_pool_scale, m_ln1_g, m_ln1_b, m_ln2_g, m_ln2_b, m_ln3_g, m_ln3_b,
               m_emb_ln_g, m_emb_ln_b, m_attn_out_g]
    small_v = [v_w_pool, v_pool_scale, v_ln1_g, v_ln1_b, v_ln2_g, v_ln2_b, v_ln3_g, v_ln3_b,
               v_emb_ln_g, v_emb_ln_b, v_attn_out_g]
    loss_like = jnp.zeros((8, LANES), F32)
    gs, ds, mos, vos = _small_sum_adamw(recv_a, recv_b, _pack([loss_like] + small_w), _pack([loss_like] + small_m),
                                        _pack([jnp.ones((8, LANES), F32)] + small_v))
    like = [loss_like] + small_w
    gs_u, ds_u, mos_u, vos_u = (_unpack(a, like) for a in (gs, ds, mos, vos))
    loss = gs_u[0][0, 0]
    small_out = {n: (gs_u[i + 1], ds_u[i + 1], mos_u[i + 1], vos_u[i + 1]) for i, n in enumerate(small_names)}

    order = ["emb_ln_g", "emb_ln_b", "w_in", "attn_out_g", "w_pool", "pool_scale", "w_out", "ln1_g", "ln1_b",
             "w_up", "w_down", "ln2_g", "ln2_b", "w_ple", "w_ple_gate", "ln3_g", "ln3_b"]
    res = {**big_out, **small_out}
    outs = [loss, grad_x.reshape(x.shape)]
    for kind in range(4):
        outs += [res[n][kind] for n in order]
    return tuple(outs)
```

```python
import functools

import jax
import jax.numpy as jnp
from jax import lax
from jax.experimental import pallas as pl
from jax.experimental.pallas import tpu as pltpu

F32 = jnp.float32
BF16 = jnp.bfloat16

D_MODEL = 1024
ATTN_WIDTH = 512
POOL_WIDTH = 512
HEAD_DIM = 64
PAIR = 2 * HEAD_DIM
N_PAIRS = ATTN_WIDTH // PAIR
N_POOL_GROUPS = 4
POOL_GROUP = 128
POOL_HALO = 16
D_FF = 4096
PLE_DIM = 256
N_CHIPS = 4
N_DEV = 8
LN_EPS = 1e-5
RMS_EPS = 1e-6
ALPHA = float(2.0 ** 0.25)
Q_SCALE = 0.125
ADAM_LR = 0.001
ADAM_B1 = 0.9
ADAM_B2 = 0.999
ADAM_EPS = 1e-08
ADAM_WD = 0.01
ADAM_STEP = 10
LANES = 128
MIB = 1024 * 1024

MESH = pl.DeviceIdType.MESH
HBM_SPEC = pl.BlockSpec(memory_space=pltpu.HBM)
VMEM_SPEC = pl.BlockSpec(memory_space=pltpu.VMEM)


def _cp(vmem_mib):
    return pltpu.CompilerParams(vmem_limit_bytes=vmem_mib * MIB)


def _dot(a, b):
    return jnp.dot(a, b, preferred_element_type=F32)


def _dot_nt(a, b):
    return lax.dot_general(a, b, (((1,), (1,)), ((), ())), preferred_element_type=F32)


def _dot_tn(a, b):
    return lax.dot_general(a, b, (((0,), (0,)), ((), ())), preferred_element_type=F32)


def _ln_fwd(pre):
    mu = jnp.mean(pre, axis=-1, keepdims=True)
    xc = pre - mu
    var = jnp.mean(xc * xc, axis=-1, keepdims=True)
    rstd = lax.rsqrt(var + LN_EPS)
    return xc * rstd, rstd


def _ln_bwd(dy, xh, rstd, g):
    dxh = dy * g
    m1 = jnp.mean(dxh, axis=-1, keepdims=True)
    m2 = jnp.mean(dxh * xh, axis=-1, keepdims=True)
    return rstd * (dxh - m1 - xh * m2)


def _colsum(a):
    return jnp.sum(a, axis=0, keepdims=True)


def _neg_softplus(z):
    return -(jnp.maximum(z, 0.0) + jnp.log(1.0 + jnp.exp(-jnp.abs(z))))


def _row_spec(tm, n):
    return pl.BlockSpec((tm, n), lambda i: (i, 0))


def _const_spec(shape):
    nd = len(shape)
    return pl.BlockSpec(shape, lambda *_: (0,) * nd)


def _hbm(*arrays):
    return [pltpu.with_memory_space_constraint(a, pltpu.HBM) for a in arrays]


def _sds(shape, dtype):
    return pltpu.HBM(shape, dtype)


def _embln_inproj(x, g0, b0, w_in_s, w_pool, pscale, tm, rider):
    S, D = x.shape
    n_t = S // tm

    def body(*refs):
        ((x_ref, g_ref, b_ref, w_ref, wp_ref, sc_ref),
         (xh_ref, rstd_ref, q_ref, k_ref, v_ref, d_ref, pooled_ref), (halo_s,), ride) = rider.split(refs, 6, 7, 1)
        i = pl.program_id(0)

        @pl.when(i == 0)
        def _():
            rider.first(ride)
            halo_s[...] = jnp.zeros_like(halo_s)

        @pl.when(i == (3 * n_t) // 4)
        def _():
            rider.mid(ride)

        xh, rstd = _ln_fwd(x_ref[...])
        xh_ref[...] = xh
        rstd_ref[...] = rstd
        xb = (xh * g_ref[...] + b_ref[...]).astype(BF16)
        q_ref[...] = (_dot(xb, w_ref[0]) * Q_SCALE).astype(BF16)
        k_ref[...] = _dot(xb, w_ref[1]).astype(BF16)
        v_ref[...] = _dot(xb, w_ref[2]).astype(BF16)
        u = _dot(xb, w_ref[3])
        halo = halo_s[...]
        pos = i * tm + lax.broadcasted_iota(jnp.int32, (tm, 1), 0)
        for g in range(N_POOL_GROUPS):
            win = 2 ** (g + 1)
            cols = slice(g * POOL_GROUP, (g + 1) * POOL_GROUP)
            ut = u[:, cols]
            s = jnp.concatenate([halo[:, cols], ut], axis=0)
            for sh in (1, 2, 4, 8)[:g + 1]:
                s = s + pltpu.roll(s, sh, 0)
            cnt = jnp.minimum(pos + 1, win).astype(F32)
            db = (s[POOL_HALO:, :] / cnt - ut).astype(BF16)
            d_ref[:, cols] = db
            pooled_ref[:, cols] = (_dot(db, wp_ref[g].astype(BF16)) * sc_ref[:, cols]).astype(BF16)
        halo_s[...] = u[tm - POOL_HALO:, :]

        @pl.when(i == n_t - 1)
        def _():
            rider.last(ride)

    half = _row_spec(tm, 512)
    return rider.call(
        body, [x, g0, b0, w_in_s, w_pool, pscale], grid=(n_t,), name="embln_inproj",
        in_specs=[_row_spec(tm, D), _const_spec((1, D)), _const_spec((1, D)), _const_spec((N_CHIPS, D, 512)),
                  _const_spec((N_POOL_GROUPS, POOL_GROUP, POOL_GROUP)), _const_spec((1, POOL_WIDTH))],
        out_specs=[_row_spec(tm, D), _row_spec(tm, 1), half, half, half, half, half],
        out_shape=[_sds((S, D), F32), _sds((S, 1), F32)] + [_sds((S, 512), BF16)] * 5,
        scratch_shapes=[pltpu.VMEM((POOL_HALO, POOL_WIDTH), F32)], vmem_mib=56)


def _tri(n, upper):
    r = lax.broadcasted_iota(jnp.int32, (n, n), 0)
    c = lax.broadcasted_iota(jnp.int32, (n, n), 1)
    keep = (r < c) if upper else (r > c)
    return jnp.where(keep, 1.0, 0.0).astype(BF16)


def _strictly_causal(n):
    return lax.broadcasted_iota(jnp.int32, (n, n), 1) < lax.broadcasted_iota(jnp.int32, (n, n), 0)


LOG_WEIGHT_FLOOR = -110.0


def _weights_left(c_ls):
    return (jnp.max(jnp.maximum(c_ls[0], c_ls[1])) > LOG_WEIGHT_FLOOR).astype(jnp.int32)


def _sb_tile(qhs, kt, low, c_ls, valid):
    valids = valid if isinstance(valid, (list, tuple)) else [valid] * len(qhs)
    zs = [_dot_nt(qh, kt) for qh in qhs]
    lrs = [_neg_softplus(z) for z in zs]
    ls_ = [lr if m is None else jnp.where(m, lr, 0.0) for lr, m in zip(lrs, valids)]
    sfx = [_dot(l.astype(BF16), low) + c_l for l, c_l in zip(ls_, c_ls)]
    lss = [z + lr for z, lr in zip(zs, lrs)]
    ws = [jnp.exp(ls + s) for ls, s in zip(lss, sfx)]
    ws = [w if m is None else jnp.where(m, w, 0.0) for w, m in zip(ws, valids)]
    return lss, ls_, ws


def _attn_fwd(q, k, v, ga, tq, rider):
    S = q.shape[0]
    nq = S // tq

    def body(*refs):
        (q_ref, k_ref, v_ref, ga_ref), (o_ref, on_ref), (low_s,), ride = rider.split(refs, 4, 2, 1)
        p, i = pl.program_id(0), pl.program_id(1)

        @pl.when(jnp.logical_and(p == 0, i == 0))
        def _():
            rider.first(ride)
            low_s[...] = _tri(tq, upper=False)

        @pl.when(jnp.logical_and(p == N_PAIRS - 1, i == 0))
        def _():
            rider.mid(ride)

        lane = lax.broadcasted_iota(jnp.int32, (1, PAIR), 1)
        m0 = lane < HEAD_DIM
        low = low_s[...]
        q2 = q_ref[...]
        qhs = [jnp.where(m0, q2, jnp.zeros_like(q2)), jnp.where(m0, jnp.zeros_like(q2), q2)]

        def tile(kb, c_ls, accs, valid):
            ks = pl.multiple_of(kb * tq, tq)
            kt = k_ref[pl.ds(ks, tq), :]
            vt = v_ref[pl.ds(ks, tq), :]
            _, ls_, ws = _sb_tile(qhs, kt, low, c_ls, valid)
            new_a = [acc + _dot(w.astype(BF16), vt) for acc, w in zip(accs, ws)]
            new_c = [c_l + jnp.sum(l, axis=1, keepdims=True) for c_l, l in zip(c_ls, ls_)]
            return new_c, new_a

        zc, za = jnp.zeros((tq, 1), F32), jnp.zeros((tq, PAIR), F32)

        def first_two():
            c_ls, accs = tile(i, [zc, zc], [za, za], _strictly_causal(tq))
            c_ls, accs = tile(i - 1, c_ls, accs, None)
            return (_weights_left(c_ls), *c_ls, *accs)

        def first_one():
            c_ls, accs = tile(i, [zc, zc], [za, za], _strictly_causal(tq))
            return (jnp.int32(0), *c_ls, *accs)

        st0 = lax.cond(i >= 1, first_two, first_one)

        def more(st):
            return jnp.logical_and(st[0] <= i, st[1] > 0)

        def step(st):
            n, _, c0, c1, a0, a1 = st
            c_ls, accs = tile(i - n, [c0, c1], [a0, a1], None)
            return (n + 1, _weights_left(c_ls), c_ls[0], c_ls[1], accs[0], accs[1])

        st = lax.while_loop(more, step, (jnp.int32(2), *st0))
        o = jnp.where(m0, st[4], st[5])
        o_ref[...] = o
        sq = o * o
        ms0 = jnp.sum(jnp.where(m0, sq, 0.0), axis=-1, keepdims=True) * (1.0 / HEAD_DIM)
        ms1 = jnp.sum(jnp.where(m0, 0.0, sq), axis=-1, keepdims=True) * (1.0 / HEAD_DIM)
        rs = jnp.where(m0, lax.rsqrt(ms0 + RMS_EPS), lax.rsqrt(ms1 + RMS_EPS))
        on_ref[...] = (o * rs * ga_ref[...]).astype(BF16)

        @pl.when(jnp.logical_and(p == N_PAIRS - 1, i == nq - 1))
        def _():
            rider.last(ride)

    return rider.call(
        body, [q, k, v, ga], grid=(N_PAIRS, nq), name="attn_fwd",
        in_specs=[pl.BlockSpec((tq, PAIR), lambda p, i: (i, p)),
                  pl.BlockSpec((S, PAIR), lambda p, i: (0, p)),
                  pl.BlockSpec((S, PAIR), lambda p, i: (0, p)),
                  pl.BlockSpec((1, PAIR), lambda p, i: (0, p))],
        out_specs=[pl.BlockSpec((tq, PAIR), lambda p, i: (i, p)),
                   pl.BlockSpec((tq, PAIR), lambda p, i: (i, p))],
        out_shape=[_sds((S, ATTN_WIDTH), F32), _sds((S, ATTN_WIDTH), BF16)],
        scratch_shapes=[pltpu.VMEM((tq, tq), BF16)], vmem_mib=40)


def _mix_ln1(on, pooled, xh0, g0, b0, w_out, g1, b1, tm, rider):
    S, D = xh0.shape
    n_t = S // tm

    def body(*refs):
        ((on_ref, po_ref, xh0_ref, g0_ref, b0_ref, w_ref, g1_ref, b1_ref), (xh_ref, rstd_ref, xb_ref), _,
         ride) = rider.split(refs, 8, 3, 0)

        @pl.when(pl.program_id(0) == 0)
        def _():
            rider.first(ride)

        mixed = _dot(on_ref[...], w_ref[:ATTN_WIDTH, :]) + _dot(po_ref[...], w_ref[ATTN_WIDTH:, :])
        x0 = xh0_ref[...] * g0_ref[...] + b0_ref[...]
        xh, rstd = _ln_fwd(ALPHA * x0 + mixed)
        xh_ref[...] = xh
        rstd_ref[...] = rstd
        xb_ref[...] = (xh * g1_ref[...] + b1_ref[...]).astype(BF16)

        @pl.when(pl.program_id(0) == n_t - 1)
        def _():
            rider.last(ride)

    return rider.call(
        body, [on, pooled, xh0, g0, b0, w_out, g1, b1], grid=(n_t,), name="mix_ln1",
        in_specs=[_row_spec(tm, ATTN_WIDTH), _row_spec(tm, POOL_WIDTH), _row_spec(tm, D),
                  _const_spec((1, D)), _const_spec((1, D)), _const_spec((D, D)),
                  _const_spec((1, D)), _const_spec((1, D))],
        out_specs=[_row_spec(tm, D), _row_spec(tm, 1), _row_spec(tm, D)],
        out_shape=[_sds((S, D), F32), _sds((S, 1), F32), _sds((S, D), BF16)],
        scratch_shapes=[], vmem_mib=56)


def _mlp_ln2(xh1, x1b, g1, b1, w_up_s, w_down, tm, rider):
    S, D = xh1.shape
    fc = D_FF // N_CHIPS
    n_t = S // tm

    def body(*refs):
        ((xh_ref, xb_ref, g_ref, b_ref, wu_ref, wd_ref), (xh2_ref, rstd_ref, r_ref), (acc_ref,),
         ride) = rider.split(refs, 6, 3, 1)
        i, j = pl.program_id(0), pl.program_id(1)

        @pl.when(jnp.logical_and(i == 0, j == 0))
        def _():
            rider.first(ride)

        @pl.when(j == 0)
        def _():
            acc_ref[...] = jnp.zeros_like(acc_ref)

        r = jnp.maximum(_dot(xb_ref[...], wu_ref[0]), 0.0)
        r_ref[...] = r.astype(BF16)
        acc_ref[...] += _dot((r * r).astype(BF16), wd_ref[...])

        @pl.when(j == N_CHIPS - 1)
        def _():
            x1 = xh_ref[...] * g_ref[...] + b_ref[...]
            xh, rstd = _ln_fwd(ALPHA * x1 + acc_ref[...])
            xh2_ref[...] = xh
            rstd_ref[...] = rstd

        @pl.when(jnp.logical_and(i == n_t - 1, j == N_CHIPS - 1))
        def _():
            rider.last(ride)

    return rider.call(
        body, [xh1, x1b, g1, b1, w_up_s, w_down], grid=(n_t, N_CHIPS), name="mlp_ln2",
        in_specs=[pl.BlockSpec((tm, D), lambda i, j: (i, 0)), pl.BlockSpec((tm, D), lambda i, j: (i, 0)),
                  pl.BlockSpec((1, D), lambda i, j: (0, 0)), pl.BlockSpec((1, D), lambda i, j: (0, 0)),
                  pl.BlockSpec((1, D, fc), lambda i, j: (j, 0, 0)),
                  pl.BlockSpec((fc, D), lambda i, j: (j, 0))],
        out_specs=[pl.BlockSpec((tm, D), lambda i, j: (i, 0)), pl.BlockSpec((tm, 1), lambda i, j: (i, 0)),
                   pl.BlockSpec((tm, fc), lambda i, j: (i, j))],
        out_shape=[_sds((S, D), F32), _sds((S, 1), F32), _sds((S, D_FF), BF16)],
        scratch_shapes=[pltpu.VMEM((tm, D), F32)], vmem_mib=56)


def _ple_ln3_loss(xh2, rstd2, g2, b2, p, w_ple_s, w_gate, g3, b3, target, tm):
    S, D = xh2.shape
    pc = D // N_CHIPS

    def body(xh2_ref, rstd2_ref, g2_ref, b2_ref, p_ref, wp_ref, wg_ref, g3_ref, b3_ref, t_ref,
             dpre2_ref, dhb_ref, dwp_ref, dwg_ref, dg3_ref, db3_ref, dg2_ref, db2_ref, loss_ref):
        i = pl.program_id(0)

        @pl.when(i == 0)
        def _():
            for r in (dwp_ref, dwg_ref, dg3_ref, db3_ref, dg2_ref, db2_ref, loss_ref):
                r[...] = jnp.zeros_like(r)

        xh2 = xh2_ref[...]
        x2 = xh2 * g2_ref[...] + b2_ref[...]
        x2b = x2.astype(BF16)
        gate = 1.0 / (1.0 + jnp.exp(-_dot(x2b, wg_ref[...])))
        pb = p_ref[...].astype(BF16)
        pe = jnp.concatenate([_dot(pb, wp_ref[c]) for c in range(N_CHIPS)], axis=1)
        xh3, rstd3 = _ln_fwd(ALPHA * x2 + pe * gate)
        diff = xh3 * g3_ref[...] + b3_ref[...] - t_ref[...]
        loss_ref[...] += (0.5 / D) * jnp.sum(diff * diff)
        dy = diff * (1.0 / D)
        dg3_ref[...] += _colsum(dy * xh3)
        db3_ref[...] += _colsum(dy)
        dpre3 = _ln_bwd(dy, xh3, rstd3, g3_ref[...])
        dpe_b = (dpre3 * gate).astype(BF16)
        dgp_b = (dpre3 * pe * gate * (1.0 - gate)).astype(BF16)
        dx2 = ALPHA * dpre3 + _dot_nt(dgp_b, wg_ref[...])
        dwg_ref[...] += _dot_tn(x2b, dgp_b)
        for c in range(N_CHIPS):
            dwp_ref[c] += _dot_tn(pb, dpe_b[:, c * pc:(c + 1) * pc])
        dg2_ref[...] += _colsum(dx2 * xh2)
        db2_ref[...] += _colsum(dx2)
        dpre2 = _ln_bwd(dx2, xh2, rstd2_ref[...], g2_ref[...])
        dpre2_ref[...] = dpre2
        dhb_ref[...] = dpre2.astype(BF16)

    vec = _const_spec((1, D))
    return pl.pallas_call(
        body, grid=(S // tm,), name="ple_ln3_loss",
        in_specs=[_row_spec(tm, D), _row_spec(tm, 1), vec, vec, _row_spec(tm, PLE_DIM),
                  _const_spec((N_CHIPS, PLE_DIM, pc)), _const_spec((D, D)), vec, vec, _row_spec(tm, D)],
        out_specs=[_row_spec(tm, D), _row_spec(tm, D), _const_spec((N_CHIPS, PLE_DIM, pc)),
                   _const_spec((D, D)), vec, vec, vec, vec, _const_spec((1, LANES))],
        out_shape=[_sds((S, D), F32), _sds((S, D), BF16), _sds((N_CHIPS, PLE_DIM, pc), F32),
                   _sds((D, D), F32), _sds((1, D), F32), _sds((1, D), F32), _sds((1, D), F32),
                   _sds((1, D), F32), _sds((1, LANES), F32)],
        compiler_params=_cp(58),
    )(*_hbm(xh2, rstd2, g2, b2, p, w_ple_s, w_gate, g3, b3, target))


def _mlp_bwd(rb, dhb, w_up_s, w_down, tm, rider):
    S, D = dhb.shape
    fc = D_FF // N_CHIPS
    n_t = S // tm

    def body(*refs):
        (r_ref, dh_ref, wu_ref, wd_ref), (dx_ref, da_ref), _, ride = rider.split(refs, 4, 2, 0)
        i, j = pl.program_id(0), pl.program_id(1)

        @pl.when(jnp.logical_and(i == 0, j == 0))
        def _():
            rider.first(ride)

        @pl.when(j == 0)
        def _():
            dx_ref[...] = jnp.zeros_like(dx_ref)

        da = (_dot_nt(dh_ref[...], wd_ref[...]) * (2.0 * r_ref[...].astype(F32))).astype(BF16)
        da_ref[...] = da
        dx_ref[...] += _dot_nt(da, wu_ref[0])

        @pl.when(jnp.logical_and(i == n_t - 1, j == N_CHIPS - 1))
        def _():
            rider.last(ride)

    return rider.call(
        body, [rb, dhb, w_up_s, w_down], grid=(n_t, N_CHIPS), name="mlp_bwd",
        in_specs=[pl.BlockSpec((tm, fc), lambda i, j: (i, j)), pl.BlockSpec((tm, D), lambda i, j: (i, 0)),
                  pl.BlockSpec((1, D, fc), lambda i, j: (j, 0, 0)),
                  pl.BlockSpec((fc, D), lambda i, j: (j, 0))],
        out_specs=[pl.BlockSpec((tm, D), lambda i, j: (i, 0)), pl.BlockSpec((tm, fc), lambda i, j: (i, j))],
        out_shape=[_sds((S, D), F32), _sds((S, D_FF), BF16)],
        scratch_shapes=[], vmem_mib=56)


def _tn_matmul(a, b, name, tk, tt, stacked, rider, square_a=False):
    T, K = a.shape
    N = b.shape[1]
    tn = 1024
    grid = (K // tk, N // tn, T // tt)

    def body(*refs):
        (a_ref, b_ref), (o_ref,), _, ride = rider.split(refs, 2, 1, 0)
        at = [pl.program_id(d) for d in range(3)]

        @pl.when(jnp.logical_and(jnp.logical_and(at[0] == 0, at[1] == 0), at[2] == 0))
        def _():
            rider.first(ride)

        @pl.when(at[2] == 0)
        def _():
            o_ref[...] = jnp.zeros_like(o_ref)

        a_t = a_ref[...]
        if square_a:
            a_t = a_t * a_t
        prod = _dot_tn(a_t, b_ref[...])
        if stacked:
            o_ref[0] += prod
        else:
            o_ref[...] += prod

        @pl.when(jnp.logical_and(jnp.logical_and(at[0] == grid[0] - 1, at[1] == grid[1] - 1),
                                 at[2] == grid[2] - 1))
        def _():
            rider.last(ride)

    if stacked:
        out_spec = pl.BlockSpec((1, tk, tn), lambda k, n, t: (n, k, 0))
        out_shape = _sds((N // tn, K, tn), F32)
    else:
        out_spec = pl.BlockSpec((tk, tn), lambda k, n, t: (k, n))
        out_shape = _sds((K, N), F32)
    return rider.call(
        body, [a, b], grid=grid, name=name,
        in_specs=[pl.BlockSpec((tt, tk), lambda k, n, t: (t, k)),
                  pl.BlockSpec((tt, tn), lambda k, n, t: (t, n))],
        out_specs=[out_spec], out_shape=[out_shape], scratch_shapes=[], vmem_mib=58)


def _mix_bwd(dpre2, dx1m, xh1, rstd1, g1, w_out, on, pooled, o_raw, ga, tm, rider):
    S, D = xh1.shape
    n_t = S // tm

    def body(*refs):
        ((dp2_ref, dxm_ref, xh_ref, rstd_ref, g_ref, w_ref, on_ref, po_ref, o_ref, ga_ref),
         (dpre1_ref, do_ref, dpo_ref, dw_ref, dg_ref, db_ref, dga_ref), _, ride) = rider.split(refs, 10, 7, 0)

        @pl.when(pl.program_id(0) == 0)
        def _():
            rider.first(ride)
            for r in (dw_ref, dg_ref, db_ref, dga_ref):
                r[...] = jnp.zeros_like(r)

        xh = xh_ref[...]
        dx1 = ALPHA * dp2_ref[...] + dxm_ref[...]
        dg_ref[...] += _colsum(dx1 * xh)
        db_ref[...] += _colsum(dx1)
        dpre1 = _ln_bwd(dx1, xh, rstd_ref[...], g_ref[...])
        dpre1_ref[...] = dpre1
        dmb = dpre1.astype(BF16)
        dcat = _dot_nt(dmb, w_ref[...])
        dpo_ref[...] = dcat[:, ATTN_WIDTH:]
        dw_ref[:ATTN_WIDTH, :] += _dot_tn(on_ref[...], dmb)
        dw_ref[ATTN_WIDTH:, :] += _dot_tn(po_ref[...], dmb)

        m0 = lax.broadcasted_iota(jnp.int32, (1, PAIR), 1) < HEAD_DIM

        def seg_mean(a):
            s0 = jnp.sum(jnp.where(m0, a, 0.0), axis=-1, keepdims=True)
            s1 = jnp.sum(jnp.where(m0, 0.0, a), axis=-1, keepdims=True)
            return jnp.where(m0, s0, s1) * (1.0 / HEAD_DIM)

        for p in range(N_PAIRS):
            cols = slice(p * PAIR, (p + 1) * PAIR)
            o = o_ref[:, cols]
            rs = lax.rsqrt(seg_mean(o * o) + RMS_EPS)
            oh = o * rs
            don = dcat[:, cols]
            dga_ref[:, cols] += _colsum(don * oh)
            doh = don * ga_ref[:, cols]
            do_ref[:, cols] = rs * (doh - oh * seg_mean(doh * oh))

        @pl.when(pl.program_id(0) == n_t - 1)
        def _():
            rider.last(ride)

    vec = _const_spec((1, D))
    return rider.call(
        body, [dpre2, dx1m, xh1, rstd1, g1, w_out, on, pooled, o_raw, ga], grid=(n_t,), name="mix_bwd",
        in_specs=[_row_spec(tm, D), _row_spec(tm, D), _row_spec(tm, D), _row_spec(tm, 1), vec,
                  _const_spec((D, D)), _row_spec(tm, ATTN_WIDTH), _row_spec(tm, POOL_WIDTH),
                  _row_spec(tm, ATTN_WIDTH), _const_spec((1, ATTN_WIDTH))],
        out_specs=[_row_spec(tm, D), _row_spec(tm, ATTN_WIDTH), _row_spec(tm, POOL_WIDTH),
                   _const_spec((D, D)), vec, vec, _const_spec((1, ATTN_WIDTH))],
        out_shape=[_sds((S, D), F32), _sds((S, ATTN_WIDTH), F32), _sds((S, POOL_WIDTH), F32),
                   _sds((D, D), F32), _sds((1, D), F32), _sds((1, D), F32), _sds((1, ATTN_WIDTH), F32)],
        scratch_shapes=[], vmem_mib=56)


def _pool_bwd(dpooled, d_b, w_pool, pscale, tm, rider):
    S = dpooled.shape[0]
    hb = tm // POOL_HALO
    n_t = S // tm
    te = tm + POOL_HALO

    def body(*refs):
        ((dp_ref, dph_ref, d_ref, wp_ref, sc_ref), (du_ref, dwp_ref, dsc_ref), _,
         ride) = rider.split(refs, 5, 3, 0)
        i = pl.program_id(0)

        @pl.when(i == 0)
        def _():
            rider.first(ride)
            dwp_ref[...] = jnp.zeros_like(dwp_ref)
            dsc_ref[...] = jnp.zeros_like(dsc_ref)

        halo = jnp.where(i < n_t - 1, dph_ref[...], 0.0)
        pos = i * tm + lax.broadcasted_iota(jnp.int32, (te, 1), 0)
        for g in range(N_POOL_GROUPS):
            win = 2 ** (g + 1)
            cols = slice(g * POOL_GROUP, (g + 1) * POOL_GROUP)
            wpb = wp_ref[g].astype(BF16)
            dpt = dp_ref[:, cols]
            dpe = jnp.concatenate([dpt, halo[:, cols]], axis=0)
            dyb = (dpe * sc_ref[:, cols]).astype(BF16)
            dd = _dot_nt(dyb, wpb)
            s = dd / jnp.minimum(pos + 1, win).astype(F32)
            for sh in (1, 2, 4, 8)[:g + 1]:
                s = s + pltpu.roll(s, te - sh, 0)
            du_ref[:, cols] = s[:tm, :] - dd[:tm, :]
            db = d_ref[:, cols]
            dwp_ref[g] += _dot_tn(db, dyb[:tm, :])
            dsc_ref[:, cols] += _colsum(dpt * _dot(db, wpb))

        @pl.when(i == n_t - 1)
        def _():
            rider.last(ride)

    return rider.call(
        body, [dpooled, dpooled, d_b, w_pool, pscale], grid=(n_t,), name="pool_bwd",
        in_specs=[_row_spec(tm, POOL_WIDTH),
                  pl.BlockSpec((POOL_HALO, POOL_WIDTH),
                               lambda i: (jnp.minimum((i + 1) * hb, S // POOL_HALO - 1), 0)),
                  _row_spec(tm, POOL_WIDTH),
                  _const_spec((N_POOL_GROUPS, POOL_GROUP, POOL_GROUP)), _const_spec((1, POOL_WIDTH))],
        out_specs=[_row_spec(tm, POOL_WIDTH), _const_spec((N_POOL_GROUPS, POOL_GROUP, POOL_GROUP)),
                   _const_spec((1, POOL_WIDTH))],
        out_shape=[_sds((S, POOL_WIDTH), F32), _sds((N_POOL_GROUPS, POOL_GROUP, POOL_GROUP), F32),
                   _sds((1, POOL_WIDTH), F32)],
        scratch_shapes=[], vmem_mib=32)


def _attn_bwd(q, k, v, do, tq, rider):
    S = q.shape[0]
    nq = S // tq

    def body(*refs):
        ((q_ref, k_ref, v_ref, do_ref), (dq_ref, dk_ref, dv_ref),
         (g_s, b_s, low_s, upp_s), ride) = rider.split(refs, 4, 3, 4)
        p, i = pl.program_id(0), pl.program_id(1)

        @pl.when(jnp.logical_and(p == 0, i == 0))
        def _():
            rider.first(ride)
            low_s[...] = _tri(tq, upper=False)
            upp_s[...] = _tri(tq, upper=True)

        @pl.when(i == 0)
        def _():
            for r in (dk_ref, dv_ref):
                r[...] = jnp.zeros_like(r)

        lane = lax.broadcasted_iota(jnp.int32, (1, PAIR), 1)
        m0 = lane < HEAD_DIM
        low = low_s[...]
        upp = upp_s[...]
        dob = do_ref[...].astype(BF16)
        q2 = q_ref[...]
        qhs = [jnp.where(m0, q2, jnp.zeros_like(q2)), jnp.where(m0, jnp.zeros_like(q2), q2)]
        dhs = [jnp.where(m0, dob, jnp.zeros_like(dob)), jnp.where(m0, jnp.zeros_like(dob), dob)]
        causal = _strictly_causal(tq)

        def down(kb, c_ls, valid):
            ks = pl.multiple_of(kb * tq, tq)
            kt = k_ref[pl.ds(ks, tq), :]
            vt = v_ref[pl.ds(ks, tq), :]
            lss, ls_, ws = _sb_tile(qhs, kt, low, c_ls, valid)
            dws = [_dot_nt(dh, vt) for dh in dhs]
            for hh in range(2):
                g_s[hh, kb] = dws[hh] * ws[hh]
                b_s[hh, kb] = jnp.exp(lss[hh])
            dv_ref[pl.ds(ks, tq), :] += (_dot_tn(ws[0].astype(BF16), dhs[0])
                                         + _dot_tn(ws[1].astype(BF16), dhs[1]))
            return [c_l + jnp.sum(l, axis=1, keepdims=True) for c_l, l in zip(c_ls, ls_)]

        zc, za = jnp.zeros((tq, 1), F32), jnp.zeros((tq, PAIR), F32)
        def first_two():
            c_ls = down(i - 1, down(i, [zc, zc], causal), None)
            return (_weights_left(c_ls), *c_ls)

        st0 = lax.cond(i >= 1, first_two, lambda: (jnp.int32(0), *down(i, [zc, zc], causal)))

        def more(st):
            return jnp.logical_and(st[0] <= i, st[1] > 0)

        def down_step(st):
            c_ls = down(i - st[0], [st[2], st[3]], None)
            return (st[0] + 1, _weights_left(c_ls), c_ls[0], c_ls[1])

        n_tiles = lax.while_loop(more, down_step, (jnp.int32(2), *st0))[0]

        def up(kb, c_gs, accs, valid):
            ks = pl.multiple_of(kb * tq, tq)
            kt = k_ref[pl.ds(ks, tq), :]
            gs = [g_s[hh, kb] for hh in range(2)]
            pres = [_dot(g.astype(BF16), upp) + c_g for g, c_g in zip(gs, c_gs)]
            dzs = []
            for hh in range(2):
                beta = b_s[hh, kb]
                dz = gs[hh] - beta * (gs[hh] + pres[hh])
                if valid is not None:
                    dz = jnp.where(valid, dz, 0.0)
                dzs.append(dz.astype(BF16))
            new_a = [acc + _dot(dzb, kt) for acc, dzb in zip(accs, dzs)]
            dk_ref[pl.ds(ks, tq), :] += _dot_tn(dzs[0], qhs[0]) + _dot_tn(dzs[1], qhs[1])
            new_c = [c_g + jnp.sum(g, axis=1, keepdims=True) for c_g, g in zip(c_gs, gs)]
            return new_c, new_a

        def up_step(kb, st):
            c_gs, accs = up(kb, [st[0], st[1]], [st[2], st[3]], None)
            return (c_gs[0], c_gs[1], accs[0], accs[1])

        st = lax.fori_loop(i - n_tiles + 1, i - 1, up_step, (zc, zc, za, za))

        def last_two():
            c_gs, accs = up(i - 1, [st[0], st[1]], [st[2], st[3]], None)
            return tuple(up(i, c_gs, accs, causal)[1])

        accs = lax.cond(i >= 1, last_two, lambda: tuple(up(i, [zc, zc], [za, za], causal)[1]))
        dq_ref[...] = jnp.where(m0, accs[0], accs[1]) * Q_SCALE

        @pl.when(jnp.logical_and(p == N_PAIRS - 1, i == nq - 1))
        def _():
            rider.last(ride)

    return rider.call(
        body, [q, k, v, do], grid=(N_PAIRS, nq), name="attn_bwd",
        in_specs=[pl.BlockSpec((tq, PAIR), lambda p, i: (i, p)),
                  pl.BlockSpec((S, PAIR), lambda p, i: (0, p)),
                  pl.BlockSpec((S, PAIR), lambda p, i: (0, p)),
                  pl.BlockSpec((tq, PAIR), lambda p, i: (i, p))],
        out_specs=[pl.BlockSpec((tq, PAIR), lambda p, i: (i, p)),
                   pl.BlockSpec((S, PAIR), lambda p, i: (0, p)),
                   pl.BlockSpec((S, PAIR), lambda p, i: (0, p))],
        out_shape=[_sds((S, ATTN_WIDTH), F32), _sds((S, ATTN_WIDTH), F32), _sds((S, ATTN_WIDTH), F32)],
        scratch_shapes=[pltpu.VMEM((2, nq, tq, tq), F32), pltpu.VMEM((2, nq, tq, tq), F32),
                        pltpu.VMEM((tq, tq), BF16), pltpu.VMEM((tq, tq), BF16)],
        vmem_mib=56)


def _inproj_bwd(dq, dk, dv, du, dpre1, xh0, rstd0, g0, b0, w_in_s, tm):
    S, D = xh0.shape

    def body(dq_ref, dk_ref, dv_ref, du_ref, dp1_ref, xh_ref, rstd_ref, g_ref, b_ref, w_ref,
             gx_ref, dw_ref, dg_ref, db_ref):
        @pl.when(pl.program_id(0) == 0)
        def _():
            for r in (dw_ref, dg_ref, db_ref):
                r[...] = jnp.zeros_like(r)

        xh = xh_ref[...]
        xb = (xh * g_ref[...] + b_ref[...]).astype(BF16)
        dx0 = ALPHA * dp1_ref[...]
        for c, r in enumerate((dq_ref, dk_ref, dv_ref, du_ref)):
            dpb = r[...].astype(BF16)
            dx0 = dx0 + _dot_nt(dpb, w_ref[c])
            dw_ref[c] += _dot_tn(xb, dpb)
        dg_ref[...] += _colsum(dx0 * xh)
        db_ref[...] += _colsum(dx0)
        gx_ref[...] = _ln_bwd(dx0, xh, rstd_ref[...], g_ref[...])

    vec = _const_spec((1, D))
    half = _row_spec(tm, 512)
    return pl.pallas_call(
        body, grid=(S // tm,), name="inproj_bwd",
        in_specs=[half, half, half, half, _row_spec(tm, D), _row_spec(tm, D), _row_spec(tm, 1), vec, vec,
                  _const_spec((N_CHIPS, D, 512))],
        out_specs=[_row_spec(tm, D), _const_spec((N_CHIPS, D, 512)), vec, vec],
        out_shape=[_sds((S, D), F32), _sds((N_CHIPS, D, 512), F32), _sds((1, D), F32), _sds((1, D), F32)],
        compiler_params=_cp(58),
    )(*_hbm(dq, dk, dv, du, dpre1, xh0, rstd0, g0, b0, w_in_s))


def _place():
    return lax.axis_index("x"), lax.axis_index("y"), lax.axis_index("c")


CHIP_FLIPS = ((0, 1), (1, 0), (1, 1))


class _Rider:
    def __init__(self, ins, out_shapes, n_sem, phases, aliases=None):
        self.ins, self.out_shapes, self.n_sem, self.phases = list(ins), list(out_shapes), n_sem, phases
        self.aliases = aliases or {}

    def __add__(self, other):
        na, ma = len(self.ins), len(self.out_shapes)

        def phases(ins, outs, ssem, rsem):
            mine = self.phases(ins[:na], outs[:ma], ssem, rsem)
            rest = pl.ds(self.n_sem, other.n_sem)
            theirs = other.phases(ins[na:], outs[ma:], ssem.at[rest], rsem.at[rest])
            assert len(mine) == 1 and len(theirs) == 1
            return [mine[0] + theirs[0]]

        aliases = {**self.aliases, **{na + i: ma + o for i, o in other.aliases.items()}}
        return _Rider(self.ins + other.ins, self.out_shapes + other.out_shapes, self.n_sem + other.n_sem, phases,
                      aliases)

    def split(self, refs, n_in, n_out, n_scratch):
        a = n_in + len(self.ins)
        b = a + n_out
        c = b + len(self.out_shapes)
        own = (refs[:n_in], refs[a:b], refs[c:c + n_scratch])
        return own + ((refs[n_in:a], refs[b:c]) + tuple(refs[c + n_scratch:]),)

    def first(self, ride):
        for make in self.phases(*ride)[0]:
            make().start()

    def mid(self, ride):
        ph = self.phases(*ride)
        if len(ph) == 2:
            for make in ph[0]:
                make().wait_recv()
            for make in ph[1]:
                make().start()

    def last(self, ride):
        ph = self.phases(*ride)
        if len(ph) == 2:
            for make in ph[0]:
                make().wait_send()
        for make in ph[-1]:
            make().wait()

    def call(self, body, args, *, grid, name, in_specs, out_specs, out_shape, scratch_shapes, vmem_mib,
             prefetch=None):
        n_in, n_out = len(in_specs), len(out_specs)
        sems = [pltpu.SemaphoreType.DMA((self.n_sem,)), pltpu.SemaphoreType.DMA((self.n_sem,))]
        n_pre = 0 if prefetch is None else 1
        grid_spec = pltpu.PrefetchScalarGridSpec(
            num_scalar_prefetch=n_pre, grid=grid,
            in_specs=list(in_specs) + [HBM_SPEC] * len(self.ins),
            out_specs=list(out_specs) + [HBM_SPEC] * len(self.out_shapes),
            scratch_shapes=list(scratch_shapes) + sems)
        return pl.pallas_call(
            body, name=name, grid_spec=grid_spec,
            out_shape=list(out_shape) + self.out_shapes,
            input_output_aliases={n_pre + n_in + i: n_out + o for i, o in self.aliases.items()},
            compiler_params=_cp(vmem_mib),
        )(*([] if prefetch is None else [prefetch]), *_hbm(*args), *self.ins)

    def run(self, name):
        def body(*refs):
            ride = self.split(refs, 0, 0, 0)[3]
            self.first(ride)
            self.mid(ride)
            self.last(ride)

        return self.call(body, [], grid=(), name=name, in_specs=[], out_specs=[], out_shape=[],
                         scratch_shapes=[], vmem_mib=16)


def _remote(src, dst, ssem, rsem, n, dev):
    return functools.partial(pltpu.make_async_remote_copy, src_ref=src, dst_ref=dst, send_sem=ssem.at[n],
                             recv_sem=rsem.at[n], device_id=dev, device_id_type=MESH)


def _cast_into_slot(w, place, name):
    R, C = w.shape
    tr = min(R, 512)

    def body(pl_ref, w_ref, o_ref):
        o_ref[0] = w_ref[...].astype(BF16)

    return pl.pallas_call(
        body, name=name,
        grid_spec=pltpu.PrefetchScalarGridSpec(
            num_scalar_prefetch=1, grid=(R // tr,),
            in_specs=[pl.BlockSpec((tr, C), lambda r, pr: (r, 0))],
            out_specs=pl.BlockSpec((1, tr, C), lambda r, pr: (pr[1], r, 0))),
        out_shape=_sds((N_CHIPS, R, C), BF16),
    )(place, w)


CAST_STEPS = 8


def _cast_rest(ws, place, rider):
    n = len(ws)

    def body(pl_ref, *refs):
        w_refs, o_refs, _, ride = rider.split(refs, n, n, 0)
        r = pl.program_id(0)

        @pl.when(r == 0)
        def _():
            rider.first(ride)

        @pl.when(r == CAST_STEPS // 2)
        def _():
            rider.mid(ride)

        for w_ref, o_ref in zip(w_refs, o_refs):
            o_ref[0] = w_ref[...].astype(BF16)

        @pl.when(r == CAST_STEPS - 1)
        def _():
            rider.last(ride)

    def rows(w):
        return w.shape[0] // CAST_STEPS

    return rider.call(
        body, ws, grid=(CAST_STEPS,), name="cast_weights", prefetch=place,
        in_specs=[pl.BlockSpec((rows(w), w.shape[1]), lambda r, pr: (r, 0)) for w in ws],
        out_specs=[pl.BlockSpec((1, rows(w), w.shape[1]), lambda r, pr: (pr[1], r, 0)) for w in ws],
        out_shape=[_sds((N_CHIPS,) + w.shape, BF16) for w in ws], scratch_shapes=[], vmem_mib=32)


def _gather_rider(stacked, part="both"):
    n, nf = len(stacked), len(CHIP_FLIPS)

    def phases(ins, outs, ssem, rsem):
        x, y, c = _place()
        slot = 2 * x + y
        ici, d2d = [], []
        for w, (i_ref, o_ref) in enumerate(zip(ins, outs)):
            hh = o_ref.shape[1] // 2
            rows = pl.ds(c * hh, hh)
            for f, (fx, fy) in enumerate(CHIP_FLIPS):
                k = w * nf + f
                theirs = 2 * (x ^ fx) + (y ^ fy)
                if part != "pair":
                    ici.append(_remote(i_ref.at[slot, rows], o_ref.at[slot, rows], ssem, rsem, k,
                                       (x ^ fx, y ^ fy, c)))
                if part != "chips":
                    d2d.append(_remote(o_ref.at[theirs, rows], o_ref.at[theirs, rows], ssem, rsem,
                                       (n * nf if part == "both" else 0) + k, (x, y, 1 - c)))
        return [ph for ph in (ici, d2d) if ph]

    return _Rider(stacked, [_sds(s.shape, s.dtype) for s in stacked], (2 if part == "both" else 1) * n * nf,
                  phases, aliases={i: i for i in range(n)})


def _pair_swap_rider(grads):
    def phases(ins, outs, ssem, rsem):
        x, y, c = _place()
        return [[_remote(g.at[:, 1 - c], o, ssem, rsem, k, (x, y, 1 - c))
                 for k, (g, o) in enumerate(zip(ins, outs))]]

    return _Rider(grads, [_sds((N_CHIPS,) + g.shape[2:], g.dtype) for g in grads], len(grads), phases)


def _chip_scatter_rider(parts):
    nf = len(CHIP_FLIPS)

    def phases(ins, outs, ssem, rsem):
        x, y, c = _place()
        return [[_remote(r.at[2 * (x ^ fx) + (y ^ fy)], o.at[f], ssem, rsem, w * nf + f, (x ^ fx, y ^ fy, c))
                 for w, (r, o) in enumerate(zip(ins, outs)) for f, (fx, fy) in enumerate(CHIP_FLIPS)]]

    return _Rider(parts, [_sds((nf,) + r.shape[1:], r.dtype) for r in parts], len(parts) * nf, phases)


def _pair_send_rider(halves):
    def phases(ins, outs, ssem, rsem):
        x, y, c = _place()
        return [[_remote(h, o, ssem, rsem, k, (x, y, 1 - c)) for k, (h, o) in enumerate(zip(ins, outs))]]

    return _Rider(halves, [_sds(h.shape, h.dtype) for h in halves], len(halves), phases)


PAIR_SUM_STEPS = 2
CHIP_SUM_STEPS = 4
ADAMW_STEPS = 4


def _no_rider():
    return _Rider([], [], 1, lambda ins, outs, ssem, rsem: [[]])


def _add_pair(grads, recvs, place, name, rider):
    n = len(grads)

    def body(pl_ref, *refs):
        ins, outs, _, ride = rider.split(refs, 2 * n, 2 * n, 0)
        h, j = pl.program_id(0), pl.program_id(1)

        @pl.when(jnp.logical_and(h == 0, j == 0))
        def _():
            rider.first(ride)

        sums = [ins[2 * w][0, 0] + ins[2 * w + 1][0] for w in range(n)]
        for w in range(n):
            outs[2 * w + 1][0] = sums[w].astype(BF16)

        @pl.when(j == pl_ref[1])
        def _():
            for w in range(n):
                outs[2 * w][...] = sums[w]

        @pl.when(jnp.logical_and(h == PAIR_SUM_STEPS - 1, j == N_CHIPS - 1))
        def _():
            rider.last(ride)

    in_specs, out_specs, out_shape, args = [], [], [], []
    for g, r in zip(grads, recvs):
        _, _, H, C = g.shape
        th = H // PAIR_SUM_STEPS
        spec = pl.BlockSpec((1, th, C), lambda h, j, pr: (j, h, 0))
        in_specs += [pl.BlockSpec((1, 1, th, C), lambda h, j, pr: (j, pr[0], h, 0)), spec]
        out_specs += [pl.BlockSpec((th, C), lambda h, j, pr: (h, 0)), spec]
        out_shape += [_sds((H, C), F32), _sds((N_CHIPS, H, C), BF16)]
        args += [g, r]
    res = rider.call(body, args, grid=(PAIR_SUM_STEPS, N_CHIPS), name=name, prefetch=place, in_specs=in_specs,
                     out_specs=out_specs, out_shape=out_shape, scratch_shapes=[], vmem_mib=32)
    return [(res[2 * w], res[2 * w + 1]) for w in range(n)], res[2 * n:]


def _add_chips(parts, recvs, place, name, rider):
    n = len(parts)

    def body(pl_ref, *refs):
        ins, outs, _, ride = rider.split(refs, 2 * n, n, 0)
        h = pl.program_id(0)

        @pl.when(h == 0)
        def _():
            rider.first(ride)

        for w in range(n):
            p_ref, r_ref = ins[2 * w], ins[2 * w + 1]
            outs[w][...] = p_ref[...] + r_ref[0].astype(F32) + r_ref[1].astype(F32) + r_ref[2].astype(F32)

        @pl.when(h == CHIP_SUM_STEPS - 1)
        def _():
            rider.last(ride)

    in_specs, out_specs, out_shape, args = [], [], [], []
    for p, r in zip(parts, recvs):
        H, C = p.shape
        th = H // CHIP_SUM_STEPS
        in_specs += [pl.BlockSpec((th, C), lambda h, pr: (h, 0)),
                     pl.BlockSpec((len(CHIP_FLIPS), th, C), lambda h, pr: (0, h, 0))]
        out_specs.append(pl.BlockSpec((th, C), lambda h, pr: (h, 0)))
        out_shape.append(_sds((H, C), F32))
        args += [p, r]
    res = rider.call(body, args, grid=(CHIP_SUM_STEPS,), name=name, prefetch=place, in_specs=in_specs,
                     out_specs=out_specs, out_shape=out_shape, scratch_shapes=[], vmem_mib=32)
    return res[:n], res[n:]


def _adamw_math(w, g, m, v):
    m = ADAM_B1 * m + (1.0 - ADAM_B1) * g
    v = ADAM_B2 * v + (1.0 - ADAM_B2) * (g * g)
    m_hat = m / (1.0 - ADAM_B1 ** ADAM_STEP)
    v_hat = v / (1.0 - ADAM_B2 ** ADAM_STEP)
    delta = -ADAM_LR * (m_hat / (jnp.sqrt(v_hat) + ADAM_EPS) + ADAM_WD * w)
    return delta, m, v


def _adamw(ws, mines, theirs, ms, vs, place, name, rider):
    n = len(ws)

    def body(pl_ref, *refs):
        ins, outs, _, ride = rider.split(refs, 5 * n, 4 * n, 0)
        h, r = pl.program_id(0), pl.program_id(1)

        @pl.when(jnp.logical_and(h == 0, r == 0))
        def _():
            rider.first(ride)

        for k in range(n):
            w_ref, a_ref, b_ref, m_ref, v_ref = ins[5 * k:5 * k + 5]
            g = jnp.where(h == pl_ref[0], a_ref[...], b_ref[...])
            d, mo, vo = _adamw_math(w_ref[...], g, m_ref[...], v_ref[...])
            for o_ref, val in zip(outs[4 * k:4 * k + 4], (g, d, mo, vo)):
                o_ref[...] = val

        @pl.when(jnp.logical_and(h == 1, r == ADAMW_STEPS - 1))
        def _():
            rider.last(ride)

    in_specs, out_specs, out_shape, args = [], [], [], []
    for w, a, b, m, v in zip(ws, mines, theirs, ms, vs):
        R, C = w.shape
        th = (R // 2) // ADAMW_STEPS
        whole = pl.BlockSpec((th, C), lambda h, r, pr: (h * ADAMW_STEPS + r, 0))
        mine_spec = pl.BlockSpec((th, C), lambda h, r, pr: (jnp.where(h == pr[0], r, 0), 0))
        theirs_spec = pl.BlockSpec((th, C), lambda h, r, pr: (jnp.where(h == pr[0], 0, r), 0))
        in_specs += [whole, mine_spec, theirs_spec, whole, whole]
        out_specs += [whole] * 4
        out_shape += [_sds((R, C), F32)] * 4
        args += [w, a, b, m, v]
    res = rider.call(body, args, grid=(2, ADAMW_STEPS), name=name, prefetch=place, in_specs=in_specs,
                     out_specs=out_specs, out_shape=out_shape, scratch_shapes=[], vmem_mib=40)
    return [tuple(res[4 * k:4 * k + 4]) for k in range(n)], res[4 * n:]


DEVICE_FLIPS = tuple((fx, fy, fc) for fx in (0, 1) for fy in (0, 1) for fc in (0, 1))[1:]


def _pack_exchange_rider(pack):
    def phases(ins, outs, ssem, rsem):
        x, y, c = _place()
        mine = outs[0].at[4 * x + 2 * y + c]
        copies = [_remote(ins[0], mine, ssem, rsem, k, (x ^ fx, y ^ fy, c ^ fc))
                  for k, (fx, fy, fc) in enumerate(DEVICE_FLIPS)]
        copies.append(functools.partial(pltpu.make_async_copy, ins[0], mine, ssem.at[len(DEVICE_FLIPS)]))
        return [copies]

    return _Rider([pack], [_sds((N_DEV,) + pack.shape, pack.dtype)], len(DEVICE_FLIPS) + 1, phases)


def _small_sum_adamw(recv_a, recv_b, wpack, mpack, vpack):
    R = wpack.shape[0]

    def body(a_ref, b_ref, w_ref, m_ref, v_ref, gs_ref, d_ref, mo_ref, vo_ref):
        ta, tb = a_ref[0], b_ref[0]
        for dev in range(1, N_DEV):
            ta = ta + a_ref[dev]
            tb = tb + b_ref[dev]
        total = jnp.concatenate([ta, tb], axis=0)
        gs_ref[...] = total
        d, mo, vo = _adamw_math(w_ref[...], total, m_ref[...], v_ref[...])
        d_ref[...] = d
        mo_ref[...] = mo
        vo_ref[...] = vo

    return pl.pallas_call(
        body, name="small_sum_adamw", in_specs=[VMEM_SPEC] * 5, out_specs=[VMEM_SPEC] * 4,
        out_shape=[_sds((R, LANES), F32)] * 4,
    )(recv_a, recv_b, wpack, mpack, vpack)


def _rows8(a):
    a = a.reshape(-1, LANES)
    pad = (-a.shape[0]) % 8
    return jnp.pad(a, ((0, pad), (0, 0))) if pad else a


def _pack(parts):
    return jnp.concatenate([_rows8(a) for a in parts], axis=0)


def _unpack(pack, like):
    out, row = [], 0
    for a in like:
        n = a.size // LANES
        out.append(pack[row:row + n].reshape(a.shape))
        row += n + (-n) % 8
    return out


def kernel(x, p, emb_ln_g, emb_ln_b, w_in, attn_out_g, w_pool, pool_scale, w_out, ln1_g, ln1_b, w_up, w_down, ln2_g, ln2_b, w_ple, w_ple_gate, ln3_g, ln3_b, loss_target, m_emb_ln_g, m_emb_ln_b, m_w_in, m_attn_out_g, m_w_pool, m_pool_scale, m_w_out, m_ln1_g, m_ln1_b, m_w_up, m_w_down, m_ln2_g, m_ln2_b, m_w_ple, m_w_ple_gate, m_ln3_g, m_ln3_b, v_emb_ln_g, v_emb_ln_b, v_w_in, v_attn_out_g, v_w_pool, v_pool_scale, v_w_out, v_ln1_g, v_ln1_b, v_w_up, v_w_down, v_ln2_g, v_ln2_b, v_w_ple, v_w_ple_gate, v_ln3_g, v_ln3_b):
    S = x.shape[1]
    tq = min(256, S)
    tm_mlp = min(1024, S)
    tm_pool = min(1024, S)
    tm_ln = min(512, S)
    tm_fwd = min(1024, S)
    xs = x[0]
    ps = p[0, 0]
    tgt = loss_target[0]
    row = lambda a: a.reshape(1, -1)
    g0, b0 = row(emb_ln_g), row(emb_ln_b)
    g1, b1, g2, b2, g3, b3 = ln1_g, ln1_b, ln2_g, ln2_b, ln3_g, ln3_b
    wp = w_pool[0]

    xi, yi, ci = _place()
    place = jnp.stack([ci, 2 * xi + yi]).astype(jnp.int32)
    names = ["w_in", "w_out", "w_up", "w_down", "w_ple", "w_ple_gate"]

    big = [w_in[0], w_out[0], w_up[0], w_down[0], w_ple[0], w_ple_gate[0]]
    s_in = _cast_into_slot(big[0], place, "cast_w_in")
    s_out, s_up, s_down, s_ple, s_gate, w_in_s = _cast_rest(big[1:], place, _gather_rider([s_in]))

    xh0, rstd0, q, k, v, d_b, pooled, s_out, s_ple, s_gate = _embln_inproj(
        xs, g0, b0, w_in_s, wp, pool_scale, tm_fwd, _gather_rider([s_out, s_ple, s_gate], "chips"))
    o_raw, on, s_up, s_down, w_out_s, w_ple_s, w_gate_s = _attn_fwd(
        q, k, v, attn_out_g, tq, _gather_rider([s_up, s_down], "chips") + _gather_rider([s_out, s_ple, s_gate], "pair"))
    w_out_f = w_out_s.reshape(D_MODEL, D_MODEL)
    w_gate_f = w_gate_s.reshape(D_MODEL, D_MODEL)
    xh1, rstd1, x1b, w_up_s, w_down_s = _mix_ln1(on, pooled, xh0, g0, b0, w_out_f, g1, b1, tm_fwd,
                                                 _gather_rider([s_up, s_down], "pair"))
    w_down_f = w_down_s.reshape(D_FF, D_MODEL)
    xh2, rstd2, rb = _mlp_ln2(xh1, x1b, g1, b1, w_up_s, w_down_f, tm_mlp, _no_rider())

    (dpre2, dhb, dw_ple, dw_gate, dg3, db3, dg2, db2, loss_row) = _ple_ln3_loss(
        xh2, rstd2, g2, b2, ps, w_ple_s, w_gate_f, g3, b3, tgt, tm_ln)
    def halves_of(g):
        return g.reshape(N_CHIPS, 2, g.shape[1] // 2, g.shape[2])

    ple_halves = [halves_of(dw_ple), halves_of(dw_gate.reshape(N_CHIPS, D_MODEL // N_CHIPS, D_MODEL))]
    dx1m, da, *ple_pair = _mlp_bwd(rb, dhb, w_up_s, w_down_f, tm_mlp, _pair_swap_rider(ple_halves))
    (dw_up,) = _tn_matmul(x1b, da, "grad_w_up", 1024, S, True, _no_rider())
    up_halves = halves_of(dw_up)
    dw_down, up_pair = _tn_matmul(rb, dhb, "grad_w_down", 1024, S, False,
                                  _pair_swap_rider([up_halves]), square_a=True)
    down_halves = halves_of(dw_down.reshape(N_CHIPS, D_FF // N_CHIPS, D_MODEL))
    dpre1, do, dpooled, dw_out, dg1, db1, dga, down_pair = _mix_bwd(
        dpre2, dx1m, xh1, rstd1, g1, w_out_f, on, pooled, o_raw, attn_out_g, tm_ln,
        _pair_swap_rider([down_halves]))
    out_halves = halves_of(dw_out.reshape(N_CHIPS, D_MODEL // N_CHIPS, D_MODEL))
    du, dwp, dsc, out_pair = _pool_bwd(dpooled, d_b, wp, pool_scale, tm_pool, _pair_swap_rider([out_halves]))
    early_sum, _ = _add_pair(
        [out_halves, up_halves, down_halves] + ple_halves, [out_pair, up_pair, down_pair] + ple_pair, place,
        "pair_sum_early", _no_rider())
    pack_a = _pack([jnp.broadcast_to(loss_row, (8, LANES)), dwp, dsc, dg1, db1, dg2, db2, dg3, db3])
    riding = _chip_scatter_rider([b for _, b in early_sum]) + _pack_exchange_rider(pack_a)
    dq, dk, dv, *arrived = _attn_bwd(q, k, v, do, tq, riding)
    early_chips, recv_a = arrived[:-1], arrived[-1]
    grad_x, dw_in, dg0, db0 = _inproj_bwd(dq, dk, dv, du, dpre1, xh0, rstd0, g0, b0, w_in_s, tm_ln)

    in_halves = halves_of(dw_in)
    pack_b = _pack([dg0, db0, dga])
    early_mine, (in_pair, recv_b) = _add_chips(
        [s for s, _ in early_sum], early_chips, place, "chip_sum_early",
        _pair_swap_rider([in_halves]) + _pack_exchange_rider(pack_b))
    (in_sum,), early_theirs = _add_pair([in_halves], [in_pair], place, "pair_sum_w_in", _pair_send_rider(early_mine))
    ms = [m_w_in, m_w_out, m_w_up, m_w_down, m_w_ple, m_w_ple_gate]
    vs = [v_w_in, v_w_out, v_w_up, v_w_down, v_w_ple, v_w_ple_gate]
    early_res, _ = _adamw(big[1:], early_mine, early_theirs, [m[0] for m in ms[1:]], [v[0] for v in vs[1:]],
                          place, "adamw_early", _no_rider())
    (in_chips,) = _chip_scatter_rider([in_sum[1]]).run("reduce_chips_late")
    (in_mine,), _ = _add_chips([in_sum[0]], [in_chips], place, "chip_sum_w_in", _no_rider())
    (in_theirs,) = _pair_send_rider([in_mine]).run("gather_pair_w_in")
    in_res, _ = _adamw(big[:1], [in_mine], [in_theirs], [ms[0][0]], [vs[0][0]], place, "adamw_w_in", _no_rider())
    big_out = {n: tuple(r.reshape(m.shape) for r in res4) for n, res4, m in zip(names, in_res + early_res, ms)}

    small_names = ["w_pool", "pool_scale", "ln1_g", "ln1_b", "ln2_g", "ln2_b", "ln3_g", "ln3_b",
                   "emb_ln_g", "emb_ln_b", "attn_out_g"]
    small_w = [w_pool, pool_scale, ln1_g, ln1_b, ln2_g, ln2_b, ln3_g, ln3_b, emb_ln_g, emb_ln_b, attn_out_g]
    small_m = [m_w_pool, m_pool_scale, m_ln1_g, m_ln1_b, m_ln2_g, m_ln2_b, m_ln3_g, m_ln3_b,
               m_emb_ln_g, m_emb_ln_b, m_attn_out_g]
    small_v = [v_w_pool, v_pool_scale, v_ln1_g, v_ln1_b, v_ln2_g, v_ln2_b, v_ln3_g, v_ln3_b,
               v_emb_ln_g, v_emb_ln_b, v_attn_out_g]
    loss_like = jnp.zeros((8, LANES), F32)
    gs, ds, mos, vos = _small_sum_adamw(recv_a, recv_b, _pack([loss_like] + small_w), _pack([loss_like] + small_m),
                                        _pack([jnp.ones((8, LANES), F32)] + small_v))
    like = [loss_like] + small_w
    gs_u, ds_u, mos_u, vos_u = (_unpack(a, like) for a in (gs, ds, mos, vos))
    loss = gs_u[0][0, 0]
    small_out = {n: (gs_u[i + 1], ds_u[i + 1], mos_u[i + 1], vos_u[i + 1]) for i, n in enumerate(small_names)}

    order = ["emb_ln_g", "emb_ln_b", "w_in", "attn_out_g", "w_pool", "pool_scale", "w_out", "ln1_g", "ln1_b",
             "w_up", "w_down", "ln2_g", "ln2_b", "w_ple", "w_ple_gate", "ln3_g", "ln3_b"]
    res = {**big_out, **small_out}
    outs = [loss, grad_x.reshape(x.shape)]
    for kind in range(4):
        outs += [res[n][kind] for n in order]
    return tuple(outs)
```

```python
import functools

import jax
import jax.numpy as jnp
from jax import lax
from jax.experimental import pallas as pl
from jax.experimental.pallas import tpu as pltpu

F32 = jnp.float32
BF16 = jnp.bfloat16

D_MODEL = 1024
ATTN_WIDTH = 512
POOL_WIDTH = 512
HEAD_DIM = 64
PAIR = 2 * HEAD_DIM
N_PAIRS = ATTN_WIDTH // PAIR
N_POOL_GROUPS = 4
POOL_GROUP = 128
POOL_HALO = 16
D_FF = 4096
PLE_DIM = 256
N_CHIPS = 4
N_DEV = 8
LN_EPS = 1e-5
RMS_EPS = 1e-6
ALPHA = float(2.0 ** 0.25)
Q_SCALE = 0.125
ADAM_LR = 0.001
ADAM_B1 = 0.9
ADAM_B2 = 0.999
ADAM_EPS = 1e-08
ADAM_WD = 0.01
ADAM_STEP = 10
LANES = 128
MIB = 1024 * 1024

MESH = pl.DeviceIdType.MESH
HBM_SPEC = pl.BlockSpec(memory_space=pltpu.HBM)
VMEM_SPEC = pl.BlockSpec(memory_space=pltpu.VMEM)


def _cp(vmem_mib):
    return pltpu.CompilerParams(vmem_limit_bytes=vmem_mib * MIB)


def _dot(a, b):
    return jnp.dot(a, b, preferred_element_type=F32)


def _dot_nt(a, b):
    return lax.dot_general(a, b, (((1,), (1,)), ((), ())), preferred_element_type=F32)


def _dot_tn(a, b):
    return lax.dot_general(a, b, (((0,), (0,)), ((), ())), preferred_element_type=F32)


def _ln_fwd(pre):
    mu = jnp.mean(pre, axis=-1, keepdims=True)
    xc = pre - mu
    var = jnp.mean(xc * xc, axis=-1, keepdims=True)
    rstd = lax.rsqrt(var + LN_EPS)
    return xc * rstd, rstd


def _ln_bwd(dy, xh, rstd, g):
    dxh = dy * g
    m1 = jnp.mean(dxh, axis=-1, keepdims=True)
    m2 = jnp.mean(dxh * xh, axis=-1, keepdims=True)
    return rstd * (dxh - m1 - xh * m2)


def _colsum(a):
    return jnp.sum(a, axis=0, keepdims=True)


def _neg_softplus(z):
    return -(jnp.maximum(z, 0.0) + jnp.log(1.0 + jnp.exp(-jnp.abs(z))))


def _row_spec(tm, n):
    return pl.BlockSpec((tm, n), lambda i: (i, 0))


def _const_spec(shape):
    nd = len(shape)
    return pl.BlockSpec(shape, lambda *_: (0,) * nd)


def _hbm(*arrays):
    return [pltpu.with_memory_space_constraint(a, pltpu.HBM) for a in arrays]


def _sds(shape, dtype):
    return pltpu.HBM(shape, dtype)


def _embln_inproj(x, g0, b0, w_in_s, w_pool, pscale, tm, rider):
    S, D = x.shape
    n_t = S // tm

    def body(*refs):
        ((x_ref, g_ref, b_ref, w_ref, wp_ref, sc_ref),
         (xh_ref, rstd_ref, q_ref, k_ref, v_ref, d_ref, pooled_ref), (halo_s,), ride) = rider.split(refs, 6, 7, 1)
        i = pl.program_id(0)

        @pl.when(i == 0)
        def _():
            rider.first(ride)
            halo_s[...] = jnp.zeros_like(halo_s)

        @pl.when(i == (3 * n_t) // 4)
        def _():
            rider.mid(ride)

        xh, rstd = _ln_fwd(x_ref[...])
        xh_ref[...] = xh
        rstd_ref[...] = rstd
        xb = (xh * g_ref[...] + b_ref[...]).astype(BF16)
        q_ref[...] = (_dot(xb, w_ref[0]) * Q_SCALE).astype(BF16)
        k_ref[...] = _dot(xb, w_ref[1]).astype(BF16)
        v_ref[...] = _dot(xb, w_ref[2]).astype(BF16)
        u = _dot(xb, w_ref[3])
        halo = halo_s[...]
        pos = i * tm + lax.broadcasted_iota(jnp.int32, (tm, 1), 0)
        for g in range(N_POOL_GROUPS):
            win = 2 ** (g + 1)
            cols = slice(g * POOL_GROUP, (g + 1) * POOL_GROUP)
            ut = u[:, cols]
            s = jnp.concatenate([halo[:, cols], ut], axis=0)
            for sh in (1, 2, 4, 8)[:g + 1]:
                s = s + pltpu.roll(s, sh, 0)
            cnt = jnp.minimum(pos + 1, win).astype(F32)
            db = (s[POOL_HALO:, :] / cnt - ut).astype(BF16)
            d_ref[:, cols] = db
            pooled_ref[:, cols] = (_dot(db, wp_ref[g].astype(BF16)) * sc_ref[:, cols]).astype(BF16)
        halo_s[...] = u[tm - POOL_HALO:, :]

        @pl.when(i == n_t - 1)
        def _():
            rider.last(ride)

    half = _row_spec(tm, 512)
    return rider.call(
        body, [x, g0, b0, w_in_s, w_pool, pscale], grid=(n_t,), name="embln_inproj",
        in_specs=[_row_spec(tm, D), _const_spec((1, D)), _const_spec((1, D)), _const_spec((N_CHIPS, D, 512)),
                  _const_spec((N_POOL_GROUPS, POOL_GROUP, POOL_GROUP)), _const_spec((1, POOL_WIDTH))],
        out_specs=[_row_spec(tm, D), _row_spec(tm, 1), half, half, half, half, half],
        out_shape=[_sds((S, D), F32), _sds((S, 1), F32)] + [_sds((S, 512), BF16)] * 5,
        scratch_shapes=[pltpu.VMEM((POOL_HALO, POOL_WIDTH), F32)], vmem_mib=56)


def _tri(n, upper):
    r = lax.broadcasted_iota(jnp.int32, (n, n), 0)
    c = lax.broadcasted_iota(jnp.int32, (n, n), 1)
    keep = (r < c) if upper else (r > c)
    return jnp.where(keep, 1.0, 0.0).astype(BF16)


def _strictly_causal(n):
    return lax.broadcasted_iota(jnp.int32, (n, n), 1) < lax.broadcasted_iota(jnp.int32, (n, n), 0)


LOG_WEIGHT_FLOOR = -110.0


def _weights_left(c_ls):
    return (jnp.max(jnp.maximum(c_ls[0], c_ls[1])) > LOG_WEIGHT_FLOOR).astype(jnp.int32)


def _sb_tile(qhs, kt, low, c_ls, valid):
    valids = valid if isinstance(valid, (list, tuple)) else [valid] * len(qhs)
    zs = [_dot_nt(qh, kt) for qh in qhs]
    lrs = [_neg_softplus(z) for z in zs]
    ls_ = [lr if m is None else jnp.where(m, lr, 0.0) for lr, m in zip(lrs, valids)]
    sfx = [_dot(l.astype(BF16), low) + c_l for l, c_l in zip(ls_, c_ls)]
    lss = [z + lr for z, lr in zip(zs, lrs)]
    ws = [jnp.exp(ls + s) for ls, s in zip(lss, sfx)]
    ws = [w if m is None else jnp.where(m, w, 0.0) for w, m in zip(ws, valids)]
    return lss, ls_, ws


def _attn_fwd(q, k, v, ga, tq, rider):
    S = q.shape[0]
    nq = S // tq

    def body(*refs):
        (q_ref, k_ref, v_ref, ga_ref), (o_ref, on_ref), (low_s,), ride = rider.split(refs, 4, 2, 1)
        p, i = pl.program_id(0), pl.program_id(1)

        @pl.when(jnp.logical_and(p == 0, i == 0))
        def _():
            rider.first(ride)
            low_s[...] = _tri(tq, upper=False)

        @pl.when(jnp.logical_and(p == N_PAIRS - 1, i == 0))
        def _():
            rider.mid(ride)

        lane = lax.broadcasted_iota(jnp.int32, (1, PAIR), 1)
        m0 = lane < HEAD_DIM
        low = low_s[...]
        q2 = q_ref[...]
        qhs = [jnp.where(m0, q2, jnp.zeros_like(q2)), jnp.where(m0, jnp.zeros_like(q2), q2)]

        def tile(kb, c_ls, accs, valid):
            ks = pl.multiple_of(kb * tq, tq)
            kt = k_ref[pl.ds(ks, tq), :]
            vt = v_ref[pl.ds(ks, tq), :]
            _, ls_, ws = _sb_tile(qhs, kt, low, c_ls, valid)
            new_a = [acc + _dot(w.astype(BF16), vt) for acc, w in zip(accs, ws)]
            new_c = [c_l + jnp.sum(l, axis=1, keepdims=True) for c_l, l in zip(c_ls, ls_)]
            return new_c, new_a

        zc, za = jnp.zeros((tq, 1), F32), jnp.zeros((tq, PAIR), F32)

        def first_two():
            c_ls, accs = tile(i, [zc, zc], [za, za], _strictly_causal(tq))
            c_ls, accs = tile(i - 1, c_ls, accs, None)
            return (_weights_left(c_ls), *c_ls, *accs)

        def first_one():
            c_ls, accs = tile(i, [zc, zc], [za, za], _strictly_causal(tq))
            return (jnp.int32(0), *c_ls, *accs)

        st0 = lax.cond(i >= 1, first_two, first_one)

        def more(st):
            return jnp.logical_and(st[0] <= i, st[1] > 0)

        def step(st):
            n, _, c0, c1, a0, a1 = st
            c_ls, accs = tile(i - n, [c0, c1], [a0, a1], None)
            return (n + 1, _weights_left(c_ls), c_ls[0], c_ls[1], accs[0], accs[1])

        st = lax.while_loop(more, step, (jnp.int32(2), *st0))
        o = jnp.where(m0, st[4], st[5])
        o_ref[...] = o
        sq = o * o
        ms0 = jnp.sum(jnp.where(m0, sq, 0.0), axis=-1, keepdims=True) * (1.0 / HEAD_DIM)
        ms1 = jnp.sum(jnp.where(m0, 0.0, sq), axis=-1, keepdims=True) * (1.0 / HEAD_DIM)
        rs = jnp.where(m0, lax.rsqrt(ms0 + RMS_EPS), lax.rsqrt(ms1 + RMS_EPS))
        on_ref[...] = (o * rs * ga_ref[...]).astype(BF16)

        @pl.when(jnp.logical_and(p == N_PAIRS - 1, i == nq - 1))
        def _():
            rider.last(ride)

    return rider.call(
        body, [q, k, v, ga], grid=(N_PAIRS, nq), name="attn_fwd",
        in_specs=[pl.BlockSpec((tq, PAIR), lambda p, i: (i, p)),
                  pl.BlockSpec((S, PAIR), lambda p, i: (0, p)),
                  pl.BlockSpec((S, PAIR), lambda p, i: (0, p)),
                  pl.BlockSpec((1, PAIR), lambda p, i: (0, p))],
        out_specs=[pl.BlockSpec((tq, PAIR), lambda p, i: (i, p)),
                   pl.BlockSpec((tq, PAIR), lambda p, i: (i, p))],
        out_shape=[_sds((S, ATTN_WIDTH), F32), _sds((S, ATTN_WIDTH), BF16)],
        scratch_shapes=[pltpu.VMEM((tq, tq), BF16)], vmem_mib=40)


def _mix_ln1(on, pooled, xh0, g0, b0, w_out, g1, b1, tm, rider):
    S, D = xh0.shape
    n_t = S // tm

    def body(*refs):
        ((on_ref, po_ref, xh0_ref, g0_ref, b0_ref, w_ref, g1_ref, b1_ref), (xh_ref, rstd_ref, xb_ref), _,
         ride) = rider.split(refs, 8, 3, 0)

        @pl.when(pl.program_id(0) == 0)
        def _():
            rider.first(ride)

        mixed = _dot(on_ref[...], w_ref[:ATTN_WIDTH, :]) + _dot(po_ref[...], w_ref[ATTN_WIDTH:, :])
        x0 = xh0_ref[...] * g0_ref[...] + b0_ref[...]
        xh, rstd = _ln_fwd(ALPHA * x0 + mixed)
        xh_ref[...] = xh
        rstd_ref[...] = rstd
        xb_ref[...] = (xh * g1_ref[...] + b1_ref[...]).astype(BF16)

        @pl.when(pl.program_id(0) == n_t - 1)
        def _():
            rider.last(ride)

    return rider.call(
        body, [on, pooled, xh0, g0, b0, w_out, g1, b1], grid=(n_t,), name="mix_ln1",
        in_specs=[_row_spec(tm, ATTN_WIDTH), _row_spec(tm, POOL_WIDTH), _row_spec(tm, D),
                  _const_spec((1, D)), _const_spec((1, D)), _const_spec((D, D)),
                  _const_spec((1, D)), _const_spec((1, D))],
        out_specs=[_row_spec(tm, D), _row_spec(tm, 1), _row_spec(tm, D)],
        out_shape=[_sds((S, D), F32), _sds((S, 1), F32), _sds((S, D), BF16)],
        scratch_shapes=[], vmem_mib=56)


def _mlp_ln2(xh1, x1b, g1, b1, w_up_s, w_down, tm, rider):
    S, D = xh1.shape
    fc = D_FF // N_CHIPS
    n_t = S // tm

    def body(*refs):
        ((xh_ref, xb_ref, g_ref, b_ref, wu_ref, wd_ref), (xh2_ref, rstd_ref, r_ref), (acc_ref,),
         ride) = rider.split(refs, 6, 3, 1)
        i, j = pl.program_id(0), pl.program_id(1)

        @pl.when(jnp.logical_and(i == 0, j == 0))
        def _():
            rider.first(ride)

        @pl.when(j == 0)
        def _():
            acc_ref[...] = jnp.zeros_like(acc_ref)

        r = jnp.maximum(_dot(xb_ref[...], wu_ref[0]), 0.0)
        r_ref[...] = r.astype(BF16)
        acc_ref[...] += _dot((r * r).astype(BF16), wd_ref[...])

        @pl.when(j == N_CHIPS - 1)
        def _():
            x1 = xh_ref[...] * g_ref[...] + b_ref[...]
            xh, rstd = _ln_fwd(ALPHA * x1 + acc_ref[...])
            xh2_ref[...] = xh
            rstd_ref[...] = rstd

        @pl.when(jnp.logical_and(i == n_t - 1, j == N_CHIPS - 1))
        def _():
            rider.last(ride)

    return rider.call(
        body, [xh1, x1b, g1, b1, w_up_s, w_down], grid=(n_t, N_CHIPS), name="mlp_ln2",
        in_specs=[pl.BlockSpec((tm, D), lambda i, j: (i, 0)), pl.BlockSpec((tm, D), lambda i, j: (i, 0)),
                  pl.BlockSpec((1, D), lambda i, j: (0, 0)), pl.BlockSpec((1, D), lambda i, j: (0, 0)),
                  pl.BlockSpec((1, D, fc), lambda i, j: (j, 0, 0)),
                  pl.BlockSpec((fc, D), lambda i, j: (j, 0))],
        out_specs=[pl.BlockSpec((tm, D), lambda i, j: (i, 0)), pl.BlockSpec((tm, 1), lambda i, j: (i, 0)),
                   pl.BlockSpec((tm, fc), lambda i, j: (i, j))],
        out_shape=[_sds((S, D), F32), _sds((S, 1), F32), _sds((S, D_FF), BF16)],
        scratch_shapes=[pltpu.VMEM((tm, D), F32)], vmem_mib=56)


def _ple_ln3_loss(xh2, rstd2, g2, b2, p, w_ple_s, w_gate, g3, b3, target, tm):
    S, D = xh2.shape
    pc = D // N_CHIPS

    def body(xh2_ref, rstd2_ref, g2_ref, b2_ref, p_ref, wp_ref, wg_ref, g3_ref, b3_ref, t_ref,
             dpre2_ref, dhb_ref, dwp_ref, dwg_ref, dg3_ref, db3_ref, dg2_ref, db2_ref, loss_ref):
        i = pl.program_id(0)

        @pl.when(i == 0)
        def _():
            for r in (dwp_ref, dwg_ref, dg3_ref, db3_ref, dg2_ref, db2_ref, loss_ref):
                r[...] = jnp.zeros_like(r)

        xh2 = xh2_ref[...]
        x2 = xh2 * g2_ref[...] + b2_ref[...]
        x2b = x2.astype(BF16)
        gate = 1.0 / (1.0 + jnp.exp(-_dot(x2b, wg_ref[...])))
        pb = p_ref[...].astype(BF16)
        pe = jnp.concatenate([_dot(pb, wp_ref[c]) for c in range(N_CHIPS)], axis=1)
        xh3, rstd3 = _ln_fwd(ALPHA * x2 + pe * gate)
        diff = xh3 * g3_ref[...] + b3_ref[...] - t_ref[...]
        loss_ref[...] += (0.5 / D) * jnp.sum(diff * diff)
        dy = diff * (1.0 / D)
        dg3_ref[...] += _colsum(dy * xh3)
        db3_ref[...] += _colsum(dy)
        dpre3 = _ln_bwd(dy, xh3, rstd3, g3_ref[...])
        dpe_b = (dpre3 * gate).astype(BF16)
        dgp_b = (dpre3 * pe * gate * (1.0 - gate)).astype(BF16)
        dx2 = ALPHA * dpre3 + _dot_nt(dgp_b, wg_ref[...])
        dwg_ref[...] += _dot_tn(x2b, dgp_b)
        for c in range(N_CHIPS):
            dwp_ref[c] += _dot_tn(pb, dpe_b[:, c * pc:(c + 1) * pc])
        dg2_ref[...] += _colsum(dx2 * xh2)
        db2_ref[...] += _colsum(dx2)
        dpre2 = _ln_bwd(dx2, xh2, rstd2_ref[...], g2_ref[...])
        dpre2_ref[...] = dpre2
        dhb_ref[...] = dpre2.astype(BF16)

    vec = _const_spec((1, D))
    return pl.pallas_call(
        body, grid=(S // tm,), name="ple_ln3_loss",
        in_specs=[_row_spec(tm, D), _row_spec(tm, 1), vec, vec, _row_spec(tm, PLE_DIM),
                  _const_spec((N_CHIPS, PLE_DIM, pc)), _const_spec((D, D)), vec, vec, _row_spec(tm, D)],
        out_specs=[_row_spec(tm, D), _row_spec(tm, D), _const_spec((N_CHIPS, PLE_DIM, pc)),
                   _const_spec((D, D)), vec, vec, vec, vec, _const_spec((1, LANES))],
        out_shape=[_sds((S, D), F32), _sds((S, D), BF16), _sds((N_CHIPS, PLE_DIM, pc), F32),
                   _sds((D, D), F32), _sds((1, D), F32), _sds((1, D), F32), _sds((1, D), F32),
                   _sds((1, D), F32), _sds((1, LANES), F32)],
        compiler_params=_cp(58),
    )(*_hbm(xh2, rstd2, g2, b2, p, w_ple_s, w_gate, g3, b3, target))


def _mlp_bwd(rb, dhb, w_up_s, w_down, tm, rider):
    S, D = dhb.shape
    fc = D_FF // N_CHIPS
    n_t = S // tm

    def body(*refs):
        (r_ref, dh_ref, wu_ref, wd_ref), (dx_ref, da_ref), _, ride = rider.split(refs, 4, 2, 0)
        i, j = pl.program_id(0), pl.program_id(1)

        @pl.when(jnp.logical_and(i == 0, j == 0))
        def _():
            rider.first(ride)

        @pl.when(j == 0)
        def _():
            dx_ref[...] = jnp.zeros_like(dx_ref)

        da = (_dot_nt(dh_ref[...], wd_ref[...]) * (2.0 * r_ref[...].astype(F32))).astype(BF16)
        da_ref[...] = da
        dx_ref[...] += _dot_nt(da, wu_ref[0])

        @pl.when(jnp.logical_and(i == n_t - 1, j == N_CHIPS - 1))
        def _():
            rider.last(ride)

    return rider.call(
        body, [rb, dhb, w_up_s, w_down], grid=(n_t, N_CHIPS), name="mlp_bwd",
        in_specs=[pl.BlockSpec((tm, fc), lambda i, j: (i, j)), pl.BlockSpec((tm, D), lambda i, j: (i, 0)),
                  pl.BlockSpec((1, D, fc), lambda i, j: (j, 0, 0)),
                  pl.BlockSpec((fc, D), lambda i, j: (j, 0))],
        out_specs=[pl.BlockSpec((tm, D), lambda i, j: (i, 0)), pl.BlockSpec((tm, fc), lambda i, j: (i, j))],
        out_shape=[_sds((S, D), F32), _sds((S, D_FF), BF16)],
        scratch_shapes=[], vmem_mib=56)


def _tn_matmul(a, b, name, tk, tt, stacked, rider, square_a=False):
    T, K = a.shape
    N = b.shape[1]
    tn = 1024
    grid = (K // tk, N // tn, T // tt)

    def body(*refs):
        (a_ref, b_ref), (o_ref,), _, ride = rider.split(refs, 2, 1, 0)
        at = [pl.program_id(d) for d in range(3)]

        @pl.when(jnp.logical_and(jnp.logical_and(at[0] == 0, at[1] == 0), at[2] == 0))
        def _():
            rider.first(ride)

        @pl.when(at[2] == 0)
        def _():
            o_ref[...] = jnp.zeros_like(o_ref)

        a_t = a_ref[...]
        if square_a:
            a_t = a_t * a_t
        prod = _dot_tn(a_t, b_ref[...])
        if stacked:
            o_ref[0] += prod
        else:
            o_ref[...] += prod

        @pl.when(jnp.logical_and(jnp.logical_and(at[0] == grid[0] - 1, at[1] == grid[1] - 1),
                                 at[2] == grid[2] - 1))
        def _():
            rider.last(ride)

    if stacked:
        out_spec = pl.BlockSpec((1, tk, tn), lambda k, n, t: (n, k, 0))
        out_shape = _sds((N // tn, K, tn), F32)
    else:
        out_spec = pl.BlockSpec((tk, tn), lambda k, n, t: (k, n))
        out_shape = _sds((K, N), F32)
    return rider.call(
        body, [a, b], grid=grid, name=name,
        in_specs=[pl.BlockSpec((tt, tk), lambda k, n, t: (t, k)),
                  pl.BlockSpec((tt, tn), lambda k, n, t: (t, n))],
        out_specs=[out_spec], out_shape=[out_shape], scratch_shapes=[], vmem_mib=58)


def _mix_bwd(dpre2, dx1m, xh1, rstd1, g1, w_out, on, pooled, o_raw, ga, tm, rider):
    S, D = xh1.shape
    n_t = S // tm

    def body(*refs):
        ((dp2_ref, dxm_ref, xh_ref, rstd_ref, g_ref, w_ref, on_ref, po_ref, o_ref, ga_ref),
         (dpre1_ref, do_ref, dpo_ref, dw_ref, dg_ref, db_ref, dga_ref), _, ride) = rider.split(refs, 10, 7, 0)

        @pl.when(pl.program_id(0) == 0)
        def _():
            rider.first(ride)
            for r in (dw_ref, dg_ref, db_ref, dga_ref):
                r[...] = jnp.zeros_like(r)

        xh = xh_ref[...]
        dx1 = ALPHA * dp2_ref[...] + dxm_ref[...]
        dg_ref[...] += _colsum(dx1 * xh)
        db_ref[...] += _colsum(dx1)
        dpre1 = _ln_bwd(dx1, xh, rstd_ref[...], g_ref[...])
        dpre1_ref[...] = dpre1
        dmb = dpre1.astype(BF16)
        dcat = _dot_nt(dmb, w_ref[...])
        dpo_ref[...] = dcat[:, ATTN_WIDTH:]
        dw_ref[:ATTN_WIDTH, :] += _dot_tn(on_ref[...], dmb)
        dw_ref[ATTN_WIDTH:, :] += _dot_tn(po_ref[...], dmb)

        m0 = lax.broadcasted_iota(jnp.int32, (1, PAIR), 1) < HEAD_DIM

        def seg_mean(a):
            s0 = jnp.sum(jnp.where(m0, a, 0.0), axis=-1, keepdims=True)
            s1 = jnp.sum(jnp.where(m0, 0.0, a), axis=-1, keepdims=True)
            return jnp.where(m0, s0, s1) * (1.0 / HEAD_DIM)

        for p in range(N_PAIRS):
            cols = slice(p * PAIR, (p + 1) * PAIR)
            o = o_ref[:, cols]
            rs = lax.rsqrt(seg_mean(o * o) + RMS_EPS)
            oh = o * rs
            don = dcat[:, cols]
            dga_ref[:, cols] += _colsum(don * oh)
            doh = don * ga_ref[:, cols]
            do_ref[:, cols] = rs * (doh - oh * seg_mean(doh * oh))

        @pl.when(pl.program_id(0) == n_t - 1)
        def _():
            rider.last(ride)

    vec = _const_spec((1, D))
    return rider.call(
        body, [dpre2, dx1m, xh1, rstd1, g1, w_out, on, pooled, o_raw, ga], grid=(n_t,), name="mix_bwd",
        in_specs=[_row_spec(tm, D), _row_spec(tm, D), _row_spec(tm, D), _row_spec(tm, 1), vec,
                  _const_spec((D, D)), _row_spec(tm, ATTN_WIDTH), _row_spec(tm, POOL_WIDTH),
                  _row_spec(tm, ATTN_WIDTH), _const_spec((1, ATTN_WIDTH))],
        out_specs=[_row_spec(tm, D), _row_spec(tm, ATTN_WIDTH), _row_spec(tm, POOL_WIDTH),
                   _const_spec((D, D)), vec, vec, _const_spec((1, ATTN_WIDTH))],
        out_shape=[_sds((S, D), F32), _sds((S, ATTN_WIDTH), F32), _sds((S, POOL_WIDTH), F32),
                   _sds((D, D), F32), _sds((1, D), F32), _sds((1, D), F32), _sds((1, ATTN_WIDTH), F32)],
        scratch_shapes=[], vmem_mib=56)


def _pool_bwd(dpooled, d_b, w_pool, pscale, tm, rider):
    S = dpooled.shape[0]
    hb = tm // POOL_HALO
    n_t = S // tm
    te = tm + POOL_HALO

    def body(*refs):
        ((dp_ref, dph_ref, d_ref, wp_ref, sc_ref), (du_ref, dwp_ref, dsc_ref), _,
         ride) = rider.split(refs, 5, 3, 0)
        i = pl.program_id(0)

        @pl.when(i == 0)
        def _():
            rider.first(ride)
            dwp_ref[...] = jnp.zeros_like(dwp_ref)
            dsc_ref[...] = jnp.zeros_like(dsc_ref)

        halo = jnp.where(i < n_t - 1, dph_ref[...], 0.0)
        pos = i * tm + lax.broadcasted_iota(jnp.int32, (te, 1), 0)
        for g in range(N_POOL_GROUPS):
            win = 2 ** (g + 1)
            cols = slice(g * POOL_GROUP, (g + 1) * POOL_GROUP)
            wpb = wp_ref[g].astype(BF16)
            dpt = dp_ref[:, cols]
            dpe = jnp.concatenate([dpt, halo[:, cols]], axis=0)
            dyb = (dpe * sc_ref[:, cols]).astype(BF16)
            dd = _dot_nt(dyb, wpb)
            s = dd / jnp.minimum(pos + 1, win).astype(F32)
            for sh in (1, 2, 4, 8)[:g + 1]:
                s = s + pltpu.roll(s, te - sh, 0)
            du_ref[:, cols] = s[:tm, :] - dd[:tm, :]
            db = d_ref[:, cols]
            dwp_ref[g] += _dot_tn(db, dyb[:tm, :])
            dsc_ref[:, cols] += _colsum(dpt * _dot(db, wpb))

        @pl.when(i == n_t - 1)
        def _():
            rider.last(ride)

    return rider.call(
        body, [dpooled, dpooled, d_b, w_pool, pscale], grid=(n_t,), name="pool_bwd",
        in_specs=[_row_spec(tm, POOL_WIDTH),
                  pl.BlockSpec((POOL_HALO, POOL_WIDTH),
                               lambda i: (jnp.minimum((i + 1) * hb, S // POOL_HALO - 1), 0)),
                  _row_spec(tm, POOL_WIDTH),
                  _const_spec((N_POOL_GROUPS, POOL_GROUP, POOL_GROUP)), _const_spec((1, POOL_WIDTH))],
        out_specs=[_row_spec(tm, POOL_WIDTH), _const_spec((N_POOL_GROUPS, POOL_GROUP, POOL_GROUP)),
                   _const_spec((1, POOL_WIDTH))],
        out_shape=[_sds((S, POOL_WIDTH), F32), _sds((N_POOL_GROUPS, POOL_GROUP, POOL_GROUP), F32),
                   _sds((1, POOL_WIDTH), F32)],
        scratch_shapes=[], vmem_mib=32)


def _attn_bwd(q, k, v, do, tq, rider):
    S = q.shape[0]
    nq = S // tq

    def body(*refs):
        ((q_ref, k_ref, v_ref, do_ref), (dq_ref, dk_ref, dv_ref),
         (g_s, b_s, low_s, upp_s), ride) = rider.split(refs, 4, 3, 4)
        p, i = pl.program_id(0), pl.program_id(1)

        @pl.when(jnp.logical_and(p == 0, i == 0))
        def _():
            rider.first(ride)
            low_s[...] = _tri(tq, upper=False)
            upp_s[...] = _tri(tq, upper=True)

        @pl.when(i == 0)
        def _():
            for r in (dk_ref, dv_ref):
                r[...] = jnp.zeros_like(r)

        lane = lax.broadcasted_iota(jnp.int32, (1, PAIR), 1)
        m0 = lane < HEAD_DIM
        low = low_s[...]
        upp = upp_s[...]
        dob = do_ref[...].astype(BF16)
        q2 = q_ref[...]
        qhs = [jnp.where(m0, q2, jnp.zeros_like(q2)), jnp.where(m0, jnp.zeros_like(q2), q2)]
        dhs = [jnp.where(m0, dob, jnp.zeros_like(dob)), jnp.where(m0, jnp.zeros_like(dob), dob)]
        causal = _strictly_causal(tq)

        def down(kb, c_ls, valid):
            ks = pl.multiple_of(kb * tq, tq)
            kt = k_ref[pl.ds(ks, tq), :]
            vt = v_ref[pl.ds(ks, tq), :]
            lss, ls_, ws = _sb_tile(qhs, kt, low, c_ls, valid)
            dws = [_dot_nt(dh, vt) for dh in dhs]
            for hh in range(2):
                g_s[hh, kb] = dws[hh] * ws[hh]
                b_s[hh, kb] = jnp.exp(lss[hh])
            dv_ref[pl.ds(ks, tq), :] += (_dot_tn(ws[0].astype(BF16), dhs[0])
                                         + _dot_tn(ws[1].astype(BF16), dhs[1]))
            return [c_l + jnp.sum(l, axis=1, keepdims=True) for c_l, l in zip(c_ls, ls_)]

        zc, za = jnp.zeros((tq, 1), F32), jnp.zeros((tq, PAIR), F32)
        def first_two():
            c_ls = down(i - 1, down(i, [zc, zc], causal), None)
            return (_weights_left(c_ls), *c_ls)

        st0 = lax.cond(i >= 1, first_two, lambda: (jnp.int32(0), *down(i, [zc, zc], causal)))

        def more(st):
            return jnp.logical_and(st[0] <= i, st[1] > 0)

        def down_step(st):
            c_ls = down(i - st[0], [st[2], st[3]], None)
            return (st[0] + 1, _weights_left(c_ls), c_ls[0], c_ls[1])

        n_tiles = lax.while_loop(more, down_step, (jnp.int32(2), *st0))[0]

        def up(kb, c_gs, accs, valid):
            ks = pl.multiple_of(kb * tq, tq)
            kt = k_ref[pl.ds(ks, tq), :]
            gs = [g_s[hh, kb] for hh in range(2)]
            pres = [_dot(g.astype(BF16), upp) + c_g for g, c_g in zip(gs, c_gs)]
            dzs = []
            for hh in range(2):
                beta = b_s[hh, kb]
                dz = gs[hh] - beta * (gs[hh] + pres[hh])
                if valid is not None:
                    dz = jnp.where(valid, dz, 0.0)
                dzs.append(dz.astype(BF16))
            new_a = [acc + _dot(dzb, kt) for acc, dzb in zip(accs, dzs)]
            dk_ref[pl.ds(ks, tq), :] += _dot_tn(dzs[0], qhs[0]) + _dot_tn(dzs[1], qhs[1])
            new_c = [c_g + jnp.sum(g, axis=1, keepdims=True) for c_g, g in zip(c_gs, gs)]
            return new_c, new_a

        def up_step(kb, st):
            c_gs, accs = up(kb, [st[0], st[1]], [st[2], st[3]], None)
            return (c_gs[0], c_gs[1], accs[0], accs[1])

        st = lax.fori_loop(i - n_tiles + 1, i - 1, up_step, (zc, zc, za, za))

        def last_two():
            c_gs, accs = up(i - 1, [st[0], st[1]], [st[2], st[3]], None)
            return tuple(up(i, c_gs, accs, causal)[1])

        accs = lax.cond(i >= 1, last_two, lambda: tuple(up(i, [zc, zc], [za, za], causal)[1]))
        dq_ref[...] = jnp.where(m0, accs[0], accs[1]) * Q_SCALE

        @pl.when(jnp.logical_and(p == N_PAIRS - 1, i == nq - 1))
        def _():
            rider.last(ride)

    return rider.call(
        body, [q, k, v, do], grid=(N_PAIRS, nq), name="attn_bwd",
        in_specs=[pl.BlockSpec((tq, PAIR), lambda p, i: (i, p)),
                  pl.BlockSpec((S, PAIR), lambda p, i: (0, p)),
                  pl.BlockSpec((S, PAIR), lambda p, i: (0, p)),
                  pl.BlockSpec((tq, PAIR), lambda p, i: (i, p))],
        out_specs=[pl.BlockSpec((tq, PAIR), lambda p, i: (i, p)),
                   pl.BlockSpec((S, PAIR), lambda p, i: (0, p)),
                   pl.BlockSpec((S, PAIR), lambda p, i: (0, p))],
        out_shape=[_sds((S, ATTN_WIDTH), F32), _sds((S, ATTN_WIDTH), F32), _sds((S, ATTN_WIDTH), F32)],
        scratch_shapes=[pltpu.VMEM((2, nq, tq, tq), F32), pltpu.VMEM((2, nq, tq, tq), F32),
                        pltpu.VMEM((tq, tq), BF16), pltpu.VMEM((tq, tq), BF16)],
        vmem_mib=56)


def _inproj_bwd(dq, dk, dv, du, dpre1, xh0, rstd0, g0, b0, w_in_s, tm):
    S, D = xh0.shape

    def body(dq_ref, dk_ref, dv_ref, du_ref, dp1_ref, xh_ref, rstd_ref, g_ref, b_ref, w_ref,
             gx_ref, dw_ref, dg_ref, db_ref):
        @pl.when(pl.program_id(0) == 0)
        def _():
            for r in (dw_ref, dg_ref, db_ref):
                r[...] = jnp.zeros_like(r)

        xh = xh_ref[...]
        xb = (xh * g_ref[...] + b_ref[...]).astype(BF16)
        dx0 = ALPHA * dp1_ref[...]
        for c, r in enumerate((dq_ref, dk_ref, dv_ref, du_ref)):
            dpb = r[...].astype(BF16)
            dx0 = dx0 + _dot_nt(dpb, w_ref[c])
            dw_ref[c] += _dot_tn(xb, dpb)
        dg_ref[...] += _colsum(dx0 * xh)
        db_ref[...] += _colsum(dx0)
        gx_ref[...] = _ln_bwd(dx0, xh, rstd_ref[...], g_ref[...])

    vec = _const_spec((1, D))
    half = _row_spec(tm, 512)
    return pl.pallas_call(
        body, grid=(S // tm,), name="inproj_bwd",
        in_specs=[half, half, half, half, _row_spec(tm, D), _row_spec(tm, D), _row_spec(tm, 1), vec, vec,
                  _const_spec((N_CHIPS, D, 512))],
        out_specs=[_row_spec(tm, D), _const_spec((N_CHIPS, D, 512)), vec, vec],
        out_shape=[_sds((S, D), F32), _sds((N_CHIPS, D, 512), F32), _sds((1, D), F32), _sds((1, D), F32)],
        compiler_params=_cp(58),
    )(*_hbm(dq, dk, dv, du, dpre1, xh0, rstd0, g0, b0, w_in_s))


def _place():
    return lax.axis_index("x"), lax.axis_index("y"), lax.axis_index("c")


CHIP_FLIPS = ((0, 1), (1, 0), (1, 1))


class _Rider:
    def __init__(self, ins, out_shapes, n_sem, phases, aliases=None):
        self.ins, self.out_shapes, self.n_sem, self.phases = list(ins), list(out_shapes), n_sem, phases
        self.aliases = aliases or {}

    def __add__(self, other):
        na, ma = len(self.ins), len(self.out_shapes)

        def phases(ins, outs, ssem, rsem):
            mine = self.phases(ins[:na], outs[:ma], ssem, rsem)
            rest = pl.ds(self.n_sem, other.n_sem)
            theirs = other.phases(ins[na:], outs[ma:], ssem.at[rest], rsem.at[rest])
            assert len(mine) == 1 and len(theirs) == 1
            return [mine[0] + theirs[0]]

        aliases = {**self.aliases, **{na + i: ma + o for i, o in other.aliases.items()}}
        return _Rider(self.ins + other.ins, self.out_shapes + other.out_shapes, self.n_sem + other.n_sem, phases,
                      aliases)

    def split(self, refs, n_in, n_out, n_scratch):
        a = n_in + len(self.ins)
        b = a + n_out
        c = b + len(self.out_shapes)
        own = (refs[:n_in], refs[a:b], refs[c:c + n_scratch])
        return own + ((refs[n_in:a], refs[b:c]) + tuple(refs[c + n_scratch:]),)

    def first(self, ride):
        for make in self.phases(*ride)[0]:
            make().start()

    def mid(self, ride):
        ph = self.phases(*ride)
        if len(ph) == 2:
            for make in ph[0]:
                make().wait_recv()
            for make in ph[1]:
                make().start()

    def last(self, ride):
        ph = self.phases(*ride)
        if len(ph) == 2:
            for make in ph[0]:
                make().wait_send()
        for make in ph[-1]:
            make().wait()

    def call(self, body, args, *, grid, name, in_specs, out_specs, out_shape, scratch_shapes, vmem_mib,
             prefetch=None):
        n_in, n_out = len(in_specs), len(out_specs)
        sems = [pltpu.SemaphoreType.DMA((self.n_sem,)), pltpu.SemaphoreType.DMA((self.n_sem,))]
        n_pre = 0 if prefetch is None else 1
        grid_spec = pltpu.PrefetchScalarGridSpec(
            num_scalar_prefetch=n_pre, grid=grid,
            in_specs=list(in_specs) + [HBM_SPEC] * len(self.ins),
            out_specs=list(out_specs) + [HBM_SPEC] * len(self.out_shapes),
            scratch_shapes=list(scratch_shapes) + sems)
        return pl.pallas_call(
            body, name=name, grid_spec=grid_spec,
            out_shape=list(out_shape) + self.out_shapes,
            input_output_aliases={n_pre + n_in + i: n_out + o for i, o in self.aliases.items()},
            compiler_params=_cp(vmem_mib),
        )(*([] if prefetch is None else [prefetch]), *_hbm(*args), *self.ins)

    def run(self, name):
        def body(*refs):
            ride = self.split(refs, 0, 0, 0)[3]
            self.first(ride)
            self.mid(ride)
            self.last(ride)

        return self.call(body, [], grid=(), name=name, in_specs=[], out_specs=[], out_shape=[],
                         scratch_shapes=[], vmem_mib=16)


def _remote(src, dst, ssem, rsem, n, dev):
    return functools.partial(pltpu.make_async_remote_copy, src_ref=src, dst_ref=dst, send_sem=ssem.at[n],
                             recv_sem=rsem.at[n], device_id=dev, device_id_type=MESH)


def _cast_into_slot(w, place, name):
    R, C = w.shape
    tr = min(R, 512)

    def body(pl_ref, w_ref, o_ref):
        o_ref[0] = w_ref[...].astype(BF16)

    return pl.pallas_call(
        body, name=name,
        grid_spec=pltpu.PrefetchScalarGridSpec(
            num_scalar_prefetch=1, grid=(R // tr,),
            in_specs=[pl.BlockSpec((tr, C), lambda r, pr: (r, 0))],
            out_specs=pl.BlockSpec((1, tr, C), lambda r, pr: (pr[1], r, 0))),
        out_shape=_sds((N_CHIPS, R, C), BF16),
    )(place, w)


CAST_STEPS = 8


def _cast_rest(ws, place, rider):
    n = len(ws)

    def body(pl_ref, *refs):
        w_refs, o_refs, _, ride = rider.split(refs, n, n, 0)
        r = pl.program_id(0)

        @pl.when(r == 0)
        def _():
            rider.first(ride)

        @pl.when(r == CAST_STEPS // 2)
        def _():
            rider.mid(ride)

        for w_ref, o_ref in zip(w_refs, o_refs):
            o_ref[0] = w_ref[...].astype(BF16)

        @pl.when(r == CAST_STEPS - 1)
        def _():
            rider.last(ride)

    def rows(w):
        return w.shape[0] // CAST_STEPS

    return rider.call(
        body, ws, grid=(CAST_STEPS,), name="cast_weights", prefetch=place,
        in_specs=[pl.BlockSpec((rows(w), w.shape[1]), lambda r, pr: (r, 0)) for w in ws],
        out_specs=[pl.BlockSpec((1, rows(w), w.shape[1]), lambda r, pr: (pr[1], r, 0)) for w in ws],
        out_shape=[_sds((N_CHIPS,) + w.shape, BF16) for w in ws], scratch_shapes=[], vmem_mib=32)


def _gather_rider(stacked, part="both"):
    n, nf = len(stacked), len(CHIP_FLIPS)

    def phases(ins, outs, ssem, rsem):
        x, y, c = _place()
        slot = 2 * x + y
        ici, d2d = [], []
        for w, (i_ref, o_ref) in enumerate(zip(ins, outs)):
            hh = o_ref.shape[1] // 2
            rows = pl.ds(c * hh, hh)
            for f, (fx, fy) in enumerate(CHIP_FLIPS):
                k = w * nf + f
                theirs = 2 * (x ^ fx) + (y ^ fy)
                if part != "pair":
                    ici.append(_remote(i_ref.at[slot, rows], o_ref.at[slot, rows], ssem, rsem, k,
                                       (x ^ fx, y ^ fy, c)))
                if part != "chips":
                    d2d.append(_remote(o_ref.at[theirs, rows], o_ref.at[theirs, rows], ssem, rsem,
                                       (n * nf if part == "both" else 0) + k, (x, y, 1 - c)))
        return [ph for ph in (ici, d2d) if ph]

    return _Rider(stacked, [_sds(s.shape, s.dtype) for s in stacked], (2 if part == "both" else 1) * n * nf,
                  phases, aliases={i: i for i in range(n)})


def _pair_swap_rider(grads):
    def phases(ins, outs, ssem, rsem):
        x, y, c = _place()
        return [[_remote(g.at[:, 1 - c], o, ssem, rsem, k, (x, y, 1 - c))
                 for k, (g, o) in enumerate(zip(ins, outs))]]

    return _Rider(grads, [_sds((N_CHIPS,) + g.shape[2:], g.dtype) for g in grads], len(grads), phases)


def _chip_scatter_rider(parts):
    nf = len(CHIP_FLIPS)

    def phases(ins, outs, ssem, rsem):
        x, y, c = _place()
        return [[_remote(r.at[2 * (x ^ fx) + (y ^ fy)], o.at[f], ssem, rsem, w * nf + f, (x ^ fx, y ^ fy, c))
                 for w, (r, o) in enumerate(zip(ins, outs)) for f, (fx, fy) in enumerate(CHIP_FLIPS)]]

    return _Rider(parts, [_sds((nf,) + r.shape[1:], r.dtype) for r in parts], len(parts) * nf, phases)


def _pair_send_rider(halves):
    def phases(ins, outs, ssem, rsem):
        x, y, c = _place()
        return [[_remote(h, o, ssem, rsem, k, (x, y, 1 - c)) for k, (h, o) in enumerate(zip(ins, outs))]]

    return _Rider(halves, [_sds(h.shape, h.dtype) for h in halves], len(halves), phases)


PAIR_SUM_STEPS = 1
CHIP_SUM_STEPS = 2
ADAMW_STEPS = 4


def _no_rider():
    return _Rider([], [], 1, lambda ins, outs, ssem, rsem: [[]])


def _add_pair(grads, recvs, place, name, rider):
    n = len(grads)

    def body(pl_ref, *refs):
        ins, outs, _, ride = rider.split(refs, 2 * n, 2 * n, 0)
        h, j = pl.program_id(0), pl.program_id(1)

        @pl.when(jnp.logical_and(h == 0, j == 0))
        def _():
            rider.first(ride)

        sums = [ins[2 * w][0, 0] + ins[2 * w + 1][0] for w in range(n)]
        for w in range(n):
            outs[2 * w + 1][0] = sums[w].astype(BF16)

        @pl.when(j == pl_ref[1])
        def _():
            for w in range(n):
                outs[2 * w][...] = sums[w]

        @pl.when(jnp.logical_and(h == PAIR_SUM_STEPS - 1, j == N_CHIPS - 1))
        def _():
            rider.last(ride)

    in_specs, out_specs, out_shape, args = [], [], [], []
    for g, r in zip(grads, recvs):
        _, _, H, C = g.shape
        th = H // PAIR_SUM_STEPS
        spec = pl.BlockSpec((1, th, C), lambda h, j, pr: (j, h, 0))
        in_specs += [pl.BlockSpec((1, 1, th, C), lambda h, j, pr: (j, pr[0], h, 0)), spec]
        out_specs += [pl.BlockSpec((th, C), lambda h, j, pr: (h, 0)), spec]
        out_shape += [_sds((H, C), F32), _sds((N_CHIPS, H, C), BF16)]
        args += [g, r]
    res = rider.call(body, args, grid=(PAIR_SUM_STEPS, N_CHIPS), name=name, prefetch=place, in_specs=in_specs,
                     out_specs=out_specs, out_shape=out_shape, scratch_shapes=[], vmem_mib=56)
    return [(res[2 * w], res[2 * w + 1]) for w in range(n)], res[2 * n:]


def _add_chips(parts, recvs, place, name, rider):
    n = len(parts)

    def body(pl_ref, *refs):
        ins, outs, _, ride = rider.split(refs, 2 * n, n, 0)
        h = pl.program_id(0)

        @pl.when(h == 0)
        def _():
            rider.first(ride)

        for w in range(n):
            p_ref, r_ref = ins[2 * w], ins[2 * w + 1]
            outs[w][...] = p_ref[...] + r_ref[0].astype(F32) + r_ref[1].astype(F32) + r_ref[2].astype(F32)

        @pl.when(h == CHIP_SUM_STEPS - 1)
        def _():
            rider.last(ride)

    in_specs, out_specs, out_shape, args = [], [], [], []
    for p, r in zip(parts, recvs):
        H, C = p.shape
        th = H // CHIP_SUM_STEPS
        in_specs += [pl.BlockSpec((th, C), lambda h, pr: (h, 0)),
                     pl.BlockSpec((len(CHIP_FLIPS), th, C), lambda h, pr: (0, h, 0))]
        out_specs.append(pl.BlockSpec((th, C), lambda h, pr: (h, 0)))
        out_shape.append(_sds((H, C), F32))
        args += [p, r]
    res = rider.call(body, args, grid=(CHIP_SUM_STEPS,), name=name, prefetch=place, in_specs=in_specs,
                     out_specs=out_specs, out_shape=out_shape, scratch_shapes=[], vmem_mib=56)
    return res[:n], res[n:]


def _adamw_math(w, g, m, v):
    m = ADAM_B1 * m + (1.0 - ADAM_B1) * g
    v = ADAM_B2 * v + (1.0 - ADAM_B2) * (g * g)
    m_hat = m / (1.0 - ADAM_B1 ** ADAM_STEP)
    v_hat = v / (1.0 - ADAM_B2 ** ADAM_STEP)
    delta = -ADAM_LR * (m_hat / (jnp.sqrt(v_hat) + ADAM_EPS) + ADAM_WD * w)
    return delta, m, v


def _adamw(ws, mines, theirs, ms, vs, place, name, rider):
    n = len(ws)

    def body(pl_ref, *refs):
        ins, outs, _, ride = rider.split(refs, 5 * n, 4 * n, 0)
        h, r = pl.program_id(0), pl.program_id(1)

        @pl.when(jnp.logical_and(h == 0, r == 0))
        def _():
            rider.first(ride)

        for k in range(n):
            w_ref, a_ref, b_ref, m_ref, v_ref = ins[5 * k:5 * k + 5]
            g = jnp.where(h == pl_ref[0], a_ref[...], b_ref[...])
            d, mo, vo = _adamw_math(w_ref[...], g, m_ref[...], v_ref[...])
            for o_ref, val in zip(outs[4 * k:4 * k + 4], (g, d, mo, vo)):
                o_ref[...] = val

        @pl.when(jnp.logical_and(h == 1, r == ADAMW_STEPS - 1))
        def _():
            rider.last(ride)

    in_specs, out_specs, out_shape, args = [], [], [], []
    for w, a, b, m, v in zip(ws, mines, theirs, ms, vs):
        R, C = w.shape
        th = (R // 2) // ADAMW_STEPS
        whole = pl.BlockSpec((th, C), lambda h, r, pr: (h * ADAMW_STEPS + r, 0))
        mine_spec = pl.BlockSpec((th, C), lambda h, r, pr: (jnp.where(h == pr[0], r, 0), 0))
        theirs_spec = pl.BlockSpec((th, C), lambda h, r, pr: (jnp.where(h == pr[0], 0, r), 0))
        in_specs += [whole, mine_spec, theirs_spec, whole, whole]
        out_specs += [whole] * 4
        out_shape += [_sds((R, C), F32)] * 4
        args += [w, a, b, m, v]
    res = rider.call(body, args, grid=(2, ADAMW_STEPS), name=name, prefetch=place, in_specs=in_specs,
                     out_specs=out_specs, out_shape=out_shape, scratch_shapes=[], vmem_mib=40)
    return [tuple(res[4 * k:4 * k + 4]) for k in range(n)], res[4 * n:]


DEVICE_FLIPS = tuple((fx, fy, fc) for fx in (0, 1) for fy in (0, 1) for fc in (0, 1))[1:]


def _pack_exchange_rider(pack):
    def phases(ins, outs, ssem, rsem):
        x, y, c = _place()
        mine = outs[0].at[4 * x + 2 * y + c]
        copies = [_remote(ins[0], mine, ssem, rsem, k, (x ^ fx, y ^ fy, c ^ fc))
                  for k, (fx, fy, fc) in enumerate(DEVICE_FLIPS)]
        copies.append(functools.partial(pltpu.make_async_copy, ins[0], mine, ssem.at[len(DEVICE_FLIPS)]))
        return [copies]

    return _Rider([pack], [_sds((N_DEV,) + pack.shape, pack.dtype)], len(DEVICE_FLIPS) + 1, phases)


def _small_sum_adamw(recv_a, recv_b, wpack, mpack, vpack):
    R = wpack.shape[0]

    def body(a_ref, b_ref, w_ref, m_ref, v_ref, gs_ref, d_ref, mo_ref, vo_ref):
        ta, tb = a_ref[0], b_ref[0]
        for dev in range(1, N_DEV):
            ta = ta + a_ref[dev]
            tb = tb + b_ref[dev]
        total = jnp.concatenate([ta, tb], axis=0)
        gs_ref[...] = total
        d, mo, vo = _adamw_math(w_ref[...], total, m_ref[...], v_ref[...])
        d_ref[...] = d
        mo_ref[...] = mo
        vo_ref[...] = vo

    return pl.pallas_call(
        body, name="small_sum_adamw", in_specs=[VMEM_SPEC] * 5, out_specs=[VMEM_SPEC] * 4,
        out_shape=[_sds((R, LANES), F32)] * 4,
    )(recv_a, recv_b, wpack, mpack, vpack)


def _rows8(a):
    a = a.reshape(-1, LANES)
    pad = (-a.shape[0]) % 8
    return jnp.pad(a, ((0, pad), (0, 0))) if pad else a


def _pack(parts):
    return jnp.concatenate([_rows8(a) for a in parts], axis=0)


def _unpack(pack, like):
    out, row = [], 0
    for a in like:
        n = a.size // LANES
        out.append(pack[row:row + n].reshape(a.shape))
        row += n + (-n) % 8
    return out


def kernel(x, p, emb_ln_g, emb_ln_b, w_in, attn_out_g, w_pool, pool_scale, w_out, ln1_g, ln1_b, w_up, w_down, ln2_g, ln2_b, w_ple, w_ple_gate, ln3_g, ln3_b, loss_target, m_emb_ln_g, m_emb_ln_b, m_w_in, m_attn_out_g, m_w_pool, m_pool_scale, m_w_out, m_ln1_g, m_ln1_b, m_w_up, m_w_down, m_ln2_g, m_ln2_b, m_w_ple, m_w_ple_gate, m_ln3_g, m_ln3_b, v_emb_ln_g, v_emb_ln_b, v_w_in, v_attn_out_g, v_w_pool, v_pool_scale, v_w_out, v_ln1_g, v_ln1_b, v_w_up, v_w_down, v_ln2_g, v_ln2_b, v_w_ple, v_w_ple_gate, v_ln3_g, v_ln3_b):
    S = x.shape[1]
    tq = min(256, S)
    tm_mlp = min(1024, S)
    tm_pool = min(1024, S)
    tm_ln = min(512, S)
    tm_fwd = min(1024, S)
    xs = x[0]
    ps = p[0, 0]
    tgt = loss_target[0]
    row = lambda a: a.reshape(1, -1)
    g0, b0 = row(emb_ln_g), row(emb_ln_b)
    g1, b1, g2, b2, g3, b3 = ln1_g, ln1_b, ln2_g, ln2_b, ln3_g, ln3_b
    wp = w_pool[0]

    xi, yi, ci = _place()
    place = jnp.stack([ci, 2 * xi + yi]).astype(jnp.int32)
    names = ["w_in", "w_out", "w_up", "w_down", "w_ple", "w_ple_gate"]

    big = [w_in[0], w_out[0], w_up[0], w_down[0], w_ple[0], w_ple_gate[0]]
    s_in = _cast_into_slot(big[0], place, "cast_w_in")
    s_out, s_up, s_down, s_ple, s_gate, w_in_s = _cast_rest(big[1:], place, _gather_rider([s_in]))

    xh0, rstd0, q, k, v, d_b, pooled, s_out, s_ple, s_gate = _embln_inproj(
        xs, g0, b0, w_in_s, wp, pool_scale, tm_fwd, _gather_rider([s_out, s_ple, s_gate], "chips"))
    o_raw, on, s_up, s_down, w_out_s, w_ple_s, w_gate_s = _attn_fwd(
        q, k, v, attn_out_g, tq, _gather_rider([s_up, s_down], "chips") + _gather_rider([s_out, s_ple, s_gate], "pair"))
    w_out_f = w_out_s.reshape(D_MODEL, D_MODEL)
    w_gate_f = w_gate_s.reshape(D_MODEL, D_MODEL)
    xh1, rstd1, x1b, w_up_s, w_down_s = _mix_ln1(on, pooled, xh0, g0, b0, w_out_f, g1, b1, tm_fwd,
                                                 _gather_rider([s_up, s_down], "pair"))
    w_down_f = w_down_s.reshape(D_FF, D_MODEL)
    xh2, rstd2, rb = _mlp_ln2(xh1, x1b, g1, b1, w_up_s, w_down_f, tm_mlp, _no_rider())

    (dpre2, dhb, dw_ple, dw_gate, dg3, db3, dg2, db2, loss_row) = _ple_ln3_loss(
        xh2, rstd2, g2, b2, ps, w_ple_s, w_gate_f, g3, b3, tgt, tm_ln)
    def halves_of(g):
        return g.reshape(N_CHIPS, 2, g.shape[1] // 2, g.shape[2])

    ple_halves = [halves_of(dw_ple), halves_of(dw_gate.reshape(N_CHIPS, D_MODEL // N_CHIPS, D_MODEL))]
    dx1m, da, *ple_pair = _mlp_bwd(rb, dhb, w_up_s, w_down_f, tm_mlp, _pair_swap_rider(ple_halves))
    (dw_up,) = _tn_matmul(x1b, da, "grad_w_up", 1024, S, True, _no_rider())
    up_halves = halves_of(dw_up)
    dw_down, up_pair = _tn_matmul(rb, dhb, "grad_w_down", 1024, S, False,
                                  _pair_swap_rider([up_halves]), square_a=True)
    down_halves = halves_of(dw_down.reshape(N_CHIPS, D_FF // N_CHIPS, D_MODEL))
    dpre1, do, dpooled, dw_out, dg1, db1, dga, down_pair = _mix_bwd(
        dpre2, dx1m, xh1, rstd1, g1, w_out_f, on, pooled, o_raw, attn_out_g, tm_ln,
        _pair_swap_rider([down_halves]))
    out_halves = halves_of(dw_out.reshape(N_CHIPS, D_MODEL // N_CHIPS, D_MODEL))
    du, dwp, dsc, out_pair = _pool_bwd(dpooled, d_b, wp, pool_scale, tm_pool, _pair_swap_rider([out_halves]))
    early_sum, _ = _add_pair(
        [out_halves, up_halves, down_halves] + ple_halves, [out_pair, up_pair, down_pair] + ple_pair, place,
        "pair_sum_early", _no_rider())
    pack_a = _pack([jnp.broadcast_to(loss_row, (8, LANES)), dwp, dsc, dg1, db1, dg2, db2, dg3, db3])
    riding = _chip_scatter_rider([b for _, b in early_sum]) + _pack_exchange_rider(pack_a)
    dq, dk, dv, *arrived = _attn_bwd(q, k, v, do, tq, riding)
    early_chips, recv_a = arrived[:-1], arrived[-1]
    grad_x, dw_in, dg0, db0 = _inproj_bwd(dq, dk, dv, du, dpre1, xh0, rstd0, g0, b0, w_in_s, tm_ln)

    in_halves = halves_of(dw_in)
    pack_b = _pack([dg0, db0, dga])
    early_mine, (in_pair, recv_b) = _add_chips(
        [s for s, _ in early_sum], early_chips, place, "chip_sum_early",
        _pair_swap_rider([in_halves]) + _pack_exchange_rider(pack_b))
    (in_sum,), early_theirs = _add_pair([in_halves], [in_pair], place, "pair_sum_w_in", _pair_send_rider(early_mine))
    ms = [m_w_in, m_w_out, m_w_up, m_w_down, m_w_ple, m_w_ple_gate]
    vs = [v_w_in, v_w_out, v_w_up, v_w_down, v_w_ple, v_w_ple_gate]
    early_res, _ = _adamw(big[1:], early_mine, early_theirs, [m[0] for m in ms[1:]], [v[0] for v in vs[1:]],
                          place, "adamw_early", _no_rider())
    (in_chips,) = _chip_scatter_rider([in_sum[1]]).run("reduce_chips_late")
    (in_mine,), _ = _add_chips([in_sum[0]], [in_chips], place, "chip_sum_w_in", _no_rider())
    (in_theirs,) = _pair_send_rider([in_mine]).run("gather_pair_w_in")
    in_res, _ = _adamw(big[:1], [in_mine], [in_theirs], [ms[0][0]], [vs[0][0]], place, "adamw_w_in", _no_rider())
    big_out = {n: tuple(r.reshape(m.shape) for r in res4) for n, res4, m in zip(names, in_res + early_res, ms)}

    small_names = ["w_pool", "pool_scale", "ln1_g", "ln1_b", "ln2_g", "ln2_b", "ln3_g", "ln3_b",
                   "emb_ln_g", "emb_ln_b", "attn_out_g"]
    small_w = [w_pool, pool_scale, ln1_g, ln1_b, ln2_g, ln2_b, ln3_g, ln3_b, emb_ln_g, emb_ln_b, attn_out_g]
    small_m = [m_w_pool, m_pool_scale, m_ln1_g, m_ln1_b, m_ln2_g, m_ln2_b, m_ln3_g, m_ln3_b,
               m_emb_ln_g, m_emb_ln_b, m_attn_out_g]
    small_v = [v_w_pool, v_pool_scale, v_ln1_g, v_ln1_b, v_ln2_g, v_ln2_b, v_ln3_g, v_ln3_b,
               v_emb_ln_g, v_emb_ln_b, v_attn_out_g]
    loss_like = jnp.zeros((8, LANES), F32)
    gs, ds, mos, vos = _small_sum_adamw(recv_a, recv_b, _pack([loss_like] + small_w), _pack([loss_like] + small_m),
                                        _pack([jnp.ones((8, LANES), F32)] + small_v))
    like = [loss_like] + small_w
    gs_u, ds_u, mos_u, vos_u = (_unpack(a, like) for a in (gs, ds, mos, vos))
    loss = gs_u[0][0, 0]
    small_out = {n: (gs_u[i + 1], ds_u[i + 1], mos_u[i + 1], vos_u[i + 1]) for i, n in enumerate(small_names)}

    order = ["emb_ln_g", "emb_ln_b", "w_in", "attn_out_g", "w_pool", "pool_scale", "w_out", "ln1_g", "ln1_b",
             "w_up", "w_down", "ln2_g", "ln2_b", "w_ple", "w_ple_gate", "ln3_g", "ln3_b"]
    res = {**big_out, **small_out}
    outs = [loss, grad_x.reshape(x.shape)]
    for kind in range(4):
        outs += [res[n][kind] for n in order]
    return tuple(outs)
```
